```python
import math
import jax, jax.numpy as jnp
from jax import lax
import numpy as np

D_MODEL = 1024
BATCH = 8
SEQ = 16384
DEPTH = 2

HEAD_DIM = 64
GRID_W = 64
EPS = 1e-6

ATT_HEADS = 8
ATT_KV_HEADS = 2
ATT_WIDTH = ATT_HEADS * HEAD_DIM
KV_WIDTH = ATT_KV_HEADS * HEAD_DIM
Q_BLOCK = 128
ROPE_THETA = 10000.0
ROPE_AXIS_DIM = HEAD_DIM // 2

CONV_GROUPS = 4
CONV_WIDTH = CONV_GROUPS * HEAD_DIM
CONV_KERNEL = 31

SG_HEADS = 4
SG_WIDTH = SG_HEADS * HEAD_DIM
SG_CHUNK = 128

D_MIX = ATT_WIDTH + CONV_WIDTH + SG_WIDTH

IN_SPLIT_SIZES = (
    ATT_WIDTH,
    KV_WIDTH,
    KV_WIDTH,
    ATT_WIDTH,
    2 * CONV_WIDTH,
    CONV_WIDTH,
    SG_WIDTH,
    SG_WIDTH,
    SG_WIDTH,
)
D_IN = sum(IN_SPLIT_SIZES)

kernel_name = "hybrid_parallel_conv_gqa_sgu_encoder"


def _split_points():
    pts, acc = [], 0
    for s in IN_SPLIT_SIZES[:-1]:
        acc += s
        pts.append(acc)
    return pts


def rms_norm(x, g):
    xf = x.astype(jnp.float32)
    y = xf * lax.rsqrt(jnp.mean(xf * xf, axis=-1, keepdims=True) + EPS) * g.astype(jnp.float32)
    return y.astype(x.dtype)


def layer_norm(x, g, b):
    xf = x.astype(jnp.float32)
    mu = jnp.mean(xf, axis=-1, keepdims=True)
    xc = xf - mu
    var = jnp.mean(xc * xc, axis=-1, keepdims=True)
    y = xc * lax.rsqrt(var + EPS) * g.astype(jnp.float32) + b.astype(jnp.float32)
    return y.astype(x.dtype)


def rope_1d(x, pos):
    d = x.shape[-1]
    half = d // 2
    inv_freq = ROPE_THETA ** (-jnp.arange(half, dtype=jnp.float32) / half)
    ang = pos[:, None] * inv_freq[None, :]
    cos = jnp.cos(ang)[:, None, :]
    sin = jnp.sin(ang)[:, None, :]
    xf = x.astype(jnp.float32)
    x1, x2 = xf[..., :half], xf[..., half:]
    out = jnp.concatenate([x1 * cos - x2 * sin, x2 * cos + x1 * sin], axis=-1)
    return out.astype(x.dtype)


def axial_rope(x, row, col):
    return jnp.concatenate([rope_1d(x[..., :ROPE_AXIS_DIM], row),
                            rope_1d(x[..., ROPE_AXIS_DIM:], col)], axis=-1)


def attention_group(q, k, v):
    B, S = q.shape[0], q.shape[1]
    G = ATT_HEADS // ATT_KV_HEADS
    nblk = S // Q_BLOCK
    qb = q.reshape(B, nblk, Q_BLOCK, ATT_KV_HEADS, G, HEAD_DIM).transpose(1, 0, 3, 4, 2, 5)
    kt = k.transpose(0, 2, 1, 3)
    vt = v.transpose(0, 2, 1, 3)
    scale = HEAD_DIM ** -0.5

    def one_block(qi):
        s = jnp.einsum('bkgqd,bksd->bkgqs', qi, kt, preferred_element_type=jnp.float32) * scale
        p = jax.nn.softmax(s, axis=-1)
        return jnp.einsum('bkgqs,bksd->bkgqd', p.astype(vt.dtype), vt)

    o = lax.map(one_block, qb)
    return o.transpose(1, 0, 4, 2, 3, 5).reshape(B, S, ATT_WIDTH)


def conv_group(a, dw_w, dw_b, ln_g, ln_b):
    h = a[..., :CONV_WIDTH] * jax.nn.sigmoid(a[..., CONV_WIDTH:])
    pad = CONV_KERNEL // 2
    h = lax.conv_general_dilated(
        h, dw_w[:, None, :].astype(h.dtype), window_strides=(1,), padding=[(pad, pad)],
        dimension_numbers=('NWC', 'WIO', 'NWC'), feature_group_count=CONV_WIDTH) + dw_b
    h = layer_norm(h, ln_g, ln_b)
    return jax.nn.silu(h)


def spatial_gating_group(u, v, ln_g, ln_b, w_s, b_s):
    B, S = u.shape[0], u.shape[1]
    u = jax.nn.gelu(u, approximate=False)
    v = layer_norm(jax.nn.gelu(v, approximate=False), ln_g, ln_b)
    n = S // SG_CHUNK
    vc = v.reshape(B, n, SG_CHUNK, SG_HEADS, HEAD_DIM)
    mixed = jnp.einsum('hpq,bnqhd->bnphd', w_s, vc) + b_s.T[None, None, :, :, None]
    return u * mixed.reshape(B, S, SG_WIDTH)


def _fwd_setup_inputs(seed: int = 0) -> dict:
    key = jax.random.key(seed)
    ks = jax.random.split(key, 16)
    f32 = jnp.float32
    x = jax.random.normal(ks[0], (BATCH, SEQ, D_MODEL), f32)
    pre_norm = 1.0 + 0.05 * jax.random.normal(ks[1], (DEPTH, D_MODEL), f32)
    post_norm = 1.0 + 0.05 * jax.random.normal(ks[2], (DEPTH, D_MODEL), f32)
    w_in = jax.random.normal(ks[3], (DEPTH, D_MODEL, D_IN), f32) * D_MODEL ** -0.5
    w_out = jax.random.normal(ks[4], (DEPTH, D_MIX, D_MODEL), f32) * D_MIX ** -0.5
    q_norm = 1.0 + 0.05 * jax.random.normal(ks[5], (DEPTH, HEAD_DIM), f32)
    k_norm = 1.0 + 0.05 * jax.random.normal(ks[6], (DEPTH, HEAD_DIM), f32)
    conv_dw = jax.random.normal(ks[7], (DEPTH, CONV_KERNEL, CONV_WIDTH), f32) * CONV_KERNEL ** -0.5
    conv_dw_b = 0.02 * jax.random.normal(ks[8], (DEPTH, CONV_WIDTH), f32)
    conv_ln_g = 1.0 + 0.05 * jax.random.normal(ks[9], (DEPTH, CONV_WIDTH), f32)
    conv_ln_b = 0.02 * jax.random.normal(ks[10], (DEPTH, CONV_WIDTH), f32)
    sg_ln_g = 1.0 + 0.05 * jax.random.normal(ks[11], (DEPTH, SG_WIDTH), f32)
    sg_ln_b = 0.02 * jax.random.normal(ks[12], (DEPTH, SG_WIDTH), f32)
    sg_w = jax.random.normal(ks[13], (DEPTH, SG_HEADS, SG_CHUNK, SG_CHUNK), f32) * SG_CHUNK ** -0.5
    sg_b = 1.0 + 0.1 * jax.random.normal(ks[14], (DEPTH, SG_HEADS, SG_CHUNK), f32)
    return {"x": x, "pre_norm": pre_norm, "post_norm": post_norm, "w_in": w_in, "w_out": w_out,
            "q_norm": q_norm, "k_norm": k_norm, "conv_dw": conv_dw, "conv_dw_b": conv_dw_b,
            "conv_ln_g": conv_ln_g, "conv_ln_b": conv_ln_b, "sg_ln_g": sg_ln_g, "sg_ln_b": sg_ln_b,
            "sg_w": sg_w, "sg_b": sg_b}


def _fwd_reference(x, pre_norm, post_norm, w_in, w_out, q_norm, k_norm, conv_dw, conv_dw_b,
              conv_ln_g, conv_ln_b, sg_ln_g, sg_ln_b, sg_w, sg_b):
    B, S = x.shape[0], x.shape[1]
    rows = S // GRID_W
    row = jnp.repeat(jnp.arange(rows, dtype=jnp.int32), GRID_W).astype(jnp.float32)
    col = jnp.tile(jnp.arange(GRID_W, dtype=jnp.int32), rows).astype(jnp.float32)
    split_pts = _split_points()

    for l in range(DEPTH):
        h = rms_norm(x, pre_norm[l])
        proj = jnp.einsum('bsd,de->bse', h, w_in[l])
        q, k, v, g_att, a_conv, g_conv, u_sg, v_sg, g_sg = jnp.split(proj, split_pts, axis=-1)

        q = axial_rope(rms_norm(q.reshape(B, S, ATT_HEADS, HEAD_DIM), q_norm[l]), row, col)
        k = axial_rope(rms_norm(k.reshape(B, S, ATT_KV_HEADS, HEAD_DIM), k_norm[l]), row, col)
        v = v.reshape(B, S, ATT_KV_HEADS, HEAD_DIM)
        att = attention_group(q, k, v) * jax.nn.silu(g_att)

        cnv = conv_group(a_conv, conv_dw[l], conv_dw_b[l], conv_ln_g[l], conv_ln_b[l]) * jax.nn.silu(g_conv)

        sgu = spatial_gating_group(u_sg, v_sg, sg_ln_g[l], sg_ln_b[l], sg_w[l], sg_b[l]) * jax.nn.silu(g_sg)

        mix = jnp.einsum('bse,ed->bsd', jnp.concatenate([att, cnv, sgu], axis=-1), w_out[l])
        x = x + rms_norm(mix, post_norm[l])
    return x


import jax as _jax
import jax.numpy as _jnp

TWIN_FORMAT = 'train_step'
FWD_PARAMS = ['x', 'pre_norm', 'post_norm', 'w_in', 'w_out', 'q_norm', 'k_norm', 'conv_dw', 'conv_dw_b', 'conv_ln_g', 'conv_ln_b', 'sg_ln_g', 'sg_ln_b', 'sg_w', 'sg_b']
TWIN_WEIGHTS = ['pre_norm', 'post_norm', 'w_in', 'w_out', 'q_norm', 'k_norm', 'conv_dw', 'conv_dw_b', 'conv_ln_g', 'conv_ln_b', 'sg_ln_g', 'sg_ln_b', 'sg_w', 'sg_b']
TWIN_DIFF_INPUT = 'x'
TWIN_INPUTS = ['x', 'pre_norm', 'post_norm', 'w_in', 'w_out', 'q_norm', 'k_norm', 'conv_dw', 'conv_dw_b', 'conv_ln_g', 'conv_ln_b', 'sg_ln_g', 'sg_ln_b', 'sg_w', 'sg_b', 'loss_target', 'm_pre_norm', 'm_post_norm', 'm_w_in', 'm_w_out', 'm_q_norm', 'm_k_norm', 'm_conv_dw', 'm_conv_dw_b', 'm_conv_ln_g', 'm_conv_ln_b', 'm_sg_ln_g', 'm_sg_ln_b', 'm_sg_w', 'm_sg_b', 'v_pre_norm', 'v_post_norm', 'v_w_in', 'v_w_out', 'v_q_norm', 'v_k_norm', 'v_conv_dw', 'v_conv_dw_b', 'v_conv_ln_g', 'v_conv_ln_b', 'v_sg_ln_g', 'v_sg_ln_b', 'v_sg_w', 'v_sg_b']
TWIN_OUTPUTS = ['loss', 'grad_x', 'grad_pre_norm', 'grad_post_norm', 'grad_w_in', 'grad_w_out', 'grad_q_norm', 'grad_k_norm', 'grad_conv_dw', 'grad_conv_dw_b', 'grad_conv_ln_g', 'grad_conv_ln_b', 'grad_sg_ln_g', 'grad_sg_ln_b', 'grad_sg_w', 'grad_sg_b', 'delta_pre_norm', 'delta_post_norm', 'delta_w_in', 'delta_w_out', 'delta_q_norm', 'delta_k_norm', 'delta_conv_dw', 'delta_conv_dw_b', 'delta_conv_ln_g', 'delta_conv_ln_b', 'delta_sg_ln_g', 'delta_sg_ln_b', 'delta_sg_w', 'delta_sg_b', 'new_m_pre_norm', 'new_m_post_norm', 'new_m_w_in', 'new_m_w_out', 'new_m_q_norm', 'new_m_k_norm', 'new_m_conv_dw', 'new_m_conv_dw_b', 'new_m_conv_ln_g', 'new_m_conv_ln_b', 'new_m_sg_ln_g', 'new_m_sg_ln_b', 'new_m_sg_w', 'new_m_sg_b', 'new_v_pre_norm', 'new_v_post_norm', 'new_v_w_in', 'new_v_w_out', 'new_v_q_norm', 'new_v_k_norm', 'new_v_conv_dw', 'new_v_conv_dw_b', 'new_v_conv_ln_g', 'new_v_conv_ln_b', 'new_v_sg_ln_g', 'new_v_sg_ln_b', 'new_v_sg_w', 'new_v_sg_b']
TWIN_LEAF_KINDS = {'loss': 'loss', 'grad_x': 'grad_x', 'grad_pre_norm': 'grad_w', 'grad_post_norm': 'grad_w', 'grad_w_in': 'grad_w', 'grad_w_out': 'grad_w', 'grad_q_norm': 'grad_w', 'grad_k_norm': 'grad_w', 'grad_conv_dw': 'grad_w', 'grad_conv_dw_b': 'grad_w', 'grad_conv_ln_g': 'grad_w', 'grad_conv_ln_b': 'grad_w', 'grad_sg_ln_g': 'grad_w', 'grad_sg_ln_b': 'grad_w', 'grad_sg_w': 'grad_w', 'grad_sg_b': 'grad_w', 'delta_pre_norm': 'delta_w', 'delta_post_norm': 'delta_w', 'delta_w_in': 'delta_w', 'delta_w_out': 'delta_w', 'delta_q_norm': 'delta_w', 'delta_k_norm': 'delta_w', 'delta_conv_dw': 'delta_w', 'delta_conv_dw_b': 'delta_w', 'delta_conv_ln_g': 'delta_w', 'delta_conv_ln_b': 'delta_w', 'delta_sg_ln_g': 'delta_w', 'delta_sg_ln_b': 'delta_w', 'delta_sg_w': 'delta_w', 'delta_sg_b': 'delta_w', 'new_m_pre_norm': 'new_m', 'new_m_post_norm': 'new_m', 'new_m_w_in': 'new_m', 'new_m_w_out': 'new_m', 'new_m_q_norm': 'new_m', 'new_m_k_norm': 'new_m', 'new_m_conv_dw': 'new_m', 'new_m_conv_dw_b': 'new_m', 'new_m_conv_ln_g': 'new_m', 'new_m_conv_ln_b': 'new_m', 'new_m_sg_ln_g': 'new_m', 'new_m_sg_ln_b': 'new_m', 'new_m_sg_w': 'new_m', 'new_m_sg_b': 'new_m', 'new_v_pre_norm': 'new_v', 'new_v_post_norm': 'new_v', 'new_v_w_in': 'new_v', 'new_v_w_out': 'new_v', 'new_v_q_norm': 'new_v', 'new_v_k_norm': 'new_v', 'new_v_conv_dw': 'new_v', 'new_v_conv_dw_b': 'new_v', 'new_v_conv_ln_g': 'new_v', 'new_v_conv_ln_b': 'new_v', 'new_v_sg_ln_g': 'new_v', 'new_v_sg_ln_b': 'new_v', 'new_v_sg_w': 'new_v', 'new_v_sg_b': 'new_v'}


def _forward(args):
    return _fwd_reference(*[args[k] for k in FWD_PARAMS])


def _output_shape():
    def fwd():
        inp = _fwd_setup_inputs(0)
        return _fwd_reference(*[inp[k] for k in FWD_PARAMS])
    out = _jax.eval_shape(fwd)
    return out.shape, out.dtype

N_MICROBATCH = 1
ADAM_LR = 0.001
ADAM_B1 = 0.9
ADAM_B2 = 0.999
ADAM_EPS = 1e-08
ADAM_WD = 0.01
ADAM_STEP = 10
PER_EXAMPLE_BATCH_AXIS = {'x': 0, 'loss_target': 0}
SHARED_INPUTS = []
_WEIGHT_DTYPES = {'pre_norm': _jnp.float32, 'post_norm': _jnp.float32, 'w_in': _jnp.float32, 'w_out': _jnp.float32, 'q_norm': _jnp.float32, 'k_norm': _jnp.float32, 'conv_dw': _jnp.float32, 'conv_dw_b': _jnp.float32, 'conv_ln_g': _jnp.float32, 'conv_ln_b': _jnp.float32, 'sg_ln_g': _jnp.float32, 'sg_ln_b': _jnp.float32, 'sg_w': _jnp.float32, 'sg_b': _jnp.float32}
MOMENT_SCALE = {'pre_norm': 1.601613e+00, 'post_norm': 1.283733e+02, 'w_in': 9.657733e-01, 'w_out': 1.784795e+00, 'q_norm': 4.749474e-01, 'k_norm': 4.735368e-01, 'conv_dw': 1.074113e+00, 'conv_dw_b': 7.843296e+00, 'conv_ln_g': 2.731457e+00, 'conv_ln_b': 4.364514e+00, 'sg_ln_g': 1.110423e+00, 'sg_ln_b': 9.637739e-01, 'sg_w': 6.830318e-01, 'sg_b': 6.687282e-01}


def _to_microbatches(a, axis):
    t = _jnp.moveaxis(a, axis, 0)
    t = t.reshape((N_MICROBATCH, t.shape[0] // N_MICROBATCH) + t.shape[1:])
    return _jnp.moveaxis(t, 1, axis + 1)


def setup_inputs(seed: int = 0) -> dict:
    inp = _fwd_setup_inputs(seed)
    key = _jax.random.fold_in(_jax.random.key(seed), 7919)
    shape, _ = _output_shape()
    out = dict(inp)
    out["loss_target"] = _jax.random.normal(_jax.random.fold_in(key, 0), shape, _jnp.float32)
    for i, name in enumerate(TWIN_WEIGHTS):
        w = inp[name].astype(_jnp.float32)
        if MOMENT_SCALE is None:
            s = _jnp.sqrt(_jnp.mean(_jnp.square(w)) + 1e-30)
        else:
            s = MOMENT_SCALE[name]
        km, kv = _jax.random.split(_jax.random.fold_in(key, i + 1))
        out[name] = w
        out["m_" + name] = s * _jax.random.normal(km, w.shape, _jnp.float32)
        out["v_" + name] = (s * s) * _jax.random.uniform(kv, w.shape, _jnp.float32, 0.5, 1.5)
    if N_MICROBATCH > 1:
        for name, axis in PER_EXAMPLE_BATCH_AXIS.items():
            out[name] = _to_microbatches(out[name], axis)
    return {'x': out['x'], 'pre_norm': out['pre_norm'], 'post_norm': out['post_norm'], 'w_in': out['w_in'], 'w_out': out['w_out'], 'q_norm': out['q_norm'], 'k_norm': out['k_norm'], 'conv_dw': out['conv_dw'], 'conv_dw_b': out['conv_dw_b'], 'conv_ln_g': out['conv_ln_g'], 'conv_ln_b': out['conv_ln_b'], 'sg_ln_g': out['sg_ln_g'], 'sg_ln_b': out['sg_ln_b'], 'sg_w': out['sg_w'], 'sg_b': out['sg_b'], 'loss_target': out['loss_target'], 'm_pre_norm': out['m_pre_norm'], 'm_post_norm': out['m_post_norm'], 'm_w_in': out['m_w_in'], 'm_w_out': out['m_w_out'], 'm_q_norm': out['m_q_norm'], 'm_k_norm': out['m_k_norm'], 'm_conv_dw': out['m_conv_dw'], 'm_conv_dw_b': out['m_conv_dw_b'], 'm_conv_ln_g': out['m_conv_ln_g'], 'm_conv_ln_b': out['m_conv_ln_b'], 'm_sg_ln_g': out['m_sg_ln_g'], 'm_sg_ln_b': out['m_sg_ln_b'], 'm_sg_w': out['m_sg_w'], 'm_sg_b': out['m_sg_b'], 'v_pre_norm': out['v_pre_norm'], 'v_post_norm': out['v_post_norm'], 'v_w_in': out['v_w_in'], 'v_w_out': out['v_w_out'], 'v_q_norm': out['v_q_norm'], 'v_k_norm': out['v_k_norm'], 'v_conv_dw': out['v_conv_dw'], 'v_conv_dw_b': out['v_conv_dw_b'], 'v_conv_ln_g': out['v_conv_ln_g'], 'v_conv_ln_b': out['v_conv_ln_b'], 'v_sg_ln_g': out['v_sg_ln_g'], 'v_sg_ln_b': out['v_sg_ln_b'], 'v_sg_w': out['v_sg_w'], 'v_sg_b': out['v_sg_b']}


def _loss(weights, diff, rest, loss_target):
    with _jax.named_scope("forward"):
        args = {**rest, TWIN_DIFF_INPUT: diff, **{k: w.astype(_WEIGHT_DTYPES[k]) for k, w in weights.items()}}
        y = _forward(args)
    with _jax.named_scope("loss_head"):
        err = _jnp.square(y.astype(_jnp.float32) - loss_target)
        return 0.5 * _jnp.sum(_jnp.mean(err, axis=-1)) if err.ndim else 0.5 * err


def _adamw(w, g, m, v):
    m = ADAM_B1 * m + (1.0 - ADAM_B1) * g
    v = ADAM_B2 * v + (1.0 - ADAM_B2) * _jnp.square(g)
    m_hat = m / (1.0 - ADAM_B1 ** ADAM_STEP)
    v_hat = v / (1.0 - ADAM_B2 ** ADAM_STEP)
    delta = -ADAM_LR * (m_hat / (_jnp.sqrt(v_hat) + ADAM_EPS) + ADAM_WD * w)
    return delta, m, v


def reference(x, pre_norm, post_norm, w_in, w_out, q_norm, k_norm, conv_dw, conv_dw_b, conv_ln_g, conv_ln_b, sg_ln_g, sg_ln_b, sg_w, sg_b, loss_target, m_pre_norm, m_post_norm, m_w_in, m_w_out, m_q_norm, m_k_norm, m_conv_dw, m_conv_dw_b, m_conv_ln_g, m_conv_ln_b, m_sg_ln_g, m_sg_ln_b, m_sg_w, m_sg_b, v_pre_norm, v_post_norm, v_w_in, v_w_out, v_q_norm, v_k_norm, v_conv_dw, v_conv_dw_b, v_conv_ln_g, v_conv_ln_b, v_sg_ln_g, v_sg_ln_b, v_sg_w, v_sg_b):
    given = dict(x=x, pre_norm=pre_norm, post_norm=post_norm, w_in=w_in, w_out=w_out, q_norm=q_norm, k_norm=k_norm, conv_dw=conv_dw, conv_dw_b=conv_dw_b, conv_ln_g=conv_ln_g, conv_ln_b=conv_ln_b, sg_ln_g=sg_ln_g, sg_ln_b=sg_ln_b, sg_w=sg_w, sg_b=sg_b, loss_target=loss_target, m_pre_norm=m_pre_norm, m_post_norm=m_post_norm, m_w_in=m_w_in, m_w_out=m_w_out, m_q_norm=m_q_norm, m_k_norm=m_k_norm, m_conv_dw=m_conv_dw, m_conv_dw_b=m_conv_dw_b, m_conv_ln_g=m_conv_ln_g, m_conv_ln_b=m_conv_ln_b, m_sg_ln_g=m_sg_ln_g, m_sg_ln_b=m_sg_ln_b, m_sg_w=m_sg_w, m_sg_b=m_sg_b, v_pre_norm=v_pre_norm, v_post_norm=v_post_norm, v_w_in=v_w_in, v_w_out=v_w_out, v_q_norm=v_q_norm, v_k_norm=v_k_norm, v_conv_dw=v_conv_dw, v_conv_dw_b=v_conv_dw_b, v_conv_ln_g=v_conv_ln_g, v_conv_ln_b=v_conv_ln_b, v_sg_ln_g=v_sg_ln_g, v_sg_ln_b=v_sg_ln_b, v_sg_w=v_sg_w, v_sg_b=v_sg_b)
    weights = {n: given[n] for n in TWIN_WEIGHTS}
    shared = {n: given[n] for n in SHARED_INPUTS}
    per_example = {n: given[n] for n in ['x']}
    grad_fn = _jax.value_and_grad(_loss, argnums=(0, 1))

    def one_microbatch(ex, loss_target):
        ex = dict(ex)
        diff = ex.pop(TWIN_DIFF_INPUT)
        return grad_fn(weights, diff, {**shared, **ex}, loss_target)

    if N_MICROBATCH == 1:
        loss, (grad_w, grad_x) = one_microbatch(per_example, given["loss_target"])
    else:
        def body(carry, xs):
            loss_sum, grad_sum = carry
            l_k, (gw_k, gx_k) = one_microbatch(xs[0], xs[1])
            with _jax.named_scope("update"):
                return (loss_sum + l_k, _jax.tree.map(_jnp.add, grad_sum, gw_k)), gx_k

        init = (_jnp.zeros((), _jnp.float32), _jax.tree.map(_jnp.zeros_like, weights))
        (loss, grad_w), grad_x = _jax.lax.scan(body, init, (per_example, given["loss_target"]))
    with _jax.named_scope("update"):
        delta_w, new_m, new_v = {}, {}, {}
        for n in TWIN_WEIGHTS:
            delta_w[n], new_m[n], new_v[n] = _adamw(weights[n], grad_w[n], given["m_" + n], given["v_" + n])
    return (loss, grad_x, *[grad_w[n] for n in TWIN_WEIGHTS], *[delta_w[n] for n in TWIN_WEIGHTS],
            *[new_m[n] for n in TWIN_WEIGHTS], *[new_v[n] for n in TWIN_WEIGHTS])
```

```python
import functools

import jax
import jax.numpy as jnp
from jax import lax
from jax.experimental import pallas as pl
from jax.experimental.pallas import tpu as pltpu

F32 = jnp.float32
BF16 = jnp.bfloat16
HI = lax.Precision.HIGHEST
MESH = pl.DeviceIdType.MESH

EPS = 1e-6
D_MODEL = 1024
D_IN = 2816
HEAD_DIM = 64
LANES = 128
N_HEADS = 8
N_KV = 2
GROUP = N_HEADS // N_KV
GRID_W = 64
ROPE_THETA = 10000.0
CONV_K = 31
CONV_W = 256
SG_W = 256
SG_CHUNK = 128
HALO = 16
ATT_SCALE = HEAD_DIM ** -0.5

C_Q, C_K, C_V, C_GA, C_A1, C_A2, C_GC, C_U, C_VS, C_GS = 0, 512, 640, 768, 1280, 1536, 1792, 2048, 2304, 2560

ADAM_LR = 0.001
ADAM_B1 = 0.9
ADAM_B2 = 0.999
ADAM_EPS = 1e-08
ADAM_WD = 0.01
ADAM_STEP = 10

N_CHIPS = 4
N_DEV = 8


def _dot(a, b):
    return jnp.dot(a, b, preferred_element_type=F32)


def _dot_hi(a, b):
    return jnp.dot(a, b, precision=HI, preferred_element_type=F32)


def _dot_nt(a, b):
    return lax.dot_general(a, b, (((1,), (1,)), ((), ())), preferred_element_type=F32)


def _dot_tn(a, b):
    return lax.dot_general(a, b, (((0,), (0,)), ((), ())), preferred_element_type=F32)


def _lane(shape):
    return lax.broadcasted_iota(jnp.int32, shape, 1)


def _sigmoid(x):
    return 1.0 / (1.0 + jnp.exp(-x))


def _silu_fwd_bwd(x):
    s = _sigmoid(x)
    return x * s, s * (1.0 + x * (1.0 - s))


def _erf(x):
    x = jnp.clip(x, -4.0, 4.0)
    x2 = x * x
    a = -2.72614225801306e-10
    a = a * x2 + 2.77068142495902e-08
    a = a * x2 + -2.10102402082508e-06
    a = a * x2 + -5.69250639462346e-05
    a = a * x2 + -7.34990630326855e-04
    a = a * x2 + -2.95459980854025e-03
    a = a * x2 + -1.60960333262415e-02
    b = -1.45660718464996e-05
    b = b * x2 + -2.13374055278905e-04
    b = b * x2 + -1.68282697438203e-03
    b = b * x2 + -7.37332916720468e-03
    b = b * x2 + -1.42647390514189e-02
    return x * a / b


def _gelu_fwd_bwd(x):
    cdf = 0.5 * (1.0 + _erf(x * 0.7071067811865476))
    pdf = jnp.exp(-0.5 * x * x) * 0.3989422804014327
    return x * cdf, cdf + x * pdf


def _rms(x):
    return lax.rsqrt(jnp.mean(x * x, axis=-1, keepdims=True) + EPS)


def _ln_hat(x):
    mu = jnp.mean(x, axis=-1, keepdims=True)
    xc = x - mu
    rs = lax.rsqrt(jnp.mean(xc * xc, axis=-1, keepdims=True) + EPS)
    return xc * rs, rs


def _ln_bwd(dxh, xh, rs):
    return rs * (dxh - jnp.mean(dxh, axis=-1, keepdims=True) - xh * jnp.mean(dxh * xh, axis=-1, keepdims=True))


def _swap16(z):
    parts = []
    for i in range(z.shape[1] // LANES):
        blk = z[:, i * LANES:(i + 1) * LANES]
        lane = _lane(blk.shape)
        parts.append(jnp.where((lane & 16) == 0, pltpu.roll(blk, LANES - 16, 1), pltpu.roll(blk, 16, 1)))
    return parts[0] if len(parts) == 1 else jnp.concatenate(parts, axis=1)


def _head_slab(pair, odd):
    src = pltpu.roll(pair, HEAD_DIM, 1) if odd else pair
    return jnp.where(_lane(pair.shape) < HEAD_DIM, src, 0.0)


def _pair_merge(even, odd):
    return jnp.where(_lane(even.shape) < HEAD_DIM, even, pltpu.roll(odd, HEAD_DIM, 1))


def _heads_to_cat(ref, n_heads):
    pairs = [_pair_merge(ref[2 * p], ref[2 * p + 1]) for p in range(n_heads // 2)]
    return pairs[0] if len(pairs) == 1 else jnp.concatenate(pairs, axis=1)


def _split3(x):
    hi = x.astype(BF16).astype(F32)
    r = x - hi
    mid = r.astype(BF16).astype(F32)
    lo = (r - mid).astype(BF16).astype(F32)
    return hi, mid, lo


def _with_spare(slab, hi, mid, lo):
    lane = _lane(slab.shape)
    return jnp.where(lane == HEAD_DIM, hi, jnp.where(lane == HEAD_DIM + 1, mid, jnp.where(lane == HEAD_DIM + 2, lo, slab)))


def _with_ones(slab):
    lane = _lane(slab.shape)
    return jnp.where((lane >= HEAD_DIM) & (lane < HEAD_DIM + 3), 1.0, slab)


def _conv_window(hext_ref, prev_ref, main, next_ref, first, last, t):
    hext_ref[0:HALO, :] = jnp.where(first, 0.0, prev_ref[...])
    hext_ref[HALO:HALO + t, :] = main
    hext_ref[HALO + t:HALO + t + HALO, :] = jnp.where(last, 0.0, next_ref[...])


def _sgu_mix(v1_bf, w_ref, n_chunks):
    rows = []
    for n in range(n_chunks):
        pairs = []
        for p in range(SG_W // LANES):
            xp = v1_bf[n * SG_CHUNK:(n + 1) * SG_CHUNK, p * LANES:(p + 1) * LANES]
            me = _dot(w_ref[2 * p], xp)
            mo = _dot(w_ref[2 * p + 1], xp)
            pairs.append(jnp.where(_lane(me.shape) < HEAD_DIM, me, mo))
        rows.append(jnp.concatenate(pairs, axis=1))
    return rows[0] if len(rows) == 1 else jnp.concatenate(rows, axis=0)


def _halo_specs(t, s, width):
    per = t // HALO
    nblk = s // HALO
    prev = pl.BlockSpec((HALO, width), lambda i: (jnp.maximum(i * per - 1, 0), 0))
    nxt = pl.BlockSpec((HALO, width), lambda i: (jnp.minimum((i + 1) * per, nblk - 1), 0))
    return prev, nxt


def _const_spec(shape):
    nd = len(shape)
    return pl.BlockSpec(shape, lambda i: (0,) * nd)


def _arb(n=1):
    return pltpu.CompilerParams(dimension_semantics=("arbitrary",) * n)


def _fwd_in(x, g_pre, w_in_bf, bd, qg, kg, cos, sin, t):
    s = x.shape[0]

    def body(x_ref, g_ref, w_ref, bd_ref, qg_ref, kg_ref, cos_ref, sin_ref,
             proj_ref, hc_ref, q_ref, k_ref, v_ref):
        xv = x_ref[...]
        h = (xv * _rms(xv) * g_ref[...]).astype(BF16)
        proj = _dot(h, w_ref[...])
        proj_ref[...] = proj
        cosv = cos_ref[...]
        sinv = sin_ref[...]
        q = proj[:, C_Q:C_K]
        qn = q * lax.rsqrt(_dot_hi(q * q, bd_ref[...]) + EPS) * qg_ref[...]
        qr = (qn * cosv + _swap16(qn) * sinv) * ATT_SCALE
        for hh in range(N_HEADS):
            pair = qr[:, (hh // 2) * LANES:(hh // 2 + 1) * LANES]
            q_ref[hh] = _head_slab(pair, hh % 2 == 1).astype(BF16)
        k = proj[:, C_K:C_V]
        kn = k * lax.rsqrt(_dot_hi(k * k, bd_ref[0:LANES, 0:LANES]) + EPS) * kg_ref[...]
        kr = kn * cosv[:, 0:LANES] + _swap16(kn) * sinv[:, 0:LANES]
        vv = proj[:, C_V:C_GA]
        for hh in range(N_KV):
            k_ref[hh] = _with_ones(_head_slab(kr, hh == 1)).astype(BF16)
            v_ref[hh] = _with_ones(_head_slab(vv, hh == 1)).astype(BF16)
        hc_ref[...] = proj[:, C_A1:C_A2] * _sigmoid(proj[:, C_A2:C_GC])

    row = lambda w: pl.BlockSpec((t, w), lambda i: (i, 0))
    heads = lambda n: pl.BlockSpec((n, t, LANES), lambda i: (0, i, 0))
    return pl.pallas_call(
        body, name="fwd_in", grid=(s // t,),
        in_specs=[row(D_MODEL), _const_spec((1, D_MODEL)), _const_spec((D_MODEL, D_IN)), _const_spec((512, 512)),
                  _const_spec((1, 512)), _const_spec((1, LANES)), row(512), row(512)],
        out_specs=[row(D_IN), row(CONV_W), heads(N_HEADS), heads(N_KV), heads(N_KV)],
        out_shape=[jax.ShapeDtypeStruct((s, D_IN), F32), jax.ShapeDtypeStruct((s, CONV_W), F32),
                   jax.ShapeDtypeStruct((N_HEADS, s, LANES), BF16), jax.ShapeDtypeStruct((N_KV, s, LANES), BF16),
                   jax.ShapeDtypeStruct((N_KV, s, LANES), BF16)],
        compiler_params=_arb(),
    )(x, g_pre, w_in_bf, bd, qg, kg, cos, sin)


def _flash_fwd(q, k, v, tq, tk):
    s = q.shape[1]
    rows = GROUP * tq
    nk = s // tk

    def body(q_ref, k_ref, v_ref, o_ref, qa_ref, m_scr, acc_scr):
        qv = q_ref[...].reshape(rows, LANES)
        m_scr[...] = jnp.full((rows, 1), -jnp.inf, F32)
        acc_scr[...] = jnp.zeros((rows, LANES), F32)

        def step(kk, carry):
            off = pl.multiple_of(kk * tk, tk)
            kc = k_ref[pl.ds(off, tk), :]
            vc = v_ref[pl.ds(off, tk), :]
            sc = _dot_nt(qv, kc)
            m_prev = m_scr[...]
            m_new = jnp.maximum(m_prev, jnp.max(sc, axis=1, keepdims=True))
            p = jnp.exp(sc - m_new)
            acc_scr[...] = jnp.exp(m_prev - m_new) * acc_scr[...] + _dot(p.astype(BF16), vc)
            m_scr[...] = m_new
            return carry

        lax.fori_loop(0, nk, step, 0)
        acc = acc_scr[...]
        lane = _lane(acc.shape)
        l = jnp.sum(jnp.where(lane == HEAD_DIM, acc, 0.0), axis=1, keepdims=True)
        o_ref[...] = jnp.where(lane < HEAD_DIM, acc / l, 0.0).reshape(GROUP, tq, LANES)
        hi, mid, lo = _split3(-(m_scr[...] + jnp.log(l)))
        qa_ref[...] = _with_spare(qv.astype(F32), hi, mid, lo).astype(BF16).reshape(GROUP, tq, LANES)

    qspec = pl.BlockSpec((GROUP, tq, LANES), lambda j, i: (j, i, 0))
    kvspec = pl.BlockSpec((None, s, LANES), lambda j, i: (j, 0, 0))
    return pl.pallas_call(
        body, name="flash_fwd", grid=(N_KV, s // tq),
        in_specs=[qspec, kvspec, kvspec],
        out_specs=[qspec, qspec],
        out_shape=[jax.ShapeDtypeStruct((N_HEADS, s, LANES), F32), jax.ShapeDtypeStruct((N_HEADS, s, LANES), BF16)],
        scratch_shapes=[pltpu.VMEM((rows, 1), F32), pltpu.VMEM((rows, LANES), F32)],
        compiler_params=_arb(2),
    )(q, k, v)


def _groups_fwd(proj_ref, o_ref, hext_ref, cw_ref, cp_ref, sp_ref, sgw_ref, sgb_ref, t):
    proj = proj_ref[...]
    r = {}
    r["att"] = _heads_to_cat(o_ref, N_HEADS)
    r["gate_a"], r["dgate_a"] = _silu_fwd_bwd(proj[:, C_GA:C_A1])
    r["att_g"] = r["att"] * r["gate_a"]
    c0 = jnp.zeros((t, CONV_W), F32) + cp_ref[0:1, :]
    for kk in range(CONV_K):
        c0 = c0 + cw_ref[kk:kk + 1, :] * hext_ref[kk + 1:kk + 1 + t, :]
    r["xh_c"], r["rs_c"] = _ln_hat(c0)
    r["c1"] = r["xh_c"] * cp_ref[1:2, :] + cp_ref[2:3, :]
    r["sg_c1"] = _sigmoid(r["c1"])
    r["c2"] = r["c1"] * r["sg_c1"]
    r["gate_c"], r["dgate_c"] = _silu_fwd_bwd(proj[:, C_GC:C_U])
    r["cnv_g"] = r["c2"] * r["gate_c"]
    r["gu"], r["dgu"] = _gelu_fwd_bwd(proj[:, C_U:C_VS])
    gv, r["dgv"] = _gelu_fwd_bwd(proj[:, C_VS:C_GS])
    r["xh_s"], r["rs_s"] = _ln_hat(gv)
    v1 = r["xh_s"] * sp_ref[0:1, :] + sp_ref[1:2, :]
    r["v1_bf"] = v1.astype(BF16)
    r["mixed"] = _sgu_mix(r["v1_bf"], sgw_ref, t // SG_CHUNK) + jnp.concatenate([sgb_ref[...]] * (t // SG_CHUNK), axis=0)
    r["um"] = r["gu"] * r["mixed"]
    r["gate_s"], r["dgate_s"] = _silu_fwd_bwd(proj[:, C_GS:D_IN])
    r["sgu_g"] = r["um"] * r["gate_s"]
    r["mc_bf"] = jnp.concatenate([r["att_g"], r["cnv_g"], r["sgu_g"]], axis=1).astype(BF16)
    return r


def _fwd_out(x, proj, o, hc, cw, cp, sp, sgw_bf, sgb, w_out_bf, g_post, t):
    s = x.shape[0]

    def body(x_ref, proj_ref, o_ref, hc_ref, hp_ref, hn_ref, cw_ref, cp_ref, sp_ref, sgw_ref, sgb_ref,
             w_ref, g_ref, mix_ref, xn_ref, hext_ref):
        i = pl.program_id(0)
        _conv_window(hext_ref, hp_ref, hc_ref[...], hn_ref, i == 0, i == pl.num_programs(0) - 1, t)
        r = _groups_fwd(proj_ref, o_ref, hext_ref, cw_ref, cp_ref, sp_ref, sgw_ref, sgb_ref, t)
        mix = _dot(r["mc_bf"], w_ref[...])
        mix_ref[...] = mix
        xn_ref[...] = x_ref[...] + mix * _rms(mix) * g_ref[...]

    row = lambda w: pl.BlockSpec((t, w), lambda i: (i, 0))
    hprev, hnext = _halo_specs(t, s, CONV_W)
    return pl.pallas_call(
        body, name="fwd_out", grid=(s // t,),
        in_specs=[row(D_MODEL), row(D_IN), pl.BlockSpec((N_HEADS, t, LANES), lambda i: (0, i, 0)), row(CONV_W),
                  hprev, hnext, _const_spec((32, CONV_W)), _const_spec((8, CONV_W)), _const_spec((8, SG_W)),
                  _const_spec((4, SG_CHUNK, SG_CHUNK)), _const_spec((SG_CHUNK, SG_W)),
                  _const_spec((D_MODEL, D_MODEL)), _const_spec((1, D_MODEL))],
        out_specs=[row(D_MODEL), row(D_MODEL)],
        out_shape=[jax.ShapeDtypeStruct((s, D_MODEL), F32), jax.ShapeDtypeStruct((s, D_MODEL), F32)],
        scratch_shapes=[pltpu.VMEM((t + 2 * HALO, CONV_W), F32)],
        compiler_params=_arb(),
    )(x, proj, o, hc, hc, hc, cw, cp, sp, sgw_bf, sgb, w_out_bf, g_post)


def _loss_grad(y, target, t):
    s = y.shape[0]

    def body(y_ref, t_ref, dy_ref, sq_ref):
        @pl.when(pl.program_id(0) == 0)
        def _():
            sq_ref[...] = jnp.zeros_like(sq_ref)

        err = y_ref[...] - t_ref[...]
        dy_ref[...] = err * (1.0 / D_MODEL)
        sq_ref[...] += jnp.sum(err * err, axis=0, keepdims=True)

    row = pl.BlockSpec((t, D_MODEL), lambda i: (i, 0))
    return pl.pallas_call(
        body, name="loss_grad", grid=(s // t,),
        in_specs=[row, row], out_specs=[row, _const_spec((1, D_MODEL))],
        out_shape=[jax.ShapeDtypeStruct((s, D_MODEL), F32), jax.ShapeDtypeStruct((1, D_MODEL), F32)],
        compiler_params=_arb(),
    )(y, target)


def _bwd_out(dy, mix, proj, o, hc, cw, cp, sp, sgw_bf, sgwt_bf, sgb, w_out_t_bf, g_post, t):
    s = dy.shape[0]
    n_chunks = t // SG_CHUNK

    def body(dy_ref, mix_ref, proj_ref, o_ref, hc_ref, hp_ref, hn_ref, cw_ref, cp_ref, sp_ref, sgw_ref, sgwt_ref,
             sgb_ref, wt_ref, g_ref,
             do_ref, dgs_ref, dc0_ref, gwo_ref, gpost_ref, gcw_ref, gvec_ref, gsgw_ref, gsgb_ref, hext_ref):
        i = pl.program_id(0)

        @pl.when(i == 0)
        def _():
            for ref in (gwo_ref, gpost_ref, gcw_ref, gvec_ref, gsgw_ref, gsgb_ref):
                ref[...] = jnp.zeros_like(ref)

        _conv_window(hext_ref, hp_ref, hc_ref[...], hn_ref, i == 0, i == pl.num_programs(0) - 1, t)
        r = _groups_fwd(proj_ref, o_ref, hext_ref, cw_ref, cp_ref, sp_ref, sgw_ref, sgb_ref, t)

        dyv = dy_ref[...]
        mix_v = mix_ref[...]
        rr = _rms(mix_v)
        gd = dyv * g_ref[...]
        dmix = rr * gd - mix_v * (rr * rr * rr * jnp.mean(gd * mix_v, axis=-1, keepdims=True))
        gpost_ref[...] += jnp.sum(dyv * mix_v * rr, axis=0, keepdims=True)
        dmix_bf = dmix.astype(BF16)
        gwo_ref[...] += _dot_tn(r["mc_bf"], dmix_bf)
        dmc = _dot(dmix_bf, wt_ref[...])

        d_att = dmc[:, 0:512]
        dg_att = d_att * r["att"] * r["dgate_a"]
        d_o = d_att * r["gate_a"]
        prod = d_o * r["att"]
        for p in range(N_HEADS // 2):
            sl = slice(p * LANES, (p + 1) * LANES)
            pr = prod[:, sl]
            tot = jnp.sum(pr, axis=1, keepdims=True)
            ev = jnp.sum(jnp.where(_lane(pr.shape) < HEAD_DIM, pr, 0.0), axis=1, keepdims=True)
            for odd, delta in ((False, ev), (True, tot - ev)):
                hi, mid, lo = _split3(-delta)
                do_ref[2 * p + int(odd)] = _with_spare(_head_slab(d_o[:, sl], odd), hi, mid, lo).astype(BF16)

        dcv = dmc[:, 512:768]
        dg_conv = dcv * r["c2"] * r["dgate_c"]
        dc1 = dcv * r["gate_c"] * (r["sg_c1"] * (1.0 + r["c1"] * (1.0 - r["sg_c1"])))
        dc0 = _ln_bwd(dc1 * cp_ref[1:2, :], r["xh_c"], r["rs_c"])
        dc0_ref[...] = dc0
        for kk in range(CONV_K):
            gcw_ref[kk:kk + 1, :] += jnp.sum(dc0 * hext_ref[kk + 1:kk + 1 + t, :], axis=0, keepdims=True)

        dsg = dmc[:, 768:1024]
        dg_sg = dsg * r["um"] * r["dgate_s"]
        du = dsg * r["mixed"] * r["gate_s"] * r["dgu"]
        dmx = dsg * r["gu"] * r["gate_s"]
        dmx_bf = dmx.astype(BF16)
        sgb_sum = dmx[0:SG_CHUNK, :]
        for n in range(1, n_chunks):
            sgb_sum = sgb_sum + dmx[n * SG_CHUNK:(n + 1) * SG_CHUNK, :]
        gsgb_ref[...] += sgb_sum
        dv1_rows = []
        for n in range(n_chunks):
            pairs = []
            for p in range(SG_W // LANES):
                rs_ = slice(n * SG_CHUNK, (n + 1) * SG_CHUNK)
                ls_ = slice(p * LANES, (p + 1) * LANES)
                dm = dmx_bf[rs_, ls_]
                xp = r["v1_bf"][rs_, ls_]
                low = _lane(dm.shape) < HEAD_DIM
                zero = jnp.zeros_like(dm)
                gsgw_ref[2 * p] += _dot_nt(jnp.where(low, dm, zero), xp)
                gsgw_ref[2 * p + 1] += _dot_nt(jnp.where(low, zero, dm), xp)
                pairs.append(jnp.where(low, _dot(sgwt_ref[2 * p], dm), _dot(sgwt_ref[2 * p + 1], dm)))
            dv1_rows.append(jnp.concatenate(pairs, axis=1))
        dv1 = dv1_rows[0] if n_chunks == 1 else jnp.concatenate(dv1_rows, axis=0)
        dvs = _ln_bwd(dv1 * sp_ref[0:1, :], r["xh_s"], r["rs_s"]) * r["dgv"]

        zrow = jnp.zeros((1, CONV_W), F32)
        gvec_ref[...] += jnp.concatenate([
            jnp.sum(dc0, axis=0, keepdims=True),
            jnp.sum(dc1 * r["xh_c"], axis=0, keepdims=True),
            jnp.sum(dc1, axis=0, keepdims=True),
            jnp.sum(dv1 * r["xh_s"], axis=0, keepdims=True),
            jnp.sum(dv1, axis=0, keepdims=True),
            zrow, zrow, zrow], axis=0)
        dgs_ref[...] = jnp.concatenate([dg_att, dg_conv, du, dvs, dg_sg], axis=1)

    row = lambda w: pl.BlockSpec((t, w), lambda i: (i, 0))
    heads = pl.BlockSpec((N_HEADS, t, LANES), lambda i: (0, i, 0))
    hprev, hnext = _halo_specs(t, s, CONV_W)
    return pl.pallas_call(
        body, name="bwd_out", grid=(s // t,),
        in_specs=[row(D_MODEL), row(D_MODEL), row(D_IN), heads, row(CONV_W), hprev, hnext,
                  _const_spec((32, CONV_W)), _const_spec((8, CONV_W)), _const_spec((8, SG_W)),
                  _const_spec((4, SG_CHUNK, SG_CHUNK)), _const_spec((4, SG_CHUNK, SG_CHUNK)),
                  _const_spec((SG_CHUNK, SG_W)), _const_spec((D_MODEL, D_MODEL)), _const_spec((1, D_MODEL))],
        out_specs=[heads, row(1536), row(CONV_W), _const_spec((D_MODEL, D_MODEL)), _const_spec((1, D_MODEL)),
                   _const_spec((32, CONV_W)), _const_spec((8, CONV_W)), _const_spec((4, SG_CHUNK, SG_CHUNK)),
                   _const_spec((SG_CHUNK, SG_W))],
        out_shape=[jax.ShapeDtypeStruct((N_HEADS, s, LANES), BF16), jax.ShapeDtypeStruct((s, 1536), F32),
                   jax.ShapeDtypeStruct((s, CONV_W), F32), jax.ShapeDtypeStruct((D_MODEL, D_MODEL), F32),
                   jax.ShapeDtypeStruct((1, D_MODEL), F32), jax.ShapeDtypeStruct((32, CONV_W), F32),
                   jax.ShapeDtypeStruct((8, CONV_W), F32), jax.ShapeDtypeStruct((4, SG_CHUNK, SG_CHUNK), F32),
                   jax.ShapeDtypeStruct((SG_CHUNK, SG_W), F32)],
        scratch_shapes=[pltpu.VMEM((t + 2 * HALO, CONV_W), F32)],
        compiler_params=_arb(),
    )(dy, mix, proj, o, hc, hc, hc, cw, cp, sp, sgw_bf, sgwt_bf, sgb, w_out_t_bf, g_post)


def _flash_bwd_kv(qa, doa, k, v, tk, tqc):
    s = qa.shape[1]
    nq = s // tqc

    def body(qa_ref, do_ref, k_ref, v_ref, dk_ref, dv_ref):
        @pl.when(pl.program_id(2) == 0)
        def _():
            dk_ref[...] = jnp.zeros_like(dk_ref)
            dv_ref[...] = jnp.zeros_like(dv_ref)

        kc = k_ref[...]
        vc = v_ref[...]

        def step(qi, carry):
            off = pl.multiple_of(qi * tqc, tqc)
            qc = qa_ref[pl.ds(off, tqc), :]
            dc = do_ref[pl.ds(off, tqc), :]
            p_t = jnp.exp(_dot_nt(kc, qc))
            ds_t = p_t * _dot_nt(vc, dc)
            dv_ref[...] += _dot(p_t.astype(BF16), dc)
            dk_ref[...] += _dot(ds_t.astype(BF16), qc)
            return carry

        lax.fori_loop(0, nq, step, 0)

    hspec = pl.BlockSpec((None, s, LANES), lambda j, kk, h: (GROUP * j + h, 0, 0))
    kvspec = pl.BlockSpec((None, tk, LANES), lambda j, kk, h: (j, kk, 0))
    return pl.pallas_call(
        body, name="flash_bwd_kv", grid=(N_KV, s // tk, GROUP),
        in_specs=[hspec, hspec, kvspec, kvspec],
        out_specs=[kvspec, kvspec],
        out_shape=[jax.ShapeDtypeStruct((N_KV, s, LANES), F32), jax.ShapeDtypeStruct((N_KV, s, LANES), F32)],
        compiler_params=_arb(3),
    )(qa, doa, k, v)


def _flash_bwd_q(qa, doa, k, v, tq, tk):
    s = qa.shape[1]
    rows = GROUP * tq
    nk = s // tk

    def body(qa_ref, do_ref, k_ref, v_ref, dq_ref, acc_scr):
        qv = qa_ref[...].reshape(rows, LANES)
        dv_ = do_ref[...].reshape(rows, LANES)
        acc_scr[...] = jnp.zeros((rows, LANES), F32)

        def step(kk, carry):
            off = pl.multiple_of(kk * tk, tk)
            kc = k_ref[pl.ds(off, tk), :]
            vc = v_ref[pl.ds(off, tk), :]
            ds = jnp.exp(_dot_nt(qv, kc)) * _dot_nt(dv_, vc)
            acc_scr[...] += _dot(ds.astype(BF16), kc)
            return carry

        lax.fori_loop(0, nk, step, 0)
        dq_ref[...] = acc_scr[...].reshape(GROUP, tq, LANES)

    qspec = pl.BlockSpec((GROUP, tq, LANES), lambda j, i: (j, i, 0))
    kvspec = pl.BlockSpec((None, s, LANES), lambda j, i: (j, 0, 0))
    return pl.pallas_call(
        body, name="flash_bwd_q", grid=(N_KV, s // tq),
        in_specs=[qspec, qspec, kvspec, kvspec],
        out_specs=qspec,
        out_shape=jax.ShapeDtypeStruct((N_HEADS, s, LANES), F32),
        scratch_shapes=[pltpu.VMEM((rows, LANES), F32)],
        compiler_params=_arb(2),
    )(qa, doa, k, v)


def _bwd_in(dy, x, proj, dq, dk, dv, dgs, dc0, cw, g_pre, w_in_t_bf, bd, qg, kg, cos, sin, t):
    s = x.shape[0]

    def body(dy_ref, x_ref, proj_ref, dq_ref, dk_ref, dv_ref, dgs_ref, dc_ref, dcp_ref, dcn_ref, cw_ref, g_ref,
             wt_ref, bd_ref, qg_ref, kg_ref, cos_ref, sin_ref,
             dx_ref, dproj_ref, h_ref, gpre_ref, gq_ref, gk_ref, dext_ref):
        i = pl.program_id(0)

        @pl.when(i == 0)
        def _():
            for ref in (gpre_ref, gq_ref, gk_ref):
                ref[...] = jnp.zeros_like(ref)

        proj = proj_ref[...]
        cosv = cos_ref[...]
        sinv = sin_ref[...]

        def head_norm_bwd(dr, z, bdm, g, cs, sn, gacc_ref):
            dn = dr * cs + _swap16(dr * sn)
            rr = lax.rsqrt(_dot_hi(z * z, bdm) + EPS)
            gdn = dn * g
            gacc_ref[...] += jnp.sum(dn * z * rr, axis=0, keepdims=True)
            return rr * gdn - z * (rr * rr * rr * _dot_hi(gdn * z, bdm))

        dq_cat = _heads_to_cat(dq_ref, N_HEADS) * ATT_SCALE
        dzq = head_norm_bwd(dq_cat, proj[:, C_Q:C_K], bd_ref[...], qg_ref[...], cosv, sinv, gq_ref)
        dk_cat = _heads_to_cat(dk_ref, N_KV)
        dzk = head_norm_bwd(dk_cat, proj[:, C_K:C_V], bd_ref[0:LANES, 0:LANES], kg_ref[...],
                            cosv[:, 0:LANES], sinv[:, 0:LANES], gk_ref)
        dv_cat = _heads_to_cat(dv_ref, N_KV)

        _conv_window(dext_ref, dcp_ref, dc_ref[...], dcn_ref, i == 0, i == pl.num_programs(0) - 1, t)
        dhc = jnp.zeros((t, CONV_W), F32)
        for kk in range(CONV_K):
            dhc = dhc + cw_ref[kk:kk + 1, :] * dext_ref[CONV_K - kk:CONV_K - kk + t, :]
        sg = _sigmoid(proj[:, C_A2:C_GC])
        da1 = dhc * sg
        da2 = dhc * proj[:, C_A1:C_A2] * sg * (1.0 - sg)

        dgs = dgs_ref[...]
        dproj_bf = jnp.concatenate([dzq, dzk, dv_cat, dgs[:, 0:512], da1, da2, dgs[:, 512:1536]], axis=1).astype(BF16)
        dproj_ref[...] = dproj_bf
        dh = _dot(dproj_bf, wt_ref[...])

        xv = x_ref[...]
        rr = _rms(xv)
        gv = g_ref[...]
        h_ref[...] = (xv * rr * gv).astype(BF16)
        gdh = dh * gv
        gpre_ref[...] += jnp.sum(dh * xv * rr, axis=0, keepdims=True)
        dx_ref[...] = dy_ref[...] + rr * gdh - xv * (rr * rr * rr * jnp.mean(gdh * xv, axis=-1, keepdims=True))

    row = lambda w: pl.BlockSpec((t, w), lambda i: (i, 0))
    heads = lambda n: pl.BlockSpec((n, t, LANES), lambda i: (0, i, 0))
    hprev, hnext = _halo_specs(t, s, CONV_W)
    return pl.pallas_call(
        body, name="bwd_in", grid=(s // t,),
        in_specs=[row(D_MODEL), row(D_MODEL), row(D_IN), heads(N_HEADS), heads(N_KV), heads(N_KV), row(1536),
                  row(CONV_W), hprev, hnext, _const_spec((32, CONV_W)), _const_spec((1, D_MODEL)),
                  _const_spec((D_IN, D_MODEL)), _const_spec((512, 512)), _const_spec((1, 512)),
                  _const_spec((1, LANES)), row(512), row(512)],
        out_specs=[row(D_MODEL), row(D_IN), row(D_MODEL), _const_spec((1, D_MODEL)), _const_spec((1, 512)),
                   _const_spec((1, LANES))],
        out_shape=[jax.ShapeDtypeStruct((s, D_MODEL), F32), jax.ShapeDtypeStruct((s, D_IN), BF16),
                   jax.ShapeDtypeStruct((s, D_MODEL), BF16), jax.ShapeDtypeStruct((1, D_MODEL), F32),
                   jax.ShapeDtypeStruct((1, 512), F32), jax.ShapeDtypeStruct((1, LANES), F32)],
        scratch_shapes=[pltpu.VMEM((t + 2 * HALO, CONV_W), F32)],
        compiler_params=_arb(),
    )(dy, x, proj, dq, dk, dv, dgs, dc0, dc0, dc0, cw, g_pre, w_in_t_bf, bd, qg, kg, cos, sin)


def _grad_w_in(h_bf, dproj_bf, t):
    s = h_bf.shape[0]
    half = D_IN // 2

    def body(h_ref, d_ref, g_ref):
        @pl.when(pl.program_id(1) == 0)
        def _():
            g_ref[...] = jnp.zeros_like(g_ref)

        g_ref[...] += _dot_tn(h_ref[...], d_ref[...])

    return pl.pallas_call(
        body, name="grad_w_in", grid=(2, s // t),
        in_specs=[pl.BlockSpec((t, D_MODEL), lambda j, i: (i, 0)), pl.BlockSpec((t, half), lambda j, i: (i, j))],
        out_specs=pl.BlockSpec((D_MODEL, half), lambda j, i: (0, j)),
        out_shape=jax.ShapeDtypeStruct((D_MODEL, D_IN), F32),
        compiler_params=_arb(2),
    )(h_bf, dproj_bf)


def _place():
    x, y, c = lax.axis_index("x"), lax.axis_index("y"), lax.axis_index("c")
    chips = [(1 - x, y), (x, 1 - y), (1 - x, 1 - y)]
    return x, y, c, chips


def _any_specs(n):
    return [pl.BlockSpec(memory_space=pl.ANY)] * n


def _gather_weights(w_in_bf, w_out_bf, cdw):
    arrs = (w_in_bf, w_out_bf, cdw)
    n = len(arrs)

    def body(*refs):
        ins, outs = refs[:n], refs[n:2 * n]
        send_sems, recv_sems, local_sems = refs[2 * n:]
        x, y, c, chips = _place()
        mine = 2 * x + y
        local = [pltpu.make_async_copy(ins[a], outs[a].at[mine], local_sems.at[a]) for a in range(n)]
        for cp in local:
            cp.start()

        def copy(j, a, slot, to):
            return pltpu.make_async_remote_copy(src_ref=ins[a], dst_ref=outs[a].at[slot], send_sem=send_sems.at[n * j + a],
                                                recv_sem=recv_sems.at[n * j + a], device_id=to, device_id_type=MESH)

        sends = [copy(j, a, mine, (px, py, c)) for j, (px, py) in enumerate(chips) for a in range(n)]
        for cp in sends:
            cp.start()
        for j, (px, py) in enumerate(chips):
            for a in range(n):
                copy(j, a, 2 * px + py, (px, py, c)).wait_recv()
        for cp in sends:
            cp.wait_send()
        for cp in local:
            cp.wait()

    return pl.pallas_call(
        body, name="gather_weights",
        in_specs=_any_specs(n), out_specs=_any_specs(n),
        out_shape=[jax.ShapeDtypeStruct((N_CHIPS,) + a.shape, a.dtype) for a in arrs],
        scratch_shapes=[pltpu.SemaphoreType.DMA((3 * n,)), pltpu.SemaphoreType.DMA((3 * n,)), pltpu.SemaphoreType.DMA((n,))],
    )(*arrs)


def _scatter_grads(gin_pieces, gw_out):
    depth = gw_out.shape[0]
    rows = gw_out.shape[1] // N_CHIPS

    def body(gin_ref, gout_ref, rin_ref, rout_ref, send_sems, recv_sems, local_sems):
        x, y, c, chips = _place()
        mine = 2 * x + y

        def out_rows(chip):
            return gout_ref.at[:, pl.ds(pl.multiple_of(chip * rows, rows), rows), :]

        local = [pltpu.make_async_copy(gin_ref.at[mine], rin_ref.at[mine], local_sems.at[0]),
                 pltpu.make_async_copy(out_rows(mine), rout_ref.at[mine], local_sems.at[1])]
        for cp in local:
            cp.start()

        def copies(j, shard, slot, to):
            kw = dict(device_id=to, device_id_type=MESH)
            return [pltpu.make_async_remote_copy(src_ref=gin_ref.at[shard], dst_ref=rin_ref.at[slot],
                                                 send_sem=send_sems.at[2 * j], recv_sem=recv_sems.at[2 * j], **kw),
                    pltpu.make_async_remote_copy(src_ref=out_rows(shard), dst_ref=rout_ref.at[slot],
                                                 send_sem=send_sems.at[2 * j + 1], recv_sem=recv_sems.at[2 * j + 1], **kw)]

        sends = [cp for j, (px, py) in enumerate(chips) for cp in copies(j, 2 * px + py, mine, (px, py, c))]
        for cp in sends:
            cp.start()
        for j, (px, py) in enumerate(chips):
            for cp in copies(j, mine, 2 * px + py, (px, py, c)):
                cp.wait_recv()
        for cp in sends:
            cp.wait_send()
        for cp in local:
            cp.wait()

    return pl.pallas_call(
        body, name="scatter_grads",
        in_specs=_any_specs(2), out_specs=_any_specs(2),
        out_shape=[jax.ShapeDtypeStruct(gin_pieces.shape, F32),
                   jax.ShapeDtypeStruct((N_CHIPS, depth, rows, gw_out.shape[2]), F32)],
        scratch_shapes=[pltpu.SemaphoreType.DMA((6,)), pltpu.SemaphoreType.DMA((6,)), pltpu.SemaphoreType.DMA((2,))],
    )(gin_pieces, gw_out)


def _sum_chips(parts, rb, name):
    _, depth, r, cdim = parts.shape

    def body(p_ref, o_ref):
        o_ref[...] = ((p_ref[0] + p_ref[1]) + p_ref[2]) + p_ref[3]

    return pl.pallas_call(
        body, name=name, grid=(depth, r // rb),
        in_specs=[pl.BlockSpec((N_CHIPS, None, rb, cdim), lambda l, i: (0, l, i, 0))],
        out_specs=pl.BlockSpec((None, rb, cdim), lambda l, i: (l, i, 0)),
        out_shape=jax.ShapeDtypeStruct((depth, r, cdim), F32),
        compiler_params=_arb(2),
    )(parts)


def _swap_with_sibling(a, b):
    arrs = (a, b)
    n = len(arrs)

    def body(*refs):
        ins, outs = refs[:n], refs[n:2 * n]
        send_sems, recv_sems = refs[2 * n:]
        x, y, c, _ = _place()
        cps = [pltpu.make_async_remote_copy(src_ref=ins[k], dst_ref=outs[k], send_sem=send_sems.at[k],
                                            recv_sem=recv_sems.at[k], device_id=(x, y, 1 - c), device_id_type=MESH)
               for k in range(n)]
        for cp in cps:
            cp.start()
        for cp in cps:
            cp.wait()

    return pl.pallas_call(
        body, name="swap_with_sibling",
        in_specs=_any_specs(n), out_specs=_any_specs(n),
        out_shape=[jax.ShapeDtypeStruct(v.shape, v.dtype) for v in arrs],
        scratch_shapes=[pltpu.SemaphoreType.DMA((n,)), pltpu.SemaphoreType.DMA((n,))],
    )(*arrs)


def _allreduce_small(slab):
    m, n = slab.shape

    def body(x_ref, out_ref, gath, send_sems, recv_sems, local_sem):
        x, y, c, chips = _place()
        me, sibling = (x, y, c), (x, y, 1 - c)

        def rows(px, py, pc):
            return gath.at[pl.ds(pl.multiple_of((4 * px + 2 * py + pc) * m, 8), m), :]

        def copy(k, block, to, src=None):
            return pltpu.make_async_remote_copy(src_ref=rows(*block) if src is None else src, dst_ref=rows(*block),
                                                send_sem=send_sems.at[k], recv_sem=recv_sems.at[k],
                                                device_id=to, device_id_type=MESH)

        mine = pltpu.make_async_copy(x_ref, rows(*me), local_sem)
        mine.start()
        first = [copy(0, me, sibling, src=x_ref)]
        first += [copy(1 + j, me, (*chip, c), src=x_ref) for j, chip in enumerate(chips)]
        for cp in first:
            cp.start()
        passed = [copy(4 + j, (*chip, c), sibling) for j, chip in enumerate(chips)]
        for j, chip in enumerate(chips):
            copy(1 + j, (*chip, c), me).wait_recv()
            passed[j].start()
        copy(0, sibling, me).wait_recv()
        for j, chip in enumerate(chips):
            copy(4 + j, (*chip, 1 - c), me).wait_recv()
        for cp in first + passed:
            cp.wait_send()
        mine.wait()
        total = gath[0:m, :]
        for d in range(1, N_DEV):
            total = total + gath[d * m:(d + 1) * m, :]
        out_ref[...] = total

    return pl.pallas_call(
        body, name="allreduce_small",
        in_specs=[pl.BlockSpec(memory_space=pltpu.VMEM)],
        out_specs=pl.BlockSpec(memory_space=pltpu.VMEM),
        out_shape=jax.ShapeDtypeStruct((m, n), F32),
        scratch_shapes=[pltpu.VMEM((N_DEV * m, n), F32), pltpu.SemaphoreType.DMA((7,)), pltpu.SemaphoreType.DMA((7,)),
                        pltpu.SemaphoreType.DMA],
    )(slab)


def _adamw(w, ga, gb, m, v, rb, name):
    depth, r, cdim = w.shape

    def body(w_ref, ga_ref, gb_ref, m_ref, v_ref, g_out, d_out, m_out, v_out):
        g = ga_ref[...] + gb_ref[...]
        m2 = ADAM_B1 * m_ref[...] + (1.0 - ADAM_B1) * g
        v2 = ADAM_B2 * v_ref[...] + (1.0 - ADAM_B2) * (g * g)
        m_hat = m2 / (1.0 - ADAM_B1 ** ADAM_STEP)
        v_hat = v2 / (1.0 - ADAM_B2 ** ADAM_STEP)
        g_out[...] = g
        d_out[...] = -ADAM_LR * (m_hat / (jnp.sqrt(v_hat) + ADAM_EPS) + ADAM_WD * w_ref[...])
        m_out[...] = m2
        v_out[...] = v2

    spec = pl.BlockSpec((None, rb, cdim), lambda l, i: (l, i, 0))
    shp = jax.ShapeDtypeStruct((depth, r, cdim), F32)
    return pl.pallas_call(
        body, name=name, grid=(depth, r // rb),
        in_specs=[spec] * 5, out_specs=[spec] * 4, out_shape=[shp] * 4,
        compiler_params=_arb(2),
    )(w, ga, gb, m, v)


def _rope_tables(s):
    t = jnp.arange(s, dtype=jnp.int32)
    row = (t // GRID_W).astype(F32)
    col = (t % GRID_W).astype(F32)
    half = HEAD_DIM // 4
    inv_freq = ROPE_THETA ** (-jnp.arange(half, dtype=F32) / half)
    ar = row[:, None] * inv_freq[None, :]
    ac = col[:, None] * inv_freq[None, :]
    cos = jnp.concatenate([jnp.cos(ar), jnp.cos(ar), jnp.cos(ac), jnp.cos(ac)], axis=1)
    sin = jnp.concatenate([-jnp.sin(ar), jnp.sin(ar), -jnp.sin(ac), jnp.sin(ac)], axis=1)
    return jnp.tile(cos, (1, N_HEADS)), jnp.tile(sin, (1, N_HEADS))


def _pad_rows(a, rows):
    return jnp.concatenate([a, jnp.zeros((rows - a.shape[0],) + a.shape[1:], a.dtype)], axis=0)


_SMALL = ("pre_norm", "post_norm", "q_norm", "k_norm", "conv_dw_b", "conv_ln_g", "conv_ln_b", "sg_ln_g", "sg_ln_b",
          "sg_w", "sg_b")


def _pack(parts):
    flat = jnp.concatenate([p.reshape(-1, LANES) for p in parts], axis=0)
    return _pad_rows(flat, -(-flat.shape[0] // 8) * 8)


def _unpack(slab, shapes):
    out, r = [], 0
    for shp in shapes:
        n = 1
        for d in shp:
            n *= d
        out.append(slab[r:r + n // LANES].reshape(shp))
        r += n // LANES
    return out


def kernel(x, pre_norm, post_norm, w_in, w_out, q_norm, k_norm, conv_dw, conv_dw_b, conv_ln_g, conv_ln_b, sg_ln_g, sg_ln_b, sg_w, sg_b, loss_target, m_pre_norm, m_post_norm, m_w_in, m_w_out, m_q_norm, m_k_norm, m_conv_dw, m_conv_dw_b, m_conv_ln_g, m_conv_ln_b, m_sg_ln_g, m_sg_ln_b, m_sg_w, m_sg_b, v_pre_norm, v_post_norm, v_w_in, v_w_out, v_q_norm, v_k_norm, v_conv_dw, v_conv_dw_b, v_conv_ln_g, v_conv_ln_b, v_sg_ln_g, v_sg_ln_b, v_sg_w, v_sg_b):
    depth = w_in.shape[0]
    s = x.shape[1]
    assert x.shape[0] == 1 and s % SG_CHUNK == 0 and x.shape[2] == D_MODEL
    t = min(256, s)
    tq = min(256, s)
    tk = min(512, s)
    shard_cols = w_in.shape[2]
    chip = 2 * lax.axis_index("x") + lax.axis_index("y")

    gin, gout, gcdw = _gather_weights(w_in.astype(BF16), w_out.astype(BF16), conv_dw)
    w_in_bf = jnp.concatenate([gin[j] for j in range(N_CHIPS)], axis=2)
    w_out_bf = jnp.concatenate([gout[j] for j in range(N_CHIPS)], axis=1)
    cdw_full = jnp.concatenate([gcdw[j] for j in range(N_CHIPS)], axis=2)
    w_in_t_bf = jnp.swapaxes(w_in_bf, 1, 2)
    w_out_t_bf = jnp.swapaxes(w_out_bf, 1, 2)
    sgw_bf = sg_w.astype(BF16)
    sgwt_bf = jnp.swapaxes(sg_w, 2, 3).astype(BF16)

    cos, sin = _rope_tables(s)
    bd = jnp.kron(jnp.eye(N_HEADS, dtype=F32), jnp.full((HEAD_DIM, HEAD_DIM), 1.0 / HEAD_DIM, F32))

    def layer_consts(l):
        cw = _pad_rows(cdw_full[l], 32)
        cp = _pad_rows(jnp.stack([conv_dw_b[l], conv_ln_g[l], conv_ln_b[l]]), 8)
        sp = _pad_rows(jnp.stack([sg_ln_g[l], sg_ln_b[l]]), 8)
        sgb = jnp.repeat(sg_b[l].T, HEAD_DIM, axis=1)
        qg = jnp.tile(q_norm[l], N_HEADS)[None, :]
        kg = jnp.tile(k_norm[l], N_KV)[None, :]
        return cw, cp, sp, sgb, qg, kg

    xs = [x[0]]
    saved = []
    for l in range(depth):
        cw, cp, sp, sgb, qg, kg = layer_consts(l)
        proj, hc, q, k, v = _fwd_in(xs[l], pre_norm[l][None, :], w_in_bf[l], bd, qg, kg, cos, sin, t)
        o, qa = _flash_fwd(q, k, v, tq, tk)
        mix, xn = _fwd_out(xs[l], proj, o, hc, cw, cp, sp, sgw_bf[l], sgb, w_out_bf[l], post_norm[l][None, :], t)
        saved.append((proj, hc, qa, k, v, o, mix))
        xs.append(xn)

    dy, sq = _loss_grad(xs[depth], loss_target[0], t)
    loss = lax.psum(0.5 * jnp.sum(sq) / D_MODEL, ("x", "y", "c"))

    g_w_in, g_w_out, g_small = [], [], {n: [] for n in _SMALL + ("conv_dw",)}
    for l in reversed(range(depth)):
        cw, cp, sp, sgb, qg, kg = layer_consts(l)
        proj, hc, qa, k, v, o, mix = saved[l]
        doa, dgs, dc0, gwo, gpost, gcw, gvec, gsgw, gsgb = _bwd_out(
            dy, mix, proj, o, hc, cw, cp, sp, sgw_bf[l], sgwt_bf[l], sgb, w_out_t_bf[l], post_norm[l][None, :], t)
        dk, dv = _flash_bwd_kv(qa, doa, k, v, tk, min(512, s))
        dq = _flash_bwd_q(qa, doa, k, v, tq, tk)
        dy, dproj_bf, h_bf, gpre, gq, gk = _bwd_in(dy, xs[l], proj, dq, dk, dv, dgs, dc0, cw, pre_norm[l][None, :],
                                                  w_in_t_bf[l], bd, qg, kg, cos, sin, t)
        g_w_in.append(_grad_w_in(h_bf, dproj_bf, min(512, s)))
        g_w_out.append(gwo)
        g_small["pre_norm"].append(gpre[0])
        g_small["post_norm"].append(gpost[0])
        g_small["q_norm"].append(gq[0].reshape(N_HEADS, HEAD_DIM).sum(0))
        g_small["k_norm"].append(gk[0].reshape(N_KV, HEAD_DIM).sum(0))
        g_small["conv_dw"].append(gcw[:CONV_K])
        g_small["conv_dw_b"].append(gvec[0])
        g_small["conv_ln_g"].append(gvec[1])
        g_small["conv_ln_b"].append(gvec[2])
        g_small["sg_ln_g"].append(gvec[3])
        g_small["sg_ln_b"].append(gvec[4])
        g_small["sg_w"].append(gsgw)
        g_small["sg_b"].append(gsgb.reshape(SG_CHUNK, SG_W // HEAD_DIM, HEAD_DIM).sum(-1).T)
    grad_x = dy[None]
    g_w_in = jnp.stack(g_w_in[::-1])
    g_w_out = jnp.stack(g_w_out[::-1])
    g_small = {n: jnp.stack(vals[::-1]) for n, vals in g_small.items()}

    gin_pieces = jnp.stack([g_w_in[:, :, j * shard_cols:(j + 1) * shard_cols] for j in range(N_CHIPS)])
    rin, rout = _scatter_grads(gin_pieces, g_w_out)
    s_in = _sum_chips(rin, 256, "sum_chips_w_in")
    s_out = _sum_chips(rout, 256, "sum_chips_w_out")
    t_in, t_out = _swap_with_sibling(s_in, s_out)
    grad_w_in, delta_w_in, new_m_w_in, new_v_w_in = _adamw(w_in, s_in, t_in, m_w_in, v_w_in, 256, "adamw_w_in")
    grad_w_out, delta_w_out, new_m_w_out, new_v_w_out = _adamw(w_out, s_out, t_out, m_w_out, v_w_out, 256, "adamw_w_out")

    small_w = dict(pre_norm=pre_norm, post_norm=post_norm, q_norm=q_norm, k_norm=k_norm, conv_dw_b=conv_dw_b,
                   conv_ln_g=conv_ln_g, conv_ln_b=conv_ln_b, sg_ln_g=sg_ln_g, sg_ln_b=sg_ln_b, sg_w=sg_w, sg_b=sg_b)
    small_m = dict(pre_norm=m_pre_norm, post_norm=m_post_norm, q_norm=m_q_norm, k_norm=m_k_norm, conv_dw_b=m_conv_dw_b,
                   conv_ln_g=m_conv_ln_g, conv_ln_b=m_conv_ln_b, sg_ln_g=m_sg_ln_g, sg_ln_b=m_sg_ln_b, sg_w=m_sg_w,
                   sg_b=m_sg_b)
    small_v = dict(pre_norm=v_pre_norm, post_norm=v_post_norm, q_norm=v_q_norm, k_norm=v_k_norm, conv_dw_b=v_conv_dw_b,
                   conv_ln_g=v_conv_ln_g, conv_ln_b=v_conv_ln_b, sg_ln_g=v_sg_ln_g, sg_ln_b=v_sg_ln_b, sg_w=v_sg_w,
                   sg_b=v_sg_b)
    shapes = [small_w[n].shape for n in _SMALL]
    red = _allreduce_small(_pack([g_small[n] for n in _SMALL] + [g_small["conv_dw"]]))
    n_rep = sum(small_w[n].size for n in _SMALL) // LANES
    g_cdw_full = red[n_rep:n_rep + g_small["conv_dw"].size // LANES].reshape(g_small["conv_dw"].shape)
    cdw_cols = conv_dw.shape[2]
    g_cdw = lax.dynamic_slice_in_dim(g_cdw_full, chip * cdw_cols, cdw_cols, axis=2)
    g_slab = _pack([red[:n_rep], g_cdw])
    w_slab = _pack([small_w[n] for n in _SMALL] + [conv_dw])
    m_slab = _pack([small_m[n] for n in _SMALL] + [m_conv_dw])
    v_slab = _pack([small_v[n] for n in _SMALL] + [v_conv_dw])
    rows = w_slab.shape[0]
    outs = _adamw(w_slab[None], g_slab[None], jnp.zeros_like(g_slab)[None], m_slab[None], v_slab[None], rows, "adamw_small")
    unpacked = [dict(zip(_SMALL + ("conv_dw",), _unpack(o_[0], shapes + [conv_dw.shape]))) for o_ in outs]

    big = [dict(w_in=a, w_out=b) for a, b in ((grad_w_in, grad_w_out), (delta_w_in, delta_w_out),
                                             (new_m_w_in, new_m_w_out), (new_v_w_in, new_v_w_out))]
    order = ("pre_norm", "post_norm", "w_in", "w_out", "q_norm", "k_norm", "conv_dw", "conv_dw_b", "conv_ln_g",
             "conv_ln_b", "sg_ln_g", "sg_ln_b", "sg_w", "sg_b")
    result = [loss, grad_x]
    for kind in range(4):
        for name in order:
            result.append(big[kind][name] if name in big[kind] else unpacked[kind][name])
    return tuple(result)
```

```python
import functools

import jax
import jax.numpy as jnp
from jax import lax
from jax.experimental import pallas as pl
from jax.experimental.pallas import tpu as pltpu

F32 = jnp.float32
BF16 = jnp.bfloat16
HI = lax.Precision.HIGHEST
MESH = pl.DeviceIdType.MESH

EPS = 1e-6
D_MODEL = 1024
D_IN = 2816
HEAD_DIM = 64
LANES = 128
N_HEADS = 8
N_KV = 2
GROUP = N_HEADS // N_KV
GRID_W = 64
ROPE_THETA = 10000.0
CONV_K = 31
CONV_W = 256
SG_W = 256
SG_CHUNK = 128
HALO = 16
ATT_SCALE = HEAD_DIM ** -0.5

C_Q, C_K, C_V, C_GA, C_A1, C_A2, C_GC, C_U, C_VS, C_GS = 0, 512, 640, 768, 1280, 1536, 1792, 2048, 2304, 2560

ADAM_LR = 0.001
ADAM_B1 = 0.9
ADAM_B2 = 0.999
ADAM_EPS = 1e-08
ADAM_WD = 0.01
ADAM_STEP = 10

N_CHIPS = 4
N_DEV = 8


def _dot(a, b):
    return jnp.dot(a, b, preferred_element_type=F32)


def _dot_hi(a, b):
    return jnp.dot(a, b, precision=HI, preferred_element_type=F32)


def _dot_nt(a, b):
    return lax.dot_general(a, b, (((1,), (1,)), ((), ())), preferred_element_type=F32)


def _dot_tn(a, b):
    return lax.dot_general(a, b, (((0,), (0,)), ((), ())), preferred_element_type=F32)


def _lane(shape):
    return lax.broadcasted_iota(jnp.int32, shape, 1)


def _sigmoid(x):
    return 1.0 / (1.0 + jnp.exp(-x))


def _silu_fwd_bwd(x):
    s = _sigmoid(x)
    return x * s, s * (1.0 + x * (1.0 - s))


def _erf(x):
    x = jnp.clip(x, -4.0, 4.0)
    x2 = x * x
    a = -2.72614225801306e-10
    a = a * x2 + 2.77068142495902e-08
    a = a * x2 + -2.10102402082508e-06
    a = a * x2 + -5.69250639462346e-05
    a = a * x2 + -7.34990630326855e-04
    a = a * x2 + -2.95459980854025e-03
    a = a * x2 + -1.60960333262415e-02
    b = -1.45660718464996e-05
    b = b * x2 + -2.13374055278905e-04
    b = b * x2 + -1.68282697438203e-03
    b = b * x2 + -7.37332916720468e-03
    b = b * x2 + -1.42647390514189e-02
    return x * a / b


def _gelu_fwd_bwd(x):
    cdf = 0.5 * (1.0 + _erf(x * 0.7071067811865476))
    pdf = jnp.exp(-0.5 * x * x) * 0.3989422804014327
    return x * cdf, cdf + x * pdf


def _rms(x):
    return lax.rsqrt(jnp.mean(x * x, axis=-1, keepdims=True) + EPS)


def _ln_hat(x):
    mu = jnp.mean(x, axis=-1, keepdims=True)
    xc = x - mu
    rs = lax.rsqrt(jnp.mean(xc * xc, axis=-1, keepdims=True) + EPS)
    return xc * rs, rs


def _ln_bwd(dxh, xh, rs):
    return rs * (dxh - jnp.mean(dxh, axis=-1, keepdims=True) - xh * jnp.mean(dxh * xh, axis=-1, keepdims=True))


def _swap16(z):
    parts = []
    for i in range(z.shape[1] // LANES):
        blk = z[:, i * LANES:(i + 1) * LANES]
        lane = _lane(blk.shape)
        parts.append(jnp.where((lane & 16) == 0, pltpu.roll(blk, LANES - 16, 1), pltpu.roll(blk, 16, 1)))
    return parts[0] if len(parts) == 1 else jnp.concatenate(parts, axis=1)


def _head_slab(pair, odd):
    src = pltpu.roll(pair, HEAD_DIM, 1) if odd else pair
    return jnp.where(_lane(pair.shape) < HEAD_DIM, src, 0.0)


def _pair_merge(even, odd):
    return jnp.where(_lane(even.shape) < HEAD_DIM, even, pltpu.roll(odd, HEAD_DIM, 1))


def _heads_to_cat(ref, n_heads):
    pairs = [_pair_merge(ref[2 * p], ref[2 * p + 1]) for p in range(n_heads // 2)]
    return pairs[0] if len(pairs) == 1 else jnp.concatenate(pairs, axis=1)


def _split3(x):
    hi = x.astype(BF16).astype(F32)
    r = x - hi
    mid = r.astype(BF16).astype(F32)
    lo = (r - mid).astype(BF16).astype(F32)
    return hi, mid, lo


def _with_spare(slab, hi, mid, lo):
    lane = _lane(slab.shape)
    return jnp.where(lane == HEAD_DIM, hi, jnp.where(lane == HEAD_DIM + 1, mid, jnp.where(lane == HEAD_DIM + 2, lo, slab)))


def _with_ones(slab):
    lane = _lane(slab.shape)
    return jnp.where((lane >= HEAD_DIM) & (lane < HEAD_DIM + 3), 1.0, slab)


def _conv_window(hext_ref, prev_ref, main, next_ref, first, last, t):
    hext_ref[0:HALO, :] = jnp.where(first, 0.0, prev_ref[...])
    hext_ref[HALO:HALO + t, :] = main
    hext_ref[HALO + t:HALO + t + HALO, :] = jnp.where(last, 0.0, next_ref[...])


def _sgu_mix(v1_bf, w_ref, n_chunks):
    rows = []
    for n in range(n_chunks):
        pairs = []
        for p in range(SG_W // LANES):
            xp = v1_bf[n * SG_CHUNK:(n + 1) * SG_CHUNK, p * LANES:(p + 1) * LANES]
            me = _dot(w_ref[2 * p], xp)
            mo = _dot(w_ref[2 * p + 1], xp)
            pairs.append(jnp.where(_lane(me.shape) < HEAD_DIM, me, mo))
        rows.append(jnp.concatenate(pairs, axis=1))
    return rows[0] if len(rows) == 1 else jnp.concatenate(rows, axis=0)


def _halo_specs(t, s, width):
    per = t // HALO
    nblk = s // HALO
    prev = pl.BlockSpec((HALO, width), lambda i: (jnp.maximum(i * per - 1, 0), 0))
    nxt = pl.BlockSpec((HALO, width), lambda i: (jnp.minimum((i + 1) * per, nblk - 1), 0))
    return prev, nxt


def _const_spec(shape):
    nd = len(shape)
    return pl.BlockSpec(shape, lambda i: (0,) * nd)


def _arb(n=1):
    return pltpu.CompilerParams(dimension_semantics=("arbitrary",) * n)


def _fwd_in(x, g_pre, w_in_bf, bd, qg, kg, cos, sin, t):
    s = x.shape[0]

    def body(x_ref, g_ref, w_ref, bd_ref, qg_ref, kg_ref, cos_ref, sin_ref,
             proj_ref, hc_ref, q_ref, k_ref, v_ref):
        xv = x_ref[...]
        h = (xv * _rms(xv) * g_ref[...]).astype(BF16)
        proj = _dot(h, w_ref[...])
        proj_ref[...] = proj
        cosv = cos_ref[...]
        sinv = sin_ref[...]
        q = proj[:, C_Q:C_K]
        qn = q * lax.rsqrt(_dot_hi(q * q, bd_ref[...]) + EPS) * qg_ref[...]
        qr = (qn * cosv + _swap16(qn) * sinv) * ATT_SCALE
        for hh in range(N_HEADS):
            pair = qr[:, (hh // 2) * LANES:(hh // 2 + 1) * LANES]
            q_ref[hh] = _head_slab(pair, hh % 2 == 1).astype(BF16)
        k = proj[:, C_K:C_V]
        kn = k * lax.rsqrt(_dot_hi(k * k, bd_ref[0:LANES, 0:LANES]) + EPS) * kg_ref[...]
        kr = kn * cosv[:, 0:LANES] + _swap16(kn) * sinv[:, 0:LANES]
        vv = proj[:, C_V:C_GA]
        for hh in range(N_KV):
            k_ref[hh] = _with_ones(_head_slab(kr, hh == 1)).astype(BF16)
            v_ref[hh] = _with_ones(_head_slab(vv, hh == 1)).astype(BF16)
        hc_ref[...] = proj[:, C_A1:C_A2] * _sigmoid(proj[:, C_A2:C_GC])

    row = lambda w: pl.BlockSpec((t, w), lambda i: (i, 0))
    heads = lambda n: pl.BlockSpec((n, t, LANES), lambda i: (0, i, 0))
    return pl.pallas_call(
        body, name="fwd_in", grid=(s // t,),
        in_specs=[row(D_MODEL), _const_spec((1, D_MODEL)), _const_spec((D_MODEL, D_IN)), _const_spec((512, 512)),
                  _const_spec((1, 512)), _const_spec((1, LANES)), row(512), row(512)],
        out_specs=[row(D_IN), row(CONV_W), heads(N_HEADS), heads(N_KV), heads(N_KV)],
        out_shape=[jax.ShapeDtypeStruct((s, D_IN), F32), jax.ShapeDtypeStruct((s, CONV_W), F32),
                   jax.ShapeDtypeStruct((N_HEADS, s, LANES), BF16), jax.ShapeDtypeStruct((N_KV, s, LANES), BF16),
                   jax.ShapeDtypeStruct((N_KV, s, LANES), BF16)],
        compiler_params=_arb(),
    )(x, g_pre, w_in_bf, bd, qg, kg, cos, sin)


def _flash_fwd(q, k, v, tq, tk):
    s = q.shape[1]
    rows = GROUP * tq
    nk = s // tk
    assert nk % 2 == 0

    def body(q_ref, k_ref, v_ref, o_ref, qa_ref, m_scr, acc_scr, s0, s1, p0, p1, a0, a1):
        s_bufs, p_bufs, a_bufs = (s0, s1), (p0, p1), (a0, a1)
        qv = q_ref[...].reshape(rows, LANES)
        m_scr[...] = jnp.full((rows, 1), -jnp.inf, F32)
        acc_scr[...] = jnp.zeros((rows, LANES), F32)
        p1[...] = jnp.zeros((rows, tk), BF16)
        a1[...] = jnp.ones((rows, 1), F32)

        def chunk(ref, c):
            return ref[pl.ds(pl.multiple_of(c * tk, tk), tk), :]

        def scores(c, slot):
            s_bufs[slot][...] = _dot_nt(qv, chunk(k_ref, c))

        def softmax(slot):
            for h in range(GROUP):
                r = slice(h * tq, (h + 1) * tq)
                sc = s_bufs[slot][r, :]
                m_prev = m_scr[r, :]
                m_new = jnp.maximum(m_prev, jnp.max(sc, axis=1, keepdims=True))
                p_bufs[slot][r, :] = jnp.exp(sc - m_new).astype(BF16)
                a_bufs[slot][r, :] = jnp.exp(m_prev - m_new)
                m_scr[r, :] = m_new

        def weighted_values(c, slot):
            acc_scr[...] = a_bufs[slot][...] * acc_scr[...] + _dot(p_bufs[slot][...], chunk(v_ref, c))

        scores(0, 0)

        def trip(t, carry):
            c = 2 * t
            scores(c + 1, 1)
            softmax(0)
            weighted_values(jnp.maximum(c - 1, 0), 1)
            scores(jnp.minimum(c + 2, nk - 1), 0)
            softmax(1)
            weighted_values(c, 0)
            return carry

        lax.fori_loop(0, nk // 2, trip, 0)
        weighted_values(nk - 1, 1)

        acc = acc_scr[...]
        lane = _lane(acc.shape)
        l = jnp.sum(jnp.where(lane == HEAD_DIM, acc, 0.0), axis=1, keepdims=True)
        o_ref[...] = jnp.where(lane < HEAD_DIM, acc / l, 0.0).reshape(GROUP, tq, LANES)
        hi, mid, lo = _split3(-(m_scr[...] + jnp.log(l)))
        qa_ref[...] = _with_spare(qv.astype(F32), hi, mid, lo).astype(BF16).reshape(GROUP, tq, LANES)

    qspec = pl.BlockSpec((GROUP, tq, LANES), lambda j, i: (j, i, 0))
    kvspec = pl.BlockSpec((None, s, LANES), lambda j, i: (j, 0, 0))
    return pl.pallas_call(
        body, name="flash_fwd", grid=(N_KV, s // tq),
        in_specs=[qspec, kvspec, kvspec],
        out_specs=[qspec, qspec],
        out_shape=[jax.ShapeDtypeStruct((N_HEADS, s, LANES), F32), jax.ShapeDtypeStruct((N_HEADS, s, LANES), BF16)],
        scratch_shapes=[pltpu.VMEM((rows, 1), F32), pltpu.VMEM((rows, LANES), F32),
                        pltpu.VMEM((rows, tk), F32), pltpu.VMEM((rows, tk), F32),
                        pltpu.VMEM((rows, tk), BF16), pltpu.VMEM((rows, tk), BF16),
                        pltpu.VMEM((rows, 1), F32), pltpu.VMEM((rows, 1), F32)],
        compiler_params=_arb(2),
    )(q, k, v)


def _groups_fwd(proj_ref, o_ref, hext_ref, cw_ref, cp_ref, sp_ref, sgw_ref, sgb_ref, t):
    proj = proj_ref[...]
    r = {}
    r["att"] = _heads_to_cat(o_ref, N_HEADS)
    r["gate_a"], r["dgate_a"] = _silu_fwd_bwd(proj[:, C_GA:C_A1])
    r["att_g"] = r["att"] * r["gate_a"]
    c0 = jnp.zeros((t, CONV_W), F32) + cp_ref[0:1, :]
    for kk in range(CONV_K):
        c0 = c0 + cw_ref[kk:kk + 1, :] * hext_ref[kk + 1:kk + 1 + t, :]
    r["xh_c"], r["rs_c"] = _ln_hat(c0)
    r["c1"] = r["xh_c"] * cp_ref[1:2, :] + cp_ref[2:3, :]
    r["sg_c1"] = _sigmoid(r["c1"])
    r["c2"] = r["c1"] * r["sg_c1"]
    r["gate_c"], r["dgate_c"] = _silu_fwd_bwd(proj[:, C_GC:C_U])
    r["cnv_g"] = r["c2"] * r["gate_c"]
    r["gu"], r["dgu"] = _gelu_fwd_bwd(proj[:, C_U:C_VS])
    gv, r["dgv"] = _gelu_fwd_bwd(proj[:, C_VS:C_GS])
    r["xh_s"], r["rs_s"] = _ln_hat(gv)
    v1 = r["xh_s"] * sp_ref[0:1, :] + sp_ref[1:2, :]
    r["v1_bf"] = v1.astype(BF16)
    r["mixed"] = _sgu_mix(r["v1_bf"], sgw_ref, t // SG_CHUNK) + jnp.concatenate([sgb_ref[...]] * (t // SG_CHUNK), axis=0)
    r["um"] = r["gu"] * r["mixed"]
    r["gate_s"], r["dgate_s"] = _silu_fwd_bwd(proj[:, C_GS:D_IN])
    r["sgu_g"] = r["um"] * r["gate_s"]
    r["mc_bf"] = jnp.concatenate([r["att_g"], r["cnv_g"], r["sgu_g"]], axis=1).astype(BF16)
    return r


def _fwd_out(x, proj, o, hc, cw, cp, sp, sgw_bf, sgb, w_out_bf, g_post, t):
    s = x.shape[0]

    def body(x_ref, proj_ref, o_ref, hc_ref, hp_ref, hn_ref, cw_ref, cp_ref, sp_ref, sgw_ref, sgb_ref,
             w_ref, g_ref, mix_ref, xn_ref, hext_ref):
        i = pl.program_id(0)
        _conv_window(hext_ref, hp_ref, hc_ref[...], hn_ref, i == 0, i == pl.num_programs(0) - 1, t)
        r = _groups_fwd(proj_ref, o_ref, hext_ref, cw_ref, cp_ref, sp_ref, sgw_ref, sgb_ref, t)
        mix = _dot(r["mc_bf"], w_ref[...])
        mix_ref[...] = mix
        xn_ref[...] = x_ref[...] + mix * _rms(mix) * g_ref[...]

    row = lambda w: pl.BlockSpec((t, w), lambda i: (i, 0))
    hprev, hnext = _halo_specs(t, s, CONV_W)
    return pl.pallas_call(
        body, name="fwd_out", grid=(s // t,),
        in_specs=[row(D_MODEL), row(D_IN), pl.BlockSpec((N_HEADS, t, LANES), lambda i: (0, i, 0)), row(CONV_W),
                  hprev, hnext, _const_spec((32, CONV_W)), _const_spec((8, CONV_W)), _const_spec((8, SG_W)),
                  _const_spec((4, SG_CHUNK, SG_CHUNK)), _const_spec((SG_CHUNK, SG_W)),
                  _const_spec((D_MODEL, D_MODEL)), _const_spec((1, D_MODEL))],
        out_specs=[row(D_MODEL), row(D_MODEL)],
        out_shape=[jax.ShapeDtypeStruct((s, D_MODEL), F32), jax.ShapeDtypeStruct((s, D_MODEL), F32)],
        scratch_shapes=[pltpu.VMEM((t + 2 * HALO, CONV_W), F32)],
        compiler_params=_arb(),
    )(x, proj, o, hc, hc, hc, cw, cp, sp, sgw_bf, sgb, w_out_bf, g_post)


def _loss_grad(y, target, t):
    s = y.shape[0]

    def body(y_ref, t_ref, dy_ref, sq_ref):
        @pl.when(pl.program_id(0) == 0)
        def _():
            sq_ref[...] = jnp.zeros_like(sq_ref)

        err = y_ref[...] - t_ref[...]
        dy_ref[...] = err * (1.0 / D_MODEL)
        sq_ref[...] += jnp.sum(err * err, axis=0, keepdims=True)

    row = pl.BlockSpec((t, D_MODEL), lambda i: (i, 0))
    return pl.pallas_call(
        body, name="loss_grad", grid=(s // t,),
        in_specs=[row, row], out_specs=[row, _const_spec((1, D_MODEL))],
        out_shape=[jax.ShapeDtypeStruct((s, D_MODEL), F32), jax.ShapeDtypeStruct((1, D_MODEL), F32)],
        compiler_params=_arb(),
    )(y, target)


def _bwd_out(dy, mix, proj, o, hc, cw, cp, sp, sgw_bf, sgwt_bf, sgb, w_out_t_bf, g_post, t):
    s = dy.shape[0]
    n_chunks = t // SG_CHUNK

    def body(dy_ref, mix_ref, proj_ref, o_ref, hc_ref, hp_ref, hn_ref, cw_ref, cp_ref, sp_ref, sgw_ref, sgwt_ref,
             sgb_ref, wt_ref, g_ref,
             do_ref, dgs_ref, dc0_ref, gwo_ref, gpost_ref, gcw_ref, gvec_ref, gsgw_ref, gsgb_ref, hext_ref):
        i = pl.program_id(0)

        @pl.when(i == 0)
        def _():
            for ref in (gwo_ref, gpost_ref, gcw_ref, gvec_ref, gsgw_ref, gsgb_ref):
                ref[...] = jnp.zeros_like(ref)

        _conv_window(hext_ref, hp_ref, hc_ref[...], hn_ref, i == 0, i == pl.num_programs(0) - 1, t)
        r = _groups_fwd(proj_ref, o_ref, hext_ref, cw_ref, cp_ref, sp_ref, sgw_ref, sgb_ref, t)

        dyv = dy_ref[...]
        mix_v = mix_ref[...]
        rr = _rms(mix_v)
        gd = dyv * g_ref[...]
        dmix = rr * gd - mix_v * (rr * rr * rr * jnp.mean(gd * mix_v, axis=-1, keepdims=True))
        gpost_ref[...] += jnp.sum(dyv * mix_v * rr, axis=0, keepdims=True)
        dmix_bf = dmix.astype(BF16)
        gwo_ref[...] += _dot_tn(r["mc_bf"], dmix_bf)
        dmc = _dot(dmix_bf, wt_ref[...])

        d_att = dmc[:, 0:512]
        dg_att = d_att * r["att"] * r["dgate_a"]
        d_o = d_att * r["gate_a"]
        prod = d_o * r["att"]
        for p in range(N_HEADS // 2):
            sl = slice(p * LANES, (p + 1) * LANES)
            pr = prod[:, sl]
            tot = jnp.sum(pr, axis=1, keepdims=True)
            ev = jnp.sum(jnp.where(_lane(pr.shape) < HEAD_DIM, pr, 0.0), axis=1, keepdims=True)
            for odd, delta in ((False, ev), (True, tot - ev)):
                hi, mid, lo = _split3(-delta)
                do_ref[2 * p + int(odd)] = _with_spare(_head_slab(d_o[:, sl], odd), hi, mid, lo).astype(BF16)

        dcv = dmc[:, 512:768]
        dg_conv = dcv * r["c2"] * r["dgate_c"]
        dc1 = dcv * r["gate_c"] * (r["sg_c1"] * (1.0 + r["c1"] * (1.0 - r["sg_c1"])))
        dc0 = _ln_bwd(dc1 * cp_ref[1:2, :], r["xh_c"], r["rs_c"])
        dc0_ref[...] = dc0
        for kk in range(CONV_K):
            gcw_ref[kk:kk + 1, :] += jnp.sum(dc0 * hext_ref[kk + 1:kk + 1 + t, :], axis=0, keepdims=True)

        dsg = dmc[:, 768:1024]
        dg_sg = dsg * r["um"] * r["dgate_s"]
        du = dsg * r["mixed"] * r["gate_s"] * r["dgu"]
        dmx = dsg * r["gu"] * r["gate_s"]
        dmx_bf = dmx.astype(BF16)
        sgb_sum = dmx[0:SG_CHUNK, :]
        for n in range(1, n_chunks):
            sgb_sum = sgb_sum + dmx[n * SG_CHUNK:(n + 1) * SG_CHUNK, :]
        gsgb_ref[...] += sgb_sum
        dv1_rows = []
        for n in range(n_chunks):
            pairs = []
            for p in range(SG_W // LANES):
                rs_ = slice(n * SG_CHUNK, (n + 1) * SG_CHUNK)
                ls_ = slice(p * LANES, (p + 1) * LANES)
                dm = dmx_bf[rs_, ls_]
                xp = r["v1_bf"][rs_, ls_]
                low = _lane(dm.shape) < HEAD_DIM
                zero = jnp.zeros_like(dm)
                gsgw_ref[2 * p] += _dot_nt(jnp.where(low, dm, zero), xp)
                gsgw_ref[2 * p + 1] += _dot_nt(jnp.where(low, zero, dm), xp)
                pairs.append(jnp.where(low, _dot(sgwt_ref[2 * p], dm), _dot(sgwt_ref[2 * p + 1], dm)))
            dv1_rows.append(jnp.concatenate(pairs, axis=1))
        dv1 = dv1_rows[0] if n_chunks == 1 else jnp.concatenate(dv1_rows, axis=0)
        dvs = _ln_bwd(dv1 * sp_ref[0:1, :], r["xh_s"], r["rs_s"]) * r["dgv"]

        zrow = jnp.zeros((1, CONV_W), F32)
        gvec_ref[...] += jnp.concatenate([
            jnp.sum(dc0, axis=0, keepdims=True),
            jnp.sum(dc1 * r["xh_c"], axis=0, keepdims=True),
            jnp.sum(dc1, axis=0, keepdims=True),
            jnp.sum(dv1 * r["xh_s"], axis=0, keepdims=True),
            jnp.sum(dv1, axis=0, keepdims=True),
            zrow, zrow, zrow], axis=0)
        dgs_ref[...] = jnp.concatenate([dg_att, dg_conv, du, dvs, dg_sg], axis=1)

    row = lambda w: pl.BlockSpec((t, w), lambda i: (i, 0))
    heads = pl.BlockSpec((N_HEADS, t, LANES), lambda i: (0, i, 0))
    hprev, hnext = _halo_specs(t, s, CONV_W)
    return pl.pallas_call(
        body, name="bwd_out", grid=(s // t,),
        in_specs=[row(D_MODEL), row(D_MODEL), row(D_IN), heads, row(CONV_W), hprev, hnext,
                  _const_spec((32, CONV_W)), _const_spec((8, CONV_W)), _const_spec((8, SG_W)),
                  _const_spec((4, SG_CHUNK, SG_CHUNK)), _const_spec((4, SG_CHUNK, SG_CHUNK)),
                  _const_spec((SG_CHUNK, SG_W)), _const_spec((D_MODEL, D_MODEL)), _const_spec((1, D_MODEL))],
        out_specs=[heads, row(1536), row(CONV_W), _const_spec((D_MODEL, D_MODEL)), _const_spec((1, D_MODEL)),
                   _const_spec((32, CONV_W)), _const_spec((8, CONV_W)), _const_spec((4, SG_CHUNK, SG_CHUNK)),
                   _const_spec((SG_CHUNK, SG_W))],
        out_shape=[jax.ShapeDtypeStruct((N_HEADS, s, LANES), BF16), jax.ShapeDtypeStruct((s, 1536), F32),
                   jax.ShapeDtypeStruct((s, CONV_W), F32), jax.ShapeDtypeStruct((D_MODEL, D_MODEL), F32),
                   jax.ShapeDtypeStruct((1, D_MODEL), F32), jax.ShapeDtypeStruct((32, CONV_W), F32),
                   jax.ShapeDtypeStruct((8, CONV_W), F32), jax.ShapeDtypeStruct((4, SG_CHUNK, SG_CHUNK), F32),
                   jax.ShapeDtypeStruct((SG_CHUNK, SG_W), F32)],
        scratch_shapes=[pltpu.VMEM((t + 2 * HALO, CONV_W), F32)],
        compiler_params=_arb(),
    )(dy, mix, proj, o, hc, hc, hc, cw, cp, sp, sgw_bf, sgwt_bf, sgb, w_out_t_bf, g_post)


def _flash_bwd_kv(qa, doa, k, v, tk, tqc):
    s = qa.shape[1]
    nq = s // tqc

    def body(qa_ref, do_ref, k_ref, v_ref, dk_ref, dv_ref):
        @pl.when(pl.program_id(2) == 0)
        def _():
            dk_ref[...] = jnp.zeros_like(dk_ref)
            dv_ref[...] = jnp.zeros_like(dv_ref)

        kc = k_ref[...]
        vc = v_ref[...]

        def step(qi, carry):
            off = pl.multiple_of(qi * tqc, tqc)
            qc = qa_ref[pl.ds(off, tqc), :]
            dc = do_ref[pl.ds(off, tqc), :]
            p_t = jnp.exp(_dot_nt(kc, qc))
            ds_t = p_t * _dot_nt(vc, dc)
            dv_ref[...] += _dot(p_t.astype(BF16), dc)
            dk_ref[...] += _dot(ds_t.astype(BF16), qc)
            return carry

        lax.fori_loop(0, nq, step, 0)

    hspec = pl.BlockSpec((None, s, LANES), lambda j, kk, h: (GROUP * j + h, 0, 0))
    kvspec = pl.BlockSpec((None, tk, LANES), lambda j, kk, h: (j, kk, 0))
    return pl.pallas_call(
        body, name="flash_bwd_kv", grid=(N_KV, s // tk, GROUP),
        in_specs=[hspec, hspec, kvspec, kvspec],
        out_specs=[kvspec, kvspec],
        out_shape=[jax.ShapeDtypeStruct((N_KV, s, LANES), F32), jax.ShapeDtypeStruct((N_KV, s, LANES), F32)],
        compiler_params=_arb(3),
    )(qa, doa, k, v)


def _flash_bwd_q(qa, doa, k, v, tq, tk):
    s = qa.shape[1]
    rows = GROUP * tq
    nk = s // tk

    def body(qa_ref, do_ref, k_ref, v_ref, dq_ref, acc_scr):
        qv = qa_ref[...].reshape(rows, LANES)
        dv_ = do_ref[...].reshape(rows, LANES)
        acc_scr[...] = jnp.zeros((rows, LANES), F32)

        def step(kk, carry):
            off = pl.multiple_of(kk * tk, tk)
            kc = k_ref[pl.ds(off, tk), :]
            vc = v_ref[pl.ds(off, tk), :]
            ds = jnp.exp(_dot_nt(qv, kc)) * _dot_nt(dv_, vc)
            acc_scr[...] += _dot(ds.astype(BF16), kc)
            return carry

        lax.fori_loop(0, nk, step, 0)
        dq_ref[...] = acc_scr[...].reshape(GROUP, tq, LANES)

    qspec = pl.BlockSpec((GROUP, tq, LANES), lambda j, i: (j, i, 0))
    kvspec = pl.BlockSpec((None, s, LANES), lambda j, i: (j, 0, 0))
    return pl.pallas_call(
        body, name="flash_bwd_q", grid=(N_KV, s // tq),
        in_specs=[qspec, qspec, kvspec, kvspec],
        out_specs=qspec,
        out_shape=jax.ShapeDtypeStruct((N_HEADS, s, LANES), F32),
        scratch_shapes=[pltpu.VMEM((rows, LANES), F32)],
        compiler_params=_arb(2),
    )(qa, doa, k, v)


def _bwd_in(dy, x, proj, dq, dk, dv, dgs, dc0, cw, g_pre, w_in_t_bf, bd, qg, kg, cos, sin, t):
    s = x.shape[0]

    def body(dy_ref, x_ref, proj_ref, dq_ref, dk_ref, dv_ref, dgs_ref, dc_ref, dcp_ref, dcn_ref, cw_ref, g_ref,
             wt_ref, bd_ref, qg_ref, kg_ref, cos_ref, sin_ref,
             dx_ref, dproj_ref, h_ref, gpre_ref, gq_ref, gk_ref, dext_ref):
        i = pl.program_id(0)

        @pl.when(i == 0)
        def _():
            for ref in (gpre_ref, gq_ref, gk_ref):
                ref[...] = jnp.zeros_like(ref)

        proj = proj_ref[...]
        cosv = cos_ref[...]
        sinv = sin_ref[...]

        def head_norm_bwd(dr, z, bdm, g, cs, sn, gacc_ref):
            dn = dr * cs + _swap16(dr * sn)
            rr = lax.rsqrt(_dot_hi(z * z, bdm) + EPS)
            gdn = dn * g
            gacc_ref[...] += jnp.sum(dn * z * rr, axis=0, keepdims=True)
            return rr * gdn - z * (rr * rr * rr * _dot_hi(gdn * z, bdm))

        dq_cat = _heads_to_cat(dq_ref, N_HEADS) * ATT_SCALE
        dzq = head_norm_bwd(dq_cat, proj[:, C_Q:C_K], bd_ref[...], qg_ref[...], cosv, sinv, gq_ref)
        dk_cat = _heads_to_cat(dk_ref, N_KV)
        dzk = head_norm_bwd(dk_cat, proj[:, C_K:C_V], bd_ref[0:LANES, 0:LANES], kg_ref[...],
                            cosv[:, 0:LANES], sinv[:, 0:LANES], gk_ref)
        dv_cat = _heads_to_cat(dv_ref, N_KV)

        _conv_window(dext_ref, dcp_ref, dc_ref[...], dcn_ref, i == 0, i == pl.num_programs(0) - 1, t)
        dhc = jnp.zeros((t, CONV_W), F32)
        for kk in range(CONV_K):
            dhc = dhc + cw_ref[kk:kk + 1, :] * dext_ref[CONV_K - kk:CONV_K - kk + t, :]
        sg = _sigmoid(proj[:, C_A2:C_GC])
        da1 = dhc * sg
        da2 = dhc * proj[:, C_A1:C_A2] * sg * (1.0 - sg)

        dgs = dgs_ref[...]
        dproj_bf = jnp.concatenate([dzq, dzk, dv_cat, dgs[:, 0:512], da1, da2, dgs[:, 512:1536]], axis=1).astype(BF16)
        dproj_ref[...] = dproj_bf
        dh = _dot(dproj_bf, wt_ref[...])

        xv = x_ref[...]
        rr = _rms(xv)
        gv = g_ref[...]
        h_ref[...] = (xv * rr * gv).astype(BF16)
        gdh = dh * gv
        gpre_ref[...] += jnp.sum(dh * xv * rr, axis=0, keepdims=True)
        dx_ref[...] = dy_ref[...] + rr * gdh - xv * (rr * rr * rr * jnp.mean(gdh * xv, axis=-1, keepdims=True))

    row = lambda w: pl.BlockSpec((t, w), lambda i: (i, 0))
    heads = lambda n: pl.BlockSpec((n, t, LANES), lambda i: (0, i, 0))
    hprev, hnext = _halo_specs(t, s, CONV_W)
    return pl.pallas_call(
        body, name="bwd_in", grid=(s // t,),
        in_specs=[row(D_MODEL), row(D_MODEL), row(D_IN), heads(N_HEADS), heads(N_KV), heads(N_KV), row(1536),
                  row(CONV_W), hprev, hnext, _const_spec((32, CONV_W)), _const_spec((1, D_MODEL)),
                  _const_spec((D_IN, D_MODEL)), _const_spec((512, 512)), _const_spec((1, 512)),
                  _const_spec((1, LANES)), row(512), row(512)],
        out_specs=[row(D_MODEL), row(D_IN), row(D_MODEL), _const_spec((1, D_MODEL)), _const_spec((1, 512)),
                   _const_spec((1, LANES))],
        out_shape=[jax.ShapeDtypeStruct((s, D_MODEL), F32), jax.ShapeDtypeStruct((s, D_IN), BF16),
                   jax.ShapeDtypeStruct((s, D_MODEL), BF16), jax.ShapeDtypeStruct((1, D_MODEL), F32),
                   jax.ShapeDtypeStruct((1, 512), F32), jax.ShapeDtypeStruct((1, LANES), F32)],
        scratch_shapes=[pltpu.VMEM((t + 2 * HALO, CONV_W), F32)],
        compiler_params=_arb(),
    )(dy, x, proj, dq, dk, dv, dgs, dc0, dc0, dc0, cw, g_pre, w_in_t_bf, bd, qg, kg, cos, sin)


def _grad_w_in(h_bf, dproj_bf, t):
    s = h_bf.shape[0]
    half = D_IN // 2

    def body(h_ref, d_ref, g_ref):
        @pl.when(pl.program_id(1) == 0)
        def _():
            g_ref[...] = jnp.zeros_like(g_ref)

        g_ref[...] += _dot_tn(h_ref[...], d_ref[...])

    return pl.pallas_call(
        body, name="grad_w_in", grid=(2, s // t),
        in_specs=[pl.BlockSpec((t, D_MODEL), lambda j, i: (i, 0)), pl.BlockSpec((t, half), lambda j, i: (i, j))],
        out_specs=pl.BlockSpec((D_MODEL, half), lambda j, i: (0, j)),
        out_shape=jax.ShapeDtypeStruct((D_MODEL, D_IN), F32),
        compiler_params=_arb(2),
    )(h_bf, dproj_bf)


def _place():
    x, y, c = lax.axis_index("x"), lax.axis_index("y"), lax.axis_index("c")
    chips = [(1 - x, y), (x, 1 - y), (1 - x, 1 - y)]
    return x, y, c, chips


def _any_specs(n):
    return [pl.BlockSpec(memory_space=pl.ANY)] * n


def _gather_weights(w_in_bf, w_out_bf, cdw):
    arrs = (w_in_bf, w_out_bf, cdw)
    n = len(arrs)

    def body(*refs):
        ins, outs = refs[:n], refs[n:2 * n]
        send_sems, recv_sems, local_sems = refs[2 * n:]
        x, y, c, chips = _place()
        mine = 2 * x + y
        local = [pltpu.make_async_copy(ins[a], outs[a].at[mine], local_sems.at[a]) for a in range(n)]
        for cp in local:
            cp.start()

        def copy(j, a, slot, to):
            return pltpu.make_async_remote_copy(src_ref=ins[a], dst_ref=outs[a].at[slot], send_sem=send_sems.at[n * j + a],
                                                recv_sem=recv_sems.at[n * j + a], device_id=to, device_id_type=MESH)

        sends = [copy(j, a, mine, (px, py, c)) for j, (px, py) in enumerate(chips) for a in range(n)]
        for cp in sends:
            cp.start()
        for j, (px, py) in enumerate(chips):
            for a in range(n):
                copy(j, a, 2 * px + py, (px, py, c)).wait_recv()
        for cp in sends:
            cp.wait_send()
        for cp in local:
            cp.wait()

    return pl.pallas_call(
        body, name="gather_weights",
        in_specs=_any_specs(n), out_specs=_any_specs(n),
        out_shape=[jax.ShapeDtypeStruct((N_CHIPS,) + a.shape, a.dtype) for a in arrs],
        scratch_shapes=[pltpu.SemaphoreType.DMA((3 * n,)), pltpu.SemaphoreType.DMA((3 * n,)), pltpu.SemaphoreType.DMA((n,))],
    )(*arrs)


def _scatter_grads(gin_pieces, gw_out):
    depth = gw_out.shape[0]
    rows = gw_out.shape[1] // N_CHIPS

    def body(gin_ref, gout_ref, rin_ref, rout_ref, send_sems, recv_sems, local_sems):
        x, y, c, chips = _place()
        mine = 2 * x + y

        def out_rows(chip):
            return gout_ref.at[:, pl.ds(pl.multiple_of(chip * rows, rows), rows), :]

        local = [pltpu.make_async_copy(gin_ref.at[mine], rin_ref.at[mine], local_sems.at[0]),
                 pltpu.make_async_copy(out_rows(mine), rout_ref.at[mine], local_sems.at[1])]
        for cp in local:
            cp.start()

        def copies(j, shard, slot, to):
            kw = dict(device_id=to, device_id_type=MESH)
            return [pltpu.make_async_remote_copy(src_ref=gin_ref.at[shard], dst_ref=rin_ref.at[slot],
                                                 send_sem=send_sems.at[2 * j], recv_sem=recv_sems.at[2 * j], **kw),
                    pltpu.make_async_remote_copy(src_ref=out_rows(shard), dst_ref=rout_ref.at[slot],
                                                 send_sem=send_sems.at[2 * j + 1], recv_sem=recv_sems.at[2 * j + 1], **kw)]

        sends = [cp for j, (px, py) in enumerate(chips) for cp in copies(j, 2 * px + py, mine, (px, py, c))]
        for cp in sends:
            cp.start()
        for j, (px, py) in enumerate(chips):
            for cp in copies(j, mine, 2 * px + py, (px, py, c)):
                cp.wait_recv()
        for cp in sends:
            cp.wait_send()
        for cp in local:
            cp.wait()

    return pl.pallas_call(
        body, name="scatter_grads",
        in_specs=_any_specs(2), out_specs=_any_specs(2),
        out_shape=[jax.ShapeDtypeStruct(gin_pieces.shape, F32),
                   jax.ShapeDtypeStruct((N_CHIPS, depth, rows, gw_out.shape[2]), F32)],
        scratch_shapes=[pltpu.SemaphoreType.DMA((6,)), pltpu.SemaphoreType.DMA((6,)), pltpu.SemaphoreType.DMA((2,))],
    )(gin_pieces, gw_out)


def _sum_chips(parts, rb, name):
    _, depth, r, cdim = parts.shape

    def body(p_ref, o_ref):
        o_ref[...] = ((p_ref[0] + p_ref[1]) + p_ref[2]) + p_ref[3]

    return pl.pallas_call(
        body, name=name, grid=(depth, r // rb),
        in_specs=[pl.BlockSpec((N_CHIPS, None, rb, cdim), lambda l, i: (0, l, i, 0))],
        out_specs=pl.BlockSpec((None, rb, cdim), lambda l, i: (l, i, 0)),
        out_shape=jax.ShapeDtypeStruct((depth, r, cdim), F32),
        compiler_params=_arb(2),
    )(parts)


def _swap_with_sibling(a, b):
    arrs = (a, b)
    n = len(arrs)

    def body(*refs):
        ins, outs = refs[:n], refs[n:2 * n]
        send_sems, recv_sems = refs[2 * n:]
        x, y, c, _ = _place()
        cps = [pltpu.make_async_remote_copy(src_ref=ins[k], dst_ref=outs[k], send_sem=send_sems.at[k],
                                            recv_sem=recv_sems.at[k], device_id=(x, y, 1 - c), device_id_type=MESH)
               for k in range(n)]
        for cp in cps:
            cp.start()
        for cp in cps:
            cp.wait()

    return pl.pallas_call(
        body, name="swap_with_sibling",
        in_specs=_any_specs(n), out_specs=_any_specs(n),
        out_shape=[jax.ShapeDtypeStruct(v.shape, v.dtype) for v in arrs],
        scratch_shapes=[pltpu.SemaphoreType.DMA((n,)), pltpu.SemaphoreType.DMA((n,))],
    )(*arrs)


def _allreduce_small(slab):
    m, n = slab.shape

    def body(x_ref, out_ref, gath, send_sems, recv_sems, local_sem):
        x, y, c, chips = _place()
        me, sibling = (x, y, c), (x, y, 1 - c)

        def rows(px, py, pc):
            return gath.at[pl.ds(pl.multiple_of((4 * px + 2 * py + pc) * m, 8), m), :]

        def copy(k, block, to, src=None):
            return pltpu.make_async_remote_copy(src_ref=rows(*block) if src is None else src, dst_ref=rows(*block),
                                                send_sem=send_sems.at[k], recv_sem=recv_sems.at[k],
                                                device_id=to, device_id_type=MESH)

        mine = pltpu.make_async_copy(x_ref, rows(*me), local_sem)
        mine.start()
        first = [copy(0, me, sibling, src=x_ref)]
        first += [copy(1 + j, me, (*chip, c), src=x_ref) for j, chip in enumerate(chips)]
        for cp in first:
            cp.start()
        passed = [copy(4 + j, (*chip, c), sibling) for j, chip in enumerate(chips)]
        for j, chip in enumerate(chips):
            copy(1 + j, (*chip, c), me).wait_recv()
            passed[j].start()
        copy(0, sibling, me).wait_recv()
        for j, chip in enumerate(chips):
            copy(4 + j, (*chip, 1 - c), me).wait_recv()
        for cp in first + passed:
            cp.wait_send()
        mine.wait()
        total = gath[0:m, :]
        for d in range(1, N_DEV):
            total = total + gath[d * m:(d + 1) * m, :]
        out_ref[...] = total

    return pl.pallas_call(
        body, name="allreduce_small",
        in_specs=[pl.BlockSpec(memory_space=pltpu.VMEM)],
        out_specs=pl.BlockSpec(memory_space=pltpu.VMEM),
        out_shape=jax.ShapeDtypeStruct((m, n), F32),
        scratch_shapes=[pltpu.VMEM((N_DEV * m, n), F32), pltpu.SemaphoreType.DMA((7,)), pltpu.SemaphoreType.DMA((7,)),
                        pltpu.SemaphoreType.DMA],
    )(slab)


def _adamw(w, ga, gb, m, v, rb, name):
    depth, r, cdim = w.shape

    def body(w_ref, ga_ref, gb_ref, m_ref, v_ref, g_out, d_out, m_out, v_out):
        g = ga_ref[...] + gb_ref[...]
        m2 = ADAM_B1 * m_ref[...] + (1.0 - ADAM_B1) * g
        v2 = ADAM_B2 * v_ref[...] + (1.0 - ADAM_B2) * (g * g)
        m_hat = m2 / (1.0 - ADAM_B1 ** ADAM_STEP)
        v_hat = v2 / (1.0 - ADAM_B2 ** ADAM_STEP)
        g_out[...] = g
        d_out[...] = -ADAM_LR * (m_hat / (jnp.sqrt(v_hat) + ADAM_EPS) + ADAM_WD * w_ref[...])
        m_out[...] = m2
        v_out[...] = v2

    spec = pl.BlockSpec((None, rb, cdim), lambda l, i: (l, i, 0))
    shp = jax.ShapeDtypeStruct((depth, r, cdim), F32)
    return pl.pallas_call(
        body, name=name, grid=(depth, r // rb),
        in_specs=[spec] * 5, out_specs=[spec] * 4, out_shape=[shp] * 4,
        compiler_params=_arb(2),
    )(w, ga, gb, m, v)


def _rope_tables(s):
    t = jnp.arange(s, dtype=jnp.int32)
    row = (t // GRID_W).astype(F32)
    col = (t % GRID_W).astype(F32)
    half = HEAD_DIM // 4
    inv_freq = ROPE_THETA ** (-jnp.arange(half, dtype=F32) / half)
    ar = row[:, None] * inv_freq[None, :]
    ac = col[:, None] * inv_freq[None, :]
    cos = jnp.concatenate([jnp.cos(ar), jnp.cos(ar), jnp.cos(ac), jnp.cos(ac)], axis=1)
    sin = jnp.concatenate([-jnp.sin(ar), jnp.sin(ar), -jnp.sin(ac), jnp.sin(ac)], axis=1)
    return jnp.tile(cos, (1, N_HEADS)), jnp.tile(sin, (1, N_HEADS))


def _pad_rows(a, rows):
    return jnp.concatenate([a, jnp.zeros((rows - a.shape[0],) + a.shape[1:], a.dtype)], axis=0)


_SMALL = ("pre_norm", "post_norm", "q_norm", "k_norm", "conv_dw_b", "conv_ln_g", "conv_ln_b", "sg_ln_g", "sg_ln_b",
          "sg_w", "sg_b")


def _pack(parts):
    flat = jnp.concatenate([p.reshape(-1, LANES) for p in parts], axis=0)
    return _pad_rows(flat, -(-flat.shape[0] // 8) * 8)


def _unpack(slab, shapes):
    out, r = [], 0
    for shp in shapes:
        n = 1
        for d in shp:
            n *= d
        out.append(slab[r:r + n // LANES].reshape(shp))
        r += n // LANES
    return out


def kernel(x, pre_norm, post_norm, w_in, w_out, q_norm, k_norm, conv_dw, conv_dw_b, conv_ln_g, conv_ln_b, sg_ln_g, sg_ln_b, sg_w, sg_b, loss_target, m_pre_norm, m_post_norm, m_w_in, m_w_out, m_q_norm, m_k_norm, m_conv_dw, m_conv_dw_b, m_conv_ln_g, m_conv_ln_b, m_sg_ln_g, m_sg_ln_b, m_sg_w, m_sg_b, v_pre_norm, v_post_norm, v_w_in, v_w_out, v_q_norm, v_k_norm, v_conv_dw, v_conv_dw_b, v_conv_ln_g, v_conv_ln_b, v_sg_ln_g, v_sg_ln_b, v_sg_w, v_sg_b):
    depth = w_in.shape[0]
    s = x.shape[1]
    assert x.shape[0] == 1 and s % SG_CHUNK == 0 and x.shape[2] == D_MODEL
    t = min(256, s)
    tq = min(256, s)
    tk = min(512, s // 2)
    shard_cols = w_in.shape[2]
    chip = 2 * lax.axis_index("x") + lax.axis_index("y")

    gin, gout, gcdw = _gather_weights(w_in.astype(BF16), w_out.astype(BF16), conv_dw)
    w_in_bf = jnp.concatenate([gin[j] for j in range(N_CHIPS)], axis=2)
    w_out_bf = jnp.concatenate([gout[j] for j in range(N_CHIPS)], axis=1)
    cdw_full = jnp.concatenate([gcdw[j] for j in range(N_CHIPS)], axis=2)
    w_in_t_bf = jnp.swapaxes(w_in_bf, 1, 2)
    w_out_t_bf = jnp.swapaxes(w_out_bf, 1, 2)
    sgw_bf = sg_w.astype(BF16)
    sgwt_bf = jnp.swapaxes(sg_w, 2, 3).astype(BF16)

    cos, sin = _rope_tables(s)
    bd = jnp.kron(jnp.eye(N_HEADS, dtype=F32), jnp.full((HEAD_DIM, HEAD_DIM), 1.0 / HEAD_DIM, F32))

    def layer_consts(l):
        cw = _pad_rows(cdw_full[l], 32)
        cp = _pad_rows(jnp.stack([conv_dw_b[l], conv_ln_g[l], conv_ln_b[l]]), 8)
        sp = _pad_rows(jnp.stack([sg_ln_g[l], sg_ln_b[l]]), 8)
        sgb = jnp.repeat(sg_b[l].T, HEAD_DIM, axis=1)
        qg = jnp.tile(q_norm[l], N_HEADS)[None, :]
        kg = jnp.tile(k_norm[l], N_KV)[None, :]
        return cw, cp, sp, sgb, qg, kg

    xs = [x[0]]
    saved = []
    for l in range(depth):
        cw, cp, sp, sgb, qg, kg = layer_consts(l)
        proj, hc, q, k, v = _fwd_in(xs[l], pre_norm[l][None, :], w_in_bf[l], bd, qg, kg, cos, sin, t)
        o, qa = _flash_fwd(q, k, v, tq, tk)
        mix, xn = _fwd_out(xs[l], proj, o, hc, cw, cp, sp, sgw_bf[l], sgb, w_out_bf[l], post_norm[l][None, :], t)
        saved.append((proj, hc, qa, k, v, o, mix))
        xs.append(xn)

    dy, sq = _loss_grad(xs[depth], loss_target[0], t)
    loss = lax.psum(0.5 * jnp.sum(sq) / D_MODEL, ("x", "y", "c"))

    g_w_in, g_w_out, g_small = [], [], {n: [] for n in _SMALL + ("conv_dw",)}
    for l in reversed(range(depth)):
        cw, cp, sp, sgb, qg, kg = layer_consts(l)
        proj, hc, qa, k, v, o, mix = saved[l]
        doa, dgs, dc0, gwo, gpost, gcw, gvec, gsgw, gsgb = _bwd_out(
            dy, mix, proj, o, hc, cw, cp, sp, sgw_bf[l], sgwt_bf[l], sgb, w_out_t_bf[l], post_norm[l][None, :], t)
        dk, dv = _flash_bwd_kv(qa, doa, k, v, tk, min(512, s))
        dq = _flash_bwd_q(qa, doa, k, v, tq, tk)
        dy, dproj_bf, h_bf, gpre, gq, gk = _bwd_in(dy, xs[l], proj, dq, dk, dv, dgs, dc0, cw, pre_norm[l][None, :],
                                                  w_in_t_bf[l], bd, qg, kg, cos, sin, t)
        g_w_in.append(_grad_w_in(h_bf, dproj_bf, min(512, s)))
        g_w_out.append(gwo)
        g_small["pre_norm"].append(gpre[0])
        g_small["post_norm"].append(gpost[0])
        g_small["q_norm"].append(gq[0].reshape(N_HEADS, HEAD_DIM).sum(0))
        g_small["k_norm"].append(gk[0].reshape(N_KV, HEAD_DIM).sum(0))
        g_small["conv_dw"].append(gcw[:CONV_K])
        g_small["conv_dw_b"].append(gvec[0])
        g_small["conv_ln_g"].append(gvec[1])
        g_small["conv_ln_b"].append(gvec[2])
        g_small["sg_ln_g"].append(gvec[3])
        g_small["sg_ln_b"].append(gvec[4])
        g_small["sg_w"].append(gsgw)
        g_small["sg_b"].append(gsgb.reshape(SG_CHUNK, SG_W // HEAD_DIM, HEAD_DIM).sum(-1).T)
    grad_x = dy[None]
    g_w_in = jnp.stack(g_w_in[::-1])
    g_w_out = jnp.stack(g_w_out[::-1])
    g_small = {n: jnp.stack(vals[::-1]) for n, vals in g_small.items()}

    gin_pieces = jnp.stack([g_w_in[:, :, j * shard_cols:(j + 1) * shard_cols] for j in range(N_CHIPS)])
    rin, rout = _scatter_grads(gin_pieces, g_w_out)
    s_in = _sum_chips(rin, 256, "sum_chips_w_in")
    s_out = _sum_chips(rout, 256, "sum_chips_w_out")
    t_in, t_out = _swap_with_sibling(s_in, s_out)
    grad_w_in, delta_w_in, new_m_w_in, new_v_w_in = _adamw(w_in, s_in, t_in, m_w_in, v_w_in, 256, "adamw_w_in")
    grad_w_out, delta_w_out, new_m_w_out, new_v_w_out = _adamw(w_out, s_out, t_out, m_w_out, v_w_out, 256, "adamw_w_out")

    small_w = dict(pre_norm=pre_norm, post_norm=post_norm, q_norm=q_norm, k_norm=k_norm, conv_dw_b=conv_dw_b,
                   conv_ln_g=conv_ln_g, conv_ln_b=conv_ln_b, sg_ln_g=sg_ln_g, sg_ln_b=sg_ln_b, sg_w=sg_w, sg_b=sg_b)
    small_m = dict(pre_norm=m_pre_norm, post_norm=m_post_norm, q_norm=m_q_norm, k_norm=m_k_norm, conv_dw_b=m_conv_dw_b,
                   conv_ln_g=m_conv_ln_g, conv_ln_b=m_conv_ln_b, sg_ln_g=m_sg_ln_g, sg_ln_b=m_sg_ln_b, sg_w=m_sg_w,
                   sg_b=m_sg_b)
    small_v = dict(pre_norm=v_pre_norm, post_norm=v_post_norm, q_norm=v_q_norm, k_norm=v_k_norm, conv_dw_b=v_conv_dw_b,
                   conv_ln_g=v_conv_ln_g, conv_ln_b=v_conv_ln_b, sg_ln_g=v_sg_ln_g, sg_ln_b=v_sg_ln_b, sg_w=v_sg_w,
                   sg_b=v_sg_b)
    shapes = [small_w[n].shape for n in _SMALL]
    red = _allreduce_small(_pack([g_small[n] for n in _SMALL] + [g_small["conv_dw"]]))
    n_rep = sum(small_w[n].size for n in _SMALL) // LANES
    g_cdw_full = red[n_rep:n_rep + g_small["conv_dw"].size // LANES].reshape(g_small["conv_dw"].shape)
    cdw_cols = conv_dw.shape[2]
    g_cdw = lax.dynamic_slice_in_dim(g_cdw_full, chip * cdw_cols, cdw_cols, axis=2)
    g_slab = _pack([red[:n_rep], g_cdw])
    w_slab = _pack([small_w[n] for n in _SMALL] + [conv_dw])
    m_slab = _pack([small_m[n] for n in _SMALL] + [m_conv_dw])
    v_slab = _pack([small_v[n] for n in _SMALL] + [v_conv_dw])
    rows = w_slab.shape[0]
    outs = _adamw(w_slab[None], g_slab[None], jnp.zeros_like(g_slab)[None], m_slab[None], v_slab[None], rows, "adamw_small")
    unpacked = [dict(zip(_SMALL + ("conv_dw",), _unpack(o_[0], shapes + [conv_dw.shape]))) for o_ in outs]

    big = [dict(w_in=a, w_out=b) for a, b in ((grad_w_in, grad_w_out), (delta_w_in, delta_w_out),
                                             (new_m_w_in, new_m_w_out), (new_v_w_in, new_v_w_out))]
    order = ("pre_norm", "post_norm", "w_in", "w_out", "q_norm", "k_norm", "conv_dw", "conv_dw_b", "conv_ln_g",
             "conv_ln_b", "sg_ln_g", "sg_ln_b", "sg_w", "sg_b")
    result = [loss, grad_x]
    for kind in range(4):
        for name in order:
            result.append(big[kind][name] if name in big[kind] else unpacked[kind][name])
    return tuple(result)
```

```python
import functools

import jax
import jax.numpy as jnp
from jax import lax
from jax.experimental import pallas as pl
from jax.experimental.pallas import tpu as pltpu

F32 = jnp.float32
BF16 = jnp.bfloat16
HI = lax.Precision.HIGHEST
MESH = pl.DeviceIdType.MESH

EPS = 1e-6
D_MODEL = 1024
D_IN = 2816
HEAD_DIM = 64
LANES = 128
N_HEADS = 8
N_KV = 2
GROUP = N_HEADS // N_KV
GRID_W = 64
ROPE_THETA = 10000.0
CONV_K = 31
CONV_W = 256
SG_W = 256
SG_CHUNK = 128
HALO = 16
ATT_SCALE = HEAD_DIM ** -0.5

C_Q, C_K, C_V, C_GA, C_A1, C_A2, C_GC, C_U, C_VS, C_GS = 0, 512, 640, 768, 1280, 1536, 1792, 2048, 2304, 2560

ADAM_LR = 0.001
ADAM_B1 = 0.9
ADAM_B2 = 0.999
ADAM_EPS = 1e-08
ADAM_WD = 0.01
ADAM_STEP = 10

N_CHIPS = 4
N_DEV = 8


def _dot(a, b):
    return jnp.dot(a, b, preferred_element_type=F32)


def _dot_hi(a, b):
    return jnp.dot(a, b, precision=HI, preferred_element_type=F32)


def _dot_nt(a, b):
    return lax.dot_general(a, b, (((1,), (1,)), ((), ())), preferred_element_type=F32)


def _dot_tn(a, b):
    return lax.dot_general(a, b, (((0,), (0,)), ((), ())), preferred_element_type=F32)


def _lane(shape):
    return lax.broadcasted_iota(jnp.int32, shape, 1)


def _sigmoid(x):
    return 1.0 / (1.0 + jnp.exp(-x))


def _silu_fwd_bwd(x):
    s = _sigmoid(x)
    return x * s, s * (1.0 + x * (1.0 - s))


def _erf(x):
    x = jnp.clip(x, -4.0, 4.0)
    x2 = x * x
    a = -2.72614225801306e-10
    a = a * x2 + 2.77068142495902e-08
    a = a * x2 + -2.10102402082508e-06
    a = a * x2 + -5.69250639462346e-05
    a = a * x2 + -7.34990630326855e-04
    a = a * x2 + -2.95459980854025e-03
    a = a * x2 + -1.60960333262415e-02
    b = -1.45660718464996e-05
    b = b * x2 + -2.13374055278905e-04
    b = b * x2 + -1.68282697438203e-03
    b = b * x2 + -7.37332916720468e-03
    b = b * x2 + -1.42647390514189e-02
    return x * a / b


def _gelu_fwd_bwd(x):
    cdf = 0.5 * (1.0 + _erf(x * 0.7071067811865476))
    pdf = jnp.exp(-0.5 * x * x) * 0.3989422804014327
    return x * cdf, cdf + x * pdf


def _rms(x):
    return lax.rsqrt(jnp.mean(x * x, axis=-1, keepdims=True) + EPS)


def _ln_hat(x):
    mu = jnp.mean(x, axis=-1, keepdims=True)
    xc = x - mu
    rs = lax.rsqrt(jnp.mean(xc * xc, axis=-1, keepdims=True) + EPS)
    return xc * rs, rs


def _ln_bwd(dxh, xh, rs):
    return rs * (dxh - jnp.mean(dxh, axis=-1, keepdims=True) - xh * jnp.mean(dxh * xh, axis=-1, keepdims=True))


def _swap16(z):
    parts = []
    for i in range(z.shape[1] // LANES):
        blk = z[:, i * LANES:(i + 1) * LANES]
        lane = _lane(blk.shape)
        parts.append(jnp.where((lane & 16) == 0, pltpu.roll(blk, LANES - 16, 1), pltpu.roll(blk, 16, 1)))
    return parts[0] if len(parts) == 1 else jnp.concatenate(parts, axis=1)


def _head_slab(pair, odd):
    src = pltpu.roll(pair, HEAD_DIM, 1) if odd else pair
    return jnp.where(_lane(pair.shape) < HEAD_DIM, src, 0.0)


def _pair_merge(even, odd):
    return jnp.where(_lane(even.shape) < HEAD_DIM, even, pltpu.roll(odd, HEAD_DIM, 1))


def _heads_to_cat(ref, n_heads):
    pairs = [_pair_merge(ref[2 * p], ref[2 * p + 1]) for p in range(n_heads // 2)]
    return pairs[0] if len(pairs) == 1 else jnp.concatenate(pairs, axis=1)


def _split3(x):
    hi = x.astype(BF16).astype(F32)
    r = x - hi
    mid = r.astype(BF16).astype(F32)
    lo = (r - mid).astype(BF16).astype(F32)
    return hi, mid, lo


def _with_spare(slab, hi, mid, lo):
    lane = _lane(slab.shape)
    return jnp.where(lane == HEAD_DIM, hi, jnp.where(lane == HEAD_DIM + 1, mid, jnp.where(lane == HEAD_DIM + 2, lo, slab)))


def _with_ones(slab):
    lane = _lane(slab.shape)
    return jnp.where((lane >= HEAD_DIM) & (lane < HEAD_DIM + 3), 1.0, slab)


def _conv_window(hext_ref, prev_ref, main, next_ref, first, last, t):
    hext_ref[0:HALO, :] = jnp.where(first, 0.0, prev_ref[...])
    hext_ref[HALO:HALO + t, :] = main
    hext_ref[HALO + t:HALO + t + HALO, :] = jnp.where(last, 0.0, next_ref[...])


def _sgu_mix(v1_bf, w_ref, n_chunks):
    rows = []
    for n in range(n_chunks):
        pairs = []
        for p in range(SG_W // LANES):
            xp = v1_bf[n * SG_CHUNK:(n + 1) * SG_CHUNK, p * LANES:(p + 1) * LANES]
            me = _dot(w_ref[2 * p], xp)
            mo = _dot(w_ref[2 * p + 1], xp)
            pairs.append(jnp.where(_lane(me.shape) < HEAD_DIM, me, mo))
        rows.append(jnp.concatenate(pairs, axis=1))
    return rows[0] if len(rows) == 1 else jnp.concatenate(rows, axis=0)


def _halo_specs(t, s, width):
    per = t // HALO
    nblk = s // HALO
    prev = pl.BlockSpec((HALO, width), lambda i: (jnp.maximum(i * per - 1, 0), 0))
    nxt = pl.BlockSpec((HALO, width), lambda i: (jnp.minimum((i + 1) * per, nblk - 1), 0))
    return prev, nxt


def _const_spec(shape):
    nd = len(shape)
    return pl.BlockSpec(shape, lambda i: (0,) * nd)


def _arb(n=1):
    return pltpu.CompilerParams(dimension_semantics=("arbitrary",) * n)


def _fwd_in(x, g_pre, w_in_bf, bd, qg, kg, cos, sin, t):
    s = x.shape[0]

    def body(x_ref, g_ref, w_ref, bd_ref, qg_ref, kg_ref, cos_ref, sin_ref,
             proj_ref, hc_ref, q_ref, k_ref, v_ref):
        xv = x_ref[...]
        h = (xv * _rms(xv) * g_ref[...]).astype(BF16)
        proj = _dot(h, w_ref[...])
        proj_ref[...] = proj
        cosv = cos_ref[...]
        sinv = sin_ref[...]
        q = proj[:, C_Q:C_K]
        qn = q * lax.rsqrt(_dot_hi(q * q, bd_ref[...]) + EPS) * qg_ref[...]
        qr = (qn * cosv + _swap16(qn) * sinv) * ATT_SCALE
        for hh in range(N_HEADS):
            pair = qr[:, (hh // 2) * LANES:(hh // 2 + 1) * LANES]
            q_ref[hh] = _head_slab(pair, hh % 2 == 1).astype(BF16)
        k = proj[:, C_K:C_V]
        kn = k * lax.rsqrt(_dot_hi(k * k, bd_ref[0:LANES, 0:LANES]) + EPS) * kg_ref[...]
        kr = kn * cosv[:, 0:LANES] + _swap16(kn) * sinv[:, 0:LANES]
        vv = proj[:, C_V:C_GA]
        for hh in range(N_KV):
            k_ref[hh] = _with_ones(_head_slab(kr, hh == 1)).astype(BF16)
            v_ref[hh] = _with_ones(_head_slab(vv, hh == 1)).astype(BF16)
        hc_ref[...] = proj[:, C_A1:C_A2] * _sigmoid(proj[:, C_A2:C_GC])

    row = lambda w: pl.BlockSpec((t, w), lambda i: (i, 0))
    heads = lambda n: pl.BlockSpec((n, t, LANES), lambda i: (0, i, 0))
    return pl.pallas_call(
        body, name="fwd_in", grid=(s // t,),
        in_specs=[row(D_MODEL), _const_spec((1, D_MODEL)), _const_spec((D_MODEL, D_IN)), _const_spec((512, 512)),
                  _const_spec((1, 512)), _const_spec((1, LANES)), row(512), row(512)],
        out_specs=[row(D_IN), row(CONV_W), heads(N_HEADS), heads(N_KV), heads(N_KV)],
        out_shape=[jax.ShapeDtypeStruct((s, D_IN), F32), jax.ShapeDtypeStruct((s, CONV_W), F32),
                   jax.ShapeDtypeStruct((N_HEADS, s, LANES), BF16), jax.ShapeDtypeStruct((N_KV, s, LANES), BF16),
                   jax.ShapeDtypeStruct((N_KV, s, LANES), BF16)],
        compiler_params=_arb(),
    )(x, g_pre, w_in_bf, bd, qg, kg, cos, sin)


def _flash_fwd(q, k, v, tq, tk):
    s = q.shape[1]
    rows = GROUP * tq
    nk = s // tk
    assert nk % 2 == 0

    def body(q_ref, k_ref, v_ref, o_ref, qa_ref, m_scr, acc_scr, s0, s1, p0, p1, a0, a1):
        s_bufs, p_bufs, a_bufs = (s0, s1), (p0, p1), (a0, a1)
        qv = q_ref[...].reshape(rows, LANES)
        m_scr[...] = jnp.full((rows, 1), -jnp.inf, F32)
        acc_scr[...] = jnp.zeros((rows, LANES), F32)
        p1[...] = jnp.zeros((rows, tk), BF16)
        a1[...] = jnp.ones((rows, 1), F32)

        def chunk(ref, c):
            return ref[pl.ds(pl.multiple_of(c * tk, tk), tk), :]

        def scores(c, slot):
            s_bufs[slot][...] = _dot_nt(qv, chunk(k_ref, c))

        def softmax(slot):
            for h in range(GROUP):
                r = slice(h * tq, (h + 1) * tq)
                sc = s_bufs[slot][r, :]
                m_prev = m_scr[r, :]
                m_new = jnp.maximum(m_prev, jnp.max(sc, axis=1, keepdims=True))
                p_bufs[slot][r, :] = jnp.exp(sc - m_new).astype(BF16)
                a_bufs[slot][r, :] = jnp.exp(m_prev - m_new)
                m_scr[r, :] = m_new

        def weighted_values(c, slot):
            acc_scr[...] = a_bufs[slot][...] * acc_scr[...] + _dot(p_bufs[slot][...], chunk(v_ref, c))

        scores(0, 0)

        def trip(t, carry):
            c = 2 * t
            scores(c + 1, 1)
            softmax(0)
            weighted_values(jnp.maximum(c - 1, 0), 1)
            scores(jnp.minimum(c + 2, nk - 1), 0)
            softmax(1)
            weighted_values(c, 0)
            return carry

        lax.fori_loop(0, nk // 2, trip, 0)
        weighted_values(nk - 1, 1)

        acc = acc_scr[...]
        lane = _lane(acc.shape)
        l = jnp.sum(jnp.where(lane == HEAD_DIM, acc, 0.0), axis=1, keepdims=True)
        o_ref[...] = jnp.where(lane < HEAD_DIM, acc / l, 0.0).reshape(GROUP, tq, LANES)
        hi, mid, lo = _split3(-(m_scr[...] + jnp.log(l)))
        qa_ref[...] = _with_spare(qv.astype(F32), hi, mid, lo).astype(BF16).reshape(GROUP, tq, LANES)

    qspec = pl.BlockSpec((GROUP, tq, LANES), lambda j, i: (j, i, 0))
    kvspec = pl.BlockSpec((None, s, LANES), lambda j, i: (j, 0, 0))
    return pl.pallas_call(
        body, name="flash_fwd", grid=(N_KV, s // tq),
        in_specs=[qspec, kvspec, kvspec],
        out_specs=[qspec, qspec],
        out_shape=[jax.ShapeDtypeStruct((N_HEADS, s, LANES), F32), jax.ShapeDtypeStruct((N_HEADS, s, LANES), BF16)],
        scratch_shapes=[pltpu.VMEM((rows, 1), F32), pltpu.VMEM((rows, LANES), F32),
                        pltpu.VMEM((rows, tk), F32), pltpu.VMEM((rows, tk), F32),
                        pltpu.VMEM((rows, tk), BF16), pltpu.VMEM((rows, tk), BF16),
                        pltpu.VMEM((rows, 1), F32), pltpu.VMEM((rows, 1), F32)],
        compiler_params=_arb(2),
    )(q, k, v)


def _groups_fwd(proj_ref, o_ref, hext_ref, cw_ref, cp_ref, sp_ref, sgw_ref, sgb_ref, t):
    proj = proj_ref[...]
    r = {}
    r["att"] = _heads_to_cat(o_ref, N_HEADS)
    r["gate_a"], r["dgate_a"] = _silu_fwd_bwd(proj[:, C_GA:C_A1])
    r["att_g"] = r["att"] * r["gate_a"]
    c0 = jnp.zeros((t, CONV_W), F32) + cp_ref[0:1, :]
    for kk in range(CONV_K):
        c0 = c0 + cw_ref[kk:kk + 1, :] * hext_ref[kk + 1:kk + 1 + t, :]
    r["xh_c"], r["rs_c"] = _ln_hat(c0)
    r["c1"] = r["xh_c"] * cp_ref[1:2, :] + cp_ref[2:3, :]
    r["sg_c1"] = _sigmoid(r["c1"])
    r["c2"] = r["c1"] * r["sg_c1"]
    r["gate_c"], r["dgate_c"] = _silu_fwd_bwd(proj[:, C_GC:C_U])
    r["cnv_g"] = r["c2"] * r["gate_c"]
    r["gu"], r["dgu"] = _gelu_fwd_bwd(proj[:, C_U:C_VS])
    gv, r["dgv"] = _gelu_fwd_bwd(proj[:, C_VS:C_GS])
    r["xh_s"], r["rs_s"] = _ln_hat(gv)
    v1 = r["xh_s"] * sp_ref[0:1, :] + sp_ref[1:2, :]
    r["v1_bf"] = v1.astype(BF16)
    r["mixed"] = _sgu_mix(r["v1_bf"], sgw_ref, t // SG_CHUNK) + jnp.concatenate([sgb_ref[...]] * (t // SG_CHUNK), axis=0)
    r["um"] = r["gu"] * r["mixed"]
    r["gate_s"], r["dgate_s"] = _silu_fwd_bwd(proj[:, C_GS:D_IN])
    r["sgu_g"] = r["um"] * r["gate_s"]
    r["mc_bf"] = jnp.concatenate([r["att_g"], r["cnv_g"], r["sgu_g"]], axis=1).astype(BF16)
    return r


def _fwd_out(x, proj, o, hc, cw, cp, sp, sgw_bf, sgb, w_out_bf, g_post, t):
    s = x.shape[0]

    def body(x_ref, proj_ref, o_ref, hc_ref, hp_ref, hn_ref, cw_ref, cp_ref, sp_ref, sgw_ref, sgb_ref,
             w_ref, g_ref, mix_ref, xn_ref, hext_ref):
        i = pl.program_id(0)
        _conv_window(hext_ref, hp_ref, hc_ref[...], hn_ref, i == 0, i == pl.num_programs(0) - 1, t)
        r = _groups_fwd(proj_ref, o_ref, hext_ref, cw_ref, cp_ref, sp_ref, sgw_ref, sgb_ref, t)
        mix = _dot(r["mc_bf"], w_ref[...])
        mix_ref[...] = mix
        xn_ref[...] = x_ref[...] + mix * _rms(mix) * g_ref[...]

    row = lambda w: pl.BlockSpec((t, w), lambda i: (i, 0))
    hprev, hnext = _halo_specs(t, s, CONV_W)
    return pl.pallas_call(
        body, name="fwd_out", grid=(s // t,),
        in_specs=[row(D_MODEL), row(D_IN), pl.BlockSpec((N_HEADS, t, LANES), lambda i: (0, i, 0)), row(CONV_W),
                  hprev, hnext, _const_spec((32, CONV_W)), _const_spec((8, CONV_W)), _const_spec((8, SG_W)),
                  _const_spec((4, SG_CHUNK, SG_CHUNK)), _const_spec((SG_CHUNK, SG_W)),
                  _const_spec((D_MODEL, D_MODEL)), _const_spec((1, D_MODEL))],
        out_specs=[row(D_MODEL), row(D_MODEL)],
        out_shape=[jax.ShapeDtypeStruct((s, D_MODEL), F32), jax.ShapeDtypeStruct((s, D_MODEL), F32)],
        scratch_shapes=[pltpu.VMEM((t + 2 * HALO, CONV_W), F32)],
        compiler_params=_arb(),
    )(x, proj, o, hc, hc, hc, cw, cp, sp, sgw_bf, sgb, w_out_bf, g_post)


def _loss_grad(y, target, t):
    s = y.shape[0]

    def body(y_ref, t_ref, dy_ref, sq_ref):
        @pl.when(pl.program_id(0) == 0)
        def _():
            sq_ref[...] = jnp.zeros_like(sq_ref)

        err = y_ref[...] - t_ref[...]
        dy_ref[...] = err * (1.0 / D_MODEL)
        sq_ref[...] += jnp.sum(err * err, axis=0, keepdims=True)

    row = pl.BlockSpec((t, D_MODEL), lambda i: (i, 0))
    return pl.pallas_call(
        body, name="loss_grad", grid=(s // t,),
        in_specs=[row, row], out_specs=[row, _const_spec((1, D_MODEL))],
        out_shape=[jax.ShapeDtypeStruct((s, D_MODEL), F32), jax.ShapeDtypeStruct((1, D_MODEL), F32)],
        compiler_params=_arb(),
    )(y, target)


def _bwd_out(dy, mix, proj, o, hc, cw, cp, sp, sgw_bf, sgwt_bf, sgb, w_out_t_bf, g_post, t):
    s = dy.shape[0]
    n_chunks = t // SG_CHUNK

    def body(dy_ref, mix_ref, proj_ref, o_ref, hc_ref, hp_ref, hn_ref, cw_ref, cp_ref, sp_ref, sgw_ref, sgwt_ref,
             sgb_ref, wt_ref, g_ref,
             do_ref, dgs_ref, dc0_ref, gwo_ref, gpost_ref, gcw_ref, gvec_ref, gsgw_ref, gsgb_ref, hext_ref):
        i = pl.program_id(0)

        @pl.when(i == 0)
        def _():
            for ref in (gwo_ref, gpost_ref, gcw_ref, gvec_ref, gsgw_ref, gsgb_ref):
                ref[...] = jnp.zeros_like(ref)

        _conv_window(hext_ref, hp_ref, hc_ref[...], hn_ref, i == 0, i == pl.num_programs(0) - 1, t)
        r = _groups_fwd(proj_ref, o_ref, hext_ref, cw_ref, cp_ref, sp_ref, sgw_ref, sgb_ref, t)

        dyv = dy_ref[...]
        mix_v = mix_ref[...]
        rr = _rms(mix_v)
        gd = dyv * g_ref[...]
        dmix = rr * gd - mix_v * (rr * rr * rr * jnp.mean(gd * mix_v, axis=-1, keepdims=True))
        gpost_ref[...] += jnp.sum(dyv * mix_v * rr, axis=0, keepdims=True)
        dmix_bf = dmix.astype(BF16)
        gwo_ref[...] += _dot_tn(r["mc_bf"], dmix_bf)
        dmc = _dot(dmix_bf, wt_ref[...])

        d_att = dmc[:, 0:512]
        dg_att = d_att * r["att"] * r["dgate_a"]
        d_o = d_att * r["gate_a"]
        prod = d_o * r["att"]
        for p in range(N_HEADS // 2):
            sl = slice(p * LANES, (p + 1) * LANES)
            pr = prod[:, sl]
            tot = jnp.sum(pr, axis=1, keepdims=True)
            ev = jnp.sum(jnp.where(_lane(pr.shape) < HEAD_DIM, pr, 0.0), axis=1, keepdims=True)
            for odd, delta in ((False, ev), (True, tot - ev)):
                hi, mid, lo = _split3(-delta)
                do_ref[2 * p + int(odd)] = _with_spare(_head_slab(d_o[:, sl], odd), hi, mid, lo).astype(BF16)

        dcv = dmc[:, 512:768]
        dg_conv = dcv * r["c2"] * r["dgate_c"]
        dc1 = dcv * r["gate_c"] * (r["sg_c1"] * (1.0 + r["c1"] * (1.0 - r["sg_c1"])))
        dc0 = _ln_bwd(dc1 * cp_ref[1:2, :], r["xh_c"], r["rs_c"])
        dc0_ref[...] = dc0
        for kk in range(CONV_K):
            gcw_ref[kk:kk + 1, :] += jnp.sum(dc0 * hext_ref[kk + 1:kk + 1 + t, :], axis=0, keepdims=True)

        dsg = dmc[:, 768:1024]
        dg_sg = dsg * r["um"] * r["dgate_s"]
        du = dsg * r["mixed"] * r["gate_s"] * r["dgu"]
        dmx = dsg * r["gu"] * r["gate_s"]
        dmx_bf = dmx.astype(BF16)
        sgb_sum = dmx[0:SG_CHUNK, :]
        for n in range(1, n_chunks):
            sgb_sum = sgb_sum + dmx[n * SG_CHUNK:(n + 1) * SG_CHUNK, :]
        gsgb_ref[...] += sgb_sum
        dv1_rows = []
        for n in range(n_chunks):
            pairs = []
            for p in range(SG_W // LANES):
                rs_ = slice(n * SG_CHUNK, (n + 1) * SG_CHUNK)
                ls_ = slice(p * LANES, (p + 1) * LANES)
                dm = dmx_bf[rs_, ls_]
                xp = r["v1_bf"][rs_, ls_]
                low = _lane(dm.shape) < HEAD_DIM
                zero = jnp.zeros_like(dm)
                gsgw_ref[2 * p] += _dot_nt(jnp.where(low, dm, zero), xp)
                gsgw_ref[2 * p + 1] += _dot_nt(jnp.where(low, zero, dm), xp)
                pairs.append(jnp.where(low, _dot(sgwt_ref[2 * p], dm), _dot(sgwt_ref[2 * p + 1], dm)))
            dv1_rows.append(jnp.concatenate(pairs, axis=1))
        dv1 = dv1_rows[0] if n_chunks == 1 else jnp.concatenate(dv1_rows, axis=0)
        dvs = _ln_bwd(dv1 * sp_ref[0:1, :], r["xh_s"], r["rs_s"]) * r["dgv"]

        zrow = jnp.zeros((1, CONV_W), F32)
        gvec_ref[...] += jnp.concatenate([
            jnp.sum(dc0, axis=0, keepdims=True),
            jnp.sum(dc1 * r["xh_c"], axis=0, keepdims=True),
            jnp.sum(dc1, axis=0, keepdims=True),
            jnp.sum(dv1 * r["xh_s"], axis=0, keepdims=True),
            jnp.sum(dv1, axis=0, keepdims=True),
            zrow, zrow, zrow], axis=0)
        dgs_ref[...] = jnp.concatenate([dg_att, dg_conv, du, dvs, dg_sg], axis=1)

    row = lambda w: pl.BlockSpec((t, w), lambda i: (i, 0))
    heads = pl.BlockSpec((N_HEADS, t, LANES), lambda i: (0, i, 0))
    hprev, hnext = _halo_specs(t, s, CONV_W)
    return pl.pallas_call(
        body, name="bwd_out", grid=(s // t,),
        in_specs=[row(D_MODEL), row(D_MODEL), row(D_IN), heads, row(CONV_W), hprev, hnext,
                  _const_spec((32, CONV_W)), _const_spec((8, CONV_W)), _const_spec((8, SG_W)),
                  _const_spec((4, SG_CHUNK, SG_CHUNK)), _const_spec((4, SG_CHUNK, SG_CHUNK)),
                  _const_spec((SG_CHUNK, SG_W)), _const_spec((D_MODEL, D_MODEL)), _const_spec((1, D_MODEL))],
        out_specs=[heads, row(1536), row(CONV_W), _const_spec((D_MODEL, D_MODEL)), _const_spec((1, D_MODEL)),
                   _const_spec((32, CONV_W)), _const_spec((8, CONV_W)), _const_spec((4, SG_CHUNK, SG_CHUNK)),
                   _const_spec((SG_CHUNK, SG_W))],
        out_shape=[jax.ShapeDtypeStruct((N_HEADS, s, LANES), BF16), jax.ShapeDtypeStruct((s, 1536), F32),
                   jax.ShapeDtypeStruct((s, CONV_W), F32), jax.ShapeDtypeStruct((D_MODEL, D_MODEL), F32),
                   jax.ShapeDtypeStruct((1, D_MODEL), F32), jax.ShapeDtypeStruct((32, CONV_W), F32),
                   jax.ShapeDtypeStruct((8, CONV_W), F32), jax.ShapeDtypeStruct((4, SG_CHUNK, SG_CHUNK), F32),
                   jax.ShapeDtypeStruct((SG_CHUNK, SG_W), F32)],
        scratch_shapes=[pltpu.VMEM((t + 2 * HALO, CONV_W), F32)],
        compiler_params=_arb(),
    )(dy, mix, proj, o, hc, hc, hc, cw, cp, sp, sgw_bf, sgwt_bf, sgb, w_out_t_bf, g_post)


def _flash_bwd(qa, doa, k, v, tq, tk):
    s = qa.shape[1]
    rows = GROUP * tq
    nk = s // tk
    n_q = s // tq
    assert nk % 2 == 0

    def body(qa_ref, do_ref, k_ref, v_ref, dq_ref, dk_hbm, dv_hbm,
             dq_scr, dk_scr, dv_scr, s0, s1, d0, d1, p0, p1, e0, e1, sems):
        j, i = pl.program_id(0), pl.program_id(1)
        s_bufs, d_bufs, p_bufs, e_bufs = (s0, s1), (d0, d1), (p0, p1), (e0, e1)
        qv = qa_ref[...].reshape(rows, LANES)
        dov = do_ref[...].reshape(rows, LANES)
        dq_scr[...] = jnp.zeros((rows, LANES), F32)
        p1[...] = jnp.zeros((rows, tk), BF16)
        e1[...] = jnp.zeros((rows, tk), BF16)

        @pl.when(i == 0)
        def _():
            dk_scr[...] = jnp.zeros_like(dk_scr)
            dv_scr[...] = jnp.zeros_like(dv_scr)

        def at(c):
            return pl.ds(pl.multiple_of(c * tk, tk), tk)

        def scores(c, slot):
            s_bufs[slot][...] = _dot_nt(qv, k_ref[at(c), :])
            d_bufs[slot][...] = _dot_nt(dov, v_ref[at(c), :])

        def probs(slot):
            for h in range(GROUP):
                r = slice(h * tq, (h + 1) * tq)
                p = jnp.exp(s_bufs[slot][r, :])
                p_bufs[slot][r, :] = p.astype(BF16)
                e_bufs[slot][r, :] = (p * d_bufs[slot][r, :]).astype(BF16)

        def grads(c, slot):
            ds = e_bufs[slot][...]
            dq_scr[...] += _dot(ds, k_ref[at(c), :])
            dv_scr[at(c), :] += _dot_tn(p_bufs[slot][...], dov)
            dk_scr[at(c), :] += _dot_tn(ds, qv)

        scores(0, 0)

        def trip(t, carry):
            c = 2 * t
            scores(c + 1, 1)
            probs(0)
            grads(jnp.maximum(c - 1, 0), 1)
            scores(jnp.minimum(c + 2, nk - 1), 0)
            probs(1)
            grads(c, 0)
            return carry

        lax.fori_loop(0, nk // 2, trip, 0)
        grads(nk - 1, 1)
        dq_ref[...] = dq_scr[...].reshape(GROUP, tq, LANES)

        @pl.when(i == n_q - 1)
        def _():
            out = [pltpu.make_async_copy(dk_scr, dk_hbm.at[j], sems.at[0]),
                   pltpu.make_async_copy(dv_scr, dv_hbm.at[j], sems.at[1])]
            for cp in out:
                cp.start()
            for cp in out:
                cp.wait()

    qspec = pl.BlockSpec((GROUP, tq, LANES), lambda j, i: (j, i, 0))
    kvspec = pl.BlockSpec((None, s, LANES), lambda j, i: (j, 0, 0))
    hbm = pl.BlockSpec(memory_space=pl.ANY)
    stage_f32 = pltpu.VMEM((rows, tk), F32)
    stage_bf = pltpu.VMEM((rows, tk), BF16)
    return pl.pallas_call(
        body, name="flash_bwd", grid=(N_KV, n_q),
        in_specs=[qspec, qspec, kvspec, kvspec],
        out_specs=[qspec, hbm, hbm],
        out_shape=[jax.ShapeDtypeStruct((N_HEADS, s, LANES), F32), jax.ShapeDtypeStruct((N_KV, s, LANES), F32),
                   jax.ShapeDtypeStruct((N_KV, s, LANES), F32)],
        scratch_shapes=[pltpu.VMEM((rows, LANES), F32), pltpu.VMEM((s, LANES), F32), pltpu.VMEM((s, LANES), F32),
                        stage_f32, stage_f32, stage_f32, stage_f32, stage_bf, stage_bf, stage_bf, stage_bf,
                        pltpu.SemaphoreType.DMA((2,))],
        compiler_params=_arb(2),
    )(qa, doa, k, v)


def _bwd_in(dy, x, proj, dq, dk, dv, dgs, dc0, cw, g_pre, w_in_t_bf, bd, qg, kg, cos, sin, t):
    s = x.shape[0]

    def body(dy_ref, x_ref, proj_ref, dq_ref, dk_ref, dv_ref, dgs_ref, dc_ref, dcp_ref, dcn_ref, cw_ref, g_ref,
             wt_ref, bd_ref, qg_ref, kg_ref, cos_ref, sin_ref,
             dx_ref, dproj_ref, h_ref, gpre_ref, gq_ref, gk_ref, dext_ref):
        i = pl.program_id(0)

        @pl.when(i == 0)
        def _():
            for ref in (gpre_ref, gq_ref, gk_ref):
                ref[...] = jnp.zeros_like(ref)

        proj = proj_ref[...]
        cosv = cos_ref[...]
        sinv = sin_ref[...]

        def head_norm_bwd(dr, z, bdm, g, cs, sn, gacc_ref):
            dn = dr * cs + _swap16(dr * sn)
            rr = lax.rsqrt(_dot_hi(z * z, bdm) + EPS)
            gdn = dn * g
            gacc_ref[...] += jnp.sum(dn * z * rr, axis=0, keepdims=True)
            return rr * gdn - z * (rr * rr * rr * _dot_hi(gdn * z, bdm))

        dq_cat = _heads_to_cat(dq_ref, N_HEADS) * ATT_SCALE
        dzq = head_norm_bwd(dq_cat, proj[:, C_Q:C_K], bd_ref[...], qg_ref[...], cosv, sinv, gq_ref)
        dk_cat = _heads_to_cat(dk_ref, N_KV)
        dzk = head_norm_bwd(dk_cat, proj[:, C_K:C_V], bd_ref[0:LANES, 0:LANES], kg_ref[...],
                            cosv[:, 0:LANES], sinv[:, 0:LANES], gk_ref)
        dv_cat = _heads_to_cat(dv_ref, N_KV)

        _conv_window(dext_ref, dcp_ref, dc_ref[...], dcn_ref, i == 0, i == pl.num_programs(0) - 1, t)
        dhc = jnp.zeros((t, CONV_W), F32)
        for kk in range(CONV_K):
            dhc = dhc + cw_ref[kk:kk + 1, :] * dext_ref[CONV_K - kk:CONV_K - kk + t, :]
        sg = _sigmoid(proj[:, C_A2:C_GC])
        da1 = dhc * sg
        da2 = dhc * proj[:, C_A1:C_A2] * sg * (1.0 - sg)

        dgs = dgs_ref[...]
        dproj_bf = jnp.concatenate([dzq, dzk, dv_cat, dgs[:, 0:512], da1, da2, dgs[:, 512:1536]], axis=1).astype(BF16)
        dproj_ref[...] = dproj_bf
        dh = _dot(dproj_bf, wt_ref[...])

        xv = x_ref[...]
        rr = _rms(xv)
        gv = g_ref[...]
        h_ref[...] = (xv * rr * gv).astype(BF16)
        gdh = dh * gv
        gpre_ref[...] += jnp.sum(dh * xv * rr, axis=0, keepdims=True)
        dx_ref[...] = dy_ref[...] + rr * gdh - xv * (rr * rr * rr * jnp.mean(gdh * xv, axis=-1, keepdims=True))

    row = lambda w: pl.BlockSpec((t, w), lambda i: (i, 0))
    heads = lambda n: pl.BlockSpec((n, t, LANES), lambda i: (0, i, 0))
    hprev, hnext = _halo_specs(t, s, CONV_W)
    return pl.pallas_call(
        body, name="bwd_in", grid=(s // t,),
        in_specs=[row(D_MODEL), row(D_MODEL), row(D_IN), heads(N_HEADS), heads(N_KV), heads(N_KV), row(1536),
                  row(CONV_W), hprev, hnext, _const_spec((32, CONV_W)), _const_spec((1, D_MODEL)),
                  _const_spec((D_IN, D_MODEL)), _const_spec((512, 512)), _const_spec((1, 512)),
                  _const_spec((1, LANES)), row(512), row(512)],
        out_specs=[row(D_MODEL), row(D_IN), row(D_MODEL), _const_spec((1, D_MODEL)), _const_spec((1, 512)),
                   _const_spec((1, LANES))],
        out_shape=[jax.ShapeDtypeStruct((s, D_MODEL), F32), jax.ShapeDtypeStruct((s, D_IN), BF16),
                   jax.ShapeDtypeStruct((s, D_MODEL), BF16), jax.ShapeDtypeStruct((1, D_MODEL), F32),
                   jax.ShapeDtypeStruct((1, 512), F32), jax.ShapeDtypeStruct((1, LANES), F32)],
        scratch_shapes=[pltpu.VMEM((t + 2 * HALO, CONV_W), F32)],
        compiler_params=_arb(),
    )(dy, x, proj, dq, dk, dv, dgs, dc0, dc0, dc0, cw, g_pre, w_in_t_bf, bd, qg, kg, cos, sin)


def _grad_w_in(h_bf, dproj_bf, t):
    s = h_bf.shape[0]
    half = D_IN // 2

    def body(h_ref, d_ref, g_ref):
        @pl.when(pl.program_id(1) == 0)
        def _():
            g_ref[...] = jnp.zeros_like(g_ref)

        g_ref[...] += _dot_tn(h_ref[...], d_ref[...])

    return pl.pallas_call(
        body, name="grad_w_in", grid=(2, s // t),
        in_specs=[pl.BlockSpec((t, D_MODEL), lambda j, i: (i, 0)), pl.BlockSpec((t, half), lambda j, i: (i, j))],
        out_specs=pl.BlockSpec((D_MODEL, half), lambda j, i: (0, j)),
        out_shape=jax.ShapeDtypeStruct((D_MODEL, D_IN), F32),
        compiler_params=_arb(2),
    )(h_bf, dproj_bf)


def _place():
    x, y, c = lax.axis_index("x"), lax.axis_index("y"), lax.axis_index("c")
    chips = [(1 - x, y), (x, 1 - y), (1 - x, 1 - y)]
    return x, y, c, chips


def _any_specs(n):
    return [pl.BlockSpec(memory_space=pl.ANY)] * n


def _gather_weights(w_in_bf, w_out_bf, cdw):
    arrs = (w_in_bf, w_out_bf, cdw)
    n = len(arrs)

    def body(*refs):
        ins, outs = refs[:n], refs[n:2 * n]
        send_sems, recv_sems, local_sems = refs[2 * n:]
        x, y, c, chips = _place()
        mine = 2 * x + y
        local = [pltpu.make_async_copy(ins[a], outs[a].at[mine], local_sems.at[a]) for a in range(n)]
        for cp in local:
            cp.start()

        def copy(j, a, slot, to):
            return pltpu.make_async_remote_copy(src_ref=ins[a], dst_ref=outs[a].at[slot], send_sem=send_sems.at[n * j + a],
                                                recv_sem=recv_sems.at[n * j + a], device_id=to, device_id_type=MESH)

        sends = [copy(j, a, mine, (px, py, c)) for j, (px, py) in enumerate(chips) for a in range(n)]
        for cp in sends:
            cp.start()
        for j, (px, py) in enumerate(chips):
            for a in range(n):
                copy(j, a, 2 * px + py, (px, py, c)).wait_recv()
        for cp in sends:
            cp.wait_send()
        for cp in local:
            cp.wait()

    return pl.pallas_call(
        body, name="gather_weights",
        in_specs=_any_specs(n), out_specs=_any_specs(n),
        out_shape=[jax.ShapeDtypeStruct((N_CHIPS,) + a.shape, a.dtype) for a in arrs],
        scratch_shapes=[pltpu.SemaphoreType.DMA((3 * n,)), pltpu.SemaphoreType.DMA((3 * n,)), pltpu.SemaphoreType.DMA((n,))],
    )(*arrs)


def _scatter_grads(gin_pieces, gw_out):
    depth = gw_out.shape[0]
    rows = gw_out.shape[1] // N_CHIPS

    def body(gin_ref, gout_ref, rin_ref, rout_ref, send_sems, recv_sems, local_sems):
        x, y, c, chips = _place()
        mine = 2 * x + y

        def out_rows(chip):
            return gout_ref.at[:, pl.ds(pl.multiple_of(chip * rows, rows), rows), :]

        local = [pltpu.make_async_copy(gin_ref.at[mine], rin_ref.at[mine], local_sems.at[0]),
                 pltpu.make_async_copy(out_rows(mine), rout_ref.at[mine], local_sems.at[1])]
        for cp in local:
            cp.start()

        def copies(j, shard, slot, to):
            kw = dict(device_id=to, device_id_type=MESH)
            return [pltpu.make_async_remote_copy(src_ref=gin_ref.at[shard], dst_ref=rin_ref.at[slot],
                                                 send_sem=send_sems.at[2 * j], recv_sem=recv_sems.at[2 * j], **kw),
                    pltpu.make_async_remote_copy(src_ref=out_rows(shard), dst_ref=rout_ref.at[slot],
                                                 send_sem=send_sems.at[2 * j + 1], recv_sem=recv_sems.at[2 * j + 1], **kw)]

        sends = [cp for j, (px, py) in enumerate(chips) for cp in copies(j, 2 * px + py, mine, (px, py, c))]
        for cp in sends:
            cp.start()
        for j, (px, py) in enumerate(chips):
            for cp in copies(j, mine, 2 * px + py, (px, py, c)):
                cp.wait_recv()
        for cp in sends:
            cp.wait_send()
        for cp in local:
            cp.wait()

    return pl.pallas_call(
        body, name="scatter_grads",
        in_specs=_any_specs(2), out_specs=_any_specs(2),
        out_shape=[jax.ShapeDtypeStruct(gin_pieces.shape, F32),
                   jax.ShapeDtypeStruct((N_CHIPS, depth, rows, gw_out.shape[2]), F32)],
        scratch_shapes=[pltpu.SemaphoreType.DMA((6,)), pltpu.SemaphoreType.DMA((6,)), pltpu.SemaphoreType.DMA((2,))],
    )(gin_pieces, gw_out)


def _sum_chips(parts, rb, name):
    _, depth, r, cdim = parts.shape

    def body(p_ref, o_ref):
        o_ref[...] = ((p_ref[0] + p_ref[1]) + p_ref[2]) + p_ref[3]

    return pl.pallas_call(
        body, name=name, grid=(depth, r // rb),
        in_specs=[pl.BlockSpec((N_CHIPS, None, rb, cdim), lambda l, i: (0, l, i, 0))],
        out_specs=pl.BlockSpec((None, rb, cdim), lambda l, i: (l, i, 0)),
        out_shape=jax.ShapeDtypeStruct((depth, r, cdim), F32),
        compiler_params=_arb(2),
    )(parts)


def _swap_with_sibling(a, b):
    arrs = (a, b)
    n = len(arrs)

    def body(*refs):
        ins, outs = refs[:n], refs[n:2 * n]
        send_sems, recv_sems = refs[2 * n:]
        x, y, c, _ = _place()
        cps = [pltpu.make_async_remote_copy(src_ref=ins[k], dst_ref=outs[k], send_sem=send_sems.at[k],
                                            recv_sem=recv_sems.at[k], device_id=(x, y, 1 - c), device_id_type=MESH)
               for k in range(n)]
        for cp in cps:
            cp.start()
        for cp in cps:
            cp.wait()

    return pl.pallas_call(
        body, name="swap_with_sibling",
        in_specs=_any_specs(n), out_specs=_any_specs(n),
        out_shape=[jax.ShapeDtypeStruct(v.shape, v.dtype) for v in arrs],
        scratch_shapes=[pltpu.SemaphoreType.DMA((n,)), pltpu.SemaphoreType.DMA((n,))],
    )(*arrs)


def _allreduce_small(slab):
    m, n = slab.shape

    def body(x_ref, out_ref, gath, send_sems, recv_sems, local_sem):
        x, y, c, chips = _place()
        me, sibling = (x, y, c), (x, y, 1 - c)

        def rows(px, py, pc):
            return gath.at[pl.ds(pl.multiple_of((4 * px + 2 * py + pc) * m, 8), m), :]

        def copy(k, block, to, src=None):
            return pltpu.make_async_remote_copy(src_ref=rows(*block) if src is None else src, dst_ref=rows(*block),
                                                send_sem=send_sems.at[k], recv_sem=recv_sems.at[k],
                                                device_id=to, device_id_type=MESH)

        mine = pltpu.make_async_copy(x_ref, rows(*me), local_sem)
        mine.start()
        first = [copy(0, me, sibling, src=x_ref)]
        first += [copy(1 + j, me, (*chip, c), src=x_ref) for j, chip in enumerate(chips)]
        for cp in first:
            cp.start()
        passed = [copy(4 + j, (*chip, c), sibling) for j, chip in enumerate(chips)]
        for j, chip in enumerate(chips):
            copy(1 + j, (*chip, c), me).wait_recv()
            passed[j].start()
        copy(0, sibling, me).wait_recv()
        for j, chip in enumerate(chips):
            copy(4 + j, (*chip, 1 - c), me).wait_recv()
        for cp in first + passed:
            cp.wait_send()
        mine.wait()
        total = gath[0:m, :]
        for d in range(1, N_DEV):
            total = total + gath[d * m:(d + 1) * m, :]
        out_ref[...] = total

    return pl.pallas_call(
        body, name="allreduce_small",
        in_specs=[pl.BlockSpec(memory_space=pltpu.VMEM)],
        out_specs=pl.BlockSpec(memory_space=pltpu.VMEM),
        out_shape=jax.ShapeDtypeStruct((m, n), F32),
        scratch_shapes=[pltpu.VMEM((N_DEV * m, n), F32), pltpu.SemaphoreType.DMA((7,)), pltpu.SemaphoreType.DMA((7,)),
                        pltpu.SemaphoreType.DMA],
    )(slab)


def _adamw(w, ga, gb, m, v, rb, name):
    depth, r, cdim = w.shape

    def body(w_ref, ga_ref, gb_ref, m_ref, v_ref, g_out, d_out, m_out, v_out):
        g = ga_ref[...] + gb_ref[...]
        m2 = ADAM_B1 * m_ref[...] + (1.0 - ADAM_B1) * g
        v2 = ADAM_B2 * v_ref[...] + (1.0 - ADAM_B2) * (g * g)
        m_hat = m2 / (1.0 - ADAM_B1 ** ADAM_STEP)
        v_hat = v2 / (1.0 - ADAM_B2 ** ADAM_STEP)
        g_out[...] = g
        d_out[...] = -ADAM_LR * (m_hat / (jnp.sqrt(v_hat) + ADAM_EPS) + ADAM_WD * w_ref[...])
        m_out[...] = m2
        v_out[...] = v2

    spec = pl.BlockSpec((None, rb, cdim), lambda l, i: (l, i, 0))
    shp = jax.ShapeDtypeStruct((depth, r, cdim), F32)
    return pl.pallas_call(
        body, name=name, grid=(depth, r // rb),
        in_specs=[spec] * 5, out_specs=[spec] * 4, out_shape=[shp] * 4,
        compiler_params=_arb(2),
    )(w, ga, gb, m, v)


def _rope_tables(s):
    t = jnp.arange(s, dtype=jnp.int32)
    row = (t // GRID_W).astype(F32)
    col = (t % GRID_W).astype(F32)
    half = HEAD_DIM // 4
    inv_freq = ROPE_THETA ** (-jnp.arange(half, dtype=F32) / half)
    ar = row[:, None] * inv_freq[None, :]
    ac = col[:, None] * inv_freq[None, :]
    cos = jnp.concatenate([jnp.cos(ar), jnp.cos(ar), jnp.cos(ac), jnp.cos(ac)], axis=1)
    sin = jnp.concatenate([-jnp.sin(ar), jnp.sin(ar), -jnp.sin(ac), jnp.sin(ac)], axis=1)
    return jnp.tile(cos, (1, N_HEADS)), jnp.tile(sin, (1, N_HEADS))


def _pad_rows(a, rows):
    return jnp.concatenate([a, jnp.zeros((rows - a.shape[0],) + a.shape[1:], a.dtype)], axis=0)


_SMALL = ("pre_norm", "post_norm", "q_norm", "k_norm", "conv_dw_b", "conv_ln_g", "conv_ln_b", "sg_ln_g", "sg_ln_b",
          "sg_w", "sg_b")


def _pack(parts):
    flat = jnp.concatenate([p.reshape(-1, LANES) for p in parts], axis=0)
    return _pad_rows(flat, -(-flat.shape[0] // 8) * 8)


def _unpack(slab, shapes):
    out, r = [], 0
    for shp in shapes:
        n = 1
        for d in shp:
            n *= d
        out.append(slab[r:r + n // LANES].reshape(shp))
        r += n // LANES
    return out


def kernel(x, pre_norm, post_norm, w_in, w_out, q_norm, k_norm, conv_dw, conv_dw_b, conv_ln_g, conv_ln_b, sg_ln_g, sg_ln_b, sg_w, sg_b, loss_target, m_pre_norm, m_post_norm, m_w_in, m_w_out, m_q_norm, m_k_norm, m_conv_dw, m_conv_dw_b, m_conv_ln_g, m_conv_ln_b, m_sg_ln_g, m_sg_ln_b, m_sg_w, m_sg_b, v_pre_norm, v_post_norm, v_w_in, v_w_out, v_q_norm, v_k_norm, v_conv_dw, v_conv_dw_b, v_conv_ln_g, v_conv_ln_b, v_sg_ln_g, v_sg_ln_b, v_sg_w, v_sg_b):
    depth = w_in.shape[0]
    s = x.shape[1]
    assert x.shape[0] == 1 and s % SG_CHUNK == 0 and x.shape[2] == D_MODEL
    t = min(256, s)
    tq = min(256, s)
    tk = min(512, s // 2)
    shard_cols = w_in.shape[2]
    chip = 2 * lax.axis_index("x") + lax.axis_index("y")

    gin, gout, gcdw = _gather_weights(w_in.astype(BF16), w_out.astype(BF16), conv_dw)
    w_in_bf = jnp.concatenate([gin[j] for j in range(N_CHIPS)], axis=2)
    w_out_bf = jnp.concatenate([gout[j] for j in range(N_CHIPS)], axis=1)
    cdw_full = jnp.concatenate([gcdw[j] for j in range(N_CHIPS)], axis=2)
    w_in_t_bf = jnp.swapaxes(w_in_bf, 1, 2)
    w_out_t_bf = jnp.swapaxes(w_out_bf, 1, 2)
    sgw_bf = sg_w.astype(BF16)
    sgwt_bf = jnp.swapaxes(sg_w, 2, 3).astype(BF16)

    cos, sin = _rope_tables(s)
    bd = jnp.kron(jnp.eye(N_HEADS, dtype=F32), jnp.full((HEAD_DIM, HEAD_DIM), 1.0 / HEAD_DIM, F32))

    def layer_consts(l):
        cw = _pad_rows(cdw_full[l], 32)
        cp = _pad_rows(jnp.stack([conv_dw_b[l], conv_ln_g[l], conv_ln_b[l]]), 8)
        sp = _pad_rows(jnp.stack([sg_ln_g[l], sg_ln_b[l]]), 8)
        sgb = jnp.repeat(sg_b[l].T, HEAD_DIM, axis=1)
        qg = jnp.tile(q_norm[l], N_HEADS)[None, :]
        kg = jnp.tile(k_norm[l], N_KV)[None, :]
        return cw, cp, sp, sgb, qg, kg

    xs = [x[0]]
    saved = []
    for l in range(depth):
        cw, cp, sp, sgb, qg, kg = layer_consts(l)
        proj, hc, q, k, v = _fwd_in(xs[l], pre_norm[l][None, :], w_in_bf[l], bd, qg, kg, cos, sin, t)
        o, qa = _flash_fwd(q, k, v, tq, tk)
        mix, xn = _fwd_out(xs[l], proj, o, hc, cw, cp, sp, sgw_bf[l], sgb, w_out_bf[l], post_norm[l][None, :], t)
        saved.append((proj, hc, qa, k, v, o, mix))
        xs.append(xn)

    dy, sq = _loss_grad(xs[depth], loss_target[0], t)
    loss = lax.psum(0.5 * jnp.sum(sq) / D_MODEL, ("x", "y", "c"))

    g_w_in, g_w_out, g_small = [], [], {n: [] for n in _SMALL + ("conv_dw",)}
    for l in reversed(range(depth)):
        cw, cp, sp, sgb, qg, kg = layer_consts(l)
        proj, hc, qa, k, v, o, mix = saved[l]
        doa, dgs, dc0, gwo, gpost, gcw, gvec, gsgw, gsgb = _bwd_out(
            dy, mix, proj, o, hc, cw, cp, sp, sgw_bf[l], sgwt_bf[l], sgb, w_out_t_bf[l], post_norm[l][None, :], t)
        dq, dk, dv = _flash_bwd(qa, doa, k, v, tq, tk)
        dy, dproj_bf, h_bf, gpre, gq, gk = _bwd_in(dy, xs[l], proj, dq, dk, dv, dgs, dc0, cw, pre_norm[l][None, :],
                                                  w_in_t_bf[l], bd, qg, kg, cos, sin, t)
        g_w_in.append(_grad_w_in(h_bf, dproj_bf, min(512, s)))
        g_w_out.append(gwo)
        g_small["pre_norm"].append(gpre[0])
        g_small["post_norm"].append(gpost[0])
        g_small["q_norm"].append(gq[0].reshape(N_HEADS, HEAD_DIM).sum(0))
        g_small["k_norm"].append(gk[0].reshape(N_KV, HEAD_DIM).sum(0))
        g_small["conv_dw"].append(gcw[:CONV_K])
        g_small["conv_dw_b"].append(gvec[0])
        g_small["conv_ln_g"].append(gvec[1])
        g_small["conv_ln_b"].append(gvec[2])
        g_small["sg_ln_g"].append(gvec[3])
        g_small["sg_ln_b"].append(gvec[4])
        g_small["sg_w"].append(gsgw)
        g_small["sg_b"].append(gsgb.reshape(SG_CHUNK, SG_W // HEAD_DIM, HEAD_DIM).sum(-1).T)
    grad_x = dy[None]
    g_w_in = jnp.stack(g_w_in[::-1])
    g_w_out = jnp.stack(g_w_out[::-1])
    g_small = {n: jnp.stack(vals[::-1]) for n, vals in g_small.items()}

    gin_pieces = jnp.stack([g_w_in[:, :, j * shard_cols:(j + 1) * shard_cols] for j in range(N_CHIPS)])
    rin, rout = _scatter_grads(gin_pieces, g_w_out)
    s_in = _sum_chips(rin, 256, "sum_chips_w_in")
    s_out = _sum_chips(rout, 256, "sum_chips_w_out")
    t_in, t_out = _swap_with_sibling(s_in, s_out)
    grad_w_in, delta_w_in, new_m_w_in, new_v_w_in = _adamw(w_in, s_in, t_in, m_w_in, v_w_in, 256, "adamw_w_in")
    grad_w_out, delta_w_out, new_m_w_out, new_v_w_out = _adamw(w_out, s_out, t_out, m_w_out, v_w_out, 256, "adamw_w_out")

    small_w = dict(pre_norm=pre_norm, post_norm=post_norm, q_norm=q_norm, k_norm=k_norm, conv_dw_b=conv_dw_b,
                   conv_ln_g=conv_ln_g, conv_ln_b=conv_ln_b, sg_ln_g=sg_ln_g, sg_ln_b=sg_ln_b, sg_w=sg_w, sg_b=sg_b)
    small_m = dict(pre_norm=m_pre_norm, post_norm=m_post_norm, q_norm=m_q_norm, k_norm=m_k_norm, conv_dw_b=m_conv_dw_b,
                   conv_ln_g=m_conv_ln_g, conv_ln_b=m_conv_ln_b, sg_ln_g=m_sg_ln_g, sg_ln_b=m_sg_ln_b, sg_w=m_sg_w,
                   sg_b=m_sg_b)
    small_v = dict(pre_norm=v_pre_norm, post_norm=v_post_norm, q_norm=v_q_norm, k_norm=v_k_norm, conv_dw_b=v_conv_dw_b,
                   conv_ln_g=v_conv_ln_g, conv_ln_b=v_conv_ln_b, sg_ln_g=v_sg_ln_g, sg_ln_b=v_sg_ln_b, sg_w=v_sg_w,
                   sg_b=v_sg_b)
    shapes = [small_w[n].shape for n in _SMALL]
    red = _allreduce_small(_pack([g_small[n] for n in _SMALL] + [g_small["conv_dw"]]))
    n_rep = sum(small_w[n].size for n in _SMALL) // LANES
    g_cdw_full = red[n_rep:n_rep + g_small["conv_dw"].size // LANES].reshape(g_small["conv_dw"].shape)
    cdw_cols = conv_dw.shape[2]
    g_cdw = lax.dynamic_slice_in_dim(g_cdw_full, chip * cdw_cols, cdw_cols, axis=2)
    g_slab = _pack([red[:n_rep], g_cdw])
    w_slab = _pack([small_w[n] for n in _SMALL] + [conv_dw])
    m_slab = _pack([small_m[n] for n in _SMALL] + [m_conv_dw])
    v_slab = _pack([small_v[n] for n in _SMALL] + [v_conv_dw])
    rows = w_slab.shape[0]
    outs = _adamw(w_slab[None], g_slab[None], jnp.zeros_like(g_slab)[None], m_slab[None], v_slab[None], rows, "adamw_small")
    unpacked = [dict(zip(_SMALL + ("conv_dw",), _unpack(o_[0], shapes + [conv_dw.shape]))) for o_ in outs]

    big = [dict(w_in=a, w_out=b) for a, b in ((grad_w_in, grad_w_out), (delta_w_in, delta_w_out),
                                             (new_m_w_in, new_m_w_out), (new_v_w_in, new_v_w_out))]
    order = ("pre_norm", "post_norm", "w_in", "w_out", "q_norm", "k_norm", "conv_dw", "conv_dw_b", "conv_ln_g",
             "conv_ln_b", "sg_ln_g", "sg_ln_b", "sg_w", "sg_b")
    result = [loss, grad_x]
    for kind in range(4):
        for name in order:
            result.append(big[kind][name] if name in big[kind] else unpacked[kind][name])
    return tuple(result)
```

```python
import functools

import jax
import jax.numpy as jnp
from jax import lax
from jax.experimental import pallas as pl
from jax.experimental.pallas import tpu as pltpu

F32 = jnp.float32
BF16 = jnp.bfloat16
MESH = pl.DeviceIdType.MESH

EPS = 1e-6
D_MODEL = 1024
D_IN = 2816
HEAD_DIM = 64
LANES = 128
SUBLANES = 8
N_HEADS = 8
N_KV = 2
GROUP = N_HEADS // N_KV
GRID_W = 64
ROPE_THETA = 10000.0
CONV_K = 31
CONV_W = 256
SG_W = 256
SG_CHUNK = 128
HALO = 16
ATT_SCALE = HEAD_DIM ** -0.5

C_Q, C_K, C_V, C_GA, C_A1, C_A2, C_GC, C_U, C_VS, C_GS = 0, 512, 640, 768, 1280, 1536, 1792, 2048, 2304, 2560

ADAM_LR = 0.001
ADAM_B1 = 0.9
ADAM_B2 = 0.999
ADAM_EPS = 1e-08
ADAM_WD = 0.01
ADAM_STEP = 10

N_CHIPS = 4
N_DEV = 8


def _dot(a, b):
    return jnp.dot(a, b, preferred_element_type=F32)


def _group_mean(x, bd_bf):
    hi = x.astype(BF16)
    lo = (x - hi.astype(F32)).astype(BF16)
    return _dot(hi, bd_bf) + _dot(lo, bd_bf)


def _dot_nt(a, b):
    return lax.dot_general(a, b, (((1,), (1,)), ((), ())), preferred_element_type=F32)


def _dot_tn(a, b):
    return lax.dot_general(a, b, (((0,), (0,)), ((), ())), preferred_element_type=F32)


def _lane(shape):
    return lax.broadcasted_iota(jnp.int32, shape, 1)


def _sigmoid(x):
    return 1.0 / (1.0 + jnp.exp(-x))


def _silu_fwd_bwd(x):
    s = _sigmoid(x)
    return x * s, s * (1.0 + x * (1.0 - s))


def _erf(x):
    x = jnp.clip(x, -4.0, 4.0)
    x2 = x * x
    a = -2.72614225801306e-10
    a = a * x2 + 2.77068142495902e-08
    a = a * x2 + -2.10102402082508e-06
    a = a * x2 + -5.69250639462346e-05
    a = a * x2 + -7.34990630326855e-04
    a = a * x2 + -2.95459980854025e-03
    a = a * x2 + -1.60960333262415e-02
    b = -1.45660718464996e-05
    b = b * x2 + -2.13374055278905e-04
    b = b * x2 + -1.68282697438203e-03
    b = b * x2 + -7.37332916720468e-03
    b = b * x2 + -1.42647390514189e-02
    return x * a / b


def _gelu_fwd_bwd(x):
    cdf = 0.5 * (1.0 + _erf(x * 0.7071067811865476))
    pdf = jnp.exp(-0.5 * x * x) * 0.3989422804014327
    return x * cdf, cdf + x * pdf


def _rms(x):
    return lax.rsqrt(jnp.mean(x * x, axis=-1, keepdims=True) + EPS)


def _ln_hat(x):
    mu = jnp.mean(x, axis=-1, keepdims=True)
    xc = x - mu
    rs = lax.rsqrt(jnp.mean(xc * xc, axis=-1, keepdims=True) + EPS)
    return xc * rs, rs


def _ln_bwd(dxh, xh, rs):
    return rs * (dxh - jnp.mean(dxh, axis=-1, keepdims=True) - xh * jnp.mean(dxh * xh, axis=-1, keepdims=True))


def _swap16(z):
    parts = []
    for i in range(z.shape[1] // LANES):
        blk = z[:, i * LANES:(i + 1) * LANES]
        lane = _lane(blk.shape)
        parts.append(jnp.where((lane & 16) == 0, pltpu.roll(blk, LANES - 16, 1), pltpu.roll(blk, 16, 1)))
    return parts[0] if len(parts) == 1 else jnp.concatenate(parts, axis=1)


def _head_slab(pair, odd):
    src = pltpu.roll(pair, HEAD_DIM, 1) if odd else pair
    return jnp.where(_lane(pair.shape) < HEAD_DIM, src, 0.0)


def _pair_merge(even, odd):
    return jnp.where(_lane(even.shape) < HEAD_DIM, even, pltpu.roll(odd, HEAD_DIM, 1))


def _heads_to_cat(ref, n_heads):
    pairs = [_pair_merge(ref[2 * p], ref[2 * p + 1]) for p in range(n_heads // 2)]
    return pairs[0] if len(pairs) == 1 else jnp.concatenate(pairs, axis=1)


def _split3(x):
    hi = x.astype(BF16).astype(F32)
    r = x - hi
    mid = r.astype(BF16).astype(F32)
    lo = (r - mid).astype(BF16).astype(F32)
    return hi, mid, lo


def _with_spare(slab, hi, mid, lo):
    lane = _lane(slab.shape)
    return jnp.where(lane == HEAD_DIM, hi, jnp.where(lane == HEAD_DIM + 1, mid, jnp.where(lane == HEAD_DIM + 2, lo, slab)))


def _with_ones(slab):
    lane = _lane(slab.shape)
    return jnp.where((lane >= HEAD_DIM) & (lane < HEAD_DIM + 3), 1.0, slab)


def _conv_window(rot_ref, prev_ref, main, next_ref, first, last, t):
    n = t + 2 * HALO
    full = jnp.concatenate([jnp.where(first, 0.0, prev_ref[...]), main, jnp.where(last, 0.0, next_ref[...])], axis=0)
    rot_ref[0] = full
    for b in range(1, SUBLANES):
        rot_ref[b] = pltpu.roll(full, n - b, 0)


def _tap(rot_ref, start, t):
    a, b = divmod(start, SUBLANES)
    return rot_ref[b, SUBLANES * a:SUBLANES * a + t, :]


def _sgu_mix(v1_bf, w_ref, n_chunks):
    rows = []
    for n in range(n_chunks):
        pairs = []
        for p in range(SG_W // LANES):
            xp = v1_bf[n * SG_CHUNK:(n + 1) * SG_CHUNK, p * LANES:(p + 1) * LANES]
            me = _dot(w_ref[2 * p], xp)
            mo = _dot(w_ref[2 * p + 1], xp)
            pairs.append(jnp.where(_lane(me.shape) < HEAD_DIM, me, mo))
        rows.append(jnp.concatenate(pairs, axis=1))
    return rows[0] if len(rows) == 1 else jnp.concatenate(rows, axis=0)


def _halo_specs(t, s, width):
    per = t // HALO
    nblk = s // HALO
    prev = pl.BlockSpec((HALO, width), lambda i: (jnp.maximum(i * per - 1, 0), 0))
    nxt = pl.BlockSpec((HALO, width), lambda i: (jnp.minimum((i + 1) * per, nblk - 1), 0))
    return prev, nxt


def _const_spec(shape):
    nd = len(shape)
    return pl.BlockSpec(shape, lambda i: (0,) * nd)


def _arb(n=1):
    return pltpu.CompilerParams(dimension_semantics=("arbitrary",) * n)


def _fwd_in(x, g_pre, w_in_bf, bd, qg, kg, cos, sin, t):
    s = x.shape[0]

    def body(x_ref, g_ref, w_ref, bd_ref, qg_ref, kg_ref, cos_ref, sin_ref,
             proj_ref, hc_ref, q_ref, k_ref, v_ref):
        xv = x_ref[...]
        h = (xv * _rms(xv) * g_ref[...]).astype(BF16)
        proj = _dot(h, w_ref[...])
        proj_ref[...] = proj
        cos_pair = cos_ref[...]
        sin_pair = sin_ref[...]
        cosv = jnp.concatenate([cos_pair] * (N_HEADS // 2), axis=1)
        sinv = jnp.concatenate([sin_pair] * (N_HEADS // 2), axis=1)
        q = proj[:, C_Q:C_K]
        qn = q * lax.rsqrt(_group_mean(q * q, bd_ref[...]) + EPS) * qg_ref[...]
        qr = (qn * cosv + _swap16(qn) * sinv) * ATT_SCALE
        for hh in range(N_HEADS):
            pair = qr[:, (hh // 2) * LANES:(hh // 2 + 1) * LANES]
            q_ref[hh] = _head_slab(pair, hh % 2 == 1).astype(BF16)
        k = proj[:, C_K:C_V]
        kn = k * lax.rsqrt(_group_mean(k * k, bd_ref[0:LANES, 0:LANES]) + EPS) * kg_ref[...]
        kr = kn * cos_pair + _swap16(kn) * sin_pair
        vv = proj[:, C_V:C_GA]
        for hh in range(N_KV):
            k_ref[hh] = _with_ones(_head_slab(kr, hh == 1)).astype(BF16)
            v_ref[hh] = _with_ones(_head_slab(vv, hh == 1)).astype(BF16)
        hc_ref[...] = proj[:, C_A1:C_A2] * _sigmoid(proj[:, C_A2:C_GC])

    row = lambda w: pl.BlockSpec((t, w), lambda i: (i, 0))
    heads = lambda n: pl.BlockSpec((n, t, LANES), lambda i: (0, i, 0))
    return pl.pallas_call(
        body, name="fwd_in", grid=(s // t,),
        in_specs=[row(D_MODEL), _const_spec((1, D_MODEL)), _const_spec((D_MODEL, D_IN)), _const_spec((512, 512)),
                  _const_spec((1, 512)), _const_spec((1, LANES)), row(LANES), row(LANES)],
        out_specs=[row(D_IN), row(CONV_W), heads(N_HEADS), heads(N_KV), heads(N_KV)],
        out_shape=[jax.ShapeDtypeStruct((s, D_IN), F32), jax.ShapeDtypeStruct((s, CONV_W), F32),
                   jax.ShapeDtypeStruct((N_HEADS, s, LANES), BF16), jax.ShapeDtypeStruct((N_KV, s, LANES), BF16),
                   jax.ShapeDtypeStruct((N_KV, s, LANES), BF16)],
        compiler_params=_arb(),
    )(x, g_pre, w_in_bf, bd, qg, kg, cos, sin)


def _chunks_per_trip(nk, want):
    assert nk % 2 == 0 and want % 2 == 0
    while nk % want:
        want //= 2
    return want


def _flash_fwd(q, k, v, tq, tk):
    s = q.shape[1]
    rows = GROUP * tq
    nk = s // tk
    per_trip = _chunks_per_trip(nk, 4)

    def body(q_ref, k_ref, v_ref, o_ref, qa_ref, m_scr, acc_scr, s0, s1, p0, p1, a0, a1):
        s_bufs, p_bufs, a_bufs = (s0, s1), (p0, p1), (a0, a1)
        qv = q_ref[...].reshape(rows, LANES)
        m_scr[...] = jnp.full((rows, 1), -jnp.inf, F32)
        acc_scr[...] = jnp.zeros((rows, LANES), F32)
        p1[...] = jnp.zeros((rows, tk), BF16)
        a1[...] = jnp.ones((rows, 1), F32)

        def chunk(ref, c):
            return ref[pl.ds(pl.multiple_of(c * tk, tk), tk), :]

        def scores(c, slot):
            s_bufs[slot][...] = _dot_nt(qv, chunk(k_ref, c))

        def softmax(slot):
            for h in range(GROUP):
                r = slice(h * tq, (h + 1) * tq)
                sc = s_bufs[slot][r, :]
                m_prev = m_scr[r, :]
                m_new = jnp.maximum(m_prev, jnp.max(sc, axis=1, keepdims=True))
                p_bufs[slot][r, :] = jnp.exp(sc - m_new).astype(BF16)
                a_bufs[slot][r, :] = jnp.exp(m_prev - m_new)
                m_scr[r, :] = m_new

        def weighted_values(c, slot):
            acc_scr[...] = a_bufs[slot][...] * acc_scr[...] + _dot(p_bufs[slot][...], chunk(v_ref, c))

        scores(0, 0)

        def trip(t, carry):
            for i in range(per_trip):
                c, slot = per_trip * t + i, i % 2
                scores(jnp.minimum(c + 1, nk - 1), 1 - slot)
                softmax(slot)
                weighted_values(jnp.maximum(c - 1, 0), 1 - slot)
            return carry

        lax.fori_loop(0, nk // per_trip, trip, 0)
        weighted_values(nk - 1, 1)

        acc = acc_scr[...]
        lane = _lane(acc.shape)
        l = jnp.sum(jnp.where(lane == HEAD_DIM, acc, 0.0), axis=1, keepdims=True)
        o_ref[...] = jnp.where(lane < HEAD_DIM, acc / l, 0.0).reshape(GROUP, tq, LANES)
        hi, mid, lo = _split3(-(m_scr[...] + jnp.log(l)))
        qa_ref[...] = _with_spare(qv.astype(F32), hi, mid, lo).astype(BF16).reshape(GROUP, tq, LANES)

    qspec = pl.BlockSpec((GROUP, tq, LANES), lambda j, i: (j, i, 0))
    kvspec = pl.BlockSpec((None, s, LANES), lambda j, i: (j, 0, 0))
    return pl.pallas_call(
        body, name="flash_fwd", grid=(N_KV, s // tq),
        in_specs=[qspec, kvspec, kvspec],
        out_specs=[qspec, qspec],
        out_shape=[jax.ShapeDtypeStruct((N_HEADS, s, LANES), F32), jax.ShapeDtypeStruct((N_HEADS, s, LANES), BF16)],
        scratch_shapes=[pltpu.VMEM((rows, 1), F32), pltpu.VMEM((rows, LANES), F32),
                        pltpu.VMEM((rows, tk), F32), pltpu.VMEM((rows, tk), F32),
                        pltpu.VMEM((rows, tk), BF16), pltpu.VMEM((rows, tk), BF16),
                        pltpu.VMEM((rows, 1), F32), pltpu.VMEM((rows, 1), F32)],
        compiler_params=_arb(2),
    )(q, k, v)


def _groups_fwd(proj_ref, o_ref, hext_ref, cw_ref, cp_ref, sp_ref, sgw_ref, sgb_ref, t):
    proj = proj_ref[...]
    r = {}
    r["att"] = _heads_to_cat(o_ref, N_HEADS)
    r["gate_a"], r["dgate_a"] = _silu_fwd_bwd(proj[:, C_GA:C_A1])
    r["att_g"] = r["att"] * r["gate_a"]
    c0 = jnp.zeros((t, CONV_W), F32) + cp_ref[0:1, :]
    for kk in range(CONV_K):
        c0 = c0 + cw_ref[kk:kk + 1, :] * _tap(hext_ref, kk + 1, t)
    r["xh_c"], r["rs_c"] = _ln_hat(c0)
    r["c1"] = r["xh_c"] * cp_ref[1:2, :] + cp_ref[2:3, :]
    r["sg_c1"] = _sigmoid(r["c1"])
    r["c2"] = r["c1"] * r["sg_c1"]
    r["gate_c"], r["dgate_c"] = _silu_fwd_bwd(proj[:, C_GC:C_U])
    r["cnv_g"] = r["c2"] * r["gate_c"]
    r["gu"], r["dgu"] = _gelu_fwd_bwd(proj[:, C_U:C_VS])
    gv, r["dgv"] = _gelu_fwd_bwd(proj[:, C_VS:C_GS])
    r["xh_s"], r["rs_s"] = _ln_hat(gv)
    v1 = r["xh_s"] * sp_ref[0:1, :] + sp_ref[1:2, :]
    r["v1_bf"] = v1.astype(BF16)
    r["mixed"] = _sgu_mix(r["v1_bf"], sgw_ref, t // SG_CHUNK) + jnp.concatenate([sgb_ref[...]] * (t // SG_CHUNK), axis=0)
    r["um"] = r["gu"] * r["mixed"]
    r["gate_s"], r["dgate_s"] = _silu_fwd_bwd(proj[:, C_GS:D_IN])
    r["sgu_g"] = r["um"] * r["gate_s"]
    r["mc_bf"] = jnp.concatenate([r["att_g"], r["cnv_g"], r["sgu_g"]], axis=1).astype(BF16)
    return r


def _fwd_out(x, proj, o, hc, cw, cp, sp, sgw_bf, sgb, w_out_bf, g_post, t):
    s = x.shape[0]

    def body(x_ref, proj_ref, o_ref, hc_ref, hp_ref, hn_ref, cw_ref, cp_ref, sp_ref, sgw_ref, sgb_ref,
             w_ref, g_ref, mix_ref, xn_ref, hext_ref):
        i = pl.program_id(0)
        _conv_window(hext_ref, hp_ref, hc_ref[...], hn_ref, i == 0, i == pl.num_programs(0) - 1, t)
        r = _groups_fwd(proj_ref, o_ref, hext_ref, cw_ref, cp_ref, sp_ref, sgw_ref, sgb_ref, t)
        mix = _dot(r["mc_bf"], w_ref[...])
        mix_ref[...] = mix
        xn_ref[...] = x_ref[...] + mix * _rms(mix) * g_ref[...]

    row = lambda w: pl.BlockSpec((t, w), lambda i: (i, 0))
    hprev, hnext = _halo_specs(t, s, CONV_W)
    return pl.pallas_call(
        body, name="fwd_out", grid=(s // t,),
        in_specs=[row(D_MODEL), row(D_IN), pl.BlockSpec((N_HEADS, t, LANES), lambda i: (0, i, 0)), row(CONV_W),
                  hprev, hnext, _const_spec((32, CONV_W)), _const_spec((8, CONV_W)), _const_spec((8, SG_W)),
                  _const_spec((4, SG_CHUNK, SG_CHUNK)), _const_spec((SG_CHUNK, SG_W)),
                  _const_spec((D_MODEL, D_MODEL)), _const_spec((1, D_MODEL))],
        out_specs=[row(D_MODEL), row(D_MODEL)],
        out_shape=[jax.ShapeDtypeStruct((s, D_MODEL), F32), jax.ShapeDtypeStruct((s, D_MODEL), F32)],
        scratch_shapes=[pltpu.VMEM((SUBLANES, t + 2 * HALO, CONV_W), F32)],
        compiler_params=_arb(),
    )(x, proj, o, hc, hc, hc, cw, cp, sp, sgw_bf, sgb, w_out_bf, g_post)


def _loss_grad(y, target, t):
    s = y.shape[0]

    def body(y_ref, t_ref, dy_ref, sq_ref):
        @pl.when(pl.program_id(0) == 0)
        def _():
            sq_ref[...] = jnp.zeros_like(sq_ref)

        err = y_ref[...] - t_ref[...]
        dy_ref[...] = err * (1.0 / D_MODEL)
        sq_ref[...] += jnp.sum(err * err, axis=0, keepdims=True)

    row = pl.BlockSpec((t, D_MODEL), lambda i: (i, 0))
    return pl.pallas_call(
        body, name="loss_grad", grid=(s // t,),
        in_specs=[row, row], out_specs=[row, _const_spec((1, D_MODEL))],
        out_shape=[jax.ShapeDtypeStruct((s, D_MODEL), F32), jax.ShapeDtypeStruct((1, D_MODEL), F32)],
        compiler_params=_arb(),
    )(y, target)


def _bwd_out(dy, mix, proj, o, hc, cw, cp, sp, sgw_bf, sgwt_bf, sgb, w_out_t_bf, g_post, t):
    s = dy.shape[0]
    n_chunks = t // SG_CHUNK

    def body(dy_ref, mix_ref, proj_ref, o_ref, hc_ref, hp_ref, hn_ref, cw_ref, cp_ref, sp_ref, sgw_ref, sgwt_ref,
             sgb_ref, wt_ref, g_ref,
             do_ref, dgs_ref, dc0_ref, gwo_ref, gpost_ref, gcw_ref, gvec_ref, gsgw_ref, gsgb_ref, hext_ref):
        i = pl.program_id(0)

        @pl.when(i == 0)
        def _():
            for ref in (gwo_ref, gpost_ref, gcw_ref, gvec_ref, gsgw_ref, gsgb_ref):
                ref[...] = jnp.zeros_like(ref)

        _conv_window(hext_ref, hp_ref, hc_ref[...], hn_ref, i == 0, i == pl.num_programs(0) - 1, t)
        r = _groups_fwd(proj_ref, o_ref, hext_ref, cw_ref, cp_ref, sp_ref, sgw_ref, sgb_ref, t)

        dyv = dy_ref[...]
        mix_v = mix_ref[...]
        rr = _rms(mix_v)
        gd = dyv * g_ref[...]
        dmix = rr * gd - mix_v * (rr * rr * rr * jnp.mean(gd * mix_v, axis=-1, keepdims=True))
        gpost_ref[...] += jnp.sum(dyv * mix_v * rr, axis=0, keepdims=True)
        dmix_bf = dmix.astype(BF16)
        gwo_ref[...] += _dot_tn(r["mc_bf"], dmix_bf)
        dmc = _dot(dmix_bf, wt_ref[...])

        d_att = dmc[:, 0:512]
        dg_att = d_att * r["att"] * r["dgate_a"]
        d_o = d_att * r["gate_a"]
        prod = d_o * r["att"]
        for p in range(N_HEADS // 2):
            sl = slice(p * LANES, (p + 1) * LANES)
            pr = prod[:, sl]
            tot = jnp.sum(pr, axis=1, keepdims=True)
            ev = jnp.sum(jnp.where(_lane(pr.shape) < HEAD_DIM, pr, 0.0), axis=1, keepdims=True)
            for odd, delta in ((False, ev), (True, tot - ev)):
                hi, mid, lo = _split3(-delta)
                do_ref[2 * p + int(odd)] = _with_spare(_head_slab(d_o[:, sl], odd), hi, mid, lo).astype(BF16)

        dcv = dmc[:, 512:768]
        dg_conv = dcv * r["c2"] * r["dgate_c"]
        dc1 = dcv * r["gate_c"] * (r["sg_c1"] * (1.0 + r["c1"] * (1.0 - r["sg_c1"])))
        dc0 = _ln_bwd(dc1 * cp_ref[1:2, :], r["xh_c"], r["rs_c"])
        dc0_ref[...] = dc0
        for kk in range(CONV_K):
            gcw_ref[kk:kk + 1, :] += jnp.sum(dc0 * _tap(hext_ref, kk + 1, t), axis=0, keepdims=True)

        dsg = dmc[:, 768:1024]
        dg_sg = dsg * r["um"] * r["dgate_s"]
        du = dsg * r["mixed"] * r["gate_s"] * r["dgu"]
        dmx = dsg * r["gu"] * r["gate_s"]
        dmx_bf = dmx.astype(BF16)
        sgb_sum = dmx[0:SG_CHUNK, :]
        for n in range(1, n_chunks):
            sgb_sum = sgb_sum + dmx[n * SG_CHUNK:(n + 1) * SG_CHUNK, :]
        gsgb_ref[...] += sgb_sum
        dv1_rows = []
        for n in range(n_chunks):
            pairs = []
            for p in range(SG_W // LANES):
                rs_ = slice(n * SG_CHUNK, (n + 1) * SG_CHUNK)
                ls_ = slice(p * LANES, (p + 1) * LANES)
                dm = dmx_bf[rs_, ls_]
                xp = r["v1_bf"][rs_, ls_]
                low = _lane(dm.shape) < HEAD_DIM
                zero = jnp.zeros_like(dm)
                gsgw_ref[2 * p] += _dot_nt(jnp.where(low, dm, zero), xp)
                gsgw_ref[2 * p + 1] += _dot_nt(jnp.where(low, zero, dm), xp)
                pairs.append(jnp.where(low, _dot(sgwt_ref[2 * p], dm), _dot(sgwt_ref[2 * p + 1], dm)))
            dv1_rows.append(jnp.concatenate(pairs, axis=1))
        dv1 = dv1_rows[0] if n_chunks == 1 else jnp.concatenate(dv1_rows, axis=0)
        dvs = _ln_bwd(dv1 * sp_ref[0:1, :], r["xh_s"], r["rs_s"]) * r["dgv"]

        zrow = jnp.zeros((1, CONV_W), F32)
        gvec_ref[...] += jnp.concatenate([
            jnp.sum(dc0, axis=0, keepdims=True),
            jnp.sum(dc1 * r["xh_c"], axis=0, keepdims=True),
            jnp.sum(dc1, axis=0, keepdims=True),
            jnp.sum(dv1 * r["xh_s"], axis=0, keepdims=True),
            jnp.sum(dv1, axis=0, keepdims=True),
            zrow, zrow, zrow], axis=0)
        dgs_ref[...] = jnp.concatenate([dg_att, dg_conv, du, dvs, dg_sg], axis=1)

    row = lambda w: pl.BlockSpec((t, w), lambda i: (i, 0))
    heads = pl.BlockSpec((N_HEADS, t, LANES), lambda i: (0, i, 0))
    hprev, hnext = _halo_specs(t, s, CONV_W)
    return pl.pallas_call(
        body, name="bwd_out", grid=(s // t,),
        in_specs=[row(D_MODEL), row(D_MODEL), row(D_IN), heads, row(CONV_W), hprev, hnext,
                  _const_spec((32, CONV_W)), _const_spec((8, CONV_W)), _const_spec((8, SG_W)),
                  _const_spec((4, SG_CHUNK, SG_CHUNK)), _const_spec((4, SG_CHUNK, SG_CHUNK)),
                  _const_spec((SG_CHUNK, SG_W)), _const_spec((D_MODEL, D_MODEL)), _const_spec((1, D_MODEL))],
        out_specs=[heads, row(1536), row(CONV_W), _const_spec((D_MODEL, D_MODEL)), _const_spec((1, D_MODEL)),
                   _const_spec((32, CONV_W)), _const_spec((8, CONV_W)), _const_spec((4, SG_CHUNK, SG_CHUNK)),
                   _const_spec((SG_CHUNK, SG_W))],
        out_shape=[jax.ShapeDtypeStruct((N_HEADS, s, LANES), BF16), jax.ShapeDtypeStruct((s, 1536), F32),
                   jax.ShapeDtypeStruct((s, CONV_W), F32), jax.ShapeDtypeStruct((D_MODEL, D_MODEL), F32),
                   jax.ShapeDtypeStruct((1, D_MODEL), F32), jax.ShapeDtypeStruct((32, CONV_W), F32),
                   jax.ShapeDtypeStruct((8, CONV_W), F32), jax.ShapeDtypeStruct((4, SG_CHUNK, SG_CHUNK), F32),
                   jax.ShapeDtypeStruct((SG_CHUNK, SG_W), F32)],
        scratch_shapes=[pltpu.VMEM((SUBLANES, t + 2 * HALO, CONV_W), F32)],
        compiler_params=_arb(),
    )(dy, mix, proj, o, hc, hc, hc, cw, cp, sp, sgw_bf, sgwt_bf, sgb, w_out_t_bf, g_post)


def _flash_bwd(qa, doa, k, v, tq, tk):
    s = qa.shape[1]
    rows = GROUP * tq
    nk = s // tk
    n_q = s // tq
    per_trip = _chunks_per_trip(nk, 2)

    def body(qa_ref, do_ref, k_ref, v_ref, dq_ref, dk_hbm, dv_hbm,
             dq_scr, dk_scr, dv_scr, s0, s1, d0, d1, p0, p1, e0, e1, sems):
        j, i = pl.program_id(0), pl.program_id(1)
        s_bufs, d_bufs, p_bufs, e_bufs = (s0, s1), (d0, d1), (p0, p1), (e0, e1)
        qv = qa_ref[...].reshape(rows, LANES)
        dov = do_ref[...].reshape(rows, LANES)
        dq_scr[...] = jnp.zeros((rows, LANES), F32)
        p1[...] = jnp.zeros((rows, tk), BF16)
        e1[...] = jnp.zeros((rows, tk), BF16)

        @pl.when(i == 0)
        def _():
            dk_scr[...] = jnp.zeros_like(dk_scr)
            dv_scr[...] = jnp.zeros_like(dv_scr)

        def at(c):
            return pl.ds(pl.multiple_of(c * tk, tk), tk)

        def scores(c, slot):
            s_bufs[slot][...] = _dot_nt(qv, k_ref[at(c), :])
            d_bufs[slot][...] = _dot_nt(dov, v_ref[at(c), :])

        def probs(slot):
            for h in range(GROUP):
                r = slice(h * tq, (h + 1) * tq)
                p = jnp.exp(s_bufs[slot][r, :])
                p_bufs[slot][r, :] = p.astype(BF16)
                e_bufs[slot][r, :] = (p * d_bufs[slot][r, :]).astype(BF16)

        def grads(c, slot):
            ds = e_bufs[slot][...]
            dq_scr[...] += _dot(ds, k_ref[at(c), :])
            dv_scr[at(c), :] += _dot_tn(p_bufs[slot][...], dov)
            dk_scr[at(c), :] += _dot_tn(ds, qv)

        scores(0, 0)

        def trip(t, carry):
            for i in range(per_trip):
                c, slot = per_trip * t + i, i % 2
                scores(jnp.minimum(c + 1, nk - 1), 1 - slot)
                probs(slot)
                grads(jnp.maximum(c - 1, 0), 1 - slot)
            return carry

        lax.fori_loop(0, nk // per_trip, trip, 0)
        grads(nk - 1, 1)
        dq_ref[...] = dq_scr[...].reshape(GROUP, tq, LANES)

        @pl.when(i == n_q - 1)
        def _():
            out = [pltpu.make_async_copy(dk_scr, dk_hbm.at[j], sems.at[0]),
                   pltpu.make_async_copy(dv_scr, dv_hbm.at[j], sems.at[1])]
            for cp in out:
                cp.start()
            for cp in out:
                cp.wait()

    qspec = pl.BlockSpec((GROUP, tq, LANES), lambda j, i: (j, i, 0))
    kvspec = pl.BlockSpec((None, s, LANES), lambda j, i: (j, 0, 0))
    hbm = pl.BlockSpec(memory_space=pl.ANY)
    stage_f32 = pltpu.VMEM((rows, tk), F32)
    stage_bf = pltpu.VMEM((rows, tk), BF16)
    return pl.pallas_call(
        body, name="flash_bwd", grid=(N_KV, n_q),
        in_specs=[qspec, qspec, kvspec, kvspec],
        out_specs=[qspec, hbm, hbm],
        out_shape=[jax.ShapeDtypeStruct((N_HEADS, s, LANES), F32), jax.ShapeDtypeStruct((N_KV, s, LANES), F32),
                   jax.ShapeDtypeStruct((N_KV, s, LANES), F32)],
        scratch_shapes=[pltpu.VMEM((rows, LANES), F32), pltpu.VMEM((s, LANES), F32), pltpu.VMEM((s, LANES), F32),
                        stage_f32, stage_f32, stage_f32, stage_f32, stage_bf, stage_bf, stage_bf, stage_bf,
                        pltpu.SemaphoreType.DMA((2,))],
        compiler_params=_arb(2),
    )(qa, doa, k, v)


def _bwd_in(dy, x, proj, dq, dk, dv, dgs, dc0, cw, g_pre, w_in_t_bf, bd, qg, kg, cos, sin, t):
    s = x.shape[0]

    def body(dy_ref, x_ref, proj_ref, dq_ref, dk_ref, dv_ref, dgs_ref, dc_ref, dcp_ref, dcn_ref, cw_ref, g_ref,
             wt_ref, bd_ref, qg_ref, kg_ref, cos_ref, sin_ref,
             dx_ref, dproj_ref, h_ref, gpre_ref, gq_ref, gk_ref, dext_ref):
        i = pl.program_id(0)

        @pl.when(i == 0)
        def _():
            for ref in (gpre_ref, gq_ref, gk_ref):
                ref[...] = jnp.zeros_like(ref)

        proj = proj_ref[...]
        cos_pair = cos_ref[...]
        sin_pair = sin_ref[...]
        cosv = jnp.concatenate([cos_pair] * (N_HEADS // 2), axis=1)
        sinv = jnp.concatenate([sin_pair] * (N_HEADS // 2), axis=1)

        def head_norm_bwd(dr, z, bdm, g, cs, sn, gacc_ref):
            dn = dr * cs + _swap16(dr * sn)
            rr = lax.rsqrt(_group_mean(z * z, bdm) + EPS)
            gdn = dn * g
            gacc_ref[...] += jnp.sum(dn * z * rr, axis=0, keepdims=True)
            return rr * gdn - z * (rr * rr * rr * _group_mean(gdn * z, bdm))

        dq_cat = _heads_to_cat(dq_ref, N_HEADS) * ATT_SCALE
        dzq = head_norm_bwd(dq_cat, proj[:, C_Q:C_K], bd_ref[...], qg_ref[...], cosv, sinv, gq_ref)
        dk_cat = _heads_to_cat(dk_ref, N_KV)
        dzk = head_norm_bwd(dk_cat, proj[:, C_K:C_V], bd_ref[0:LANES, 0:LANES], kg_ref[...],
                            cos_pair, sin_pair, gk_ref)
        dv_cat = _heads_to_cat(dv_ref, N_KV)

        _conv_window(dext_ref, dcp_ref, dc_ref[...], dcn_ref, i == 0, i == pl.num_programs(0) - 1, t)
        dhc = jnp.zeros((t, CONV_W), F32)
        for kk in range(CONV_K):
            dhc = dhc + cw_ref[kk:kk + 1, :] * _tap(dext_ref, CONV_K - kk, t)
        sg = _sigmoid(proj[:, C_A2:C_GC])
        da1 = dhc * sg
        da2 = dhc * proj[:, C_A1:C_A2] * sg * (1.0 - sg)

        dgs = dgs_ref[...]
        dproj_bf = jnp.concatenate([dzq, dzk, dv_cat, dgs[:, 0:512], da1, da2, dgs[:, 512:1536]], axis=1).astype(BF16)
        dproj_ref[...] = dproj_bf
        dh = _dot(dproj_bf, wt_ref[...])

        xv = x_ref[...]
        rr = _rms(xv)
        gv = g_ref[...]
        h_ref[...] = (xv * rr * gv).astype(BF16)
        gdh = dh * gv
        gpre_ref[...] += jnp.sum(dh * xv * rr, axis=0, keepdims=True)
        dx_ref[...] = dy_ref[...] + rr * gdh - xv * (rr * rr * rr * jnp.mean(gdh * xv, axis=-1, keepdims=True))

    row = lambda w: pl.BlockSpec((t, w), lambda i: (i, 0))
    heads = lambda n: pl.BlockSpec((n, t, LANES), lambda i: (0, i, 0))
    hprev, hnext = _halo_specs(t, s, CONV_W)
    return pl.pallas_call(
        body, name="bwd_in", grid=(s // t,),
        in_specs=[row(D_MODEL), row(D_MODEL), row(D_IN), heads(N_HEADS), heads(N_KV), heads(N_KV), row(1536),
                  row(CONV_W), hprev, hnext, _const_spec((32, CONV_W)), _const_spec((1, D_MODEL)),
                  _const_spec((D_IN, D_MODEL)), _const_spec((512, 512)), _const_spec((1, 512)),
                  _const_spec((1, LANES)), row(LANES), row(LANES)],
        out_specs=[row(D_MODEL), row(D_IN), row(D_MODEL), _const_spec((1, D_MODEL)), _const_spec((1, 512)),
                   _const_spec((1, LANES))],
        out_shape=[jax.ShapeDtypeStruct((s, D_MODEL), F32), jax.ShapeDtypeStruct((s, D_IN), BF16),
                   jax.ShapeDtypeStruct((s, D_MODEL), BF16), jax.ShapeDtypeStruct((1, D_MODEL), F32),
                   jax.ShapeDtypeStruct((1, 512), F32), jax.ShapeDtypeStruct((1, LANES), F32)],
        scratch_shapes=[pltpu.VMEM((SUBLANES, t + 2 * HALO, CONV_W), F32)],
        compiler_params=_arb(),
    )(dy, x, proj, dq, dk, dv, dgs, dc0, dc0, dc0, cw, g_pre, w_in_t_bf, bd, qg, kg, cos, sin)


def _grad_w_in(h_bf, dproj_bf, t):
    s = h_bf.shape[0]
    half = D_IN // 2

    def body(h_ref, d_ref, g_ref):
        @pl.when(pl.program_id(1) == 0)
        def _():
            g_ref[...] = jnp.zeros_like(g_ref)

        g_ref[...] += _dot_tn(h_ref[...], d_ref[...])

    return pl.pallas_call(
        body, name="grad_w_in", grid=(2, s // t),
        in_specs=[pl.BlockSpec((t, D_MODEL), lambda j, i: (i, 0)), pl.BlockSpec((t, half), lambda j, i: (i, j))],
        out_specs=pl.BlockSpec((D_MODEL, half), lambda j, i: (0, j)),
        out_shape=jax.ShapeDtypeStruct((D_MODEL, D_IN), F32),
        compiler_params=_arb(2),
    )(h_bf, dproj_bf)


def _place():
    x, y, c = lax.axis_index("x"), lax.axis_index("y"), lax.axis_index("c")
    chips = [(1 - x, y), (x, 1 - y), (1 - x, 1 - y)]
    return x, y, c, chips


def _any_specs(n):
    return [pl.BlockSpec(memory_space=pl.ANY)] * n


def _gather_weights(w_in_bf, w_out_bf, cdw):
    arrs = (w_in_bf, w_out_bf, cdw)
    n = len(arrs)

    def body(*refs):
        ins, outs = refs[:n], refs[n:2 * n]
        send_sems, recv_sems, local_sems = refs[2 * n:]
        x, y, c, chips = _place()
        mine = 2 * x + y
        local = [pltpu.make_async_copy(ins[a], outs[a].at[mine], local_sems.at[a]) for a in range(n)]
        for cp in local:
            cp.start()

        def copy(j, a, slot, to):
            return pltpu.make_async_remote_copy(src_ref=ins[a], dst_ref=outs[a].at[slot], send_sem=send_sems.at[n * j + a],
                                                recv_sem=recv_sems.at[n * j + a], device_id=to, device_id_type=MESH)

        sends = [copy(j, a, mine, (px, py, c)) for j, (px, py) in enumerate(chips) for a in range(n)]
        for cp in sends:
            cp.start()
        for j, (px, py) in enumerate(chips):
            for a in range(n):
                copy(j, a, 2 * px + py, (px, py, c)).wait_recv()
        for cp in sends:
            cp.wait_send()
        for cp in local:
            cp.wait()

    return pl.pallas_call(
        body, name="gather_weights",
        in_specs=_any_specs(n), out_specs=_any_specs(n),
        out_shape=[jax.ShapeDtypeStruct((N_CHIPS,) + a.shape, a.dtype) for a in arrs],
        scratch_shapes=[pltpu.SemaphoreType.DMA((3 * n,)), pltpu.SemaphoreType.DMA((3 * n,)), pltpu.SemaphoreType.DMA((n,))],
    )(*arrs)


def _scatter_grads(gin_pieces, gw_out):
    depth = gw_out.shape[0]
    rows = gw_out.shape[1] // N_CHIPS

    def body(gin_ref, gout_ref, rin_ref, rout_ref, send_sems, recv_sems, local_sems):
        x, y, c, chips = _place()
        mine = 2 * x + y

        def out_rows(chip):
            return gout_ref.at[:, pl.ds(pl.multiple_of(chip * rows, rows), rows), :]

        local = [pltpu.make_async_copy(gin_ref.at[mine], rin_ref.at[mine], local_sems.at[0]),
                 pltpu.make_async_copy(out_rows(mine), rout_ref.at[mine], local_sems.at[1])]
        for cp in local:
            cp.start()

        def copies(j, shard, slot, to):
            kw = dict(device_id=to, device_id_type=MESH)
            return [pltpu.make_async_remote_copy(src_ref=gin_ref.at[shard], dst_ref=rin_ref.at[slot],
                                                 send_sem=send_sems.at[2 * j], recv_sem=recv_sems.at[2 * j], **kw),
                    pltpu.make_async_remote_copy(src_ref=out_rows(shard), dst_ref=rout_ref.at[slot],
                                                 send_sem=send_sems.at[2 * j + 1], recv_sem=recv_sems.at[2 * j + 1], **kw)]

        sends = [cp for j, (px, py) in enumerate(chips) for cp in copies(j, 2 * px + py, mine, (px, py, c))]
        for cp in sends:
            cp.start()
        for j, (px, py) in enumerate(chips):
            for cp in copies(j, mine, 2 * px + py, (px, py, c)):
                cp.wait_recv()
        for cp in sends:
            cp.wait_send()
        for cp in local:
            cp.wait()

    return pl.pallas_call(
        body, name="scatter_grads",
        in_specs=_any_specs(2), out_specs=_any_specs(2),
        out_shape=[jax.ShapeDtypeStruct(gin_pieces.shape, F32),
                   jax.ShapeDtypeStruct((N_CHIPS, depth, rows, gw_out.shape[2]), F32)],
        scratch_shapes=[pltpu.SemaphoreType.DMA((6,)), pltpu.SemaphoreType.DMA((6,)), pltpu.SemaphoreType.DMA((2,))],
    )(gin_pieces, gw_out)


def _sum_chips(parts, rb, name):
    _, depth, r, cdim = parts.shape

    def body(p_ref, o_ref):
        o_ref[...] = ((p_ref[0] + p_ref[1]) + p_ref[2]) + p_ref[3]

    return pl.pallas_call(
        body, name=name, grid=(depth, r // rb),
        in_specs=[pl.BlockSpec((N_CHIPS, None, rb, cdim), lambda l, i: (0, l, i, 0))],
        out_specs=pl.BlockSpec((None, rb, cdim), lambda l, i: (l, i, 0)),
        out_shape=jax.ShapeDtypeStruct((depth, r, cdim), F32),
        compiler_params=_arb(2),
    )(parts)


def _swap_with_sibling(a, b):
    arrs = (a, b)
    n = len(arrs)

    def body(*refs):
        ins, outs = refs[:n], refs[n:2 * n]
        send_sems, recv_sems = refs[2 * n:]
        x, y, c, _ = _place()
        cps = [pltpu.make_async_remote_copy(src_ref=ins[k], dst_ref=outs[k], send_sem=send_sems.at[k],
                                            recv_sem=recv_sems.at[k], device_id=(x, y, 1 - c), device_id_type=MESH)
               for k in range(n)]
        for cp in cps:
            cp.start()
        for cp in cps:
            cp.wait()

    return pl.pallas_call(
        body, name="swap_with_sibling",
        in_specs=_any_specs(n), out_specs=_any_specs(n),
        out_shape=[jax.ShapeDtypeStruct(v.shape, v.dtype) for v in arrs],
        scratch_shapes=[pltpu.SemaphoreType.DMA((n,)), pltpu.SemaphoreType.DMA((n,))],
    )(*arrs)


def _allreduce_small(slab):
    m, n = slab.shape

    def body(x_ref, out_ref, gath, send_sems, recv_sems, local_sem):
        x, y, c, chips = _place()
        me, sibling = (x, y, c), (x, y, 1 - c)

        def rows(px, py, pc):
            return gath.at[pl.ds(pl.multiple_of((4 * px + 2 * py + pc) * m, 8), m), :]

        def copy(k, block, to, src=None):
            return pltpu.make_async_remote_copy(src_ref=rows(*block) if src is None else src, dst_ref=rows(*block),
                                                send_sem=send_sems.at[k], recv_sem=recv_sems.at[k],
                                                device_id=to, device_id_type=MESH)

        mine = pltpu.make_async_copy(x_ref, rows(*me), local_sem)
        mine.start()
        first = [copy(0, me, sibling, src=x_ref)]
        first += [copy(1 + j, me, (*chip, c), src=x_ref) for j, chip in enumerate(chips)]
        for cp in first:
            cp.start()
        passed = [copy(4 + j, (*chip, c), sibling) for j, chip in enumerate(chips)]
        for j, chip in enumerate(chips):
            copy(1 + j, (*chip, c), me).wait_recv()
            passed[j].start()
        copy(0, sibling, me).wait_recv()
        for j, chip in enumerate(chips):
            copy(4 + j, (*chip, 1 - c), me).wait_recv()
        for cp in first + passed:
            cp.wait_send()
        mine.wait()
        total = gath[0:m, :]
        for d in range(1, N_DEV):
            total = total + gath[d * m:(d + 1) * m, :]
        out_ref[...] = total

    return pl.pallas_call(
        body, name="allreduce_small",
        in_specs=[pl.BlockSpec(memory_space=pltpu.VMEM)],
        out_specs=pl.BlockSpec(memory_space=pltpu.VMEM),
        out_shape=jax.ShapeDtypeStruct((m, n), F32),
        scratch_shapes=[pltpu.VMEM((N_DEV * m, n), F32), pltpu.SemaphoreType.DMA((7,)), pltpu.SemaphoreType.DMA((7,)),
                        pltpu.SemaphoreType.DMA],
    )(slab)


def _adamw(w, ga, gb, m, v, rb, name):
    depth, r, cdim = w.shape

    def body(w_ref, ga_ref, gb_ref, m_ref, v_ref, g_out, d_out, m_out, v_out):
        g = ga_ref[...] + gb_ref[...]
        m2 = ADAM_B1 * m_ref[...] + (1.0 - ADAM_B1) * g
        v2 = ADAM_B2 * v_ref[...] + (1.0 - ADAM_B2) * (g * g)
        m_hat = m2 / (1.0 - ADAM_B1 ** ADAM_STEP)
        v_hat = v2 / (1.0 - ADAM_B2 ** ADAM_STEP)
        g_out[...] = g
        d_out[...] = -ADAM_LR * (m_hat / (jnp.sqrt(v_hat) + ADAM_EPS) + ADAM_WD * w_ref[...])
        m_out[...] = m2
        v_out[...] = v2

    spec = pl.BlockSpec((None, rb, cdim), lambda l, i: (l, i, 0))
    shp = jax.ShapeDtypeStruct((depth, r, cdim), F32)
    return pl.pallas_call(
        body, name=name, grid=(depth, r // rb),
        in_specs=[spec] * 5, out_specs=[spec] * 4, out_shape=[shp] * 4,
        compiler_params=_arb(2),
    )(w, ga, gb, m, v)


def _rope_tables(s):
    t = jnp.arange(s, dtype=jnp.int32)
    row = (t // GRID_W).astype(F32)
    col = (t % GRID_W).astype(F32)
    half = HEAD_DIM // 4
    inv_freq = ROPE_THETA ** (-jnp.arange(half, dtype=F32) / half)
    ar = row[:, None] * inv_freq[None, :]
    ac = col[:, None] * inv_freq[None, :]
    cos = jnp.concatenate([jnp.cos(ar), jnp.cos(ar), jnp.cos(ac), jnp.cos(ac)], axis=1)
    sin = jnp.concatenate([-jnp.sin(ar), jnp.sin(ar), -jnp.sin(ac), jnp.sin(ac)], axis=1)
    return jnp.tile(cos, (1, 2)), jnp.tile(sin, (1, 2))


def _pad_rows(a, rows):
    return jnp.concatenate([a, jnp.zeros((rows - a.shape[0],) + a.shape[1:], a.dtype)], axis=0)


_SMALL = ("pre_norm", "post_norm", "q_norm", "k_norm", "conv_dw_b", "conv_ln_g", "conv_ln_b", "sg_ln_g", "sg_ln_b",
          "sg_w", "sg_b")


def _pack(parts):
    flat = jnp.concatenate([p.reshape(-1, LANES) for p in parts], axis=0)
    return _pad_rows(flat, -(-flat.shape[0] // 8) * 8)


def _unpack(slab, shapes):
    out, r = [], 0
    for shp in shapes:
        n = 1
        for d in shp:
            n *= d
        out.append(slab[r:r + n // LANES].reshape(shp))
        r += n // LANES
    return out


def kernel(x, pre_norm, post_norm, w_in, w_out, q_norm, k_norm, conv_dw, conv_dw_b, conv_ln_g, conv_ln_b, sg_ln_g, sg_ln_b, sg_w, sg_b, loss_target, m_pre_norm, m_post_norm, m_w_in, m_w_out, m_q_norm, m_k_norm, m_conv_dw, m_conv_dw_b, m_conv_ln_g, m_conv_ln_b, m_sg_ln_g, m_sg_ln_b, m_sg_w, m_sg_b, v_pre_norm, v_post_norm, v_w_in, v_w_out, v_q_norm, v_k_norm, v_conv_dw, v_conv_dw_b, v_conv_ln_g, v_conv_ln_b, v_sg_ln_g, v_sg_ln_b, v_sg_w, v_sg_b):
    depth = w_in.shape[0]
    s = x.shape[1]
    assert x.shape[0] == 1 and s % SG_CHUNK == 0 and x.shape[2] == D_MODEL
    t = min(256, s)
    tq = min(256, s)
    tk = min(512, s // 2)
    shard_cols = w_in.shape[2]
    chip = 2 * lax.axis_index("x") + lax.axis_index("y")

    gin, gout, gcdw = _gather_weights(w_in.astype(BF16), w_out.astype(BF16), conv_dw)
    w_in_bf = jnp.concatenate([gin[j] for j in range(N_CHIPS)], axis=2)
    w_out_bf = jnp.concatenate([gout[j] for j in range(N_CHIPS)], axis=1)
    cdw_full = jnp.concatenate([gcdw[j] for j in range(N_CHIPS)], axis=2)
    w_in_t_bf = jnp.swapaxes(w_in_bf, 1, 2)
    w_out_t_bf = jnp.swapaxes(w_out_bf, 1, 2)
    sgw_bf = sg_w.astype(BF16)
    sgwt_bf = jnp.swapaxes(sg_w, 2, 3).astype(BF16)

    cos, sin = _rope_tables(s)
    bd = jnp.kron(jnp.eye(N_HEADS, dtype=F32), jnp.full((HEAD_DIM, HEAD_DIM), 1.0 / HEAD_DIM, F32)).astype(BF16)

    def layer_consts(l):
        cw = _pad_rows(cdw_full[l], 32)
        cp = _pad_rows(jnp.stack([conv_dw_b[l], conv_ln_g[l], conv_ln_b[l]]), 8)
        sp = _pad_rows(jnp.stack([sg_ln_g[l], sg_ln_b[l]]), 8)
        sgb = jnp.repeat(sg_b[l].T, HEAD_DIM, axis=1)
        qg = jnp.tile(q_norm[l], N_HEADS)[None, :]
        kg = jnp.tile(k_norm[l], N_KV)[None, :]
        return cw, cp, sp, sgb, qg, kg

    xs = [x[0]]
    saved = []
    for l in range(depth):
        cw, cp, sp, sgb, qg, kg = layer_consts(l)
        proj, hc, q, k, v = _fwd_in(xs[l], pre_norm[l][None, :], w_in_bf[l], bd, qg, kg, cos, sin, t)
        o, qa = _flash_fwd(q, k, v, tq, tk)
        mix, xn = _fwd_out(xs[l], proj, o, hc, cw, cp, sp, sgw_bf[l], sgb, w_out_bf[l], post_norm[l][None, :], t)
        saved.append((proj, hc, qa, k, v, o, mix))
        xs.append(xn)

    dy, sq = _loss_grad(xs[depth], loss_target[0], t)
    loss = lax.psum(0.5 * jnp.sum(sq) / D_MODEL, ("x", "y", "c"))

    g_w_in, g_w_out, g_small = [], [], {n: [] for n in _SMALL + ("conv_dw",)}
    for l in reversed(range(depth)):
        cw, cp, sp, sgb, qg, kg = layer_consts(l)
        proj, hc, qa, k, v, o, mix = saved[l]
        doa, dgs, dc0, gwo, gpost, gcw, gvec, gsgw, gsgb = _bwd_out(
            dy, mix, proj, o, hc, cw, cp, sp, sgw_bf[l], sgwt_bf[l], sgb, w_out_t_bf[l], post_norm[l][None, :], t)
        dq, dk, dv = _flash_bwd(qa, doa, k, v, tq, tk)
        dy, dproj_bf, h_bf, gpre, gq, gk = _bwd_in(dy, xs[l], proj, dq, dk, dv, dgs, dc0, cw, pre_norm[l][None, :],
                                                  w_in_t_bf[l], bd, qg, kg, cos, sin, t)
        g_w_in.append(_grad_w_in(h_bf, dproj_bf, min(512, s)))
        g_w_out.append(gwo)
        g_small["pre_norm"].append(gpre[0])
        g_small["post_norm"].append(gpost[0])
        g_small["q_norm"].append(gq[0].reshape(N_HEADS, HEAD_DIM).sum(0))
        g_small["k_norm"].append(gk[0].reshape(N_KV, HEAD_DIM).sum(0))
        g_small["conv_dw"].append(gcw[:CONV_K])
        g_small["conv_dw_b"].append(gvec[0])
        g_small["conv_ln_g"].append(gvec[1])
        g_small["conv_ln_b"].append(gvec[2])
        g_small["sg_ln_g"].append(gvec[3])
        g_small["sg_ln_b"].append(gvec[4])
        g_small["sg_w"].append(gsgw)
        g_small["sg_b"].append(gsgb.reshape(SG_CHUNK, SG_W // HEAD_DIM, HEAD_DIM).sum(-1).T)
    grad_x = dy[None]
    g_w_in = jnp.stack(g_w_in[::-1])
    g_w_out = jnp.stack(g_w_out[::-1])
    g_small = {n: jnp.stack(vals[::-1]) for n, vals in g_small.items()}

    gin_pieces = jnp.stack([g_w_in[:, :, j * shard_cols:(j + 1) * shard_cols] for j in range(N_CHIPS)])
    rin, rout = _scatter_grads(gin_pieces, g_w_out)
    s_in = _sum_chips(rin, 256, "sum_chips_w_in")
    s_out = _sum_chips(rout, 256, "sum_chips_w_out")
    t_in, t_out = _swap_with_sibling(s_in, s_out)
    grad_w_in, delta_w_in, new_m_w_in, new_v_w_in = _adamw(w_in, s_in, t_in, m_w_in, v_w_in, 256, "adamw_w_in")
    grad_w_out, delta_w_out, new_m_w_out, new_v_w_out = _adamw(w_out, s_out, t_out, m_w_out, v_w_out, 256, "adamw_w_out")

    small_w = dict(pre_norm=pre_norm, post_norm=post_norm, q_norm=q_norm, k_norm=k_norm, conv_dw_b=conv_dw_b,
                   conv_ln_g=conv_ln_g, conv_ln_b=conv_ln_b, sg_ln_g=sg_ln_g, sg_ln_b=sg_ln_b, sg_w=sg_w, sg_b=sg_b)
    small_m = dict(pre_norm=m_pre_norm, post_norm=m_post_norm, q_norm=m_q_norm, k_norm=m_k_norm, conv_dw_b=m_conv_dw_b,
                   conv_ln_g=m_conv_ln_g, conv_ln_b=m_conv_ln_b, sg_ln_g=m_sg_ln_g, sg_ln_b=m_sg_ln_b, sg_w=m_sg_w,
                   sg_b=m_sg_b)
    small_v = dict(pre_norm=v_pre_norm, post_norm=v_post_norm, q_norm=v_q_norm, k_norm=v_k_norm, conv_dw_b=v_conv_dw_b,
                   conv_ln_g=v_conv_ln_g, conv_ln_b=v_conv_ln_b, sg_ln_g=v_sg_ln_g, sg_ln_b=v_sg_ln_b, sg_w=v_sg_w,
                   sg_b=v_sg_b)
    shapes = [small_w[n].shape for n in _SMALL]
    red = _allreduce_small(_pack([g_small[n] for n in _SMALL] + [g_small["conv_dw"]]))
    n_rep = sum(small_w[n].size for n in _SMALL) // LANES
    g_cdw_full = red[n_rep:n_rep + g_small["conv_dw"].size // LANES].reshape(g_small["conv_dw"].shape)
    cdw_cols = conv_dw.shape[2]
    g_cdw = lax.dynamic_slice_in_dim(g_cdw_full, chip * cdw_cols, cdw_cols, axis=2)
    g_slab = _pack([red[:n_rep], g_cdw])
    w_slab = _pack([small_w[n] for n in _SMALL] + [conv_dw])
    m_slab = _pack([small_m[n] for n in _SMALL] + [m_conv_dw])
    v_slab = _pack([small_v[n] for n in _SMALL] + [v_conv_dw])
    rows = w_slab.shape[0]
    outs = _adamw(w_slab[None], g_slab[None], jnp.zeros_like(g_slab)[None], m_slab[None], v_slab[None], rows, "adamw_small")
    unpacked = [dict(zip(_SMALL + ("conv_dw",), _unpack(o_[0], shapes + [conv_dw.shape]))) for o_ in outs]

    big = [dict(w_in=a, w_out=b) for a, b in ((grad_w_in, grad_w_out), (delta_w_in, delta_w_out),
                                             (new_m_w_in, new_m_w_out), (new_v_w_in, new_v_w_out))]
    order = ("pre_norm", "post_norm", "w_in", "w_out", "q_norm", "k_norm", "conv_dw", "conv_dw_b", "conv_ln_g",
             "conv_ln_b", "sg_ln_g", "sg_ln_b", "sg_w", "sg_b")
    result = [loss, grad_x]
    for kind in range(4):
        for name in order:
            result.append(big[kind][name] if name in big[kind] else unpacked[kind][name])
    return tuple(result)
```

```python
import functools

import jax
import jax.numpy as jnp
from jax import lax
from jax.experimental import pallas as pl
from jax.experimental.pallas import tpu as pltpu

F32 = jnp.float32
BF16 = jnp.bfloat16
MESH = pl.DeviceIdType.MESH

EPS = 1e-6
D_MODEL = 1024
D_IN = 2816
HEAD_DIM = 64
LANES = 128
SUBLANES = 8
N_HEADS = 8
N_KV = 2
GROUP = N_HEADS // N_KV
GRID_W = 64
ROPE_THETA = 10000.0
CONV_K = 31
CONV_W = 256
SG_W = 256
SG_CHUNK = 128
HALO = 16
ATT_SCALE = HEAD_DIM ** -0.5

C_Q, C_K, C_V, C_GA, C_A1, C_A2, C_GC, C_U, C_VS, C_GS = 0, 512, 640, 768, 1280, 1536, 1792, 2048, 2304, 2560

ADAM_LR = 0.001
ADAM_B1 = 0.9
ADAM_B2 = 0.999
ADAM_EPS = 1e-08
ADAM_WD = 0.01
ADAM_STEP = 10

N_CHIPS = 4
N_DEV = 8


def _dot(a, b):
    return jnp.dot(a, b, preferred_element_type=F32)


def _group_mean(x, bd_bf):
    hi = x.astype(BF16)
    lo = (x - hi.astype(F32)).astype(BF16)
    return _dot(hi, bd_bf) + _dot(lo, bd_bf)


def _dot_nt(a, b):
    return lax.dot_general(a, b, (((1,), (1,)), ((), ())), preferred_element_type=F32)


def _dot_tn(a, b):
    return lax.dot_general(a, b, (((0,), (0,)), ((), ())), preferred_element_type=F32)


def _lane(shape):
    return lax.broadcasted_iota(jnp.int32, shape, 1)


def _sigmoid(x):
    return 1.0 / (1.0 + jnp.exp(-x))


def _silu_fwd_bwd(x):
    s = _sigmoid(x)
    return x * s, s * (1.0 + x * (1.0 - s))


def _erf(x):
    x = jnp.clip(x, -4.0, 4.0)
    x2 = x * x
    a = -2.72614225801306e-10
    a = a * x2 + 2.77068142495902e-08
    a = a * x2 + -2.10102402082508e-06
    a = a * x2 + -5.69250639462346e-05
    a = a * x2 + -7.34990630326855e-04
    a = a * x2 + -2.95459980854025e-03
    a = a * x2 + -1.60960333262415e-02
    b = -1.45660718464996e-05
    b = b * x2 + -2.13374055278905e-04
    b = b * x2 + -1.68282697438203e-03
    b = b * x2 + -7.37332916720468e-03
    b = b * x2 + -1.42647390514189e-02
    return x * a / b


def _gelu_fwd_bwd(x):
    cdf = 0.5 * (1.0 + _erf(x * 0.7071067811865476))
    pdf = jnp.exp(-0.5 * x * x) * 0.3989422804014327
    return x * cdf, cdf + x * pdf


def _rms(x):
    return lax.rsqrt(jnp.mean(x * x, axis=-1, keepdims=True) + EPS)


def _ln_hat(x):
    mu = jnp.mean(x, axis=-1, keepdims=True)
    xc = x - mu
    rs = lax.rsqrt(jnp.mean(xc * xc, axis=-1, keepdims=True) + EPS)
    return xc * rs, rs


def _ln_bwd(dxh, xh, rs):
    return rs * (dxh - jnp.mean(dxh, axis=-1, keepdims=True) - xh * jnp.mean(dxh * xh, axis=-1, keepdims=True))


def _swap16(z):
    parts = []
    for i in range(z.shape[1] // LANES):
        blk = z[:, i * LANES:(i + 1) * LANES]
        lane = _lane(blk.shape)
        parts.append(jnp.where((lane & 16) == 0, pltpu.roll(blk, LANES - 16, 1), pltpu.roll(blk, 16, 1)))
    return parts[0] if len(parts) == 1 else jnp.concatenate(parts, axis=1)


def _head_slab(pair, odd):
    src = pltpu.roll(pair, HEAD_DIM, 1) if odd else pair
    return jnp.where(_lane(pair.shape) < HEAD_DIM, src, 0.0)


def _pair_merge(even, odd):
    return jnp.where(_lane(even.shape) < HEAD_DIM, even, pltpu.roll(odd, HEAD_DIM, 1))


def _heads_to_cat(ref, n_heads):
    pairs = [_pair_merge(ref[2 * p], ref[2 * p + 1]) for p in range(n_heads // 2)]
    return pairs[0] if len(pairs) == 1 else jnp.concatenate(pairs, axis=1)


def _split3(x):
    hi = x.astype(BF16).astype(F32)
    r = x - hi
    mid = r.astype(BF16).astype(F32)
    lo = (r - mid).astype(BF16).astype(F32)
    return hi, mid, lo


def _with_spare(slab, hi, mid, lo):
    lane = _lane(slab.shape)
    return jnp.where(lane == HEAD_DIM, hi, jnp.where(lane == HEAD_DIM + 1, mid, jnp.where(lane == HEAD_DIM + 2, lo, slab)))


def _with_ones(slab):
    lane = _lane(slab.shape)
    return jnp.where((lane >= HEAD_DIM) & (lane < HEAD_DIM + 3), 1.0, slab)


def _conv_window(rot_ref, prev_ref, main, next_ref, first, last, t):
    n = t + 2 * HALO
    full = jnp.concatenate([jnp.where(first, 0.0, prev_ref[...]), main, jnp.where(last, 0.0, next_ref[...])], axis=0)
    rot_ref[0] = full
    for b in range(1, SUBLANES):
        rot_ref[b] = pltpu.roll(full, n - b, 0)


def _tap(rot_ref, start, t):
    a, b = divmod(start, SUBLANES)
    return rot_ref[b, SUBLANES * a:SUBLANES * a + t, :]


def _sgu_mix(v1_bf, w_ref, n_chunks):
    rows = []
    for n in range(n_chunks):
        pairs = []
        for p in range(SG_W // LANES):
            xp = v1_bf[n * SG_CHUNK:(n + 1) * SG_CHUNK, p * LANES:(p + 1) * LANES]
            me = _dot(w_ref[2 * p], xp)
            mo = _dot(w_ref[2 * p + 1], xp)
            pairs.append(jnp.where(_lane(me.shape) < HEAD_DIM, me, mo))
        rows.append(jnp.concatenate(pairs, axis=1))
    return rows[0] if len(rows) == 1 else jnp.concatenate(rows, axis=0)


def _halo_specs(t, s, width):
    per = t // HALO
    nblk = s // HALO
    prev = pl.BlockSpec((HALO, width), lambda i: (jnp.maximum(i * per - 1, 0), 0))
    nxt = pl.BlockSpec((HALO, width), lambda i: (jnp.minimum((i + 1) * per, nblk - 1), 0))
    return prev, nxt


def _const_spec(shape):
    nd = len(shape)
    return pl.BlockSpec(shape, lambda i: (0,) * nd)


def _arb(n=1):
    return pltpu.CompilerParams(dimension_semantics=("arbitrary",) * n)


def _fwd_in(x, g_pre, w_in_bf, bd, qg, kg, cos, sin, t):
    s = x.shape[0]

    def body(x_ref, g_ref, w_ref, bd_ref, qg_ref, kg_ref, cos_ref, sin_ref,
             proj_ref, hc_ref, q_ref, k_ref, v_ref):
        xv = x_ref[...]
        h = (xv * _rms(xv) * g_ref[...]).astype(BF16)
        proj = _dot(h, w_ref[...])
        proj_ref[...] = proj
        cos_pair = cos_ref[...]
        sin_pair = sin_ref[...]
        cosv = jnp.concatenate([cos_pair] * (N_HEADS // 2), axis=1)
        sinv = jnp.concatenate([sin_pair] * (N_HEADS // 2), axis=1)
        q = proj[:, C_Q:C_K]
        qn = q * lax.rsqrt(_group_mean(q * q, bd_ref[...]) + EPS) * qg_ref[...]
        qr = (qn * cosv + _swap16(qn) * sinv) * ATT_SCALE
        for hh in range(N_HEADS):
            pair = qr[:, (hh // 2) * LANES:(hh // 2 + 1) * LANES]
            q_ref[hh] = _head_slab(pair, hh % 2 == 1).astype(BF16)
        k = proj[:, C_K:C_V]
        kn = k * lax.rsqrt(_group_mean(k * k, bd_ref[0:LANES, 0:LANES]) + EPS) * kg_ref[...]
        kr = kn * cos_pair + _swap16(kn) * sin_pair
        vv = proj[:, C_V:C_GA]
        for hh in range(N_KV):
            k_ref[hh] = _with_ones(_head_slab(kr, hh == 1)).astype(BF16)
            v_ref[hh] = _with_ones(_head_slab(vv, hh == 1)).astype(BF16)
        hc_ref[...] = proj[:, C_A1:C_A2] * _sigmoid(proj[:, C_A2:C_GC])

    row = lambda w: pl.BlockSpec((t, w), lambda i: (i, 0))
    heads = lambda n: pl.BlockSpec((n, t, LANES), lambda i: (0, i, 0))
    return pl.pallas_call(
        body, name="fwd_in", grid=(s // t,),
        in_specs=[row(D_MODEL), _const_spec((1, D_MODEL)), _const_spec((D_MODEL, D_IN)), _const_spec((512, 512)),
                  _const_spec((1, 512)), _const_spec((1, LANES)), row(LANES), row(LANES)],
        out_specs=[row(D_IN), row(CONV_W), heads(N_HEADS), heads(N_KV), heads(N_KV)],
        out_shape=[jax.ShapeDtypeStruct((s, D_IN), F32), jax.ShapeDtypeStruct((s, CONV_W), F32),
                   jax.ShapeDtypeStruct((N_HEADS, s, LANES), BF16), jax.ShapeDtypeStruct((N_KV, s, LANES), BF16),
                   jax.ShapeDtypeStruct((N_KV, s, LANES), BF16)],
        compiler_params=_arb(),
    )(x, g_pre, w_in_bf, bd, qg, kg, cos, sin)


def _chunk_rows(c, tk):
    return pl.ds(c * tk, tk) if isinstance(c, int) else pl.ds(pl.multiple_of(c * tk, tk), tk)


def _three_stage_pipeline(nk, per_trip, stage1, stage2, stage3, peel):
    assert nk % 2 == 0 and per_trip % 2 == 0

    def step(t, parity, first=False, last=False):
        if not last:
            stage1(t + 1, 1 - parity)
        stage2(parity)
        if not first:
            stage3(t - 1, 1 - parity)

    stage1(0, 0)
    if not peel:
        assert nk % per_trip == 0

        def whole_trip(i, carry):
            for u in range(per_trip):
                c = per_trip * i + u
                stage1(jnp.minimum(c + 1, nk - 1), 1 - u % 2)
                stage2(u % 2)
                stage3(jnp.maximum(c - 1, 0), 1 - u % 2)
            return carry

        lax.fori_loop(0, nk // per_trip, whole_trip, 0)
        stage3(nk - 1, 1)
        return

    step(0, 0, first=True)
    n_trips, left = divmod(nk - 2, per_trip)

    def trip(i, carry):
        for u in range(per_trip):
            step(1 + per_trip * i + u, (1 + u) % 2)
        return carry

    if n_trips:
        lax.fori_loop(0, n_trips, trip, 0)
    for t in range(1 + n_trips * per_trip, 1 + n_trips * per_trip + left):
        step(t, t % 2)
    step(nk - 1, 1, last=True)
    stage3(nk - 1, 1)


def _flash_fwd(q, k, v, tq, tk):
    s = q.shape[1]
    rows = GROUP * tq
    nk = s // tk

    def body(q_ref, k_ref, vt_ref, o_ref, qa_ref, m_scr, acc_scr, s0, s1, p0, p1, a0, a1):
        s_bufs, p_bufs, a_bufs = (s0, s1), (p0, p1), (a0, a1)
        qv = q_ref[...].reshape(rows, LANES)
        m_scr[...] = jnp.full((1, rows), -jnp.inf, F32)
        acc_scr[...] = jnp.zeros((LANES, rows), F32)

        def scores(c, slot):
            s_bufs[slot][...] = _dot_nt(k_ref[_chunk_rows(c, tk), :], qv)

        def softmax(slot):
            for h in range(GROUP):
                r = slice(h * tq, (h + 1) * tq)
                sc = s_bufs[slot][:, r]
                m_prev = m_scr[:, r]
                m_new = jnp.maximum(m_prev, jnp.max(sc, axis=0, keepdims=True))
                p_bufs[slot][:, r] = jnp.exp(sc - m_new).astype(BF16)
                a_bufs[slot][:, r] = jnp.exp(m_prev - m_new)
                m_scr[:, r] = m_new

        def weighted_values(c, slot):
            acc_scr[...] = a_bufs[slot][...] * acc_scr[...] + _dot(vt_ref[c], p_bufs[slot][...])

        _three_stage_pipeline(nk, 4, scores, softmax, weighted_values, peel=True)

        acc = acc_scr[...]
        row = lax.broadcasted_iota(jnp.int32, acc.shape, 0)
        l = jnp.sum(jnp.where(row == HEAD_DIM, acc, 0.0), axis=0, keepdims=True)
        o_ref[...] = jnp.where(row < HEAD_DIM, acc / l, 0.0).T.reshape(GROUP, tq, LANES)
        hi, mid, lo = _split3(-(m_scr[...] + jnp.log(l)))
        qa_t = jnp.where(row == HEAD_DIM, hi, jnp.where(row == HEAD_DIM + 1, mid,
                                                        jnp.where(row == HEAD_DIM + 2, lo, qv.astype(F32).T)))
        qa_ref[...] = qa_t.T.astype(BF16).reshape(GROUP, tq, LANES)

    qspec = pl.BlockSpec((GROUP, tq, LANES), lambda j, i: (j, i, 0))
    kspec = pl.BlockSpec((None, s, LANES), lambda j, i: (j, 0, 0))
    vtspec = pl.BlockSpec((None, nk, LANES, tk), lambda j, i: (j, 0, 0, 0))
    v_t = jnp.swapaxes(v.reshape(N_KV, nk, tk, LANES), 2, 3)
    return pl.pallas_call(
        body, name="flash_fwd", grid=(N_KV, s // tq),
        in_specs=[qspec, kspec, vtspec],
        out_specs=[qspec, qspec],
        out_shape=[jax.ShapeDtypeStruct((N_HEADS, s, LANES), F32), jax.ShapeDtypeStruct((N_HEADS, s, LANES), BF16)],
        scratch_shapes=[pltpu.VMEM((1, rows), F32), pltpu.VMEM((LANES, rows), F32),
                        pltpu.VMEM((tk, rows), F32), pltpu.VMEM((tk, rows), F32),
                        pltpu.VMEM((tk, rows), BF16), pltpu.VMEM((tk, rows), BF16),
                        pltpu.VMEM((1, rows), F32), pltpu.VMEM((1, rows), F32)],
        compiler_params=_arb(2),
    )(q, k, v_t)


def _groups_fwd(proj_ref, o_ref, hext_ref, cw_ref, cp_ref, sp_ref, sgw_ref, sgb_ref, t):
    proj = proj_ref[...]
    r = {}
    r["att"] = _heads_to_cat(o_ref, N_HEADS)
    r["gate_a"], r["dgate_a"] = _silu_fwd_bwd(proj[:, C_GA:C_A1])
    r["att_g"] = r["att"] * r["gate_a"]
    c0 = jnp.zeros((t, CONV_W), F32) + cp_ref[0:1, :]
    for kk in range(CONV_K):
        c0 = c0 + cw_ref[kk:kk + 1, :] * _tap(hext_ref, kk + 1, t)
    r["xh_c"], r["rs_c"] = _ln_hat(c0)
    r["c1"] = r["xh_c"] * cp_ref[1:2, :] + cp_ref[2:3, :]
    r["sg_c1"] = _sigmoid(r["c1"])
    r["c2"] = r["c1"] * r["sg_c1"]
    r["gate_c"], r["dgate_c"] = _silu_fwd_bwd(proj[:, C_GC:C_U])
    r["cnv_g"] = r["c2"] * r["gate_c"]
    r["gu"], r["dgu"] = _gelu_fwd_bwd(proj[:, C_U:C_VS])
    gv, r["dgv"] = _gelu_fwd_bwd(proj[:, C_VS:C_GS])
    r["xh_s"], r["rs_s"] = _ln_hat(gv)
    v1 = r["xh_s"] * sp_ref[0:1, :] + sp_ref[1:2, :]
    r["v1_bf"] = v1.astype(BF16)
    r["mixed"] = _sgu_mix(r["v1_bf"], sgw_ref, t // SG_CHUNK) + jnp.concatenate([sgb_ref[...]] * (t // SG_CHUNK), axis=0)
    r["um"] = r["gu"] * r["mixed"]
    r["gate_s"], r["dgate_s"] = _silu_fwd_bwd(proj[:, C_GS:D_IN])
    r["sgu_g"] = r["um"] * r["gate_s"]
    r["mc_bf"] = jnp.concatenate([r["att_g"], r["cnv_g"], r["sgu_g"]], axis=1).astype(BF16)
    return r


def _fwd_out(x, proj, o, hc, cw, cp, sp, sgw_bf, sgb, w_out_bf, g_post, t):
    s = x.shape[0]

    def body(x_ref, proj_ref, o_ref, hc_ref, hp_ref, hn_ref, cw_ref, cp_ref, sp_ref, sgw_ref, sgb_ref,
             w_ref, g_ref, mix_ref, xn_ref, hext_ref):
        i = pl.program_id(0)
        _conv_window(hext_ref, hp_ref, hc_ref[...], hn_ref, i == 0, i == pl.num_programs(0) - 1, t)
        r = _groups_fwd(proj_ref, o_ref, hext_ref, cw_ref, cp_ref, sp_ref, sgw_ref, sgb_ref, t)
        mix = _dot(r["mc_bf"], w_ref[...])
        mix_ref[...] = mix
        xn_ref[...] = x_ref[...] + mix * _rms(mix) * g_ref[...]

    row = lambda w: pl.BlockSpec((t, w), lambda i: (i, 0))
    hprev, hnext = _halo_specs(t, s, CONV_W)
    return pl.pallas_call(
        body, name="fwd_out", grid=(s // t,),
        in_specs=[row(D_MODEL), row(D_IN), pl.BlockSpec((N_HEADS, t, LANES), lambda i: (0, i, 0)), row(CONV_W),
                  hprev, hnext, _const_spec((32, CONV_W)), _const_spec((8, CONV_W)), _const_spec((8, SG_W)),
                  _const_spec((4, SG_CHUNK, SG_CHUNK)), _const_spec((SG_CHUNK, SG_W)),
                  _const_spec((D_MODEL, D_MODEL)), _const_spec((1, D_MODEL))],
        out_specs=[row(D_MODEL), row(D_MODEL)],
        out_shape=[jax.ShapeDtypeStruct((s, D_MODEL), F32), jax.ShapeDtypeStruct((s, D_MODEL), F32)],
        scratch_shapes=[pltpu.VMEM((SUBLANES, t + 2 * HALO, CONV_W), F32)],
        compiler_params=_arb(),
    )(x, proj, o, hc, hc, hc, cw, cp, sp, sgw_bf, sgb, w_out_bf, g_post)


def _loss_grad(y, target, t):
    s = y.shape[0]

    def body(y_ref, t_ref, dy_ref, sq_ref):
        @pl.when(pl.program_id(0) == 0)
        def _():
            sq_ref[...] = jnp.zeros_like(sq_ref)

        err = y_ref[...] - t_ref[...]
        dy_ref[...] = err * (1.0 / D_MODEL)
        sq_ref[...] += jnp.sum(err * err, axis=0, keepdims=True)

    row = pl.BlockSpec((t, D_MODEL), lambda i: (i, 0))
    return pl.pallas_call(
        body, name="loss_grad", grid=(s // t,),
        in_specs=[row, row], out_specs=[row, _const_spec((1, D_MODEL))],
        out_shape=[jax.ShapeDtypeStruct((s, D_MODEL), F32), jax.ShapeDtypeStruct((1, D_MODEL), F32)],
        compiler_params=_arb(),
    )(y, target)


def _bwd_out(dy, mix, proj, o, hc, cw, cp, sp, sgw_bf, sgwt_bf, sgb, w_out_t_bf, g_post, t):
    s = dy.shape[0]
    n_chunks = t // SG_CHUNK

    def body(dy_ref, mix_ref, proj_ref, o_ref, hc_ref, hp_ref, hn_ref, cw_ref, cp_ref, sp_ref, sgw_ref, sgwt_ref,
             sgb_ref, wt_ref, g_ref,
             do_ref, dgs_ref, dc0_ref, gwo_ref, gpost_ref, gcw_ref, gvec_ref, gsgw_ref, gsgb_ref, hext_ref):
        i = pl.program_id(0)

        @pl.when(i == 0)
        def _():
            for ref in (gwo_ref, gpost_ref, gcw_ref, gvec_ref, gsgw_ref, gsgb_ref):
                ref[...] = jnp.zeros_like(ref)

        _conv_window(hext_ref, hp_ref, hc_ref[...], hn_ref, i == 0, i == pl.num_programs(0) - 1, t)
        r = _groups_fwd(proj_ref, o_ref, hext_ref, cw_ref, cp_ref, sp_ref, sgw_ref, sgb_ref, t)

        dyv = dy_ref[...]
        mix_v = mix_ref[...]
        rr = _rms(mix_v)
        gd = dyv * g_ref[...]
        dmix = rr * gd - mix_v * (rr * rr * rr * jnp.mean(gd * mix_v, axis=-1, keepdims=True))
        gpost_ref[...] += jnp.sum(dyv * mix_v * rr, axis=0, keepdims=True)
        dmix_bf = dmix.astype(BF16)
        gwo_ref[...] += _dot_tn(r["mc_bf"], dmix_bf)
        dmc = _dot(dmix_bf, wt_ref[...])

        d_att = dmc[:, 0:512]
        dg_att = d_att * r["att"] * r["dgate_a"]
        d_o = d_att * r["gate_a"]
        prod = d_o * r["att"]
        for p in range(N_HEADS // 2):
            sl = slice(p * LANES, (p + 1) * LANES)
            pr = prod[:, sl]
            tot = jnp.sum(pr, axis=1, keepdims=True)
            ev = jnp.sum(jnp.where(_lane(pr.shape) < HEAD_DIM, pr, 0.0), axis=1, keepdims=True)
            for odd, delta in ((False, ev), (True, tot - ev)):
                hi, mid, lo = _split3(-delta)
                do_ref[2 * p + int(odd)] = _with_spare(_head_slab(d_o[:, sl], odd), hi, mid, lo).astype(BF16)

        dcv = dmc[:, 512:768]
        dg_conv = dcv * r["c2"] * r["dgate_c"]
        dc1 = dcv * r["gate_c"] * (r["sg_c1"] * (1.0 + r["c1"] * (1.0 - r["sg_c1"])))
        dc0 = _ln_bwd(dc1 * cp_ref[1:2, :], r["xh_c"], r["rs_c"])
        dc0_ref[...] = dc0
        for kk in range(CONV_K):
            gcw_ref[kk:kk + 1, :] += jnp.sum(dc0 * _tap(hext_ref, kk + 1, t), axis=0, keepdims=True)

        dsg = dmc[:, 768:1024]
        dg_sg = dsg * r["um"] * r["dgate_s"]
        du = dsg * r["mixed"] * r["gate_s"] * r["dgu"]
        dmx = dsg * r["gu"] * r["gate_s"]
        dmx_bf = dmx.astype(BF16)
        sgb_sum = dmx[0:SG_CHUNK, :]
        for n in range(1, n_chunks):
            sgb_sum = sgb_sum + dmx[n * SG_CHUNK:(n + 1) * SG_CHUNK, :]
        gsgb_ref[...] += sgb_sum
        dv1_rows = []
        for n in range(n_chunks):
            pairs = []
            for p in range(SG_W // LANES):
                rs_ = slice(n * SG_CHUNK, (n + 1) * SG_CHUNK)
                ls_ = slice(p * LANES, (p + 1) * LANES)
                dm = dmx_bf[rs_, ls_]
                xp = r["v1_bf"][rs_, ls_]
                low = _lane(dm.shape) < HEAD_DIM
                zero = jnp.zeros_like(dm)
                gsgw_ref[2 * p] += _dot_nt(jnp.where(low, dm, zero), xp)
                gsgw_ref[2 * p + 1] += _dot_nt(jnp.where(low, zero, dm), xp)
                pairs.append(jnp.where(low, _dot(sgwt_ref[2 * p], dm), _dot(sgwt_ref[2 * p + 1], dm)))
            dv1_rows.append(jnp.concatenate(pairs, axis=1))
        dv1 = dv1_rows[0] if n_chunks == 1 else jnp.concatenate(dv1_rows, axis=0)
        dvs = _ln_bwd(dv1 * sp_ref[0:1, :], r["xh_s"], r["rs_s"]) * r["dgv"]

        zrow = jnp.zeros((1, CONV_W), F32)
        gvec_ref[...] += jnp.concatenate([
            jnp.sum(dc0, axis=0, keepdims=True),
            jnp.sum(dc1 * r["xh_c"], axis=0, keepdims=True),
            jnp.sum(dc1, axis=0, keepdims=True),
            jnp.sum(dv1 * r["xh_s"], axis=0, keepdims=True),
            jnp.sum(dv1, axis=0, keepdims=True),
            zrow, zrow, zrow], axis=0)
        dgs_ref[...] = jnp.concatenate([dg_att, dg_conv, du, dvs, dg_sg], axis=1)

    row = lambda w: pl.BlockSpec((t, w), lambda i: (i, 0))
    heads = pl.BlockSpec((N_HEADS, t, LANES), lambda i: (0, i, 0))
    hprev, hnext = _halo_specs(t, s, CONV_W)
    return pl.pallas_call(
        body, name="bwd_out", grid=(s // t,),
        in_specs=[row(D_MODEL), row(D_MODEL), row(D_IN), heads, row(CONV_W), hprev, hnext,
                  _const_spec((32, CONV_W)), _const_spec((8, CONV_W)), _const_spec((8, SG_W)),
                  _const_spec((4, SG_CHUNK, SG_CHUNK)), _const_spec((4, SG_CHUNK, SG_CHUNK)),
                  _const_spec((SG_CHUNK, SG_W)), _const_spec((D_MODEL, D_MODEL)), _const_spec((1, D_MODEL))],
        out_specs=[heads, row(1536), row(CONV_W), _const_spec((D_MODEL, D_MODEL)), _const_spec((1, D_MODEL)),
                   _const_spec((32, CONV_W)), _const_spec((8, CONV_W)), _const_spec((4, SG_CHUNK, SG_CHUNK)),
                   _const_spec((SG_CHUNK, SG_W))],
        out_shape=[jax.ShapeDtypeStruct((N_HEADS, s, LANES), BF16), jax.ShapeDtypeStruct((s, 1536), F32),
                   jax.ShapeDtypeStruct((s, CONV_W), F32), jax.ShapeDtypeStruct((D_MODEL, D_MODEL), F32),
                   jax.ShapeDtypeStruct((1, D_MODEL), F32), jax.ShapeDtypeStruct((32, CONV_W), F32),
                   jax.ShapeDtypeStruct((8, CONV_W), F32), jax.ShapeDtypeStruct((4, SG_CHUNK, SG_CHUNK), F32),
                   jax.ShapeDtypeStruct((SG_CHUNK, SG_W), F32)],
        scratch_shapes=[pltpu.VMEM((SUBLANES, t + 2 * HALO, CONV_W), F32)],
        compiler_params=_arb(),
    )(dy, mix, proj, o, hc, hc, hc, cw, cp, sp, sgw_bf, sgwt_bf, sgb, w_out_t_bf, g_post)


def _flash_bwd(qa, doa, k, v, tq, tk):
    s = qa.shape[1]
    rows = GROUP * tq
    nk = s // tk
    n_q = s // tq

    def body(qa_ref, do_ref, k_ref, v_ref, dq_ref, dk_hbm, dv_hbm,
             dq_scr, dk_scr, dv_scr, s0, s1, d0, d1, p0, p1, e0, e1, sems):
        j, i = pl.program_id(0), pl.program_id(1)
        s_bufs, d_bufs, p_bufs, e_bufs = (s0, s1), (d0, d1), (p0, p1), (e0, e1)
        qv = qa_ref[...].reshape(rows, LANES)
        dov = do_ref[...].reshape(rows, LANES)
        q_t = qv.astype(F32).T.astype(BF16)
        do_t = dov.astype(F32).T.astype(BF16)
        dq_scr[...] = jnp.zeros((rows, LANES), F32)

        @pl.when(i == 0)
        def _():
            dk_scr[...] = jnp.zeros_like(dk_scr)
            dv_scr[...] = jnp.zeros_like(dv_scr)

        def at(c):
            return _chunk_rows(c, tk)

        def scores(c, slot):
            s_bufs[slot][...] = _dot_nt(qv, k_ref[at(c), :])
            d_bufs[slot][...] = _dot_nt(dov, v_ref[at(c), :])

        def probs(slot):
            for h in range(GROUP):
                r = slice(h * tq, (h + 1) * tq)
                p = jnp.exp(s_bufs[slot][r, :])
                p_bufs[slot][r, :] = p.astype(BF16)
                e_bufs[slot][r, :] = (p * d_bufs[slot][r, :]).astype(BF16)

        def grads(c, slot):
            ds = e_bufs[slot][...]
            dq_scr[...] += _dot(ds, k_ref[at(c), :])
            dv_scr[c] += _dot(do_t, p_bufs[slot][...])
            dk_scr[c] += _dot(q_t, ds)

        p1[...] = jnp.zeros((rows, tk), BF16)
        e1[...] = jnp.zeros((rows, tk), BF16)
        _three_stage_pipeline(nk, 2, scores, probs, grads, peel=False)
        dq_ref[...] = dq_scr[...].reshape(GROUP, tq, LANES)

        @pl.when(i == n_q - 1)
        def _():
            out = [pltpu.make_async_copy(dk_scr, dk_hbm.at[j], sems.at[0]),
                   pltpu.make_async_copy(dv_scr, dv_hbm.at[j], sems.at[1])]
            for cp in out:
                cp.start()
            for cp in out:
                cp.wait()

    qspec = pl.BlockSpec((GROUP, tq, LANES), lambda j, i: (j, i, 0))
    kvspec = pl.BlockSpec((None, s, LANES), lambda j, i: (j, 0, 0))
    hbm = pl.BlockSpec(memory_space=pl.ANY)
    stage_f32 = pltpu.VMEM((rows, tk), F32)
    stage_bf = pltpu.VMEM((rows, tk), BF16)
    kv_t = jax.ShapeDtypeStruct((N_KV, nk, LANES, tk), F32)
    dq, dk_t, dv_t = pl.pallas_call(
        body, name="flash_bwd", grid=(N_KV, n_q),
        in_specs=[qspec, qspec, kvspec, kvspec],
        out_specs=[qspec, hbm, hbm],
        out_shape=[jax.ShapeDtypeStruct((N_HEADS, s, LANES), F32), kv_t, kv_t],
        scratch_shapes=[pltpu.VMEM((rows, LANES), F32), pltpu.VMEM((nk, LANES, tk), F32), pltpu.VMEM((nk, LANES, tk), F32),
                        stage_f32, stage_f32, stage_f32, stage_f32, stage_bf, stage_bf, stage_bf, stage_bf,
                        pltpu.SemaphoreType.DMA((2,))],
        compiler_params=_arb(2),
    )(qa, doa, k, v)
    untranspose = lambda a: jnp.swapaxes(a, 2, 3).reshape(N_KV, s, LANES)
    return dq, untranspose(dk_t), untranspose(dv_t)


def _bwd_in(dy, x, proj, dq, dk, dv, dgs, dc0, cw, g_pre, w_in_t_bf, bd, qg, kg, cos, sin, t):
    s = x.shape[0]

    def body(dy_ref, x_ref, proj_ref, dq_ref, dk_ref, dv_ref, dgs_ref, dc_ref, dcp_ref, dcn_ref, cw_ref, g_ref,
             wt_ref, bd_ref, qg_ref, kg_ref, cos_ref, sin_ref,
             dx_ref, dproj_ref, h_ref, gpre_ref, gq_ref, gk_ref, dext_ref):
        i = pl.program_id(0)

        @pl.when(i == 0)
        def _():
            for ref in (gpre_ref, gq_ref, gk_ref):
                ref[...] = jnp.zeros_like(ref)

        proj = proj_ref[...]
        cos_pair = cos_ref[...]
        sin_pair = sin_ref[...]
        cosv = jnp.concatenate([cos_pair] * (N_HEADS // 2), axis=1)
        sinv = jnp.concatenate([sin_pair] * (N_HEADS // 2), axis=1)

        def head_norm_bwd(dr, z, bdm, g, cs, sn, gacc_ref):
            dn = dr * cs + _swap16(dr * sn)
            rr = lax.rsqrt(_group_mean(z * z, bdm) + EPS)
            gdn = dn * g
            gacc_ref[...] += jnp.sum(dn * z * rr, axis=0, keepdims=True)
            return rr * gdn - z * (rr * rr * rr * _group_mean(gdn * z, bdm))

        dq_cat = _heads_to_cat(dq_ref, N_HEADS) * ATT_SCALE
        dzq = head_norm_bwd(dq_cat, proj[:, C_Q:C_K], bd_ref[...], qg_ref[...], cosv, sinv, gq_ref)
        dk_cat = _heads_to_cat(dk_ref, N_KV)
        dzk = head_norm_bwd(dk_cat, proj[:, C_K:C_V], bd_ref[0:LANES, 0:LANES], kg_ref[...],
                            cos_pair, sin_pair, gk_ref)
        dv_cat = _heads_to_cat(dv_ref, N_KV)

        _conv_window(dext_ref, dcp_ref, dc_ref[...], dcn_ref, i == 0, i == pl.num_programs(0) - 1, t)
        dhc = jnp.zeros((t, CONV_W), F32)
        for kk in range(CONV_K):
            dhc = dhc + cw_ref[kk:kk + 1, :] * _tap(dext_ref, CONV_K - kk, t)
        sg = _sigmoid(proj[:, C_A2:C_GC])
        da1 = dhc * sg
        da2 = dhc * proj[:, C_A1:C_A2] * sg * (1.0 - sg)

        dgs = dgs_ref[...]
        dproj_bf = jnp.concatenate([dzq, dzk, dv_cat, dgs[:, 0:512], da1, da2, dgs[:, 512:1536]], axis=1).astype(BF16)
        dproj_ref[...] = dproj_bf
        dh = _dot(dproj_bf, wt_ref[...])

        xv = x_ref[...]
        rr = _rms(xv)
        gv = g_ref[...]
        h_ref[...] = (xv * rr * gv).astype(BF16)
        gdh = dh * gv
        gpre_ref[...] += jnp.sum(dh * xv * rr, axis=0, keepdims=True)
        dx_ref[...] = dy_ref[...] + rr * gdh - xv * (rr * rr * rr * jnp.mean(gdh * xv, axis=-1, keepdims=True))

    row = lambda w: pl.BlockSpec((t, w), lambda i: (i, 0))
    heads = lambda n: pl.BlockSpec((n, t, LANES), lambda i: (0, i, 0))
    hprev, hnext = _halo_specs(t, s, CONV_W)
    return pl.pallas_call(
        body, name="bwd_in", grid=(s // t,),
        in_specs=[row(D_MODEL), row(D_MODEL), row(D_IN), heads(N_HEADS), heads(N_KV), heads(N_KV), row(1536),
                  row(CONV_W), hprev, hnext, _const_spec((32, CONV_W)), _const_spec((1, D_MODEL)),
                  _const_spec((D_IN, D_MODEL)), _const_spec((512, 512)), _const_spec((1, 512)),
                  _const_spec((1, LANES)), row(LANES), row(LANES)],
        out_specs=[row(D_MODEL), row(D_IN), row(D_MODEL), _const_spec((1, D_MODEL)), _const_spec((1, 512)),
                   _const_spec((1, LANES))],
        out_shape=[jax.ShapeDtypeStruct((s, D_MODEL), F32), jax.ShapeDtypeStruct((s, D_IN), BF16),
                   jax.ShapeDtypeStruct((s, D_MODEL), BF16), jax.ShapeDtypeStruct((1, D_MODEL), F32),
                   jax.ShapeDtypeStruct((1, 512), F32), jax.ShapeDtypeStruct((1, LANES), F32)],
        scratch_shapes=[pltpu.VMEM((SUBLANES, t + 2 * HALO, CONV_W), F32)],
        compiler_params=_arb(),
    )(dy, x, proj, dq, dk, dv, dgs, dc0, dc0, dc0, cw, g_pre, w_in_t_bf, bd, qg, kg, cos, sin)


def _grad_w_in(h_bf, dproj_bf, t):
    s = h_bf.shape[0]
    half = D_IN // 2

    def body(h_ref, d_ref, g_ref):
        @pl.when(pl.program_id(1) == 0)
        def _():
            g_ref[...] = jnp.zeros_like(g_ref)

        g_ref[...] += _dot_tn(h_ref[...], d_ref[...])

    return pl.pallas_call(
        body, name="grad_w_in", grid=(2, s // t),
        in_specs=[pl.BlockSpec((t, D_MODEL), lambda j, i: (i, 0)), pl.BlockSpec((t, half), lambda j, i: (i, j))],
        out_specs=pl.BlockSpec((D_MODEL, half), lambda j, i: (0, j)),
        out_shape=jax.ShapeDtypeStruct((D_MODEL, D_IN), F32),
        compiler_params=_arb(2),
    )(h_bf, dproj_bf)


def _place():
    x, y, c = lax.axis_index("x"), lax.axis_index("y"), lax.axis_index("c")
    chips = [(1 - x, y), (x, 1 - y), (1 - x, 1 - y)]
    return x, y, c, chips


def _any_specs(n):
    return [pl.BlockSpec(memory_space=pl.ANY)] * n


def _gather_weights(w_in_bf, w_out_bf, cdw):
    arrs = (w_in_bf, w_out_bf, cdw)
    n = len(arrs)

    def body(*refs):
        ins, outs = refs[:n], refs[n:2 * n]
        send_sems, recv_sems, local_sems = refs[2 * n:]
        x, y, c, chips = _place()
        mine = 2 * x + y
        local = [pltpu.make_async_copy(ins[a], outs[a].at[mine], local_sems.at[a]) for a in range(n)]
        for cp in local:
            cp.start()

        def copy(j, a, slot, to):
            return pltpu.make_async_remote_copy(src_ref=ins[a], dst_ref=outs[a].at[slot], send_sem=send_sems.at[n * j + a],
                                                recv_sem=recv_sems.at[n * j + a], device_id=to, device_id_type=MESH)

        sends = [copy(j, a, mine, (px, py, c)) for j, (px, py) in enumerate(chips) for a in range(n)]
        for cp in sends:
            cp.start()
        for j, (px, py) in enumerate(chips):
            for a in range(n):
                copy(j, a, 2 * px + py, (px, py, c)).wait_recv()
        for cp in sends:
            cp.wait_send()
        for cp in local:
            cp.wait()

    return pl.pallas_call(
        body, name="gather_weights",
        in_specs=_any_specs(n), out_specs=_any_specs(n),
        out_shape=[jax.ShapeDtypeStruct((N_CHIPS,) + a.shape, a.dtype) for a in arrs],
        scratch_shapes=[pltpu.SemaphoreType.DMA((3 * n,)), pltpu.SemaphoreType.DMA((3 * n,)), pltpu.SemaphoreType.DMA((n,))],
    )(*arrs)


def _scatter_grads(gin_pieces, gw_out):
    depth = gw_out.shape[0]
    rows = gw_out.shape[1] // N_CHIPS

    def body(gin_ref, gout_ref, rin_ref, rout_ref, send_sems, recv_sems, local_sems):
        x, y, c, chips = _place()
        mine = 2 * x + y

        def out_rows(chip):
            return gout_ref.at[:, pl.ds(pl.multiple_of(chip * rows, rows), rows), :]

        local = [pltpu.make_async_copy(gin_ref.at[mine], rin_ref.at[mine], local_sems.at[0]),
                 pltpu.make_async_copy(out_rows(mine), rout_ref.at[mine], local_sems.at[1])]
        for cp in local:
            cp.start()

        def copies(j, shard, slot, to):
            kw = dict(device_id=to, device_id_type=MESH)
            return [pltpu.make_async_remote_copy(src_ref=gin_ref.at[shard], dst_ref=rin_ref.at[slot],
                                                 send_sem=send_sems.at[2 * j], recv_sem=recv_sems.at[2 * j], **kw),
                    pltpu.make_async_remote_copy(src_ref=out_rows(shard), dst_ref=rout_ref.at[slot],
                                                 send_sem=send_sems.at[2 * j + 1], recv_sem=recv_sems.at[2 * j + 1], **kw)]

        sends = [cp for j, (px, py) in enumerate(chips) for cp in copies(j, 2 * px + py, mine, (px, py, c))]
        for cp in sends:
            cp.start()
        for j, (px, py) in enumerate(chips):
            for cp in copies(j, mine, 2 * px + py, (px, py, c)):
                cp.wait_recv()
        for cp in sends:
            cp.wait_send()
        for cp in local:
            cp.wait()

    return pl.pallas_call(
        body, name="scatter_grads",
        in_specs=_any_specs(2), out_specs=_any_specs(2),
        out_shape=[jax.ShapeDtypeStruct(gin_pieces.shape, F32),
                   jax.ShapeDtypeStruct((N_CHIPS, depth, rows, gw_out.shape[2]), F32)],
        scratch_shapes=[pltpu.SemaphoreType.DMA((6,)), pltpu.SemaphoreType.DMA((6,)), pltpu.SemaphoreType.DMA((2,))],
    )(gin_pieces, gw_out)


def _sum_chips(parts, rb, name):
    _, depth, r, cdim = parts.shape

    def body(p_ref, o_ref):
        o_ref[...] = ((p_ref[0] + p_ref[1]) + p_ref[2]) + p_ref[3]

    return pl.pallas_call(
        body, name=name, grid=(depth, r // rb),
        in_specs=[pl.BlockSpec((N_CHIPS, None, rb, cdim), lambda l, i: (0, l, i, 0))],
        out_specs=pl.BlockSpec((None, rb, cdim), lambda l, i: (l, i, 0)),
        out_shape=jax.ShapeDtypeStruct((depth, r, cdim), F32),
        compiler_params=_arb(2),
    )(parts)


def _swap_with_sibling(a, b):
    arrs = (a, b)
    n = len(arrs)

    def body(*refs):
        ins, outs = refs[:n], refs[n:2 * n]
        send_sems, recv_sems = refs[2 * n:]
        x, y, c, _ = _place()
        cps = [pltpu.make_async_remote_copy(src_ref=ins[k], dst_ref=outs[k], send_sem=send_sems.at[k],
                                            recv_sem=recv_sems.at[k], device_id=(x, y, 1 - c), device_id_type=MESH)
               for k in range(n)]
        for cp in cps:
            cp.start()
        for cp in cps:
            cp.wait()

    return pl.pallas_call(
        body, name="swap_with_sibling",
        in_specs=_any_specs(n), out_specs=_any_specs(n),
        out_shape=[jax.ShapeDtypeStruct(v.shape, v.dtype) for v in arrs],
        scratch_shapes=[pltpu.SemaphoreType.DMA((n,)), pltpu.SemaphoreType.DMA((n,))],
    )(*arrs)


def _allreduce_small(slab):
    m, n = slab.shape

    def body(x_ref, out_ref, gath, send_sems, recv_sems, local_sem):
        x, y, c, chips = _place()
        me, sibling = (x, y, c), (x, y, 1 - c)

        def rows(px, py, pc):
            return gath.at[pl.ds(pl.multiple_of((4 * px + 2 * py + pc) * m, 8), m), :]

        def copy(k, block, to, src=None):
            return pltpu.make_async_remote_copy(src_ref=rows(*block) if src is None else src, dst_ref=rows(*block),
                                                send_sem=send_sems.at[k], recv_sem=recv_sems.at[k],
                                                device_id=to, device_id_type=MESH)

        mine = pltpu.make_async_copy(x_ref, rows(*me), local_sem)
        mine.start()
        first = [copy(0, me, sibling, src=x_ref)]
        first += [copy(1 + j, me, (*chip, c), src=x_ref) for j, chip in enumerate(chips)]
        for cp in first:
            cp.start()
        passed = [copy(4 + j, (*chip, c), sibling) for j, chip in enumerate(chips)]
        for j, chip in enumerate(chips):
            copy(1 + j, (*chip, c), me).wait_recv()
            passed[j].start()
        copy(0, sibling, me).wait_recv()
        for j, chip in enumerate(chips):
            copy(4 + j, (*chip, 1 - c), me).wait_recv()
        for cp in first + passed:
            cp.wait_send()
        mine.wait()
        total = gath[0:m, :]
        for d in range(1, N_DEV):
            total = total + gath[d * m:(d + 1) * m, :]
        out_ref[...] = total

    return pl.pallas_call(
        body, name="allreduce_small",
        in_specs=[pl.BlockSpec(memory_space=pltpu.VMEM)],
        out_specs=pl.BlockSpec(memory_space=pltpu.VMEM),
        out_shape=jax.ShapeDtypeStruct((m, n), F32),
        scratch_shapes=[pltpu.VMEM((N_DEV * m, n), F32), pltpu.SemaphoreType.DMA((7,)), pltpu.SemaphoreType.DMA((7,)),
                        pltpu.SemaphoreType.DMA],
    )(slab)


def _adamw(w, ga, gb, m, v, rb, name):
    depth, r, cdim = w.shape

    def body(w_ref, ga_ref, gb_ref, m_ref, v_ref, g_out, d_out, m_out, v_out):
        g = ga_ref[...] + gb_ref[...]
        m2 = ADAM_B1 * m_ref[...] + (1.0 - ADAM_B1) * g
        v2 = ADAM_B2 * v_ref[...] + (1.0 - ADAM_B2) * (g * g)
        m_hat = m2 / (1.0 - ADAM_B1 ** ADAM_STEP)
        v_hat = v2 / (1.0 - ADAM_B2 ** ADAM_STEP)
        g_out[...] = g
        d_out[...] = -ADAM_LR * (m_hat / (jnp.sqrt(v_hat) + ADAM_EPS) + ADAM_WD * w_ref[...])
        m_out[...] = m2
        v_out[...] = v2

    spec = pl.BlockSpec((None, rb, cdim), lambda l, i: (l, i, 0))
    shp = jax.ShapeDtypeStruct((depth, r, cdim), F32)
    return pl.pallas_call(
        body, name=name, grid=(depth, r // rb),
        in_specs=[spec] * 5, out_specs=[spec] * 4, out_shape=[shp] * 4,
        compiler_params=_arb(2),
    )(w, ga, gb, m, v)


def _rope_tables(s):
    t = jnp.arange(s, dtype=jnp.int32)
    row = (t // GRID_W).astype(F32)
    col = (t % GRID_W).astype(F32)
    half = HEAD_DIM // 4
    inv_freq = ROPE_THETA ** (-jnp.arange(half, dtype=F32) / half)
    ar = row[:, None] * inv_freq[None, :]
    ac = col[:, None] * inv_freq[None, :]
    cos = jnp.concatenate([jnp.cos(ar), jnp.cos(ar), jnp.cos(ac), jnp.cos(ac)], axis=1)
    sin = jnp.concatenate([-jnp.sin(ar), jnp.sin(ar), -jnp.sin(ac), jnp.sin(ac)], axis=1)
    return jnp.tile(cos, (1, 2)), jnp.tile(sin, (1, 2))


def _pad_rows(a, rows):
    return jnp.concatenate([a, jnp.zeros((rows - a.shape[0],) + a.shape[1:], a.dtype)], axis=0)


_SMALL = ("pre_norm", "post_norm", "q_norm", "k_norm", "conv_dw_b", "conv_ln_g", "conv_ln_b", "sg_ln_g", "sg_ln_b",
          "sg_w", "sg_b")


def _pack(parts):
    flat = jnp.concatenate([p.reshape(-1, LANES) for p in parts], axis=0)
    return _pad_rows(flat, -(-flat.shape[0] // 8) * 8)


def _unpack(slab, shapes):
    out, r = [], 0
    for shp in shapes:
        n = 1
        for d in shp:
            n *= d
        out.append(slab[r:r + n // LANES].reshape(shp))
        r += n // LANES
    return out


def kernel(x, pre_norm, post_norm, w_in, w_out, q_norm, k_norm, conv_dw, conv_dw_b, conv_ln_g, conv_ln_b, sg_ln_g, sg_ln_b, sg_w, sg_b, loss_target, m_pre_norm, m_post_norm, m_w_in, m_w_out, m_q_norm, m_k_norm, m_conv_dw, m_conv_dw_b, m_conv_ln_g, m_conv_ln_b, m_sg_ln_g, m_sg_ln_b, m_sg_w, m_sg_b, v_pre_norm, v_post_norm, v_w_in, v_w_out, v_q_norm, v_k_norm, v_conv_dw, v_conv_dw_b, v_conv_ln_g, v_conv_ln_b, v_sg_ln_g, v_sg_ln_b, v_sg_w, v_sg_b):
    depth = w_in.shape[0]
    s = x.shape[1]
    assert x.shape[0] == 1 and s % SG_CHUNK == 0 and x.shape[2] == D_MODEL
    t = min(256, s)
    tq = min(256, s)
    tk = min(512, s // 2)
    shard_cols = w_in.shape[2]
    chip = 2 * lax.axis_index("x") + lax.axis_index("y")

    gin, gout, gcdw = _gather_weights(w_in.astype(BF16), w_out.astype(BF16), conv_dw)
    w_in_bf = jnp.concatenate([gin[j] for j in range(N_CHIPS)], axis=2)
    w_out_bf = jnp.concatenate([gout[j] for j in range(N_CHIPS)], axis=1)
    cdw_full = jnp.concatenate([gcdw[j] for j in range(N_CHIPS)], axis=2)
    w_in_t_bf = jnp.swapaxes(w_in_bf, 1, 2)
    w_out_t_bf = jnp.swapaxes(w_out_bf, 1, 2)
    sgw_bf = sg_w.astype(BF16)
    sgwt_bf = jnp.swapaxes(sg_w, 2, 3).astype(BF16)

    cos, sin = _rope_tables(s)
    bd = jnp.kron(jnp.eye(N_HEADS, dtype=F32), jnp.full((HEAD_DIM, HEAD_DIM), 1.0 / HEAD_DIM, F32)).astype(BF16)

    def layer_consts(l):
        cw = _pad_rows(cdw_full[l], 32)
        cp = _pad_rows(jnp.stack([conv_dw_b[l], conv_ln_g[l], conv_ln_b[l]]), 8)
        sp = _pad_rows(jnp.stack([sg_ln_g[l], sg_ln_b[l]]), 8)
        sgb = jnp.repeat(sg_b[l].T, HEAD_DIM, axis=1)
        qg = jnp.tile(q_norm[l], N_HEADS)[None, :]
        kg = jnp.tile(k_norm[l], N_KV)[None, :]
        return cw, cp, sp, sgb, qg, kg

    xs = [x[0]]
    saved = []
    for l in range(depth):
        cw, cp, sp, sgb, qg, kg = layer_consts(l)
        proj, hc, q, k, v = _fwd_in(xs[l], pre_norm[l][None, :], w_in_bf[l], bd, qg, kg, cos, sin, t)
        o, qa = _flash_fwd(q, k, v, tq, tk)
        mix, xn = _fwd_out(xs[l], proj, o, hc, cw, cp, sp, sgw_bf[l], sgb, w_out_bf[l], post_norm[l][None, :], t)
        saved.append((proj, hc, qa, k, v, o, mix))
        xs.append(xn)

    dy, sq = _loss_grad(xs[depth], loss_target[0], t)
    loss = lax.psum(0.5 * jnp.sum(sq) / D_MODEL, ("x", "y", "c"))

    g_w_in, g_w_out, g_small = [], [], {n: [] for n in _SMALL + ("conv_dw",)}
    for l in reversed(range(depth)):
        cw, cp, sp, sgb, qg, kg = layer_consts(l)
        proj, hc, qa, k, v, o, mix = saved[l]
        doa, dgs, dc0, gwo, gpost, gcw, gvec, gsgw, gsgb = _bwd_out(
            dy, mix, proj, o, hc, cw, cp, sp, sgw_bf[l], sgwt_bf[l], sgb, w_out_t_bf[l], post_norm[l][None, :], t)
        dq, dk, dv = _flash_bwd(qa, doa, k, v, tq, tk)
        dy, dproj_bf, h_bf, gpre, gq, gk = _bwd_in(dy, xs[l], proj, dq, dk, dv, dgs, dc0, cw, pre_norm[l][None, :],
                                                  w_in_t_bf[l], bd, qg, kg, cos, sin, t)
        g_w_in.append(_grad_w_in(h_bf, dproj_bf, min(512, s)))
        g_w_out.append(gwo)
        g_small["pre_norm"].append(gpre[0])
        g_small["post_norm"].append(gpost[0])
        g_small["q_norm"].append(gq[0].reshape(N_HEADS, HEAD_DIM).sum(0))
        g_small["k_norm"].append(gk[0].reshape(N_KV, HEAD_DIM).sum(0))
        g_small["conv_dw"].append(gcw[:CONV_K])
        g_small["conv_dw_b"].append(gvec[0])
        g_small["conv_ln_g"].append(gvec[1])
        g_small["conv_ln_b"].append(gvec[2])
        g_small["sg_ln_g"].append(gvec[3])
        g_small["sg_ln_b"].append(gvec[4])
        g_small["sg_w"].append(gsgw)
        g_small["sg_b"].append(gsgb.reshape(SG_CHUNK, SG_W // HEAD_DIM, HEAD_DIM).sum(-1).T)
    grad_x = dy[None]
    g_w_in = jnp.stack(g_w_in[::-1])
    g_w_out = jnp.stack(g_w_out[::-1])
    g_small = {n: jnp.stack(vals[::-1]) for n, vals in g_small.items()}

    gin_pieces = jnp.stack([g_w_in[:, :, j * shard_cols:(j + 1) * shard_cols] for j in range(N_CHIPS)])
    rin, rout = _scatter_grads(gin_pieces, g_w_out)
    s_in = _sum_chips(rin, 256, "sum_chips_w_in")
    s_out = _sum_chips(rout, 256, "sum_chips_w_out")
    t_in, t_out = _swap_with_sibling(s_in, s_out)
    grad_w_in, delta_w_in, new_m_w_in, new_v_w_in = _adamw(w_in, s_in, t_in, m_w_in, v_w_in, 256, "adamw_w_in")
    grad_w_out, delta_w_out, new_m_w_out, new_v_w_out = _adamw(w_out, s_out, t_out, m_w_out, v_w_out, 256, "adamw_w_out")

    small_w = dict(pre_norm=pre_norm, post_norm=post_norm, q_norm=q_norm, k_norm=k_norm, conv_dw_b=conv_dw_b,
                   conv_ln_g=conv_ln_g, conv_ln_b=conv_ln_b, sg_ln_g=sg_ln_g, sg_ln_b=sg_ln_b, sg_w=sg_w, sg_b=sg_b)
    small_m = dict(pre_norm=m_pre_norm, post_norm=m_post_norm, q_norm=m_q_norm, k_norm=m_k_norm, conv_dw_b=m_conv_dw_b,
                   conv_ln_g=m_conv_ln_g, conv_ln_b=m_conv_ln_b, sg_ln_g=m_sg_ln_g, sg_ln_b=m_sg_ln_b, sg_w=m_sg_w,
                   sg_b=m_sg_b)
    small_v = dict(pre_norm=v_pre_norm, post_norm=v_post_norm, q_norm=v_q_norm, k_norm=v_k_norm, conv_dw_b=v_conv_dw_b,
                   conv_ln_g=v_conv_ln_g, conv_ln_b=v_conv_ln_b, sg_ln_g=v_sg_ln_g, sg_ln_b=v_sg_ln_b, sg_w=v_sg_w,
                   sg_b=v_sg_b)
    shapes = [small_w[n].shape for n in _SMALL]
    red = _allreduce_small(_pack([g_small[n] for n in _SMALL] + [g_small["conv_dw"]]))
    n_rep = sum(small_w[n].size for n in _SMALL) // LANES
    g_cdw_full = red[n_rep:n_rep + g_small["conv_dw"].size // LANES].reshape(g_small["conv_dw"].shape)
    cdw_cols = conv_dw.shape[2]
    g_cdw = lax.dynamic_slice_in_dim(g_cdw_full, chip * cdw_cols, cdw_cols, axis=2)
    g_slab = _pack([red[:n_rep], g_cdw])
    w_slab = _pack([small_w[n] for n in _SMALL] + [conv_dw])
    m_slab = _pack([small_m[n] for n in _SMALL] + [m_conv_dw])
    v_slab = _pack([small_v[n] for n in _SMALL] + [v_conv_dw])
    rows = w_slab.shape[0]
    outs = _adamw(w_slab[None], g_slab[None], jnp.zeros_like(g_slab)[None], m_slab[None], v_slab[None], rows, "adamw_small")
    unpacked = [dict(zip(_SMALL + ("conv_dw",), _unpack(o_[0], shapes + [conv_dw.shape]))) for o_ in outs]

    big = [dict(w_in=a, w_out=b) for a, b in ((grad_w_in, grad_w_out), (delta_w_in, delta_w_out),
                                             (new_m_w_in, new_m_w_out), (new_v_w_in, new_v_w_out))]
    order = ("pre_norm", "post_norm", "w_in", "w_out", "q_norm", "k_norm", "conv_dw", "conv_dw_b", "conv_ln_g",
             "conv_ln_b", "sg_ln_g", "sg_ln_b", "sg_w", "sg_b")
    result = [loss, grad_x]
    for kind in range(4):
        for name in order:
            result.append(big[kind][name] if name in big[kind] else unpacked[kind][name])
    return tuple(result)
```

```python
import functools

import jax
import jax.numpy as jnp
from jax import lax
from jax.experimental import pallas as pl
from jax.experimental.pallas import tpu as pltpu

F32 = jnp.float32
BF16 = jnp.bfloat16
MESH = pl.DeviceIdType.MESH

EPS = 1e-6
D_MODEL = 1024
D_IN = 2816
HEAD_DIM = 64
LANES = 128
SUBLANES = 8
N_HEADS = 8
N_KV = 2
GROUP = N_HEADS // N_KV
GRID_W = 64
ROPE_THETA = 10000.0
CONV_K = 31
CONV_W = 256
SG_W = 256
SG_CHUNK = 128
HALO = 16
ATT_SCALE = HEAD_DIM ** -0.5

C_Q, C_K, C_V, C_GA, C_A1, C_A2, C_GC, C_U, C_VS, C_GS = 0, 512, 640, 768, 1280, 1536, 1792, 2048, 2304, 2560

ADAM_LR = 0.001
ADAM_B1 = 0.9
ADAM_B2 = 0.999
ADAM_EPS = 1e-08
ADAM_WD = 0.01
ADAM_STEP = 10

N_CHIPS = 4
N_DEV = 8


def _dot(a, b):
    return jnp.dot(a, b, preferred_element_type=F32)


def _group_mean(x, bd_bf):
    hi = x.astype(BF16)
    lo = (x - hi.astype(F32)).astype(BF16)
    return _dot(hi, bd_bf) + _dot(lo, bd_bf)


def _dot_nt(a, b):
    return lax.dot_general(a, b, (((1,), (1,)), ((), ())), preferred_element_type=F32)


def _dot_tn(a, b):
    return lax.dot_general(a, b, (((0,), (0,)), ((), ())), preferred_element_type=F32)


def _lane(shape):
    return lax.broadcasted_iota(jnp.int32, shape, 1)


def _sigmoid(x):
    return 1.0 / (1.0 + jnp.exp(-x))


def _silu_fwd_bwd(x):
    s = _sigmoid(x)
    return x * s, s * (1.0 + x * (1.0 - s))


def _erf(x):
    x = jnp.clip(x, -4.0, 4.0)
    x2 = x * x
    a = -2.72614225801306e-10
    a = a * x2 + 2.77068142495902e-08
    a = a * x2 + -2.10102402082508e-06
    a = a * x2 + -5.69250639462346e-05
    a = a * x2 + -7.34990630326855e-04
    a = a * x2 + -2.95459980854025e-03
    a = a * x2 + -1.60960333262415e-02
    b = -1.45660718464996e-05
    b = b * x2 + -2.13374055278905e-04
    b = b * x2 + -1.68282697438203e-03
    b = b * x2 + -7.37332916720468e-03
    b = b * x2 + -1.42647390514189e-02
    return x * a / b


def _gelu_fwd_bwd(x):
    cdf = 0.5 * (1.0 + _erf(x * 0.7071067811865476))
    pdf = jnp.exp(-0.5 * x * x) * 0.3989422804014327
    return x * cdf, cdf + x * pdf


def _rms(x):
    return lax.rsqrt(jnp.mean(x * x, axis=-1, keepdims=True) + EPS)


def _ln_hat(x):
    mu = jnp.mean(x, axis=-1, keepdims=True)
    xc = x - mu
    rs = lax.rsqrt(jnp.mean(xc * xc, axis=-1, keepdims=True) + EPS)
    return xc * rs, rs


def _ln_bwd(dxh, xh, rs):
    return rs * (dxh - jnp.mean(dxh, axis=-1, keepdims=True) - xh * jnp.mean(dxh * xh, axis=-1, keepdims=True))


def _swap16(z):
    parts = []
    for i in range(z.shape[1] // LANES):
        blk = z[:, i * LANES:(i + 1) * LANES]
        lane = _lane(blk.shape)
        parts.append(jnp.where((lane & 16) == 0, pltpu.roll(blk, LANES - 16, 1), pltpu.roll(blk, 16, 1)))
    return parts[0] if len(parts) == 1 else jnp.concatenate(parts, axis=1)


def _head_slab(pair, odd):
    src = pltpu.roll(pair, HEAD_DIM, 1) if odd else pair
    return jnp.where(_lane(pair.shape) < HEAD_DIM, src, 0.0)


def _pair_merge(even, odd):
    return jnp.where(_lane(even.shape) < HEAD_DIM, even, pltpu.roll(odd, HEAD_DIM, 1))


def _heads_to_cat(ref, n_heads):
    pairs = [_pair_merge(ref[2 * p], ref[2 * p + 1]) for p in range(n_heads // 2)]
    return pairs[0] if len(pairs) == 1 else jnp.concatenate(pairs, axis=1)


def _split3(x):
    hi = x.astype(BF16).astype(F32)
    r = x - hi
    mid = r.astype(BF16).astype(F32)
    lo = (r - mid).astype(BF16).astype(F32)
    return hi, mid, lo


def _with_spare(slab, hi, mid, lo):
    lane = _lane(slab.shape)
    return jnp.where(lane == HEAD_DIM, hi, jnp.where(lane == HEAD_DIM + 1, mid, jnp.where(lane == HEAD_DIM + 2, lo, slab)))


def _with_ones(slab):
    lane = _lane(slab.shape)
    return jnp.where((lane >= HEAD_DIM) & (lane < HEAD_DIM + 3), 1.0, slab)


def _conv_window(rot_ref, prev_ref, main, next_ref, first, last, t):
    n = t + 2 * HALO
    full = jnp.concatenate([jnp.where(first, 0.0, prev_ref[...]), main, jnp.where(last, 0.0, next_ref[...])], axis=0)
    rot_ref[0] = full
    for b in range(1, SUBLANES):
        rot_ref[b] = pltpu.roll(full, n - b, 0)


def _tap(rot_ref, start, t):
    a, b = divmod(start, SUBLANES)
    return rot_ref[b, SUBLANES * a:SUBLANES * a + t, :]


def _sgu_mix(v1_bf, w_ref, n_chunks):
    rows = []
    for n in range(n_chunks):
        pairs = []
        for p in range(SG_W // LANES):
            xp = v1_bf[n * SG_CHUNK:(n + 1) * SG_CHUNK, p * LANES:(p + 1) * LANES]
            me = _dot(w_ref[2 * p], xp)
            mo = _dot(w_ref[2 * p + 1], xp)
            pairs.append(jnp.where(_lane(me.shape) < HEAD_DIM, me, mo))
        rows.append(jnp.concatenate(pairs, axis=1))
    return rows[0] if len(rows) == 1 else jnp.concatenate(rows, axis=0)


def _halo_specs(t, s, width):
    per = t // HALO
    nblk = s // HALO
    prev = pl.BlockSpec((HALO, width), lambda i: (jnp.maximum(i * per - 1, 0), 0))
    nxt = pl.BlockSpec((HALO, width), lambda i: (jnp.minimum((i + 1) * per, nblk - 1), 0))
    return prev, nxt


def _const_spec(shape):
    nd = len(shape)
    return pl.BlockSpec(shape, lambda i: (0,) * nd)


def _arb(n=1):
    return pltpu.CompilerParams(dimension_semantics=("arbitrary",) * n)


def _fwd_in(x, g_pre, w_in_bf, bd, qg, kg, cos, sin, t):
    s = x.shape[0]

    def body(x_ref, g_ref, w_ref, bd_ref, qg_ref, kg_ref, cos_ref, sin_ref,
             proj_ref, hc_ref, q_ref, k_ref, v_ref):
        xv = x_ref[...]
        h = (xv * _rms(xv) * g_ref[...]).astype(BF16)
        proj = _dot(h, w_ref[...])
        proj_ref[...] = proj
        cos_pair = cos_ref[...]
        sin_pair = sin_ref[...]
        cosv = jnp.concatenate([cos_pair] * (N_HEADS // 2), axis=1)
        sinv = jnp.concatenate([sin_pair] * (N_HEADS // 2), axis=1)
        q = proj[:, C_Q:C_K]
        qn = q * lax.rsqrt(_group_mean(q * q, bd_ref[...]) + EPS) * qg_ref[...]
        qr = (qn * cosv + _swap16(qn) * sinv) * ATT_SCALE
        for hh in range(N_HEADS):
            pair = qr[:, (hh // 2) * LANES:(hh // 2 + 1) * LANES]
            q_ref[hh] = _head_slab(pair, hh % 2 == 1).astype(BF16)
        k = proj[:, C_K:C_V]
        kn = k * lax.rsqrt(_group_mean(k * k, bd_ref[0:LANES, 0:LANES]) + EPS) * kg_ref[...]
        kr = kn * cos_pair + _swap16(kn) * sin_pair
        vv = proj[:, C_V:C_GA]
        for hh in range(N_KV):
            k_ref[hh] = _with_ones(_head_slab(kr, hh == 1)).astype(BF16)
            v_ref[hh] = _with_ones(_head_slab(vv, hh == 1)).astype(BF16)
        hc_ref[...] = proj[:, C_A1:C_A2] * _sigmoid(proj[:, C_A2:C_GC])

    row = lambda w: pl.BlockSpec((t, w), lambda i: (i, 0))
    heads = lambda n: pl.BlockSpec((n, t, LANES), lambda i: (0, i, 0))
    return pl.pallas_call(
        body, name="fwd_in", grid=(s // t,),
        in_specs=[row(D_MODEL), _const_spec((1, D_MODEL)), _const_spec((D_MODEL, D_IN)), _const_spec((512, 512)),
                  _const_spec((1, 512)), _const_spec((1, LANES)), row(LANES), row(LANES)],
        out_specs=[row(D_IN), row(CONV_W), heads(N_HEADS), heads(N_KV), heads(N_KV)],
        out_shape=[jax.ShapeDtypeStruct((s, D_IN), F32), jax.ShapeDtypeStruct((s, CONV_W), F32),
                   jax.ShapeDtypeStruct((N_HEADS, s, LANES), BF16), jax.ShapeDtypeStruct((N_KV, s, LANES), BF16),
                   jax.ShapeDtypeStruct((N_KV, s, LANES), BF16)],
        compiler_params=_arb(),
    )(x, g_pre, w_in_bf, bd, qg, kg, cos, sin)


def _chunk_rows(c, tk):
    return pl.ds(c * tk, tk) if isinstance(c, int) else pl.ds(pl.multiple_of(c * tk, tk), tk)


def _three_stage_pipeline(nk, per_trip, stage1, stage2, stage3, peel):
    assert nk % 2 == 0 and per_trip % 2 == 0

    def step(t, parity, first=False, last=False):
        if not last:
            stage1(t + 1, 1 - parity)
        stage2(parity)
        if not first:
            stage3(t - 1, 1 - parity)

    stage1(0, 0)
    if not peel:
        assert nk % per_trip == 0

        def whole_trip(i, carry):
            for u in range(per_trip):
                c = per_trip * i + u
                stage1(jnp.minimum(c + 1, nk - 1), 1 - u % 2)
                stage2(u % 2)
                stage3(jnp.maximum(c - 1, 0), 1 - u % 2)
            return carry

        lax.fori_loop(0, nk // per_trip, whole_trip, 0)
        stage3(nk - 1, 1)
        return

    step(0, 0, first=True)
    n_trips, left = divmod(nk - 2, per_trip)

    def trip(i, carry):
        for u in range(per_trip):
            step(1 + per_trip * i + u, (1 + u) % 2)
        return carry

    if n_trips:
        lax.fori_loop(0, n_trips, trip, 0)
    for t in range(1 + n_trips * per_trip, 1 + n_trips * per_trip + left):
        step(t, t % 2)
    step(nk - 1, 1, last=True)
    stage3(nk - 1, 1)


def _flash_fwd(q, k, v, tq, tk):
    s = q.shape[1]
    rows = GROUP * tq
    nk = s // tk

    def body(q_ref, k_ref, vt_ref, o_ref, qa_ref, m_scr, acc_scr, s0, s1, p0, p1, a0, a1):
        s_bufs, p_bufs, a_bufs = (s0, s1), (p0, p1), (a0, a1)
        qv = q_ref[...].reshape(rows, LANES)
        q_t = qv.astype(F32).T
        q_t_bf = q_t.astype(BF16)
        m_scr[...] = jnp.full((1, rows), -jnp.inf, F32)
        acc_scr[...] = jnp.zeros((LANES, rows), F32)

        def scores(c, slot):
            s_bufs[slot][...] = _dot(k_ref[_chunk_rows(c, tk), :], q_t_bf)

        def softmax(slot):
            for h in range(GROUP):
                r = slice(h * tq, (h + 1) * tq)
                sc = s_bufs[slot][:, r]
                m_prev = m_scr[:, r]
                m_new = jnp.maximum(m_prev, jnp.max(sc, axis=0, keepdims=True))
                p_bufs[slot][:, r] = jnp.exp((sc - m_new).astype(BF16))
                a_bufs[slot][:, r] = jnp.exp(m_prev - m_new)
                m_scr[:, r] = m_new

        def weighted_values(c, slot):
            acc_scr[...] = a_bufs[slot][...] * acc_scr[...] + _dot(vt_ref[c], p_bufs[slot][...])

        _three_stage_pipeline(nk, 4, scores, softmax, weighted_values, peel=True)

        acc = acc_scr[...]
        row = lax.broadcasted_iota(jnp.int32, acc.shape, 0)
        l = jnp.sum(jnp.where(row == HEAD_DIM, acc, 0.0), axis=0, keepdims=True)
        o_ref[...] = jnp.where(row < HEAD_DIM, acc / l, 0.0).T.reshape(GROUP, tq, LANES)
        hi, mid, lo = _split3(-(m_scr[...] + jnp.log(l)))
        qa_t = jnp.where(row == HEAD_DIM, hi, jnp.where(row == HEAD_DIM + 1, mid,
                                                        jnp.where(row == HEAD_DIM + 2, lo, q_t)))
        qa_ref[...] = qa_t.T.astype(BF16).reshape(GROUP, tq, LANES)

    qspec = pl.BlockSpec((GROUP, tq, LANES), lambda j, i: (j, i, 0))
    kspec = pl.BlockSpec((None, s, LANES), lambda j, i: (j, 0, 0))
    vtspec = pl.BlockSpec((None, nk, LANES, tk), lambda j, i: (j, 0, 0, 0))
    v_t = jnp.swapaxes(v.reshape(N_KV, nk, tk, LANES), 2, 3)
    return pl.pallas_call(
        body, name="flash_fwd", grid=(N_KV, s // tq),
        in_specs=[qspec, kspec, vtspec],
        out_specs=[qspec, qspec],
        out_shape=[jax.ShapeDtypeStruct((N_HEADS, s, LANES), F32), jax.ShapeDtypeStruct((N_HEADS, s, LANES), BF16)],
        scratch_shapes=[pltpu.VMEM((1, rows), F32), pltpu.VMEM((LANES, rows), F32),
                        pltpu.VMEM((tk, rows), F32), pltpu.VMEM((tk, rows), F32),
                        pltpu.VMEM((tk, rows), BF16), pltpu.VMEM((tk, rows), BF16),
                        pltpu.VMEM((1, rows), F32), pltpu.VMEM((1, rows), F32)],
        compiler_params=_arb(2),
    )(q, k, v_t)


def _groups_fwd(proj_ref, o_ref, hext_ref, cw_ref, cp_ref, sp_ref, sgw_ref, sgb_ref, t):
    proj = proj_ref[...]
    r = {}
    r["att"] = _heads_to_cat(o_ref, N_HEADS)
    r["gate_a"], r["dgate_a"] = _silu_fwd_bwd(proj[:, C_GA:C_A1])
    r["att_g"] = r["att"] * r["gate_a"]
    c0 = jnp.zeros((t, CONV_W), F32) + cp_ref[0:1, :]
    for kk in range(CONV_K):
        c0 = c0 + cw_ref[kk:kk + 1, :] * _tap(hext_ref, kk + 1, t)
    r["xh_c"], r["rs_c"] = _ln_hat(c0)
    r["c1"] = r["xh_c"] * cp_ref[1:2, :] + cp_ref[2:3, :]
    r["sg_c1"] = _sigmoid(r["c1"])
    r["c2"] = r["c1"] * r["sg_c1"]
    r["gate_c"], r["dgate_c"] = _silu_fwd_bwd(proj[:, C_GC:C_U])
    r["cnv_g"] = r["c2"] * r["gate_c"]
    r["gu"], r["dgu"] = _gelu_fwd_bwd(proj[:, C_U:C_VS])
    gv, r["dgv"] = _gelu_fwd_bwd(proj[:, C_VS:C_GS])
    r["xh_s"], r["rs_s"] = _ln_hat(gv)
    v1 = r["xh_s"] * sp_ref[0:1, :] + sp_ref[1:2, :]
    r["v1_bf"] = v1.astype(BF16)
    r["mixed"] = _sgu_mix(r["v1_bf"], sgw_ref, t // SG_CHUNK) + jnp.concatenate([sgb_ref[...]] * (t // SG_CHUNK), axis=0)
    r["um"] = r["gu"] * r["mixed"]
    r["gate_s"], r["dgate_s"] = _silu_fwd_bwd(proj[:, C_GS:D_IN])
    r["sgu_g"] = r["um"] * r["gate_s"]
    r["mc_bf"] = jnp.concatenate([r["att_g"], r["cnv_g"], r["sgu_g"]], axis=1).astype(BF16)
    return r


def _fwd_out(x, proj, o, hc, cw, cp, sp, sgw_bf, sgb, w_out_bf, g_post, t):
    s = x.shape[0]

    def body(x_ref, proj_ref, o_ref, hc_ref, hp_ref, hn_ref, cw_ref, cp_ref, sp_ref, sgw_ref, sgb_ref,
             w_ref, g_ref, mix_ref, xn_ref, hext_ref):
        i = pl.program_id(0)
        _conv_window(hext_ref, hp_ref, hc_ref[...], hn_ref, i == 0, i == pl.num_programs(0) - 1, t)
        r = _groups_fwd(proj_ref, o_ref, hext_ref, cw_ref, cp_ref, sp_ref, sgw_ref, sgb_ref, t)
        mix = _dot(r["mc_bf"], w_ref[...])
        mix_ref[...] = mix
        xn_ref[...] = x_ref[...] + mix * _rms(mix) * g_ref[...]

    row = lambda w: pl.BlockSpec((t, w), lambda i: (i, 0))
    hprev, hnext = _halo_specs(t, s, CONV_W)
    return pl.pallas_call(
        body, name="fwd_out", grid=(s // t,),
        in_specs=[row(D_MODEL), row(D_IN), pl.BlockSpec((N_HEADS, t, LANES), lambda i: (0, i, 0)), row(CONV_W),
                  hprev, hnext, _const_spec((32, CONV_W)), _const_spec((8, CONV_W)), _const_spec((8, SG_W)),
                  _const_spec((4, SG_CHUNK, SG_CHUNK)), _const_spec((SG_CHUNK, SG_W)),
                  _const_spec((D_MODEL, D_MODEL)), _const_spec((1, D_MODEL))],
        out_specs=[row(D_MODEL), row(D_MODEL)],
        out_shape=[jax.ShapeDtypeStruct((s, D_MODEL), F32), jax.ShapeDtypeStruct((s, D_MODEL), F32)],
        scratch_shapes=[pltpu.VMEM((SUBLANES, t + 2 * HALO, CONV_W), F32)],
        compiler_params=_arb(),
    )(x, proj, o, hc, hc, hc, cw, cp, sp, sgw_bf, sgb, w_out_bf, g_post)


def _loss_grad(y, target, t):
    s = y.shape[0]

    def body(y_ref, t_ref, dy_ref, sq_ref):
        @pl.when(pl.program_id(0) == 0)
        def _():
            sq_ref[...] = jnp.zeros_like(sq_ref)

        err = y_ref[...] - t_ref[...]
        dy_ref[...] = err * (1.0 / D_MODEL)
        sq_ref[...] += jnp.sum(err * err, axis=0, keepdims=True)

    row = pl.BlockSpec((t, D_MODEL), lambda i: (i, 0))
    return pl.pallas_call(
        body, name="loss_grad", grid=(s // t,),
        in_specs=[row, row], out_specs=[row, _const_spec((1, D_MODEL))],
        out_shape=[jax.ShapeDtypeStruct((s, D_MODEL), F32), jax.ShapeDtypeStruct((1, D_MODEL), F32)],
        compiler_params=_arb(),
    )(y, target)


def _bwd_out(dy, mix, proj, o, hc, cw, cp, sp, sgw_bf, sgwt_bf, sgb, w_out_t_bf, g_post, t):
    s = dy.shape[0]
    n_chunks = t // SG_CHUNK

    def body(dy_ref, mix_ref, proj_ref, o_ref, hc_ref, hp_ref, hn_ref, cw_ref, cp_ref, sp_ref, sgw_ref, sgwt_ref,
             sgb_ref, wt_ref, g_ref,
             do_ref, dgs_ref, dc0_ref, gwo_ref, gpost_ref, gcw_ref, gvec_ref, gsgw_ref, gsgb_ref, hext_ref):
        i = pl.program_id(0)

        @pl.when(i == 0)
        def _():
            for ref in (gwo_ref, gpost_ref, gcw_ref, gvec_ref, gsgw_ref, gsgb_ref):
                ref[...] = jnp.zeros_like(ref)

        _conv_window(hext_ref, hp_ref, hc_ref[...], hn_ref, i == 0, i == pl.num_programs(0) - 1, t)
        r = _groups_fwd(proj_ref, o_ref, hext_ref, cw_ref, cp_ref, sp_ref, sgw_ref, sgb_ref, t)

        dyv = dy_ref[...]
        mix_v = mix_ref[...]
        rr = _rms(mix_v)
        gd = dyv * g_ref[...]
        dmix = rr * gd - mix_v * (rr * rr * rr * jnp.mean(gd * mix_v, axis=-1, keepdims=True))
        gpost_ref[...] += jnp.sum(dyv * mix_v * rr, axis=0, keepdims=True)
        dmix_bf = dmix.astype(BF16)
        gwo_ref[...] += _dot_tn(r["mc_bf"], dmix_bf)
        dmc = _dot(dmix_bf, wt_ref[...])

        d_att = dmc[:, 0:512]
        dg_att = d_att * r["att"] * r["dgate_a"]
        d_o = d_att * r["gate_a"]
        prod = d_o * r["att"]
        for p in range(N_HEADS // 2):
            sl = slice(p * LANES, (p + 1) * LANES)
            pr = prod[:, sl]
            tot = jnp.sum(pr, axis=1, keepdims=True)
            ev = jnp.sum(jnp.where(_lane(pr.shape) < HEAD_DIM, pr, 0.0), axis=1, keepdims=True)
            for odd, delta in ((False, ev), (True, tot - ev)):
                hi, mid, lo = _split3(-delta)
                do_ref[2 * p + int(odd)] = _with_spare(_head_slab(d_o[:, sl], odd), hi, mid, lo).astype(BF16)

        dcv = dmc[:, 512:768]
        dg_conv = dcv * r["c2"] * r["dgate_c"]
        dc1 = dcv * r["gate_c"] * (r["sg_c1"] * (1.0 + r["c1"] * (1.0 - r["sg_c1"])))
        dc0 = _ln_bwd(dc1 * cp_ref[1:2, :], r["xh_c"], r["rs_c"])
        dc0_ref[...] = dc0
        for kk in range(CONV_K):
            gcw_ref[kk:kk + 1, :] += jnp.sum(dc0 * _tap(hext_ref, kk + 1, t), axis=0, keepdims=True)

        dsg = dmc[:, 768:1024]
        dg_sg = dsg * r["um"] * r["dgate_s"]
        du = dsg * r["mixed"] * r["gate_s"] * r["dgu"]
        dmx = dsg * r["gu"] * r["gate_s"]
        dmx_bf = dmx.astype(BF16)
        sgb_sum = dmx[0:SG_CHUNK, :]
        for n in range(1, n_chunks):
            sgb_sum = sgb_sum + dmx[n * SG_CHUNK:(n + 1) * SG_CHUNK, :]
        gsgb_ref[...] += sgb_sum
        dv1_rows = []
        for n in range(n_chunks):
            pairs = []
            for p in range(SG_W // LANES):
                rs_ = slice(n * SG_CHUNK, (n + 1) * SG_CHUNK)
                ls_ = slice(p * LANES, (p + 1) * LANES)
                dm = dmx_bf[rs_, ls_]
                xp = r["v1_bf"][rs_, ls_]
                low = _lane(dm.shape) < HEAD_DIM
                zero = jnp.zeros_like(dm)
                gsgw_ref[2 * p] += _dot_nt(jnp.where(low, dm, zero), xp)
                gsgw_ref[2 * p + 1] += _dot_nt(jnp.where(low, zero, dm), xp)
                pairs.append(jnp.where(low, _dot(sgwt_ref[2 * p], dm), _dot(sgwt_ref[2 * p + 1], dm)))
            dv1_rows.append(jnp.concatenate(pairs, axis=1))
        dv1 = dv1_rows[0] if n_chunks == 1 else jnp.concatenate(dv1_rows, axis=0)
        dvs = _ln_bwd(dv1 * sp_ref[0:1, :], r["xh_s"], r["rs_s"]) * r["dgv"]

        zrow = jnp.zeros((1, CONV_W), F32)
        gvec_ref[...] += jnp.concatenate([
            jnp.sum(dc0, axis=0, keepdims=True),
            jnp.sum(dc1 * r["xh_c"], axis=0, keepdims=True),
            jnp.sum(dc1, axis=0, keepdims=True),
            jnp.sum(dv1 * r["xh_s"], axis=0, keepdims=True),
            jnp.sum(dv1, axis=0, keepdims=True),
            zrow, zrow, zrow], axis=0)
        dgs_ref[...] = jnp.concatenate([dg_att, dg_conv, du, dvs, dg_sg], axis=1)

    row = lambda w: pl.BlockSpec((t, w), lambda i: (i, 0))
    heads = pl.BlockSpec((N_HEADS, t, LANES), lambda i: (0, i, 0))
    hprev, hnext = _halo_specs(t, s, CONV_W)
    return pl.pallas_call(
        body, name="bwd_out", grid=(s // t,),
        in_specs=[row(D_MODEL), row(D_MODEL), row(D_IN), heads, row(CONV_W), hprev, hnext,
                  _const_spec((32, CONV_W)), _const_spec((8, CONV_W)), _const_spec((8, SG_W)),
                  _const_spec((4, SG_CHUNK, SG_CHUNK)), _const_spec((4, SG_CHUNK, SG_CHUNK)),
                  _const_spec((SG_CHUNK, SG_W)), _const_spec((D_MODEL, D_MODEL)), _const_spec((1, D_MODEL))],
        out_specs=[heads, row(1536), row(CONV_W), _const_spec((D_MODEL, D_MODEL)), _const_spec((1, D_MODEL)),
                   _const_spec((32, CONV_W)), _const_spec((8, CONV_W)), _const_spec((4, SG_CHUNK, SG_CHUNK)),
                   _const_spec((SG_CHUNK, SG_W))],
        out_shape=[jax.ShapeDtypeStruct((N_HEADS, s, LANES), BF16), jax.ShapeDtypeStruct((s, 1536), F32),
                   jax.ShapeDtypeStruct((s, CONV_W), F32), jax.ShapeDtypeStruct((D_MODEL, D_MODEL), F32),
                   jax.ShapeDtypeStruct((1, D_MODEL), F32), jax.ShapeDtypeStruct((32, CONV_W), F32),
                   jax.ShapeDtypeStruct((8, CONV_W), F32), jax.ShapeDtypeStruct((4, SG_CHUNK, SG_CHUNK), F32),
                   jax.ShapeDtypeStruct((SG_CHUNK, SG_W), F32)],
        scratch_shapes=[pltpu.VMEM((SUBLANES, t + 2 * HALO, CONV_W), F32)],
        compiler_params=_arb(),
    )(dy, mix, proj, o, hc, hc, hc, cw, cp, sp, sgw_bf, sgwt_bf, sgb, w_out_t_bf, g_post)


def _flash_bwd(qa, doa, k, v, tq, tk):
    s = qa.shape[1]
    rows = GROUP * tq
    nk = s // tk
    n_q = s // tq

    def body(qa_ref, do_ref, k_ref, v_ref, dq_ref, dk_hbm, dv_hbm,
             dq_scr, dk_scr, dv_scr, s0, s1, d0, d1, p0, p1, e0, e1, sems):
        j, i = pl.program_id(0), pl.program_id(1)
        s_bufs, d_bufs, p_bufs, e_bufs = (s0, s1), (d0, d1), (p0, p1), (e0, e1)
        qv = qa_ref[...].reshape(rows, LANES)
        dov = do_ref[...].reshape(rows, LANES)
        q_t = qv.astype(F32).T.astype(BF16)
        do_t = dov.astype(F32).T.astype(BF16)
        dq_scr[...] = jnp.zeros((rows, LANES), F32)

        @pl.when(i == 0)
        def _():
            dk_scr[...] = jnp.zeros_like(dk_scr)
            dv_scr[...] = jnp.zeros_like(dv_scr)

        def at(c):
            return _chunk_rows(c, tk)

        def scores(c, slot):
            s_bufs[slot][...] = _dot_nt(qv, k_ref[at(c), :])
            d_bufs[slot][...] = _dot_nt(dov, v_ref[at(c), :])

        def probs(slot):
            for h in range(GROUP):
                r = slice(h * tq, (h + 1) * tq)
                p = jnp.exp(s_bufs[slot][r, :])
                p_bufs[slot][r, :] = p.astype(BF16)
                e_bufs[slot][r, :] = (p * d_bufs[slot][r, :]).astype(BF16)

        def grads(c, slot):
            ds = e_bufs[slot][...]
            dq_scr[...] += _dot(ds, k_ref[at(c), :])
            dv_scr[c] += _dot(do_t, p_bufs[slot][...])
            dk_scr[c] += _dot(q_t, ds)

        p1[...] = jnp.zeros((rows, tk), BF16)
        e1[...] = jnp.zeros((rows, tk), BF16)
        _three_stage_pipeline(nk, 2, scores, probs, grads, peel=False)
        dq_ref[...] = dq_scr[...].reshape(GROUP, tq, LANES)

        @pl.when(i == n_q - 1)
        def _():
            out = [pltpu.make_async_copy(dk_scr, dk_hbm.at[j], sems.at[0]),
                   pltpu.make_async_copy(dv_scr, dv_hbm.at[j], sems.at[1])]
            for cp in out:
                cp.start()
            for cp in out:
                cp.wait()

    qspec = pl.BlockSpec((GROUP, tq, LANES), lambda j, i: (j, i, 0))
    kvspec = pl.BlockSpec((None, s, LANES), lambda j, i: (j, 0, 0))
    hbm = pl.BlockSpec(memory_space=pl.ANY)
    stage_f32 = pltpu.VMEM((rows, tk), F32)
    stage_bf = pltpu.VMEM((rows, tk), BF16)
    kv_t = jax.ShapeDtypeStruct((N_KV, nk, LANES, tk), F32)
    dq, dk_t, dv_t = pl.pallas_call(
        body, name="flash_bwd", grid=(N_KV, n_q),
        in_specs=[qspec, qspec, kvspec, kvspec],
        out_specs=[qspec, hbm, hbm],
        out_shape=[jax.ShapeDtypeStruct((N_HEADS, s, LANES), F32), kv_t, kv_t],
        scratch_shapes=[pltpu.VMEM((rows, LANES), F32), pltpu.VMEM((nk, LANES, tk), F32), pltpu.VMEM((nk, LANES, tk), F32),
                        stage_f32, stage_f32, stage_f32, stage_f32, stage_bf, stage_bf, stage_bf, stage_bf,
                        pltpu.SemaphoreType.DMA((2,))],
        compiler_params=_arb(2),
    )(qa, doa, k, v)
    untranspose = lambda a: jnp.swapaxes(a, 2, 3).reshape(N_KV, s, LANES)
    return dq, untranspose(dk_t), untranspose(dv_t)


def _bwd_in(dy, x, proj, dq, dk, dv, dgs, dc0, cw, g_pre, w_in_t_bf, bd, qg, kg, cos, sin, t):
    s = x.shape[0]

    def body(dy_ref, x_ref, proj_ref, dq_ref, dk_ref, dv_ref, dgs_ref, dc_ref, dcp_ref, dcn_ref, cw_ref, g_ref,
             wt_ref, bd_ref, qg_ref, kg_ref, cos_ref, sin_ref,
             dx_ref, dproj_ref, h_ref, gpre_ref, gq_ref, gk_ref, dext_ref):
        i = pl.program_id(0)

        @pl.when(i == 0)
        def _():
            for ref in (gpre_ref, gq_ref, gk_ref):
                ref[...] = jnp.zeros_like(ref)

        proj = proj_ref[...]
        cos_pair = cos_ref[...]
        sin_pair = sin_ref[...]
        cosv = jnp.concatenate([cos_pair] * (N_HEADS // 2), axis=1)
        sinv = jnp.concatenate([sin_pair] * (N_HEADS // 2), axis=1)

        def head_norm_bwd(dr, z, bdm, g, cs, sn, gacc_ref):
            dn = dr * cs + _swap16(dr * sn)
            rr = lax.rsqrt(_group_mean(z * z, bdm) + EPS)
            gdn = dn * g
            gacc_ref[...] += jnp.sum(dn * z * rr, axis=0, keepdims=True)
            return rr * gdn - z * (rr * rr * rr * _group_mean(gdn * z, bdm))

        dq_cat = _heads_to_cat(dq_ref, N_HEADS) * ATT_SCALE
        dzq = head_norm_bwd(dq_cat, proj[:, C_Q:C_K], bd_ref[...], qg_ref[...], cosv, sinv, gq_ref)
        dk_cat = _heads_to_cat(dk_ref, N_KV)
        dzk = head_norm_bwd(dk_cat, proj[:, C_K:C_V], bd_ref[0:LANES, 0:LANES], kg_ref[...],
                            cos_pair, sin_pair, gk_ref)
        dv_cat = _heads_to_cat(dv_ref, N_KV)

        _conv_window(dext_ref, dcp_ref, dc_ref[...], dcn_ref, i == 0, i == pl.num_programs(0) - 1, t)
        dhc = jnp.zeros((t, CONV_W), F32)
        for kk in range(CONV_K):
            dhc = dhc + cw_ref[kk:kk + 1, :] * _tap(dext_ref, CONV_K - kk, t)
        sg = _sigmoid(proj[:, C_A2:C_GC])
        da1 = dhc * sg
        da2 = dhc * proj[:, C_A1:C_A2] * sg * (1.0 - sg)

        dgs = dgs_ref[...]
        dproj_bf = jnp.concatenate([dzq, dzk, dv_cat, dgs[:, 0:512], da1, da2, dgs[:, 512:1536]], axis=1).astype(BF16)
        dproj_ref[...] = dproj_bf
        dh = _dot(dproj_bf, wt_ref[...])

        xv = x_ref[...]
        rr = _rms(xv)
        gv = g_ref[...]
        h_ref[...] = (xv * rr * gv).astype(BF16)
        gdh = dh * gv
        gpre_ref[...] += jnp.sum(dh * xv * rr, axis=0, keepdims=True)
        dx_ref[...] = dy_ref[...] + rr * gdh - xv * (rr * rr * rr * jnp.mean(gdh * xv, axis=-1, keepdims=True))

    row = lambda w: pl.BlockSpec((t, w), lambda i: (i, 0))
    heads = lambda n: pl.BlockSpec((n, t, LANES), lambda i: (0, i, 0))
    hprev, hnext = _halo_specs(t, s, CONV_W)
    return pl.pallas_call(
        body, name="bwd_in", grid=(s // t,),
        in_specs=[row(D_MODEL), row(D_MODEL), row(D_IN), heads(N_HEADS), heads(N_KV), heads(N_KV), row(1536),
                  row(CONV_W), hprev, hnext, _const_spec((32, CONV_W)), _const_spec((1, D_MODEL)),
                  _const_spec((D_IN, D_MODEL)), _const_spec((512, 512)), _const_spec((1, 512)),
                  _const_spec((1, LANES)), row(LANES), row(LANES)],
        out_specs=[row(D_MODEL), row(D_IN), row(D_MODEL), _const_spec((1, D_MODEL)), _const_spec((1, 512)),
                   _const_spec((1, LANES))],
        out_shape=[jax.ShapeDtypeStruct((s, D_MODEL), F32), jax.ShapeDtypeStruct((s, D_IN), BF16),
                   jax.ShapeDtypeStruct((s, D_MODEL), BF16), jax.ShapeDtypeStruct((1, D_MODEL), F32),
                   jax.ShapeDtypeStruct((1, 512), F32), jax.ShapeDtypeStruct((1, LANES), F32)],
        scratch_shapes=[pltpu.VMEM((SUBLANES, t + 2 * HALO, CONV_W), F32)],
        compiler_params=_arb(),
    )(dy, x, proj, dq, dk, dv, dgs, dc0, dc0, dc0, cw, g_pre, w_in_t_bf, bd, qg, kg, cos, sin)


def _grad_w_in(h_bf, dproj_bf, t):
    s = h_bf.shape[0]
    half = D_IN // 2

    def body(h_ref, d_ref, g_ref):
        @pl.when(pl.program_id(1) == 0)
        def _():
            g_ref[...] = jnp.zeros_like(g_ref)

        g_ref[...] += _dot_tn(h_ref[...], d_ref[...])

    return pl.pallas_call(
        body, name="grad_w_in", grid=(2, s // t),
        in_specs=[pl.BlockSpec((t, D_MODEL), lambda j, i: (i, 0)), pl.BlockSpec((t, half), lambda j, i: (i, j))],
        out_specs=pl.BlockSpec((D_MODEL, half), lambda j, i: (0, j)),
        out_shape=jax.ShapeDtypeStruct((D_MODEL, D_IN), F32),
        compiler_params=_arb(2),
    )(h_bf, dproj_bf)


def _place():
    x, y, c = lax.axis_index("x"), lax.axis_index("y"), lax.axis_index("c")
    chips = [(1 - x, y), (x, 1 - y), (1 - x, 1 - y)]
    return x, y, c, chips


def _any_specs(n):
    return [pl.BlockSpec(memory_space=pl.ANY)] * n


def _gather_weights(w_in_bf, w_out_bf, cdw):
    arrs = (w_in_bf, w_out_bf, cdw)
    n = len(arrs)

    def body(*refs):
        ins, outs = refs[:n], refs[n:2 * n]
        send_sems, recv_sems, local_sems = refs[2 * n:]
        x, y, c, chips = _place()
        mine = 2 * x + y
        local = [pltpu.make_async_copy(ins[a], outs[a].at[mine], local_sems.at[a]) for a in range(n)]
        for cp in local:
            cp.start()

        def copy(j, a, slot, to):
            return pltpu.make_async_remote_copy(src_ref=ins[a], dst_ref=outs[a].at[slot], send_sem=send_sems.at[n * j + a],
                                                recv_sem=recv_sems.at[n * j + a], device_id=to, device_id_type=MESH)

        sends = [copy(j, a, mine, (px, py, c)) for j, (px, py) in enumerate(chips) for a in range(n)]
        for cp in sends:
            cp.start()
        for j, (px, py) in enumerate(chips):
            for a in range(n):
                copy(j, a, 2 * px + py, (px, py, c)).wait_recv()
        for cp in sends:
            cp.wait_send()
        for cp in local:
            cp.wait()

    return pl.pallas_call(
        body, name="gather_weights",
        in_specs=_any_specs(n), out_specs=_any_specs(n),
        out_shape=[jax.ShapeDtypeStruct((N_CHIPS,) + a.shape, a.dtype) for a in arrs],
        scratch_shapes=[pltpu.SemaphoreType.DMA((3 * n,)), pltpu.SemaphoreType.DMA((3 * n,)), pltpu.SemaphoreType.DMA((n,))],
    )(*arrs)


def _scatter_grads(gin_pieces, gw_out):
    depth = gw_out.shape[0]
    rows = gw_out.shape[1] // N_CHIPS

    def body(gin_ref, gout_ref, rin_ref, rout_ref, send_sems, recv_sems, local_sems):
        x, y, c, chips = _place()
        mine = 2 * x + y

        def out_rows(chip):
            return gout_ref.at[:, pl.ds(pl.multiple_of(chip * rows, rows), rows), :]

        local = [pltpu.make_async_copy(gin_ref.at[mine], rin_ref.at[mine], local_sems.at[0]),
                 pltpu.make_async_copy(out_rows(mine), rout_ref.at[mine], local_sems.at[1])]
        for cp in local:
            cp.start()

        def copies(j, shard, slot, to):
            kw = dict(device_id=to, device_id_type=MESH)
            return [pltpu.make_async_remote_copy(src_ref=gin_ref.at[shard], dst_ref=rin_ref.at[slot],
                                                 send_sem=send_sems.at[2 * j], recv_sem=recv_sems.at[2 * j], **kw),
                    pltpu.make_async_remote_copy(src_ref=out_rows(shard), dst_ref=rout_ref.at[slot],
                                                 send_sem=send_sems.at[2 * j + 1], recv_sem=recv_sems.at[2 * j + 1], **kw)]

        sends = [cp for j, (px, py) in enumerate(chips) for cp in copies(j, 2 * px + py, mine, (px, py, c))]
        for cp in sends:
            cp.start()
        for j, (px, py) in enumerate(chips):
            for cp in copies(j, mine, 2 * px + py, (px, py, c)):
                cp.wait_recv()
        for cp in sends:
            cp.wait_send()
        for cp in local:
            cp.wait()

    return pl.pallas_call(
        body, name="scatter_grads",
        in_specs=_any_specs(2), out_specs=_any_specs(2),
        out_shape=[jax.ShapeDtypeStruct(gin_pieces.shape, gin_pieces.dtype),
                   jax.ShapeDtypeStruct((N_CHIPS, depth, rows, gw_out.shape[2]), gw_out.dtype)],
        scratch_shapes=[pltpu.SemaphoreType.DMA((6,)), pltpu.SemaphoreType.DMA((6,)), pltpu.SemaphoreType.DMA((2,))],
    )(gin_pieces, gw_out)


def _sum_chips(parts, rb, name):
    _, depth, r, cdim = parts.shape

    def body(p_ref, o_ref):
        part = lambda j: p_ref[j].astype(F32)
        o_ref[...] = ((part(0) + part(1)) + part(2)) + part(3)

    return pl.pallas_call(
        body, name=name, grid=(depth, r // rb),
        in_specs=[pl.BlockSpec((N_CHIPS, None, rb, cdim), lambda l, i: (0, l, i, 0))],
        out_specs=pl.BlockSpec((None, rb, cdim), lambda l, i: (l, i, 0)),
        out_shape=jax.ShapeDtypeStruct((depth, r, cdim), F32),
        compiler_params=_arb(2),
    )(parts)


def _swap_with_sibling(a, b):
    arrs = (a, b)
    n = len(arrs)

    def body(*refs):
        ins, outs = refs[:n], refs[n:2 * n]
        send_sems, recv_sems = refs[2 * n:]
        x, y, c, _ = _place()
        cps = [pltpu.make_async_remote_copy(src_ref=ins[k], dst_ref=outs[k], send_sem=send_sems.at[k],
                                            recv_sem=recv_sems.at[k], device_id=(x, y, 1 - c), device_id_type=MESH)
               for k in range(n)]
        for cp in cps:
            cp.start()
        for cp in cps:
            cp.wait()

    return pl.pallas_call(
        body, name="swap_with_sibling",
        in_specs=_any_specs(n), out_specs=_any_specs(n),
        out_shape=[jax.ShapeDtypeStruct(v.shape, v.dtype) for v in arrs],
        scratch_shapes=[pltpu.SemaphoreType.DMA((n,)), pltpu.SemaphoreType.DMA((n,))],
    )(*arrs)


def _allreduce_small(slab):
    m, n = slab.shape

    def body(x_ref, out_ref, gath, send_sems, recv_sems, local_sem):
        x, y, c, chips = _place()
        me, sibling = (x, y, c), (x, y, 1 - c)

        def rows(px, py, pc):
            return gath.at[pl.ds(pl.multiple_of((4 * px + 2 * py + pc) * m, 8), m), :]

        def copy(k, block, to, src=None):
            return pltpu.make_async_remote_copy(src_ref=rows(*block) if src is None else src, dst_ref=rows(*block),
                                                send_sem=send_sems.at[k], recv_sem=recv_sems.at[k],
                                                device_id=to, device_id_type=MESH)

        mine = pltpu.make_async_copy(x_ref, rows(*me), local_sem)
        mine.start()
        first = [copy(0, me, sibling, src=x_ref)]
        first += [copy(1 + j, me, (*chip, c), src=x_ref) for j, chip in enumerate(chips)]
        for cp in first:
            cp.start()
        passed = [copy(4 + j, (*chip, c), sibling) for j, chip in enumerate(chips)]
        for j, chip in enumerate(chips):
            copy(1 + j, (*chip, c), me).wait_recv()
            passed[j].start()
        copy(0, sibling, me).wait_recv()
        for j, chip in enumerate(chips):
            copy(4 + j, (*chip, 1 - c), me).wait_recv()
        for cp in first + passed:
            cp.wait_send()
        mine.wait()
        total = gath[0:m, :]
        for d in range(1, N_DEV):
            total = total + gath[d * m:(d + 1) * m, :]
        out_ref[...] = total

    return pl.pallas_call(
        body, name="allreduce_small",
        in_specs=[pl.BlockSpec(memory_space=pltpu.VMEM)],
        out_specs=pl.BlockSpec(memory_space=pltpu.VMEM),
        out_shape=jax.ShapeDtypeStruct((m, n), F32),
        scratch_shapes=[pltpu.VMEM((N_DEV * m, n), F32), pltpu.SemaphoreType.DMA((7,)), pltpu.SemaphoreType.DMA((7,)),
                        pltpu.SemaphoreType.DMA],
    )(slab)


def _adamw(w, ga, gb, m, v, rb, name):
    depth, r, cdim = w.shape

    def body(w_ref, ga_ref, gb_ref, m_ref, v_ref, g_out, d_out, m_out, v_out):
        g = ga_ref[...] + gb_ref[...]
        m2 = ADAM_B1 * m_ref[...] + (1.0 - ADAM_B1) * g
        v2 = ADAM_B2 * v_ref[...] + (1.0 - ADAM_B2) * (g * g)
        m_hat = m2 / (1.0 - ADAM_B1 ** ADAM_STEP)
        v_hat = v2 / (1.0 - ADAM_B2 ** ADAM_STEP)
        g_out[...] = g
        d_out[...] = -ADAM_LR * (m_hat / (jnp.sqrt(v_hat) + ADAM_EPS) + ADAM_WD * w_ref[...])
        m_out[...] = m2
        v_out[...] = v2

    spec = pl.BlockSpec((None, rb, cdim), lambda l, i: (l, i, 0))
    shp = jax.ShapeDtypeStruct((depth, r, cdim), F32)
    return pl.pallas_call(
        body, name=name, grid=(depth, r // rb),
        in_specs=[spec] * 5, out_specs=[spec] * 4, out_shape=[shp] * 4,
        compiler_params=_arb(2),
    )(w, ga, gb, m, v)


def _rope_tables(s):
    t = jnp.arange(s, dtype=jnp.int32)
    row = (t // GRID_W).astype(F32)
    col = (t % GRID_W).astype(F32)
    half = HEAD_DIM // 4
    inv_freq = ROPE_THETA ** (-jnp.arange(half, dtype=F32) / half)
    ar = row[:, None] * inv_freq[None, :]
    ac = col[:, None] * inv_freq[None, :]
    cos = jnp.concatenate([jnp.cos(ar), jnp.cos(ar), jnp.cos(ac), jnp.cos(ac)], axis=1)
    sin = jnp.concatenate([-jnp.sin(ar), jnp.sin(ar), -jnp.sin(ac), jnp.sin(ac)], axis=1)
    return jnp.tile(cos, (1, 2)), jnp.tile(sin, (1, 2))


def _pad_rows(a, rows):
    return jnp.concatenate([a, jnp.zeros((rows - a.shape[0],) + a.shape[1:], a.dtype)], axis=0)


_SMALL = ("pre_norm", "post_norm", "q_norm", "k_norm", "conv_dw_b", "conv_ln_g", "conv_ln_b", "sg_ln_g", "sg_ln_b",
          "sg_w", "sg_b")


def _pack(parts):
    flat = jnp.concatenate([p.reshape(-1, LANES) for p in parts], axis=0)
    return _pad_rows(flat, -(-flat.shape[0] // 8) * 8)


def _unpack(slab, shapes):
    out, r = [], 0
    for shp in shapes:
        n = 1
        for d in shp:
            n *= d
        out.append(slab[r:r + n // LANES].reshape(shp))
        r += n // LANES
    return out


def kernel(x, pre_norm, post_norm, w_in, w_out, q_norm, k_norm, conv_dw, conv_dw_b, conv_ln_g, conv_ln_b, sg_ln_g, sg_ln_b, sg_w, sg_b, loss_target, m_pre_norm, m_post_norm, m_w_in, m_w_out, m_q_norm, m_k_norm, m_conv_dw, m_conv_dw_b, m_conv_ln_g, m_conv_ln_b, m_sg_ln_g, m_sg_ln_b, m_sg_w, m_sg_b, v_pre_norm, v_post_norm, v_w_in, v_w_out, v_q_norm, v_k_norm, v_conv_dw, v_conv_dw_b, v_conv_ln_g, v_conv_ln_b, v_sg_ln_g, v_sg_ln_b, v_sg_w, v_sg_b):
    depth = w_in.shape[0]
    s = x.shape[1]
    assert x.shape[0] == 1 and s % SG_CHUNK == 0 and x.shape[2] == D_MODEL
    t = min(256, s)
    tq = min(256, s)
    tk = min(512, s // 2)
    shard_cols = w_in.shape[2]
    chip = 2 * lax.axis_index("x") + lax.axis_index("y")

    gin, gout, gcdw = _gather_weights(w_in.astype(BF16), w_out.astype(BF16), conv_dw)
    w_in_bf = jnp.concatenate([gin[j] for j in range(N_CHIPS)], axis=2)
    w_out_bf = jnp.concatenate([gout[j] for j in range(N_CHIPS)], axis=1)
    cdw_full = jnp.concatenate([gcdw[j] for j in range(N_CHIPS)], axis=2)
    w_in_t_bf = jnp.swapaxes(w_in_bf, 1, 2)
    w_out_t_bf = jnp.swapaxes(w_out_bf, 1, 2)
    sgw_bf = sg_w.astype(BF16)
    sgwt_bf = jnp.swapaxes(sg_w, 2, 3).astype(BF16)

    cos, sin = _rope_tables(s)
    bd = jnp.kron(jnp.eye(N_HEADS, dtype=F32), jnp.full((HEAD_DIM, HEAD_DIM), 1.0 / HEAD_DIM, F32)).astype(BF16)

    def layer_consts(l):
        cw = _pad_rows(cdw_full[l], 32)
        cp = _pad_rows(jnp.stack([conv_dw_b[l], conv_ln_g[l], conv_ln_b[l]]), 8)
        sp = _pad_rows(jnp.stack([sg_ln_g[l], sg_ln_b[l]]), 8)
        sgb = jnp.repeat(sg_b[l].T, HEAD_DIM, axis=1)
        qg = jnp.tile(q_norm[l], N_HEADS)[None, :]
        kg = jnp.tile(k_norm[l], N_KV)[None, :]
        return cw, cp, sp, sgb, qg, kg

    xs = [x[0]]
    saved = []
    for l in range(depth):
        cw, cp, sp, sgb, qg, kg = layer_consts(l)
        proj, hc, q, k, v = _fwd_in(xs[l], pre_norm[l][None, :], w_in_bf[l], bd, qg, kg, cos, sin, t)
        o, qa = _flash_fwd(q, k, v, tq, tk)
        mix, xn = _fwd_out(xs[l], proj, o, hc, cw, cp, sp, sgw_bf[l], sgb, w_out_bf[l], post_norm[l][None, :], t)
        saved.append((proj, hc, qa, k, v, o, mix))
        xs.append(xn)

    dy, sq = _loss_grad(xs[depth], loss_target[0], t)
    loss = lax.psum(0.5 * jnp.sum(sq) / D_MODEL, ("x", "y", "c"))

    g_w_in, g_w_out, g_small = [], [], {n: [] for n in _SMALL + ("conv_dw",)}
    for l in reversed(range(depth)):
        cw, cp, sp, sgb, qg, kg = layer_consts(l)
        proj, hc, qa, k, v, o, mix = saved[l]
        doa, dgs, dc0, gwo, gpost, gcw, gvec, gsgw, gsgb = _bwd_out(
            dy, mix, proj, o, hc, cw, cp, sp, sgw_bf[l], sgwt_bf[l], sgb, w_out_t_bf[l], post_norm[l][None, :], t)
        dq, dk, dv = _flash_bwd(qa, doa, k, v, tq, tk)
        dy, dproj_bf, h_bf, gpre, gq, gk = _bwd_in(dy, xs[l], proj, dq, dk, dv, dgs, dc0, cw, pre_norm[l][None, :],
                                                  w_in_t_bf[l], bd, qg, kg, cos, sin, t)
        g_w_in.append(_grad_w_in(h_bf, dproj_bf, min(512, s)))
        g_w_out.append(gwo)
        g_small["pre_norm"].append(gpre[0])
        g_small["post_norm"].append(gpost[0])
        g_small["q_norm"].append(gq[0].reshape(N_HEADS, HEAD_DIM).sum(0))
        g_small["k_norm"].append(gk[0].reshape(N_KV, HEAD_DIM).sum(0))
        g_small["conv_dw"].append(gcw[:CONV_K])
        g_small["conv_dw_b"].append(gvec[0])
        g_small["conv_ln_g"].append(gvec[1])
        g_small["conv_ln_b"].append(gvec[2])
        g_small["sg_ln_g"].append(gvec[3])
        g_small["sg_ln_b"].append(gvec[4])
        g_small["sg_w"].append(gsgw)
        g_small["sg_b"].append(gsgb.reshape(SG_CHUNK, SG_W // HEAD_DIM, HEAD_DIM).sum(-1).T)
    grad_x = dy[None]
    g_w_in = jnp.stack(g_w_in[::-1])
    g_w_out = jnp.stack(g_w_out[::-1])
    g_small = {n: jnp.stack(vals[::-1]) for n, vals in g_small.items()}

    gin_pieces = jnp.stack([g_w_in[:, :, j * shard_cols:(j + 1) * shard_cols] for j in range(N_CHIPS)])
    rin, rout = _scatter_grads(gin_pieces.astype(BF16), g_w_out.astype(BF16))
    s_in = _sum_chips(rin, 256, "sum_chips_w_in")
    s_out = _sum_chips(rout, 256, "sum_chips_w_out")
    t_in, t_out = _swap_with_sibling(s_in, s_out)
    grad_w_in, delta_w_in, new_m_w_in, new_v_w_in = _adamw(w_in, s_in, t_in, m_w_in, v_w_in, 256, "adamw_w_in")
    grad_w_out, delta_w_out, new_m_w_out, new_v_w_out = _adamw(w_out, s_out, t_out, m_w_out, v_w_out, 256, "adamw_w_out")

    small_w = dict(pre_norm=pre_norm, post_norm=post_norm, q_norm=q_norm, k_norm=k_norm, conv_dw_b=conv_dw_b,
                   conv_ln_g=conv_ln_g, conv_ln_b=conv_ln_b, sg_ln_g=sg_ln_g, sg_ln_b=sg_ln_b, sg_w=sg_w, sg_b=sg_b)
    small_m = dict(pre_norm=m_pre_norm, post_norm=m_post_norm, q_norm=m_q_norm, k_norm=m_k_norm, conv_dw_b=m_conv_dw_b,
                   conv_ln_g=m_conv_ln_g, conv_ln_b=m_conv_ln_b, sg_ln_g=m_sg_ln_g, sg_ln_b=m_sg_ln_b, sg_w=m_sg_w,
                   sg_b=m_sg_b)
    small_v = dict(pre_norm=v_pre_norm, post_norm=v_post_norm, q_norm=v_q_norm, k_norm=v_k_norm, conv_dw_b=v_conv_dw_b,
                   conv_ln_g=v_conv_ln_g, conv_ln_b=v_conv_ln_b, sg_ln_g=v_sg_ln_g, sg_ln_b=v_sg_ln_b, sg_w=v_sg_w,
                   sg_b=v_sg_b)
    shapes = [small_w[n].shape for n in _SMALL]
    red = _allreduce_small(_pack([g_small[n] for n in _SMALL] + [g_small["conv_dw"]]))
    n_rep = sum(small_w[n].size for n in _SMALL) // LANES
    g_cdw_full = red[n_rep:n_rep + g_small["conv_dw"].size // LANES].reshape(g_small["conv_dw"].shape)
    cdw_cols = conv_dw.shape[2]
    g_cdw = lax.dynamic_slice_in_dim(g_cdw_full, chip * cdw_cols, cdw_cols, axis=2)
    g_slab = _pack([red[:n_rep], g_cdw])
    w_slab = _pack([small_w[n] for n in _SMALL] + [conv_dw])
    m_slab = _pack([small_m[n] for n in _SMALL] + [m_conv_dw])
    v_slab = _pack([small_v[n] for n in _SMALL] + [v_conv_dw])
    rows = w_slab.shape[0]
    outs = _adamw(w_slab[None], g_slab[None], jnp.zeros_like(g_slab)[None], m_slab[None], v_slab[None], rows, "adamw_small")
    unpacked = [dict(zip(_SMALL + ("conv_dw",), _unpack(o_[0], shapes + [conv_dw.shape]))) for o_ in outs]

    big = [dict(w_in=a, w_out=b) for a, b in ((grad_w_in, grad_w_out), (delta_w_in, delta_w_out),
                                             (new_m_w_in, new_m_w_out), (new_v_w_in, new_v_w_out))]
    order = ("pre_norm", "post_norm", "w_in", "w_out", "q_norm", "k_norm", "conv_dw", "conv_dw_b", "conv_ln_g",
             "conv_ln_b", "sg_ln_g", "sg_ln_b", "sg_w", "sg_b")
    result = [loss, grad_x]
    for kind in range(4):
        for name in order:
            result.append(big[kind][name] if name in big[kind] else unpacked[kind][name])
    return tuple(result)
```

```python
import functools

import jax
import jax.numpy as jnp
from jax import lax
from jax.experimental import pallas as pl
from jax.experimental.pallas import tpu as pltpu

F32 = jnp.float32
BF16 = jnp.bfloat16
MESH = pl.DeviceIdType.MESH

EPS = 1e-6
D_MODEL = 1024
D_IN = 2816
HEAD_DIM = 64
LANES = 128
SUBLANES = 8
N_HEADS = 8
N_KV = 2
GROUP = N_HEADS // N_KV
GRID_W = 64
ROPE_THETA = 10000.0
CONV_K = 31
CONV_W = 256
SG_W = 256
SG_CHUNK = 128
HALO = 16
ATT_SCALE = HEAD_DIM ** -0.5

C_Q, C_K, C_V, C_GA, C_A1, C_A2, C_GC, C_U, C_VS, C_GS = 0, 512, 640, 768, 1280, 1536, 1792, 2048, 2304, 2560

ADAM_LR = 0.001
ADAM_B1 = 0.9
ADAM_B2 = 0.999
ADAM_EPS = 1e-08
ADAM_WD = 0.01
ADAM_STEP = 10

N_CHIPS = 4
N_DEV = 8


def _dot(a, b):
    return jnp.dot(a, b, preferred_element_type=F32)


def _group_mean(x, bd_bf):
    hi = x.astype(BF16)
    lo = (x - hi.astype(F32)).astype(BF16)
    return _dot(hi, bd_bf) + _dot(lo, bd_bf)


def _dot_nt(a, b):
    return lax.dot_general(a, b, (((1,), (1,)), ((), ())), preferred_element_type=F32)


def _dot_tn(a, b):
    return lax.dot_general(a, b, (((0,), (0,)), ((), ())), preferred_element_type=F32)


def _lane(shape):
    return lax.broadcasted_iota(jnp.int32, shape, 1)


def _sigmoid(x):
    return 1.0 / (1.0 + jnp.exp(-x))


def _silu_fwd_bwd(x):
    s = _sigmoid(x)
    return x * s, s * (1.0 + x * (1.0 - s))


def _erf(x):
    x = jnp.clip(x, -4.0, 4.0)
    x2 = x * x
    a = -2.72614225801306e-10
    a = a * x2 + 2.77068142495902e-08
    a = a * x2 + -2.10102402082508e-06
    a = a * x2 + -5.69250639462346e-05
    a = a * x2 + -7.34990630326855e-04
    a = a * x2 + -2.95459980854025e-03
    a = a * x2 + -1.60960333262415e-02
    b = -1.45660718464996e-05
    b = b * x2 + -2.13374055278905e-04
    b = b * x2 + -1.68282697438203e-03
    b = b * x2 + -7.37332916720468e-03
    b = b * x2 + -1.42647390514189e-02
    return x * a / b


def _gelu_fwd_bwd(x):
    cdf = 0.5 * (1.0 + _erf(x * 0.7071067811865476))
    pdf = jnp.exp(-0.5 * x * x) * 0.3989422804014327
    return x * cdf, cdf + x * pdf


def _rms(x):
    return lax.rsqrt(jnp.mean(x * x, axis=-1, keepdims=True) + EPS)


def _ln_hat(x):
    mu = jnp.mean(x, axis=-1, keepdims=True)
    xc = x - mu
    rs = lax.rsqrt(jnp.mean(xc * xc, axis=-1, keepdims=True) + EPS)
    return xc * rs, rs


def _ln_bwd(dxh, xh, rs):
    return rs * (dxh - jnp.mean(dxh, axis=-1, keepdims=True) - xh * jnp.mean(dxh * xh, axis=-1, keepdims=True))


def _swap16(z):
    parts = []
    for i in range(z.shape[1] // LANES):
        blk = z[:, i * LANES:(i + 1) * LANES]
        lane = _lane(blk.shape)
        parts.append(jnp.where((lane & 16) == 0, pltpu.roll(blk, LANES - 16, 1), pltpu.roll(blk, 16, 1)))
    return parts[0] if len(parts) == 1 else jnp.concatenate(parts, axis=1)


def _head_slab(pair, odd):
    src = pltpu.roll(pair, HEAD_DIM, 1) if odd else pair
    return jnp.where(_lane(pair.shape) < HEAD_DIM, src, 0.0)


def _pair_merge(even, odd):
    return jnp.where(_lane(even.shape) < HEAD_DIM, even, pltpu.roll(odd, HEAD_DIM, 1))


def _heads_to_cat(ref, n_heads):
    pairs = [_pair_merge(ref[2 * p], ref[2 * p + 1]) for p in range(n_heads // 2)]
    return pairs[0] if len(pairs) == 1 else jnp.concatenate(pairs, axis=1)


def _split3(x):
    hi = x.astype(BF16).astype(F32)
    r = x - hi
    mid = r.astype(BF16).astype(F32)
    lo = (r - mid).astype(BF16).astype(F32)
    return hi, mid, lo


def _with_spare(slab, hi, mid, lo):
    lane = _lane(slab.shape)
    return jnp.where(lane == HEAD_DIM, hi, jnp.where(lane == HEAD_DIM + 1, mid, jnp.where(lane == HEAD_DIM + 2, lo, slab)))


def _with_ones(slab):
    lane = _lane(slab.shape)
    return jnp.where((lane >= HEAD_DIM) & (lane < HEAD_DIM + 3), 1.0, slab)


def _conv_window(rot_ref, prev_ref, main, next_ref, first, last, t):
    n = t + 2 * HALO
    full = jnp.concatenate([jnp.where(first, 0.0, prev_ref[...]), main, jnp.where(last, 0.0, next_ref[...])], axis=0)
    rot_ref[0] = full
    for b in range(1, SUBLANES):
        rot_ref[b] = pltpu.roll(full, n - b, 0)


def _tap(rot_ref, start, t):
    a, b = divmod(start, SUBLANES)
    return rot_ref[b, SUBLANES * a:SUBLANES * a + t, :]


def _sgu_mix(v1_bf, w_ref, n_chunks):
    rows = []
    for n in range(n_chunks):
        pairs = []
        for p in range(SG_W // LANES):
            xp = v1_bf[n * SG_CHUNK:(n + 1) * SG_CHUNK, p * LANES:(p + 1) * LANES]
            me = _dot(w_ref[2 * p], xp)
            mo = _dot(w_ref[2 * p + 1], xp)
            pairs.append(jnp.where(_lane(me.shape) < HEAD_DIM, me, mo))
        rows.append(jnp.concatenate(pairs, axis=1))
    return rows[0] if len(rows) == 1 else jnp.concatenate(rows, axis=0)


def _halo_specs(t, s, width):
    per = t // HALO
    nblk = s // HALO
    prev = pl.BlockSpec((HALO, width), lambda i: (jnp.maximum(i * per - 1, 0), 0))
    nxt = pl.BlockSpec((HALO, width), lambda i: (jnp.minimum((i + 1) * per, nblk - 1), 0))
    return prev, nxt


def _const_spec(shape):
    nd = len(shape)
    return pl.BlockSpec(shape, lambda i: (0,) * nd)


def _arb(n=1):
    return pltpu.CompilerParams(dimension_semantics=("arbitrary",) * n)


def _fwd_in(x, g_pre, w_in_bf, bd, qg, kg, cos, sin, t):
    s = x.shape[0]

    def body(x_ref, g_ref, w_ref, bd_ref, qg_ref, kg_ref, cos_ref, sin_ref,
             proj_ref, hc_ref, q_ref, k_ref, v_ref):
        xv = x_ref[...]
        h = (xv * _rms(xv) * g_ref[...]).astype(BF16)
        proj = _dot(h, w_ref[...])
        proj_ref[...] = proj
        cos_pair = cos_ref[...]
        sin_pair = sin_ref[...]
        cosv = jnp.concatenate([cos_pair] * (N_HEADS // 2), axis=1)
        sinv = jnp.concatenate([sin_pair] * (N_HEADS // 2), axis=1)
        q = proj[:, C_Q:C_K]
        qn = q * lax.rsqrt(_group_mean(q * q, bd_ref[...]) + EPS) * qg_ref[...]
        qr = (qn * cosv + _swap16(qn) * sinv) * ATT_SCALE
        for hh in range(N_HEADS):
            pair = qr[:, (hh // 2) * LANES:(hh // 2 + 1) * LANES]
            q_ref[hh] = _head_slab(pair, hh % 2 == 1).astype(BF16)
        k = proj[:, C_K:C_V]
        kn = k * lax.rsqrt(_group_mean(k * k, bd_ref[0:LANES, 0:LANES]) + EPS) * kg_ref[...]
        kr = kn * cos_pair + _swap16(kn) * sin_pair
        vv = proj[:, C_V:C_GA]
        for hh in range(N_KV):
            k_ref[hh] = _with_ones(_head_slab(kr, hh == 1)).astype(BF16)
            v_ref[hh] = _with_ones(_head_slab(vv, hh == 1)).astype(BF16)
        hc_ref[...] = proj[:, C_A1:C_A2] * _sigmoid(proj[:, C_A2:C_GC])

    row = lambda w: pl.BlockSpec((t, w), lambda i: (i, 0))
    heads = lambda n: pl.BlockSpec((n, t, LANES), lambda i: (0, i, 0))
    return pl.pallas_call(
        body, name="fwd_in", grid=(s // t,),
        in_specs=[row(D_MODEL), _const_spec((1, D_MODEL)), _const_spec((D_MODEL, D_IN)), _const_spec((512, 512)),
                  _const_spec((1, 512)), _const_spec((1, LANES)), row(LANES), row(LANES)],
        out_specs=[row(D_IN), row(CONV_W), heads(N_HEADS), heads(N_KV), heads(N_KV)],
        out_shape=[jax.ShapeDtypeStruct((s, D_IN), F32), jax.ShapeDtypeStruct((s, CONV_W), F32),
                   jax.ShapeDtypeStruct((N_HEADS, s, LANES), BF16), jax.ShapeDtypeStruct((N_KV, s, LANES), BF16),
                   jax.ShapeDtypeStruct((N_KV, s, LANES), BF16)],
        compiler_params=_arb(),
    )(x, g_pre, w_in_bf, bd, qg, kg, cos, sin)


def _chunk_rows(c, tk):
    return pl.ds(c * tk, tk) if isinstance(c, int) else pl.ds(pl.multiple_of(c * tk, tk), tk)


def _three_stage_pipeline(nk, per_trip, stage1, stage2, stage3, peel):
    assert nk % 2 == 0 and per_trip % 2 == 0

    def step(t, parity, first=False, last=False):
        if not last:
            stage1(t + 1, 1 - parity)
        stage2(parity)
        if not first:
            stage3(t - 1, 1 - parity)

    stage1(0, 0)
    if not peel:
        assert nk % per_trip == 0

        def whole_trip(i, carry):
            for u in range(per_trip):
                c = per_trip * i + u
                stage1(jnp.minimum(c + 1, nk - 1), 1 - u % 2)
                stage2(u % 2)
                stage3(jnp.maximum(c - 1, 0), 1 - u % 2)
            return carry

        lax.fori_loop(0, nk // per_trip, whole_trip, 0)
        stage3(nk - 1, 1)
        return

    step(0, 0, first=True)
    n_trips, left = divmod(nk - 2, per_trip)

    def trip(i, carry):
        for u in range(per_trip):
            step(1 + per_trip * i + u, (1 + u) % 2)
        return carry

    if n_trips:
        lax.fori_loop(0, n_trips, trip, 0)
    for t in range(1 + n_trips * per_trip, 1 + n_trips * per_trip + left):
        step(t, t % 2)
    step(nk - 1, 1, last=True)
    stage3(nk - 1, 1)


def _flash_fwd(q, k, v, tq, tk):
    s = q.shape[1]
    rows = GROUP * tq
    nk = s // tk

    def body(q_ref, k_ref, vt_ref, o_ref, qa_ref, m_scr, acc_scr, s0, s1, p0, p1, a0, a1):
        s_bufs, p_bufs, a_bufs = (s0, s1), (p0, p1), (a0, a1)
        qv = q_ref[...].reshape(rows, LANES)
        q_t = qv.astype(F32).T
        q_t_bf = q_t.astype(BF16)
        m_scr[...] = jnp.full((1, rows), -jnp.inf, F32)
        acc_scr[...] = jnp.zeros((LANES, rows), F32)

        def scores(c, slot):
            s_bufs[slot][...] = _dot(k_ref[_chunk_rows(c, tk), :], q_t_bf)

        def softmax(slot):
            for h in range(GROUP):
                r = slice(h * tq, (h + 1) * tq)
                sc = s_bufs[slot][:, r]
                m_prev = m_scr[:, r]
                m_new = jnp.maximum(m_prev, jnp.max(sc, axis=0, keepdims=True))
                p_bufs[slot][:, r] = jnp.exp((sc - m_new).astype(BF16))
                a_bufs[slot][:, r] = jnp.exp(m_prev - m_new)
                m_scr[:, r] = m_new

        def weighted_values(c, slot):
            acc_scr[...] = a_bufs[slot][...] * acc_scr[...] + _dot(vt_ref[c], p_bufs[slot][...])

        _three_stage_pipeline(nk, 4, scores, softmax, weighted_values, peel=True)

        acc = acc_scr[...]
        row = lax.broadcasted_iota(jnp.int32, acc.shape, 0)
        l = jnp.sum(jnp.where(row == HEAD_DIM, acc, 0.0), axis=0, keepdims=True)
        o_ref[...] = jnp.where(row < HEAD_DIM, acc / l, 0.0).T.reshape(GROUP, tq, LANES)
        hi, mid, lo = _split3(-(m_scr[...] + jnp.log(l)))
        qa_t = jnp.where(row == HEAD_DIM, hi, jnp.where(row == HEAD_DIM + 1, mid,
                                                        jnp.where(row == HEAD_DIM + 2, lo, q_t)))
        qa_ref[...] = qa_t.T.astype(BF16).reshape(GROUP, tq, LANES)

    qspec = pl.BlockSpec((GROUP, tq, LANES), lambda j, i: (j, i, 0))
    kspec = pl.BlockSpec((None, s, LANES), lambda j, i: (j, 0, 0))
    vtspec = pl.BlockSpec((None, nk, LANES, tk), lambda j, i: (j, 0, 0, 0))
    v_t = jnp.swapaxes(v.reshape(N_KV, nk, tk, LANES), 2, 3)
    return pl.pallas_call(
        body, name="flash_fwd", grid=(N_KV, s // tq),
        in_specs=[qspec, kspec, vtspec],
        out_specs=[qspec, qspec],
        out_shape=[jax.ShapeDtypeStruct((N_HEADS, s, LANES), F32), jax.ShapeDtypeStruct((N_HEADS, s, LANES), BF16)],
        scratch_shapes=[pltpu.VMEM((1, rows), F32), pltpu.VMEM((LANES, rows), F32),
                        pltpu.VMEM((tk, rows), F32), pltpu.VMEM((tk, rows), F32),
                        pltpu.VMEM((tk, rows), BF16), pltpu.VMEM((tk, rows), BF16),
                        pltpu.VMEM((1, rows), F32), pltpu.VMEM((1, rows), F32)],
        compiler_params=_arb(2),
    )(q, k, v_t)


def _groups_fwd(proj_ref, o_ref, hext_ref, cw_ref, cp_ref, sp_ref, sgw_ref, sgb_ref, t, c0=None):
    proj = proj_ref[...]
    r = {}
    r["att"] = _heads_to_cat(o_ref, N_HEADS)
    r["gate_a"], r["dgate_a"] = _silu_fwd_bwd(proj[:, C_GA:C_A1])
    r["att_g"] = r["att"] * r["gate_a"]
    if c0 is None:
        c0 = jnp.zeros((t, CONV_W), F32) + cp_ref[0:1, :]
        for kk in range(CONV_K):
            c0 = c0 + cw_ref[kk:kk + 1, :] * _tap(hext_ref, kk + 1, t)
    r["c0"] = c0
    r["xh_c"], r["rs_c"] = _ln_hat(c0)
    r["c1"] = r["xh_c"] * cp_ref[1:2, :] + cp_ref[2:3, :]
    r["sg_c1"] = _sigmoid(r["c1"])
    r["c2"] = r["c1"] * r["sg_c1"]
    r["gate_c"], r["dgate_c"] = _silu_fwd_bwd(proj[:, C_GC:C_U])
    r["cnv_g"] = r["c2"] * r["gate_c"]
    r["gu"], r["dgu"] = _gelu_fwd_bwd(proj[:, C_U:C_VS])
    gv, r["dgv"] = _gelu_fwd_bwd(proj[:, C_VS:C_GS])
    r["xh_s"], r["rs_s"] = _ln_hat(gv)
    v1 = r["xh_s"] * sp_ref[0:1, :] + sp_ref[1:2, :]
    r["v1_bf"] = v1.astype(BF16)
    r["mixed"] = _sgu_mix(r["v1_bf"], sgw_ref, t // SG_CHUNK) + jnp.concatenate([sgb_ref[...]] * (t // SG_CHUNK), axis=0)
    r["um"] = r["gu"] * r["mixed"]
    r["gate_s"], r["dgate_s"] = _silu_fwd_bwd(proj[:, C_GS:D_IN])
    r["sgu_g"] = r["um"] * r["gate_s"]
    r["mc_bf"] = jnp.concatenate([r["att_g"], r["cnv_g"], r["sgu_g"]], axis=1).astype(BF16)
    return r


def _fwd_out(x, proj, o, hc, cw, cp, sp, sgw_bf, sgb, w_out_bf, g_post, t, target=None):
    s = x.shape[0]
    last_layer = target is not None

    def body(*refs):
        (x_ref, proj_ref, o_ref, hc_ref, hp_ref, hn_ref, cw_ref, cp_ref, sp_ref, sgw_ref, sgb_ref,
         w_ref, g_ref) = refs[:13]
        rest = refs[13:]
        if last_layer:
            t_ref, mix_ref, out_ref, c0_ref, sq_ref, hext_ref = rest
        else:
            mix_ref, out_ref, c0_ref, hext_ref = rest
        i = pl.program_id(0)
        _conv_window(hext_ref, hp_ref, hc_ref[...], hn_ref, i == 0, i == pl.num_programs(0) - 1, t)
        r = _groups_fwd(proj_ref, o_ref, hext_ref, cw_ref, cp_ref, sp_ref, sgw_ref, sgb_ref, t)
        c0_ref[...] = r["c0"]
        mix = _dot(r["mc_bf"], w_ref[...])
        mix_ref[...] = mix
        y = x_ref[...] + mix * _rms(mix) * g_ref[...]
        if last_layer:
            @pl.when(i == 0)
            def _():
                sq_ref[...] = jnp.zeros_like(sq_ref)

            err = y - t_ref[...]
            out_ref[...] = err * (1.0 / D_MODEL)
            sq_ref[...] += jnp.sum(err * err, axis=0, keepdims=True)
        else:
            out_ref[...] = y

    row = lambda w: pl.BlockSpec((t, w), lambda i: (i, 0))
    hprev, hnext = _halo_specs(t, s, CONV_W)
    big = jax.ShapeDtypeStruct((s, D_MODEL), F32)
    return pl.pallas_call(
        body, name="fwd_out_loss" if last_layer else "fwd_out", grid=(s // t,),
        in_specs=[row(D_MODEL), row(D_IN), pl.BlockSpec((N_HEADS, t, LANES), lambda i: (0, i, 0)), row(CONV_W),
                  hprev, hnext, _const_spec((32, CONV_W)), _const_spec((8, CONV_W)), _const_spec((8, SG_W)),
                  _const_spec((4, SG_CHUNK, SG_CHUNK)), _const_spec((SG_CHUNK, SG_W)),
                  _const_spec((D_MODEL, D_MODEL)), _const_spec((1, D_MODEL))] + ([row(D_MODEL)] if last_layer else []),
        out_specs=[row(D_MODEL), row(D_MODEL), row(CONV_W)] + ([_const_spec((1, D_MODEL))] if last_layer else []),
        out_shape=[big, big, jax.ShapeDtypeStruct((s, CONV_W), F32)]
        + ([jax.ShapeDtypeStruct((1, D_MODEL), F32)] if last_layer else []),
        scratch_shapes=[pltpu.VMEM((SUBLANES, t + 2 * HALO, CONV_W), F32)],
        compiler_params=_arb(),
    )(*((x, proj, o, hc, hc, hc, cw, cp, sp, sgw_bf, sgb, w_out_bf, g_post) + ((target,) if last_layer else ())))


def _bwd_out(dy, mix, proj, o, hc, c0, cw, cp, sp, sgw_bf, sgwt_bf, sgb, w_out_t_bf, g_post, t):
    s = dy.shape[0]
    n_chunks = t // SG_CHUNK

    def body(dy_ref, mix_ref, proj_ref, o_ref, hc_ref, hp_ref, hn_ref, c0_ref, cw_ref, cp_ref, sp_ref, sgw_ref,
             sgwt_ref, sgb_ref, wt_ref, g_ref,
             do_ref, dgs_ref, dc0_ref, gwo_ref, gpost_ref, gcw_ref, gvec_ref, gsgw_ref, gsgb_ref, hext_ref):
        i = pl.program_id(0)

        @pl.when(i == 0)
        def _():
            for ref in (gwo_ref, gpost_ref, gcw_ref, gvec_ref, gsgw_ref, gsgb_ref):
                ref[...] = jnp.zeros_like(ref)

        _conv_window(hext_ref, hp_ref, hc_ref[...], hn_ref, i == 0, i == pl.num_programs(0) - 1, t)
        r = _groups_fwd(proj_ref, o_ref, hext_ref, cw_ref, cp_ref, sp_ref, sgw_ref, sgb_ref, t, c0=c0_ref[...])

        dyv = dy_ref[...]
        mix_v = mix_ref[...]
        rr = _rms(mix_v)
        gd = dyv * g_ref[...]
        dmix = rr * gd - mix_v * (rr * rr * rr * jnp.mean(gd * mix_v, axis=-1, keepdims=True))
        gpost_ref[...] += jnp.sum(dyv * mix_v * rr, axis=0, keepdims=True)
        dmix_bf = dmix.astype(BF16)
        gwo_ref[...] += _dot_tn(r["mc_bf"], dmix_bf)
        dmc = _dot(dmix_bf, wt_ref[...])

        d_att = dmc[:, 0:512]
        dg_att = d_att * r["att"] * r["dgate_a"]
        d_o = d_att * r["gate_a"]
        prod = d_o * r["att"]
        for p in range(N_HEADS // 2):
            sl = slice(p * LANES, (p + 1) * LANES)
            pr = prod[:, sl]
            tot = jnp.sum(pr, axis=1, keepdims=True)
            ev = jnp.sum(jnp.where(_lane(pr.shape) < HEAD_DIM, pr, 0.0), axis=1, keepdims=True)
            for odd, delta in ((False, ev), (True, tot - ev)):
                hi, mid, lo = _split3(-delta)
                do_ref[2 * p + int(odd)] = _with_spare(_head_slab(d_o[:, sl], odd), hi, mid, lo).astype(BF16)

        dcv = dmc[:, 512:768]
        dg_conv = dcv * r["c2"] * r["dgate_c"]
        dc1 = dcv * r["gate_c"] * (r["sg_c1"] * (1.0 + r["c1"] * (1.0 - r["sg_c1"])))
        dc0 = _ln_bwd(dc1 * cp_ref[1:2, :], r["xh_c"], r["rs_c"])
        dc0_ref[...] = dc0
        for kk in range(CONV_K):
            gcw_ref[kk:kk + 1, :] += jnp.sum(dc0 * _tap(hext_ref, kk + 1, t), axis=0, keepdims=True)

        dsg = dmc[:, 768:1024]
        dg_sg = dsg * r["um"] * r["dgate_s"]
        du = dsg * r["mixed"] * r["gate_s"] * r["dgu"]
        dmx = dsg * r["gu"] * r["gate_s"]
        dmx_bf = dmx.astype(BF16)
        sgb_sum = dmx[0:SG_CHUNK, :]
        for n in range(1, n_chunks):
            sgb_sum = sgb_sum + dmx[n * SG_CHUNK:(n + 1) * SG_CHUNK, :]
        gsgb_ref[...] += sgb_sum
        dv1_rows = []
        for n in range(n_chunks):
            pairs = []
            for p in range(SG_W // LANES):
                rs_ = slice(n * SG_CHUNK, (n + 1) * SG_CHUNK)
                ls_ = slice(p * LANES, (p + 1) * LANES)
                dm = dmx_bf[rs_, ls_]
                xp = r["v1_bf"][rs_, ls_]
                low = _lane(dm.shape) < HEAD_DIM
                zero = jnp.zeros_like(dm)
                gsgw_ref[2 * p] += _dot_nt(jnp.where(low, dm, zero), xp)
                gsgw_ref[2 * p + 1] += _dot_nt(jnp.where(low, zero, dm), xp)
                pairs.append(jnp.where(low, _dot(sgwt_ref[2 * p], dm), _dot(sgwt_ref[2 * p + 1], dm)))
            dv1_rows.append(jnp.concatenate(pairs, axis=1))
        dv1 = dv1_rows[0] if n_chunks == 1 else jnp.concatenate(dv1_rows, axis=0)
        dvs = _ln_bwd(dv1 * sp_ref[0:1, :], r["xh_s"], r["rs_s"]) * r["dgv"]

        zrow = jnp.zeros((1, CONV_W), F32)
        gvec_ref[...] += jnp.concatenate([
            jnp.sum(dc0, axis=0, keepdims=True),
            jnp.sum(dc1 * r["xh_c"], axis=0, keepdims=True),
            jnp.sum(dc1, axis=0, keepdims=True),
            jnp.sum(dv1 * r["xh_s"], axis=0, keepdims=True),
            jnp.sum(dv1, axis=0, keepdims=True),
            zrow, zrow, zrow], axis=0)
        dgs_ref[...] = jnp.concatenate([dg_att, dg_conv, du, dvs, dg_sg], axis=1)

    row = lambda w: pl.BlockSpec((t, w), lambda i: (i, 0))
    heads = pl.BlockSpec((N_HEADS, t, LANES), lambda i: (0, i, 0))
    hprev, hnext = _halo_specs(t, s, CONV_W)
    return pl.pallas_call(
        body, name="bwd_out", grid=(s // t,),
        in_specs=[row(D_MODEL), row(D_MODEL), row(D_IN), heads, row(CONV_W), hprev, hnext, row(CONV_W),
                  _const_spec((32, CONV_W)), _const_spec((8, CONV_W)), _const_spec((8, SG_W)),
                  _const_spec((4, SG_CHUNK, SG_CHUNK)), _const_spec((4, SG_CHUNK, SG_CHUNK)),
                  _const_spec((SG_CHUNK, SG_W)), _const_spec((D_MODEL, D_MODEL)), _const_spec((1, D_MODEL))],
        out_specs=[heads, row(1536), row(CONV_W), _const_spec((D_MODEL, D_MODEL)), _const_spec((1, D_MODEL)),
                   _const_spec((32, CONV_W)), _const_spec((8, CONV_W)), _const_spec((4, SG_CHUNK, SG_CHUNK)),
                   _const_spec((SG_CHUNK, SG_W))],
        out_shape=[jax.ShapeDtypeStruct((N_HEADS, s, LANES), BF16), jax.ShapeDtypeStruct((s, 1536), F32),
                   jax.ShapeDtypeStruct((s, CONV_W), F32), jax.ShapeDtypeStruct((D_MODEL, D_MODEL), F32),
                   jax.ShapeDtypeStruct((1, D_MODEL), F32), jax.ShapeDtypeStruct((32, CONV_W), F32),
                   jax.ShapeDtypeStruct((8, CONV_W), F32), jax.ShapeDtypeStruct((4, SG_CHUNK, SG_CHUNK), F32),
                   jax.ShapeDtypeStruct((SG_CHUNK, SG_W), F32)],
        scratch_shapes=[pltpu.VMEM((SUBLANES, t + 2 * HALO, CONV_W), F32)],
        compiler_params=_arb(),
    )(dy, mix, proj, o, hc, hc, hc, c0, cw, cp, sp, sgw_bf, sgwt_bf, sgb, w_out_t_bf, g_post)


def _flash_bwd(qa, doa, k, v, tq, tk):
    s = qa.shape[1]
    rows = GROUP * tq
    nk = s // tk
    n_q = s // tq

    def body(qa_ref, do_ref, k_ref, v_ref, dq_ref, dk_hbm, dv_hbm,
             dq_scr, dk_scr, dv_scr, s0, s1, d0, d1, p0, p1, e0, e1, sems):
        j, i = pl.program_id(0), pl.program_id(1)
        s_bufs, d_bufs, p_bufs, e_bufs = (s0, s1), (d0, d1), (p0, p1), (e0, e1)
        qv = qa_ref[...].reshape(rows, LANES)
        dov = do_ref[...].reshape(rows, LANES)
        q_t = qv.astype(F32).T.astype(BF16)
        do_t = dov.astype(F32).T.astype(BF16)
        dq_scr[...] = jnp.zeros((rows, LANES), F32)

        @pl.when(i == 0)
        def _():
            dk_scr[...] = jnp.zeros_like(dk_scr)
            dv_scr[...] = jnp.zeros_like(dv_scr)

        def at(c):
            return _chunk_rows(c, tk)

        def scores(c, slot):
            s_bufs[slot][...] = _dot_nt(qv, k_ref[at(c), :])
            d_bufs[slot][...] = _dot_nt(dov, v_ref[at(c), :])

        def probs(slot):
            for h in range(GROUP):
                r = slice(h * tq, (h + 1) * tq)
                p = jnp.exp(s_bufs[slot][r, :])
                p_bufs[slot][r, :] = p.astype(BF16)
                e_bufs[slot][r, :] = (p * d_bufs[slot][r, :]).astype(BF16)

        def grads(c, slot):
            ds = e_bufs[slot][...]
            dq_scr[...] += _dot(ds, k_ref[at(c), :])
            dv_scr[c] += _dot(do_t, p_bufs[slot][...])
            dk_scr[c] += _dot(q_t, ds)

        p1[...] = jnp.zeros((rows, tk), BF16)
        e1[...] = jnp.zeros((rows, tk), BF16)
        _three_stage_pipeline(nk, 2, scores, probs, grads, peel=False)
        dq_ref[...] = dq_scr[...].reshape(GROUP, tq, LANES)

        @pl.when(i == n_q - 1)
        def _():
            out = [pltpu.make_async_copy(dk_scr, dk_hbm.at[j], sems.at[0]),
                   pltpu.make_async_copy(dv_scr, dv_hbm.at[j], sems.at[1])]
            for cp in out:
                cp.start()
            for cp in out:
                cp.wait()

    qspec = pl.BlockSpec((GROUP, tq, LANES), lambda j, i: (j, i, 0))
    kvspec = pl.BlockSpec((None, s, LANES), lambda j, i: (j, 0, 0))
    hbm = pl.BlockSpec(memory_space=pl.ANY)
    stage_f32 = pltpu.VMEM((rows, tk), F32)
    stage_bf = pltpu.VMEM((rows, tk), BF16)
    kv_t = jax.ShapeDtypeStruct((N_KV, nk, LANES, tk), F32)
    return pl.pallas_call(
        body, name="flash_bwd", grid=(N_KV, n_q),
        in_specs=[qspec, qspec, kvspec, kvspec],
        out_specs=[qspec, hbm, hbm],
        out_shape=[jax.ShapeDtypeStruct((N_HEADS, s, LANES), F32), kv_t, kv_t],
        scratch_shapes=[pltpu.VMEM((rows, LANES), F32), pltpu.VMEM((nk, LANES, tk), F32), pltpu.VMEM((nk, LANES, tk), F32),
                        stage_f32, stage_f32, stage_f32, stage_f32, stage_bf, stage_bf, stage_bf, stage_bf,
                        pltpu.SemaphoreType.DMA((2,))],
        compiler_params=_arb(2),
    )(qa, doa, k, v)


def _bwd_in(dy, x, proj, dq, dk, dv, dgs, dc0, cw, g_pre, w_in_t_bf, bd, qg, kg, cos, sin, t):
    s = x.shape[0]

    def body(dy_ref, x_ref, proj_ref, dq_ref, dk_ref, dv_ref, dgs_ref, dc_ref, dcp_ref, dcn_ref, cw_ref, g_ref,
             wt_ref, bd_ref, qg_ref, kg_ref, cos_ref, sin_ref,
             dx_ref, dproj_ref, h_ref, gpre_ref, gq_ref, gk_ref, dext_ref):
        i = pl.program_id(0)

        @pl.when(i == 0)
        def _():
            for ref in (gpre_ref, gq_ref, gk_ref):
                ref[...] = jnp.zeros_like(ref)

        proj = proj_ref[...]
        cos_pair = cos_ref[...]
        sin_pair = sin_ref[...]
        cosv = jnp.concatenate([cos_pair] * (N_HEADS // 2), axis=1)
        sinv = jnp.concatenate([sin_pair] * (N_HEADS // 2), axis=1)

        def head_norm_bwd(dr, z, bdm, g, cs, sn, gacc_ref):
            dn = dr * cs + _swap16(dr * sn)
            rr = lax.rsqrt(_group_mean(z * z, bdm) + EPS)
            gdn = dn * g
            gacc_ref[...] += jnp.sum(dn * z * rr, axis=0, keepdims=True)
            return rr * gdn - z * (rr * rr * rr * _group_mean(gdn * z, bdm))

        dq_cat = _heads_to_cat(dq_ref, N_HEADS) * ATT_SCALE
        dzq = head_norm_bwd(dq_cat, proj[:, C_Q:C_K], bd_ref[...], qg_ref[...], cosv, sinv, gq_ref)

        def kv_pair(ref):
            return jnp.concatenate([ref[0, 0:HEAD_DIM, :], ref[1, 0:HEAD_DIM, :]], axis=0).T

        dk_cat = kv_pair(dk_ref)
        dzk = head_norm_bwd(dk_cat, proj[:, C_K:C_V], bd_ref[0:LANES, 0:LANES], kg_ref[...],
                            cos_pair, sin_pair, gk_ref)
        dv_cat = kv_pair(dv_ref)

        _conv_window(dext_ref, dcp_ref, dc_ref[...], dcn_ref, i == 0, i == pl.num_programs(0) - 1, t)
        dhc = jnp.zeros((t, CONV_W), F32)
        for kk in range(CONV_K):
            dhc = dhc + cw_ref[kk:kk + 1, :] * _tap(dext_ref, CONV_K - kk, t)
        sg = _sigmoid(proj[:, C_A2:C_GC])
        da1 = dhc * sg
        da2 = dhc * proj[:, C_A1:C_A2] * sg * (1.0 - sg)

        dgs = dgs_ref[...]
        dproj_bf = jnp.concatenate([dzq, dzk, dv_cat, dgs[:, 0:512], da1, da2, dgs[:, 512:1536]], axis=1).astype(BF16)
        dproj_ref[...] = dproj_bf
        dh = _dot(dproj_bf, wt_ref[...])

        xv = x_ref[...]
        rr = _rms(xv)
        gv = g_ref[...]
        h_ref[...] = (xv * rr * gv).astype(BF16)
        gdh = dh * gv
        gpre_ref[...] += jnp.sum(dh * xv * rr, axis=0, keepdims=True)
        dx_ref[...] = dy_ref[...] + rr * gdh - xv * (rr * rr * rr * jnp.mean(gdh * xv, axis=-1, keepdims=True))

    row = lambda w: pl.BlockSpec((t, w), lambda i: (i, 0))
    heads = lambda n: pl.BlockSpec((n, t, LANES), lambda i: (0, i, 0))
    hprev, hnext = _halo_specs(t, s, CONV_W)
    tk = dk.shape[3]
    assert tk % t == 0
    kv_t = pl.BlockSpec((N_KV, None, LANES, t), lambda i: (0, i // (tk // t), 0, i % (tk // t)))
    return pl.pallas_call(
        body, name="bwd_in", grid=(s // t,),
        in_specs=[row(D_MODEL), row(D_MODEL), row(D_IN), heads(N_HEADS), kv_t, kv_t, row(1536),
                  row(CONV_W), hprev, hnext, _const_spec((32, CONV_W)), _const_spec((1, D_MODEL)),
                  _const_spec((D_IN, D_MODEL)), _const_spec((512, 512)), _const_spec((1, 512)),
                  _const_spec((1, LANES)), row(LANES), row(LANES)],
        out_specs=[row(D_MODEL), row(D_IN), row(D_MODEL), _const_spec((1, D_MODEL)), _const_spec((1, 512)),
                   _const_spec((1, LANES))],
        out_shape=[jax.ShapeDtypeStruct((s, D_MODEL), F32), jax.ShapeDtypeStruct((s, D_IN), BF16),
                   jax.ShapeDtypeStruct((s, D_MODEL), BF16), jax.ShapeDtypeStruct((1, D_MODEL), F32),
                   jax.ShapeDtypeStruct((1, 512), F32), jax.ShapeDtypeStruct((1, LANES), F32)],
        scratch_shapes=[pltpu.VMEM((SUBLANES, t + 2 * HALO, CONV_W), F32)],
        compiler_params=_arb(),
    )(dy, x, proj, dq, dk, dv, dgs, dc0, dc0, dc0, cw, g_pre, w_in_t_bf, bd, qg, kg, cos, sin)


def _grad_w_in(h_bf, dproj_bf, t):
    s = h_bf.shape[0]
    half = D_IN // 2

    def body(h_ref, d_ref, g_ref):
        @pl.when(pl.program_id(1) == 0)
        def _():
            g_ref[...] = jnp.zeros_like(g_ref)

        g_ref[...] += _dot_tn(h_ref[...], d_ref[...])

    return pl.pallas_call(
        body, name="grad_w_in", grid=(2, s // t),
        in_specs=[pl.BlockSpec((t, D_MODEL), lambda j, i: (i, 0)), pl.BlockSpec((t, half), lambda j, i: (i, j))],
        out_specs=pl.BlockSpec((D_MODEL, half), lambda j, i: (0, j)),
        out_shape=jax.ShapeDtypeStruct((D_MODEL, D_IN), F32),
        compiler_params=_arb(2),
    )(h_bf, dproj_bf)


def _place():
    x, y, c = lax.axis_index("x"), lax.axis_index("y"), lax.axis_index("c")
    chips = [(1 - x, y), (x, 1 - y), (1 - x, 1 - y)]
    return x, y, c, chips


def _any_specs(n):
    return [pl.BlockSpec(memory_space=pl.ANY)] * n


def _gather_weights(w_in_bf, w_out_bf, cdw):
    arrs = (w_in_bf, w_out_bf, cdw)
    n = len(arrs)

    def body(*refs):
        ins, outs = refs[:n], refs[n:2 * n]
        send_sems, recv_sems, local_sems = refs[2 * n:]
        x, y, c, chips = _place()
        mine = 2 * x + y
        local = [pltpu.make_async_copy(ins[a], outs[a].at[mine], local_sems.at[a]) for a in range(n)]
        for cp in local:
            cp.start()

        def copy(j, a, slot, to):
            return pltpu.make_async_remote_copy(src_ref=ins[a], dst_ref=outs[a].at[slot], send_sem=send_sems.at[n * j + a],
                                                recv_sem=recv_sems.at[n * j + a], device_id=to, device_id_type=MESH)

        sends = [copy(j, a, mine, (px, py, c)) for j, (px, py) in enumerate(chips) for a in range(n)]
        for cp in sends:
            cp.start()
        for j, (px, py) in enumerate(chips):
            for a in range(n):
                copy(j, a, 2 * px + py, (px, py, c)).wait_recv()
        for cp in sends:
            cp.wait_send()
        for cp in local:
            cp.wait()

    return pl.pallas_call(
        body, name="gather_weights",
        in_specs=_any_specs(n), out_specs=_any_specs(n),
        out_shape=[jax.ShapeDtypeStruct((N_CHIPS,) + a.shape, a.dtype) for a in arrs],
        scratch_shapes=[pltpu.SemaphoreType.DMA((3 * n,)), pltpu.SemaphoreType.DMA((3 * n,)), pltpu.SemaphoreType.DMA((n,))],
    )(*arrs)


def _scatter_grads(gin_pieces, gw_out):
    depth = gw_out.shape[0]
    rows = gw_out.shape[1] // N_CHIPS

    def body(gin_ref, gout_ref, rin_ref, rout_ref, send_sems, recv_sems, local_sems):
        x, y, c, chips = _place()
        mine = 2 * x + y

        def out_rows(chip):
            return gout_ref.at[:, pl.ds(pl.multiple_of(chip * rows, rows), rows), :]

        local = [pltpu.make_async_copy(gin_ref.at[mine], rin_ref.at[mine], local_sems.at[0]),
                 pltpu.make_async_copy(out_rows(mine), rout_ref.at[mine], local_sems.at[1])]
        for cp in local:
            cp.start()

        def copies(j, shard, slot, to):
            kw = dict(device_id=to, device_id_type=MESH)
            return [pltpu.make_async_remote_copy(src_ref=gin_ref.at[shard], dst_ref=rin_ref.at[slot],
                                                 send_sem=send_sems.at[2 * j], recv_sem=recv_sems.at[2 * j], **kw),
                    pltpu.make_async_remote_copy(src_ref=out_rows(shard), dst_ref=rout_ref.at[slot],
                                                 send_sem=send_sems.at[2 * j + 1], recv_sem=recv_sems.at[2 * j + 1], **kw)]

        sends = [cp for j, (px, py) in enumerate(chips) for cp in copies(j, 2 * px + py, mine, (px, py, c))]
        for cp in sends:
            cp.start()
        for j, (px, py) in enumerate(chips):
            for cp in copies(j, mine, 2 * px + py, (px, py, c)):
                cp.wait_recv()
        for cp in sends:
            cp.wait_send()
        for cp in local:
            cp.wait()

    return pl.pallas_call(
        body, name="scatter_grads",
        in_specs=_any_specs(2), out_specs=_any_specs(2),
        out_shape=[jax.ShapeDtypeStruct(gin_pieces.shape, gin_pieces.dtype),
                   jax.ShapeDtypeStruct((N_CHIPS, depth, rows, gw_out.shape[2]), gw_out.dtype)],
        scratch_shapes=[pltpu.SemaphoreType.DMA((6,)), pltpu.SemaphoreType.DMA((6,)), pltpu.SemaphoreType.DMA((2,))],
    )(gin_pieces, gw_out)


def _sum_chips(parts, rb, name):
    _, depth, r, cdim = parts.shape

    def body(p_ref, o_ref):
        part = lambda j: p_ref[j].astype(F32)
        o_ref[...] = ((part(0) + part(1)) + part(2)) + part(3)

    return pl.pallas_call(
        body, name=name, grid=(depth, r // rb),
        in_specs=[pl.BlockSpec((N_CHIPS, None, rb, cdim), lambda l, i: (0, l, i, 0))],
        out_specs=pl.BlockSpec((None, rb, cdim), lambda l, i: (l, i, 0)),
        out_shape=jax.ShapeDtypeStruct((depth, r, cdim), F32),
        compiler_params=_arb(2),
    )(parts)


def _swap_with_sibling(a, b):
    arrs = (a, b)
    n = len(arrs)

    def body(*refs):
        ins, outs = refs[:n], refs[n:2 * n]
        send_sems, recv_sems = refs[2 * n:]
        x, y, c, _ = _place()
        cps = [pltpu.make_async_remote_copy(src_ref=ins[k], dst_ref=outs[k], send_sem=send_sems.at[k],
                                            recv_sem=recv_sems.at[k], device_id=(x, y, 1 - c), device_id_type=MESH)
               for k in range(n)]
        for cp in cps:
            cp.start()
        for cp in cps:
            cp.wait()

    return pl.pallas_call(
        body, name="swap_with_sibling",
        in_specs=_any_specs(n), out_specs=_any_specs(n),
        out_shape=[jax.ShapeDtypeStruct(v.shape, v.dtype) for v in arrs],
        scratch_shapes=[pltpu.SemaphoreType.DMA((n,)), pltpu.SemaphoreType.DMA((n,))],
    )(*arrs)


def _allreduce_small(slab):
    m, n = slab.shape

    def body(x_ref, out_ref, gath, send_sems, recv_sems, local_sem):
        x, y, c, chips = _place()
        me, sibling = (x, y, c), (x, y, 1 - c)

        def rows(px, py, pc):
            return gath.at[pl.ds(pl.multiple_of((4 * px + 2 * py + pc) * m, 8), m), :]

        def copy(k, block, to, src=None):
            return pltpu.make_async_remote_copy(src_ref=rows(*block) if src is None else src, dst_ref=rows(*block),
                                                send_sem=send_sems.at[k], recv_sem=recv_sems.at[k],
                                                device_id=to, device_id_type=MESH)

        mine = pltpu.make_async_copy(x_ref, rows(*me), local_sem)
        mine.start()
        first = [copy(0, me, sibling, src=x_ref)]
        first += [copy(1 + j, me, (*chip, c), src=x_ref) for j, chip in enumerate(chips)]
        for cp in first:
            cp.start()
        passed = [copy(4 + j, (*chip, c), sibling) for j, chip in enumerate(chips)]
        for j, chip in enumerate(chips):
            copy(1 + j, (*chip, c), me).wait_recv()
            passed[j].start()
        copy(0, sibling, me).wait_recv()
        for j, chip in enumerate(chips):
            copy(4 + j, (*chip, 1 - c), me).wait_recv()
        for cp in first + passed:
            cp.wait_send()
        mine.wait()
        total = gath[0:m, :]
        for d in range(1, N_DEV):
            total = total + gath[d * m:(d + 1) * m, :]
        out_ref[...] = total

    return pl.pallas_call(
        body, name="allreduce_small",
        in_specs=[pl.BlockSpec(memory_space=pltpu.VMEM)],
        out_specs=pl.BlockSpec(memory_space=pltpu.VMEM),
        out_shape=jax.ShapeDtypeStruct((m, n), F32),
        scratch_shapes=[pltpu.VMEM((N_DEV * m, n), F32), pltpu.SemaphoreType.DMA((7,)), pltpu.SemaphoreType.DMA((7,)),
                        pltpu.SemaphoreType.DMA],
    )(slab)


def _adamw(w, ga, gb, m, v, rb, name):
    depth, r, cdim = w.shape

    def body(w_ref, ga_ref, gb_ref, m_ref, v_ref, g_out, d_out, m_out, v_out):
        g = ga_ref[...] + gb_ref[...]
        m2 = ADAM_B1 * m_ref[...] + (1.0 - ADAM_B1) * g
        v2 = ADAM_B2 * v_ref[...] + (1.0 - ADAM_B2) * (g * g)
        m_hat = m2 / (1.0 - ADAM_B1 ** ADAM_STEP)
        v_hat = v2 / (1.0 - ADAM_B2 ** ADAM_STEP)
        g_out[...] = g
        d_out[...] = -ADAM_LR * (m_hat / (jnp.sqrt(v_hat) + ADAM_EPS) + ADAM_WD * w_ref[...])
        m_out[...] = m2
        v_out[...] = v2

    spec = pl.BlockSpec((None, rb, cdim), lambda l, i: (l, i, 0))
    shp = jax.ShapeDtypeStruct((depth, r, cdim), F32)
    return pl.pallas_call(
        body, name=name, grid=(depth, r // rb),
        in_specs=[spec] * 5, out_specs=[spec] * 4, out_shape=[shp] * 4,
        compiler_params=_arb(2),
    )(w, ga, gb, m, v)


def _rope_tables(s):
    t = jnp.arange(s, dtype=jnp.int32)
    row = (t // GRID_W).astype(F32)
    col = (t % GRID_W).astype(F32)
    half = HEAD_DIM // 4
    inv_freq = ROPE_THETA ** (-jnp.arange(half, dtype=F32) / half)
    ar = row[:, None] * inv_freq[None, :]
    ac = col[:, None] * inv_freq[None, :]
    cos = jnp.concatenate([jnp.cos(ar), jnp.cos(ar), jnp.cos(ac), jnp.cos(ac)], axis=1)
    sin = jnp.concatenate([-jnp.sin(ar), jnp.sin(ar), -jnp.sin(ac), jnp.sin(ac)], axis=1)
    return jnp.tile(cos, (1, 2)), jnp.tile(sin, (1, 2))


def _pad_rows(a, rows):
    return jnp.pad(a, ((0, rows - a.shape[0]),) + ((0, 0),) * (a.ndim - 1))


_SMALL = ("pre_norm", "post_norm", "q_norm", "k_norm", "conv_dw_b", "conv_ln_g", "conv_ln_b", "sg_ln_g", "sg_ln_b",
          "sg_w", "sg_b")


def _pack(parts):
    flat = jnp.concatenate([p.reshape(-1, LANES) for p in parts], axis=0)
    return _pad_rows(flat, -(-flat.shape[0] // 8) * 8)


def _unpack(slab, shapes):
    out, r = [], 0
    for shp in shapes:
        n = 1
        for d in shp:
            n *= d
        out.append(slab[r:r + n // LANES].reshape(shp))
        r += n // LANES
    return out


def kernel(x, pre_norm, post_norm, w_in, w_out, q_norm, k_norm, conv_dw, conv_dw_b, conv_ln_g, conv_ln_b, sg_ln_g, sg_ln_b, sg_w, sg_b, loss_target, m_pre_norm, m_post_norm, m_w_in, m_w_out, m_q_norm, m_k_norm, m_conv_dw, m_conv_dw_b, m_conv_ln_g, m_conv_ln_b, m_sg_ln_g, m_sg_ln_b, m_sg_w, m_sg_b, v_pre_norm, v_post_norm, v_w_in, v_w_out, v_q_norm, v_k_norm, v_conv_dw, v_conv_dw_b, v_conv_ln_g, v_conv_ln_b, v_sg_ln_g, v_sg_ln_b, v_sg_w, v_sg_b):
    depth = w_in.shape[0]
    s = x.shape[1]
    assert x.shape[0] == 1 and s % SG_CHUNK == 0 and x.shape[2] == D_MODEL
    tq = min(256, s)
    tk = min(512, s // 2)
    t = min(256, tk)
    shard_cols = w_in.shape[2]
    chip = 2 * lax.axis_index("x") + lax.axis_index("y")

    gin, gout, gcdw = _gather_weights(w_in.astype(BF16), w_out.astype(BF16), conv_dw)
    w_in_bf = jnp.concatenate([gin[j] for j in range(N_CHIPS)], axis=2)
    w_out_bf = jnp.concatenate([gout[j] for j in range(N_CHIPS)], axis=1)
    cdw_full = jnp.concatenate([gcdw[j] for j in range(N_CHIPS)], axis=2)
    w_in_t_bf = jnp.swapaxes(w_in_bf, 1, 2)
    w_out_t_bf = jnp.swapaxes(w_out_bf, 1, 2)
    sgw_bf = sg_w.astype(BF16)
    sgwt_bf = jnp.swapaxes(sg_w, 2, 3).astype(BF16)

    cos, sin = _rope_tables(s)
    bd = jnp.kron(jnp.eye(N_HEADS, dtype=F32), jnp.full((HEAD_DIM, HEAD_DIM), 1.0 / HEAD_DIM, F32)).astype(BF16)

    def layer_consts(l):
        cw = _pad_rows(cdw_full[l], 32)
        cp = _pad_rows(jnp.stack([conv_dw_b[l], conv_ln_g[l], conv_ln_b[l]]), 8)
        sp = _pad_rows(jnp.stack([sg_ln_g[l], sg_ln_b[l]]), 8)
        sgb = jnp.repeat(sg_b[l].T, HEAD_DIM, axis=1)
        qg = jnp.tile(q_norm[l], N_HEADS)[None, :]
        kg = jnp.tile(k_norm[l], N_KV)[None, :]
        return cw, cp, sp, sgb, qg, kg

    xs = [x[0]]
    saved = []
    for l in range(depth):
        cw, cp, sp, sgb, qg, kg = layer_consts(l)
        proj, hc, q, k, v = _fwd_in(xs[l], pre_norm[l][None, :], w_in_bf[l], bd, qg, kg, cos, sin, t)
        o, qa = _flash_fwd(q, k, v, tq, tk)
        outs = _fwd_out(xs[l], proj, o, hc, cw, cp, sp, sgw_bf[l], sgb, w_out_bf[l], post_norm[l][None, :], t,
                        target=loss_target[0] if l == depth - 1 else None)
        saved.append((proj, hc, qa, k, v, o, outs[0], outs[2]))
        if l < depth - 1:
            xs.append(outs[1])
    dy, sq = outs[1], outs[3]
    loss = lax.psum(0.5 * jnp.sum(sq) / D_MODEL, ("x", "y", "c"))

    g_w_in, g_w_out, g_small = [], [], {n: [] for n in _SMALL + ("conv_dw",)}
    for l in reversed(range(depth)):
        cw, cp, sp, sgb, qg, kg = layer_consts(l)
        proj, hc, qa, k, v, o, mix, c0 = saved[l]
        doa, dgs, dc0, gwo, gpost, gcw, gvec, gsgw, gsgb = _bwd_out(
            dy, mix, proj, o, hc, c0, cw, cp, sp, sgw_bf[l], sgwt_bf[l], sgb, w_out_t_bf[l], post_norm[l][None, :], t)
        dq, dk, dv = _flash_bwd(qa, doa, k, v, tq, tk)
        dy, dproj_bf, h_bf, gpre, gq, gk = _bwd_in(dy, xs[l], proj, dq, dk, dv, dgs, dc0, cw, pre_norm[l][None, :],
                                                  w_in_t_bf[l], bd, qg, kg, cos, sin, t)
        g_w_in.append(_grad_w_in(h_bf, dproj_bf, min(512, s)))
        g_w_out.append(gwo)
        g_small["pre_norm"].append(gpre[0])
        g_small["post_norm"].append(gpost[0])
        g_small["q_norm"].append(gq[0].reshape(N_HEADS, HEAD_DIM).sum(0))
        g_small["k_norm"].append(gk[0].reshape(N_KV, HEAD_DIM).sum(0))
        g_small["conv_dw"].append(gcw[:CONV_K])
        g_small["conv_dw_b"].append(gvec[0])
        g_small["conv_ln_g"].append(gvec[1])
        g_small["conv_ln_b"].append(gvec[2])
        g_small["sg_ln_g"].append(gvec[3])
        g_small["sg_ln_b"].append(gvec[4])
        g_small["sg_w"].append(gsgw)
        g_small["sg_b"].append(gsgb.reshape(SG_CHUNK, SG_W // HEAD_DIM, HEAD_DIM).sum(-1).T)
    grad_x = dy[None]
    g_w_in = jnp.stack(g_w_in[::-1])
    g_w_out = jnp.stack(g_w_out[::-1])
    g_small = {n: jnp.stack(vals[::-1]) for n, vals in g_small.items()}

    gin_pieces = jnp.stack([g_w_in[:, :, j * shard_cols:(j + 1) * shard_cols] for j in range(N_CHIPS)])
    rin, rout = _scatter_grads(gin_pieces.astype(BF16), g_w_out.astype(BF16))
    s_in = _sum_chips(rin, 256, "sum_chips_w_in")
    s_out = _sum_chips(rout, 256, "sum_chips_w_out")
    t_in, t_out = _swap_with_sibling(s_in, s_out)
    grad_w_in, delta_w_in, new_m_w_in, new_v_w_in = _adamw(w_in, s_in, t_in, m_w_in, v_w_in, 256, "adamw_w_in")
    grad_w_out, delta_w_out, new_m_w_out, new_v_w_out = _adamw(w_out, s_out, t_out, m_w_out, v_w_out, 256, "adamw_w_out")

    small_w = dict(pre_norm=pre_norm, post_norm=post_norm, q_norm=q_norm, k_norm=k_norm, conv_dw_b=conv_dw_b,
                   conv_ln_g=conv_ln_g, conv_ln_b=conv_ln_b, sg_ln_g=sg_ln_g, sg_ln_b=sg_ln_b, sg_w=sg_w, sg_b=sg_b)
    small_m = dict(pre_norm=m_pre_norm, post_norm=m_post_norm, q_norm=m_q_norm, k_norm=m_k_norm, conv_dw_b=m_conv_dw_b,
                   conv_ln_g=m_conv_ln_g, conv_ln_b=m_conv_ln_b, sg_ln_g=m_sg_ln_g, sg_ln_b=m_sg_ln_b, sg_w=m_sg_w,
                   sg_b=m_sg_b)
    small_v = dict(pre_norm=v_pre_norm, post_norm=v_post_norm, q_norm=v_q_norm, k_norm=v_k_norm, conv_dw_b=v_conv_dw_b,
                   conv_ln_g=v_conv_ln_g, conv_ln_b=v_conv_ln_b, sg_ln_g=v_sg_ln_g, sg_ln_b=v_sg_ln_b, sg_w=v_sg_w,
                   sg_b=v_sg_b)
    shapes = [small_w[n].shape for n in _SMALL]
    red = _allreduce_small(_pack([g_small[n] for n in _SMALL] + [g_small["conv_dw"]]))
    n_rep = sum(small_w[n].size for n in _SMALL) // LANES
    g_cdw_full = red[n_rep:n_rep + g_small["conv_dw"].size // LANES].reshape(g_small["conv_dw"].shape)
    cdw_cols = conv_dw.shape[2]
    g_cdw = lax.dynamic_slice_in_dim(g_cdw_full, chip * cdw_cols, cdw_cols, axis=2)
    g_slab = _pack([red[:n_rep], g_cdw])
    w_slab = _pack([small_w[n] for n in _SMALL] + [conv_dw])
    m_slab = _pack([small_m[n] for n in _SMALL] + [m_conv_dw])
    v_slab = _pack([small_v[n] for n in _SMALL] + [v_conv_dw])
    rows = w_slab.shape[0]
    outs = _adamw(w_slab[None], g_slab[None], jnp.zeros_like(g_slab)[None], m_slab[None], v_slab[None], rows, "adamw_small")
    unpacked = [dict(zip(_SMALL + ("conv_dw",), _unpack(o_[0], shapes + [conv_dw.shape]))) for o_ in outs]

    big = [dict(w_in=a, w_out=b) for a, b in ((grad_w_in, grad_w_out), (delta_w_in, delta_w_out),
                                             (new_m_w_in, new_m_w_out), (new_v_w_in, new_v_w_out))]
    order = ("pre_norm", "post_norm", "w_in", "w_out", "q_norm", "k_norm", "conv_dw", "conv_dw_b", "conv_ln_g",
             "conv_ln_b", "sg_ln_g", "sg_ln_b", "sg_w", "sg_b")
    result = [loss, grad_x]
    for kind in range(4):
        for name in order:
            result.append(big[kind][name] if name in big[kind] else unpacked[kind][name])
    return tuple(result)
```

```python
import functools

import jax
import jax.numpy as jnp
from jax import lax
from jax.experimental import pallas as pl
from jax.experimental.pallas import tpu as pltpu

F32 = jnp.float32
BF16 = jnp.bfloat16
MESH = pl.DeviceIdType.MESH

EPS = 1e-6
D_MODEL = 1024
D_IN = 2816
HEAD_DIM = 64
LANES = 128
SUBLANES = 8
N_HEADS = 8
N_KV = 2
GROUP = N_HEADS // N_KV
GRID_W = 64
ROPE_THETA = 10000.0
CONV_K = 31
CONV_W = 256
SG_W = 256
SG_CHUNK = 128
HALO = 16
ATT_SCALE = HEAD_DIM ** -0.5

C_Q, C_K, C_V, C_GA, C_A1, C_A2, C_GC, C_U, C_VS, C_GS = 0, 512, 640, 768, 1280, 1536, 1792, 2048, 2304, 2560

ADAM_LR = 0.001
ADAM_B1 = 0.9
ADAM_B2 = 0.999
ADAM_EPS = 1e-08
ADAM_WD = 0.01
ADAM_STEP = 10

N_CHIPS = 4
N_DEV = 8


def _dot(a, b):
    return jnp.dot(a, b, preferred_element_type=F32)


def _group_mean(x, bd_bf):
    hi = x.astype(BF16)
    lo = (x - hi.astype(F32)).astype(BF16)
    return _dot(hi, bd_bf) + _dot(lo, bd_bf)


def _dot_nt(a, b):
    return lax.dot_general(a, b, (((1,), (1,)), ((), ())), preferred_element_type=F32)


def _dot_tn(a, b):
    return lax.dot_general(a, b, (((0,), (0,)), ((), ())), preferred_element_type=F32)


def _lane(shape):
    return lax.broadcasted_iota(jnp.int32, shape, 1)


def _sigmoid(x):
    return 1.0 / (1.0 + jnp.exp(-x))


def _silu_fwd_bwd(x):
    s = _sigmoid(x)
    return x * s, s * (1.0 + x * (1.0 - s))


def _erf(x):
    x = jnp.clip(x, -4.0, 4.0)
    x2 = x * x
    a = -2.72614225801306e-10
    a = a * x2 + 2.77068142495902e-08
    a = a * x2 + -2.10102402082508e-06
    a = a * x2 + -5.69250639462346e-05
    a = a * x2 + -7.34990630326855e-04
    a = a * x2 + -2.95459980854025e-03
    a = a * x2 + -1.60960333262415e-02
    b = -1.45660718464996e-05
    b = b * x2 + -2.13374055278905e-04
    b = b * x2 + -1.68282697438203e-03
    b = b * x2 + -7.37332916720468e-03
    b = b * x2 + -1.42647390514189e-02
    return x * a / b


def _gelu_fwd_bwd(x):
    cdf = 0.5 * (1.0 + _erf(x * 0.7071067811865476))
    pdf = jnp.exp(-0.5 * x * x) * 0.3989422804014327
    return x * cdf, cdf + x * pdf


def _rms(x):
    return lax.rsqrt(jnp.mean(x * x, axis=-1, keepdims=True) + EPS)


def _ln_hat(x):
    mu = jnp.mean(x, axis=-1, keepdims=True)
    xc = x - mu
    rs = lax.rsqrt(jnp.mean(xc * xc, axis=-1, keepdims=True) + EPS)
    return xc * rs, rs


def _ln_bwd(dxh, xh, rs):
    return rs * (dxh - jnp.mean(dxh, axis=-1, keepdims=True) - xh * jnp.mean(dxh * xh, axis=-1, keepdims=True))


def _swap16(z):
    parts = []
    for i in range(z.shape[1] // LANES):
        blk = z[:, i * LANES:(i + 1) * LANES]
        lane = _lane(blk.shape)
        parts.append(jnp.where((lane & 16) == 0, pltpu.roll(blk, LANES - 16, 1), pltpu.roll(blk, 16, 1)))
    return parts[0] if len(parts) == 1 else jnp.concatenate(parts, axis=1)


def _head_slab(pair, odd):
    src = pltpu.roll(pair, HEAD_DIM, 1) if odd else pair
    return jnp.where(_lane(pair.shape) < HEAD_DIM, src, 0.0)


def _pair_merge(even, odd):
    return jnp.where(_lane(even.shape) < HEAD_DIM, even, pltpu.roll(odd, HEAD_DIM, 1))


def _heads_to_cat(ref, n_heads):
    pairs = [_pair_merge(ref[2 * p], ref[2 * p + 1]) for p in range(n_heads // 2)]
    return pairs[0] if len(pairs) == 1 else jnp.concatenate(pairs, axis=1)


def _split3(x):
    hi = x.astype(BF16).astype(F32)
    r = x - hi
    mid = r.astype(BF16).astype(F32)
    lo = (r - mid).astype(BF16).astype(F32)
    return hi, mid, lo


def _with_spare(slab, hi, mid, lo):
    lane = _lane(slab.shape)
    return jnp.where(lane == HEAD_DIM, hi, jnp.where(lane == HEAD_DIM + 1, mid, jnp.where(lane == HEAD_DIM + 2, lo, slab)))


def _with_ones(slab):
    lane = _lane(slab.shape)
    return jnp.where((lane >= HEAD_DIM) & (lane < HEAD_DIM + 3), 1.0, slab)


def _conv_window(rot_ref, prev_ref, main, next_ref, first, last, t):
    n = t + 2 * HALO
    full = jnp.concatenate([jnp.where(first, 0.0, prev_ref[...]), main, jnp.where(last, 0.0, next_ref[...])], axis=0)
    rot_ref[0] = full
    for b in range(1, SUBLANES):
        rot_ref[b] = pltpu.roll(full, n - b, 0)


def _tap(rot_ref, start, t):
    a, b = divmod(start, SUBLANES)
    return rot_ref[b, SUBLANES * a:SUBLANES * a + t, :]


def _sgu_mix(v1_bf, w_ref, n_chunks):
    rows = []
    for n in range(n_chunks):
        pairs = []
        for p in range(SG_W // LANES):
            xp = v1_bf[n * SG_CHUNK:(n + 1) * SG_CHUNK, p * LANES:(p + 1) * LANES]
            me = _dot(w_ref[2 * p], xp)
            mo = _dot(w_ref[2 * p + 1], xp)
            pairs.append(jnp.where(_lane(me.shape) < HEAD_DIM, me, mo))
        rows.append(jnp.concatenate(pairs, axis=1))
    return rows[0] if len(rows) == 1 else jnp.concatenate(rows, axis=0)


def _halo_specs(t, s, width):
    per = t // HALO
    nblk = s // HALO
    prev = pl.BlockSpec((HALO, width), lambda i: (jnp.maximum(i * per - 1, 0), 0))
    nxt = pl.BlockSpec((HALO, width), lambda i: (jnp.minimum((i + 1) * per, nblk - 1), 0))
    return prev, nxt


def _const_spec(shape):
    nd = len(shape)
    return pl.BlockSpec(shape, lambda i: (0,) * nd)


def _arb(n=1):
    return pltpu.CompilerParams(dimension_semantics=("arbitrary",) * n)


def _fwd_in(x, g_pre, w_in_bf, bd, qg, kg, cos, sin, t, gather=()):
    s = x.shape[0]
    ex_args, ex_in, ex_out, ex_shapes, ex_sems, ex_hook = _riding_exchange(False, gather)
    n_ex = len(gather)

    def body(*refs):
        x_ref, g_ref, w_ref, bd_ref, qg_ref, kg_ref, cos_ref, sin_ref = refs[:8]
        proj_ref, hc_ref, q_ref, k_ref, v_ref = refs[8 + n_ex:13 + n_ex]
        if n_ex:
            ex_hook(refs[8:8 + n_ex], refs[13 + n_ex:13 + 2 * n_ex], refs[13 + 2 * n_ex:])
        xv = x_ref[...]
        h = (xv * _rms(xv) * g_ref[...]).astype(BF16)
        proj = _dot(h, w_ref[...])
        proj_ref[...] = proj
        cos_pair = cos_ref[...]
        sin_pair = sin_ref[...]
        cosv = jnp.concatenate([cos_pair] * (N_HEADS // 2), axis=1)
        sinv = jnp.concatenate([sin_pair] * (N_HEADS // 2), axis=1)
        q = proj[:, C_Q:C_K]
        qn = q * lax.rsqrt(_group_mean(q * q, bd_ref[...]) + EPS) * qg_ref[...]
        qr = (qn * cosv + _swap16(qn) * sinv) * ATT_SCALE
        for hh in range(N_HEADS):
            pair = qr[:, (hh // 2) * LANES:(hh // 2 + 1) * LANES]
            q_ref[hh] = _head_slab(pair, hh % 2 == 1).astype(BF16)
        k = proj[:, C_K:C_V]
        kn = k * lax.rsqrt(_group_mean(k * k, bd_ref[0:LANES, 0:LANES]) + EPS) * kg_ref[...]
        kr = kn * cos_pair + _swap16(kn) * sin_pair
        vv = proj[:, C_V:C_GA]
        for hh in range(N_KV):
            k_ref[hh] = _with_ones(_head_slab(kr, hh == 1)).astype(BF16)
            v_ref[hh] = _with_ones(_head_slab(vv, hh == 1)).astype(BF16)
        hc_ref[...] = proj[:, C_A1:C_A2] * _sigmoid(proj[:, C_A2:C_GC])

    row = lambda w: pl.BlockSpec((t, w), lambda i: (i, 0))
    heads = lambda n: pl.BlockSpec((n, t, LANES), lambda i: (0, i, 0))
    return pl.pallas_call(
        body, name="fwd_in_gather" if n_ex else "fwd_in", grid=(s // t,),
        in_specs=[row(D_MODEL), _const_spec((1, D_MODEL)), _const_spec((D_MODEL, D_IN)), _const_spec((512, 512)),
                  _const_spec((1, 512)), _const_spec((1, LANES)), row(LANES), row(LANES)] + ex_in,
        out_specs=[row(D_IN), row(CONV_W), heads(N_HEADS), heads(N_KV), heads(N_KV)] + ex_out,
        out_shape=[jax.ShapeDtypeStruct((s, D_IN), F32), jax.ShapeDtypeStruct((s, CONV_W), F32),
                   jax.ShapeDtypeStruct((N_HEADS, s, LANES), BF16), jax.ShapeDtypeStruct((N_KV, s, LANES), BF16),
                   jax.ShapeDtypeStruct((N_KV, s, LANES), BF16)] + ex_shapes,
        scratch_shapes=ex_sems if n_ex else [],
        compiler_params=_arb(),
    )(x, g_pre, w_in_bf, bd, qg, kg, cos, sin, *ex_args)


def _chunk_rows(c, tk):
    return pl.ds(c * tk, tk) if isinstance(c, int) else pl.ds(pl.multiple_of(c * tk, tk), tk)


def _three_stage_pipeline(nk, per_trip, stage1, stage2, stage3, peel):
    assert nk % 2 == 0 and per_trip % 2 == 0

    def step(t, parity, first=False, last=False):
        if not last:
            stage1(t + 1, 1 - parity)
        stage2(parity)
        if not first:
            stage3(t - 1, 1 - parity)

    stage1(0, 0)
    if not peel:
        while nk % per_trip:
            per_trip //= 2

        def whole_trip(i, carry):
            for u in range(per_trip):
                c = per_trip * i + u
                stage1(jnp.minimum(c + 1, nk - 1), 1 - u % 2)
                stage2(u % 2)
                stage3(jnp.maximum(c - 1, 0), 1 - u % 2)
            return carry

        lax.fori_loop(0, nk // per_trip, whole_trip, 0)
        stage3(nk - 1, 1)
        return

    step(0, 0, first=True)
    n_trips, left = divmod(nk - 2, per_trip)

    def trip(i, carry):
        for u in range(per_trip):
            step(1 + per_trip * i + u, (1 + u) % 2)
        return carry

    if n_trips:
        lax.fori_loop(0, n_trips, trip, 0)
    for t in range(1 + n_trips * per_trip, 1 + n_trips * per_trip + left):
        step(t, t % 2)
    step(nk - 1, 1, last=True)
    stage3(nk - 1, 1)


def _flash_fwd(q, k, v, tq, tk):
    s = q.shape[1]
    rows = GROUP * tq
    nk = s // tk

    def body(q_ref, k_ref, vt_ref, o_ref, qa_ref, m_scr, acc_scr, s0, s1, p0, p1, a0, a1):
        s_bufs, p_bufs, a_bufs = (s0, s1), (p0, p1), (a0, a1)
        qv = q_ref[...].reshape(rows, LANES)
        q_t = qv.astype(F32).T
        q_t_bf = q_t.astype(BF16)
        m_scr[...] = jnp.full((1, rows), -jnp.inf, F32)
        acc_scr[...] = jnp.zeros((LANES, rows), F32)

        def scores(c, slot):
            s_bufs[slot][...] = _dot(k_ref[_chunk_rows(c, tk), :], q_t_bf)

        def softmax(slot):
            for h in range(GROUP):
                r = slice(h * tq, (h + 1) * tq)
                sc = s_bufs[slot][:, r]
                m_prev = m_scr[:, r]
                m_new = jnp.maximum(m_prev, jnp.max(sc, axis=0, keepdims=True))
                p_bufs[slot][:, r] = jnp.exp((sc - m_new).astype(BF16))
                a_bufs[slot][:, r] = jnp.exp(m_prev - m_new)
                m_scr[:, r] = m_new

        def weighted_values(c, slot):
            acc_scr[...] = a_bufs[slot][...] * acc_scr[...] + _dot(vt_ref[c], p_bufs[slot][...])

        _three_stage_pipeline(nk, 4, scores, softmax, weighted_values, peel=True)

        acc = acc_scr[...]
        row = lax.broadcasted_iota(jnp.int32, acc.shape, 0)
        l = jnp.sum(jnp.where(row == HEAD_DIM, acc, 0.0), axis=0, keepdims=True)
        o_ref[...] = jnp.where(row < HEAD_DIM, acc / l, 0.0).T.reshape(GROUP, tq, LANES)
        hi, mid, lo = _split3(-(m_scr[...] + jnp.log(l)))
        qa_t = jnp.where(row == HEAD_DIM, hi, jnp.where(row == HEAD_DIM + 1, mid,
                                                        jnp.where(row == HEAD_DIM + 2, lo, q_t)))
        qa_ref[...] = qa_t.T.astype(BF16).reshape(GROUP, tq, LANES)

    qspec = pl.BlockSpec((GROUP, tq, LANES), lambda j, i: (j, i, 0))
    kspec = pl.BlockSpec((None, s, LANES), lambda j, i: (j, 0, 0))
    vtspec = pl.BlockSpec((None, nk, LANES, tk), lambda j, i: (j, 0, 0, 0))
    v_t = jnp.swapaxes(v.reshape(N_KV, nk, tk, LANES), 2, 3)
    return pl.pallas_call(
        body, name="flash_fwd", grid=(N_KV, s // tq),
        in_specs=[qspec, kspec, vtspec],
        out_specs=[qspec, qspec],
        out_shape=[jax.ShapeDtypeStruct((N_HEADS, s, LANES), F32), jax.ShapeDtypeStruct((N_HEADS, s, LANES), BF16)],
        scratch_shapes=[pltpu.VMEM((1, rows), F32), pltpu.VMEM((LANES, rows), F32),
                        pltpu.VMEM((tk, rows), F32), pltpu.VMEM((tk, rows), F32),
                        pltpu.VMEM((tk, rows), BF16), pltpu.VMEM((tk, rows), BF16),
                        pltpu.VMEM((1, rows), F32), pltpu.VMEM((1, rows), F32)],
        compiler_params=_arb(2),
    )(q, k, v_t)


def _groups_fwd(proj_ref, o_ref, hext_ref, cw_ref, cp_ref, sp_ref, sgw_ref, sgb_ref, t, c0=None):
    proj = proj_ref[...]
    r = {}
    r["att"] = _heads_to_cat(o_ref, N_HEADS)
    r["gate_a"], r["dgate_a"] = _silu_fwd_bwd(proj[:, C_GA:C_A1])
    r["att_g"] = r["att"] * r["gate_a"]
    if c0 is None:
        c0 = jnp.zeros((t, CONV_W), F32) + cp_ref[0:1, :]
        for kk in range(CONV_K):
            c0 = c0 + cw_ref[kk:kk + 1, :] * _tap(hext_ref, kk + 1, t)
    r["c0"] = c0
    r["xh_c"], r["rs_c"] = _ln_hat(c0)
    r["c1"] = r["xh_c"] * cp_ref[1:2, :] + cp_ref[2:3, :]
    r["sg_c1"] = _sigmoid(r["c1"])
    r["c2"] = r["c1"] * r["sg_c1"]
    r["gate_c"], r["dgate_c"] = _silu_fwd_bwd(proj[:, C_GC:C_U])
    r["cnv_g"] = r["c2"] * r["gate_c"]
    r["gu"], r["dgu"] = _gelu_fwd_bwd(proj[:, C_U:C_VS])
    gv, r["dgv"] = _gelu_fwd_bwd(proj[:, C_VS:C_GS])
    r["xh_s"], r["rs_s"] = _ln_hat(gv)
    v1 = r["xh_s"] * sp_ref[0:1, :] + sp_ref[1:2, :]
    r["v1_bf"] = v1.astype(BF16)
    r["mixed"] = _sgu_mix(r["v1_bf"], sgw_ref, t // SG_CHUNK) + jnp.concatenate([sgb_ref[...]] * (t // SG_CHUNK), axis=0)
    r["um"] = r["gu"] * r["mixed"]
    r["gate_s"], r["dgate_s"] = _silu_fwd_bwd(proj[:, C_GS:D_IN])
    r["sgu_g"] = r["um"] * r["gate_s"]
    r["mc_bf"] = jnp.concatenate([r["att_g"], r["cnv_g"], r["sgu_g"]], axis=1).astype(BF16)
    return r


def _fwd_out(x, proj, o, hc, cw, cp, sp, sgw_bf, sgb, w_out_bf, g_post, t, target=None):
    s = x.shape[0]
    last_layer = target is not None

    def body(*refs):
        (x_ref, proj_ref, o_ref, hc_ref, hp_ref, hn_ref, cw_ref, cp_ref, sp_ref, sgw_ref, sgb_ref,
         w_ref, g_ref) = refs[:13]
        rest = refs[13:]
        if last_layer:
            t_ref, mix_ref, out_ref, c0_ref, sq_ref, hext_ref = rest
        else:
            mix_ref, out_ref, c0_ref, hext_ref = rest
        i = pl.program_id(0)
        _conv_window(hext_ref, hp_ref, hc_ref[...], hn_ref, i == 0, i == pl.num_programs(0) - 1, t)
        r = _groups_fwd(proj_ref, o_ref, hext_ref, cw_ref, cp_ref, sp_ref, sgw_ref, sgb_ref, t)
        c0_ref[...] = r["c0"]
        mix = _dot(r["mc_bf"], w_ref[...])
        mix_ref[...] = mix
        y = x_ref[...] + mix * _rms(mix) * g_ref[...]
        if last_layer:
            @pl.when(i == 0)
            def _():
                sq_ref[...] = jnp.zeros_like(sq_ref)

            err = y - t_ref[...]
            out_ref[...] = err * (1.0 / D_MODEL)
            sq_ref[...] += jnp.sum(err * err, axis=0, keepdims=True)
        else:
            out_ref[...] = y

    row = lambda w: pl.BlockSpec((t, w), lambda i: (i, 0))
    hprev, hnext = _halo_specs(t, s, CONV_W)
    big = jax.ShapeDtypeStruct((s, D_MODEL), F32)
    return pl.pallas_call(
        body, name="fwd_out_loss" if last_layer else "fwd_out", grid=(s // t,),
        in_specs=[row(D_MODEL), row(D_IN), pl.BlockSpec((N_HEADS, t, LANES), lambda i: (0, i, 0)), row(CONV_W),
                  hprev, hnext, _const_spec((32, CONV_W)), _const_spec((8, CONV_W)), _const_spec((8, SG_W)),
                  _const_spec((4, SG_CHUNK, SG_CHUNK)), _const_spec((SG_CHUNK, SG_W)),
                  _const_spec((D_MODEL, D_MODEL)), _const_spec((1, D_MODEL))] + ([row(D_MODEL)] if last_layer else []),
        out_specs=[row(D_MODEL), row(D_MODEL), row(CONV_W)] + ([_const_spec((1, D_MODEL))] if last_layer else []),
        out_shape=[big, big, jax.ShapeDtypeStruct((s, CONV_W), F32)]
        + ([jax.ShapeDtypeStruct((1, D_MODEL), F32)] if last_layer else []),
        scratch_shapes=[pltpu.VMEM((SUBLANES, t + 2 * HALO, CONV_W), F32)],
        compiler_params=_arb(),
    )(*((x, proj, o, hc, hc, hc, cw, cp, sp, sgw_bf, sgb, w_out_bf, g_post) + ((target,) if last_layer else ())))


def _bwd_out(dy, mix, proj, o, hc, c0, cw, cp, sp, sgw_bf, sgwt_bf, sgb, w_out_t_bf, g_post, t, scatter=()):
    s = dy.shape[0]
    n_chunks = t // SG_CHUNK
    ex_args, ex_in, ex_out, ex_shapes, ex_sems, ex_hook = _riding_exchange(True, scatter)
    n_ex = len(scatter)

    def body(*refs):
        (dy_ref, mix_ref, proj_ref, o_ref, hc_ref, hp_ref, hn_ref, c0_ref, cw_ref, cp_ref, sp_ref, sgw_ref,
         sgwt_ref, sgb_ref, wt_ref, g_ref) = refs[:16]
        (do_ref, dgs_ref, dc0_ref, gwo_ref, gpost_ref, gcw_ref, gvec_ref, gsgw_ref,
         gsgb_ref) = refs[16 + n_ex:25 + n_ex]
        hext_ref = refs[25 + 2 * n_ex]
        if n_ex:
            ex_hook(refs[16:16 + n_ex], refs[25 + n_ex:25 + 2 * n_ex], refs[26 + 2 * n_ex:])
        i = pl.program_id(0)

        @pl.when(i == 0)
        def _():
            for ref in (gwo_ref, gpost_ref, gcw_ref, gvec_ref, gsgw_ref, gsgb_ref):
                ref[...] = jnp.zeros_like(ref)

        _conv_window(hext_ref, hp_ref, hc_ref[...], hn_ref, i == 0, i == pl.num_programs(0) - 1, t)
        r = _groups_fwd(proj_ref, o_ref, hext_ref, cw_ref, cp_ref, sp_ref, sgw_ref, sgb_ref, t, c0=c0_ref[...])

        dyv = dy_ref[...]
        mix_v = mix_ref[...]
        rr = _rms(mix_v)
        gd = dyv * g_ref[...]
        dmix = rr * gd - mix_v * (rr * rr * rr * jnp.mean(gd * mix_v, axis=-1, keepdims=True))
        gpost_ref[...] += jnp.sum(dyv * mix_v * rr, axis=0, keepdims=True)
        dmix_bf = dmix.astype(BF16)
        gwo_ref[...] += _dot_tn(r["mc_bf"], dmix_bf)
        dmc = _dot(dmix_bf, wt_ref[...])

        d_att = dmc[:, 0:512]
        dg_att = d_att * r["att"] * r["dgate_a"]
        d_o = d_att * r["gate_a"]
        prod = d_o * r["att"]
        for p in range(N_HEADS // 2):
            sl = slice(p * LANES, (p + 1) * LANES)
            pr = prod[:, sl]
            tot = jnp.sum(pr, axis=1, keepdims=True)
            ev = jnp.sum(jnp.where(_lane(pr.shape) < HEAD_DIM, pr, 0.0), axis=1, keepdims=True)
            for odd, delta in ((False, ev), (True, tot - ev)):
                hi, mid, lo = _split3(-delta)
                do_ref[2 * p + int(odd)] = _with_spare(_head_slab(d_o[:, sl], odd), hi, mid, lo).astype(BF16)

        dcv = dmc[:, 512:768]
        dg_conv = dcv * r["c2"] * r["dgate_c"]
        dc1 = dcv * r["gate_c"] * (r["sg_c1"] * (1.0 + r["c1"] * (1.0 - r["sg_c1"])))
        dc0 = _ln_bwd(dc1 * cp_ref[1:2, :], r["xh_c"], r["rs_c"])
        dc0_ref[...] = dc0
        for kk in range(CONV_K):
            gcw_ref[kk:kk + 1, :] += jnp.sum(dc0 * _tap(hext_ref, kk + 1, t), axis=0, keepdims=True)

        dsg = dmc[:, 768:1024]
        dg_sg = dsg * r["um"] * r["dgate_s"]
        du = dsg * r["mixed"] * r["gate_s"] * r["dgu"]
        dmx = dsg * r["gu"] * r["gate_s"]
        dmx_bf = dmx.astype(BF16)
        sgb_sum = dmx[0:SG_CHUNK, :]
        for n in range(1, n_chunks):
            sgb_sum = sgb_sum + dmx[n * SG_CHUNK:(n + 1) * SG_CHUNK, :]
        gsgb_ref[...] += sgb_sum
        dv1_rows = []
        for n in range(n_chunks):
            pairs = []
            for p in range(SG_W // LANES):
                rs_ = slice(n * SG_CHUNK, (n + 1) * SG_CHUNK)
                ls_ = slice(p * LANES, (p + 1) * LANES)
                dm = dmx_bf[rs_, ls_]
                xp = r["v1_bf"][rs_, ls_]
                low = _lane(dm.shape) < HEAD_DIM
                zero = jnp.zeros_like(dm)
                gsgw_ref[2 * p] += _dot_nt(jnp.where(low, dm, zero), xp)
                gsgw_ref[2 * p + 1] += _dot_nt(jnp.where(low, zero, dm), xp)
                pairs.append(jnp.where(low, _dot(sgwt_ref[2 * p], dm), _dot(sgwt_ref[2 * p + 1], dm)))
            dv1_rows.append(jnp.concatenate(pairs, axis=1))
        dv1 = dv1_rows[0] if n_chunks == 1 else jnp.concatenate(dv1_rows, axis=0)
        dvs = _ln_bwd(dv1 * sp_ref[0:1, :], r["xh_s"], r["rs_s"]) * r["dgv"]

        zrow = jnp.zeros((1, CONV_W), F32)
        gvec_ref[...] += jnp.concatenate([
            jnp.sum(dc0, axis=0, keepdims=True),
            jnp.sum(dc1 * r["xh_c"], axis=0, keepdims=True),
            jnp.sum(dc1, axis=0, keepdims=True),
            jnp.sum(dv1 * r["xh_s"], axis=0, keepdims=True),
            jnp.sum(dv1, axis=0, keepdims=True),
            zrow, zrow, zrow], axis=0)
        dgs_ref[...] = jnp.concatenate([dg_att, dg_conv, du, dvs, dg_sg], axis=1)

    row = lambda w: pl.BlockSpec((t, w), lambda i: (i, 0))
    heads = pl.BlockSpec((N_HEADS, t, LANES), lambda i: (0, i, 0))
    hprev, hnext = _halo_specs(t, s, CONV_W)
    return pl.pallas_call(
        body, name="bwd_out_scatter" if n_ex else "bwd_out", grid=(s // t,),
        in_specs=[row(D_MODEL), row(D_MODEL), row(D_IN), heads, row(CONV_W), hprev, hnext, row(CONV_W),
                  _const_spec((32, CONV_W)), _const_spec((8, CONV_W)), _const_spec((8, SG_W)),
                  _const_spec((4, SG_CHUNK, SG_CHUNK)), _const_spec((4, SG_CHUNK, SG_CHUNK)),
                  _const_spec((SG_CHUNK, SG_W)), _const_spec((D_MODEL, D_MODEL)), _const_spec((1, D_MODEL))] + ex_in,
        out_specs=[heads, row(1536), row(CONV_W), _const_spec((D_MODEL, D_MODEL)), _const_spec((1, D_MODEL)),
                   _const_spec((32, CONV_W)), _const_spec((8, CONV_W)), _const_spec((4, SG_CHUNK, SG_CHUNK)),
                   _const_spec((SG_CHUNK, SG_W))] + ex_out,
        out_shape=[jax.ShapeDtypeStruct((N_HEADS, s, LANES), BF16), jax.ShapeDtypeStruct((s, 1536), F32),
                   jax.ShapeDtypeStruct((s, CONV_W), F32), jax.ShapeDtypeStruct((D_MODEL, D_MODEL), F32),
                   jax.ShapeDtypeStruct((1, D_MODEL), F32), jax.ShapeDtypeStruct((32, CONV_W), F32),
                   jax.ShapeDtypeStruct((8, CONV_W), F32), jax.ShapeDtypeStruct((4, SG_CHUNK, SG_CHUNK), F32),
                   jax.ShapeDtypeStruct((SG_CHUNK, SG_W), F32)] + ex_shapes,
        scratch_shapes=[pltpu.VMEM((SUBLANES, t + 2 * HALO, CONV_W), F32)] + (ex_sems if n_ex else []),
        compiler_params=_arb(),
    )(dy, mix, proj, o, hc, hc, hc, c0, cw, cp, sp, sgw_bf, sgwt_bf, sgb, w_out_t_bf, g_post, *ex_args)


def _flash_bwd(qa, doa, k, v, tq, tk):
    s = qa.shape[1]
    rows = GROUP * tq
    nk = s // tk
    n_q = s // tq

    def body(qa_ref, do_ref, k_ref, v_ref, dq_ref, dk_hbm, dv_hbm,
             dq_scr, dk_scr, dv_scr, s0, s1, d0, d1, p0, p1, e0, e1, sems):
        j, i = pl.program_id(0), pl.program_id(1)
        s_bufs, d_bufs, p_bufs, e_bufs = (s0, s1), (d0, d1), (p0, p1), (e0, e1)
        qv = qa_ref[...].reshape(rows, LANES)
        dov = do_ref[...].reshape(rows, LANES)
        q_t = qv.astype(F32).T.astype(BF16)
        do_t = dov.astype(F32).T.astype(BF16)
        dq_scr[...] = jnp.zeros((rows, LANES), F32)

        @pl.when(i == 0)
        def _():
            dk_scr[...] = jnp.zeros_like(dk_scr)
            dv_scr[...] = jnp.zeros_like(dv_scr)

        def at(c):
            return _chunk_rows(c, tk)

        def scores(c, slot):
            s_bufs[slot][...] = _dot_nt(qv, k_ref[at(c), :])
            d_bufs[slot][...] = _dot_nt(dov, v_ref[at(c), :])

        def probs(slot):
            for h in range(GROUP):
                r = slice(h * tq, (h + 1) * tq)
                p = jnp.exp(s_bufs[slot][r, :])
                p_bufs[slot][r, :] = p.astype(BF16)
                e_bufs[slot][r, :] = (p * d_bufs[slot][r, :]).astype(BF16)

        def grads(c, slot):
            ds = e_bufs[slot][...]
            dq_scr[...] += _dot(ds, k_ref[at(c), :])
            dv_scr[c] += _dot(do_t, p_bufs[slot][...])
            dk_scr[c] += _dot(q_t, ds)

        p1[...] = jnp.zeros((rows, tk), BF16)
        e1[...] = jnp.zeros((rows, tk), BF16)
        _three_stage_pipeline(nk, 4, scores, probs, grads, peel=False)
        dq_ref[...] = dq_scr[...].reshape(GROUP, tq, LANES)

        @pl.when(i == n_q - 1)
        def _():
            out = [pltpu.make_async_copy(dk_scr, dk_hbm.at[j], sems.at[0]),
                   pltpu.make_async_copy(dv_scr, dv_hbm.at[j], sems.at[1])]
            for cp in out:
                cp.start()
            for cp in out:
                cp.wait()

    qspec = pl.BlockSpec((GROUP, tq, LANES), lambda j, i: (j, i, 0))
    kvspec = pl.BlockSpec((None, s, LANES), lambda j, i: (j, 0, 0))
    hbm = pl.BlockSpec(memory_space=pl.ANY)
    stage_f32 = pltpu.VMEM((rows, tk), F32)
    stage_bf = pltpu.VMEM((rows, tk), BF16)
    kv_t = jax.ShapeDtypeStruct((N_KV, nk, LANES, tk), F32)
    return pl.pallas_call(
        body, name="flash_bwd", grid=(N_KV, n_q),
        in_specs=[qspec, qspec, kvspec, kvspec],
        out_specs=[qspec, hbm, hbm],
        out_shape=[jax.ShapeDtypeStruct((N_HEADS, s, LANES), F32), kv_t, kv_t],
        scratch_shapes=[pltpu.VMEM((rows, LANES), F32), pltpu.VMEM((nk, LANES, tk), F32), pltpu.VMEM((nk, LANES, tk), F32),
                        stage_f32, stage_f32, stage_f32, stage_f32, stage_bf, stage_bf, stage_bf, stage_bf,
                        pltpu.SemaphoreType.DMA((2,))],
        compiler_params=_arb(2),
    )(qa, doa, k, v)


def _bwd_in(dy, x, proj, dq, dk, dv, dgs, dc0, cw, g_pre, w_in_t_bf, bd, qg, kg, cos, sin, t):
    s = x.shape[0]

    def body(dy_ref, x_ref, proj_ref, dq_ref, dk_ref, dv_ref, dgs_ref, dc_ref, dcp_ref, dcn_ref, cw_ref, g_ref,
             wt_ref, bd_ref, qg_ref, kg_ref, cos_ref, sin_ref,
             dx_ref, dproj_ref, h_ref, gpre_ref, gq_ref, gk_ref, dext_ref):
        i = pl.program_id(0)

        @pl.when(i == 0)
        def _():
            for ref in (gpre_ref, gq_ref, gk_ref):
                ref[...] = jnp.zeros_like(ref)

        proj = proj_ref[...]
        cos_pair = cos_ref[...]
        sin_pair = sin_ref[...]
        cosv = jnp.concatenate([cos_pair] * (N_HEADS // 2), axis=1)
        sinv = jnp.concatenate([sin_pair] * (N_HEADS // 2), axis=1)

        def head_norm_bwd(dr, z, bdm, g, cs, sn, gacc_ref):
            dn = dr * cs + _swap16(dr * sn)
            rr = lax.rsqrt(_group_mean(z * z, bdm) + EPS)
            gdn = dn * g
            gacc_ref[...] += jnp.sum(dn * z * rr, axis=0, keepdims=True)
            return rr * gdn - z * (rr * rr * rr * _group_mean(gdn * z, bdm))

        dq_cat = _heads_to_cat(dq_ref, N_HEADS) * ATT_SCALE
        dzq = head_norm_bwd(dq_cat, proj[:, C_Q:C_K], bd_ref[...], qg_ref[...], cosv, sinv, gq_ref)

        def kv_pair(ref):
            return jnp.concatenate([ref[0, 0:HEAD_DIM, :], ref[1, 0:HEAD_DIM, :]], axis=0).T

        dk_cat = kv_pair(dk_ref)
        dzk = head_norm_bwd(dk_cat, proj[:, C_K:C_V], bd_ref[0:LANES, 0:LANES], kg_ref[...],
                            cos_pair, sin_pair, gk_ref)
        dv_cat = kv_pair(dv_ref)

        _conv_window(dext_ref, dcp_ref, dc_ref[...], dcn_ref, i == 0, i == pl.num_programs(0) - 1, t)
        dhc = jnp.zeros((t, CONV_W), F32)
        for kk in range(CONV_K):
            dhc = dhc + cw_ref[kk:kk + 1, :] * _tap(dext_ref, CONV_K - kk, t)
        sg = _sigmoid(proj[:, C_A2:C_GC])
        da1 = dhc * sg
        da2 = dhc * proj[:, C_A1:C_A2] * sg * (1.0 - sg)

        dgs = dgs_ref[...]
        dproj_bf = jnp.concatenate([dzq, dzk, dv_cat, dgs[:, 0:512], da1, da2, dgs[:, 512:1536]], axis=1).astype(BF16)
        dproj_ref[...] = dproj_bf
        dh = _dot(dproj_bf, wt_ref[...])

        xv = x_ref[...]
        rr = _rms(xv)
        gv = g_ref[...]
        h_ref[...] = (xv * rr * gv).astype(BF16)
        gdh = dh * gv
        gpre_ref[...] += jnp.sum(dh * xv * rr, axis=0, keepdims=True)
        dx_ref[...] = dy_ref[...] + rr * gdh - xv * (rr * rr * rr * jnp.mean(gdh * xv, axis=-1, keepdims=True))

    row = lambda w: pl.BlockSpec((t, w), lambda i: (i, 0))
    heads = lambda n: pl.BlockSpec((n, t, LANES), lambda i: (0, i, 0))
    hprev, hnext = _halo_specs(t, s, CONV_W)
    tk = dk.shape[3]
    assert tk % t == 0
    kv_t = pl.BlockSpec((N_KV, None, LANES, t), lambda i: (0, i // (tk // t), 0, i % (tk // t)))
    return pl.pallas_call(
        body, name="bwd_in", grid=(s // t,),
        in_specs=[row(D_MODEL), row(D_MODEL), row(D_IN), heads(N_HEADS), kv_t, kv_t, row(1536),
                  row(CONV_W), hprev, hnext, _const_spec((32, CONV_W)), _const_spec((1, D_MODEL)),
                  _const_spec((D_IN, D_MODEL)), _const_spec((512, 512)), _const_spec((1, 512)),
                  _const_spec((1, LANES)), row(LANES), row(LANES)],
        out_specs=[row(D_MODEL), row(D_IN), row(D_MODEL), _const_spec((1, D_MODEL)), _const_spec((1, 512)),
                   _const_spec((1, LANES))],
        out_shape=[jax.ShapeDtypeStruct((s, D_MODEL), F32), jax.ShapeDtypeStruct((s, D_IN), BF16),
                   jax.ShapeDtypeStruct((s, D_MODEL), BF16), jax.ShapeDtypeStruct((1, D_MODEL), F32),
                   jax.ShapeDtypeStruct((1, 512), F32), jax.ShapeDtypeStruct((1, LANES), F32)],
        scratch_shapes=[pltpu.VMEM((SUBLANES, t + 2 * HALO, CONV_W), F32)],
        compiler_params=_arb(),
    )(dy, x, proj, dq, dk, dv, dgs, dc0, dc0, dc0, cw, g_pre, w_in_t_bf, bd, qg, kg, cos, sin)


def _grad_w_in(h_bf, dproj_bf, t):
    s = h_bf.shape[0]
    half = D_IN // 2

    def body(h_ref, d_ref, g_ref):
        @pl.when(pl.program_id(1) == 0)
        def _():
            g_ref[...] = jnp.zeros_like(g_ref)

        g_ref[...] += _dot_tn(h_ref[...], d_ref[...])

    return pl.pallas_call(
        body, name="grad_w_in", grid=(2, s // t),
        in_specs=[pl.BlockSpec((t, D_MODEL), lambda j, i: (i, 0)), pl.BlockSpec((t, half), lambda j, i: (i, j))],
        out_specs=pl.BlockSpec((D_MODEL, half), lambda j, i: (0, j)),
        out_shape=jax.ShapeDtypeStruct((D_MODEL, D_IN), F32),
        compiler_params=_arb(2),
    )(h_bf, dproj_bf)


def _place():
    x, y, c = lax.axis_index("x"), lax.axis_index("y"), lax.axis_index("c")
    chips = [(1 - x, y), (x, 1 - y), (1 - x, 1 - y)]
    return x, y, c, chips


def _any_specs(n):
    return [pl.BlockSpec(memory_space=pl.ANY)] * n


class _ChipExchange:
    def __init__(self, scatter, ins, outs, send_sems, recv_sems, local_sems):
        n = len(ins)
        x, y, c, chips = _place()
        mine = 2 * x + y
        src = (lambda a, piece: ins[a].at[piece]) if scatter else (lambda a, piece: ins[a])

        def copy(j, a, piece, slot, to):
            return pltpu.make_async_remote_copy(src_ref=src(a, piece), dst_ref=outs[a].at[slot],
                                                send_sem=send_sems.at[n * j + a], recv_sem=recv_sems.at[n * j + a],
                                                device_id=to, device_id_type=MESH)

        self.local = [pltpu.make_async_copy(src(a, mine), outs[a].at[mine], local_sems.at[a]) for a in range(n)]
        self.sends = [copy(j, a, 2 * px + py, mine, (px, py, c)) for j, (px, py) in enumerate(chips) for a in range(n)]
        self.arrivals = lambda: [copy(j, a, mine, 2 * px + py, (px, py, c))
                                 for j, (px, py) in enumerate(chips) for a in range(n)]

    def start(self):
        for cp in self.local + self.sends:
            cp.start()

    def finish(self):
        for cp in self.arrivals():
            cp.wait_recv()
        for cp in self.sends:
            cp.wait_send()
        for cp in self.local:
            cp.wait()

    @staticmethod
    def out_shapes(scatter, arrs):
        return [jax.ShapeDtypeStruct(a.shape if scatter else (N_CHIPS,) + a.shape, a.dtype) for a in arrs]

    @staticmethod
    def semaphores(n):
        return [pltpu.SemaphoreType.DMA((3 * n,)), pltpu.SemaphoreType.DMA((3 * n,)), pltpu.SemaphoreType.DMA((n,))]


def _exchange(scatter, arrs, name):
    n = len(arrs)

    def body(*refs):
        ex = _ChipExchange(scatter, refs[:n], refs[n:2 * n], *refs[2 * n:])
        ex.start()
        ex.finish()

    return pl.pallas_call(
        body, name=name, in_specs=_any_specs(n), out_specs=_any_specs(n),
        out_shape=_ChipExchange.out_shapes(scatter, arrs), scratch_shapes=_ChipExchange.semaphores(n),
    )(*arrs)


def _riding_exchange(scatter, arrs):
    n = len(arrs)

    def hook(ins, outs, sems):
        i = pl.program_id(0)

        @pl.when(i == 0)
        def _():
            _ChipExchange(scatter, ins, outs, *sems).start()

        @pl.when(i == pl.num_programs(0) - 1)
        def _():
            _ChipExchange(scatter, ins, outs, *sems).finish()

    return (tuple(arrs), _any_specs(n), _any_specs(n), _ChipExchange.out_shapes(scatter, arrs),
            _ChipExchange.semaphores(n), hook)


def _sum_chips(parts, rb, name):
    depth = len(parts)
    _, r, cdim = parts[0].shape

    def body(*refs):
        o_ref = refs[depth]
        for l in range(depth):
            def add(p_ref=refs[l]):
                part = lambda j: p_ref[j].astype(F32)
                o_ref[...] = ((part(0) + part(1)) + part(2)) + part(3)

            pl.when(pl.program_id(0) == l)(add)

    return pl.pallas_call(
        body, name=name, grid=(depth, r // rb),
        in_specs=[pl.BlockSpec((N_CHIPS, rb, cdim), lambda l, i: (0, i, 0))] * depth,
        out_specs=pl.BlockSpec((None, rb, cdim), lambda l, i: (l, i, 0)),
        out_shape=jax.ShapeDtypeStruct((depth, r, cdim), F32),
        compiler_params=_arb(2),
    )(*parts)


def _swap_with_sibling(a, b):
    arrs = (a, b)
    n = len(arrs)

    def body(*refs):
        ins, outs = refs[:n], refs[n:2 * n]
        send_sems, recv_sems = refs[2 * n:]
        x, y, c, _ = _place()
        cps = [pltpu.make_async_remote_copy(src_ref=ins[k], dst_ref=outs[k], send_sem=send_sems.at[k],
                                            recv_sem=recv_sems.at[k], device_id=(x, y, 1 - c), device_id_type=MESH)
               for k in range(n)]
        for cp in cps:
            cp.start()
        for cp in cps:
            cp.wait()

    return pl.pallas_call(
        body, name="swap_with_sibling",
        in_specs=_any_specs(n), out_specs=_any_specs(n),
        out_shape=[jax.ShapeDtypeStruct(v.shape, v.dtype) for v in arrs],
        scratch_shapes=[pltpu.SemaphoreType.DMA((n,)), pltpu.SemaphoreType.DMA((n,))],
    )(*arrs)


def _allreduce_small(slab):
    m, n = slab.shape

    def body(x_ref, out_ref, gath, send_sems, recv_sems, local_sem):
        x, y, c, chips = _place()
        me, sibling = (x, y, c), (x, y, 1 - c)

        def rows(px, py, pc):
            return gath.at[pl.ds(pl.multiple_of((4 * px + 2 * py + pc) * m, 8), m), :]

        def copy(k, block, to, src=None):
            return pltpu.make_async_remote_copy(src_ref=rows(*block) if src is None else src, dst_ref=rows(*block),
                                                send_sem=send_sems.at[k], recv_sem=recv_sems.at[k],
                                                device_id=to, device_id_type=MESH)

        mine = pltpu.make_async_copy(x_ref, rows(*me), local_sem)
        mine.start()
        first = [copy(0, me, sibling, src=x_ref)]
        first += [copy(1 + j, me, (*chip, c), src=x_ref) for j, chip in enumerate(chips)]
        for cp in first:
            cp.start()
        passed = [copy(4 + j, (*chip, c), sibling) for j, chip in enumerate(chips)]
        for j, chip in enumerate(chips):
            copy(1 + j, (*chip, c), me).wait_recv()
            passed[j].start()
        copy(0, sibling, me).wait_recv()
        for j, chip in enumerate(chips):
            copy(4 + j, (*chip, 1 - c), me).wait_recv()
        for cp in first + passed:
            cp.wait_send()
        mine.wait()
        total = gath[0:m, :]
        for d in range(1, N_DEV):
            total = total + gath[d * m:(d + 1) * m, :]
        out_ref[...] = total

    return pl.pallas_call(
        body, name="allreduce_small",
        in_specs=[pl.BlockSpec(memory_space=pltpu.VMEM)],
        out_specs=pl.BlockSpec(memory_space=pltpu.VMEM),
        out_shape=jax.ShapeDtypeStruct((m, n), F32),
        scratch_shapes=[pltpu.VMEM((N_DEV * m, n), F32), pltpu.SemaphoreType.DMA((7,)), pltpu.SemaphoreType.DMA((7,)),
                        pltpu.SemaphoreType.DMA],
    )(slab)


def _adamw(w, ga, gb, m, v, rb, name):
    depth, r, cdim = w.shape

    def body(w_ref, ga_ref, gb_ref, m_ref, v_ref, g_out, d_out, m_out, v_out):
        g = ga_ref[...] + gb_ref[...]
        m2 = ADAM_B1 * m_ref[...] + (1.0 - ADAM_B1) * g
        v2 = ADAM_B2 * v_ref[...] + (1.0 - ADAM_B2) * (g * g)
        m_hat = m2 / (1.0 - ADAM_B1 ** ADAM_STEP)
        v_hat = v2 / (1.0 - ADAM_B2 ** ADAM_STEP)
        g_out[...] = g
        d_out[...] = -ADAM_LR * (m_hat / (jnp.sqrt(v_hat) + ADAM_EPS) + ADAM_WD * w_ref[...])
        m_out[...] = m2
        v_out[...] = v2

    spec = pl.BlockSpec((None, rb, cdim), lambda l, i: (l, i, 0))
    shp = jax.ShapeDtypeStruct((depth, r, cdim), F32)
    return pl.pallas_call(
        body, name=name, grid=(depth, r // rb),
        in_specs=[spec] * 5, out_specs=[spec] * 4, out_shape=[shp] * 4,
        compiler_params=_arb(2),
    )(w, ga, gb, m, v)


def _rope_tables(s):
    t = jnp.arange(s, dtype=jnp.int32)
    row = (t // GRID_W).astype(F32)
    col = (t % GRID_W).astype(F32)
    half = HEAD_DIM // 4
    inv_freq = ROPE_THETA ** (-jnp.arange(half, dtype=F32) / half)
    ar = row[:, None] * inv_freq[None, :]
    ac = col[:, None] * inv_freq[None, :]
    cos = jnp.concatenate([jnp.cos(ar), jnp.cos(ar), jnp.cos(ac), jnp.cos(ac)], axis=1)
    sin = jnp.concatenate([-jnp.sin(ar), jnp.sin(ar), -jnp.sin(ac), jnp.sin(ac)], axis=1)
    return jnp.tile(cos, (1, 2)), jnp.tile(sin, (1, 2))


def _pad_rows(a, rows):
    return jnp.pad(a, ((0, rows - a.shape[0]),) + ((0, 0),) * (a.ndim - 1))


_SMALL = ("pre_norm", "post_norm", "q_norm", "k_norm", "conv_dw_b", "conv_ln_g", "conv_ln_b", "sg_ln_g", "sg_ln_b",
          "sg_w", "sg_b")


def _pack(parts):
    flat = jnp.concatenate([p.reshape(-1, LANES) for p in parts], axis=0)
    return _pad_rows(flat, -(-flat.shape[0] // 8) * 8)


def _unpack(slab, shapes):
    out, r = [], 0
    for shp in shapes:
        n = 1
        for d in shp:
            n *= d
        out.append(slab[r:r + n // LANES].reshape(shp))
        r += n // LANES
    return out


def kernel(x, pre_norm, post_norm, w_in, w_out, q_norm, k_norm, conv_dw, conv_dw_b, conv_ln_g, conv_ln_b, sg_ln_g, sg_ln_b, sg_w, sg_b, loss_target, m_pre_norm, m_post_norm, m_w_in, m_w_out, m_q_norm, m_k_norm, m_conv_dw, m_conv_dw_b, m_conv_ln_g, m_conv_ln_b, m_sg_ln_g, m_sg_ln_b, m_sg_w, m_sg_b, v_pre_norm, v_post_norm, v_w_in, v_w_out, v_q_norm, v_k_norm, v_conv_dw, v_conv_dw_b, v_conv_ln_g, v_conv_ln_b, v_sg_ln_g, v_sg_ln_b, v_sg_w, v_sg_b):
    depth = w_in.shape[0]
    s = x.shape[1]
    assert x.shape[0] == 1 and s % SG_CHUNK == 0 and x.shape[2] == D_MODEL
    tq = min(256, s)
    tk = min(512, s // 2)
    t = min(256, tk)
    shard_cols = w_in.shape[2]
    chip = 2 * lax.axis_index("x") + lax.axis_index("y")

    w_in_sh, w_out_sh = w_in.astype(BF16), w_out.astype(BF16)
    whole_w_in = lambda g: jnp.concatenate([g[j] for j in range(N_CHIPS)], axis=2)
    w_in_bf = [whole_w_in(_exchange(False, (w_in_sh[0:1],), "gather_w_in_first")[0])[0]]
    riders = ((w_in_sh[1:],) if depth > 1 else ()) + (w_out_sh, conv_dw)
    sgw_bf = sg_w.astype(BF16)
    sgwt_bf = jnp.swapaxes(sg_w, 2, 3).astype(BF16)

    cos, sin = _rope_tables(s)
    bd = jnp.kron(jnp.eye(N_HEADS, dtype=F32), jnp.full((HEAD_DIM, HEAD_DIM), 1.0 / HEAD_DIM, F32)).astype(BF16)

    def head_gains(l):
        return jnp.tile(q_norm[l], N_HEADS)[None, :], jnp.tile(k_norm[l], N_KV)[None, :]

    def layer_consts(l):
        cw = _pad_rows(cdw_full[l], 32)
        cp = _pad_rows(jnp.stack([conv_dw_b[l], conv_ln_g[l], conv_ln_b[l]]), 8)
        sp = _pad_rows(jnp.stack([sg_ln_g[l], sg_ln_b[l]]), 8)
        sgb = jnp.repeat(sg_b[l].T, HEAD_DIM, axis=1)
        return cw, cp, sp, sgb

    xs = [x[0]]
    saved = []
    for l in range(depth):
        qg, kg = head_gains(l)
        outs = _fwd_in(xs[l], pre_norm[l][None, :], w_in_bf[l], bd, qg, kg, cos, sin, t, gather=riders if l == 0 else ())
        proj, hc, q, k, v = outs[:5]
        if l == 0:
            gathered = list(outs[5:])
            if depth > 1:
                w_in_bf += list(whole_w_in(gathered.pop(0)))
            w_out_bf = jnp.concatenate([gathered[0][j] for j in range(N_CHIPS)], axis=1)
            cdw_full = jnp.concatenate([gathered[1][j] for j in range(N_CHIPS)], axis=2)
            w_out_t_bf = jnp.swapaxes(w_out_bf, 1, 2)
        cw, cp, sp, sgb = layer_consts(l)
        o, qa = _flash_fwd(q, k, v, tq, tk)
        outs = _fwd_out(xs[l], proj, o, hc, cw, cp, sp, sgw_bf[l], sgb, w_out_bf[l], post_norm[l][None, :], t,
                        target=loss_target[0] if l == depth - 1 else None)
        saved.append((proj, hc, qa, k, v, o, outs[0], outs[2]))
        if l < depth - 1:
            xs.append(outs[1])
    dy, sq = outs[1], outs[3]
    loss = lax.psum(0.5 * jnp.sum(sq) / D_MODEL, ("x", "y", "c"))

    g_small = {n: [] for n in _SMALL + ("conv_dw",)}
    received = [None] * depth
    pieces = ()
    for l in reversed(range(depth)):
        cw, cp, sp, sgb = layer_consts(l)
        qg, kg = head_gains(l)
        proj, hc, qa, k, v, o, mix, c0 = saved[l]
        outs = _bwd_out(dy, mix, proj, o, hc, c0, cw, cp, sp, sgw_bf[l], sgwt_bf[l], sgb, w_out_t_bf[l],
                        post_norm[l][None, :], t, scatter=pieces)
        doa, dgs, dc0, gwo, gpost, gcw, gvec, gsgw, gsgb = outs[:9]
        if pieces:
            received[l + 1] = outs[9:]
        dq, dk, dv = _flash_bwd(qa, doa, k, v, tq, tk)
        dy, dproj_bf, h_bf, gpre, gq, gk = _bwd_in(dy, xs[l], proj, dq, dk, dv, dgs, dc0, cw, pre_norm[l][None, :],
                                                  jnp.swapaxes(w_in_bf[l], 0, 1), bd, qg, kg, cos, sin, t)
        gwi = _grad_w_in(h_bf, dproj_bf, min(512, s))
        pieces = (jnp.stack([gwi[:, j * shard_cols:(j + 1) * shard_cols] for j in range(N_CHIPS)]).astype(BF16),
                  gwo.reshape(N_CHIPS, gwo.shape[0] // N_CHIPS, gwo.shape[1]).astype(BF16))
        g_small["pre_norm"].append(gpre[0])
        g_small["post_norm"].append(gpost[0])
        g_small["q_norm"].append(gq[0].reshape(N_HEADS, HEAD_DIM).sum(0))
        g_small["k_norm"].append(gk[0].reshape(N_KV, HEAD_DIM).sum(0))
        g_small["conv_dw"].append(gcw[:CONV_K])
        g_small["conv_dw_b"].append(gvec[0])
        g_small["conv_ln_g"].append(gvec[1])
        g_small["conv_ln_b"].append(gvec[2])
        g_small["sg_ln_g"].append(gvec[3])
        g_small["sg_ln_b"].append(gvec[4])
        g_small["sg_w"].append(gsgw)
        g_small["sg_b"].append(gsgb.reshape(SG_CHUNK, SG_W // HEAD_DIM, HEAD_DIM).sum(-1).T)
    grad_x = dy[None]
    g_small = {n: jnp.stack(vals[::-1]) for n, vals in g_small.items()}

    received[0] = _exchange(True, pieces, "scatter_grads")
    s_in = _sum_chips([r[0] for r in received], 256, "sum_chips_w_in")
    s_out = _sum_chips([r[1] for r in received], 256, "sum_chips_w_out")
    t_in, t_out = _swap_with_sibling(s_in, s_out)
    grad_w_in, delta_w_in, new_m_w_in, new_v_w_in = _adamw(w_in, s_in, t_in, m_w_in, v_w_in, 256, "adamw_w_in")
    grad_w_out, delta_w_out, new_m_w_out, new_v_w_out = _adamw(w_out, s_out, t_out, m_w_out, v_w_out, 256, "adamw_w_out")

    small_w = dict(pre_norm=pre_norm, post_norm=post_norm, q_norm=q_norm, k_norm=k_norm, conv_dw_b=conv_dw_b,
                   conv_ln_g=conv_ln_g, conv_ln_b=conv_ln_b, sg_ln_g=sg_ln_g, sg_ln_b=sg_ln_b, sg_w=sg_w, sg_b=sg_b)
    small_m = dict(pre_norm=m_pre_norm, post_norm=m_post_norm, q_norm=m_q_norm, k_norm=m_k_norm, conv_dw_b=m_conv_dw_b,
                   conv_ln_g=m_conv_ln_g, conv_ln_b=m_conv_ln_b, sg_ln_g=m_sg_ln_g, sg_ln_b=m_sg_ln_b, sg_w=m_sg_w,
                   sg_b=m_sg_b)
    small_v = dict(pre_norm=v_pre_norm, post_norm=v_post_norm, q_norm=v_q_norm, k_norm=v_k_norm, conv_dw_b=v_conv_dw_b,
                   conv_ln_g=v_conv_ln_g, conv_ln_b=v_conv_ln_b, sg_ln_g=v_sg_ln_g, sg_ln_b=v_sg_ln_b, sg_w=v_sg_w,
                   sg_b=v_sg_b)
    shapes = [small_w[n].shape for n in _SMALL]
    red = _allreduce_small(_pack([g_small[n] for n in _SMALL] + [g_small["conv_dw"]]))
    n_rep = sum(small_w[n].size for n in _SMALL) // LANES
    g_cdw_full = red[n_rep:n_rep + g_small["conv_dw"].size // LANES].reshape(g_small["conv_dw"].shape)
    cdw_cols = conv_dw.shape[2]
    g_cdw = lax.dynamic_slice_in_dim(g_cdw_full, chip * cdw_cols, cdw_cols, axis=2)
    g_slab = _pack([red[:n_rep], g_cdw])
    w_slab = _pack([small_w[n] for n in _SMALL] + [conv_dw])
    m_slab = _pack([small_m[n] for n in _SMALL] + [m_conv_dw])
    v_slab = _pack([small_v[n] for n in _SMALL] + [v_conv_dw])
    rows = w_slab.shape[0]
    outs = _adamw(w_slab[None], g_slab[None], jnp.zeros_like(g_slab)[None], m_slab[None], v_slab[None], rows, "adamw_small")
    unpacked = [dict(zip(_SMALL + ("conv_dw",), _unpack(o_[0], shapes + [conv_dw.shape]))) for o_ in outs]

    big = [dict(w_in=a, w_out=b) for a, b in ((grad_w_in, grad_w_out), (delta_w_in, delta_w_out),
                                             (new_m_w_in, new_m_w_out), (new_v_w_in, new_v_w_out))]
    order = ("pre_norm", "post_norm", "w_in", "w_out", "q_norm", "k_norm", "conv_dw", "conv_dw_b", "conv_ln_g",
             "conv_ln_b", "sg_ln_g", "sg_ln_b", "sg_w", "sg_b")
    result = [loss, grad_x]
    for kind in range(4):
        for name in order:
            result.append(big[kind][name] if name in big[kind] else unpacked[kind][name])
    return tuple(result)
```

```python
import functools

import jax
import jax.numpy as jnp
from jax import lax
from jax.experimental import pallas as pl
from jax.experimental.pallas import tpu as pltpu

F32 = jnp.float32
BF16 = jnp.bfloat16
MESH = pl.DeviceIdType.MESH

EPS = 1e-6
D_MODEL = 1024
D_IN = 2816
HEAD_DIM = 64
LANES = 128
SUBLANES = 8
N_HEADS = 8
N_KV = 2
GROUP = N_HEADS // N_KV
GRID_W = 64
ROPE_THETA = 10000.0
CONV_K = 31
CONV_W = 256
SG_W = 256
SG_CHUNK = 128
HALO = 16
ATT_SCALE = HEAD_DIM ** -0.5

C_Q, C_K, C_V, C_GA, C_A1, C_A2, C_GC, C_U, C_VS, C_GS = 0, 512, 640, 768, 1280, 1536, 1792, 2048, 2304, 2560

ADAM_LR = 0.001
ADAM_B1 = 0.9
ADAM_B2 = 0.999
ADAM_EPS = 1e-08
ADAM_WD = 0.01
ADAM_STEP = 10

N_CHIPS = 4
N_DEV = 8


def _dot(a, b):
    return jnp.dot(a, b, preferred_element_type=F32)


def _group_mean(x, bd_bf):
    hi = x.astype(BF16)
    lo = (x - hi.astype(F32)).astype(BF16)
    return _dot(hi, bd_bf) + _dot(lo, bd_bf)


def _dot_nt(a, b):
    return lax.dot_general(a, b, (((1,), (1,)), ((), ())), preferred_element_type=F32)


def _dot_tn(a, b):
    return lax.dot_general(a, b, (((0,), (0,)), ((), ())), preferred_element_type=F32)


def _lane(shape):
    return lax.broadcasted_iota(jnp.int32, shape, 1)


def _sigmoid(x):
    return 1.0 / (1.0 + jnp.exp(-x))


def _silu_fwd_bwd(x):
    s = _sigmoid(x)
    return x * s, s * (1.0 + x * (1.0 - s))


def _erf(x):
    x = jnp.clip(x, -4.0, 4.0)
    x2 = x * x
    a = -2.72614225801306e-10
    a = a * x2 + 2.77068142495902e-08
    a = a * x2 + -2.10102402082508e-06
    a = a * x2 + -5.69250639462346e-05
    a = a * x2 + -7.34990630326855e-04
    a = a * x2 + -2.95459980854025e-03
    a = a * x2 + -1.60960333262415e-02
    b = -1.45660718464996e-05
    b = b * x2 + -2.13374055278905e-04
    b = b * x2 + -1.68282697438203e-03
    b = b * x2 + -7.37332916720468e-03
    b = b * x2 + -1.42647390514189e-02
    return x * a / b


def _gelu_fwd_bwd(x):
    cdf = 0.5 * (1.0 + _erf(x * 0.7071067811865476))
    pdf = jnp.exp(-0.5 * x * x) * 0.3989422804014327
    return x * cdf, cdf + x * pdf


def _rms(x):
    return lax.rsqrt(jnp.mean(x * x, axis=-1, keepdims=True) + EPS)


def _ln_hat(x):
    mu = jnp.mean(x, axis=-1, keepdims=True)
    xc = x - mu
    rs = lax.rsqrt(jnp.mean(xc * xc, axis=-1, keepdims=True) + EPS)
    return xc * rs, rs


def _ln_bwd(dxh, xh, rs):
    return rs * (dxh - jnp.mean(dxh, axis=-1, keepdims=True) - xh * jnp.mean(dxh * xh, axis=-1, keepdims=True))


def _swap16(z):
    parts = []
    for i in range(z.shape[1] // LANES):
        blk = z[:, i * LANES:(i + 1) * LANES]
        lane = _lane(blk.shape)
        parts.append(jnp.where((lane & 16) == 0, pltpu.roll(blk, LANES - 16, 1), pltpu.roll(blk, 16, 1)))
    return parts[0] if len(parts) == 1 else jnp.concatenate(parts, axis=1)


def _head_slab(pair, odd):
    src = pltpu.roll(pair, HEAD_DIM, 1) if odd else pair
    return jnp.where(_lane(pair.shape) < HEAD_DIM, src, 0.0)


def _pair_merge(even, odd):
    return jnp.where(_lane(even.shape) < HEAD_DIM, even, pltpu.roll(odd, HEAD_DIM, 1))


def _heads_to_cat(ref, n_heads):
    pairs = [_pair_merge(ref[2 * p], ref[2 * p + 1]) for p in range(n_heads // 2)]
    return pairs[0] if len(pairs) == 1 else jnp.concatenate(pairs, axis=1)


def _split3(x):
    hi = x.astype(BF16).astype(F32)
    r = x - hi
    mid = r.astype(BF16).astype(F32)
    lo = (r - mid).astype(BF16).astype(F32)
    return hi, mid, lo


def _with_spare(slab, hi, mid, lo):
    lane = _lane(slab.shape)
    return jnp.where(lane == HEAD_DIM, hi, jnp.where(lane == HEAD_DIM + 1, mid, jnp.where(lane == HEAD_DIM + 2, lo, slab)))


def _with_ones(slab):
    lane = _lane(slab.shape)
    return jnp.where((lane >= HEAD_DIM) & (lane < HEAD_DIM + 3), 1.0, slab)


def _conv_window(rot_ref, prev_ref, main, next_ref, first, last, t):
    n = t + 2 * HALO
    full = jnp.concatenate([jnp.where(first, 0.0, prev_ref[...]), main, jnp.where(last, 0.0, next_ref[...])], axis=0)
    rot_ref[0] = full
    for b in range(1, SUBLANES):
        rot_ref[b] = pltpu.roll(full, n - b, 0)


def _tap(rot_ref, start, t):
    a, b = divmod(start, SUBLANES)
    return rot_ref[b, SUBLANES * a:SUBLANES * a + t, :]


def _sgu_mix(v1_bf, w_ref, n_chunks):
    rows = []
    for n in range(n_chunks):
        pairs = []
        for p in range(SG_W // LANES):
            xp = v1_bf[n * SG_CHUNK:(n + 1) * SG_CHUNK, p * LANES:(p + 1) * LANES]
            me = _dot(w_ref[2 * p], xp)
            mo = _dot(w_ref[2 * p + 1], xp)
            pairs.append(jnp.where(_lane(me.shape) < HEAD_DIM, me, mo))
        rows.append(jnp.concatenate(pairs, axis=1))
    return rows[0] if len(rows) == 1 else jnp.concatenate(rows, axis=0)


def _halo_specs(t, s, width):
    per = t // HALO
    nblk = s // HALO
    prev = pl.BlockSpec((HALO, width), lambda i: (jnp.maximum(i * per - 1, 0), 0))
    nxt = pl.BlockSpec((HALO, width), lambda i: (jnp.minimum((i + 1) * per, nblk - 1), 0))
    return prev, nxt


def _const_spec(shape):
    nd = len(shape)
    return pl.BlockSpec(shape, lambda i: (0,) * nd)


def _arb(n=1):
    return pltpu.CompilerParams(dimension_semantics=("arbitrary",) * n)


def _fwd_in(x, g_pre, w_in_bf, bd, qg, kg, cos, sin, t, gather=()):
    s = x.shape[0]
    ex_args, ex_in, ex_out, ex_shapes, ex_sems, ex_hook = _riding_exchange(False, gather)
    n_ex = len(gather)

    def body(*refs):
        x_ref, g_ref, w_ref, bd_ref, qg_ref, kg_ref, cos_ref, sin_ref = refs[:8]
        proj_ref, hc_ref, q_ref, k_ref, v_ref = refs[8 + n_ex:13 + n_ex]
        if n_ex:
            ex_hook(refs[8:8 + n_ex], refs[13 + n_ex:13 + 2 * n_ex], refs[13 + 2 * n_ex:])
        xv = x_ref[...]
        h = (xv * _rms(xv) * g_ref[...]).astype(BF16)
        proj = _dot(h, w_ref[...])
        proj_ref[...] = proj
        cos_pair = cos_ref[...]
        sin_pair = sin_ref[...]
        cosv = jnp.concatenate([cos_pair] * (N_HEADS // 2), axis=1)
        sinv = jnp.concatenate([sin_pair] * (N_HEADS // 2), axis=1)
        q = proj[:, C_Q:C_K]
        qn = q * lax.rsqrt(_group_mean(q * q, bd_ref[...]) + EPS) * qg_ref[...]
        qr = (qn * cosv + _swap16(qn) * sinv) * ATT_SCALE
        for hh in range(N_HEADS):
            pair = qr[:, (hh // 2) * LANES:(hh // 2 + 1) * LANES]
            q_ref[hh] = _head_slab(pair, hh % 2 == 1).astype(BF16)
        k = proj[:, C_K:C_V]
        kn = k * lax.rsqrt(_group_mean(k * k, bd_ref[0:LANES, 0:LANES]) + EPS) * kg_ref[...]
        kr = kn * cos_pair + _swap16(kn) * sin_pair
        vv = proj[:, C_V:C_GA]
        for hh in range(N_KV):
            k_ref[hh] = _with_ones(_head_slab(kr, hh == 1)).astype(BF16)
            v_ref[hh] = _with_ones(_head_slab(vv, hh == 1)).astype(BF16)
        hc_ref[...] = proj[:, C_A1:C_A2] * _sigmoid(proj[:, C_A2:C_GC])

    row = lambda w: pl.BlockSpec((t, w), lambda i: (i, 0))
    heads = lambda n: pl.BlockSpec((n, t, LANES), lambda i: (0, i, 0))
    return pl.pallas_call(
        body, name="fwd_in_gather" if n_ex else "fwd_in", grid=(s // t,),
        in_specs=[row(D_MODEL), _const_spec((1, D_MODEL)), _const_spec((D_MODEL, D_IN)), _const_spec((512, 512)),
                  _const_spec((1, 512)), _const_spec((1, LANES)), row(LANES), row(LANES)] + ex_in,
        out_specs=[row(D_IN), row(CONV_W), heads(N_HEADS), heads(N_KV), heads(N_KV)] + ex_out,
        out_shape=[jax.ShapeDtypeStruct((s, D_IN), F32), jax.ShapeDtypeStruct((s, CONV_W), F32),
                   jax.ShapeDtypeStruct((N_HEADS, s, LANES), BF16), jax.ShapeDtypeStruct((N_KV, s, LANES), BF16),
                   jax.ShapeDtypeStruct((N_KV, s, LANES), BF16)] + ex_shapes,
        scratch_shapes=ex_sems if n_ex else [],
        compiler_params=_arb(),
    )(x, g_pre, w_in_bf, bd, qg, kg, cos, sin, *ex_args)


def _chunk_rows(c, tk):
    return pl.ds(c * tk, tk) if isinstance(c, int) else pl.ds(pl.multiple_of(c * tk, tk), tk)


def _three_stage_pipeline(nk, per_trip, stage1, stage2, stage3, peel):
    assert nk % 2 == 0 and per_trip % 2 == 0

    def step(t, parity, first=False, last=False):
        if not last:
            stage1(t + 1, 1 - parity)
        stage2(parity)
        if not first:
            stage3(t - 1, 1 - parity)

    stage1(0, 0)
    if not peel:
        while nk % per_trip:
            per_trip //= 2

        def whole_trip(i, carry):
            for u in range(per_trip):
                c = per_trip * i + u
                stage1(jnp.minimum(c + 1, nk - 1), 1 - u % 2)
                stage2(u % 2)
                stage3(jnp.maximum(c - 1, 0), 1 - u % 2)
            return carry

        lax.fori_loop(0, nk // per_trip, whole_trip, 0)
        stage3(nk - 1, 1)
        return

    step(0, 0, first=True)
    n_trips, left = divmod(nk - 2, per_trip)

    def trip(i, carry):
        for u in range(per_trip):
            step(1 + per_trip * i + u, (1 + u) % 2)
        return carry

    if n_trips:
        lax.fori_loop(0, n_trips, trip, 0)
    for t in range(1 + n_trips * per_trip, 1 + n_trips * per_trip + left):
        step(t, t % 2)
    step(nk - 1, 1, last=True)
    stage3(nk - 1, 1)


def _flash_fwd(q, k, v, tq, tk):
    s = q.shape[1]
    rows = GROUP * tq
    nk = s // tk

    def body(q_ref, k_ref, vt_ref, o_ref, qa_ref, m_scr, acc_scr, s0, s1, p0, p1, a0, a1):
        s_bufs, p_bufs, a_bufs = (s0, s1), (p0, p1), (a0, a1)
        qv = q_ref[...].reshape(rows, LANES)
        q_t = qv.astype(F32).T
        q_t_bf = q_t.astype(BF16)
        m_scr[...] = jnp.full((1, rows), -jnp.inf, F32)
        acc_scr[...] = jnp.zeros((LANES, rows), F32)

        def scores(c, slot):
            s_bufs[slot][...] = _dot(k_ref[_chunk_rows(c, tk), :], q_t_bf)

        def softmax(slot):
            for h in range(GROUP):
                r = slice(h * tq, (h + 1) * tq)
                sc = s_bufs[slot][:, r]
                m_prev = m_scr[:, r]
                m_new = jnp.maximum(m_prev, jnp.max(sc, axis=0, keepdims=True))
                p_bufs[slot][:, r] = jnp.exp((sc - m_new).astype(BF16))
                a_bufs[slot][:, r] = jnp.exp(m_prev - m_new)
                m_scr[:, r] = m_new

        def weighted_values(c, slot):
            acc_scr[...] = a_bufs[slot][...] * acc_scr[...] + _dot(vt_ref[c], p_bufs[slot][...])

        _three_stage_pipeline(nk, 4, scores, softmax, weighted_values, peel=True)

        acc = acc_scr[...]
        row = lax.broadcasted_iota(jnp.int32, acc.shape, 0)
        l = jnp.sum(jnp.where(row == HEAD_DIM, acc, 0.0), axis=0, keepdims=True)
        o_ref[...] = jnp.where(row < HEAD_DIM, acc / l, 0.0).T.reshape(GROUP, tq, LANES)
        hi, mid, lo = _split3(-(m_scr[...] + jnp.log(l)))
        qa_t = jnp.where(row == HEAD_DIM, hi, jnp.where(row == HEAD_DIM + 1, mid,
                                                        jnp.where(row == HEAD_DIM + 2, lo, q_t)))
        qa_ref[...] = qa_t.T.astype(BF16).reshape(GROUP, tq, LANES)

    qspec = pl.BlockSpec((GROUP, tq, LANES), lambda j, i: (j, i, 0))
    kspec = pl.BlockSpec((None, s, LANES), lambda j, i: (j, 0, 0))
    vtspec = pl.BlockSpec((None, nk, LANES, tk), lambda j, i: (j, 0, 0, 0))
    v_t = jnp.swapaxes(v.reshape(N_KV, nk, tk, LANES), 2, 3)
    return pl.pallas_call(
        body, name="flash_fwd", grid=(N_KV, s // tq),
        in_specs=[qspec, kspec, vtspec],
        out_specs=[qspec, qspec],
        out_shape=[jax.ShapeDtypeStruct((N_HEADS, s, LANES), F32), jax.ShapeDtypeStruct((N_HEADS, s, LANES), BF16)],
        scratch_shapes=[pltpu.VMEM((1, rows), F32), pltpu.VMEM((LANES, rows), F32),
                        pltpu.VMEM((tk, rows), F32), pltpu.VMEM((tk, rows), F32),
                        pltpu.VMEM((tk, rows), BF16), pltpu.VMEM((tk, rows), BF16),
                        pltpu.VMEM((1, rows), F32), pltpu.VMEM((1, rows), F32)],
        compiler_params=_arb(2),
    )(q, k, v_t)


def _groups_fwd(proj_ref, o_ref, hext_ref, cw_ref, cp_ref, sp_ref, sgw_ref, sgb_ref, t, c0=None):
    proj = proj_ref[...]
    r = {}
    r["att"] = _heads_to_cat(o_ref, N_HEADS)
    r["gate_a"], r["dgate_a"] = _silu_fwd_bwd(proj[:, C_GA:C_A1])
    r["att_g"] = r["att"] * r["gate_a"]
    if c0 is None:
        c0 = jnp.zeros((t, CONV_W), F32) + cp_ref[0:1, :]
        for kk in range(CONV_K):
            c0 = c0 + cw_ref[kk:kk + 1, :] * _tap(hext_ref, kk + 1, t)
    r["c0"] = c0
    r["xh_c"], r["rs_c"] = _ln_hat(c0)
    r["c1"] = r["xh_c"] * cp_ref[1:2, :] + cp_ref[2:3, :]
    r["sg_c1"] = _sigmoid(r["c1"])
    r["c2"] = r["c1"] * r["sg_c1"]
    r["gate_c"], r["dgate_c"] = _silu_fwd_bwd(proj[:, C_GC:C_U])
    r["cnv_g"] = r["c2"] * r["gate_c"]
    r["gu"], r["dgu"] = _gelu_fwd_bwd(proj[:, C_U:C_VS])
    gv, r["dgv"] = _gelu_fwd_bwd(proj[:, C_VS:C_GS])
    r["xh_s"], r["rs_s"] = _ln_hat(gv)
    v1 = r["xh_s"] * sp_ref[0:1, :] + sp_ref[1:2, :]
    r["v1_bf"] = v1.astype(BF16)
    r["mixed"] = _sgu_mix(r["v1_bf"], sgw_ref, t // SG_CHUNK) + jnp.concatenate([sgb_ref[...]] * (t // SG_CHUNK), axis=0)
    r["um"] = r["gu"] * r["mixed"]
    r["gate_s"], r["dgate_s"] = _silu_fwd_bwd(proj[:, C_GS:D_IN])
    r["sgu_g"] = r["um"] * r["gate_s"]
    r["mc_bf"] = jnp.concatenate([r["att_g"], r["cnv_g"], r["sgu_g"]], axis=1).astype(BF16)
    return r


def _fwd_out(x, proj, o, hc, cw, cp, sp, sgw_bf, sgb, w_out_bf, g_post, t, target=None):
    s = x.shape[0]
    last_layer = target is not None

    def body(*refs):
        (x_ref, proj_ref, o_ref, hc_ref, hp_ref, hn_ref, cw_ref, cp_ref, sp_ref, sgw_ref, sgb_ref,
         w_ref, g_ref) = refs[:13]
        rest = refs[13:]
        if last_layer:
            t_ref, mix_ref, out_ref, c0_ref, sq_ref, hext_ref = rest
        else:
            mix_ref, out_ref, c0_ref, hext_ref = rest
        i = pl.program_id(0)
        _conv_window(hext_ref, hp_ref, hc_ref[...], hn_ref, i == 0, i == pl.num_programs(0) - 1, t)
        r = _groups_fwd(proj_ref, o_ref, hext_ref, cw_ref, cp_ref, sp_ref, sgw_ref, sgb_ref, t)
        c0_ref[...] = r["c0"]
        mix = _dot(r["mc_bf"], w_ref[...])
        mix_ref[...] = mix
        y = x_ref[...] + mix * _rms(mix) * g_ref[...]
        if last_layer:
            @pl.when(i == 0)
            def _():
                sq_ref[...] = jnp.zeros_like(sq_ref)

            err = y - t_ref[...]
            out_ref[...] = err * (1.0 / D_MODEL)
            sq_ref[...] += jnp.sum(err * err, axis=0, keepdims=True)
        else:
            out_ref[...] = y

    row = lambda w: pl.BlockSpec((t, w), lambda i: (i, 0))
    hprev, hnext = _halo_specs(t, s, CONV_W)
    big = jax.ShapeDtypeStruct((s, D_MODEL), F32)
    return pl.pallas_call(
        body, name="fwd_out_loss" if last_layer else "fwd_out", grid=(s // t,),
        in_specs=[row(D_MODEL), row(D_IN), pl.BlockSpec((N_HEADS, t, LANES), lambda i: (0, i, 0)), row(CONV_W),
                  hprev, hnext, _const_spec((32, CONV_W)), _const_spec((8, CONV_W)), _const_spec((8, SG_W)),
                  _const_spec((4, SG_CHUNK, SG_CHUNK)), _const_spec((SG_CHUNK, SG_W)),
                  _const_spec((D_MODEL, D_MODEL)), _const_spec((1, D_MODEL))] + ([row(D_MODEL)] if last_layer else []),
        out_specs=[row(D_MODEL), row(D_MODEL), row(CONV_W)] + ([_const_spec((1, D_MODEL))] if last_layer else []),
        out_shape=[big, big, jax.ShapeDtypeStruct((s, CONV_W), F32)]
        + ([jax.ShapeDtypeStruct((1, D_MODEL), F32)] if last_layer else []),
        scratch_shapes=[pltpu.VMEM((SUBLANES, t + 2 * HALO, CONV_W), F32)],
        compiler_params=_arb(),
    )(*((x, proj, o, hc, hc, hc, cw, cp, sp, sgw_bf, sgb, w_out_bf, g_post) + ((target,) if last_layer else ())))


def _bwd_out(dy, mix, proj, o, hc, c0, cw, cp, sp, sgw_bf, sgwt_bf, sgb, w_out_t_bf, g_post, t, scatter=()):
    s = dy.shape[0]
    n_chunks = t // SG_CHUNK
    ex_args, ex_in, ex_out, ex_shapes, ex_sems, ex_hook = _riding_exchange(True, scatter)
    n_ex = len(scatter)

    def body(*refs):
        (dy_ref, mix_ref, proj_ref, o_ref, hc_ref, hp_ref, hn_ref, c0_ref, cw_ref, cp_ref, sp_ref, sgw_ref,
         sgwt_ref, sgb_ref, wt_ref, g_ref) = refs[:16]
        (do_ref, dgs_ref, dc0_ref, gwo_ref, gpost_ref, gcw_ref, gvec_ref, gsgw_ref,
         gsgb_ref) = refs[16 + n_ex:25 + n_ex]
        hext_ref = refs[25 + 2 * n_ex]
        if n_ex:
            ex_hook(refs[16:16 + n_ex], refs[25 + n_ex:25 + 2 * n_ex], refs[26 + 2 * n_ex:])
        i = pl.program_id(0)

        @pl.when(i == 0)
        def _():
            for ref in (gwo_ref, gpost_ref, gcw_ref, gvec_ref, gsgw_ref, gsgb_ref):
                ref[...] = jnp.zeros_like(ref)

        _conv_window(hext_ref, hp_ref, hc_ref[...], hn_ref, i == 0, i == pl.num_programs(0) - 1, t)
        r = _groups_fwd(proj_ref, o_ref, hext_ref, cw_ref, cp_ref, sp_ref, sgw_ref, sgb_ref, t, c0=c0_ref[...])

        dyv = dy_ref[...]
        mix_v = mix_ref[...]
        rr = _rms(mix_v)
        gd = dyv * g_ref[...]
        dmix = rr * gd - mix_v * (rr * rr * rr * jnp.mean(gd * mix_v, axis=-1, keepdims=True))
        gpost_ref[...] += jnp.sum(dyv * mix_v * rr, axis=0, keepdims=True)
        dmix_bf = dmix.astype(BF16)
        gwo_ref[...] += _dot_tn(r["mc_bf"], dmix_bf)
        dmc = _dot(dmix_bf, wt_ref[...])

        d_att = dmc[:, 0:512]
        dg_att = d_att * r["att"] * r["dgate_a"]
        d_o = d_att * r["gate_a"]
        prod = d_o * r["att"]
        for p in range(N_HEADS // 2):
            sl = slice(p * LANES, (p + 1) * LANES)
            pr = prod[:, sl]
            tot = jnp.sum(pr, axis=1, keepdims=True)
            ev = jnp.sum(jnp.where(_lane(pr.shape) < HEAD_DIM, pr, 0.0), axis=1, keepdims=True)
            for odd, delta in ((False, ev), (True, tot - ev)):
                hi, mid, lo = _split3(-delta)
                do_ref[2 * p + int(odd)] = _with_spare(_head_slab(d_o[:, sl], odd), hi, mid, lo).astype(BF16)

        dcv = dmc[:, 512:768]
        dg_conv = dcv * r["c2"] * r["dgate_c"]
        dc1 = dcv * r["gate_c"] * (r["sg_c1"] * (1.0 + r["c1"] * (1.0 - r["sg_c1"])))
        dc0 = _ln_bwd(dc1 * cp_ref[1:2, :], r["xh_c"], r["rs_c"])
        dc0_ref[...] = dc0
        for kk in range(CONV_K):
            gcw_ref[kk:kk + 1, :] += jnp.sum(dc0 * _tap(hext_ref, kk + 1, t), axis=0, keepdims=True)

        dsg = dmc[:, 768:1024]
        dg_sg = dsg * r["um"] * r["dgate_s"]
        du = dsg * r["mixed"] * r["gate_s"] * r["dgu"]
        dmx = dsg * r["gu"] * r["gate_s"]
        dmx_bf = dmx.astype(BF16)
        sgb_sum = dmx[0:SG_CHUNK, :]
        for n in range(1, n_chunks):
            sgb_sum = sgb_sum + dmx[n * SG_CHUNK:(n + 1) * SG_CHUNK, :]
        gsgb_ref[...] += sgb_sum
        dv1_rows = []
        for n in range(n_chunks):
            pairs = []
            for p in range(SG_W // LANES):
                rs_ = slice(n * SG_CHUNK, (n + 1) * SG_CHUNK)
                ls_ = slice(p * LANES, (p + 1) * LANES)
                dm = dmx_bf[rs_, ls_]
                xp = r["v1_bf"][rs_, ls_]
                low = _lane(dm.shape) < HEAD_DIM
                zero = jnp.zeros_like(dm)
                gsgw_ref[2 * p] += _dot_nt(jnp.where(low, dm, zero), xp)
                gsgw_ref[2 * p + 1] += _dot_nt(jnp.where(low, zero, dm), xp)
                pairs.append(jnp.where(low, _dot(sgwt_ref[2 * p], dm), _dot(sgwt_ref[2 * p + 1], dm)))
            dv1_rows.append(jnp.concatenate(pairs, axis=1))
        dv1 = dv1_rows[0] if n_chunks == 1 else jnp.concatenate(dv1_rows, axis=0)
        dvs = _ln_bwd(dv1 * sp_ref[0:1, :], r["xh_s"], r["rs_s"]) * r["dgv"]

        zrow = jnp.zeros((1, CONV_W), F32)
        gvec_ref[...] += jnp.concatenate([
            jnp.sum(dc0, axis=0, keepdims=True),
            jnp.sum(dc1 * r["xh_c"], axis=0, keepdims=True),
            jnp.sum(dc1, axis=0, keepdims=True),
            jnp.sum(dv1 * r["xh_s"], axis=0, keepdims=True),
            jnp.sum(dv1, axis=0, keepdims=True),
            zrow, zrow, zrow], axis=0)
        dgs_ref[...] = jnp.concatenate([dg_att, dg_conv, du, dvs, dg_sg], axis=1)

    row = lambda w: pl.BlockSpec((t, w), lambda i: (i, 0))
    heads = pl.BlockSpec((N_HEADS, t, LANES), lambda i: (0, i, 0))
    hprev, hnext = _halo_specs(t, s, CONV_W)
    return pl.pallas_call(
        body, name="bwd_out_scatter" if n_ex else "bwd_out", grid=(s // t,),
        in_specs=[row(D_MODEL), row(D_MODEL), row(D_IN), heads, row(CONV_W), hprev, hnext, row(CONV_W),
                  _const_spec((32, CONV_W)), _const_spec((8, CONV_W)), _const_spec((8, SG_W)),
                  _const_spec((4, SG_CHUNK, SG_CHUNK)), _const_spec((4, SG_CHUNK, SG_CHUNK)),
                  _const_spec((SG_CHUNK, SG_W)), _const_spec((D_MODEL, D_MODEL)), _const_spec((1, D_MODEL))] + ex_in,
        out_specs=[heads, row(1536), row(CONV_W), _const_spec((D_MODEL, D_MODEL)), _const_spec((1, D_MODEL)),
                   _const_spec((32, CONV_W)), _const_spec((8, CONV_W)), _const_spec((4, SG_CHUNK, SG_CHUNK)),
                   _const_spec((SG_CHUNK, SG_W))] + ex_out,
        out_shape=[jax.ShapeDtypeStruct((N_HEADS, s, LANES), BF16), jax.ShapeDtypeStruct((s, 1536), F32),
                   jax.ShapeDtypeStruct((s, CONV_W), F32), jax.ShapeDtypeStruct((D_MODEL, D_MODEL), F32),
                   jax.ShapeDtypeStruct((1, D_MODEL), F32), jax.ShapeDtypeStruct((32, CONV_W), F32),
                   jax.ShapeDtypeStruct((8, CONV_W), F32), jax.ShapeDtypeStruct((4, SG_CHUNK, SG_CHUNK), F32),
                   jax.ShapeDtypeStruct((SG_CHUNK, SG_W), F32)] + ex_shapes,
        scratch_shapes=[pltpu.VMEM((SUBLANES, t + 2 * HALO, CONV_W), F32)] + (ex_sems if n_ex else []),
        compiler_params=_arb(),
    )(dy, mix, proj, o, hc, hc, hc, c0, cw, cp, sp, sgw_bf, sgwt_bf, sgb, w_out_t_bf, g_post, *ex_args)


def _flash_bwd(qa, doa, k, v, tq, tk):
    s = qa.shape[1]
    rows = GROUP * tq
    nk = s // tk
    n_q = s // tq

    def body(qa_ref, do_ref, k_ref, v_ref, dq_ref, dk_hbm, dv_hbm,
             dq_scr, dk_scr, dv_scr, s0, s1, d0, d1, p0, p1, e0, e1, sems):
        j, i = pl.program_id(0), pl.program_id(1)
        s_bufs, d_bufs, p_bufs, e_bufs = (s0, s1), (d0, d1), (p0, p1), (e0, e1)
        qv = qa_ref[...].reshape(rows, LANES)
        dov = do_ref[...].reshape(rows, LANES)
        q_t = qv.astype(F32).T.astype(BF16)
        do_t = dov.astype(F32).T.astype(BF16)
        dq_scr[...] = jnp.zeros((rows, LANES), F32)

        @pl.when(i == 0)
        def _():
            dk_scr[...] = jnp.zeros_like(dk_scr)
            dv_scr[...] = jnp.zeros_like(dv_scr)

        def at(c):
            return _chunk_rows(c, tk)

        def scores(c, slot):
            s_bufs[slot][...] = _dot_nt(qv, k_ref[at(c), :])
            d_bufs[slot][...] = _dot_nt(dov, v_ref[at(c), :])

        def probs(slot):
            for h in range(GROUP):
                r = slice(h * tq, (h + 1) * tq)
                p = jnp.exp(s_bufs[slot][r, :])
                p_bufs[slot][r, :] = p.astype(BF16)
                e_bufs[slot][r, :] = (p * d_bufs[slot][r, :]).astype(BF16)

        def grads(c, slot):
            ds = e_bufs[slot][...]
            dq_scr[...] += _dot(ds, k_ref[at(c), :])
            dv_scr[c] += _dot(do_t, p_bufs[slot][...])
            dk_scr[c] += _dot(q_t, ds)

        p1[...] = jnp.zeros((rows, tk), BF16)
        e1[...] = jnp.zeros((rows, tk), BF16)
        _three_stage_pipeline(nk, 4, scores, probs, grads, peel=False)
        dq_ref[...] = dq_scr[...].reshape(GROUP, tq, LANES)

        @pl.when(i == n_q - 1)
        def _():
            out = [pltpu.make_async_copy(dk_scr, dk_hbm.at[j], sems.at[0]),
                   pltpu.make_async_copy(dv_scr, dv_hbm.at[j], sems.at[1])]
            for cp in out:
                cp.start()
            for cp in out:
                cp.wait()

    qspec = pl.BlockSpec((GROUP, tq, LANES), lambda j, i: (j, i, 0))
    kvspec = pl.BlockSpec((None, s, LANES), lambda j, i: (j, 0, 0))
    hbm = pl.BlockSpec(memory_space=pl.ANY)
    stage_f32 = pltpu.VMEM((rows, tk), F32)
    stage_bf = pltpu.VMEM((rows, tk), BF16)
    kv_t = jax.ShapeDtypeStruct((N_KV, nk, LANES, tk), F32)
    return pl.pallas_call(
        body, name="flash_bwd", grid=(N_KV, n_q),
        in_specs=[qspec, qspec, kvspec, kvspec],
        out_specs=[qspec, hbm, hbm],
        out_shape=[jax.ShapeDtypeStruct((N_HEADS, s, LANES), F32), kv_t, kv_t],
        scratch_shapes=[pltpu.VMEM((rows, LANES), F32), pltpu.VMEM((nk, LANES, tk), F32), pltpu.VMEM((nk, LANES, tk), F32),
                        stage_f32, stage_f32, stage_f32, stage_f32, stage_bf, stage_bf, stage_bf, stage_bf,
                        pltpu.SemaphoreType.DMA((2,))],
        compiler_params=_arb(2),
    )(qa, doa, k, v)


def _bwd_in(dy, x, proj, dq, dk, dv, dgs, dc0, cw, g_pre, w_in_t_bf, bd, qg, kg, cos, sin, t):
    s = x.shape[0]

    def body(dy_ref, x_ref, proj_ref, dq_ref, dk_ref, dv_ref, dgs_ref, dc_ref, dcp_ref, dcn_ref, cw_ref, g_ref,
             wt_ref, bd_ref, qg_ref, kg_ref, cos_ref, sin_ref,
             dx_ref, dproj_ref, h_ref, gpre_ref, gq_ref, gk_ref, dext_ref):
        i = pl.program_id(0)

        @pl.when(i == 0)
        def _():
            for ref in (gpre_ref, gq_ref, gk_ref):
                ref[...] = jnp.zeros_like(ref)

        proj = proj_ref[...]
        cos_pair = cos_ref[...]
        sin_pair = sin_ref[...]
        cosv = jnp.concatenate([cos_pair] * (N_HEADS // 2), axis=1)
        sinv = jnp.concatenate([sin_pair] * (N_HEADS // 2), axis=1)

        def head_norm_bwd(dr, z, bdm, g, cs, sn, gacc_ref):
            dn = dr * cs + _swap16(dr * sn)
            rr = lax.rsqrt(_group_mean(z * z, bdm) + EPS)
            gdn = dn * g
            gacc_ref[...] += jnp.sum(dn * z * rr, axis=0, keepdims=True)
            return rr * gdn - z * (rr * rr * rr * _group_mean(gdn * z, bdm))

        dq_cat = _heads_to_cat(dq_ref, N_HEADS) * ATT_SCALE
        dzq = head_norm_bwd(dq_cat, proj[:, C_Q:C_K], bd_ref[...], qg_ref[...], cosv, sinv, gq_ref)

        def kv_pair(ref):
            return jnp.concatenate([ref[0, 0:HEAD_DIM, :], ref[1, 0:HEAD_DIM, :]], axis=0).T

        dk_cat = kv_pair(dk_ref)
        dzk = head_norm_bwd(dk_cat, proj[:, C_K:C_V], bd_ref[0:LANES, 0:LANES], kg_ref[...],
                            cos_pair, sin_pair, gk_ref)
        dv_cat = kv_pair(dv_ref)

        _conv_window(dext_ref, dcp_ref, dc_ref[...], dcn_ref, i == 0, i == pl.num_programs(0) - 1, t)
        dhc = jnp.zeros((t, CONV_W), F32)
        for kk in range(CONV_K):
            dhc = dhc + cw_ref[kk:kk + 1, :] * _tap(dext_ref, CONV_K - kk, t)
        sg = _sigmoid(proj[:, C_A2:C_GC])
        da1 = dhc * sg
        da2 = dhc * proj[:, C_A1:C_A2] * sg * (1.0 - sg)

        dgs = dgs_ref[...]
        dproj_bf = jnp.concatenate([dzq, dzk, dv_cat, dgs[:, 0:512], da1, da2, dgs[:, 512:1536]], axis=1).astype(BF16)
        dproj_ref[...] = dproj_bf
        dh = _dot(dproj_bf, wt_ref[...])

        xv = x_ref[...]
        rr = _rms(xv)
        gv = g_ref[...]
        h_ref[...] = (xv * rr * gv).astype(BF16)
        gdh = dh * gv
        gpre_ref[...] += jnp.sum(dh * xv * rr, axis=0, keepdims=True)
        dx_ref[...] = dy_ref[...] + rr * gdh - xv * (rr * rr * rr * jnp.mean(gdh * xv, axis=-1, keepdims=True))

    row = lambda w: pl.BlockSpec((t, w), lambda i: (i, 0))
    heads = lambda n: pl.BlockSpec((n, t, LANES), lambda i: (0, i, 0))
    hprev, hnext = _halo_specs(t, s, CONV_W)
    tk = dk.shape[3]
    assert tk % t == 0
    kv_t = pl.BlockSpec((N_KV, None, LANES, t), lambda i: (0, i // (tk // t), 0, i % (tk // t)))
    return pl.pallas_call(
        body, name="bwd_in", grid=(s // t,),
        in_specs=[row(D_MODEL), row(D_MODEL), row(D_IN), heads(N_HEADS), kv_t, kv_t, row(1536),
                  row(CONV_W), hprev, hnext, _const_spec((32, CONV_W)), _const_spec((1, D_MODEL)),
                  _const_spec((D_IN, D_MODEL)), _const_spec((512, 512)), _const_spec((1, 512)),
                  _const_spec((1, LANES)), row(LANES), row(LANES)],
        out_specs=[row(D_MODEL), row(D_IN), row(D_MODEL), _const_spec((1, D_MODEL)), _const_spec((1, 512)),
                   _const_spec((1, LANES))],
        out_shape=[jax.ShapeDtypeStruct((s, D_MODEL), F32), jax.ShapeDtypeStruct((s, D_IN), BF16),
                   jax.ShapeDtypeStruct((s, D_MODEL), BF16), jax.ShapeDtypeStruct((1, D_MODEL), F32),
                   jax.ShapeDtypeStruct((1, 512), F32), jax.ShapeDtypeStruct((1, LANES), F32)],
        scratch_shapes=[pltpu.VMEM((SUBLANES, t + 2 * HALO, CONV_W), F32)],
        compiler_params=_arb(),
    )(dy, x, proj, dq, dk, dv, dgs, dc0, dc0, dc0, cw, g_pre, w_in_t_bf, bd, qg, kg, cos, sin)


def _grad_w_in(h_bf, dproj_bf, t):
    s = h_bf.shape[0]
    half = D_IN // 2

    def body(h_ref, d_ref, g_ref):
        @pl.when(pl.program_id(1) == 0)
        def _():
            g_ref[...] = jnp.zeros_like(g_ref)

        g_ref[...] += _dot_tn(h_ref[...], d_ref[...])

    return pl.pallas_call(
        body, name="grad_w_in", grid=(2, s // t),
        in_specs=[pl.BlockSpec((t, D_MODEL), lambda j, i: (i, 0)), pl.BlockSpec((t, half), lambda j, i: (i, j))],
        out_specs=pl.BlockSpec((D_MODEL, half), lambda j, i: (0, j)),
        out_shape=jax.ShapeDtypeStruct((D_MODEL, D_IN), F32),
        compiler_params=_arb(2),
    )(h_bf, dproj_bf)


def _place():
    x, y, c = lax.axis_index("x"), lax.axis_index("y"), lax.axis_index("c")
    chips = [(1 - x, y), (x, 1 - y), (1 - x, 1 - y)]
    return x, y, c, chips


def _any_specs(n):
    return [pl.BlockSpec(memory_space=pl.ANY)] * n


class _ChipExchange:
    def __init__(self, scatter, ins, outs, send_sems, recv_sems, local_sems):
        n = len(ins)
        x, y, c, chips = _place()
        mine = 2 * x + y
        src = (lambda a, piece: ins[a].at[piece]) if scatter else (lambda a, piece: ins[a])

        def copy(j, a, piece, slot, to):
            return pltpu.make_async_remote_copy(src_ref=src(a, piece), dst_ref=outs[a].at[slot],
                                                send_sem=send_sems.at[n * j + a], recv_sem=recv_sems.at[n * j + a],
                                                device_id=to, device_id_type=MESH)

        self.local = [pltpu.make_async_copy(src(a, mine), outs[a].at[mine], local_sems.at[a]) for a in range(n)]
        self.sends = [copy(j, a, 2 * px + py, mine, (px, py, c)) for j, (px, py) in enumerate(chips) for a in range(n)]
        self.arrivals = lambda: [copy(j, a, mine, 2 * px + py, (px, py, c))
                                 for j, (px, py) in enumerate(chips) for a in range(n)]

    def start(self):
        for cp in self.local + self.sends:
            cp.start()

    def finish(self):
        for cp in self.arrivals():
            cp.wait_recv()
        for cp in self.sends:
            cp.wait_send()
        for cp in self.local:
            cp.wait()

    @staticmethod
    def out_shapes(scatter, arrs):
        return [jax.ShapeDtypeStruct(a.shape if scatter else (N_CHIPS,) + a.shape, a.dtype) for a in arrs]

    @staticmethod
    def semaphores(n):
        return [pltpu.SemaphoreType.DMA((3 * n,)), pltpu.SemaphoreType.DMA((3 * n,)), pltpu.SemaphoreType.DMA((n,))]


def _exchange(scatter, arrs, name):
    n = len(arrs)

    def body(*refs):
        ex = _ChipExchange(scatter, refs[:n], refs[n:2 * n], *refs[2 * n:])
        ex.start()
        ex.finish()

    return pl.pallas_call(
        body, name=name, in_specs=_any_specs(n), out_specs=_any_specs(n),
        out_shape=_ChipExchange.out_shapes(scatter, arrs), scratch_shapes=_ChipExchange.semaphores(n),
    )(*arrs)


def _riding_exchange(scatter, arrs):
    n = len(arrs)

    def hook(ins, outs, sems):
        i = pl.program_id(0)

        @pl.when(i == 0)
        def _():
            _ChipExchange(scatter, ins, outs, *sems).start()

        @pl.when(i == pl.num_programs(0) - 1)
        def _():
            _ChipExchange(scatter, ins, outs, *sems).finish()

    return (tuple(arrs), _any_specs(n), _any_specs(n), _ChipExchange.out_shapes(scatter, arrs),
            _ChipExchange.semaphores(n), hook)


def _sum_chips(parts, rb, name):
    depth = len(parts)
    _, r, cdim = parts[0].shape

    def body(*refs):
        o_ref = refs[depth]
        for l in range(depth):
            def add(p_ref=refs[l]):
                part = lambda j: p_ref[j].astype(F32)
                o_ref[...] = ((part(0) + part(1)) + part(2)) + part(3)

            pl.when(pl.program_id(0) == l)(add)

    return pl.pallas_call(
        body, name=name, grid=(depth, r // rb),
        in_specs=[pl.BlockSpec((N_CHIPS, rb, cdim), lambda l, i: (0, i, 0))] * depth,
        out_specs=pl.BlockSpec((None, rb, cdim), lambda l, i: (l, i, 0)),
        out_shape=jax.ShapeDtypeStruct((depth, r, cdim), F32),
        compiler_params=_arb(2),
    )(*parts)


def _swap_with_sibling(a, b):
    arrs = (a, b)
    n = len(arrs)

    def body(*refs):
        ins, outs = refs[:n], refs[n:2 * n]
        send_sems, recv_sems = refs[2 * n:]
        x, y, c, _ = _place()
        cps = [pltpu.make_async_remote_copy(src_ref=ins[k], dst_ref=outs[k], send_sem=send_sems.at[k],
                                            recv_sem=recv_sems.at[k], device_id=(x, y, 1 - c), device_id_type=MESH)
               for k in range(n)]
        for cp in cps:
            cp.start()
        for cp in cps:
            cp.wait()

    return pl.pallas_call(
        body, name="swap_with_sibling",
        in_specs=_any_specs(n), out_specs=_any_specs(n),
        out_shape=[jax.ShapeDtypeStruct(v.shape, v.dtype) for v in arrs],
        scratch_shapes=[pltpu.SemaphoreType.DMA((n,)), pltpu.SemaphoreType.DMA((n,))],
    )(*arrs)


def _allreduce_small(slab):
    m, n = slab.shape

    def body(x_ref, out_ref, gath, send_sems, recv_sems, local_sem):
        x, y, c, chips = _place()
        me, sibling = (x, y, c), (x, y, 1 - c)

        def rows(px, py, pc):
            return gath.at[pl.ds(pl.multiple_of((4 * px + 2 * py + pc) * m, 8), m), :]

        def copy(k, block, to, src=None):
            return pltpu.make_async_remote_copy(src_ref=rows(*block) if src is None else src, dst_ref=rows(*block),
                                                send_sem=send_sems.at[k], recv_sem=recv_sems.at[k],
                                                device_id=to, device_id_type=MESH)

        mine = pltpu.make_async_copy(x_ref, rows(*me), local_sem)
        mine.start()
        first = [copy(0, me, sibling, src=x_ref)]
        first += [copy(1 + j, me, (*chip, c), src=x_ref) for j, chip in enumerate(chips)]
        for cp in first:
            cp.start()
        passed = [copy(4 + j, (*chip, c), sibling) for j, chip in enumerate(chips)]
        for j, chip in enumerate(chips):
            copy(1 + j, (*chip, c), me).wait_recv()
            passed[j].start()
        copy(0, sibling, me).wait_recv()
        for j, chip in enumerate(chips):
            copy(4 + j, (*chip, 1 - c), me).wait_recv()
        for cp in first + passed:
            cp.wait_send()
        mine.wait()
        total = gath[0:m, :]
        for d in range(1, N_DEV):
            total = total + gath[d * m:(d + 1) * m, :]
        out_ref[...] = total

    return pl.pallas_call(
        body, name="allreduce_small",
        in_specs=[pl.BlockSpec(memory_space=pltpu.VMEM)],
        out_specs=pl.BlockSpec(memory_space=pltpu.VMEM),
        out_shape=jax.ShapeDtypeStruct((m, n), F32),
        scratch_shapes=[pltpu.VMEM((N_DEV * m, n), F32), pltpu.SemaphoreType.DMA((7,)), pltpu.SemaphoreType.DMA((7,)),
                        pltpu.SemaphoreType.DMA],
    )(slab)


def _adamw(w, ga, gb, m, v, rb, name):
    depth, r, cdim = w.shape

    def body(w_ref, ga_ref, gb_ref, m_ref, v_ref, g_out, d_out, m_out, v_out):
        g = ga_ref[...] + gb_ref[...]
        m2 = ADAM_B1 * m_ref[...] + (1.0 - ADAM_B1) * g
        v2 = ADAM_B2 * v_ref[...] + (1.0 - ADAM_B2) * (g * g)
        m_hat = m2 / (1.0 - ADAM_B1 ** ADAM_STEP)
        v_hat = v2 / (1.0 - ADAM_B2 ** ADAM_STEP)
        g_out[...] = g
        d_out[...] = -ADAM_LR * (m_hat / (jnp.sqrt(v_hat) + ADAM_EPS) + ADAM_WD * w_ref[...])
        m_out[...] = m2
        v_out[...] = v2

    spec = pl.BlockSpec((None, rb, cdim), lambda l, i: (l, i, 0))
    shp = jax.ShapeDtypeStruct((depth, r, cdim), F32)
    return pl.pallas_call(
        body, name=name, grid=(depth, r // rb),
        in_specs=[spec] * 5, out_specs=[spec] * 4, out_shape=[shp] * 4,
        compiler_params=_arb(2),
    )(w, ga, gb, m, v)


def _rope_tables(s):
    t = jnp.arange(s, dtype=jnp.int32)
    row = (t // GRID_W).astype(F32)
    col = (t % GRID_W).astype(F32)
    half = HEAD_DIM // 4
    inv_freq = ROPE_THETA ** (-jnp.arange(half, dtype=F32) / half)
    ar = row[:, None] * inv_freq[None, :]
    ac = col[:, None] * inv_freq[None, :]
    cos = jnp.concatenate([jnp.cos(ar), jnp.cos(ar), jnp.cos(ac), jnp.cos(ac)], axis=1)
    sin = jnp.concatenate([-jnp.sin(ar), jnp.sin(ar), -jnp.sin(ac), jnp.sin(ac)], axis=1)
    return jnp.tile(cos, (1, 2)), jnp.tile(sin, (1, 2))


def _pad_rows(a, rows):
    return jnp.pad(a, ((0, rows - a.shape[0]),) + ((0, 0),) * (a.ndim - 1))


_SMALL = ("pre_norm", "post_norm", "q_norm", "k_norm", "conv_dw_b", "conv_ln_g", "conv_ln_b", "sg_ln_g", "sg_ln_b",
          "sg_w", "sg_b")


def _pack(parts):
    flat = jnp.concatenate([p.reshape(-1, LANES) for p in parts], axis=0)
    return _pad_rows(flat, -(-flat.shape[0] // 8) * 8)


def _unpack(slab, shapes):
    out, r = [], 0
    for shp in shapes:
        n = 1
        for d in shp:
            n *= d
        out.append(slab[r:r + n // LANES].reshape(shp))
        r += n // LANES
    return out


def kernel(x, pre_norm, post_norm, w_in, w_out, q_norm, k_norm, conv_dw, conv_dw_b, conv_ln_g, conv_ln_b, sg_ln_g, sg_ln_b, sg_w, sg_b, loss_target, m_pre_norm, m_post_norm, m_w_in, m_w_out, m_q_norm, m_k_norm, m_conv_dw, m_conv_dw_b, m_conv_ln_g, m_conv_ln_b, m_sg_ln_g, m_sg_ln_b, m_sg_w, m_sg_b, v_pre_norm, v_post_norm, v_w_in, v_w_out, v_q_norm, v_k_norm, v_conv_dw, v_conv_dw_b, v_conv_ln_g, v_conv_ln_b, v_sg_ln_g, v_sg_ln_b, v_sg_w, v_sg_b):
    depth = w_in.shape[0]
    s = x.shape[1]
    assert x.shape[0] == 1 and s % SG_CHUNK == 0 and x.shape[2] == D_MODEL
    tq = min(256, s)
    tk = min(512, s // 2)
    tk_fwd = min(1024, s // 2)
    t = min(256, tk)
    shard_cols = w_in.shape[2]
    chip = 2 * lax.axis_index("x") + lax.axis_index("y")

    w_in_sh, w_out_sh = w_in.astype(BF16), w_out.astype(BF16)
    whole_w_in = lambda g: jnp.concatenate([g[j] for j in range(N_CHIPS)], axis=2)
    w_in_bf = [whole_w_in(_exchange(False, (w_in_sh[0:1],), "gather_w_in_first")[0])[0]]
    riders = ((w_in_sh[1:],) if depth > 1 else ()) + (w_out_sh, conv_dw)
    sgw_bf = sg_w.astype(BF16)
    sgwt_bf = jnp.swapaxes(sg_w, 2, 3).astype(BF16)

    cos, sin = _rope_tables(s)
    bd = jnp.kron(jnp.eye(N_HEADS, dtype=F32), jnp.full((HEAD_DIM, HEAD_DIM), 1.0 / HEAD_DIM, F32)).astype(BF16)

    def head_gains(l):
        return jnp.tile(q_norm[l], N_HEADS)[None, :], jnp.tile(k_norm[l], N_KV)[None, :]

    def layer_consts(l):
        cw = _pad_rows(cdw_full[l], 32)
        cp = _pad_rows(jnp.stack([conv_dw_b[l], conv_ln_g[l], conv_ln_b[l]]), 8)
        sp = _pad_rows(jnp.stack([sg_ln_g[l], sg_ln_b[l]]), 8)
        sgb = jnp.repeat(sg_b[l].T, HEAD_DIM, axis=1)
        return cw, cp, sp, sgb

    xs = [x[0]]
    saved = []
    for l in range(depth):
        qg, kg = head_gains(l)
        outs = _fwd_in(xs[l], pre_norm[l][None, :], w_in_bf[l], bd, qg, kg, cos, sin, t, gather=riders if l == 0 else ())
        proj, hc, q, k, v = outs[:5]
        if l == 0:
            gathered = list(outs[5:])
            if depth > 1:
                w_in_bf += list(whole_w_in(gathered.pop(0)))
            w_out_bf = jnp.concatenate([gathered[0][j] for j in range(N_CHIPS)], axis=1)
            cdw_full = jnp.concatenate([gathered[1][j] for j in range(N_CHIPS)], axis=2)
            w_out_t_bf = jnp.swapaxes(w_out_bf, 1, 2)
        cw, cp, sp, sgb = layer_consts(l)
        o, qa = _flash_fwd(q, k, v, tq, tk_fwd)
        outs = _fwd_out(xs[l], proj, o, hc, cw, cp, sp, sgw_bf[l], sgb, w_out_bf[l], post_norm[l][None, :], t,
                        target=loss_target[0] if l == depth - 1 else None)
        saved.append((proj, hc, qa, k, v, o, outs[0], outs[2]))
        if l < depth - 1:
            xs.append(outs[1])
    dy, sq = outs[1], outs[3]
    loss = lax.psum(0.5 * jnp.sum(sq) / D_MODEL, ("x", "y", "c"))

    g_small = {n: [] for n in _SMALL + ("conv_dw",)}
    received = [None] * depth
    pieces = ()
    for l in reversed(range(depth)):
        cw, cp, sp, sgb = layer_consts(l)
        qg, kg = head_gains(l)
        proj, hc, qa, k, v, o, mix, c0 = saved[l]
        outs = _bwd_out(dy, mix, proj, o, hc, c0, cw, cp, sp, sgw_bf[l], sgwt_bf[l], sgb, w_out_t_bf[l],
                        post_norm[l][None, :], t, scatter=pieces)
        doa, dgs, dc0, gwo, gpost, gcw, gvec, gsgw, gsgb = outs[:9]
        if pieces:
            received[l + 1] = outs[9:]
        dq, dk, dv = _flash_bwd(qa, doa, k, v, tq, tk)
        dy, dproj_bf, h_bf, gpre, gq, gk = _bwd_in(dy, xs[l], proj, dq, dk, dv, dgs, dc0, cw, pre_norm[l][None, :],
                                                  jnp.swapaxes(w_in_bf[l], 0, 1), bd, qg, kg, cos, sin, t)
        gwi = _grad_w_in(h_bf, dproj_bf, min(512, s))
        pieces = (jnp.stack([gwi[:, j * shard_cols:(j + 1) * shard_cols] for j in range(N_CHIPS)]).astype(BF16),
                  gwo.reshape(N_CHIPS, gwo.shape[0] // N_CHIPS, gwo.shape[1]).astype(BF16))
        g_small["pre_norm"].append(gpre[0])
        g_small["post_norm"].append(gpost[0])
        g_small["q_norm"].append(gq[0].reshape(N_HEADS, HEAD_DIM).sum(0))
        g_small["k_norm"].append(gk[0].reshape(N_KV, HEAD_DIM).sum(0))
        g_small["conv_dw"].append(gcw[:CONV_K])
        g_small["conv_dw_b"].append(gvec[0])
        g_small["conv_ln_g"].append(gvec[1])
        g_small["conv_ln_b"].append(gvec[2])
        g_small["sg_ln_g"].append(gvec[3])
        g_small["sg_ln_b"].append(gvec[4])
        g_small["sg_w"].append(gsgw)
        g_small["sg_b"].append(gsgb.reshape(SG_CHUNK, SG_W // HEAD_DIM, HEAD_DIM).sum(-1).T)
    grad_x = dy[None]
    g_small = {n: jnp.stack(vals[::-1]) for n, vals in g_small.items()}

    received[0] = _exchange(True, pieces, "scatter_grads")
    s_in = _sum_chips([r[0] for r in received], 256, "sum_chips_w_in")
    s_out = _sum_chips([r[1] for r in received], 256, "sum_chips_w_out")
    t_in, t_out = _swap_with_sibling(s_in, s_out)
    grad_w_in, delta_w_in, new_m_w_in, new_v_w_in = _adamw(w_in, s_in, t_in, m_w_in, v_w_in, 256, "adamw_w_in")
    grad_w_out, delta_w_out, new_m_w_out, new_v_w_out = _adamw(w_out, s_out, t_out, m_w_out, v_w_out, 256, "adamw_w_out")

    small_w = dict(pre_norm=pre_norm, post_norm=post_norm, q_norm=q_norm, k_norm=k_norm, conv_dw_b=conv_dw_b,
                   conv_ln_g=conv_ln_g, conv_ln_b=conv_ln_b, sg_ln_g=sg_ln_g, sg_ln_b=sg_ln_b, sg_w=sg_w, sg_b=sg_b)
    small_m = dict(pre_norm=m_pre_norm, post_norm=m_post_norm, q_norm=m_q_norm, k_norm=m_k_norm, conv_dw_b=m_conv_dw_b,
                   conv_ln_g=m_conv_ln_g, conv_ln_b=m_conv_ln_b, sg_ln_g=m_sg_ln_g, sg_ln_b=m_sg_ln_b, sg_w=m_sg_w,
                   sg_b=m_sg_b)
    small_v = dict(pre_norm=v_pre_norm, post_norm=v_post_norm, q_norm=v_q_norm, k_norm=v_k_norm, conv_dw_b=v_conv_dw_b,
                   conv_ln_g=v_conv_ln_g, conv_ln_b=v_conv_ln_b, sg_ln_g=v_sg_ln_g, sg_ln_b=v_sg_ln_b, sg_w=v_sg_w,
                   sg_b=v_sg_b)
    shapes = [small_w[n].shape for n in _SMALL]
    red = _allreduce_small(_pack([g_small[n] for n in _SMALL] + [g_small["conv_dw"]]))
    n_rep = sum(small_w[n].size for n in _SMALL) // LANES
    g_cdw_full = red[n_rep:n_rep + g_small["conv_dw"].size // LANES].reshape(g_small["conv_dw"].shape)
    cdw_cols = conv_dw.shape[2]
    g_cdw = lax.dynamic_slice_in_dim(g_cdw_full, chip * cdw_cols, cdw_cols, axis=2)
    g_slab = _pack([red[:n_rep], g_cdw])
    w_slab = _pack([small_w[n] for n in _SMALL] + [conv_dw])
    m_slab = _pack([small_m[n] for n in _SMALL] + [m_conv_dw])
    v_slab = _pack([small_v[n] for n in _SMALL] + [v_conv_dw])
    rows = w_slab.shape[0]
    outs = _adamw(w_slab[None], g_slab[None], jnp.zeros_like(g_slab)[None], m_slab[None], v_slab[None], rows, "adamw_small")
    unpacked = [dict(zip(_SMALL + ("conv_dw",), _unpack(o_[0], shapes + [conv_dw.shape]))) for o_ in outs]

    big = [dict(w_in=a, w_out=b) for a, b in ((grad_w_in, grad_w_out), (delta_w_in, delta_w_out),
                                             (new_m_w_in, new_m_w_out), (new_v_w_in, new_v_w_out))]
    order = ("pre_norm", "post_norm", "w_in", "w_out", "q_norm", "k_norm", "conv_dw", "conv_dw_b", "conv_ln_g",
             "conv_ln_b", "sg_ln_g", "sg_ln_b", "sg_w", "sg_b")
    result = [loss, grad_x]
    for kind in range(4):
        for name in order:
            result.append(big[kind][name] if name in big[kind] else unpacked[kind][name])
    return tuple(result)
```

```python
import functools

import jax
import jax.numpy as jnp
from jax import lax
from jax.experimental import pallas as pl
from jax.experimental.pallas import tpu as pltpu

F32 = jnp.float32
BF16 = jnp.bfloat16
MESH = pl.DeviceIdType.MESH

EPS = 1e-6
D_MODEL = 1024
D_IN = 2816
HEAD_DIM = 64
LANES = 128
SUBLANES = 8
N_HEADS = 8
N_KV = 2
GROUP = N_HEADS // N_KV
GRID_W = 64
ROPE_THETA = 10000.0
CONV_K = 31
CONV_W = 256
SG_W = 256
SG_CHUNK = 128
HALO = 16
ATT_SCALE = HEAD_DIM ** -0.5

C_Q, C_K, C_V, C_GA, C_A1, C_A2, C_GC, C_U, C_VS, C_GS = 0, 512, 640, 768, 1280, 1536, 1792, 2048, 2304, 2560

ADAM_LR = 0.001
ADAM_B1 = 0.9
ADAM_B2 = 0.999
ADAM_EPS = 1e-08
ADAM_WD = 0.01
ADAM_STEP = 10

N_CHIPS = 4
N_DEV = 8


def _dot(a, b):
    return jnp.dot(a, b, preferred_element_type=F32)


def _group_mean(x, bd_bf):
    hi = x.astype(BF16)
    lo = (x - hi.astype(F32)).astype(BF16)
    return _dot(hi, bd_bf) + _dot(lo, bd_bf)


def _dot_nt(a, b):
    return lax.dot_general(a, b, (((1,), (1,)), ((), ())), preferred_element_type=F32)


def _dot_tn(a, b):
    return lax.dot_general(a, b, (((0,), (0,)), ((), ())), preferred_element_type=F32)


def _lane(shape):
    return lax.broadcasted_iota(jnp.int32, shape, 1)


def _sigmoid(x):
    return 1.0 / (1.0 + jnp.exp(-x))


def _silu_fwd_bwd(x):
    s = _sigmoid(x)
    return x * s, s * (1.0 + x * (1.0 - s))


def _erf(x):
    x = jnp.clip(x, -4.0, 4.0)
    x2 = x * x
    a = -2.72614225801306e-10
    a = a * x2 + 2.77068142495902e-08
    a = a * x2 + -2.10102402082508e-06
    a = a * x2 + -5.69250639462346e-05
    a = a * x2 + -7.34990630326855e-04
    a = a * x2 + -2.95459980854025e-03
    a = a * x2 + -1.60960333262415e-02
    b = -1.45660718464996e-05
    b = b * x2 + -2.13374055278905e-04
    b = b * x2 + -1.68282697438203e-03
    b = b * x2 + -7.37332916720468e-03
    b = b * x2 + -1.42647390514189e-02
    return x * a / b


def _gelu_fwd_bwd(x):
    cdf = 0.5 * (1.0 + _erf(x * 0.7071067811865476))
    pdf = jnp.exp(-0.5 * x * x) * 0.3989422804014327
    return x * cdf, cdf + x * pdf


def _rms(x):
    return lax.rsqrt(jnp.mean(x * x, axis=-1, keepdims=True) + EPS)


def _ln_hat(x):
    mu = jnp.mean(x, axis=-1, keepdims=True)
    xc = x - mu
    rs = lax.rsqrt(jnp.mean(xc * xc, axis=-1, keepdims=True) + EPS)
    return xc * rs, rs


def _ln_bwd(dxh, xh, rs):
    return rs * (dxh - jnp.mean(dxh, axis=-1, keepdims=True) - xh * jnp.mean(dxh * xh, axis=-1, keepdims=True))


def _swap16(z):
    parts = []
    for i in range(z.shape[1] // LANES):
        blk = z[:, i * LANES:(i + 1) * LANES]
        lane = _lane(blk.shape)
        parts.append(jnp.where((lane & 16) == 0, pltpu.roll(blk, LANES - 16, 1), pltpu.roll(blk, 16, 1)))
    return parts[0] if len(parts) == 1 else jnp.concatenate(parts, axis=1)


def _head_slab(pair, odd):
    src = pltpu.roll(pair, HEAD_DIM, 1) if odd else pair
    return jnp.where(_lane(pair.shape) < HEAD_DIM, src, 0.0)


def _pair_merge(even, odd):
    return jnp.where(_lane(even.shape) < HEAD_DIM, even, pltpu.roll(odd, HEAD_DIM, 1))


def _heads_to_cat(ref, n_heads):
    pairs = [_pair_merge(ref[2 * p], ref[2 * p + 1]) for p in range(n_heads // 2)]
    return pairs[0] if len(pairs) == 1 else jnp.concatenate(pairs, axis=1)


def _split3(x):
    hi = x.astype(BF16).astype(F32)
    r = x - hi
    mid = r.astype(BF16).astype(F32)
    lo = (r - mid).astype(BF16).astype(F32)
    return hi, mid, lo


def _with_spare(slab, hi, mid, lo):
    lane = _lane(slab.shape)
    return jnp.where(lane == HEAD_DIM, hi, jnp.where(lane == HEAD_DIM + 1, mid, jnp.where(lane == HEAD_DIM + 2, lo, slab)))


def _with_ones(slab):
    lane = _lane(slab.shape)
    return jnp.where((lane >= HEAD_DIM) & (lane < HEAD_DIM + 3), 1.0, slab)


def _conv_window(rot_ref, prev_ref, main, next_ref, first, last, t):
    n = t + 2 * HALO
    full = jnp.concatenate([jnp.where(first, 0.0, prev_ref[...]), main, jnp.where(last, 0.0, next_ref[...])], axis=0)
    rot_ref[0] = full
    for b in range(1, SUBLANES):
        rot_ref[b] = pltpu.roll(full, n - b, 0)


def _tap(rot_ref, start, t):
    a, b = divmod(start, SUBLANES)
    return rot_ref[b, SUBLANES * a:SUBLANES * a + t, :]


def _sgu_mix(v1_bf, w_ref, n_chunks):
    rows = []
    for n in range(n_chunks):
        pairs = []
        for p in range(SG_W // LANES):
            xp = v1_bf[n * SG_CHUNK:(n + 1) * SG_CHUNK, p * LANES:(p + 1) * LANES]
            me = _dot(w_ref[2 * p], xp)
            mo = _dot(w_ref[2 * p + 1], xp)
            pairs.append(jnp.where(_lane(me.shape) < HEAD_DIM, me, mo))
        rows.append(jnp.concatenate(pairs, axis=1))
    return rows[0] if len(rows) == 1 else jnp.concatenate(rows, axis=0)


def _halo_specs(t, s, width):
    per = t // HALO
    nblk = s // HALO
    prev = pl.BlockSpec((HALO, width), lambda i: (jnp.maximum(i * per - 1, 0), 0))
    nxt = pl.BlockSpec((HALO, width), lambda i: (jnp.minimum((i + 1) * per, nblk - 1), 0))
    return prev, nxt


def _const_spec(shape):
    nd = len(shape)
    return pl.BlockSpec(shape, lambda i: (0,) * nd)


def _arb(n=1):
    return pltpu.CompilerParams(dimension_semantics=("arbitrary",) * n)


def _fwd_in(x, g_pre, w_in_bf, bd, qg, kg, cos, sin, t, gather=()):
    s = x.shape[0]
    ex_args, ex_in, ex_out, ex_shapes, ex_sems, ex_hook = _riding_exchange(False, gather)
    n_ex = len(gather)

    def body(*refs):
        x_ref, g_ref, w_ref, bd_ref, qg_ref, kg_ref, cos_ref, sin_ref = refs[:8]
        proj_ref, hc_ref, q_ref, k_ref, v_ref = refs[8 + n_ex:13 + n_ex]
        if n_ex:
            ex_hook(refs[8:8 + n_ex], refs[13 + n_ex:13 + 2 * n_ex], refs[13 + 2 * n_ex:])
        xv = x_ref[...]
        h = (xv * _rms(xv) * g_ref[...]).astype(BF16)
        proj = _dot(h, w_ref[...])
        proj_ref[...] = proj
        cos_pair = cos_ref[...]
        sin_pair = sin_ref[...]
        cosv = jnp.concatenate([cos_pair] * (N_HEADS // 2), axis=1)
        sinv = jnp.concatenate([sin_pair] * (N_HEADS // 2), axis=1)
        q = proj[:, C_Q:C_K]
        qn = q * lax.rsqrt(_group_mean(q * q, bd_ref[...]) + EPS) * qg_ref[...]
        qr = (qn * cosv + _swap16(qn) * sinv) * ATT_SCALE
        for hh in range(N_HEADS):
            pair = qr[:, (hh // 2) * LANES:(hh // 2 + 1) * LANES]
            q_ref[hh] = _head_slab(pair, hh % 2 == 1).astype(BF16)
        k = proj[:, C_K:C_V]
        kn = k * lax.rsqrt(_group_mean(k * k, bd_ref[0:LANES, 0:LANES]) + EPS) * kg_ref[...]
        kr = kn * cos_pair + _swap16(kn) * sin_pair
        vv = proj[:, C_V:C_GA]
        for hh in range(N_KV):
            k_ref[hh] = _with_ones(_head_slab(kr, hh == 1)).astype(BF16)
            v_ref[hh] = _with_ones(_head_slab(vv, hh == 1)).astype(BF16)
        hc_ref[...] = proj[:, C_A1:C_A2] * _sigmoid(proj[:, C_A2:C_GC])

    row = lambda w: pl.BlockSpec((t, w), lambda i: (i, 0))
    heads = lambda n: pl.BlockSpec((n, t, LANES), lambda i: (0, i, 0))
    return pl.pallas_call(
        body, name="fwd_in_gather" if n_ex else "fwd_in", grid=(s // t,),
        in_specs=[row(D_MODEL), _const_spec((1, D_MODEL)), _const_spec((D_MODEL, D_IN)), _const_spec((512, 512)),
                  _const_spec((1, 512)), _const_spec((1, LANES)), row(LANES), row(LANES)] + ex_in,
        out_specs=[row(D_IN), row(CONV_W), heads(N_HEADS), heads(N_KV), heads(N_KV)] + ex_out,
        out_shape=[jax.ShapeDtypeStruct((s, D_IN), F32), jax.ShapeDtypeStruct((s, CONV_W), F32),
                   jax.ShapeDtypeStruct((N_HEADS, s, LANES), BF16), jax.ShapeDtypeStruct((N_KV, s, LANES), BF16),
                   jax.ShapeDtypeStruct((N_KV, s, LANES), BF16)] + ex_shapes,
        scratch_shapes=ex_sems if n_ex else [],
        compiler_params=_arb(),
    )(x, g_pre, w_in_bf, bd, qg, kg, cos, sin, *ex_args)


def _chunk_rows(c, tk):
    return pl.ds(c * tk, tk) if isinstance(c, int) else pl.ds(pl.multiple_of(c * tk, tk), tk)


def _three_stage_pipeline(nk, per_trip, stage1, stage2, stage3, peel):
    assert nk % 2 == 0 and per_trip % 2 == 0

    def step(t, parity, first=False, last=False):
        if not last:
            stage1(t + 1, 1 - parity)
        stage2(parity)
        if not first:
            stage3(t - 1, 1 - parity)

    stage1(0, 0)
    if not peel:
        while nk % per_trip:
            per_trip //= 2

        def whole_trip(i, carry):
            for u in range(per_trip):
                c = per_trip * i + u
                stage1(jnp.minimum(c + 1, nk - 1), 1 - u % 2)
                stage2(u % 2)
                stage3(jnp.maximum(c - 1, 0), 1 - u % 2)
            return carry

        lax.fori_loop(0, nk // per_trip, whole_trip, 0)
        stage3(nk - 1, 1)
        return

    step(0, 0, first=True)
    n_trips, left = divmod(nk - 2, per_trip)

    def trip(i, carry):
        for u in range(per_trip):
            step(1 + per_trip * i + u, (1 + u) % 2)
        return carry

    if n_trips:
        lax.fori_loop(0, n_trips, trip, 0)
    for t in range(1 + n_trips * per_trip, 1 + n_trips * per_trip + left):
        step(t, t % 2)
    step(nk - 1, 1, last=True)
    stage3(nk - 1, 1)


def _flash_fwd(q, k, v, tq, tk):
    s = q.shape[1]
    rows = GROUP * tq
    nk = s // tk

    def body(q_ref, k_ref, vt_ref, o_ref, qa_ref, m_scr, acc_scr, s0, s1, p0, p1, a0, a1):
        s_bufs, p_bufs, a_bufs = (s0, s1), (p0, p1), (a0, a1)
        qv = q_ref[...].reshape(rows, LANES)
        q_t = qv.astype(F32).T
        q_t_bf = q_t.astype(BF16)
        m_scr[...] = jnp.full((1, rows), -jnp.inf, F32)
        acc_scr[...] = jnp.zeros((LANES, rows), F32)

        def scores(c, slot):
            s_bufs[slot][...] = _dot(k_ref[_chunk_rows(c, tk), :], q_t_bf)

        def softmax(slot):
            for h in range(GROUP):
                r = slice(h * tq, (h + 1) * tq)
                sc = s_bufs[slot][:, r]
                m_prev = m_scr[:, r]
                m_new = jnp.maximum(m_prev, jnp.max(sc, axis=0, keepdims=True))
                p_bufs[slot][:, r] = jnp.exp((sc - m_new).astype(BF16))
                a_bufs[slot][:, r] = jnp.exp(m_prev - m_new)
                m_scr[:, r] = m_new

        def weighted_values(c, slot):
            acc_scr[...] = a_bufs[slot][...] * acc_scr[...] + _dot(vt_ref[c], p_bufs[slot][...])

        _three_stage_pipeline(nk, 4, scores, softmax, weighted_values, peel=True)

        acc = acc_scr[...]
        row = lax.broadcasted_iota(jnp.int32, acc.shape, 0)
        l = jnp.sum(jnp.where(row == HEAD_DIM, acc, 0.0), axis=0, keepdims=True)
        o_ref[...] = jnp.where(row < HEAD_DIM, acc / l, 0.0).T.reshape(GROUP, tq, LANES)
        hi, mid, lo = _split3(-(m_scr[...] + jnp.log(l)))
        qa_t = jnp.where(row == HEAD_DIM, hi, jnp.where(row == HEAD_DIM + 1, mid,
                                                        jnp.where(row == HEAD_DIM + 2, lo, q_t)))
        qa_ref[...] = qa_t.T.astype(BF16).reshape(GROUP, tq, LANES)

    qspec = pl.BlockSpec((GROUP, tq, LANES), lambda j, i: (j, i, 0))
    kspec = pl.BlockSpec((None, s, LANES), lambda j, i: (j, 0, 0))
    vtspec = pl.BlockSpec((None, nk, LANES, tk), lambda j, i: (j, 0, 0, 0))
    v_t = jnp.swapaxes(v.reshape(N_KV, nk, tk, LANES), 2, 3)
    return pl.pallas_call(
        body, name="flash_fwd", grid=(N_KV, s // tq),
        in_specs=[qspec, kspec, vtspec],
        out_specs=[qspec, qspec],
        out_shape=[jax.ShapeDtypeStruct((N_HEADS, s, LANES), F32), jax.ShapeDtypeStruct((N_HEADS, s, LANES), BF16)],
        scratch_shapes=[pltpu.VMEM((1, rows), F32), pltpu.VMEM((LANES, rows), F32),
                        pltpu.VMEM((tk, rows), F32), pltpu.VMEM((tk, rows), F32),
                        pltpu.VMEM((tk, rows), BF16), pltpu.VMEM((tk, rows), BF16),
                        pltpu.VMEM((1, rows), F32), pltpu.VMEM((1, rows), F32)],
        compiler_params=_arb(2),
    )(q, k, v_t)


def _groups_fwd(proj_ref, o_ref, hext_ref, cw_ref, cp_ref, sp_ref, sgw_ref, sgb_ref, t, c0=None):
    proj = proj_ref[...]
    r = {}
    r["att"] = _heads_to_cat(o_ref, N_HEADS)
    r["gate_a"], r["dgate_a"] = _silu_fwd_bwd(proj[:, C_GA:C_A1])
    r["att_g"] = r["att"] * r["gate_a"]
    if c0 is None:
        c0 = jnp.zeros((t, CONV_W), F32) + cp_ref[0:1, :]
        for kk in range(CONV_K):
            c0 = c0 + cw_ref[kk:kk + 1, :] * _tap(hext_ref, kk + 1, t)
    r["c0"] = c0
    r["xh_c"], r["rs_c"] = _ln_hat(c0)
    r["c1"] = r["xh_c"] * cp_ref[1:2, :] + cp_ref[2:3, :]
    r["sg_c1"] = _sigmoid(r["c1"])
    r["c2"] = r["c1"] * r["sg_c1"]
    r["gate_c"], r["dgate_c"] = _silu_fwd_bwd(proj[:, C_GC:C_U])
    r["cnv_g"] = r["c2"] * r["gate_c"]
    r["gu"], r["dgu"] = _gelu_fwd_bwd(proj[:, C_U:C_VS])
    gv, r["dgv"] = _gelu_fwd_bwd(proj[:, C_VS:C_GS])
    r["xh_s"], r["rs_s"] = _ln_hat(gv)
    v1 = r["xh_s"] * sp_ref[0:1, :] + sp_ref[1:2, :]
    r["v1_bf"] = v1.astype(BF16)
    r["mixed"] = _sgu_mix(r["v1_bf"], sgw_ref, t // SG_CHUNK) + jnp.concatenate([sgb_ref[...]] * (t // SG_CHUNK), axis=0)
    r["um"] = r["gu"] * r["mixed"]
    r["gate_s"], r["dgate_s"] = _silu_fwd_bwd(proj[:, C_GS:D_IN])
    r["sgu_g"] = r["um"] * r["gate_s"]
    r["mc_bf"] = jnp.concatenate([r["att_g"], r["cnv_g"], r["sgu_g"]], axis=1).astype(BF16)
    return r


def _fwd_out(x, proj, o, hc, cw, cp, sp, sgw_bf, sgb, w_out_bf, g_post, t, target=None):
    s = x.shape[0]
    last_layer = target is not None

    def body(*refs):
        (x_ref, proj_ref, o_ref, hc_ref, hp_ref, hn_ref, cw_ref, cp_ref, sp_ref, sgw_ref, sgb_ref,
         w_ref, g_ref) = refs[:13]
        rest = refs[13:]
        if last_layer:
            t_ref, mix_ref, out_ref, c0_ref, sq_ref, hext_ref = rest
        else:
            mix_ref, out_ref, c0_ref, hext_ref = rest
        i = pl.program_id(0)
        _conv_window(hext_ref, hp_ref, hc_ref[...], hn_ref, i == 0, i == pl.num_programs(0) - 1, t)
        r = _groups_fwd(proj_ref, o_ref, hext_ref, cw_ref, cp_ref, sp_ref, sgw_ref, sgb_ref, t)
        c0_ref[...] = r["c0"]
        mix = _dot(r["mc_bf"], w_ref[...])
        mix_ref[...] = mix
        y = x_ref[...] + mix * _rms(mix) * g_ref[...]
        if last_layer:
            @pl.when(i == 0)
            def _():
                sq_ref[...] = jnp.zeros_like(sq_ref)

            err = y - t_ref[...]
            out_ref[...] = err * (1.0 / D_MODEL)
            sq_ref[...] += jnp.sum(err * err, axis=0, keepdims=True)
        else:
            out_ref[...] = y

    row = lambda w: pl.BlockSpec((t, w), lambda i: (i, 0))
    hprev, hnext = _halo_specs(t, s, CONV_W)
    big = jax.ShapeDtypeStruct((s, D_MODEL), F32)
    return pl.pallas_call(
        body, name="fwd_out_loss" if last_layer else "fwd_out", grid=(s // t,),
        in_specs=[row(D_MODEL), row(D_IN), pl.BlockSpec((N_HEADS, t, LANES), lambda i: (0, i, 0)), row(CONV_W),
                  hprev, hnext, _const_spec((32, CONV_W)), _const_spec((8, CONV_W)), _const_spec((8, SG_W)),
                  _const_spec((4, SG_CHUNK, SG_CHUNK)), _const_spec((SG_CHUNK, SG_W)),
                  _const_spec((D_MODEL, D_MODEL)), _const_spec((1, D_MODEL))] + ([row(D_MODEL)] if last_layer else []),
        out_specs=[row(D_MODEL), row(D_MODEL), row(CONV_W)] + ([_const_spec((1, D_MODEL))] if last_layer else []),
        out_shape=[big, big, jax.ShapeDtypeStruct((s, CONV_W), F32)]
        + ([jax.ShapeDtypeStruct((1, D_MODEL), F32)] if last_layer else []),
        scratch_shapes=[pltpu.VMEM((SUBLANES, t + 2 * HALO, CONV_W), F32)],
        compiler_params=_arb(),
    )(*((x, proj, o, hc, hc, hc, cw, cp, sp, sgw_bf, sgb, w_out_bf, g_post) + ((target,) if last_layer else ())))


def _bwd_out(dy, mix, proj, o, hc, c0, cw, cp, sp, sgw_bf, sgwt_bf, sgb, w_out_t_bf, g_post, t, scatter=()):
    s = dy.shape[0]
    n_chunks = t // SG_CHUNK
    ex_args, ex_in, ex_out, ex_shapes, ex_sems, ex_hook = _riding_exchange(True, scatter)
    n_ex = len(scatter)

    def body(*refs):
        (dy_ref, mix_ref, proj_ref, o_ref, hc_ref, hp_ref, hn_ref, c0_ref, cw_ref, cp_ref, sp_ref, sgw_ref,
         sgwt_ref, sgb_ref, wt_ref, g_ref) = refs[:16]
        (do_ref, dgs_ref, dc0_ref, gwo_ref, gpost_ref, gcw_ref, gvec_ref, gsgw_ref,
         gsgb_ref) = refs[16 + n_ex:25 + n_ex]
        hext_ref = refs[25 + 2 * n_ex]
        if n_ex:
            ex_hook(refs[16:16 + n_ex], refs[25 + n_ex:25 + 2 * n_ex], refs[26 + 2 * n_ex:])
        i = pl.program_id(0)

        @pl.when(i == 0)
        def _():
            for ref in (gwo_ref, gpost_ref, gcw_ref, gvec_ref, gsgw_ref, gsgb_ref):
                ref[...] = jnp.zeros_like(ref)

        _conv_window(hext_ref, hp_ref, hc_ref[...], hn_ref, i == 0, i == pl.num_programs(0) - 1, t)
        r = _groups_fwd(proj_ref, o_ref, hext_ref, cw_ref, cp_ref, sp_ref, sgw_ref, sgb_ref, t, c0=c0_ref[...])

        dyv = dy_ref[...]
        mix_v = mix_ref[...]
        rr = _rms(mix_v)
        gd = dyv * g_ref[...]
        dmix = rr * gd - mix_v * (rr * rr * rr * jnp.mean(gd * mix_v, axis=-1, keepdims=True))
        gpost_ref[...] += jnp.sum(dyv * mix_v * rr, axis=0, keepdims=True)
        dmix_bf = dmix.astype(BF16)
        gwo_ref[...] += _dot_tn(r["mc_bf"], dmix_bf)
        dmc = _dot(dmix_bf, wt_ref[...])

        d_att = dmc[:, 0:512]
        dg_att = d_att * r["att"] * r["dgate_a"]
        d_o = d_att * r["gate_a"]
        prod = d_o * r["att"]
        for p in range(N_HEADS // 2):
            sl = slice(p * LANES, (p + 1) * LANES)
            pr = prod[:, sl]
            tot = jnp.sum(pr, axis=1, keepdims=True)
            ev = jnp.sum(jnp.where(_lane(pr.shape) < HEAD_DIM, pr, 0.0), axis=1, keepdims=True)
            for odd, delta in ((False, ev), (True, tot - ev)):
                hi, mid, lo = _split3(-delta)
                do_ref[2 * p + int(odd)] = _with_spare(_head_slab(d_o[:, sl], odd), hi, mid, lo).astype(BF16)

        dcv = dmc[:, 512:768]
        dg_conv = dcv * r["c2"] * r["dgate_c"]
        dc1 = dcv * r["gate_c"] * (r["sg_c1"] * (1.0 + r["c1"] * (1.0 - r["sg_c1"])))
        dc0 = _ln_bwd(dc1 * cp_ref[1:2, :], r["xh_c"], r["rs_c"])
        dc0_ref[...] = dc0
        for kk in range(CONV_K):
            gcw_ref[kk:kk + 1, :] += jnp.sum(dc0 * _tap(hext_ref, kk + 1, t), axis=0, keepdims=True)

        dsg = dmc[:, 768:1024]
        dg_sg = dsg * r["um"] * r["dgate_s"]
        du = dsg * r["mixed"] * r["gate_s"] * r["dgu"]
        dmx = dsg * r["gu"] * r["gate_s"]
        dmx_bf = dmx.astype(BF16)
        sgb_sum = dmx[0:SG_CHUNK, :]
        for n in range(1, n_chunks):
            sgb_sum = sgb_sum + dmx[n * SG_CHUNK:(n + 1) * SG_CHUNK, :]
        gsgb_ref[...] += sgb_sum
        dv1_rows = []
        for n in range(n_chunks):
            pairs = []
            for p in range(SG_W // LANES):
                rs_ = slice(n * SG_CHUNK, (n + 1) * SG_CHUNK)
                ls_ = slice(p * LANES, (p + 1) * LANES)
                dm = dmx_bf[rs_, ls_]
                xp = r["v1_bf"][rs_, ls_]
                low = _lane(dm.shape) < HEAD_DIM
                zero = jnp.zeros_like(dm)
                gsgw_ref[2 * p] += _dot_nt(jnp.where(low, dm, zero), xp)
                gsgw_ref[2 * p + 1] += _dot_nt(jnp.where(low, zero, dm), xp)
                pairs.append(jnp.where(low, _dot(sgwt_ref[2 * p], dm), _dot(sgwt_ref[2 * p + 1], dm)))
            dv1_rows.append(jnp.concatenate(pairs, axis=1))
        dv1 = dv1_rows[0] if n_chunks == 1 else jnp.concatenate(dv1_rows, axis=0)
        dvs = _ln_bwd(dv1 * sp_ref[0:1, :], r["xh_s"], r["rs_s"]) * r["dgv"]

        zrow = jnp.zeros((1, CONV_W), F32)
        gvec_ref[...] += jnp.concatenate([
            jnp.sum(dc0, axis=0, keepdims=True),
            jnp.sum(dc1 * r["xh_c"], axis=0, keepdims=True),
            jnp.sum(dc1, axis=0, keepdims=True),
            jnp.sum(dv1 * r["xh_s"], axis=0, keepdims=True),
            jnp.sum(dv1, axis=0, keepdims=True),
            zrow, zrow, zrow], axis=0)
        dgs_ref[...] = jnp.concatenate([dg_att, dg_conv, du, dvs, dg_sg], axis=1)

    row = lambda w: pl.BlockSpec((t, w), lambda i: (i, 0))
    heads = pl.BlockSpec((N_HEADS, t, LANES), lambda i: (0, i, 0))
    hprev, hnext = _halo_specs(t, s, CONV_W)
    return pl.pallas_call(
        body, name="bwd_out_scatter" if n_ex else "bwd_out", grid=(s // t,),
        in_specs=[row(D_MODEL), row(D_MODEL), row(D_IN), heads, row(CONV_W), hprev, hnext, row(CONV_W),
                  _const_spec((32, CONV_W)), _const_spec((8, CONV_W)), _const_spec((8, SG_W)),
                  _const_spec((4, SG_CHUNK, SG_CHUNK)), _const_spec((4, SG_CHUNK, SG_CHUNK)),
                  _const_spec((SG_CHUNK, SG_W)), _const_spec((D_MODEL, D_MODEL)), _const_spec((1, D_MODEL))] + ex_in,
        out_specs=[heads, row(1536), row(CONV_W), _const_spec((D_MODEL, D_MODEL)), _const_spec((1, D_MODEL)),
                   _const_spec((32, CONV_W)), _const_spec((8, CONV_W)), _const_spec((4, SG_CHUNK, SG_CHUNK)),
                   _const_spec((SG_CHUNK, SG_W))] + ex_out,
        out_shape=[jax.ShapeDtypeStruct((N_HEADS, s, LANES), BF16), jax.ShapeDtypeStruct((s, 1536), F32),
                   jax.ShapeDtypeStruct((s, CONV_W), F32), jax.ShapeDtypeStruct((D_MODEL, D_MODEL), F32),
                   jax.ShapeDtypeStruct((1, D_MODEL), F32), jax.ShapeDtypeStruct((32, CONV_W), F32),
                   jax.ShapeDtypeStruct((8, CONV_W), F32), jax.ShapeDtypeStruct((4, SG_CHUNK, SG_CHUNK), F32),
                   jax.ShapeDtypeStruct((SG_CHUNK, SG_W), F32)] + ex_shapes,
        scratch_shapes=[pltpu.VMEM((SUBLANES, t + 2 * HALO, CONV_W), F32)] + (ex_sems if n_ex else []),
        compiler_params=_arb(),
    )(dy, mix, proj, o, hc, hc, hc, c0, cw, cp, sp, sgw_bf, sgwt_bf, sgb, w_out_t_bf, g_post, *ex_args)


def _flash_bwd(qa, doa, k, v, tq, tk):
    s = qa.shape[1]
    rows = GROUP * tq
    nk = s // tk
    n_q = s // tq

    def body(qa_ref, do_ref, k_ref, v_ref, dq_ref, dk_hbm, dv_hbm,
             dq_scr, dk_scr, dv_scr, s0, s1, d0, d1, p0, p1, e0, e1, sems):
        j, i = pl.program_id(0), pl.program_id(1)
        s_bufs, d_bufs, p_bufs, e_bufs = (s0, s1), (d0, d1), (p0, p1), (e0, e1)
        qv = qa_ref[...].reshape(rows, LANES)
        dov = do_ref[...].reshape(rows, LANES)
        q_t = qv.astype(F32).T.astype(BF16)
        do_t = dov.astype(F32).T.astype(BF16)
        dq_scr[...] = jnp.zeros((rows, LANES), F32)

        @pl.when(i == 0)
        def _():
            dk_scr[...] = jnp.zeros_like(dk_scr)
            dv_scr[...] = jnp.zeros_like(dv_scr)

        def at(c):
            return _chunk_rows(c, tk)

        def scores(c, slot):
            s_bufs[slot][...] = _dot_nt(qv, k_ref[at(c), :])
            d_bufs[slot][...] = _dot_nt(dov, v_ref[at(c), :])

        def probs(slot):
            for h in range(GROUP):
                r = slice(h * tq, (h + 1) * tq)
                p = jnp.exp(s_bufs[slot][r, :])
                p_bufs[slot][r, :] = p.astype(BF16)
                e_bufs[slot][r, :] = (p * d_bufs[slot][r, :]).astype(BF16)

        def grads(c, slot):
            ds = e_bufs[slot][...]
            dq_scr[...] += _dot(ds, k_ref[at(c), :])
            dv_scr[c] += _dot(do_t, p_bufs[slot][...])
            dk_scr[c] += _dot(q_t, ds)

        p1[...] = jnp.zeros((rows, tk), BF16)
        e1[...] = jnp.zeros((rows, tk), BF16)
        _three_stage_pipeline(nk, 4, scores, probs, grads, peel=False)
        dq_ref[...] = dq_scr[...].reshape(GROUP, tq, LANES)

        @pl.when(i == n_q - 1)
        def _():
            out = [pltpu.make_async_copy(dk_scr, dk_hbm.at[j], sems.at[0]),
                   pltpu.make_async_copy(dv_scr, dv_hbm.at[j], sems.at[1])]
            for cp in out:
                cp.start()
            for cp in out:
                cp.wait()

    qspec = pl.BlockSpec((GROUP, tq, LANES), lambda j, i: (j, i, 0))
    kvspec = pl.BlockSpec((None, s, LANES), lambda j, i: (j, 0, 0))
    hbm = pl.BlockSpec(memory_space=pl.ANY)
    stage_f32 = pltpu.VMEM((rows, tk), F32)
    stage_bf = pltpu.VMEM((rows, tk), BF16)
    kv_t = jax.ShapeDtypeStruct((N_KV, nk, LANES, tk), F32)
    return pl.pallas_call(
        body, name="flash_bwd", grid=(N_KV, n_q),
        in_specs=[qspec, qspec, kvspec, kvspec],
        out_specs=[qspec, hbm, hbm],
        out_shape=[jax.ShapeDtypeStruct((N_HEADS, s, LANES), F32), kv_t, kv_t],
        scratch_shapes=[pltpu.VMEM((rows, LANES), F32), pltpu.VMEM((nk, LANES, tk), F32), pltpu.VMEM((nk, LANES, tk), F32),
                        stage_f32, stage_f32, stage_f32, stage_f32, stage_bf, stage_bf, stage_bf, stage_bf,
                        pltpu.SemaphoreType.DMA((2,))],
        compiler_params=_arb(2),
    )(qa, doa, k, v)


def _bwd_in(dy, x, proj, dq, dk, dv, dgs, dc0, cw, g_pre, w_in_t_bf, bd, qg, kg, cos, sin, t):
    s = x.shape[0]

    def body(dy_ref, x_ref, proj_ref, dq_ref, dk_ref, dv_ref, dgs_ref, dc_ref, dcp_ref, dcn_ref, cw_ref, g_ref,
             wt_ref, bd_ref, qg_ref, kg_ref, cos_ref, sin_ref,
             dx_ref, dproj_ref, h_ref, gpre_ref, gq_ref, gk_ref, dext_ref):
        i = pl.program_id(0)

        @pl.when(i == 0)
        def _():
            for ref in (gpre_ref, gq_ref, gk_ref):
                ref[...] = jnp.zeros_like(ref)

        proj = proj_ref[...]
        cos_pair = cos_ref[...]
        sin_pair = sin_ref[...]
        cosv = jnp.concatenate([cos_pair] * (N_HEADS // 2), axis=1)
        sinv = jnp.concatenate([sin_pair] * (N_HEADS // 2), axis=1)

        def head_norm_bwd(dr, z, bdm, g, cs, sn, gacc_ref):
            dn = dr * cs + _swap16(dr * sn)
            rr = lax.rsqrt(_group_mean(z * z, bdm) + EPS)
            gdn = dn * g
            gacc_ref[...] += jnp.sum(dn * z * rr, axis=0, keepdims=True)
            return rr * gdn - z * (rr * rr * rr * _group_mean(gdn * z, bdm))

        dq_cat = _heads_to_cat(dq_ref, N_HEADS) * ATT_SCALE
        dzq = head_norm_bwd(dq_cat, proj[:, C_Q:C_K], bd_ref[...], qg_ref[...], cosv, sinv, gq_ref)

        def kv_pair(ref):
            return jnp.concatenate([ref[0, 0:HEAD_DIM, :], ref[1, 0:HEAD_DIM, :]], axis=0).T

        dk_cat = kv_pair(dk_ref)
        dzk = head_norm_bwd(dk_cat, proj[:, C_K:C_V], bd_ref[0:LANES, 0:LANES], kg_ref[...],
                            cos_pair, sin_pair, gk_ref)
        dv_cat = kv_pair(dv_ref)

        _conv_window(dext_ref, dcp_ref, dc_ref[...], dcn_ref, i == 0, i == pl.num_programs(0) - 1, t)
        dhc = jnp.zeros((t, CONV_W), F32)
        for kk in range(CONV_K):
            dhc = dhc + cw_ref[kk:kk + 1, :] * _tap(dext_ref, CONV_K - kk, t)
        sg = _sigmoid(proj[:, C_A2:C_GC])
        da1 = dhc * sg
        da2 = dhc * proj[:, C_A1:C_A2] * sg * (1.0 - sg)

        dgs = dgs_ref[...]
        dproj_bf = jnp.concatenate([dzq, dzk, dv_cat, dgs[:, 0:512], da1, da2, dgs[:, 512:1536]], axis=1).astype(BF16)
        dproj_ref[...] = dproj_bf
        dh = _dot(dproj_bf, wt_ref[...])

        xv = x_ref[...]
        rr = _rms(xv)
        gv = g_ref[...]
        h_ref[...] = (xv * rr * gv).astype(BF16)
        gdh = dh * gv
        gpre_ref[...] += jnp.sum(dh * xv * rr, axis=0, keepdims=True)
        dx_ref[...] = dy_ref[...] + rr * gdh - xv * (rr * rr * rr * jnp.mean(gdh * xv, axis=-1, keepdims=True))

    row = lambda w: pl.BlockSpec((t, w), lambda i: (i, 0))
    heads = lambda n: pl.BlockSpec((n, t, LANES), lambda i: (0, i, 0))
    hprev, hnext = _halo_specs(t, s, CONV_W)
    tk = dk.shape[3]
    assert tk % t == 0
    kv_t = pl.BlockSpec((N_KV, None, LANES, t), lambda i: (0, i // (tk // t), 0, i % (tk // t)))
    return pl.pallas_call(
        body, name="bwd_in", grid=(s // t,),
        in_specs=[row(D_MODEL), row(D_MODEL), row(D_IN), heads(N_HEADS), kv_t, kv_t, row(1536),
                  row(CONV_W), hprev, hnext, _const_spec((32, CONV_W)), _const_spec((1, D_MODEL)),
                  _const_spec((D_IN, D_MODEL)), _const_spec((512, 512)), _const_spec((1, 512)),
                  _const_spec((1, LANES)), row(LANES), row(LANES)],
        out_specs=[row(D_MODEL), row(D_IN), row(D_MODEL), _const_spec((1, D_MODEL)), _const_spec((1, 512)),
                   _const_spec((1, LANES))],
        out_shape=[jax.ShapeDtypeStruct((s, D_MODEL), F32), jax.ShapeDtypeStruct((s, D_IN), BF16),
                   jax.ShapeDtypeStruct((s, D_MODEL), BF16), jax.ShapeDtypeStruct((1, D_MODEL), F32),
                   jax.ShapeDtypeStruct((1, 512), F32), jax.ShapeDtypeStruct((1, LANES), F32)],
        scratch_shapes=[pltpu.VMEM((SUBLANES, t + 2 * HALO, CONV_W), F32)],
        compiler_params=_arb(),
    )(dy, x, proj, dq, dk, dv, dgs, dc0, dc0, dc0, cw, g_pre, w_in_t_bf, bd, qg, kg, cos, sin)


def _grad_w_in(h_bf, dproj_bf, t):
    s = h_bf.shape[0]
    half = D_IN // 2

    def body(h_ref, d_ref, g_ref):
        @pl.when(pl.program_id(1) == 0)
        def _():
            g_ref[...] = jnp.zeros_like(g_ref)

        g_ref[...] += _dot_tn(h_ref[...], d_ref[...])

    return pl.pallas_call(
        body, name="grad_w_in", grid=(2, s // t),
        in_specs=[pl.BlockSpec((t, D_MODEL), lambda j, i: (i, 0)), pl.BlockSpec((t, half), lambda j, i: (i, j))],
        out_specs=pl.BlockSpec((D_MODEL, half), lambda j, i: (0, j)),
        out_shape=jax.ShapeDtypeStruct((D_MODEL, D_IN), F32),
        compiler_params=_arb(2),
    )(h_bf, dproj_bf)


def _place():
    x, y, c = lax.axis_index("x"), lax.axis_index("y"), lax.axis_index("c")
    chips = [(1 - x, y), (x, 1 - y), (1 - x, 1 - y)]
    return x, y, c, chips


def _any_specs(n):
    return [pl.BlockSpec(memory_space=pl.ANY)] * n


class _ChipExchange:
    def __init__(self, scatter, ins, outs, send_sems, recv_sems, local_sems):
        n = len(ins)
        x, y, c, chips = _place()
        mine = 2 * x + y
        src = (lambda a, piece: ins[a].at[piece]) if scatter else (lambda a, piece: ins[a])

        def copy(j, a, piece, slot, to):
            return pltpu.make_async_remote_copy(src_ref=src(a, piece), dst_ref=outs[a].at[slot],
                                                send_sem=send_sems.at[n * j + a], recv_sem=recv_sems.at[n * j + a],
                                                device_id=to, device_id_type=MESH)

        self.local = [pltpu.make_async_copy(src(a, mine), outs[a].at[mine], local_sems.at[a]) for a in range(n)]
        self.sends = [copy(j, a, 2 * px + py, mine, (px, py, c)) for j, (px, py) in enumerate(chips) for a in range(n)]
        self.arrivals = lambda: [copy(j, a, mine, 2 * px + py, (px, py, c))
                                 for j, (px, py) in enumerate(chips) for a in range(n)]

    def start(self):
        for cp in self.local + self.sends:
            cp.start()

    def finish(self):
        for cp in self.arrivals():
            cp.wait_recv()
        for cp in self.sends:
            cp.wait_send()
        for cp in self.local:
            cp.wait()

    @staticmethod
    def out_shapes(scatter, arrs):
        return [jax.ShapeDtypeStruct(a.shape if scatter else (N_CHIPS,) + a.shape, a.dtype) for a in arrs]

    @staticmethod
    def semaphores(n):
        return [pltpu.SemaphoreType.DMA((3 * n,)), pltpu.SemaphoreType.DMA((3 * n,)), pltpu.SemaphoreType.DMA((n,))]


def _exchange(scatter, arrs, name):
    n = len(arrs)

    def body(*refs):
        ex = _ChipExchange(scatter, refs[:n], refs[n:2 * n], *refs[2 * n:])
        ex.start()
        ex.finish()

    return pl.pallas_call(
        body, name=name, in_specs=_any_specs(n), out_specs=_any_specs(n),
        out_shape=_ChipExchange.out_shapes(scatter, arrs), scratch_shapes=_ChipExchange.semaphores(n),
    )(*arrs)


def _riding_exchange(scatter, arrs):
    n = len(arrs)

    def hook(ins, outs, sems):
        i = pl.program_id(0)

        @pl.when(i == 0)
        def _():
            _ChipExchange(scatter, ins, outs, *sems).start()

        @pl.when(i == pl.num_programs(0) - 1)
        def _():
            _ChipExchange(scatter, ins, outs, *sems).finish()

    return (tuple(arrs), _any_specs(n), _any_specs(n), _ChipExchange.out_shapes(scatter, arrs),
            _ChipExchange.semaphores(n), hook)


def _sum_chips(parts, rb, name):
    depth = len(parts)
    _, r, cdim = parts[0].shape

    def body(*refs):
        o_ref = refs[depth]
        for l in range(depth):
            def add(p_ref=refs[l]):
                part = lambda j: p_ref[j].astype(F32)
                o_ref[...] = ((part(0) + part(1)) + part(2)) + part(3)

            pl.when(pl.program_id(0) == l)(add)

    return pl.pallas_call(
        body, name=name, grid=(depth, r // rb),
        in_specs=[pl.BlockSpec((N_CHIPS, rb, cdim), lambda l, i: (0, i, 0))] * depth,
        out_specs=pl.BlockSpec((None, rb, cdim), lambda l, i: (l, i, 0)),
        out_shape=jax.ShapeDtypeStruct((depth, r, cdim), F32),
        compiler_params=_arb(2),
    )(*parts)


def _swap_with_sibling(a, b):
    arrs = (a, b)
    n = len(arrs)

    def body(*refs):
        ins, outs = refs[:n], refs[n:2 * n]
        send_sems, recv_sems = refs[2 * n:]
        x, y, c, _ = _place()
        cps = [pltpu.make_async_remote_copy(src_ref=ins[k], dst_ref=outs[k], send_sem=send_sems.at[k],
                                            recv_sem=recv_sems.at[k], device_id=(x, y, 1 - c), device_id_type=MESH)
               for k in range(n)]
        for cp in cps:
            cp.start()
        for cp in cps:
            cp.wait()

    return pl.pallas_call(
        body, name="swap_with_sibling",
        in_specs=_any_specs(n), out_specs=_any_specs(n),
        out_shape=[jax.ShapeDtypeStruct(v.shape, v.dtype) for v in arrs],
        scratch_shapes=[pltpu.SemaphoreType.DMA((n,)), pltpu.SemaphoreType.DMA((n,))],
    )(*arrs)


def _allreduce_small(slab):
    m, n = slab.shape

    def body(x_ref, out_ref, gath, send_sems, recv_sems, local_sem):
        x, y, c, chips = _place()
        me, sibling = (x, y, c), (x, y, 1 - c)

        def rows(px, py, pc):
            return gath.at[pl.ds(pl.multiple_of((4 * px + 2 * py + pc) * m, 8), m), :]

        def copy(k, block, to, src=None):
            return pltpu.make_async_remote_copy(src_ref=rows(*block) if src is None else src, dst_ref=rows(*block),
                                                send_sem=send_sems.at[k], recv_sem=recv_sems.at[k],
                                                device_id=to, device_id_type=MESH)

        mine = pltpu.make_async_copy(x_ref, rows(*me), local_sem)
        mine.start()
        first = [copy(0, me, sibling, src=x_ref)]
        first += [copy(1 + j, me, (*chip, c), src=x_ref) for j, chip in enumerate(chips)]
        for cp in first:
            cp.start()
        passed = [copy(4 + j, (*chip, c), sibling) for j, chip in enumerate(chips)]
        for j, chip in enumerate(chips):
            copy(1 + j, (*chip, c), me).wait_recv()
            passed[j].start()
        copy(0, sibling, me).wait_recv()
        for j, chip in enumerate(chips):
            copy(4 + j, (*chip, 1 - c), me).wait_recv()
        for cp in first + passed:
            cp.wait_send()
        mine.wait()
        total = gath[0:m, :]
        for d in range(1, N_DEV):
            total = total + gath[d * m:(d + 1) * m, :]
        out_ref[...] = total

    return pl.pallas_call(
        body, name="allreduce_small",
        in_specs=[pl.BlockSpec(memory_space=pltpu.VMEM)],
        out_specs=pl.BlockSpec(memory_space=pltpu.VMEM),
        out_shape=jax.ShapeDtypeStruct((m, n), F32),
        scratch_shapes=[pltpu.VMEM((N_DEV * m, n), F32), pltpu.SemaphoreType.DMA((7,)), pltpu.SemaphoreType.DMA((7,)),
                        pltpu.SemaphoreType.DMA],
    )(slab)


def _adamw(w, ga, gb, m, v, rb, name):
    depth, r, cdim = w.shape

    def body(w_ref, ga_ref, gb_ref, m_ref, v_ref, g_out, d_out, m_out, v_out):
        g = ga_ref[...] + gb_ref[...]
        m2 = ADAM_B1 * m_ref[...] + (1.0 - ADAM_B1) * g
        v2 = ADAM_B2 * v_ref[...] + (1.0 - ADAM_B2) * (g * g)
        m_hat = m2 / (1.0 - ADAM_B1 ** ADAM_STEP)
        v_hat = v2 / (1.0 - ADAM_B2 ** ADAM_STEP)
        g_out[...] = g
        d_out[...] = -ADAM_LR * (m_hat / (jnp.sqrt(v_hat) + ADAM_EPS) + ADAM_WD * w_ref[...])
        m_out[...] = m2
        v_out[...] = v2

    spec = pl.BlockSpec((None, rb, cdim), lambda l, i: (l, i, 0))
    shp = jax.ShapeDtypeStruct((depth, r, cdim), F32)
    return pl.pallas_call(
        body, name=name, grid=(depth, r // rb),
        in_specs=[spec] * 5, out_specs=[spec] * 4, out_shape=[shp] * 4,
        compiler_params=_arb(2),
    )(w, ga, gb, m, v)


def _rope_tables(s):
    t = jnp.arange(s, dtype=jnp.int32)
    row = (t // GRID_W).astype(F32)
    col = (t % GRID_W).astype(F32)
    half = HEAD_DIM // 4
    inv_freq = ROPE_THETA ** (-jnp.arange(half, dtype=F32) / half)
    ar = row[:, None] * inv_freq[None, :]
    ac = col[:, None] * inv_freq[None, :]
    cos = jnp.concatenate([jnp.cos(ar), jnp.cos(ar), jnp.cos(ac), jnp.cos(ac)], axis=1)
    sin = jnp.concatenate([-jnp.sin(ar), jnp.sin(ar), -jnp.sin(ac), jnp.sin(ac)], axis=1)
    return jnp.tile(cos, (1, 2)), jnp.tile(sin, (1, 2))


def _pad_rows(a, rows):
    return jnp.pad(a, ((0, rows - a.shape[0]),) + ((0, 0),) * (a.ndim - 1))


_SMALL = ("pre_norm", "post_norm", "q_norm", "k_norm", "conv_dw_b", "conv_ln_g", "conv_ln_b", "sg_ln_g", "sg_ln_b",
          "sg_w", "sg_b")


def _pack(parts):
    flat = jnp.concatenate([p.reshape(-1, LANES) for p in parts], axis=0)
    return _pad_rows(flat, -(-flat.shape[0] // 8) * 8)


def _unpack(slab, shapes):
    out, r = [], 0
    for shp in shapes:
        n = 1
        for d in shp:
            n *= d
        out.append(slab[r:r + n // LANES].reshape(shp))
        r += n // LANES
    return out


def kernel(x, pre_norm, post_norm, w_in, w_out, q_norm, k_norm, conv_dw, conv_dw_b, conv_ln_g, conv_ln_b, sg_ln_g, sg_ln_b, sg_w, sg_b, loss_target, m_pre_norm, m_post_norm, m_w_in, m_w_out, m_q_norm, m_k_norm, m_conv_dw, m_conv_dw_b, m_conv_ln_g, m_conv_ln_b, m_sg_ln_g, m_sg_ln_b, m_sg_w, m_sg_b, v_pre_norm, v_post_norm, v_w_in, v_w_out, v_q_norm, v_k_norm, v_conv_dw, v_conv_dw_b, v_conv_ln_g, v_conv_ln_b, v_sg_ln_g, v_sg_ln_b, v_sg_w, v_sg_b):
    depth = w_in.shape[0]
    s = x.shape[1]
    assert x.shape[0] == 1 and s % SG_CHUNK == 0 and x.shape[2] == D_MODEL
    tq = min(256, s)
    tq_fwd = min(512, s)
    tk = min(512, s // 2)
    t = min(256, tk)
    shard_cols = w_in.shape[2]
    chip = 2 * lax.axis_index("x") + lax.axis_index("y")

    w_in_sh, w_out_sh = w_in.astype(BF16), w_out.astype(BF16)
    whole_w_in = lambda g: jnp.concatenate([g[j] for j in range(N_CHIPS)], axis=2)
    w_in_bf = [whole_w_in(_exchange(False, (w_in_sh[0:1],), "gather_w_in_first")[0])[0]]
    riders = ((w_in_sh[1:],) if depth > 1 else ()) + (w_out_sh, conv_dw)
    sgw_bf = sg_w.astype(BF16)
    sgwt_bf = jnp.swapaxes(sg_w, 2, 3).astype(BF16)

    cos, sin = _rope_tables(s)
    bd = jnp.kron(jnp.eye(N_HEADS, dtype=F32), jnp.full((HEAD_DIM, HEAD_DIM), 1.0 / HEAD_DIM, F32)).astype(BF16)

    def head_gains(l):
        return jnp.tile(q_norm[l], N_HEADS)[None, :], jnp.tile(k_norm[l], N_KV)[None, :]

    def layer_consts(l):
        cw = _pad_rows(cdw_full[l], 32)
        cp = _pad_rows(jnp.stack([conv_dw_b[l], conv_ln_g[l], conv_ln_b[l]]), 8)
        sp = _pad_rows(jnp.stack([sg_ln_g[l], sg_ln_b[l]]), 8)
        sgb = jnp.repeat(sg_b[l].T, HEAD_DIM, axis=1)
        return cw, cp, sp, sgb

    xs = [x[0]]
    saved = []
    for l in range(depth):
        qg, kg = head_gains(l)
        outs = _fwd_in(xs[l], pre_norm[l][None, :], w_in_bf[l], bd, qg, kg, cos, sin, t, gather=riders if l == 0 else ())
        proj, hc, q, k, v = outs[:5]
        if l == 0:
            gathered = list(outs[5:])
            if depth > 1:
                w_in_bf += list(whole_w_in(gathered.pop(0)))
            w_out_bf = jnp.concatenate([gathered[0][j] for j in range(N_CHIPS)], axis=1)
            cdw_full = jnp.concatenate([gathered[1][j] for j in range(N_CHIPS)], axis=2)
            w_out_t_bf = jnp.swapaxes(w_out_bf, 1, 2)
        cw, cp, sp, sgb = layer_consts(l)
        o, qa = _flash_fwd(q, k, v, tq_fwd, tk)
        outs = _fwd_out(xs[l], proj, o, hc, cw, cp, sp, sgw_bf[l], sgb, w_out_bf[l], post_norm[l][None, :], t,
                        target=loss_target[0] if l == depth - 1 else None)
        saved.append((proj, hc, qa, k, v, o, outs[0], outs[2]))
        if l < depth - 1:
            xs.append(outs[1])
    dy, sq = outs[1], outs[3]
    loss = lax.psum(0.5 * jnp.sum(sq) / D_MODEL, ("x", "y", "c"))

    g_small = {n: [] for n in _SMALL + ("conv_dw",)}
    received = [None] * depth
    pieces = ()
    for l in reversed(range(depth)):
        cw, cp, sp, sgb = layer_consts(l)
        qg, kg = head_gains(l)
        proj, hc, qa, k, v, o, mix, c0 = saved[l]
        outs = _bwd_out(dy, mix, proj, o, hc, c0, cw, cp, sp, sgw_bf[l], sgwt_bf[l], sgb, w_out_t_bf[l],
                        post_norm[l][None, :], t, scatter=pieces)
        doa, dgs, dc0, gwo, gpost, gcw, gvec, gsgw, gsgb = outs[:9]
        if pieces:
            received[l + 1] = outs[9:]
        dq, dk, dv = _flash_bwd(qa, doa, k, v, tq, tk)
        dy, dproj_bf, h_bf, gpre, gq, gk = _bwd_in(dy, xs[l], proj, dq, dk, dv, dgs, dc0, cw, pre_norm[l][None, :],
                                                  jnp.swapaxes(w_in_bf[l], 0, 1), bd, qg, kg, cos, sin, t)
        gwi = _grad_w_in(h_bf, dproj_bf, min(512, s))
        pieces = (jnp.stack([gwi[:, j * shard_cols:(j + 1) * shard_cols] for j in range(N_CHIPS)]).astype(BF16),
                  gwo.reshape(N_CHIPS, gwo.shape[0] // N_CHIPS, gwo.shape[1]).astype(BF16))
        g_small["pre_norm"].append(gpre[0])
        g_small["post_norm"].append(gpost[0])
        g_small["q_norm"].append(gq[0].reshape(N_HEADS, HEAD_DIM).sum(0))
        g_small["k_norm"].append(gk[0].reshape(N_KV, HEAD_DIM).sum(0))
        g_small["conv_dw"].append(gcw[:CONV_K])
        g_small["conv_dw_b"].append(gvec[0])
        g_small["conv_ln_g"].append(gvec[1])
        g_small["conv_ln_b"].append(gvec[2])
        g_small["sg_ln_g"].append(gvec[3])
        g_small["sg_ln_b"].append(gvec[4])
        g_small["sg_w"].append(gsgw)
        g_small["sg_b"].append(gsgb.reshape(SG_CHUNK, SG_W // HEAD_DIM, HEAD_DIM).sum(-1).T)
    grad_x = dy[None]
    g_small = {n: jnp.stack(vals[::-1]) for n, vals in g_small.items()}

    received[0] = _exchange(True, pieces, "scatter_grads")
    s_in = _sum_chips([r[0] for r in received], 256, "sum_chips_w_in")
    s_out = _sum_chips([r[1] for r in received], 256, "sum_chips_w_out")
    t_in, t_out = _swap_with_sibling(s_in, s_out)
    grad_w_in, delta_w_in, new_m_w_in, new_v_w_in = _adamw(w_in, s_in, t_in, m_w_in, v_w_in, 256, "adamw_w_in")
    grad_w_out, delta_w_out, new_m_w_out, new_v_w_out = _adamw(w_out, s_out, t_out, m_w_out, v_w_out, 256, "adamw_w_out")

    small_w = dict(pre_norm=pre_norm, post_norm=post_norm, q_norm=q_norm, k_norm=k_norm, conv_dw_b=conv_dw_b,
                   conv_ln_g=conv_ln_g, conv_ln_b=conv_ln_b, sg_ln_g=sg_ln_g, sg_ln_b=sg_ln_b, sg_w=sg_w, sg_b=sg_b)
    small_m = dict(pre_norm=m_pre_norm, post_norm=m_post_norm, q_norm=m_q_norm, k_norm=m_k_norm, conv_dw_b=m_conv_dw_b,
                   conv_ln_g=m_conv_ln_g, conv_ln_b=m_conv_ln_b, sg_ln_g=m_sg_ln_g, sg_ln_b=m_sg_ln_b, sg_w=m_sg_w,
                   sg_b=m_sg_b)
    small_v = dict(pre_norm=v_pre_norm, post_norm=v_post_norm, q_norm=v_q_norm, k_norm=v_k_norm, conv_dw_b=v_conv_dw_b,
                   conv_ln_g=v_conv_ln_g, conv_ln_b=v_conv_ln_b, sg_ln_g=v_sg_ln_g, sg_ln_b=v_sg_ln_b, sg_w=v_sg_w,
                   sg_b=v_sg_b)
    shapes = [small_w[n].shape for n in _SMALL]
    red = _allreduce_small(_pack([g_small[n] for n in _SMALL] + [g_small["conv_dw"]]))
    n_rep = sum(small_w[n].size for n in _SMALL) // LANES
    g_cdw_full = red[n_rep:n_rep + g_small["conv_dw"].size // LANES].reshape(g_small["conv_dw"].shape)
    cdw_cols = conv_dw.shape[2]
    g_cdw = lax.dynamic_slice_in_dim(g_cdw_full, chip * cdw_cols, cdw_cols, axis=2)
    g_slab = _pack([red[:n_rep], g_cdw])
    w_slab = _pack([small_w[n] for n in _SMALL] + [conv_dw])
    m_slab = _pack([small_m[n] for n in _SMALL] + [m_conv_dw])
    v_slab = _pack([small_v[n] for n in _SMALL] + [v_conv_dw])
    rows = w_slab.shape[0]
    outs = _adamw(w_slab[None], g_slab[None], jnp.zeros_like(g_slab)[None], m_slab[None], v_slab[None], rows, "adamw_small")
    unpacked = [dict(zip(_SMALL + ("conv_dw",), _unpack(o_[0], shapes + [conv_dw.shape]))) for o_ in outs]

    big = [dict(w_in=a, w_out=b) for a, b in ((grad_w_in, grad_w_out), (delta_w_in, delta_w_out),
                                             (new_m_w_in, new_m_w_out), (new_v_w_in, new_v_w_out))]
    order = ("pre_norm", "post_norm", "w_in", "w_out", "q_norm", "k_norm", "conv_dw", "conv_dw_b", "conv_ln_g",
             "conv_ln_b", "sg_ln_g", "sg_ln_b", "sg_w", "sg_b")
    result = [loss, grad_x]
    for kind in range(4):
        for name in order:
            result.append(big[kind][name] if name in big[kind] else unpacked[kind][name])
    return tuple(result)
```

```python
import jax
import jax.numpy as jnp
from jax import lax
from jax.experimental import pallas as pl
from jax.experimental.pallas import tpu as pltpu

F32 = jnp.float32
BF16 = jnp.bfloat16
MESH = pl.DeviceIdType.MESH

EPS = 1e-6
D_MODEL = 1024
D_IN = 2816
HEAD_DIM = 64
LANES = 128
SUBLANES = 8
N_HEADS = 8
N_KV = 2
GROUP = N_HEADS // N_KV
GRID_W = 64
ROPE_THETA = 10000.0
CONV_K = 31
CONV_W = 256
SG_W = 256
SG_CHUNK = 128
KEPT_W = CONV_W + 2 * SG_W
HALO = 16
ATT_SCALE = HEAD_DIM ** -0.5

ATT_W = N_HEADS * HEAD_DIM
C_Q, C_K, C_V, C_GA, C_A1, C_A2, C_GC, C_U, C_VS, C_GS = 0, 512, 640, 768, 1280, 1536, 1792, 2048, 2304, 2560
GATES_W = D_IN - C_GA - 2 * CONV_W

ADAM_LR = 0.001
ADAM_B1 = 0.9
ADAM_B2 = 0.999
ADAM_EPS = 1e-08
ADAM_WD = 0.01
ADAM_STEP = 10

N_CHIPS = 4
N_DEV = 8


def _dot(a, b):
    return jnp.dot(a, b, preferred_element_type=F32)


def _group_mean(x, bd_bf):
    hi = x.astype(BF16)
    lo = (x - hi.astype(F32)).astype(BF16)
    return _dot(hi, bd_bf) + _dot(lo, bd_bf)


def _dot_nt(a, b):
    return lax.dot_general(a, b, (((1,), (1,)), ((), ())), preferred_element_type=F32)


def _dot_tn(a, b):
    return lax.dot_general(a, b, (((0,), (0,)), ((), ())), preferred_element_type=F32)


def _lane(shape):
    return lax.broadcasted_iota(jnp.int32, shape, 1)


def _sigmoid(x):
    return 1.0 / (1.0 + jnp.exp(-x))


def _silu_fwd_bwd(x):
    s = _sigmoid(x)
    return x * s, s * (1.0 + x * (1.0 - s))


def _erf(x):
    x = jnp.clip(x, -4.0, 4.0)
    x2 = x * x
    a = -2.72614225801306e-10
    a = a * x2 + 2.77068142495902e-08
    a = a * x2 + -2.10102402082508e-06
    a = a * x2 + -5.69250639462346e-05
    a = a * x2 + -7.34990630326855e-04
    a = a * x2 + -2.95459980854025e-03
    a = a * x2 + -1.60960333262415e-02
    b = -1.45660718464996e-05
    b = b * x2 + -2.13374055278905e-04
    b = b * x2 + -1.68282697438203e-03
    b = b * x2 + -7.37332916720468e-03
    b = b * x2 + -1.42647390514189e-02
    return x * a / b


def _gelu_fwd_bwd(x, cdf=None):
    if cdf is None:
        cdf = 0.5 * (1.0 + _erf(x * 0.7071067811865476))
    pdf = jnp.exp(-0.5 * x * x) * 0.3989422804014327
    return x * cdf, cdf + x * pdf, cdf


def _rms(x):
    return lax.rsqrt(jnp.mean(x * x, axis=-1, keepdims=True) + EPS)


def _ln_hat(x):
    mu = jnp.mean(x, axis=-1, keepdims=True)
    xc = x - mu
    rs = lax.rsqrt(jnp.mean(xc * xc, axis=-1, keepdims=True) + EPS)
    return xc * rs, rs


def _ln_bwd(dxh, xh, rs):
    return rs * (dxh - jnp.mean(dxh, axis=-1, keepdims=True) - xh * jnp.mean(dxh * xh, axis=-1, keepdims=True))


def _swap16(z):
    parts = []
    for i in range(z.shape[1] // LANES):
        blk = z[:, i * LANES:(i + 1) * LANES]
        lane = _lane(blk.shape)
        parts.append(jnp.where((lane & 16) == 0, pltpu.roll(blk, LANES - 16, 1), pltpu.roll(blk, 16, 1)))
    return parts[0] if len(parts) == 1 else jnp.concatenate(parts, axis=1)


def _head_slab(pair, odd):
    src = pltpu.roll(pair, HEAD_DIM, 1) if odd else pair
    return jnp.where(_lane(pair.shape) < HEAD_DIM, src, 0.0)


def _pair_merge(even, odd):
    return jnp.where(_lane(even.shape) < HEAD_DIM, even, pltpu.roll(odd, HEAD_DIM, 1))


def _heads_to_cat(ref, n_heads):
    pairs = [_pair_merge(ref[2 * p], ref[2 * p + 1]) for p in range(n_heads // 2)]
    return pairs[0] if len(pairs) == 1 else jnp.concatenate(pairs, axis=1)


def _split3(x):
    hi = x.astype(BF16).astype(F32)
    r = x - hi
    mid = r.astype(BF16).astype(F32)
    lo = (r - mid).astype(BF16).astype(F32)
    return hi, mid, lo


def _with_spare(slab, hi, mid, lo):
    lane = _lane(slab.shape)
    return jnp.where(lane == HEAD_DIM, hi, jnp.where(lane == HEAD_DIM + 1, mid, jnp.where(lane == HEAD_DIM + 2, lo, slab)))


def _with_ones(slab):
    lane = _lane(slab.shape)
    return jnp.where((lane >= HEAD_DIM) & (lane < HEAD_DIM + 3), 1.0, slab)


def _conv_window(rot_ref, prev_ref, main, next_ref, first, last, t):
    n = t + 2 * HALO
    full = jnp.concatenate([jnp.where(first, 0.0, prev_ref[...]), main, jnp.where(last, 0.0, next_ref[...])], axis=0)
    rot_ref[0] = full
    for b in range(1, SUBLANES):
        rot_ref[b] = pltpu.roll(full, n - b, 0)


def _tap(rot_ref, start, t):
    a, b = divmod(start, SUBLANES)
    return rot_ref[b, SUBLANES * a:SUBLANES * a + t, :]


def _sgu_mix(v1_bf, w_ref, n_chunks):
    rows = []
    for n in range(n_chunks):
        pairs = []
        for p in range(SG_W // LANES):
            xp = v1_bf[n * SG_CHUNK:(n + 1) * SG_CHUNK, p * LANES:(p + 1) * LANES]
            me = _dot(w_ref[2 * p], xp)
            mo = _dot(w_ref[2 * p + 1], xp)
            pairs.append(jnp.where(_lane(me.shape) < HEAD_DIM, me, mo))
        rows.append(jnp.concatenate(pairs, axis=1))
    return rows[0] if len(rows) == 1 else jnp.concatenate(rows, axis=0)


def _halo_specs(t, s, width):
    per = t // HALO
    nblk = s // HALO
    prev = pl.BlockSpec((HALO, width), lambda i: (jnp.maximum(i * per - 1, 0), 0))
    nxt = pl.BlockSpec((HALO, width), lambda i: (jnp.minimum((i + 1) * per, nblk - 1), 0))
    return prev, nxt


def _const_spec(shape):
    nd = len(shape)
    return pl.BlockSpec(shape, lambda i: (0,) * nd)


def _arb(n=1):
    return pltpu.CompilerParams(dimension_semantics=("arbitrary",) * n)


def _fwd_in(x, g_pre, w_in_bf, bd, qg, kg, cos, sin, t, gather=()):
    s = x.shape[0]
    ex_args, ex_in, ex_out, ex_shapes, ex_sems, ex_hook = _riding_exchange(False, gather)
    n_ex = len(gather)

    def body(*refs):
        x_ref, g_ref, w_ref, bd_ref, qg_ref, kg_ref, cos_ref, sin_ref = refs[:8]
        proj_ref, hc_ref, q_ref, k_ref, v_ref = refs[8 + n_ex:13 + n_ex]
        if n_ex:
            ex_hook(refs[8:8 + n_ex], refs[13 + n_ex:13 + 2 * n_ex], refs[13 + 2 * n_ex:])
        xv = x_ref[...]
        h = (xv * _rms(xv) * g_ref[...]).astype(BF16)
        proj = _dot(h, w_ref[...])
        proj_ref[...] = proj
        cos_pair = cos_ref[...]
        sin_pair = sin_ref[...]
        cosv = jnp.concatenate([cos_pair] * (N_HEADS // 2), axis=1)
        sinv = jnp.concatenate([sin_pair] * (N_HEADS // 2), axis=1)
        q = proj[:, C_Q:C_K]
        qn = q * lax.rsqrt(_group_mean(q * q, bd_ref[...]) + EPS) * qg_ref[...]
        qr = (qn * cosv + _swap16(qn) * sinv) * ATT_SCALE
        for hh in range(N_HEADS):
            pair = qr[:, (hh // 2) * LANES:(hh // 2 + 1) * LANES]
            q_ref[hh] = _head_slab(pair, hh % 2 == 1).astype(BF16)
        k = proj[:, C_K:C_V]
        kn = k * lax.rsqrt(_group_mean(k * k, bd_ref[0:LANES, 0:LANES]) + EPS) * kg_ref[...]
        kr = kn * cos_pair + _swap16(kn) * sin_pair
        vv = proj[:, C_V:C_GA]
        for hh in range(N_KV):
            k_ref[hh] = _with_ones(_head_slab(kr, hh == 1)).astype(BF16)
            v_ref[hh] = _with_ones(_head_slab(vv, hh == 1)).astype(BF16)
        hc_ref[...] = proj[:, C_A1:C_A2] * _sigmoid(proj[:, C_A2:C_GC])

    row = lambda w: pl.BlockSpec((t, w), lambda i: (i, 0))
    heads = lambda n: pl.BlockSpec((n, t, LANES), lambda i: (0, i, 0))
    return pl.pallas_call(
        body, name="fwd_in_gather" if n_ex else "fwd_in", grid=(s // t,),
        in_specs=[row(D_MODEL), _const_spec((1, D_MODEL)), _const_spec((D_MODEL, D_IN)), _const_spec((ATT_W, ATT_W)),
                  _const_spec((1, ATT_W)), _const_spec((1, LANES)), row(LANES), row(LANES)] + ex_in,
        out_specs=[row(D_IN), row(CONV_W), heads(N_HEADS), heads(N_KV), heads(N_KV)] + ex_out,
        out_shape=[jax.ShapeDtypeStruct((s, D_IN), F32), jax.ShapeDtypeStruct((s, CONV_W), F32),
                   jax.ShapeDtypeStruct((N_HEADS, s, LANES), BF16), jax.ShapeDtypeStruct((N_KV, s, LANES), BF16),
                   jax.ShapeDtypeStruct((N_KV, s, LANES), BF16)] + ex_shapes,
        scratch_shapes=ex_sems if n_ex else [],
        compiler_params=_arb(),
    )(x, g_pre, w_in_bf, bd, qg, kg, cos, sin, *ex_args)


def _chunk_rows(c, tk):
    return pl.ds(c * tk, tk) if isinstance(c, int) else pl.ds(pl.multiple_of(c * tk, tk), tk)


def _three_stage_pipeline(nk, per_trip, stage1, stage2, stage3, peel):
    assert nk % 2 == 0 and per_trip % 2 == 0

    def step(t, parity, first=False, last=False):
        if not last:
            stage1(t + 1, 1 - parity)
        stage2(parity)
        if not first:
            stage3(t - 1, 1 - parity)

    stage1(0, 0)
    if not peel:
        while nk % per_trip:
            per_trip //= 2

        def whole_trip(i, carry):
            for u in range(per_trip):
                c = per_trip * i + u
                stage1(jnp.minimum(c + 1, nk - 1), 1 - u % 2)
                stage2(u % 2)
                stage3(jnp.maximum(c - 1, 0), 1 - u % 2)
            return carry

        lax.fori_loop(0, nk // per_trip, whole_trip, 0)
        stage3(nk - 1, 1)
        return

    step(0, 0, first=True)
    n_trips, left = divmod(nk - 2, per_trip)

    def trip(i, carry):
        for u in range(per_trip):
            step(1 + per_trip * i + u, (1 + u) % 2)
        return carry

    if n_trips:
        lax.fori_loop(0, n_trips, trip, 0)
    for t in range(1 + n_trips * per_trip, 1 + n_trips * per_trip + left):
        step(t, t % 2)
    step(nk - 1, 1, last=True)
    stage3(nk - 1, 1)


def _flash_fwd(q, k, v, tq, tk):
    s = q.shape[1]
    rows = GROUP * tq
    nk = s // tk

    def body(q_ref, k_ref, vt_ref, o_ref, qa_ref, m_scr, acc_scr, s0, s1, p0, p1, a0, a1):
        s_bufs, p_bufs, a_bufs = (s0, s1), (p0, p1), (a0, a1)
        qv = q_ref[...].reshape(rows, LANES)
        q_t = qv.astype(F32).T
        q_t_bf = q_t.astype(BF16)
        m_scr[...] = jnp.full((1, rows), -jnp.inf, F32)
        acc_scr[...] = jnp.zeros((LANES, rows), F32)

        def scores(c, slot):
            s_bufs[slot][...] = _dot(k_ref[_chunk_rows(c, tk), :], q_t_bf)

        def softmax(slot):
            for h in range(GROUP):
                r = slice(h * tq, (h + 1) * tq)
                sc = s_bufs[slot][:, r]
                m_prev = m_scr[:, r]
                m_new = jnp.maximum(m_prev, jnp.max(sc, axis=0, keepdims=True))
                p_bufs[slot][:, r] = jnp.exp((sc - m_new).astype(BF16))
                a_bufs[slot][:, r] = jnp.exp(m_prev - m_new)
                m_scr[:, r] = m_new

        def weighted_values(c, slot):
            acc_scr[...] = a_bufs[slot][...] * acc_scr[...] + _dot(vt_ref[c], p_bufs[slot][...])

        _three_stage_pipeline(nk, 4, scores, softmax, weighted_values, peel=True)

        acc = acc_scr[...]
        row = lax.broadcasted_iota(jnp.int32, acc.shape, 0)
        l = jnp.sum(jnp.where(row == HEAD_DIM, acc, 0.0), axis=0, keepdims=True)
        o_ref[...] = jnp.where(row < HEAD_DIM, acc / l, 0.0).T.reshape(GROUP, tq, LANES)
        hi, mid, lo = _split3(-(m_scr[...] + jnp.log(l)))
        qa_t = jnp.where(row == HEAD_DIM, hi, jnp.where(row == HEAD_DIM + 1, mid,
                                                        jnp.where(row == HEAD_DIM + 2, lo, q_t)))
        qa_ref[...] = qa_t.T.astype(BF16).reshape(GROUP, tq, LANES)

    qspec = pl.BlockSpec((GROUP, tq, LANES), lambda j, i: (j, i, 0))
    kspec = pl.BlockSpec((None, s, LANES), lambda j, i: (j, 0, 0))
    vtspec = pl.BlockSpec((None, nk, LANES, tk), lambda j, i: (j, 0, 0, 0))
    v_t = jnp.swapaxes(v.reshape(N_KV, nk, tk, LANES), 2, 3)
    return pl.pallas_call(
        body, name="flash_fwd", grid=(N_KV, s // tq),
        in_specs=[qspec, kspec, vtspec],
        out_specs=[qspec, qspec],
        out_shape=[jax.ShapeDtypeStruct((N_HEADS, s, LANES), F32), jax.ShapeDtypeStruct((N_HEADS, s, LANES), BF16)],
        scratch_shapes=[pltpu.VMEM((1, rows), F32), pltpu.VMEM((LANES, rows), F32),
                        pltpu.VMEM((tk, rows), F32), pltpu.VMEM((tk, rows), F32),
                        pltpu.VMEM((tk, rows), BF16), pltpu.VMEM((tk, rows), BF16),
                        pltpu.VMEM((1, rows), F32), pltpu.VMEM((1, rows), F32)],
        compiler_params=_arb(2),
    )(q, k, v_t)


def _groups_fwd(proj_ref, o_ref, hext_ref, cw_ref, cp_ref, sp_ref, sgw_ref, sgb_ref, t, kept=None):
    proj = proj_ref[...]
    r = {}
    r["att"] = _heads_to_cat(o_ref, N_HEADS)
    r["gate_a"], r["dgate_a"] = _silu_fwd_bwd(proj[:, C_GA:C_A1])
    r["att_g"] = r["att"] * r["gate_a"]
    if kept is None:
        c0 = jnp.zeros((t, CONV_W), F32) + cp_ref[0:1, :]
        for kk in range(CONV_K):
            c0 = c0 + cw_ref[kk:kk + 1, :] * _tap(hext_ref, kk + 1, t)
        cdf_u = cdf_v = None
    else:
        c0, cdf_u, cdf_v = kept[:, 0:CONV_W], kept[:, CONV_W:CONV_W + SG_W], kept[:, CONV_W + SG_W:KEPT_W]
    r["xh_c"], r["rs_c"] = _ln_hat(c0)
    r["c1"] = r["xh_c"] * cp_ref[1:2, :] + cp_ref[2:3, :]
    r["sg_c1"] = _sigmoid(r["c1"])
    r["c2"] = r["c1"] * r["sg_c1"]
    r["gate_c"], r["dgate_c"] = _silu_fwd_bwd(proj[:, C_GC:C_U])
    r["cnv_g"] = r["c2"] * r["gate_c"]
    r["gu"], r["dgu"], cdf_u = _gelu_fwd_bwd(proj[:, C_U:C_VS], cdf_u)
    gv, r["dgv"], cdf_v = _gelu_fwd_bwd(proj[:, C_VS:C_GS], cdf_v)
    r["kept"] = jnp.concatenate([c0, cdf_u, cdf_v], axis=1)
    r["xh_s"], r["rs_s"] = _ln_hat(gv)
    v1 = r["xh_s"] * sp_ref[0:1, :] + sp_ref[1:2, :]
    r["v1_bf"] = v1.astype(BF16)
    r["mixed"] = _sgu_mix(r["v1_bf"], sgw_ref, t // SG_CHUNK) + jnp.concatenate([sgb_ref[...]] * (t // SG_CHUNK), axis=0)
    r["um"] = r["gu"] * r["mixed"]
    r["gate_s"], r["dgate_s"] = _silu_fwd_bwd(proj[:, C_GS:D_IN])
    r["sgu_g"] = r["um"] * r["gate_s"]
    r["mc_bf"] = jnp.concatenate([r["att_g"], r["cnv_g"], r["sgu_g"]], axis=1).astype(BF16)
    return r


def _fwd_out(x, proj, o, hc, cw, cp, sp, sgw_bf, sgb, w_out_bf, g_post, t, target=None):
    s = x.shape[0]
    last_layer = target is not None

    def body(*refs):
        (x_ref, proj_ref, o_ref, hc_ref, hp_ref, hn_ref, cw_ref, cp_ref, sp_ref, sgw_ref, sgb_ref,
         w_ref, g_ref) = refs[:13]
        rest = refs[13:]
        if last_layer:
            t_ref, mix_ref, out_ref, kept_ref, sq_ref, hext_ref = rest
        else:
            mix_ref, out_ref, kept_ref, hext_ref = rest
        i = pl.program_id(0)
        _conv_window(hext_ref, hp_ref, hc_ref[...], hn_ref, i == 0, i == pl.num_programs(0) - 1, t)
        r = _groups_fwd(proj_ref, o_ref, hext_ref, cw_ref, cp_ref, sp_ref, sgw_ref, sgb_ref, t)
        kept_ref[...] = r["kept"]
        mix = _dot(r["mc_bf"], w_ref[...])
        mix_ref[...] = mix
        y = x_ref[...] + mix * _rms(mix) * g_ref[...]
        if last_layer:
            @pl.when(i == 0)
            def _():
                sq_ref[...] = jnp.zeros_like(sq_ref)

            err = y - t_ref[...]
            out_ref[...] = err * (1.0 / D_MODEL)
            sq_ref[...] += jnp.sum(err * err, axis=0, keepdims=True)
        else:
            out_ref[...] = y

    row = lambda w: pl.BlockSpec((t, w), lambda i: (i, 0))
    hprev, hnext = _halo_specs(t, s, CONV_W)
    big = jax.ShapeDtypeStruct((s, D_MODEL), F32)
    return pl.pallas_call(
        body, name="fwd_out_loss" if last_layer else "fwd_out", grid=(s // t,),
        in_specs=[row(D_MODEL), row(D_IN), pl.BlockSpec((N_HEADS, t, LANES), lambda i: (0, i, 0)), row(CONV_W),
                  hprev, hnext, _const_spec((32, CONV_W)), _const_spec((8, CONV_W)), _const_spec((8, SG_W)),
                  _const_spec((4, SG_CHUNK, SG_CHUNK)), _const_spec((SG_CHUNK, SG_W)),
                  _const_spec((D_MODEL, D_MODEL)), _const_spec((1, D_MODEL))] + ([row(D_MODEL)] if last_layer else []),
        out_specs=[row(D_MODEL), row(D_MODEL), row(KEPT_W)] + ([_const_spec((1, D_MODEL))] if last_layer else []),
        out_shape=[big, big, jax.ShapeDtypeStruct((s, KEPT_W), F32)]
        + ([jax.ShapeDtypeStruct((1, D_MODEL), F32)] if last_layer else []),
        scratch_shapes=[pltpu.VMEM((SUBLANES, t + 2 * HALO, CONV_W), F32)],
        compiler_params=_arb(),
    )(*((x, proj, o, hc, hc, hc, cw, cp, sp, sgw_bf, sgb, w_out_bf, g_post) + ((target,) if last_layer else ())))


def _bwd_out(dy, mix, proj, o, hc, kept, cw, cp, sp, sgw_bf, sgwt_bf, sgb, w_out_t_bf, g_post, t, scatter=()):
    s = dy.shape[0]
    n_chunks = t // SG_CHUNK
    ex_args, ex_in, ex_out, ex_shapes, ex_sems, ex_hook = _riding_exchange(True, scatter)
    n_ex = len(scatter)

    def body(*refs):
        (dy_ref, mix_ref, proj_ref, o_ref, hc_ref, hp_ref, hn_ref, kept_ref, cw_ref, cp_ref, sp_ref, sgw_ref,
         sgwt_ref, sgb_ref, wt_ref, g_ref) = refs[:16]
        (do_ref, dgs_ref, dc0_ref, gwo_ref, gpost_ref, gcw_ref, gvec_ref, gsgw_ref,
         gsgb_ref) = refs[16 + n_ex:25 + n_ex]
        hext_ref = refs[25 + 2 * n_ex]
        if n_ex:
            ex_hook(refs[16:16 + n_ex], refs[25 + n_ex:25 + 2 * n_ex], refs[26 + 2 * n_ex:])
        i = pl.program_id(0)

        @pl.when(i == 0)
        def _():
            for ref in (gwo_ref, gpost_ref, gcw_ref, gvec_ref, gsgw_ref, gsgb_ref):
                ref[...] = jnp.zeros_like(ref)

        _conv_window(hext_ref, hp_ref, hc_ref[...], hn_ref, i == 0, i == pl.num_programs(0) - 1, t)
        r = _groups_fwd(proj_ref, o_ref, hext_ref, cw_ref, cp_ref, sp_ref, sgw_ref, sgb_ref, t, kept=kept_ref[...])

        dyv = dy_ref[...]
        mix_v = mix_ref[...]
        rr = _rms(mix_v)
        gd = dyv * g_ref[...]
        dmix = rr * gd - mix_v * (rr * rr * rr * jnp.mean(gd * mix_v, axis=-1, keepdims=True))
        gpost_ref[...] += jnp.sum(dyv * mix_v * rr, axis=0, keepdims=True)
        dmix_bf = dmix.astype(BF16)
        gwo_ref[...] += _dot_tn(r["mc_bf"], dmix_bf)
        dmc = _dot(dmix_bf, wt_ref[...])

        d_att = dmc[:, 0:ATT_W]
        dg_att = d_att * r["att"] * r["dgate_a"]
        d_o = d_att * r["gate_a"]
        prod = d_o * r["att"]
        for p in range(N_HEADS // 2):
            sl = slice(p * LANES, (p + 1) * LANES)
            pr = prod[:, sl]
            tot = jnp.sum(pr, axis=1, keepdims=True)
            ev = jnp.sum(jnp.where(_lane(pr.shape) < HEAD_DIM, pr, 0.0), axis=1, keepdims=True)
            for odd, delta in ((False, ev), (True, tot - ev)):
                hi, mid, lo = _split3(-delta)
                do_ref[2 * p + int(odd)] = _with_spare(_head_slab(d_o[:, sl], odd), hi, mid, lo).astype(BF16)

        dcv = dmc[:, ATT_W:ATT_W + CONV_W]
        dg_conv = dcv * r["c2"] * r["dgate_c"]
        dc1 = dcv * r["gate_c"] * (r["sg_c1"] * (1.0 + r["c1"] * (1.0 - r["sg_c1"])))
        dc0 = _ln_bwd(dc1 * cp_ref[1:2, :], r["xh_c"], r["rs_c"])
        dc0_ref[...] = dc0
        for kk in range(CONV_K):
            gcw_ref[kk:kk + 1, :] += jnp.sum(dc0 * _tap(hext_ref, kk + 1, t), axis=0, keepdims=True)

        dsg = dmc[:, ATT_W + CONV_W:D_MODEL]
        dg_sg = dsg * r["um"] * r["dgate_s"]
        du = dsg * r["mixed"] * r["gate_s"] * r["dgu"]
        dmx = dsg * r["gu"] * r["gate_s"]
        dmx_bf = dmx.astype(BF16)
        sgb_sum = dmx[0:SG_CHUNK, :]
        for n in range(1, n_chunks):
            sgb_sum = sgb_sum + dmx[n * SG_CHUNK:(n + 1) * SG_CHUNK, :]
        gsgb_ref[...] += sgb_sum
        dv1_rows = []
        for n in range(n_chunks):
            pairs = []
            for p in range(SG_W // LANES):
                rs_ = slice(n * SG_CHUNK, (n + 1) * SG_CHUNK)
                ls_ = slice(p * LANES, (p + 1) * LANES)
                dm = dmx_bf[rs_, ls_]
                xp = r["v1_bf"][rs_, ls_]
                low = _lane(dm.shape) < HEAD_DIM
                zero = jnp.zeros_like(dm)
                gsgw_ref[2 * p] += _dot_nt(jnp.where(low, dm, zero), xp)
                gsgw_ref[2 * p + 1] += _dot_nt(jnp.where(low, zero, dm), xp)
                pairs.append(jnp.where(low, _dot(sgwt_ref[2 * p], dm), _dot(sgwt_ref[2 * p + 1], dm)))
            dv1_rows.append(jnp.concatenate(pairs, axis=1))
        dv1 = dv1_rows[0] if n_chunks == 1 else jnp.concatenate(dv1_rows, axis=0)
        dvs = _ln_bwd(dv1 * sp_ref[0:1, :], r["xh_s"], r["rs_s"]) * r["dgv"]

        zrow = jnp.zeros((1, CONV_W), F32)
        gvec_ref[...] += jnp.concatenate([
            jnp.sum(dc0, axis=0, keepdims=True),
            jnp.sum(dc1 * r["xh_c"], axis=0, keepdims=True),
            jnp.sum(dc1, axis=0, keepdims=True),
            jnp.sum(dv1 * r["xh_s"], axis=0, keepdims=True),
            jnp.sum(dv1, axis=0, keepdims=True),
            zrow, zrow, zrow], axis=0)
        dgs_ref[...] = jnp.concatenate([dg_att, dg_conv, du, dvs, dg_sg], axis=1)

    row = lambda w: pl.BlockSpec((t, w), lambda i: (i, 0))
    heads = pl.BlockSpec((N_HEADS, t, LANES), lambda i: (0, i, 0))
    hprev, hnext = _halo_specs(t, s, CONV_W)
    return pl.pallas_call(
        body, name="bwd_out_scatter" if n_ex else "bwd_out", grid=(s // t,),
        in_specs=[row(D_MODEL), row(D_MODEL), row(D_IN), heads, row(CONV_W), hprev, hnext, row(KEPT_W),
                  _const_spec((32, CONV_W)), _const_spec((8, CONV_W)), _const_spec((8, SG_W)),
                  _const_spec((4, SG_CHUNK, SG_CHUNK)), _const_spec((4, SG_CHUNK, SG_CHUNK)),
                  _const_spec((SG_CHUNK, SG_W)), _const_spec((D_MODEL, D_MODEL)), _const_spec((1, D_MODEL))] + ex_in,
        out_specs=[heads, row(GATES_W), row(CONV_W), _const_spec((D_MODEL, D_MODEL)), _const_spec((1, D_MODEL)),
                   _const_spec((32, CONV_W)), _const_spec((8, CONV_W)), _const_spec((4, SG_CHUNK, SG_CHUNK)),
                   _const_spec((SG_CHUNK, SG_W))] + ex_out,
        out_shape=[jax.ShapeDtypeStruct((N_HEADS, s, LANES), BF16), jax.ShapeDtypeStruct((s, GATES_W), F32),
                   jax.ShapeDtypeStruct((s, CONV_W), F32), jax.ShapeDtypeStruct((D_MODEL, D_MODEL), F32),
                   jax.ShapeDtypeStruct((1, D_MODEL), F32), jax.ShapeDtypeStruct((32, CONV_W), F32),
                   jax.ShapeDtypeStruct((8, CONV_W), F32), jax.ShapeDtypeStruct((4, SG_CHUNK, SG_CHUNK), F32),
                   jax.ShapeDtypeStruct((SG_CHUNK, SG_W), F32)] + ex_shapes,
        scratch_shapes=[pltpu.VMEM((SUBLANES, t + 2 * HALO, CONV_W), F32)] + (ex_sems if n_ex else []),
        compiler_params=_arb(),
    )(dy, mix, proj, o, hc, hc, hc, kept, cw, cp, sp, sgw_bf, sgwt_bf, sgb, w_out_t_bf, g_post, *ex_args)


def _flash_bwd(qa, doa, k, v, tq, tk):
    s = qa.shape[1]
    rows = GROUP * tq
    nk = s // tk
    n_q = s // tq

    def body(qa_ref, do_ref, k_ref, v_ref, dq_ref, dk_hbm, dv_hbm,
             dq_scr, dk_scr, dv_scr, s0, s1, d0, d1, p0, p1, e0, e1, sems):
        j, i = pl.program_id(0), pl.program_id(1)
        s_bufs, d_bufs, p_bufs, e_bufs = (s0, s1), (d0, d1), (p0, p1), (e0, e1)
        qv = qa_ref[...].reshape(rows, LANES)
        dov = do_ref[...].reshape(rows, LANES)
        q_t = qv.astype(F32).T.astype(BF16)
        do_t = dov.astype(F32).T.astype(BF16)
        dq_scr[...] = jnp.zeros((rows, LANES), F32)

        @pl.when(i == 0)
        def _():
            dk_scr[...] = jnp.zeros_like(dk_scr)
            dv_scr[...] = jnp.zeros_like(dv_scr)

        def at(c):
            return _chunk_rows(c, tk)

        def scores(c, slot):
            s_bufs[slot][...] = _dot_nt(qv, k_ref[at(c), :])
            d_bufs[slot][...] = _dot_nt(dov, v_ref[at(c), :])

        def probs(slot):
            for h in range(GROUP):
                r = slice(h * tq, (h + 1) * tq)
                p = jnp.exp(s_bufs[slot][r, :])
                p_bufs[slot][r, :] = p.astype(BF16)
                e_bufs[slot][r, :] = (p * d_bufs[slot][r, :]).astype(BF16)

        def grads(c, slot):
            ds = e_bufs[slot][...]
            dq_scr[...] += _dot(ds, k_ref[at(c), :])
            dv_scr[c] += _dot(do_t, p_bufs[slot][...])
            dk_scr[c] += _dot(q_t, ds)

        p1[...] = jnp.zeros((rows, tk), BF16)
        e1[...] = jnp.zeros((rows, tk), BF16)
        _three_stage_pipeline(nk, 4, scores, probs, grads, peel=False)
        dq_ref[...] = dq_scr[...].reshape(GROUP, tq, LANES)

        @pl.when(i == n_q - 1)
        def _():
            out = [pltpu.make_async_copy(dk_scr, dk_hbm.at[j], sems.at[0]),
                   pltpu.make_async_copy(dv_scr, dv_hbm.at[j], sems.at[1])]
            for cp in out:
                cp.start()
            for cp in out:
                cp.wait()

    qspec = pl.BlockSpec((GROUP, tq, LANES), lambda j, i: (j, i, 0))
    kvspec = pl.BlockSpec((None, s, LANES), lambda j, i: (j, 0, 0))
    hbm = pl.BlockSpec(memory_space=pl.ANY)
    stage_f32 = pltpu.VMEM((rows, tk), F32)
    stage_bf = pltpu.VMEM((rows, tk), BF16)
    kv_t = jax.ShapeDtypeStruct((N_KV, nk, LANES, tk), F32)
    return pl.pallas_call(
        body, name="flash_bwd", grid=(N_KV, n_q),
        in_specs=[qspec, qspec, kvspec, kvspec],
        out_specs=[qspec, hbm, hbm],
        out_shape=[jax.ShapeDtypeStruct((N_HEADS, s, LANES), F32), kv_t, kv_t],
        scratch_shapes=[pltpu.VMEM((rows, LANES), F32), pltpu.VMEM((nk, LANES, tk), F32), pltpu.VMEM((nk, LANES, tk), F32),
                        stage_f32, stage_f32, stage_f32, stage_f32, stage_bf, stage_bf, stage_bf, stage_bf,
                        pltpu.SemaphoreType.DMA((2,))],
        compiler_params=_arb(2),
    )(qa, doa, k, v)


def _bwd_in(dy, x, proj, dq, dk, dv, dgs, dc0, cw, g_pre, w_in_t_bf, bd, qg, kg, cos, sin, t):
    s = x.shape[0]

    def body(dy_ref, x_ref, proj_ref, dq_ref, dk_ref, dv_ref, dgs_ref, dc_ref, dcp_ref, dcn_ref, cw_ref, g_ref,
             wt_ref, bd_ref, qg_ref, kg_ref, cos_ref, sin_ref,
             dx_ref, dproj_ref, h_ref, gpre_ref, gq_ref, gk_ref, dext_ref):
        i = pl.program_id(0)

        @pl.when(i == 0)
        def _():
            for ref in (gpre_ref, gq_ref, gk_ref):
                ref[...] = jnp.zeros_like(ref)

        proj = proj_ref[...]
        cos_pair = cos_ref[...]
        sin_pair = sin_ref[...]
        cosv = jnp.concatenate([cos_pair] * (N_HEADS // 2), axis=1)
        sinv = jnp.concatenate([sin_pair] * (N_HEADS // 2), axis=1)

        def head_norm_bwd(dr, z, bdm, g, cs, sn, gacc_ref):
            dn = dr * cs + _swap16(dr * sn)
            rr = lax.rsqrt(_group_mean(z * z, bdm) + EPS)
            gdn = dn * g
            gacc_ref[...] += jnp.sum(dn * z * rr, axis=0, keepdims=True)
            return rr * gdn - z * (rr * rr * rr * _group_mean(gdn * z, bdm))

        dq_cat = _heads_to_cat(dq_ref, N_HEADS) * ATT_SCALE
        dzq = head_norm_bwd(dq_cat, proj[:, C_Q:C_K], bd_ref[...], qg_ref[...], cosv, sinv, gq_ref)

        def kv_pair(ref):
            return jnp.concatenate([ref[0, 0:HEAD_DIM, :], ref[1, 0:HEAD_DIM, :]], axis=0).T

        dk_cat = kv_pair(dk_ref)
        dzk = head_norm_bwd(dk_cat, proj[:, C_K:C_V], bd_ref[0:LANES, 0:LANES], kg_ref[...],
                            cos_pair, sin_pair, gk_ref)
        dv_cat = kv_pair(dv_ref)

        _conv_window(dext_ref, dcp_ref, dc_ref[...], dcn_ref, i == 0, i == pl.num_programs(0) - 1, t)
        dhc = jnp.zeros((t, CONV_W), F32)
        for kk in range(CONV_K):
            dhc = dhc + cw_ref[kk:kk + 1, :] * _tap(dext_ref, CONV_K - kk, t)
        sg = _sigmoid(proj[:, C_A2:C_GC])
        da1 = dhc * sg
        da2 = dhc * proj[:, C_A1:C_A2] * sg * (1.0 - sg)

        dgs = dgs_ref[...]
        dproj_bf = jnp.concatenate([dzq, dzk, dv_cat, dgs[:, 0:ATT_W], da1, da2, dgs[:, ATT_W:GATES_W]],
                                   axis=1).astype(BF16)
        dproj_ref[...] = dproj_bf
        dh = _dot(dproj_bf, wt_ref[...])

        xv = x_ref[...]
        rr = _rms(xv)
        gv = g_ref[...]
        h_ref[...] = (xv * rr * gv).astype(BF16)
        gdh = dh * gv
        gpre_ref[...] += jnp.sum(dh * xv * rr, axis=0, keepdims=True)
        dx_ref[...] = dy_ref[...] + rr * gdh - xv * (rr * rr * rr * jnp.mean(gdh * xv, axis=-1, keepdims=True))

    row = lambda w: pl.BlockSpec((t, w), lambda i: (i, 0))
    heads = lambda n: pl.BlockSpec((n, t, LANES), lambda i: (0, i, 0))
    hprev, hnext = _halo_specs(t, s, CONV_W)
    tk = dk.shape[3]
    assert tk % t == 0
    kv_t = pl.BlockSpec((N_KV, None, LANES, t), lambda i: (0, i // (tk // t), 0, i % (tk // t)))
    return pl.pallas_call(
        body, name="bwd_in", grid=(s // t,),
        in_specs=[row(D_MODEL), row(D_MODEL), row(D_IN), heads(N_HEADS), kv_t, kv_t, row(GATES_W),
                  row(CONV_W), hprev, hnext, _const_spec((32, CONV_W)), _const_spec((1, D_MODEL)),
                  _const_spec((D_IN, D_MODEL)), _const_spec((ATT_W, ATT_W)), _const_spec((1, ATT_W)),
                  _const_spec((1, LANES)), row(LANES), row(LANES)],
        out_specs=[row(D_MODEL), row(D_IN), row(D_MODEL), _const_spec((1, D_MODEL)), _const_spec((1, ATT_W)),
                   _const_spec((1, LANES))],
        out_shape=[jax.ShapeDtypeStruct((s, D_MODEL), F32), jax.ShapeDtypeStruct((s, D_IN), BF16),
                   jax.ShapeDtypeStruct((s, D_MODEL), BF16), jax.ShapeDtypeStruct((1, D_MODEL), F32),
                   jax.ShapeDtypeStruct((1, ATT_W), F32), jax.ShapeDtypeStruct((1, LANES), F32)],
        scratch_shapes=[pltpu.VMEM((SUBLANES, t + 2 * HALO, CONV_W), F32)],
        compiler_params=_arb(),
    )(dy, x, proj, dq, dk, dv, dgs, dc0, dc0, dc0, cw, g_pre, w_in_t_bf, bd, qg, kg, cos, sin)


def _grad_w_in(h_bf, dproj_bf, t):
    s = h_bf.shape[0]
    half = D_IN // 2

    def body(h_ref, d_ref, g_ref):
        @pl.when(pl.program_id(1) == 0)
        def _():
            g_ref[...] = jnp.zeros_like(g_ref)

        g_ref[...] += _dot_tn(h_ref[...], d_ref[...])

    return pl.pallas_call(
        body, name="grad_w_in", grid=(2, s // t),
        in_specs=[pl.BlockSpec((t, D_MODEL), lambda j, i: (i, 0)), pl.BlockSpec((t, half), lambda j, i: (i, j))],
        out_specs=pl.BlockSpec((D_MODEL, half), lambda j, i: (0, j)),
        out_shape=jax.ShapeDtypeStruct((D_MODEL, D_IN), F32),
        compiler_params=_arb(2),
    )(h_bf, dproj_bf)


def _place():
    x, y, c = lax.axis_index("x"), lax.axis_index("y"), lax.axis_index("c")
    chips = [(1 - x, y), (x, 1 - y), (1 - x, 1 - y)]
    return x, y, c, chips


def _any_specs(n):
    return [pl.BlockSpec(memory_space=pl.ANY)] * n


class _ChipExchange:
    def __init__(self, scatter, ins, outs, send_sems, recv_sems, local_sems):
        n = len(ins)
        x, y, c, chips = _place()
        mine = 2 * x + y
        src = (lambda a, piece: ins[a].at[piece]) if scatter else (lambda a, piece: ins[a])

        def copy(j, a, piece, slot, to):
            return pltpu.make_async_remote_copy(src_ref=src(a, piece), dst_ref=outs[a].at[slot],
                                                send_sem=send_sems.at[n * j + a], recv_sem=recv_sems.at[n * j + a],
                                                device_id=to, device_id_type=MESH)

        self.local = [pltpu.make_async_copy(src(a, mine), outs[a].at[mine], local_sems.at[a]) for a in range(n)]
        self.sends = [copy(j, a, 2 * px + py, mine, (px, py, c)) for j, (px, py) in enumerate(chips) for a in range(n)]
        self.arrivals = lambda: [copy(j, a, mine, 2 * px + py, (px, py, c))
                                 for j, (px, py) in enumerate(chips) for a in range(n)]

    def start(self):
        for cp in self.local + self.sends:
            cp.start()

    def finish(self):
        for cp in self.arrivals():
            cp.wait_recv()
        for cp in self.sends:
            cp.wait_send()
        for cp in self.local:
            cp.wait()

    @staticmethod
    def out_shapes(scatter, arrs):
        return [jax.ShapeDtypeStruct(a.shape if scatter else (N_CHIPS,) + a.shape, a.dtype) for a in arrs]

    @staticmethod
    def semaphores(n):
        return [pltpu.SemaphoreType.DMA((3 * n,)), pltpu.SemaphoreType.DMA((3 * n,)), pltpu.SemaphoreType.DMA((n,))]


def _exchange(scatter, arrs, name):
    n = len(arrs)

    def body(*refs):
        ex = _ChipExchange(scatter, refs[:n], refs[n:2 * n], *refs[2 * n:])
        ex.start()
        ex.finish()

    return pl.pallas_call(
        body, name=name, in_specs=_any_specs(n), out_specs=_any_specs(n),
        out_shape=_ChipExchange.out_shapes(scatter, arrs), scratch_shapes=_ChipExchange.semaphores(n),
    )(*arrs)


def _riding_exchange(scatter, arrs):
    n = len(arrs)

    def hook(ins, outs, sems):
        i = pl.program_id(0)

        @pl.when(i == 0)
        def _():
            _ChipExchange(scatter, ins, outs, *sems).start()

        @pl.when(i == pl.num_programs(0) - 1)
        def _():
            _ChipExchange(scatter, ins, outs, *sems).finish()

    return (tuple(arrs), _any_specs(n), _any_specs(n), _ChipExchange.out_shapes(scatter, arrs),
            _ChipExchange.semaphores(n), hook)


def _sum_chips(parts, rb, name):
    depth = len(parts)
    _, r, cdim = parts[0].shape

    def body(*refs):
        o_ref = refs[depth]
        for l in range(depth):
            def add(p_ref=refs[l]):
                part = lambda j: p_ref[j].astype(F32)
                o_ref[...] = ((part(0) + part(1)) + part(2)) + part(3)

            pl.when(pl.program_id(0) == l)(add)

    return pl.pallas_call(
        body, name=name, grid=(depth, r // rb),
        in_specs=[pl.BlockSpec((N_CHIPS, rb, cdim), lambda l, i: (0, i, 0))] * depth,
        out_specs=pl.BlockSpec((None, rb, cdim), lambda l, i: (l, i, 0)),
        out_shape=jax.ShapeDtypeStruct((depth, r, cdim), F32),
        compiler_params=_arb(2),
    )(*parts)


def _swap_with_sibling(a, b):
    arrs = (a, b)
    n = len(arrs)

    def body(*refs):
        ins, outs = refs[:n], refs[n:2 * n]
        send_sems, recv_sems = refs[2 * n:]
        x, y, c, _ = _place()
        cps = [pltpu.make_async_remote_copy(src_ref=ins[k], dst_ref=outs[k], send_sem=send_sems.at[k],
                                            recv_sem=recv_sems.at[k], device_id=(x, y, 1 - c), device_id_type=MESH)
               for k in range(n)]
        for cp in cps:
            cp.start()
        for cp in cps:
            cp.wait()

    return pl.pallas_call(
        body, name="swap_with_sibling",
        in_specs=_any_specs(n), out_specs=_any_specs(n),
        out_shape=[jax.ShapeDtypeStruct(v.shape, v.dtype) for v in arrs],
        scratch_shapes=[pltpu.SemaphoreType.DMA((n,)), pltpu.SemaphoreType.DMA((n,))],
    )(*arrs)


def _allreduce_small(slab):
    m, n = slab.shape

    def body(x_ref, out_ref, gath, send_sems, recv_sems, local_sem):
        x, y, c, chips = _place()
        me, sibling = (x, y, c), (x, y, 1 - c)

        def rows(px, py, pc):
            return gath.at[pl.ds(pl.multiple_of((4 * px + 2 * py + pc) * m, 8), m), :]

        def copy(k, block, to, src=None):
            return pltpu.make_async_remote_copy(src_ref=rows(*block) if src is None else src, dst_ref=rows(*block),
                                                send_sem=send_sems.at[k], recv_sem=recv_sems.at[k],
                                                device_id=to, device_id_type=MESH)

        mine = pltpu.make_async_copy(x_ref, rows(*me), local_sem)
        mine.start()
        first = [copy(0, me, sibling, src=x_ref)]
        first += [copy(1 + j, me, (*chip, c), src=x_ref) for j, chip in enumerate(chips)]
        for cp in first:
            cp.start()
        passed = [copy(4 + j, (*chip, c), sibling) for j, chip in enumerate(chips)]
        for j, chip in enumerate(chips):
            copy(1 + j, (*chip, c), me).wait_recv()
            passed[j].start()
        copy(0, sibling, me).wait_recv()
        for j, chip in enumerate(chips):
            copy(4 + j, (*chip, 1 - c), me).wait_recv()
        for cp in first + passed:
            cp.wait_send()
        mine.wait()
        total = gath[0:m, :]
        for d in range(1, N_DEV):
            total = total + gath[d * m:(d + 1) * m, :]
        out_ref[...] = total

    return pl.pallas_call(
        body, name="allreduce_small",
        in_specs=[pl.BlockSpec(memory_space=pltpu.VMEM)],
        out_specs=pl.BlockSpec(memory_space=pltpu.VMEM),
        out_shape=jax.ShapeDtypeStruct((m, n), F32),
        scratch_shapes=[pltpu.VMEM((N_DEV * m, n), F32), pltpu.SemaphoreType.DMA((7,)), pltpu.SemaphoreType.DMA((7,)),
                        pltpu.SemaphoreType.DMA],
    )(slab)


def _adamw(w, ga, gb, m, v, rb, name):
    depth, r, cdim = w.shape

    def body(w_ref, ga_ref, gb_ref, m_ref, v_ref, g_out, d_out, m_out, v_out):
        g = ga_ref[...] + gb_ref[...]
        m2 = ADAM_B1 * m_ref[...] + (1.0 - ADAM_B1) * g
        v2 = ADAM_B2 * v_ref[...] + (1.0 - ADAM_B2) * (g * g)
        m_hat = m2 / (1.0 - ADAM_B1 ** ADAM_STEP)
        v_hat = v2 / (1.0 - ADAM_B2 ** ADAM_STEP)
        g_out[...] = g
        d_out[...] = -ADAM_LR * (m_hat / (jnp.sqrt(v_hat) + ADAM_EPS) + ADAM_WD * w_ref[...])
        m_out[...] = m2
        v_out[...] = v2

    spec = pl.BlockSpec((None, rb, cdim), lambda l, i: (l, i, 0))
    shp = jax.ShapeDtypeStruct((depth, r, cdim), F32)
    return pl.pallas_call(
        body, name=name, grid=(depth, r // rb),
        in_specs=[spec] * 5, out_specs=[spec] * 4, out_shape=[shp] * 4,
        compiler_params=_arb(2),
    )(w, ga, gb, m, v)


def _rope_tables(s):
    t = jnp.arange(s, dtype=jnp.int32)
    row = (t // GRID_W).astype(F32)
    col = (t % GRID_W).astype(F32)
    half = HEAD_DIM // 4
    inv_freq = ROPE_THETA ** (-jnp.arange(half, dtype=F32) / half)
    ar = row[:, None] * inv_freq[None, :]
    ac = col[:, None] * inv_freq[None, :]
    cos = jnp.concatenate([jnp.cos(ar), jnp.cos(ar), jnp.cos(ac), jnp.cos(ac)], axis=1)
    sin = jnp.concatenate([-jnp.sin(ar), jnp.sin(ar), -jnp.sin(ac), jnp.sin(ac)], axis=1)
    return jnp.tile(cos, (1, 2)), jnp.tile(sin, (1, 2))


def _pad_rows(a, rows):
    return jnp.pad(a, ((0, rows - a.shape[0]),) + ((0, 0),) * (a.ndim - 1))


_SMALL = ("pre_norm", "post_norm", "q_norm", "k_norm", "conv_dw_b", "conv_ln_g", "conv_ln_b", "sg_ln_g", "sg_ln_b",
          "sg_w", "sg_b")


def _pack(parts):
    flat = jnp.concatenate([p.reshape(-1, LANES) for p in parts], axis=0)
    return _pad_rows(flat, -(-flat.shape[0] // 8) * 8)


def _unpack(slab, shapes):
    out, r = [], 0
    for shp in shapes:
        n = 1
        for d in shp:
            n *= d
        out.append(slab[r:r + n // LANES].reshape(shp))
        r += n // LANES
    return out


def kernel(x, pre_norm, post_norm, w_in, w_out, q_norm, k_norm, conv_dw, conv_dw_b, conv_ln_g, conv_ln_b, sg_ln_g, sg_ln_b, sg_w, sg_b, loss_target, m_pre_norm, m_post_norm, m_w_in, m_w_out, m_q_norm, m_k_norm, m_conv_dw, m_conv_dw_b, m_conv_ln_g, m_conv_ln_b, m_sg_ln_g, m_sg_ln_b, m_sg_w, m_sg_b, v_pre_norm, v_post_norm, v_w_in, v_w_out, v_q_norm, v_k_norm, v_conv_dw, v_conv_dw_b, v_conv_ln_g, v_conv_ln_b, v_sg_ln_g, v_sg_ln_b, v_sg_w, v_sg_b):
    depth = w_in.shape[0]
    s = x.shape[1]
    assert x.shape[0] == 1 and s % SG_CHUNK == 0 and x.shape[2] == D_MODEL
    tq = min(256, s)
    tq_fwd = min(512, s)
    tk = min(512, s // 2)
    t = min(256, tk)
    shard_cols = w_in.shape[2]
    chip = 2 * lax.axis_index("x") + lax.axis_index("y")

    w_in_sh, w_out_sh = w_in.astype(BF16), w_out.astype(BF16)
    whole_w_in = lambda g: jnp.concatenate([g[j] for j in range(N_CHIPS)], axis=2)
    w_in_bf = [whole_w_in(_exchange(False, (w_in_sh[0:1],), "gather_w_in_first")[0])[0]]
    riders = ((w_in_sh[1:],) if depth > 1 else ()) + (w_out_sh, conv_dw)
    sgw_bf = sg_w.astype(BF16)
    sgwt_bf = jnp.swapaxes(sg_w, 2, 3).astype(BF16)

    cos, sin = _rope_tables(s)
    bd = jnp.kron(jnp.eye(N_HEADS, dtype=F32), jnp.full((HEAD_DIM, HEAD_DIM), 1.0 / HEAD_DIM, F32)).astype(BF16)

    def head_gains(l):
        return jnp.tile(q_norm[l], N_HEADS)[None, :], jnp.tile(k_norm[l], N_KV)[None, :]

    def layer_consts(l):
        cw = _pad_rows(cdw_full[l], 32)
        cp = _pad_rows(jnp.stack([conv_dw_b[l], conv_ln_g[l], conv_ln_b[l]]), 8)
        sp = _pad_rows(jnp.stack([sg_ln_g[l], sg_ln_b[l]]), 8)
        sgb = jnp.repeat(sg_b[l].T, HEAD_DIM, axis=1)
        return cw, cp, sp, sgb

    xs = [x[0]]
    saved = []
    for l in range(depth):
        qg, kg = head_gains(l)
        outs = _fwd_in(xs[l], pre_norm[l][None, :], w_in_bf[l], bd, qg, kg, cos, sin, t, gather=riders if l == 0 else ())
        proj, hc, q, k, v = outs[:5]
        if l == 0:
            gathered = list(outs[5:])
            if depth > 1:
                w_in_bf += list(whole_w_in(gathered.pop(0)))
            w_out_bf = jnp.concatenate([gathered[0][j] for j in range(N_CHIPS)], axis=1)
            cdw_full = jnp.concatenate([gathered[1][j] for j in range(N_CHIPS)], axis=2)
            w_out_t_bf = jnp.swapaxes(w_out_bf, 1, 2)
        cw, cp, sp, sgb = layer_consts(l)
        o, qa = _flash_fwd(q, k, v, tq_fwd, tk)
        outs = _fwd_out(xs[l], proj, o, hc, cw, cp, sp, sgw_bf[l], sgb, w_out_bf[l], post_norm[l][None, :], t,
                        target=loss_target[0] if l == depth - 1 else None)
        saved.append((proj, hc, qa, k, v, o, outs[0], outs[2]))
        if l < depth - 1:
            xs.append(outs[1])
    dy, sq = outs[1], outs[3]
    loss = lax.psum(0.5 * jnp.sum(sq) / D_MODEL, ("x", "y", "c"))

    g_small = {n: [] for n in _SMALL + ("conv_dw",)}
    received = [None] * depth
    pieces = ()
    for l in reversed(range(depth)):
        cw, cp, sp, sgb = layer_consts(l)
        qg, kg = head_gains(l)
        proj, hc, qa, k, v, o, mix, kept = saved[l]
        outs = _bwd_out(dy, mix, proj, o, hc, kept, cw, cp, sp, sgw_bf[l], sgwt_bf[l], sgb, w_out_t_bf[l],
                        post_norm[l][None, :], t, scatter=pieces)
        doa, dgs, dc0, gwo, gpost, gcw, gvec, gsgw, gsgb = outs[:9]
        if pieces:
            received[l + 1] = outs[9:]
        dq, dk, dv = _flash_bwd(qa, doa, k, v, tq, tk)
        dy, dproj_bf, h_bf, gpre, gq, gk = _bwd_in(dy, xs[l], proj, dq, dk, dv, dgs, dc0, cw, pre_norm[l][None, :],
                                                  jnp.swapaxes(w_in_bf[l], 0, 1), bd, qg, kg, cos, sin, t)
        gwi = _grad_w_in(h_bf, dproj_bf, min(512, s))
        pieces = (jnp.stack([gwi[:, j * shard_cols:(j + 1) * shard_cols] for j in range(N_CHIPS)]).astype(BF16),
                  gwo.reshape(N_CHIPS, gwo.shape[0] // N_CHIPS, gwo.shape[1]).astype(BF16))
        g_small["pre_norm"].append(gpre[0])
        g_small["post_norm"].append(gpost[0])
        g_small["q_norm"].append(gq[0].reshape(N_HEADS, HEAD_DIM).sum(0))
        g_small["k_norm"].append(gk[0].reshape(N_KV, HEAD_DIM).sum(0))
        g_small["conv_dw"].append(gcw[:CONV_K])
        g_small["conv_dw_b"].append(gvec[0])
        g_small["conv_ln_g"].append(gvec[1])
        g_small["conv_ln_b"].append(gvec[2])
        g_small["sg_ln_g"].append(gvec[3])
        g_small["sg_ln_b"].append(gvec[4])
        g_small["sg_w"].append(gsgw)
        g_small["sg_b"].append(gsgb.reshape(SG_CHUNK, SG_W // HEAD_DIM, HEAD_DIM).sum(-1).T)
    grad_x = dy[None]
    g_small = {n: jnp.stack(vals[::-1]) for n, vals in g_small.items()}

    received[0] = _exchange(True, pieces, "scatter_grads")
    s_in = _sum_chips([r[0] for r in received], 256, "sum_chips_w_in")
    s_out = _sum_chips([r[1] for r in received], 256, "sum_chips_w_out")
    t_in, t_out = _swap_with_sibling(s_in, s_out)
    grad_w_in, delta_w_in, new_m_w_in, new_v_w_in = _adamw(w_in, s_in, t_in, m_w_in, v_w_in, 256, "adamw_w_in")
    grad_w_out, delta_w_out, new_m_w_out, new_v_w_out = _adamw(w_out, s_out, t_out, m_w_out, v_w_out, 256, "adamw_w_out")

    small_w = dict(pre_norm=pre_norm, post_norm=post_norm, q_norm=q_norm, k_norm=k_norm, conv_dw_b=conv_dw_b,
                   conv_ln_g=conv_ln_g, conv_ln_b=conv_ln_b, sg_ln_g=sg_ln_g, sg_ln_b=sg_ln_b, sg_w=sg_w, sg_b=sg_b)
    small_m = dict(pre_norm=m_pre_norm, post_norm=m_post_norm, q_norm=m_q_norm, k_norm=m_k_norm, conv_dw_b=m_conv_dw_b,
                   conv_ln_g=m_conv_ln_g, conv_ln_b=m_conv_ln_b, sg_ln_g=m_sg_ln_g, sg_ln_b=m_sg_ln_b, sg_w=m_sg_w,
                   sg_b=m_sg_b)
    small_v = dict(pre_norm=v_pre_norm, post_norm=v_post_norm, q_norm=v_q_norm, k_norm=v_k_norm, conv_dw_b=v_conv_dw_b,
                   conv_ln_g=v_conv_ln_g, conv_ln_b=v_conv_ln_b, sg_ln_g=v_sg_ln_g, sg_ln_b=v_sg_ln_b, sg_w=v_sg_w,
                   sg_b=v_sg_b)
    shapes = [small_w[n].shape for n in _SMALL]
    red = _allreduce_small(_pack([g_small[n] for n in _SMALL] + [g_small["conv_dw"]]))
    n_rep = sum(small_w[n].size for n in _SMALL) // LANES
    g_cdw_full = red[n_rep:n_rep + g_small["conv_dw"].size // LANES].reshape(g_small["conv_dw"].shape)
    cdw_cols = conv_dw.shape[2]
    g_cdw = lax.dynamic_slice_in_dim(g_cdw_full, chip * cdw_cols, cdw_cols, axis=2)
    g_slab = _pack([red[:n_rep], g_cdw])
    w_slab = _pack([small_w[n] for n in _SMALL] + [conv_dw])
    m_slab = _pack([small_m[n] for n in _SMALL] + [m_conv_dw])
    v_slab = _pack([small_v[n] for n in _SMALL] + [v_conv_dw])
    rows = w_slab.shape[0]
    outs = _adamw(w_slab[None], g_slab[None], jnp.zeros_like(g_slab)[None], m_slab[None], v_slab[None], rows, "adamw_small")
    unpacked = [dict(zip(_SMALL + ("conv_dw",), _unpack(o_[0], shapes + [conv_dw.shape]))) for o_ in outs]

    big = [dict(w_in=a, w_out=b) for a, b in ((grad_w_in, grad_w_out), (delta_w_in, delta_w_out),
                                             (new_m_w_in, new_m_w_out), (new_v_w_in, new_v_w_out))]
    order = ("pre_norm", "post_norm", "w_in", "w_out", "q_norm", "k_norm", "conv_dw", "conv_dw_b", "conv_ln_g",
             "conv_ln_b", "sg_ln_g", "sg_ln_b", "sg_w", "sg_b")
    result = [loss, grad_x]
    for kind in range(4):
        for name in order:
            result.append(big[kind][name] if name in big[kind] else unpacked[kind][name])
    return tuple(result)
```

```python
import jax
import jax.numpy as jnp
from jax import lax
from jax.experimental import pallas as pl
from jax.experimental.pallas import tpu as pltpu

F32 = jnp.float32
BF16 = jnp.bfloat16
MESH = pl.DeviceIdType.MESH

EPS = 1e-6
D_MODEL = 1024
D_IN = 2816
HEAD_DIM = 64
LANES = 128
SUBLANES = 8
N_HEADS = 8
N_KV = 2
GROUP = N_HEADS // N_KV
GRID_W = 64
ROPE_THETA = 10000.0
CONV_K = 31
CONV_W = 256
SG_W = 256
SG_CHUNK = 128
KEPT_W = CONV_W + 2 * SG_W
HALO = 16
ATT_SCALE = HEAD_DIM ** -0.5

ATT_W = N_HEADS * HEAD_DIM
C_Q, C_K, C_V, C_GA, C_A1, C_A2, C_GC, C_U, C_VS, C_GS = 0, 512, 640, 768, 1280, 1536, 1792, 2048, 2304, 2560
GATES_W = D_IN - C_GA - 2 * CONV_W

ADAM_LR = 0.001
ADAM_B1 = 0.9
ADAM_B2 = 0.999
ADAM_EPS = 1e-08
ADAM_WD = 0.01
ADAM_STEP = 10

N_CHIPS = 4
N_DEV = 8


def _dot(a, b):
    return jnp.dot(a, b, preferred_element_type=F32)


def _group_mean(x, bd_bf):
    hi = x.astype(BF16)
    lo = (x - hi.astype(F32)).astype(BF16)
    return _dot(hi, bd_bf) + _dot(lo, bd_bf)


def _dot_nt(a, b):
    return lax.dot_general(a, b, (((1,), (1,)), ((), ())), preferred_element_type=F32)


def _dot_tn(a, b):
    return lax.dot_general(a, b, (((0,), (0,)), ((), ())), preferred_element_type=F32)


def _lane(shape):
    return lax.broadcasted_iota(jnp.int32, shape, 1)


def _sigmoid(x):
    return 1.0 / (1.0 + jnp.exp(-x))


def _silu_fwd_bwd(x):
    s = _sigmoid(x)
    return x * s, s * (1.0 + x * (1.0 - s))


def _erf(x):
    x = jnp.clip(x, -4.0, 4.0)
    x2 = x * x
    a = -2.72614225801306e-10
    a = a * x2 + 2.77068142495902e-08
    a = a * x2 + -2.10102402082508e-06
    a = a * x2 + -5.69250639462346e-05
    a = a * x2 + -7.34990630326855e-04
    a = a * x2 + -2.95459980854025e-03
    a = a * x2 + -1.60960333262415e-02
    b = -1.45660718464996e-05
    b = b * x2 + -2.13374055278905e-04
    b = b * x2 + -1.68282697438203e-03
    b = b * x2 + -7.37332916720468e-03
    b = b * x2 + -1.42647390514189e-02
    return x * a / b


def _gelu_fwd_bwd(x, cdf=None):
    if cdf is None:
        cdf = 0.5 * (1.0 + _erf(x * 0.7071067811865476))
    pdf = jnp.exp(-0.5 * x * x) * 0.3989422804014327
    return x * cdf, cdf + x * pdf, cdf


def _rms(x):
    return lax.rsqrt(jnp.mean(x * x, axis=-1, keepdims=True) + EPS)


def _ln_hat(x):
    mu = jnp.mean(x, axis=-1, keepdims=True)
    xc = x - mu
    rs = lax.rsqrt(jnp.mean(xc * xc, axis=-1, keepdims=True) + EPS)
    return xc * rs, rs


def _ln_bwd(dxh, xh, rs):
    return rs * (dxh - jnp.mean(dxh, axis=-1, keepdims=True) - xh * jnp.mean(dxh * xh, axis=-1, keepdims=True))


def _swap16(z):
    parts = []
    for i in range(z.shape[1] // LANES):
        blk = z[:, i * LANES:(i + 1) * LANES]
        lane = _lane(blk.shape)
        parts.append(jnp.where((lane & 16) == 0, pltpu.roll(blk, LANES - 16, 1), pltpu.roll(blk, 16, 1)))
    return parts[0] if len(parts) == 1 else jnp.concatenate(parts, axis=1)


def _head_slab(pair, odd):
    src = pltpu.roll(pair, HEAD_DIM, 1) if odd else pair
    return jnp.where(_lane(pair.shape) < HEAD_DIM, src, 0.0)


def _pair_merge(even, odd):
    return jnp.where(_lane(even.shape) < HEAD_DIM, even, pltpu.roll(odd, HEAD_DIM, 1))


def _heads_to_cat(ref, n_heads):
    pairs = [_pair_merge(ref[2 * p], ref[2 * p + 1]) for p in range(n_heads // 2)]
    return pairs[0] if len(pairs) == 1 else jnp.concatenate(pairs, axis=1)


def _split3(x):
    hi = x.astype(BF16).astype(F32)
    r = x - hi
    mid = r.astype(BF16).astype(F32)
    lo = (r - mid).astype(BF16).astype(F32)
    return hi, mid, lo


def _with_spare(slab, hi, mid, lo):
    lane = _lane(slab.shape)
    return jnp.where(lane == HEAD_DIM, hi, jnp.where(lane == HEAD_DIM + 1, mid, jnp.where(lane == HEAD_DIM + 2, lo, slab)))


def _with_ones(slab):
    lane = _lane(slab.shape)
    return jnp.where((lane >= HEAD_DIM) & (lane < HEAD_DIM + 3), 1.0, slab)


def _conv_window(rot_ref, prev_ref, main, next_ref, first, last, t):
    n = t + 2 * HALO
    full = jnp.concatenate([jnp.where(first, 0.0, prev_ref[...]), main, jnp.where(last, 0.0, next_ref[...])], axis=0)
    rot_ref[0] = full
    for b in range(1, SUBLANES):
        rot_ref[b] = pltpu.roll(full, n - b, 0)


def _tap(rot_ref, start, t):
    a, b = divmod(start, SUBLANES)
    return rot_ref[b, SUBLANES * a:SUBLANES * a + t, :]


def _sgu_mix(v1_bf, w_ref, n_chunks):
    rows = []
    for n in range(n_chunks):
        pairs = []
        for p in range(SG_W // LANES):
            xp = v1_bf[n * SG_CHUNK:(n + 1) * SG_CHUNK, p * LANES:(p + 1) * LANES]
            me = _dot(w_ref[2 * p], xp)
            mo = _dot(w_ref[2 * p + 1], xp)
            pairs.append(jnp.where(_lane(me.shape) < HEAD_DIM, me, mo))
        rows.append(jnp.concatenate(pairs, axis=1))
    return rows[0] if len(rows) == 1 else jnp.concatenate(rows, axis=0)


def _halo_specs(t, s, width):
    per = t // HALO
    nblk = s // HALO
    prev = pl.BlockSpec((HALO, width), lambda i: (jnp.maximum(i * per - 1, 0), 0))
    nxt = pl.BlockSpec((HALO, width), lambda i: (jnp.minimum((i + 1) * per, nblk - 1), 0))
    return prev, nxt


def _const_spec(shape):
    nd = len(shape)
    return pl.BlockSpec(shape, lambda i: (0,) * nd)


def _arb(n=1):
    return pltpu.CompilerParams(dimension_semantics=("arbitrary",) * n)


def _fwd_in(x, g_pre, w_in_bf, bd, qg, kg, cos, sin, t, gather=()):
    s = x.shape[0]
    ex_args, ex_in, ex_out, ex_shapes, ex_sems, ex_hook = _riding_exchange(False, gather)
    n_ex = len(gather)

    def body(*refs):
        x_ref, g_ref, w_ref, bd_ref, qg_ref, kg_ref, cos_ref, sin_ref = refs[:8]
        proj_ref, hc_ref, q_ref, k_ref, v_ref = refs[8 + n_ex:13 + n_ex]
        if n_ex:
            ex_hook(refs[8:8 + n_ex], refs[13 + n_ex:13 + 2 * n_ex], refs[13 + 2 * n_ex:])
        xv = x_ref[...]
        h = (xv * _rms(xv) * g_ref[...]).astype(BF16)
        proj = _dot(h, w_ref[...])
        proj_ref[...] = proj
        cos_pair = cos_ref[...]
        sin_pair = sin_ref[...]
        cosv = jnp.concatenate([cos_pair] * (N_HEADS // 2), axis=1)
        sinv = jnp.concatenate([sin_pair] * (N_HEADS // 2), axis=1)
        q = proj[:, C_Q:C_K]
        qn = q * lax.rsqrt(_group_mean(q * q, bd_ref[...]) + EPS) * qg_ref[...]
        qr = (qn * cosv + _swap16(qn) * sinv) * ATT_SCALE
        for hh in range(N_HEADS):
            pair = qr[:, (hh // 2) * LANES:(hh // 2 + 1) * LANES]
            q_ref[hh] = _head_slab(pair, hh % 2 == 1).astype(BF16)
        k = proj[:, C_K:C_V]
        kn = k * lax.rsqrt(_group_mean(k * k, bd_ref[0:LANES, 0:LANES]) + EPS) * kg_ref[...]
        kr = kn * cos_pair + _swap16(kn) * sin_pair
        vv = proj[:, C_V:C_GA]
        for hh in range(N_KV):
            k_ref[hh] = _with_ones(_head_slab(kr, hh == 1)).astype(BF16)
            v_ref[hh] = _with_ones(_head_slab(vv, hh == 1)).astype(BF16)
        hc_ref[...] = proj[:, C_A1:C_A2] * _sigmoid(proj[:, C_A2:C_GC])

    row = lambda w: pl.BlockSpec((t, w), lambda i: (i, 0))
    heads = lambda n: pl.BlockSpec((n, t, LANES), lambda i: (0, i, 0))
    return pl.pallas_call(
        body, name="fwd_in_gather" if n_ex else "fwd_in", grid=(s // t,),
        in_specs=[row(D_MODEL), _const_spec((1, D_MODEL)), _const_spec((D_MODEL, D_IN)), _const_spec((ATT_W, ATT_W)),
                  _const_spec((1, ATT_W)), _const_spec((1, LANES)), row(LANES), row(LANES)] + ex_in,
        out_specs=[row(D_IN), row(CONV_W), heads(N_HEADS), heads(N_KV), heads(N_KV)] + ex_out,
        out_shape=[jax.ShapeDtypeStruct((s, D_IN), F32), jax.ShapeDtypeStruct((s, CONV_W), F32),
                   jax.ShapeDtypeStruct((N_HEADS, s, LANES), BF16), jax.ShapeDtypeStruct((N_KV, s, LANES), BF16),
                   jax.ShapeDtypeStruct((N_KV, s, LANES), BF16)] + ex_shapes,
        scratch_shapes=ex_sems if n_ex else [],
        compiler_params=_arb(),
    )(x, g_pre, w_in_bf, bd, qg, kg, cos, sin, *ex_args)


def _chunk_rows(c, tk):
    return pl.ds(c * tk, tk) if isinstance(c, int) else pl.ds(pl.multiple_of(c * tk, tk), tk)


def _three_stage_pipeline(nk, per_trip, stage1, stage2, stage3, peel):
    assert nk % 2 == 0 and per_trip % 2 == 0

    def step(t, parity, first=False, last=False):
        if not last:
            stage1(t + 1, 1 - parity)
        stage2(parity)
        if not first:
            stage3(t - 1, 1 - parity)

    stage1(0, 0)
    if not peel:
        while nk % per_trip:
            per_trip //= 2

        def whole_trip(i, carry):
            for u in range(per_trip):
                c = per_trip * i + u
                stage1(jnp.minimum(c + 1, nk - 1), 1 - u % 2)
                stage2(u % 2)
                stage3(jnp.maximum(c - 1, 0), 1 - u % 2)
            return carry

        lax.fori_loop(0, nk // per_trip, whole_trip, 0)
        stage3(nk - 1, 1)
        return

    step(0, 0, first=True)
    n_trips, left = divmod(nk - 2, per_trip)

    def trip(i, carry):
        for u in range(per_trip):
            step(1 + per_trip * i + u, (1 + u) % 2)
        return carry

    if n_trips:
        lax.fori_loop(0, n_trips, trip, 0)
    for t in range(1 + n_trips * per_trip, 1 + n_trips * per_trip + left):
        step(t, t % 2)
    step(nk - 1, 1, last=True)
    stage3(nk - 1, 1)


def _flash_fwd(q, k, v, tq, tk):
    s = q.shape[1]
    rows = GROUP * tq
    nk = s // tk

    def body(q_ref, k_ref, vt_ref, o_ref, qa_ref, m_scr, acc_scr, s0, s1, p0, p1, a0, a1):
        s_bufs, p_bufs, a_bufs = (s0, s1), (p0, p1), (a0, a1)
        qv = q_ref[...].reshape(rows, LANES)
        q_t = qv.astype(F32).T
        q_t_bf = q_t.astype(BF16)
        m_scr[...] = jnp.full((1, rows), -jnp.inf, F32)
        acc_scr[...] = jnp.zeros((LANES, rows), F32)

        def scores(c, slot):
            s_bufs[slot][...] = _dot(k_ref[_chunk_rows(c, tk), :], q_t_bf)

        def softmax(slot):
            for h in range(GROUP):
                r = slice(h * tq, (h + 1) * tq)
                sc = s_bufs[slot][:, r]
                m_prev = m_scr[:, r]
                m_new = jnp.maximum(m_prev, jnp.max(sc, axis=0, keepdims=True))
                p_bufs[slot][:, r] = jnp.exp((sc - m_new).astype(BF16))
                a_bufs[slot][:, r] = jnp.exp(m_prev - m_new)
                m_scr[:, r] = m_new

        def weighted_values(c, slot):
            acc_scr[...] = a_bufs[slot][...] * acc_scr[...] + _dot(vt_ref[c], p_bufs[slot][...])

        _three_stage_pipeline(nk, 4, scores, softmax, weighted_values, peel=True)

        acc = acc_scr[...]
        row = lax.broadcasted_iota(jnp.int32, acc.shape, 0)
        l = jnp.sum(jnp.where(row == HEAD_DIM, acc, 0.0), axis=0, keepdims=True)
        o_ref[...] = jnp.where(row < HEAD_DIM, acc / l, 0.0).T.reshape(GROUP, tq, LANES)
        hi, mid, lo = _split3(-(m_scr[...] + jnp.log(l)))
        qa_t = jnp.where(row == HEAD_DIM, hi, jnp.where(row == HEAD_DIM + 1, mid,
                                                        jnp.where(row == HEAD_DIM + 2, lo, q_t)))
        qa_ref[...] = qa_t.T.astype(BF16).reshape(GROUP, tq, LANES)

    qspec = pl.BlockSpec((GROUP, tq, LANES), lambda j, i: (j, i, 0))
    kspec = pl.BlockSpec((None, s, LANES), lambda j, i: (j, 0, 0))
    vtspec = pl.BlockSpec((None, nk, LANES, tk), lambda j, i: (j, 0, 0, 0))
    v_t = jnp.swapaxes(v.reshape(N_KV, nk, tk, LANES), 2, 3)
    return pl.pallas_call(
        body, name="flash_fwd", grid=(N_KV, s // tq),
        in_specs=[qspec, kspec, vtspec],
        out_specs=[qspec, qspec],
        out_shape=[jax.ShapeDtypeStruct((N_HEADS, s, LANES), F32), jax.ShapeDtypeStruct((N_HEADS, s, LANES), BF16)],
        scratch_shapes=[pltpu.VMEM((1, rows), F32), pltpu.VMEM((LANES, rows), F32),
                        pltpu.VMEM((tk, rows), F32), pltpu.VMEM((tk, rows), F32),
                        pltpu.VMEM((tk, rows), BF16), pltpu.VMEM((tk, rows), BF16),
                        pltpu.VMEM((1, rows), F32), pltpu.VMEM((1, rows), F32)],
        compiler_params=_arb(2),
    )(q, k, v_t)


def _groups_fwd(proj_ref, o_ref, hext_ref, cw_ref, cp_ref, sp_ref, sgw_ref, sgb_ref, t, kept=None):
    proj = proj_ref[...]
    r = {}
    r["att"] = _heads_to_cat(o_ref, N_HEADS)
    r["gate_a"], r["dgate_a"] = _silu_fwd_bwd(proj[:, C_GA:C_A1])
    r["att_g"] = r["att"] * r["gate_a"]
    if kept is None:
        c0 = jnp.zeros((t, CONV_W), F32) + cp_ref[0:1, :]
        for kk in range(CONV_K):
            c0 = c0 + cw_ref[kk:kk + 1, :] * _tap(hext_ref, kk + 1, t)
        cdf_u = cdf_v = None
    else:
        c0, cdf_u, cdf_v = kept[:, 0:CONV_W], kept[:, CONV_W:CONV_W + SG_W], kept[:, CONV_W + SG_W:KEPT_W]
    r["xh_c"], r["rs_c"] = _ln_hat(c0)
    r["c1"] = r["xh_c"] * cp_ref[1:2, :] + cp_ref[2:3, :]
    r["sg_c1"] = _sigmoid(r["c1"])
    r["c2"] = r["c1"] * r["sg_c1"]
    r["gate_c"], r["dgate_c"] = _silu_fwd_bwd(proj[:, C_GC:C_U])
    r["cnv_g"] = r["c2"] * r["gate_c"]
    r["gu"], r["dgu"], cdf_u = _gelu_fwd_bwd(proj[:, C_U:C_VS], cdf_u)
    gv, r["dgv"], cdf_v = _gelu_fwd_bwd(proj[:, C_VS:C_GS], cdf_v)
    r["kept"] = jnp.concatenate([c0, cdf_u, cdf_v], axis=1)
    r["xh_s"], r["rs_s"] = _ln_hat(gv)
    v1 = r["xh_s"] * sp_ref[0:1, :] + sp_ref[1:2, :]
    r["v1_bf"] = v1.astype(BF16)
    r["mixed"] = _sgu_mix(r["v1_bf"], sgw_ref, t // SG_CHUNK) + jnp.concatenate([sgb_ref[...]] * (t // SG_CHUNK), axis=0)
    r["um"] = r["gu"] * r["mixed"]
    r["gate_s"], r["dgate_s"] = _silu_fwd_bwd(proj[:, C_GS:D_IN])
    r["sgu_g"] = r["um"] * r["gate_s"]
    r["mc_bf"] = jnp.concatenate([r["att_g"], r["cnv_g"], r["sgu_g"]], axis=1).astype(BF16)
    return r


def _fwd_out(x, proj, o, hc, cw, cp, sp, sgw_bf, sgb, w_out_bf, g_post, t, target=None):
    s = x.shape[0]
    last_layer = target is not None

    def body(*refs):
        (x_ref, proj_ref, o_ref, hc_ref, hp_ref, hn_ref, cw_ref, cp_ref, sp_ref, sgw_ref, sgb_ref,
         w_ref, g_ref) = refs[:13]
        rest = refs[13:]
        if last_layer:
            t_ref, mix_ref, out_ref, kept_ref, sq_ref, hext_ref = rest
        else:
            mix_ref, out_ref, kept_ref, hext_ref = rest
        i = pl.program_id(0)
        _conv_window(hext_ref, hp_ref, hc_ref[...], hn_ref, i == 0, i == pl.num_programs(0) - 1, t)
        r = _groups_fwd(proj_ref, o_ref, hext_ref, cw_ref, cp_ref, sp_ref, sgw_ref, sgb_ref, t)
        kept_ref[...] = r["kept"]
        mix = _dot(r["mc_bf"], w_ref[...])
        mix_ref[...] = mix
        y = x_ref[...] + mix * _rms(mix) * g_ref[...]
        if last_layer:
            @pl.when(i == 0)
            def _():
                sq_ref[...] = jnp.zeros_like(sq_ref)

            err = y - t_ref[...]
            out_ref[...] = err * (1.0 / D_MODEL)
            sq_ref[...] += jnp.sum(err * err, axis=0, keepdims=True)
        else:
            out_ref[...] = y

    row = lambda w: pl.BlockSpec((t, w), lambda i: (i, 0))
    hprev, hnext = _halo_specs(t, s, CONV_W)
    big = jax.ShapeDtypeStruct((s, D_MODEL), F32)
    return pl.pallas_call(
        body, name="fwd_out_loss" if last_layer else "fwd_out", grid=(s // t,),
        in_specs=[row(D_MODEL), row(D_IN), pl.BlockSpec((N_HEADS, t, LANES), lambda i: (0, i, 0)), row(CONV_W),
                  hprev, hnext, _const_spec((32, CONV_W)), _const_spec((8, CONV_W)), _const_spec((8, SG_W)),
                  _const_spec((4, SG_CHUNK, SG_CHUNK)), _const_spec((SG_CHUNK, SG_W)),
                  _const_spec((D_MODEL, D_MODEL)), _const_spec((1, D_MODEL))] + ([row(D_MODEL)] if last_layer else []),
        out_specs=[row(D_MODEL), row(D_MODEL), row(KEPT_W)] + ([_const_spec((1, D_MODEL))] if last_layer else []),
        out_shape=[big, big, jax.ShapeDtypeStruct((s, KEPT_W), F32)]
        + ([jax.ShapeDtypeStruct((1, D_MODEL), F32)] if last_layer else []),
        scratch_shapes=[pltpu.VMEM((SUBLANES, t + 2 * HALO, CONV_W), F32)],
        compiler_params=_arb(),
    )(*((x, proj, o, hc, hc, hc, cw, cp, sp, sgw_bf, sgb, w_out_bf, g_post) + ((target,) if last_layer else ())))


def _bwd_out(dy, mix, proj, o, hc, kept, cw, cp, sp, sgw_bf, sgwt_bf, sgb, w_out_t_bf, g_post, t, scatter=()):
    s = dy.shape[0]
    n_chunks = t // SG_CHUNK
    ex_args, ex_in, ex_out, ex_shapes, ex_sems, ex_hook = _riding_exchange(True, scatter)
    n_ex = len(scatter)

    def body(*refs):
        (dy_ref, mix_ref, proj_ref, o_ref, hc_ref, hp_ref, hn_ref, kept_ref, cw_ref, cp_ref, sp_ref, sgw_ref,
         sgwt_ref, sgb_ref, wt_ref, g_ref) = refs[:16]
        (do_ref, dgs_ref, dc0_ref, gwo_ref, gpost_ref, gcw_ref, gvec_ref, gsgw_ref,
         gsgb_ref) = refs[16 + n_ex:25 + n_ex]
        hext_ref = refs[25 + 2 * n_ex]
        if n_ex:
            ex_hook(refs[16:16 + n_ex], refs[25 + n_ex:25 + 2 * n_ex], refs[26 + 2 * n_ex:])
        i = pl.program_id(0)

        @pl.when(i == 0)
        def _():
            for ref in (gwo_ref, gpost_ref, gcw_ref, gvec_ref, gsgw_ref, gsgb_ref):
                ref[...] = jnp.zeros_like(ref)

        _conv_window(hext_ref, hp_ref, hc_ref[...], hn_ref, i == 0, i == pl.num_programs(0) - 1, t)
        r = _groups_fwd(proj_ref, o_ref, hext_ref, cw_ref, cp_ref, sp_ref, sgw_ref, sgb_ref, t, kept=kept_ref[...])

        dyv = dy_ref[...]
        mix_v = mix_ref[...]
        rr = _rms(mix_v)
        gd = dyv * g_ref[...]
        dmix = rr * gd - mix_v * (rr * rr * rr * jnp.mean(gd * mix_v, axis=-1, keepdims=True))
        gpost_ref[...] += jnp.sum(dyv * mix_v * rr, axis=0, keepdims=True)
        dmix_bf = dmix.astype(BF16)
        gwo_ref[...] += _dot_tn(r["mc_bf"], dmix_bf)
        dmc = _dot(dmix_bf, wt_ref[...])

        d_att = dmc[:, 0:ATT_W]
        dg_att = d_att * r["att"] * r["dgate_a"]
        d_o = d_att * r["gate_a"]
        prod = d_o * r["att"]
        for p in range(N_HEADS // 2):
            sl = slice(p * LANES, (p + 1) * LANES)
            pr = prod[:, sl]
            tot = jnp.sum(pr, axis=1, keepdims=True)
            ev = jnp.sum(jnp.where(_lane(pr.shape) < HEAD_DIM, pr, 0.0), axis=1, keepdims=True)
            for odd, delta in ((False, ev), (True, tot - ev)):
                hi, mid, lo = _split3(-delta)
                do_ref[2 * p + int(odd)] = _with_spare(_head_slab(d_o[:, sl], odd), hi, mid, lo).astype(BF16)

        dcv = dmc[:, ATT_W:ATT_W + CONV_W]
        dg_conv = dcv * r["c2"] * r["dgate_c"]
        dc1 = dcv * r["gate_c"] * (r["sg_c1"] * (1.0 + r["c1"] * (1.0 - r["sg_c1"])))
        dc0 = _ln_bwd(dc1 * cp_ref[1:2, :], r["xh_c"], r["rs_c"])
        dc0_ref[...] = dc0
        for kk in range(CONV_K):
            gcw_ref[kk:kk + 1, :] += jnp.sum(dc0 * _tap(hext_ref, kk + 1, t), axis=0, keepdims=True)

        dsg = dmc[:, ATT_W + CONV_W:D_MODEL]
        dg_sg = dsg * r["um"] * r["dgate_s"]
        du = dsg * r["mixed"] * r["gate_s"] * r["dgu"]
        dmx = dsg * r["gu"] * r["gate_s"]
        dmx_bf = dmx.astype(BF16)
        sgb_sum = dmx[0:SG_CHUNK, :]
        for n in range(1, n_chunks):
            sgb_sum = sgb_sum + dmx[n * SG_CHUNK:(n + 1) * SG_CHUNK, :]
        gsgb_ref[...] += sgb_sum
        dv1_rows = []
        for n in range(n_chunks):
            pairs = []
            for p in range(SG_W // LANES):
                rs_ = slice(n * SG_CHUNK, (n + 1) * SG_CHUNK)
                ls_ = slice(p * LANES, (p + 1) * LANES)
                dm = dmx_bf[rs_, ls_]
                xp = r["v1_bf"][rs_, ls_]
                low = _lane(dm.shape) < HEAD_DIM
                zero = jnp.zeros_like(dm)
                gsgw_ref[2 * p] += _dot_nt(jnp.where(low, dm, zero), xp)
                gsgw_ref[2 * p + 1] += _dot_nt(jnp.where(low, zero, dm), xp)
                pairs.append(jnp.where(low, _dot(sgwt_ref[2 * p], dm), _dot(sgwt_ref[2 * p + 1], dm)))
            dv1_rows.append(jnp.concatenate(pairs, axis=1))
        dv1 = dv1_rows[0] if n_chunks == 1 else jnp.concatenate(dv1_rows, axis=0)
        dvs = _ln_bwd(dv1 * sp_ref[0:1, :], r["xh_s"], r["rs_s"]) * r["dgv"]

        zrow = jnp.zeros((1, CONV_W), F32)
        gvec_ref[...] += jnp.concatenate([
            jnp.sum(dc0, axis=0, keepdims=True),
            jnp.sum(dc1 * r["xh_c"], axis=0, keepdims=True),
            jnp.sum(dc1, axis=0, keepdims=True),
            jnp.sum(dv1 * r["xh_s"], axis=0, keepdims=True),
            jnp.sum(dv1, axis=0, keepdims=True),
            zrow, zrow, zrow], axis=0)
        dgs_ref[...] = jnp.concatenate([dg_att, dg_conv, du, dvs, dg_sg], axis=1)

    row = lambda w: pl.BlockSpec((t, w), lambda i: (i, 0))
    heads = pl.BlockSpec((N_HEADS, t, LANES), lambda i: (0, i, 0))
    hprev, hnext = _halo_specs(t, s, CONV_W)
    return pl.pallas_call(
        body, name="bwd_out_scatter" if n_ex else "bwd_out", grid=(s // t,),
        in_specs=[row(D_MODEL), row(D_MODEL), row(D_IN), heads, row(CONV_W), hprev, hnext, row(KEPT_W),
                  _const_spec((32, CONV_W)), _const_spec((8, CONV_W)), _const_spec((8, SG_W)),
                  _const_spec((4, SG_CHUNK, SG_CHUNK)), _const_spec((4, SG_CHUNK, SG_CHUNK)),
                  _const_spec((SG_CHUNK, SG_W)), _const_spec((D_MODEL, D_MODEL)), _const_spec((1, D_MODEL))] + ex_in,
        out_specs=[heads, row(GATES_W), row(CONV_W), _const_spec((D_MODEL, D_MODEL)), _const_spec((1, D_MODEL)),
                   _const_spec((32, CONV_W)), _const_spec((8, CONV_W)), _const_spec((4, SG_CHUNK, SG_CHUNK)),
                   _const_spec((SG_CHUNK, SG_W))] + ex_out,
        out_shape=[jax.ShapeDtypeStruct((N_HEADS, s, LANES), BF16), jax.ShapeDtypeStruct((s, GATES_W), F32),
                   jax.ShapeDtypeStruct((s, CONV_W), F32), jax.ShapeDtypeStruct((D_MODEL, D_MODEL), F32),
                   jax.ShapeDtypeStruct((1, D_MODEL), F32), jax.ShapeDtypeStruct((32, CONV_W), F32),
                   jax.ShapeDtypeStruct((8, CONV_W), F32), jax.ShapeDtypeStruct((4, SG_CHUNK, SG_CHUNK), F32),
                   jax.ShapeDtypeStruct((SG_CHUNK, SG_W), F32)] + ex_shapes,
        scratch_shapes=[pltpu.VMEM((SUBLANES, t + 2 * HALO, CONV_W), F32)] + (ex_sems if n_ex else []),
        compiler_params=_arb(),
    )(dy, mix, proj, o, hc, hc, hc, kept, cw, cp, sp, sgw_bf, sgwt_bf, sgb, w_out_t_bf, g_post, *ex_args)


def _flash_bwd(qa, doa, k, v, tq, tk):
    s = qa.shape[1]
    rows = GROUP * tq
    nk = s // tk
    n_q = s // tq

    def body(qa_ref, do_ref, k_ref, v_ref, dq_ref, dk_hbm, dv_hbm,
             dq_scr, dk_scr, dv_scr, s0, s1, d0, d1, p0, p1, e0, e1, sems):
        j, i = pl.program_id(0), pl.program_id(1)
        s_bufs, d_bufs, p_bufs, e_bufs = (s0, s1), (d0, d1), (p0, p1), (e0, e1)
        qv = qa_ref[...].reshape(rows, LANES)
        dov = do_ref[...].reshape(rows, LANES)
        q_t = qv.astype(F32).T.astype(BF16)
        do_t = dov.astype(F32).T.astype(BF16)
        dq_scr[...] = jnp.zeros((rows, LANES), F32)

        @pl.when(i == 0)
        def _():
            dk_scr[...] = jnp.zeros_like(dk_scr)
            dv_scr[...] = jnp.zeros_like(dv_scr)

        def at(c):
            return _chunk_rows(c, tk)

        def scores(c, slot):
            s_bufs[slot][...] = _dot_nt(qv, k_ref[at(c), :])
            d_bufs[slot][...] = _dot_nt(dov, v_ref[at(c), :])

        def probs(slot):
            for h in range(GROUP):
                r = slice(h * tq, (h + 1) * tq)
                p = jnp.exp(s_bufs[slot][r, :])
                p_bufs[slot][r, :] = p.astype(BF16)
                e_bufs[slot][r, :] = (p * d_bufs[slot][r, :]).astype(BF16)

        def grads(c, slot):
            ds = e_bufs[slot][...]
            dq_scr[...] += _dot(ds, k_ref[at(c), :])
            dv_scr[c] += _dot(do_t, p_bufs[slot][...])
            dk_scr[c] += _dot(q_t, ds)

        p1[...] = jnp.zeros((rows, tk), BF16)
        e1[...] = jnp.zeros((rows, tk), BF16)
        _three_stage_pipeline(nk, 4, scores, probs, grads, peel=False)
        dq_ref[...] = dq_scr[...].reshape(GROUP, tq, LANES)

        @pl.when(i == n_q - 1)
        def _():
            out = [pltpu.make_async_copy(dk_scr, dk_hbm.at[j], sems.at[0]),
                   pltpu.make_async_copy(dv_scr, dv_hbm.at[j], sems.at[1])]
            for cp in out:
                cp.start()
            for cp in out:
                cp.wait()

    qspec = pl.BlockSpec((GROUP, tq, LANES), lambda j, i: (j, i, 0))
    kvspec = pl.BlockSpec((None, s, LANES), lambda j, i: (j, 0, 0), pipeline_mode=pl.Buffered(1))
    hbm = pl.BlockSpec(memory_space=pl.ANY)
    stage_f32 = pltpu.VMEM((rows, tk), F32)
    stage_bf = pltpu.VMEM((rows, tk), BF16)
    kv_t = jax.ShapeDtypeStruct((N_KV, nk, LANES, tk), F32)
    return pl.pallas_call(
        body, name="flash_bwd", grid=(N_KV, n_q),
        in_specs=[qspec, qspec, kvspec, kvspec],
        out_specs=[qspec, hbm, hbm],
        out_shape=[jax.ShapeDtypeStruct((N_HEADS, s, LANES), F32), kv_t, kv_t],
        scratch_shapes=[pltpu.VMEM((rows, LANES), F32), pltpu.VMEM((nk, LANES, tk), F32), pltpu.VMEM((nk, LANES, tk), F32),
                        stage_f32, stage_f32, stage_f32, stage_f32, stage_bf, stage_bf, stage_bf, stage_bf,
                        pltpu.SemaphoreType.DMA((2,))],
        compiler_params=_arb(2),
    )(qa, doa, k, v)


def _bwd_in(dy, x, proj, dq, dk, dv, dgs, dc0, cw, g_pre, w_in_t_bf, bd, qg, kg, cos, sin, t):
    s = x.shape[0]

    def body(dy_ref, x_ref, proj_ref, dq_ref, dk_ref, dv_ref, dgs_ref, dc_ref, dcp_ref, dcn_ref, cw_ref, g_ref,
             wt_ref, bd_ref, qg_ref, kg_ref, cos_ref, sin_ref,
             dx_ref, dproj_ref, h_ref, gpre_ref, gq_ref, gk_ref, dext_ref):
        i = pl.program_id(0)

        @pl.when(i == 0)
        def _():
            for ref in (gpre_ref, gq_ref, gk_ref):
                ref[...] = jnp.zeros_like(ref)

        proj = proj_ref[...]
        cos_pair = cos_ref[...]
        sin_pair = sin_ref[...]
        cosv = jnp.concatenate([cos_pair] * (N_HEADS // 2), axis=1)
        sinv = jnp.concatenate([sin_pair] * (N_HEADS // 2), axis=1)

        def head_norm_bwd(dr, z, bdm, g, cs, sn, gacc_ref):
            dn = dr * cs + _swap16(dr * sn)
            rr = lax.rsqrt(_group_mean(z * z, bdm) + EPS)
            gdn = dn * g
            gacc_ref[...] += jnp.sum(dn * z * rr, axis=0, keepdims=True)
            return rr * gdn - z * (rr * rr * rr * _group_mean(gdn * z, bdm))

        dq_cat = _heads_to_cat(dq_ref, N_HEADS) * ATT_SCALE
        dzq = head_norm_bwd(dq_cat, proj[:, C_Q:C_K], bd_ref[...], qg_ref[...], cosv, sinv, gq_ref)

        def kv_pair(ref):
            return jnp.concatenate([ref[0, 0:HEAD_DIM, :], ref[1, 0:HEAD_DIM, :]], axis=0).T

        dk_cat = kv_pair(dk_ref)
        dzk = head_norm_bwd(dk_cat, proj[:, C_K:C_V], bd_ref[0:LANES, 0:LANES], kg_ref[...],
                            cos_pair, sin_pair, gk_ref)
        dv_cat = kv_pair(dv_ref)

        _conv_window(dext_ref, dcp_ref, dc_ref[...], dcn_ref, i == 0, i == pl.num_programs(0) - 1, t)
        dhc = jnp.zeros((t, CONV_W), F32)
        for kk in range(CONV_K):
            dhc = dhc + cw_ref[kk:kk + 1, :] * _tap(dext_ref, CONV_K - kk, t)
        sg = _sigmoid(proj[:, C_A2:C_GC])
        da1 = dhc * sg
        da2 = dhc * proj[:, C_A1:C_A2] * sg * (1.0 - sg)

        dgs = dgs_ref[...]
        dproj_bf = jnp.concatenate([dzq, dzk, dv_cat, dgs[:, 0:ATT_W], da1, da2, dgs[:, ATT_W:GATES_W]],
                                   axis=1).astype(BF16)
        dproj_ref[...] = dproj_bf
        dh = _dot(dproj_bf, wt_ref[...])

        xv = x_ref[...]
        rr = _rms(xv)
        gv = g_ref[...]
        h_ref[...] = (xv * rr * gv).astype(BF16)
        gdh = dh * gv
        gpre_ref[...] += jnp.sum(dh * xv * rr, axis=0, keepdims=True)
        dx_ref[...] = dy_ref[...] + rr * gdh - xv * (rr * rr * rr * jnp.mean(gdh * xv, axis=-1, keepdims=True))

    row = lambda w: pl.BlockSpec((t, w), lambda i: (i, 0))
    heads = lambda n: pl.BlockSpec((n, t, LANES), lambda i: (0, i, 0))
    hprev, hnext = _halo_specs(t, s, CONV_W)
    tk = dk.shape[3]
    assert tk % t == 0
    kv_t = pl.BlockSpec((N_KV, None, LANES, t), lambda i: (0, i // (tk // t), 0, i % (tk // t)))
    return pl.pallas_call(
        body, name="bwd_in", grid=(s // t,),
        in_specs=[row(D_MODEL), row(D_MODEL), row(D_IN), heads(N_HEADS), kv_t, kv_t, row(GATES_W),
                  row(CONV_W), hprev, hnext, _const_spec((32, CONV_W)), _const_spec((1, D_MODEL)),
                  _const_spec((D_IN, D_MODEL)), _const_spec((ATT_W, ATT_W)), _const_spec((1, ATT_W)),
                  _const_spec((1, LANES)), row(LANES), row(LANES)],
        out_specs=[row(D_MODEL), row(D_IN), row(D_MODEL), _const_spec((1, D_MODEL)), _const_spec((1, ATT_W)),
                   _const_spec((1, LANES))],
        out_shape=[jax.ShapeDtypeStruct((s, D_MODEL), F32), jax.ShapeDtypeStruct((s, D_IN), BF16),
                   jax.ShapeDtypeStruct((s, D_MODEL), BF16), jax.ShapeDtypeStruct((1, D_MODEL), F32),
                   jax.ShapeDtypeStruct((1, ATT_W), F32), jax.ShapeDtypeStruct((1, LANES), F32)],
        scratch_shapes=[pltpu.VMEM((SUBLANES, t + 2 * HALO, CONV_W), F32)],
        compiler_params=_arb(),
    )(dy, x, proj, dq, dk, dv, dgs, dc0, dc0, dc0, cw, g_pre, w_in_t_bf, bd, qg, kg, cos, sin)


def _grad_w_in(h_bf, dproj_bf, t):
    s = h_bf.shape[0]
    half = D_IN // 2

    def body(h_ref, d_ref, g_ref):
        @pl.when(pl.program_id(1) == 0)
        def _():
            g_ref[...] = jnp.zeros_like(g_ref)

        g_ref[...] += _dot_tn(h_ref[...], d_ref[...])

    return pl.pallas_call(
        body, name="grad_w_in", grid=(2, s // t),
        in_specs=[pl.BlockSpec((t, D_MODEL), lambda j, i: (i, 0)), pl.BlockSpec((t, half), lambda j, i: (i, j))],
        out_specs=pl.BlockSpec((D_MODEL, half), lambda j, i: (0, j)),
        out_shape=jax.ShapeDtypeStruct((D_MODEL, D_IN), F32),
        compiler_params=_arb(2),
    )(h_bf, dproj_bf)


def _place():
    x, y, c = lax.axis_index("x"), lax.axis_index("y"), lax.axis_index("c")
    chips = [(1 - x, y), (x, 1 - y), (1 - x, 1 - y)]
    return x, y, c, chips


def _any_specs(n):
    return [pl.BlockSpec(memory_space=pl.ANY)] * n


class _ChipExchange:
    def __init__(self, scatter, ins, outs, send_sems, recv_sems, local_sems):
        n = len(ins)
        x, y, c, chips = _place()
        mine = 2 * x + y
        src = (lambda a, piece: ins[a].at[piece]) if scatter else (lambda a, piece: ins[a])

        def copy(j, a, piece, slot, to):
            return pltpu.make_async_remote_copy(src_ref=src(a, piece), dst_ref=outs[a].at[slot],
                                                send_sem=send_sems.at[n * j + a], recv_sem=recv_sems.at[n * j + a],
                                                device_id=to, device_id_type=MESH)

        self.local = [pltpu.make_async_copy(src(a, mine), outs[a].at[mine], local_sems.at[a]) for a in range(n)]
        self.sends = [copy(j, a, 2 * px + py, mine, (px, py, c)) for j, (px, py) in enumerate(chips) for a in range(n)]
        self.arrivals = lambda: [copy(j, a, mine, 2 * px + py, (px, py, c))
                                 for j, (px, py) in enumerate(chips) for a in range(n)]

    def start(self):
        for cp in self.local + self.sends:
            cp.start()

    def finish(self):
        for cp in self.arrivals():
            cp.wait_recv()
        for cp in self.sends:
            cp.wait_send()
        for cp in self.local:
            cp.wait()

    @staticmethod
    def out_shapes(scatter, arrs):
        return [jax.ShapeDtypeStruct(a.shape if scatter else (N_CHIPS,) + a.shape, a.dtype) for a in arrs]

    @staticmethod
    def semaphores(n):
        return [pltpu.SemaphoreType.DMA((3 * n,)), pltpu.SemaphoreType.DMA((3 * n,)), pltpu.SemaphoreType.DMA((n,))]


def _exchange(scatter, arrs, name):
    n = len(arrs)

    def body(*refs):
        ex = _ChipExchange(scatter, refs[:n], refs[n:2 * n], *refs[2 * n:])
        ex.start()
        ex.finish()

    return pl.pallas_call(
        body, name=name, in_specs=_any_specs(n), out_specs=_any_specs(n),
        out_shape=_ChipExchange.out_shapes(scatter, arrs), scratch_shapes=_ChipExchange.semaphores(n),
    )(*arrs)


def _riding_exchange(scatter, arrs):
    n = len(arrs)

    def hook(ins, outs, sems):
        i = pl.program_id(0)

        @pl.when(i == 0)
        def _():
            _ChipExchange(scatter, ins, outs, *sems).start()

        @pl.when(i == pl.num_programs(0) - 1)
        def _():
            _ChipExchange(scatter, ins, outs, *sems).finish()

    return (tuple(arrs), _any_specs(n), _any_specs(n), _ChipExchange.out_shapes(scatter, arrs),
            _ChipExchange.semaphores(n), hook)


def _sum_chips(parts, rb, name):
    depth = len(parts)
    _, r, cdim = parts[0].shape

    def body(*refs):
        o_ref = refs[depth]
        for l in range(depth):
            def add(p_ref=refs[l]):
                part = lambda j: p_ref[j].astype(F32)
                o_ref[...] = ((part(0) + part(1)) + part(2)) + part(3)

            pl.when(pl.program_id(0) == l)(add)

    return pl.pallas_call(
        body, name=name, grid=(depth, r // rb),
        in_specs=[pl.BlockSpec((N_CHIPS, rb, cdim), lambda l, i: (0, i, 0))] * depth,
        out_specs=pl.BlockSpec((None, rb, cdim), lambda l, i: (l, i, 0)),
        out_shape=jax.ShapeDtypeStruct((depth, r, cdim), F32),
        compiler_params=_arb(2),
    )(*parts)


def _swap_with_sibling(a, b):
    arrs = (a, b)
    n = len(arrs)

    def body(*refs):
        ins, outs = refs[:n], refs[n:2 * n]
        send_sems, recv_sems = refs[2 * n:]
        x, y, c, _ = _place()
        cps = [pltpu.make_async_remote_copy(src_ref=ins[k], dst_ref=outs[k], send_sem=send_sems.at[k],
                                            recv_sem=recv_sems.at[k], device_id=(x, y, 1 - c), device_id_type=MESH)
               for k in range(n)]
        for cp in cps:
            cp.start()
        for cp in cps:
            cp.wait()

    return pl.pallas_call(
        body, name="swap_with_sibling",
        in_specs=_any_specs(n), out_specs=_any_specs(n),
        out_shape=[jax.ShapeDtypeStruct(v.shape, v.dtype) for v in arrs],
        scratch_shapes=[pltpu.SemaphoreType.DMA((n,)), pltpu.SemaphoreType.DMA((n,))],
    )(*arrs)


def _allreduce_small(slab):
    m, n = slab.shape

    def body(x_ref, out_ref, gath, send_sems, recv_sems, local_sem):
        x, y, c, chips = _place()
        me, sibling = (x, y, c), (x, y, 1 - c)

        def rows(px, py, pc):
            return gath.at[pl.ds(pl.multiple_of((4 * px + 2 * py + pc) * m, 8), m), :]

        def copy(k, block, to, src=None):
            return pltpu.make_async_remote_copy(src_ref=rows(*block) if src is None else src, dst_ref=rows(*block),
                                                send_sem=send_sems.at[k], recv_sem=recv_sems.at[k],
                                                device_id=to, device_id_type=MESH)

        mine = pltpu.make_async_copy(x_ref, rows(*me), local_sem)
        mine.start()
        first = [copy(0, me, sibling, src=x_ref)]
        first += [copy(1 + j, me, (*chip, c), src=x_ref) for j, chip in enumerate(chips)]
        for cp in first:
            cp.start()
        passed = [copy(4 + j, (*chip, c), sibling) for j, chip in enumerate(chips)]
        for j, chip in enumerate(chips):
            copy(1 + j, (*chip, c), me).wait_recv()
            passed[j].start()
        copy(0, sibling, me).wait_recv()
        for j, chip in enumerate(chips):
            copy(4 + j, (*chip, 1 - c), me).wait_recv()
        for cp in first + passed:
            cp.wait_send()
        mine.wait()
        total = gath[0:m, :]
        for d in range(1, N_DEV):
            total = total + gath[d * m:(d + 1) * m, :]
        out_ref[...] = total

    return pl.pallas_call(
        body, name="allreduce_small",
        in_specs=[pl.BlockSpec(memory_space=pltpu.VMEM)],
        out_specs=pl.BlockSpec(memory_space=pltpu.VMEM),
        out_shape=jax.ShapeDtypeStruct((m, n), F32),
        scratch_shapes=[pltpu.VMEM((N_DEV * m, n), F32), pltpu.SemaphoreType.DMA((7,)), pltpu.SemaphoreType.DMA((7,)),
                        pltpu.SemaphoreType.DMA],
    )(slab)


def _adamw(w, ga, gb, m, v, rb, name):
    depth, r, cdim = w.shape

    def body(w_ref, ga_ref, gb_ref, m_ref, v_ref, g_out, d_out, m_out, v_out):
        g = ga_ref[...] + gb_ref[...]
        m2 = ADAM_B1 * m_ref[...] + (1.0 - ADAM_B1) * g
        v2 = ADAM_B2 * v_ref[...] + (1.0 - ADAM_B2) * (g * g)
        m_hat = m2 / (1.0 - ADAM_B1 ** ADAM_STEP)
        v_hat = v2 / (1.0 - ADAM_B2 ** ADAM_STEP)
        g_out[...] = g
        d_out[...] = -ADAM_LR * (m_hat / (jnp.sqrt(v_hat) + ADAM_EPS) + ADAM_WD * w_ref[...])
        m_out[...] = m2
        v_out[...] = v2

    spec = pl.BlockSpec((None, rb, cdim), lambda l, i: (l, i, 0))
    shp = jax.ShapeDtypeStruct((depth, r, cdim), F32)
    return pl.pallas_call(
        body, name=name, grid=(depth, r // rb),
        in_specs=[spec] * 5, out_specs=[spec] * 4, out_shape=[shp] * 4,
        compiler_params=_arb(2),
    )(w, ga, gb, m, v)


def _rope_tables(s):
    t = jnp.arange(s, dtype=jnp.int32)
    row = (t // GRID_W).astype(F32)
    col = (t % GRID_W).astype(F32)
    half = HEAD_DIM // 4
    inv_freq = ROPE_THETA ** (-jnp.arange(half, dtype=F32) / half)
    ar = row[:, None] * inv_freq[None, :]
    ac = col[:, None] * inv_freq[None, :]
    cos = jnp.concatenate([jnp.cos(ar), jnp.cos(ar), jnp.cos(ac), jnp.cos(ac)], axis=1)
    sin = jnp.concatenate([-jnp.sin(ar), jnp.sin(ar), -jnp.sin(ac), jnp.sin(ac)], axis=1)
    return jnp.tile(cos, (1, 2)), jnp.tile(sin, (1, 2))


def _pad_rows(a, rows):
    return jnp.pad(a, ((0, rows - a.shape[0]),) + ((0, 0),) * (a.ndim - 1))


_SMALL = ("pre_norm", "post_norm", "q_norm", "k_norm", "conv_dw_b", "conv_ln_g", "conv_ln_b", "sg_ln_g", "sg_ln_b",
          "sg_w", "sg_b")


def _pack(parts):
    flat = jnp.concatenate([p.reshape(-1, LANES) for p in parts], axis=0)
    return _pad_rows(flat, -(-flat.shape[0] // 8) * 8)


def _unpack(slab, shapes):
    out, r = [], 0
    for shp in shapes:
        n = 1
        for d in shp:
            n *= d
        out.append(slab[r:r + n // LANES].reshape(shp))
        r += n // LANES
    return out


def kernel(x, pre_norm, post_norm, w_in, w_out, q_norm, k_norm, conv_dw, conv_dw_b, conv_ln_g, conv_ln_b, sg_ln_g, sg_ln_b, sg_w, sg_b, loss_target, m_pre_norm, m_post_norm, m_w_in, m_w_out, m_q_norm, m_k_norm, m_conv_dw, m_conv_dw_b, m_conv_ln_g, m_conv_ln_b, m_sg_ln_g, m_sg_ln_b, m_sg_w, m_sg_b, v_pre_norm, v_post_norm, v_w_in, v_w_out, v_q_norm, v_k_norm, v_conv_dw, v_conv_dw_b, v_conv_ln_g, v_conv_ln_b, v_sg_ln_g, v_sg_ln_b, v_sg_w, v_sg_b):
    depth = w_in.shape[0]
    s = x.shape[1]
    assert x.shape[0] == 1 and s % SG_CHUNK == 0 and x.shape[2] == D_MODEL
    tq = min(256, s)
    tq_fwd = min(512, s)
    tk = min(512, s // 2)
    t = min(256, tk)
    shard_cols = w_in.shape[2]
    chip = 2 * lax.axis_index("x") + lax.axis_index("y")

    w_in_sh, w_out_sh = w_in.astype(BF16), w_out.astype(BF16)
    whole_w_in = lambda g: jnp.concatenate([g[j] for j in range(N_CHIPS)], axis=2)
    w_in_bf = [whole_w_in(_exchange(False, (w_in_sh[0:1],), "gather_w_in_first")[0])[0]]
    riders = ((w_in_sh[1:],) if depth > 1 else ()) + (w_out_sh, conv_dw)
    sgw_bf = sg_w.astype(BF16)
    sgwt_bf = jnp.swapaxes(sg_w, 2, 3).astype(BF16)

    cos, sin = _rope_tables(s)
    bd = jnp.kron(jnp.eye(N_HEADS, dtype=F32), jnp.full((HEAD_DIM, HEAD_DIM), 1.0 / HEAD_DIM, F32)).astype(BF16)

    def head_gains(l):
        return jnp.tile(q_norm[l], N_HEADS)[None, :], jnp.tile(k_norm[l], N_KV)[None, :]

    def layer_consts(l):
        cw = _pad_rows(cdw_full[l], 32)
        cp = _pad_rows(jnp.stack([conv_dw_b[l], conv_ln_g[l], conv_ln_b[l]]), 8)
        sp = _pad_rows(jnp.stack([sg_ln_g[l], sg_ln_b[l]]), 8)
        sgb = jnp.repeat(sg_b[l].T, HEAD_DIM, axis=1)
        return cw, cp, sp, sgb

    xs = [x[0]]
    saved = []
    for l in range(depth):
        qg, kg = head_gains(l)
        outs = _fwd_in(xs[l], pre_norm[l][None, :], w_in_bf[l], bd, qg, kg, cos, sin, t, gather=riders if l == 0 else ())
        proj, hc, q, k, v = outs[:5]
        if l == 0:
            gathered = list(outs[5:])
            if depth > 1:
                w_in_bf += list(whole_w_in(gathered.pop(0)))
            w_out_bf = jnp.concatenate([gathered[0][j] for j in range(N_CHIPS)], axis=1)
            cdw_full = jnp.concatenate([gathered[1][j] for j in range(N_CHIPS)], axis=2)
            w_out_t_bf = jnp.swapaxes(w_out_bf, 1, 2)
        cw, cp, sp, sgb = layer_consts(l)
        o, qa = _flash_fwd(q, k, v, tq_fwd, tk)
        outs = _fwd_out(xs[l], proj, o, hc, cw, cp, sp, sgw_bf[l], sgb, w_out_bf[l], post_norm[l][None, :], t,
                        target=loss_target[0] if l == depth - 1 else None)
        saved.append((proj, hc, qa, k, v, o, outs[0], outs[2]))
        if l < depth - 1:
            xs.append(outs[1])
    dy, sq = outs[1], outs[3]
    loss = lax.psum(0.5 * jnp.sum(sq) / D_MODEL, ("x", "y", "c"))

    g_small = {n: [] for n in _SMALL + ("conv_dw",)}
    received = [None] * depth
    pieces = ()
    for l in reversed(range(depth)):
        cw, cp, sp, sgb = layer_consts(l)
        qg, kg = head_gains(l)
        proj, hc, qa, k, v, o, mix, kept = saved[l]
        outs = _bwd_out(dy, mix, proj, o, hc, kept, cw, cp, sp, sgw_bf[l], sgwt_bf[l], sgb, w_out_t_bf[l],
                        post_norm[l][None, :], t, scatter=pieces)
        doa, dgs, dc0, gwo, gpost, gcw, gvec, gsgw, gsgb = outs[:9]
        if pieces:
            received[l + 1] = outs[9:]
        dq, dk, dv = _flash_bwd(qa, doa, k, v, tq_fwd, tk)
        dy, dproj_bf, h_bf, gpre, gq, gk = _bwd_in(dy, xs[l], proj, dq, dk, dv, dgs, dc0, cw, pre_norm[l][None, :],
                                                  jnp.swapaxes(w_in_bf[l], 0, 1), bd, qg, kg, cos, sin, t)
        gwi = _grad_w_in(h_bf, dproj_bf, min(512, s))
        pieces = (jnp.stack([gwi[:, j * shard_cols:(j + 1) * shard_cols] for j in range(N_CHIPS)]).astype(BF16),
                  gwo.reshape(N_CHIPS, gwo.shape[0] // N_CHIPS, gwo.shape[1]).astype(BF16))
        g_small["pre_norm"].append(gpre[0])
        g_small["post_norm"].append(gpost[0])
        g_small["q_norm"].append(gq[0].reshape(N_HEADS, HEAD_DIM).sum(0))
        g_small["k_norm"].append(gk[0].reshape(N_KV, HEAD_DIM).sum(0))
        g_small["conv_dw"].append(gcw[:CONV_K])
        g_small["conv_dw_b"].append(gvec[0])
        g_small["conv_ln_g"].append(gvec[1])
        g_small["conv_ln_b"].append(gvec[2])
        g_small["sg_ln_g"].append(gvec[3])
        g_small["sg_ln_b"].append(gvec[4])
        g_small["sg_w"].append(gsgw)
        g_small["sg_b"].append(gsgb.reshape(SG_CHUNK, SG_W // HEAD_DIM, HEAD_DIM).sum(-1).T)
    grad_x = dy[None]
    g_small = {n: jnp.stack(vals[::-1]) for n, vals in g_small.items()}

    received[0] = _exchange(True, pieces, "scatter_grads")
    s_in = _sum_chips([r[0] for r in received], 256, "sum_chips_w_in")
    s_out = _sum_chips([r[1] for r in received], 256, "sum_chips_w_out")
    t_in, t_out = _swap_with_sibling(s_in, s_out)
    grad_w_in, delta_w_in, new_m_w_in, new_v_w_in = _adamw(w_in, s_in, t_in, m_w_in, v_w_in, 256, "adamw_w_in")
    grad_w_out, delta_w_out, new_m_w_out, new_v_w_out = _adamw(w_out, s_out, t_out, m_w_out, v_w_out, 256, "adamw_w_out")

    small_w = dict(pre_norm=pre_norm, post_norm=post_norm, q_norm=q_norm, k_norm=k_norm, conv_dw_b=conv_dw_b,
                   conv_ln_g=conv_ln_g, conv_ln_b=conv_ln_b, sg_ln_g=sg_ln_g, sg_ln_b=sg_ln_b, sg_w=sg_w, sg_b=sg_b)
    small_m = dict(pre_norm=m_pre_norm, post_norm=m_post_norm, q_norm=m_q_norm, k_norm=m_k_norm, conv_dw_b=m_conv_dw_b,
                   conv_ln_g=m_conv_ln_g, conv_ln_b=m_conv_ln_b, sg_ln_g=m_sg_ln_g, sg_ln_b=m_sg_ln_b, sg_w=m_sg_w,
                   sg_b=m_sg_b)
    small_v = dict(pre_norm=v_pre_norm, post_norm=v_post_norm, q_norm=v_q_norm, k_norm=v_k_norm, conv_dw_b=v_conv_dw_b,
                   conv_ln_g=v_conv_ln_g, conv_ln_b=v_conv_ln_b, sg_ln_g=v_sg_ln_g, sg_ln_b=v_sg_ln_b, sg_w=v_sg_w,
                   sg_b=v_sg_b)
    shapes = [small_w[n].shape for n in _SMALL]
    red = _allreduce_small(_pack([g_small[n] for n in _SMALL] + [g_small["conv_dw"]]))
    n_rep = sum(small_w[n].size for n in _SMALL) // LANES
    g_cdw_full = red[n_rep:n_rep + g_small["conv_dw"].size // LANES].reshape(g_small["conv_dw"].shape)
    cdw_cols = conv_dw.shape[2]
    g_cdw = lax.dynamic_slice_in_dim(g_cdw_full, chip * cdw_cols, cdw_cols, axis=2)
    g_slab = _pack([red[:n_rep], g_cdw])
    w_slab = _pack([small_w[n] for n in _SMALL] + [conv_dw])
    m_slab = _pack([small_m[n] for n in _SMALL] + [m_conv_dw])
    v_slab = _pack([small_v[n] for n in _SMALL] + [v_conv_dw])
    rows = w_slab.shape[0]
    outs = _adamw(w_slab[None], g_slab[None], jnp.zeros_like(g_slab)[None], m_slab[None], v_slab[None], rows, "adamw_small")
    unpacked = [dict(zip(_SMALL + ("conv_dw",), _unpack(o_[0], shapes + [conv_dw.shape]))) for o_ in outs]

    big = [dict(w_in=a, w_out=b) for a, b in ((grad_w_in, grad_w_out), (delta_w_in, delta_w_out),
                                             (new_m_w_in, new_m_w_out), (new_v_w_in, new_v_w_out))]
    order = ("pre_norm", "post_norm", "w_in", "w_out", "q_norm", "k_norm", "conv_dw", "conv_dw_b", "conv_ln_g",
             "conv_ln_b", "sg_ln_g", "sg_ln_b", "sg_w", "sg_b")
    result = [loss, grad_x]
    for kind in range(4):
        for name in order:
            result.append(big[kind][name] if name in big[kind] else unpacked[kind][name])
    return tuple(result)
```

```python
import jax
import jax.numpy as jnp
from jax import lax
from jax.experimental import pallas as pl
from jax.experimental.pallas import tpu as pltpu

F32 = jnp.float32
BF16 = jnp.bfloat16
MESH = pl.DeviceIdType.MESH

EPS = 1e-6
D_MODEL = 1024
D_IN = 2816
HEAD_DIM = 64
LANES = 128
SUBLANES = 8
N_HEADS = 8
N_KV = 2
GROUP = N_HEADS // N_KV
GRID_W = 64
ROPE_THETA = 10000.0
CONV_K = 31
CONV_W = 256
SG_W = 256
SG_CHUNK = 128
KEPT_W = CONV_W + 2 * SG_W
HALO = 16
ATT_SCALE = HEAD_DIM ** -0.5

ATT_W = N_HEADS * HEAD_DIM
C_Q, C_K, C_V, C_GA, C_A1, C_A2, C_GC, C_U, C_VS, C_GS = 0, 512, 640, 768, 1280, 1536, 1792, 2048, 2304, 2560
GATES_W = D_IN - C_GA - 2 * CONV_W

ADAM_LR = 0.001
ADAM_B1 = 0.9
ADAM_B2 = 0.999
ADAM_EPS = 1e-08
ADAM_WD = 0.01
ADAM_STEP = 10

N_CHIPS = 4
N_DEV = 8


def _dot(a, b):
    return jnp.dot(a, b, preferred_element_type=F32)


def _group_mean(x, bd_bf):
    hi = x.astype(BF16)
    lo = (x - hi.astype(F32)).astype(BF16)
    return _dot(hi, bd_bf) + _dot(lo, bd_bf)


def _dot_nt(a, b):
    return lax.dot_general(a, b, (((1,), (1,)), ((), ())), preferred_element_type=F32)


def _dot_tn(a, b):
    return lax.dot_general(a, b, (((0,), (0,)), ((), ())), preferred_element_type=F32)


def _lane(shape):
    return lax.broadcasted_iota(jnp.int32, shape, 1)


def _sigmoid(x):
    return 1.0 / (1.0 + jnp.exp(-x))


def _silu_fwd_bwd(x):
    s = _sigmoid(x)
    return x * s, s * (1.0 + x * (1.0 - s))


def _erf(x):
    x = jnp.clip(x, -4.0, 4.0)
    x2 = x * x
    a = -2.72614225801306e-10
    a = a * x2 + 2.77068142495902e-08
    a = a * x2 + -2.10102402082508e-06
    a = a * x2 + -5.69250639462346e-05
    a = a * x2 + -7.34990630326855e-04
    a = a * x2 + -2.95459980854025e-03
    a = a * x2 + -1.60960333262415e-02
    b = -1.45660718464996e-05
    b = b * x2 + -2.13374055278905e-04
    b = b * x2 + -1.68282697438203e-03
    b = b * x2 + -7.37332916720468e-03
    b = b * x2 + -1.42647390514189e-02
    return x * a / b


def _gelu_fwd_bwd(x, cdf=None):
    if cdf is None:
        cdf = 0.5 * (1.0 + _erf(x * 0.7071067811865476))
    pdf = jnp.exp(-0.5 * x * x) * 0.3989422804014327
    return x * cdf, cdf + x * pdf, cdf


def _rms(x):
    return lax.rsqrt(jnp.mean(x * x, axis=-1, keepdims=True) + EPS)


def _ln_hat(x):
    mu = jnp.mean(x, axis=-1, keepdims=True)
    xc = x - mu
    rs = lax.rsqrt(jnp.mean(xc * xc, axis=-1, keepdims=True) + EPS)
    return xc * rs, rs


def _ln_bwd(dxh, xh, rs):
    return rs * (dxh - jnp.mean(dxh, axis=-1, keepdims=True) - xh * jnp.mean(dxh * xh, axis=-1, keepdims=True))


def _swap16(z):
    parts = []
    for i in range(z.shape[1] // LANES):
        blk = z[:, i * LANES:(i + 1) * LANES]
        lane = _lane(blk.shape)
        parts.append(jnp.where((lane & 16) == 0, pltpu.roll(blk, LANES - 16, 1), pltpu.roll(blk, 16, 1)))
    return parts[0] if len(parts) == 1 else jnp.concatenate(parts, axis=1)


def _head_slab(pair, odd):
    src = pltpu.roll(pair, HEAD_DIM, 1) if odd else pair
    return jnp.where(_lane(pair.shape) < HEAD_DIM, src, 0.0)


def _pair_merge(even, odd):
    return jnp.where(_lane(even.shape) < HEAD_DIM, even, pltpu.roll(odd, HEAD_DIM, 1))


def _heads_to_cat(ref, n_heads):
    pairs = [_pair_merge(ref[2 * p], ref[2 * p + 1]) for p in range(n_heads // 2)]
    return pairs[0] if len(pairs) == 1 else jnp.concatenate(pairs, axis=1)


def _split3(x):
    hi = x.astype(BF16).astype(F32)
    r = x - hi
    mid = r.astype(BF16).astype(F32)
    lo = (r - mid).astype(BF16).astype(F32)
    return hi, mid, lo


def _with_spare(slab, hi, mid, lo):
    lane = _lane(slab.shape)
    return jnp.where(lane == HEAD_DIM, hi, jnp.where(lane == HEAD_DIM + 1, mid, jnp.where(lane == HEAD_DIM + 2, lo, slab)))


def _with_ones(slab):
    lane = _lane(slab.shape)
    return jnp.where((lane >= HEAD_DIM) & (lane < HEAD_DIM + 3), 1.0, slab)


def _conv_window(rot_ref, prev_ref, main, next_ref, first, last, t):
    n = t + 2 * HALO
    full = jnp.concatenate([jnp.where(first, 0.0, prev_ref[...]), main, jnp.where(last, 0.0, next_ref[...])], axis=0)
    rot_ref[0] = full
    for b in range(1, SUBLANES):
        rot_ref[b] = pltpu.roll(full, n - b, 0)


def _tap(rot_ref, start, t):
    a, b = divmod(start, SUBLANES)
    return rot_ref[b, SUBLANES * a:SUBLANES * a + t, :]


def _sgu_mix(v1_bf, w_ref, n_chunks):
    rows = []
    for n in range(n_chunks):
        pairs = []
        for p in range(SG_W // LANES):
            xp = v1_bf[n * SG_CHUNK:(n + 1) * SG_CHUNK, p * LANES:(p + 1) * LANES]
            me = _dot(w_ref[2 * p], xp)
            mo = _dot(w_ref[2 * p + 1], xp)
            pairs.append(jnp.where(_lane(me.shape) < HEAD_DIM, me, mo))
        rows.append(jnp.concatenate(pairs, axis=1))
    return rows[0] if len(rows) == 1 else jnp.concatenate(rows, axis=0)


def _halo_specs(t, s, width):
    per = t // HALO
    nblk = s // HALO
    prev = pl.BlockSpec((HALO, width), lambda i: (jnp.maximum(i * per - 1, 0), 0))
    nxt = pl.BlockSpec((HALO, width), lambda i: (jnp.minimum((i + 1) * per, nblk - 1), 0))
    return prev, nxt


def _const_spec(shape):
    nd = len(shape)
    return pl.BlockSpec(shape, lambda i: (0,) * nd)


def _arb(n=1):
    return pltpu.CompilerParams(dimension_semantics=("arbitrary",) * n)


def _fwd_in(x, g_pre, w_in_bf, bd, qg, kg, cos, sin, t, gather=()):
    s = x.shape[0]
    ex_args, ex_in, ex_out, ex_shapes, ex_sems, ex_hook = _riding_exchange(False, gather)
    n_ex = len(gather)

    def body(*refs):
        x_ref, g_ref, w_ref, bd_ref, qg_ref, kg_ref, cos_ref, sin_ref = refs[:8]
        proj_ref, hc_ref, q_ref, k_ref, v_ref = refs[8 + n_ex:13 + n_ex]
        if n_ex:
            ex_hook(refs[8:8 + n_ex], refs[13 + n_ex:13 + 2 * n_ex], refs[13 + 2 * n_ex:])
        xv = x_ref[...]
        h = (xv * _rms(xv) * g_ref[...]).astype(BF16)
        proj = _dot(h, w_ref[...])
        proj_ref[...] = proj
        cos_pair = cos_ref[...]
        sin_pair = sin_ref[...]
        cosv = jnp.concatenate([cos_pair] * (N_HEADS // 2), axis=1)
        sinv = jnp.concatenate([sin_pair] * (N_HEADS // 2), axis=1)
        q = proj[:, C_Q:C_K]
        qn = q * lax.rsqrt(_group_mean(q * q, bd_ref[...]) + EPS) * qg_ref[...]
        qr = (qn * cosv + _swap16(qn) * sinv) * ATT_SCALE
        for hh in range(N_HEADS):
            pair = qr[:, (hh // 2) * LANES:(hh // 2 + 1) * LANES]
            q_ref[hh] = _head_slab(pair, hh % 2 == 1).astype(BF16)
        k = proj[:, C_K:C_V]
        kn = k * lax.rsqrt(_group_mean(k * k, bd_ref[0:LANES, 0:LANES]) + EPS) * kg_ref[...]
        kr = kn * cos_pair + _swap16(kn) * sin_pair
        vv = proj[:, C_V:C_GA]
        for hh in range(N_KV):
            k_ref[hh] = _with_ones(_head_slab(kr, hh == 1)).astype(BF16)
            v_ref[hh] = _with_ones(_head_slab(vv, hh == 1)).astype(BF16)
        hc_ref[...] = proj[:, C_A1:C_A2] * _sigmoid(proj[:, C_A2:C_GC])

    row = lambda w: pl.BlockSpec((t, w), lambda i: (i, 0))
    heads = lambda n: pl.BlockSpec((n, t, LANES), lambda i: (0, i, 0))
    return pl.pallas_call(
        body, name="fwd_in_gather" if n_ex else "fwd_in", grid=(s // t,),
        in_specs=[row(D_MODEL), _const_spec((1, D_MODEL)), _const_spec((D_MODEL, D_IN)), _const_spec((ATT_W, ATT_W)),
                  _const_spec((1, ATT_W)), _const_spec((1, LANES)), row(LANES), row(LANES)] + ex_in,
        out_specs=[row(D_IN), row(CONV_W), heads(N_HEADS), heads(N_KV), heads(N_KV)] + ex_out,
        out_shape=[jax.ShapeDtypeStruct((s, D_IN), F32), jax.ShapeDtypeStruct((s, CONV_W), F32),
                   jax.ShapeDtypeStruct((N_HEADS, s, LANES), BF16), jax.ShapeDtypeStruct((N_KV, s, LANES), BF16),
                   jax.ShapeDtypeStruct((N_KV, s, LANES), BF16)] + ex_shapes,
        scratch_shapes=ex_sems if n_ex else [],
        compiler_params=_arb(),
    )(x, g_pre, w_in_bf, bd, qg, kg, cos, sin, *ex_args)


def _chunk_rows(c, tk):
    return pl.ds(c * tk, tk) if isinstance(c, int) else pl.ds(pl.multiple_of(c * tk, tk), tk)


def _three_stage_pipeline(nk, per_trip, stage1, stage2, stage3, peel):
    assert nk % 2 == 0 and per_trip % 2 == 0

    def step(t, parity, first=False, last=False):
        if not last:
            stage1(t + 1, 1 - parity)
        stage2(parity)
        if not first:
            stage3(t - 1, 1 - parity)

    stage1(0, 0)
    if not peel:
        while nk % per_trip:
            per_trip //= 2

        def whole_trip(i, carry):
            for u in range(per_trip):
                c = per_trip * i + u
                stage1(jnp.minimum(c + 1, nk - 1), 1 - u % 2)
                stage2(u % 2)
                stage3(jnp.maximum(c - 1, 0), 1 - u % 2)
            return carry

        lax.fori_loop(0, nk // per_trip, whole_trip, 0)
        stage3(nk - 1, 1)
        return

    step(0, 0, first=True)
    n_trips, left = divmod(nk - 2, per_trip)

    def trip(i, carry):
        for u in range(per_trip):
            step(1 + per_trip * i + u, (1 + u) % 2)
        return carry

    if n_trips:
        lax.fori_loop(0, n_trips, trip, 0)
    for t in range(1 + n_trips * per_trip, 1 + n_trips * per_trip + left):
        step(t, t % 2)
    step(nk - 1, 1, last=True)
    stage3(nk - 1, 1)


def _flash_fwd(q, k, v, tq, tk):
    s = q.shape[1]
    rows = GROUP * tq
    nk = s // tk

    def body(q_ref, k_ref, vt_ref, o_ref, qa_ref, m_scr, acc_scr, s0, s1, p0, p1, a0, a1):
        s_bufs, p_bufs, a_bufs = (s0, s1), (p0, p1), (a0, a1)
        qv = q_ref[...].reshape(rows, LANES)
        q_t = qv.astype(F32).T
        q_t_bf = q_t.astype(BF16)
        m_scr[...] = jnp.full((1, rows), -jnp.inf, F32)
        acc_scr[...] = jnp.zeros((LANES, rows), F32)

        def scores(c, slot):
            s_bufs[slot][...] = _dot(k_ref[_chunk_rows(c, tk), :], q_t_bf)

        def softmax(slot):
            for h in range(GROUP):
                r = slice(h * tq, (h + 1) * tq)
                sc = s_bufs[slot][:, r]
                m_prev = m_scr[:, r]
                m_new = jnp.maximum(m_prev, jnp.max(sc, axis=0, keepdims=True))
                p_bufs[slot][:, r] = jnp.exp((sc - m_new).astype(BF16))
                a_bufs[slot][:, r] = jnp.exp(m_prev - m_new)
                m_scr[:, r] = m_new

        def weighted_values(c, slot):
            acc_scr[...] = a_bufs[slot][...] * acc_scr[...] + _dot(vt_ref[c], p_bufs[slot][...])

        _three_stage_pipeline(nk, 4, scores, softmax, weighted_values, peel=True)

        acc = acc_scr[...]
        row = lax.broadcasted_iota(jnp.int32, acc.shape, 0)
        l = jnp.sum(jnp.where(row == HEAD_DIM, acc, 0.0), axis=0, keepdims=True)
        o_ref[...] = jnp.where(row < HEAD_DIM, acc / l, 0.0).T.reshape(GROUP, tq, LANES)
        hi, mid, lo = _split3(-(m_scr[...] + jnp.log(l)))
        qa_t = jnp.where(row == HEAD_DIM, hi, jnp.where(row == HEAD_DIM + 1, mid,
                                                        jnp.where(row == HEAD_DIM + 2, lo, q_t)))
        qa_ref[...] = qa_t.T.astype(BF16).reshape(GROUP, tq, LANES)

    qspec = pl.BlockSpec((GROUP, tq, LANES), lambda j, i: (j, i, 0))
    kspec = pl.BlockSpec((None, s, LANES), lambda j, i: (j, 0, 0))
    vtspec = pl.BlockSpec((None, nk, LANES, tk), lambda j, i: (j, 0, 0, 0))
    v_t = jnp.swapaxes(v.reshape(N_KV, nk, tk, LANES), 2, 3)
    return pl.pallas_call(
        body, name="flash_fwd", grid=(N_KV, s // tq),
        in_specs=[qspec, kspec, vtspec],
        out_specs=[qspec, qspec],
        out_shape=[jax.ShapeDtypeStruct((N_HEADS, s, LANES), F32), jax.ShapeDtypeStruct((N_HEADS, s, LANES), BF16)],
        scratch_shapes=[pltpu.VMEM((1, rows), F32), pltpu.VMEM((LANES, rows), F32),
                        pltpu.VMEM((tk, rows), F32), pltpu.VMEM((tk, rows), F32),
                        pltpu.VMEM((tk, rows), BF16), pltpu.VMEM((tk, rows), BF16),
                        pltpu.VMEM((1, rows), F32), pltpu.VMEM((1, rows), F32)],
        compiler_params=_arb(2),
    )(q, k, v_t)


def _groups_fwd(proj_ref, o_ref, hext_ref, cw_ref, cp_ref, sp_ref, sgw_ref, sgb_ref, t, kept=None):
    proj = proj_ref[...]
    r = {}
    r["att"] = _heads_to_cat(o_ref, N_HEADS)
    r["gate_a"], r["dgate_a"] = _silu_fwd_bwd(proj[:, C_GA:C_A1])
    r["att_g"] = r["att"] * r["gate_a"]
    if kept is None:
        c0 = jnp.zeros((t, CONV_W), F32) + cp_ref[0:1, :]
        for kk in range(CONV_K):
            c0 = c0 + cw_ref[kk:kk + 1, :] * _tap(hext_ref, kk + 1, t)
        cdf_u = cdf_v = None
    else:
        c0, cdf_u, cdf_v = kept[:, 0:CONV_W], kept[:, CONV_W:CONV_W + SG_W], kept[:, CONV_W + SG_W:KEPT_W]
    r["xh_c"], r["rs_c"] = _ln_hat(c0)
    r["c1"] = r["xh_c"] * cp_ref[1:2, :] + cp_ref[2:3, :]
    r["sg_c1"] = _sigmoid(r["c1"])
    r["c2"] = r["c1"] * r["sg_c1"]
    r["gate_c"], r["dgate_c"] = _silu_fwd_bwd(proj[:, C_GC:C_U])
    r["cnv_g"] = r["c2"] * r["gate_c"]
    r["gu"], r["dgu"], cdf_u = _gelu_fwd_bwd(proj[:, C_U:C_VS], cdf_u)
    gv, r["dgv"], cdf_v = _gelu_fwd_bwd(proj[:, C_VS:C_GS], cdf_v)
    r["kept"] = jnp.concatenate([c0, cdf_u, cdf_v], axis=1)
    r["xh_s"], r["rs_s"] = _ln_hat(gv)
    v1 = r["xh_s"] * sp_ref[0:1, :] + sp_ref[1:2, :]
    r["v1_bf"] = v1.astype(BF16)
    r["mixed"] = _sgu_mix(r["v1_bf"], sgw_ref, t // SG_CHUNK) + jnp.concatenate([sgb_ref[...]] * (t // SG_CHUNK), axis=0)
    r["um"] = r["gu"] * r["mixed"]
    r["gate_s"], r["dgate_s"] = _silu_fwd_bwd(proj[:, C_GS:D_IN])
    r["sgu_g"] = r["um"] * r["gate_s"]
    r["mc_bf"] = jnp.concatenate([r["att_g"], r["cnv_g"], r["sgu_g"]], axis=1).astype(BF16)
    return r


def _fwd_out(x, proj, o, hc, cw, cp, sp, sgw_bf, sgb, w_out_bf, g_post, t, target=None):
    s = x.shape[0]
    last_layer = target is not None

    def body(*refs):
        (x_ref, proj_ref, o_ref, hc_ref, hp_ref, hn_ref, cw_ref, cp_ref, sp_ref, sgw_ref, sgb_ref,
         w_ref, g_ref) = refs[:13]
        rest = refs[13:]
        if last_layer:
            t_ref, mix_ref, out_ref, kept_ref, sq_ref, hext_ref = rest
        else:
            mix_ref, out_ref, kept_ref, hext_ref = rest
        i = pl.program_id(0)
        _conv_window(hext_ref, hp_ref, hc_ref[...], hn_ref, i == 0, i == pl.num_programs(0) - 1, t)
        r = _groups_fwd(proj_ref, o_ref, hext_ref, cw_ref, cp_ref, sp_ref, sgw_ref, sgb_ref, t)
        kept_ref[...] = r["kept"]
        mix = _dot(r["mc_bf"], w_ref[...])
        mix_ref[...] = mix
        y = x_ref[...] + mix * _rms(mix) * g_ref[...]
        if last_layer:
            @pl.when(i == 0)
            def _():
                sq_ref[...] = jnp.zeros_like(sq_ref)

            err = y - t_ref[...]
            out_ref[...] = err * (1.0 / D_MODEL)
            sq_ref[...] += jnp.sum(err * err, axis=0, keepdims=True)
        else:
            out_ref[...] = y

    row = lambda w: pl.BlockSpec((t, w), lambda i: (i, 0))
    hprev, hnext = _halo_specs(t, s, CONV_W)
    big = jax.ShapeDtypeStruct((s, D_MODEL), F32)
    return pl.pallas_call(
        body, name="fwd_out_loss" if last_layer else "fwd_out", grid=(s // t,),
        in_specs=[row(D_MODEL), row(D_IN), pl.BlockSpec((N_HEADS, t, LANES), lambda i: (0, i, 0)), row(CONV_W),
                  hprev, hnext, _const_spec((32, CONV_W)), _const_spec((8, CONV_W)), _const_spec((8, SG_W)),
                  _const_spec((4, SG_CHUNK, SG_CHUNK)), _const_spec((SG_CHUNK, SG_W)),
                  _const_spec((D_MODEL, D_MODEL)), _const_spec((1, D_MODEL))] + ([row(D_MODEL)] if last_layer else []),
        out_specs=[row(D_MODEL), row(D_MODEL), row(KEPT_W)] + ([_const_spec((1, D_MODEL))] if last_layer else []),
        out_shape=[big, big, jax.ShapeDtypeStruct((s, KEPT_W), F32)]
        + ([jax.ShapeDtypeStruct((1, D_MODEL), F32)] if last_layer else []),
        scratch_shapes=[pltpu.VMEM((SUBLANES, t + 2 * HALO, CONV_W), F32)],
        compiler_params=_arb(),
    )(*((x, proj, o, hc, hc, hc, cw, cp, sp, sgw_bf, sgb, w_out_bf, g_post) + ((target,) if last_layer else ())))


def _bwd_out(dy, mix, proj, o, hc, kept, cw, cp, sp, sgw_bf, sgwt_bf, sgb, w_out_t_bf, g_post, t, scatter=()):
    s = dy.shape[0]
    n_chunks = t // SG_CHUNK
    ex_args, ex_in, ex_out, ex_shapes, ex_sems, ex_hook = _riding_exchange(True, scatter)
    n_ex = len(scatter)

    def body(*refs):
        (dy_ref, mix_ref, proj_ref, o_ref, hc_ref, hp_ref, hn_ref, kept_ref, cw_ref, cp_ref, sp_ref, sgw_ref,
         sgwt_ref, sgb_ref, wt_ref, g_ref) = refs[:16]
        (do_ref, dgs_ref, dc0_ref, gwo_ref, gpost_ref, gcw_ref, gvec_ref, gsgw_ref,
         gsgb_ref) = refs[16 + n_ex:25 + n_ex]
        hext_ref = refs[25 + 2 * n_ex]
        if n_ex:
            ex_hook(refs[16:16 + n_ex], refs[25 + n_ex:25 + 2 * n_ex], refs[26 + 2 * n_ex:])
        i = pl.program_id(0)

        @pl.when(i == 0)
        def _():
            for ref in (gwo_ref, gpost_ref, gcw_ref, gvec_ref, gsgw_ref, gsgb_ref):
                ref[...] = jnp.zeros_like(ref)

        _conv_window(hext_ref, hp_ref, hc_ref[...], hn_ref, i == 0, i == pl.num_programs(0) - 1, t)
        r = _groups_fwd(proj_ref, o_ref, hext_ref, cw_ref, cp_ref, sp_ref, sgw_ref, sgb_ref, t, kept=kept_ref[...])

        dyv = dy_ref[...]
        mix_v = mix_ref[...]
        rr = _rms(mix_v)
        gd = dyv * g_ref[...]
        dmix = rr * gd - mix_v * (rr * rr * rr * jnp.mean(gd * mix_v, axis=-1, keepdims=True))
        gpost_ref[...] += jnp.sum(dyv * mix_v * rr, axis=0, keepdims=True)
        dmix_bf = dmix.astype(BF16)
        gwo_ref[...] += _dot_tn(r["mc_bf"], dmix_bf)
        dmc = _dot(dmix_bf, wt_ref[...])

        d_att = dmc[:, 0:ATT_W]
        dg_att = d_att * r["att"] * r["dgate_a"]
        d_o = d_att * r["gate_a"]
        prod = d_o * r["att"]
        for p in range(N_HEADS // 2):
            sl = slice(p * LANES, (p + 1) * LANES)
            pr = prod[:, sl]
            tot = jnp.sum(pr, axis=1, keepdims=True)
            ev = jnp.sum(jnp.where(_lane(pr.shape) < HEAD_DIM, pr, 0.0), axis=1, keepdims=True)
            for odd, delta in ((False, ev), (True, tot - ev)):
                hi, mid, lo = _split3(-delta)
                do_ref[2 * p + int(odd)] = _with_spare(_head_slab(d_o[:, sl], odd), hi, mid, lo).astype(BF16)

        dcv = dmc[:, ATT_W:ATT_W + CONV_W]
        dg_conv = dcv * r["c2"] * r["dgate_c"]
        dc1 = dcv * r["gate_c"] * (r["sg_c1"] * (1.0 + r["c1"] * (1.0 - r["sg_c1"])))
        dc0 = _ln_bwd(dc1 * cp_ref[1:2, :], r["xh_c"], r["rs_c"])
        dc0_ref[...] = dc0
        for kk in range(CONV_K):
            gcw_ref[kk:kk + 1, :] += jnp.sum(dc0 * _tap(hext_ref, kk + 1, t), axis=0, keepdims=True)

        dsg = dmc[:, ATT_W + CONV_W:D_MODEL]
        dg_sg = dsg * r["um"] * r["dgate_s"]
        du = dsg * r["mixed"] * r["gate_s"] * r["dgu"]
        dmx = dsg * r["gu"] * r["gate_s"]
        dmx_bf = dmx.astype(BF16)
        sgb_sum = dmx[0:SG_CHUNK, :]
        for n in range(1, n_chunks):
            sgb_sum = sgb_sum + dmx[n * SG_CHUNK:(n + 1) * SG_CHUNK, :]
        gsgb_ref[...] += sgb_sum
        dv1_rows = []
        for n in range(n_chunks):
            pairs = []
            for p in range(SG_W // LANES):
                rs_ = slice(n * SG_CHUNK, (n + 1) * SG_CHUNK)
                ls_ = slice(p * LANES, (p + 1) * LANES)
                dm = dmx_bf[rs_, ls_]
                xp = r["v1_bf"][rs_, ls_]
                low = _lane(dm.shape) < HEAD_DIM
                zero = jnp.zeros_like(dm)
                gsgw_ref[2 * p] += _dot_nt(jnp.where(low, dm, zero), xp)
                gsgw_ref[2 * p + 1] += _dot_nt(jnp.where(low, zero, dm), xp)
                pairs.append(jnp.where(low, _dot(sgwt_ref[2 * p], dm), _dot(sgwt_ref[2 * p + 1], dm)))
            dv1_rows.append(jnp.concatenate(pairs, axis=1))
        dv1 = dv1_rows[0] if n_chunks == 1 else jnp.concatenate(dv1_rows, axis=0)
        dvs = _ln_bwd(dv1 * sp_ref[0:1, :], r["xh_s"], r["rs_s"]) * r["dgv"]

        zrow = jnp.zeros((1, CONV_W), F32)
        gvec_ref[...] += jnp.concatenate([
            jnp.sum(dc0, axis=0, keepdims=True),
            jnp.sum(dc1 * r["xh_c"], axis=0, keepdims=True),
            jnp.sum(dc1, axis=0, keepdims=True),
            jnp.sum(dv1 * r["xh_s"], axis=0, keepdims=True),
            jnp.sum(dv1, axis=0, keepdims=True),
            zrow, zrow, zrow], axis=0)
        dgs_ref[...] = jnp.concatenate([dg_att, dg_conv, du, dvs, dg_sg], axis=1)

    row = lambda w: pl.BlockSpec((t, w), lambda i: (i, 0))
    heads = pl.BlockSpec((N_HEADS, t, LANES), lambda i: (0, i, 0))
    hprev, hnext = _halo_specs(t, s, CONV_W)
    return pl.pallas_call(
        body, name="bwd_out_scatter" if n_ex else "bwd_out", grid=(s // t,),
        in_specs=[row(D_MODEL), row(D_MODEL), row(D_IN), heads, row(CONV_W), hprev, hnext, row(KEPT_W),
                  _const_spec((32, CONV_W)), _const_spec((8, CONV_W)), _const_spec((8, SG_W)),
                  _const_spec((4, SG_CHUNK, SG_CHUNK)), _const_spec((4, SG_CHUNK, SG_CHUNK)),
                  _const_spec((SG_CHUNK, SG_W)), _const_spec((D_MODEL, D_MODEL)), _const_spec((1, D_MODEL))] + ex_in,
        out_specs=[heads, row(GATES_W), row(CONV_W), _const_spec((D_MODEL, D_MODEL)), _const_spec((1, D_MODEL)),
                   _const_spec((32, CONV_W)), _const_spec((8, CONV_W)), _const_spec((4, SG_CHUNK, SG_CHUNK)),
                   _const_spec((SG_CHUNK, SG_W))] + ex_out,
        out_shape=[jax.ShapeDtypeStruct((N_HEADS, s, LANES), BF16), jax.ShapeDtypeStruct((s, GATES_W), F32),
                   jax.ShapeDtypeStruct((s, CONV_W), F32), jax.ShapeDtypeStruct((D_MODEL, D_MODEL), F32),
                   jax.ShapeDtypeStruct((1, D_MODEL), F32), jax.ShapeDtypeStruct((32, CONV_W), F32),
                   jax.ShapeDtypeStruct((8, CONV_W), F32), jax.ShapeDtypeStruct((4, SG_CHUNK, SG_CHUNK), F32),
                   jax.ShapeDtypeStruct((SG_CHUNK, SG_W), F32)] + ex_shapes,
        scratch_shapes=[pltpu.VMEM((SUBLANES, t + 2 * HALO, CONV_W), F32)] + (ex_sems if n_ex else []),
        compiler_params=_arb(),
    )(dy, mix, proj, o, hc, hc, hc, kept, cw, cp, sp, sgw_bf, sgwt_bf, sgb, w_out_t_bf, g_post, *ex_args)


def _flash_bwd(qa, doa, k, v, tq, tk):
    s = qa.shape[1]
    rows = GROUP * tq
    nk = s // tk
    n_q = s // tq

    def body(qa_ref, do_ref, k_ref, v_ref, dq_ref, dk_hbm, dv_hbm,
             dq_scr, dk_scr, dv_scr, s0, s1, d0, d1, p0, p1, e0, e1, sems):
        j, i = pl.program_id(0), pl.program_id(1)
        s_bufs, d_bufs, p_bufs, e_bufs = (s0, s1), (d0, d1), (p0, p1), (e0, e1)
        qv = qa_ref[...].reshape(rows, LANES)
        dov = do_ref[...].reshape(rows, LANES)
        q_t = qv.astype(F32).T.astype(BF16)
        do_t = dov.astype(F32).T.astype(BF16)
        dq_scr[...] = jnp.zeros((rows, LANES), F32)

        @pl.when(i == 0)
        def _():
            dk_scr[...] = jnp.zeros_like(dk_scr)
            dv_scr[...] = jnp.zeros_like(dv_scr)

        def at(c):
            return _chunk_rows(c, tk)

        def scores(c, slot):
            s_bufs[slot][...] = _dot_nt(qv, k_ref[at(c), :])
            d_bufs[slot][...] = _dot_nt(dov, v_ref[at(c), :])

        def probs(slot):
            for h in range(GROUP):
                r = slice(h * tq, (h + 1) * tq)
                p = jnp.exp(s_bufs[slot][r, :].astype(BF16))
                p_bufs[slot][r, :] = p
                e_bufs[slot][r, :] = (p.astype(F32) * d_bufs[slot][r, :]).astype(BF16)

        def grads(c, slot):
            ds = e_bufs[slot][...]
            dq_scr[...] += _dot(ds, k_ref[at(c), :])
            dv_scr[c] += _dot(do_t, p_bufs[slot][...])
            dk_scr[c] += _dot(q_t, ds)

        p1[...] = jnp.zeros((rows, tk), BF16)
        e1[...] = jnp.zeros((rows, tk), BF16)
        _three_stage_pipeline(nk, 4, scores, probs, grads, peel=False)
        dq_ref[...] = dq_scr[...].reshape(GROUP, tq, LANES)

        @pl.when(i == n_q - 1)
        def _():
            out = [pltpu.make_async_copy(dk_scr, dk_hbm.at[j], sems.at[0]),
                   pltpu.make_async_copy(dv_scr, dv_hbm.at[j], sems.at[1])]
            for cp in out:
                cp.start()
            for cp in out:
                cp.wait()

    qspec = pl.BlockSpec((GROUP, tq, LANES), lambda j, i: (j, i, 0))
    kvspec = pl.BlockSpec((None, s, LANES), lambda j, i: (j, 0, 0), pipeline_mode=pl.Buffered(1))
    hbm = pl.BlockSpec(memory_space=pl.ANY)
    stage_f32 = pltpu.VMEM((rows, tk), F32)
    stage_bf = pltpu.VMEM((rows, tk), BF16)
    kv_t = jax.ShapeDtypeStruct((N_KV, nk, LANES, tk), F32)
    return pl.pallas_call(
        body, name="flash_bwd", grid=(N_KV, n_q),
        in_specs=[qspec, qspec, kvspec, kvspec],
        out_specs=[qspec, hbm, hbm],
        out_shape=[jax.ShapeDtypeStruct((N_HEADS, s, LANES), F32), kv_t, kv_t],
        scratch_shapes=[pltpu.VMEM((rows, LANES), F32), pltpu.VMEM((nk, LANES, tk), F32), pltpu.VMEM((nk, LANES, tk), F32),
                        stage_f32, stage_f32, stage_f32, stage_f32, stage_bf, stage_bf, stage_bf, stage_bf,
                        pltpu.SemaphoreType.DMA((2,))],
        compiler_params=_arb(2),
    )(qa, doa, k, v)


def _bwd_in(dy, x, proj, dq, dk, dv, dgs, dc0, cw, g_pre, w_in_t_bf, bd, qg, kg, cos, sin, t):
    s = x.shape[0]

    def body(dy_ref, x_ref, proj_ref, dq_ref, dk_ref, dv_ref, dgs_ref, dc_ref, dcp_ref, dcn_ref, cw_ref, g_ref,
             wt_ref, bd_ref, qg_ref, kg_ref, cos_ref, sin_ref,
             dx_ref, dproj_ref, h_ref, gpre_ref, gq_ref, gk_ref, dext_ref):
        i = pl.program_id(0)

        @pl.when(i == 0)
        def _():
            for ref in (gpre_ref, gq_ref, gk_ref):
                ref[...] = jnp.zeros_like(ref)

        proj = proj_ref[...]
        cos_pair = cos_ref[...]
        sin_pair = sin_ref[...]
        cosv = jnp.concatenate([cos_pair] * (N_HEADS // 2), axis=1)
        sinv = jnp.concatenate([sin_pair] * (N_HEADS // 2), axis=1)

        def head_norm_bwd(dr, z, bdm, g, cs, sn, gacc_ref):
            dn = dr * cs + _swap16(dr * sn)
            rr = lax.rsqrt(_group_mean(z * z, bdm) + EPS)
            gdn = dn * g
            gacc_ref[...] += jnp.sum(dn * z * rr, axis=0, keepdims=True)
            return rr * gdn - z * (rr * rr * rr * _group_mean(gdn * z, bdm))

        dq_cat = _heads_to_cat(dq_ref, N_HEADS) * ATT_SCALE
        dzq = head_norm_bwd(dq_cat, proj[:, C_Q:C_K], bd_ref[...], qg_ref[...], cosv, sinv, gq_ref)

        def kv_pair(ref):
            return jnp.concatenate([ref[0, 0:HEAD_DIM, :], ref[1, 0:HEAD_DIM, :]], axis=0).T

        dk_cat = kv_pair(dk_ref)
        dzk = head_norm_bwd(dk_cat, proj[:, C_K:C_V], bd_ref[0:LANES, 0:LANES], kg_ref[...],
                            cos_pair, sin_pair, gk_ref)
        dv_cat = kv_pair(dv_ref)

        _conv_window(dext_ref, dcp_ref, dc_ref[...], dcn_ref, i == 0, i == pl.num_programs(0) - 1, t)
        dhc = jnp.zeros((t, CONV_W), F32)
        for kk in range(CONV_K):
            dhc = dhc + cw_ref[kk:kk + 1, :] * _tap(dext_ref, CONV_K - kk, t)
        sg = _sigmoid(proj[:, C_A2:C_GC])
        da1 = dhc * sg
        da2 = dhc * proj[:, C_A1:C_A2] * sg * (1.0 - sg)

        dgs = dgs_ref[...]
        dproj_bf = jnp.concatenate([dzq, dzk, dv_cat, dgs[:, 0:ATT_W], da1, da2, dgs[:, ATT_W:GATES_W]],
                                   axis=1).astype(BF16)
        dproj_ref[...] = dproj_bf
        dh = _dot(dproj_bf, wt_ref[...])

        xv = x_ref[...]
        rr = _rms(xv)
        gv = g_ref[...]
        h_ref[...] = (xv * rr * gv).astype(BF16)
        gdh = dh * gv
        gpre_ref[...] += jnp.sum(dh * xv * rr, axis=0, keepdims=True)
        dx_ref[...] = dy_ref[...] + rr * gdh - xv * (rr * rr * rr * jnp.mean(gdh * xv, axis=-1, keepdims=True))

    row = lambda w: pl.BlockSpec((t, w), lambda i: (i, 0))
    heads = lambda n: pl.BlockSpec((n, t, LANES), lambda i: (0, i, 0))
    hprev, hnext = _halo_specs(t, s, CONV_W)
    tk = dk.shape[3]
    assert tk % t == 0
    kv_t = pl.BlockSpec((N_KV, None, LANES, t), lambda i: (0, i // (tk // t), 0, i % (tk // t)))
    return pl.pallas_call(
        body, name="bwd_in", grid=(s // t,),
        in_specs=[row(D_MODEL), row(D_MODEL), row(D_IN), heads(N_HEADS), kv_t, kv_t, row(GATES_W),
                  row(CONV_W), hprev, hnext, _const_spec((32, CONV_W)), _const_spec((1, D_MODEL)),
                  _const_spec((D_IN, D_MODEL)), _const_spec((ATT_W, ATT_W)), _const_spec((1, ATT_W)),
                  _const_spec((1, LANES)), row(LANES), row(LANES)],
        out_specs=[row(D_MODEL), row(D_IN), row(D_MODEL), _const_spec((1, D_MODEL)), _const_spec((1, ATT_W)),
                   _const_spec((1, LANES))],
        out_shape=[jax.ShapeDtypeStruct((s, D_MODEL), F32), jax.ShapeDtypeStruct((s, D_IN), BF16),
                   jax.ShapeDtypeStruct((s, D_MODEL), BF16), jax.ShapeDtypeStruct((1, D_MODEL), F32),
                   jax.ShapeDtypeStruct((1, ATT_W), F32), jax.ShapeDtypeStruct((1, LANES), F32)],
        scratch_shapes=[pltpu.VMEM((SUBLANES, t + 2 * HALO, CONV_W), F32)],
        compiler_params=_arb(),
    )(dy, x, proj, dq, dk, dv, dgs, dc0, dc0, dc0, cw, g_pre, w_in_t_bf, bd, qg, kg, cos, sin)


def _grad_w_in(h_bf, dproj_bf, t):
    s = h_bf.shape[0]
    half = D_IN // 2

    def body(h_ref, d_ref, g_ref):
        @pl.when(pl.program_id(1) == 0)
        def _():
            g_ref[...] = jnp.zeros_like(g_ref)

        g_ref[...] += _dot_tn(h_ref[...], d_ref[...])

    return pl.pallas_call(
        body, name="grad_w_in", grid=(2, s // t),
        in_specs=[pl.BlockSpec((t, D_MODEL), lambda j, i: (i, 0)), pl.BlockSpec((t, half), lambda j, i: (i, j))],
        out_specs=pl.BlockSpec((D_MODEL, half), lambda j, i: (0, j)),
        out_shape=jax.ShapeDtypeStruct((D_MODEL, D_IN), F32),
        compiler_params=_arb(2),
    )(h_bf, dproj_bf)


def _place():
    x, y, c = lax.axis_index("x"), lax.axis_index("y"), lax.axis_index("c")
    chips = [(1 - x, y), (x, 1 - y), (1 - x, 1 - y)]
    return x, y, c, chips


def _any_specs(n):
    return [pl.BlockSpec(memory_space=pl.ANY)] * n


class _ChipExchange:
    def __init__(self, scatter, ins, outs, send_sems, recv_sems, local_sems):
        n = len(ins)
        x, y, c, chips = _place()
        mine = 2 * x + y
        src = (lambda a, piece: ins[a].at[piece]) if scatter else (lambda a, piece: ins[a])

        def copy(j, a, piece, slot, to):
            return pltpu.make_async_remote_copy(src_ref=src(a, piece), dst_ref=outs[a].at[slot],
                                                send_sem=send_sems.at[n * j + a], recv_sem=recv_sems.at[n * j + a],
                                                device_id=to, device_id_type=MESH)

        self.local = [pltpu.make_async_copy(src(a, mine), outs[a].at[mine], local_sems.at[a]) for a in range(n)]
        self.sends = [copy(j, a, 2 * px + py, mine, (px, py, c)) for j, (px, py) in enumerate(chips) for a in range(n)]
        self.arrivals = lambda: [copy(j, a, mine, 2 * px + py, (px, py, c))
                                 for j, (px, py) in enumerate(chips) for a in range(n)]

    def start(self):
        for cp in self.local + self.sends:
            cp.start()

    def finish(self):
        for cp in self.arrivals():
            cp.wait_recv()
        for cp in self.sends:
            cp.wait_send()
        for cp in self.local:
            cp.wait()

    @staticmethod
    def out_shapes(scatter, arrs):
        return [jax.ShapeDtypeStruct(a.shape if scatter else (N_CHIPS,) + a.shape, a.dtype) for a in arrs]

    @staticmethod
    def semaphores(n):
        return [pltpu.SemaphoreType.DMA((3 * n,)), pltpu.SemaphoreType.DMA((3 * n,)), pltpu.SemaphoreType.DMA((n,))]


def _exchange(scatter, arrs, name):
    n = len(arrs)

    def body(*refs):
        ex = _ChipExchange(scatter, refs[:n], refs[n:2 * n], *refs[2 * n:])
        ex.start()
        ex.finish()

    return pl.pallas_call(
        body, name=name, in_specs=_any_specs(n), out_specs=_any_specs(n),
        out_shape=_ChipExchange.out_shapes(scatter, arrs), scratch_shapes=_ChipExchange.semaphores(n),
    )(*arrs)


def _riding_exchange(scatter, arrs):
    n = len(arrs)

    def hook(ins, outs, sems):
        i = pl.program_id(0)

        @pl.when(i == 0)
        def _():
            _ChipExchange(scatter, ins, outs, *sems).start()

        @pl.when(i == pl.num_programs(0) - 1)
        def _():
            _ChipExchange(scatter, ins, outs, *sems).finish()

    return (tuple(arrs), _any_specs(n), _any_specs(n), _ChipExchange.out_shapes(scatter, arrs),
            _ChipExchange.semaphores(n), hook)


def _sum_chips(parts, rb, name):
    depth = len(parts)
    _, r, cdim = parts[0].shape

    def body(*refs):
        o_ref = refs[depth]
        for l in range(depth):
            def add(p_ref=refs[l]):
                part = lambda j: p_ref[j].astype(F32)
                o_ref[...] = ((part(0) + part(1)) + part(2)) + part(3)

            pl.when(pl.program_id(0) == l)(add)

    return pl.pallas_call(
        body, name=name, grid=(depth, r // rb),
        in_specs=[pl.BlockSpec((N_CHIPS, rb, cdim), lambda l, i: (0, i, 0))] * depth,
        out_specs=pl.BlockSpec((None, rb, cdim), lambda l, i: (l, i, 0)),
        out_shape=jax.ShapeDtypeStruct((depth, r, cdim), F32),
        compiler_params=_arb(2),
    )(*parts)


def _swap_with_sibling(a, b):
    arrs = (a, b)
    n = len(arrs)

    def body(*refs):
        ins, outs = refs[:n], refs[n:2 * n]
        send_sems, recv_sems = refs[2 * n:]
        x, y, c, _ = _place()
        cps = [pltpu.make_async_remote_copy(src_ref=ins[k], dst_ref=outs[k], send_sem=send_sems.at[k],
                                            recv_sem=recv_sems.at[k], device_id=(x, y, 1 - c), device_id_type=MESH)
               for k in range(n)]
        for cp in cps:
            cp.start()
        for cp in cps:
            cp.wait()

    return pl.pallas_call(
        body, name="swap_with_sibling",
        in_specs=_any_specs(n), out_specs=_any_specs(n),
        out_shape=[jax.ShapeDtypeStruct(v.shape, v.dtype) for v in arrs],
        scratch_shapes=[pltpu.SemaphoreType.DMA((n,)), pltpu.SemaphoreType.DMA((n,))],
    )(*arrs)


def _allreduce_small(slab):
    m, n = slab.shape

    def body(x_ref, out_ref, gath, send_sems, recv_sems, local_sem):
        x, y, c, chips = _place()
        me, sibling = (x, y, c), (x, y, 1 - c)

        def rows(px, py, pc):
            return gath.at[pl.ds(pl.multiple_of((4 * px + 2 * py + pc) * m, 8), m), :]

        def copy(k, block, to, src=None):
            return pltpu.make_async_remote_copy(src_ref=rows(*block) if src is None else src, dst_ref=rows(*block),
                                                send_sem=send_sems.at[k], recv_sem=recv_sems.at[k],
                                                device_id=to, device_id_type=MESH)

        mine = pltpu.make_async_copy(x_ref, rows(*me), local_sem)
        mine.start()
        first = [copy(0, me, sibling, src=x_ref)]
        first += [copy(1 + j, me, (*chip, c), src=x_ref) for j, chip in enumerate(chips)]
        for cp in first:
            cp.start()
        passed = [copy(4 + j, (*chip, c), sibling) for j, chip in enumerate(chips)]
        for j, chip in enumerate(chips):
            copy(1 + j, (*chip, c), me).wait_recv()
            passed[j].start()
        copy(0, sibling, me).wait_recv()
        for j, chip in enumerate(chips):
            copy(4 + j, (*chip, 1 - c), me).wait_recv()
        for cp in first + passed:
            cp.wait_send()
        mine.wait()
        total = gath[0:m, :]
        for d in range(1, N_DEV):
            total = total + gath[d * m:(d + 1) * m, :]
        out_ref[...] = total

    return pl.pallas_call(
        body, name="allreduce_small",
        in_specs=[pl.BlockSpec(memory_space=pltpu.VMEM)],
        out_specs=pl.BlockSpec(memory_space=pltpu.VMEM),
        out_shape=jax.ShapeDtypeStruct((m, n), F32),
        scratch_shapes=[pltpu.VMEM((N_DEV * m, n), F32), pltpu.SemaphoreType.DMA((7,)), pltpu.SemaphoreType.DMA((7,)),
                        pltpu.SemaphoreType.DMA],
    )(slab)


def _adamw(w, ga, gb, m, v, rb, name):
    depth, r, cdim = w.shape

    def body(w_ref, ga_ref, gb_ref, m_ref, v_ref, g_out, d_out, m_out, v_out):
        g = ga_ref[...] + gb_ref[...]
        m2 = ADAM_B1 * m_ref[...] + (1.0 - ADAM_B1) * g
        v2 = ADAM_B2 * v_ref[...] + (1.0 - ADAM_B2) * (g * g)
        m_hat = m2 / (1.0 - ADAM_B1 ** ADAM_STEP)
        v_hat = v2 / (1.0 - ADAM_B2 ** ADAM_STEP)
        g_out[...] = g
        d_out[...] = -ADAM_LR * (m_hat / (jnp.sqrt(v_hat) + ADAM_EPS) + ADAM_WD * w_ref[...])
        m_out[...] = m2
        v_out[...] = v2

    spec = pl.BlockSpec((None, rb, cdim), lambda l, i: (l, i, 0))
    shp = jax.ShapeDtypeStruct((depth, r, cdim), F32)
    return pl.pallas_call(
        body, name=name, grid=(depth, r // rb),
        in_specs=[spec] * 5, out_specs=[spec] * 4, out_shape=[shp] * 4,
        compiler_params=_arb(2),
    )(w, ga, gb, m, v)


def _rope_tables(s):
    t = jnp.arange(s, dtype=jnp.int32)
    row = (t // GRID_W).astype(F32)
    col = (t % GRID_W).astype(F32)
    half = HEAD_DIM // 4
    inv_freq = ROPE_THETA ** (-jnp.arange(half, dtype=F32) / half)
    ar = row[:, None] * inv_freq[None, :]
    ac = col[:, None] * inv_freq[None, :]
    cos = jnp.concatenate([jnp.cos(ar), jnp.cos(ar), jnp.cos(ac), jnp.cos(ac)], axis=1)
    sin = jnp.concatenate([-jnp.sin(ar), jnp.sin(ar), -jnp.sin(ac), jnp.sin(ac)], axis=1)
    return jnp.tile(cos, (1, 2)), jnp.tile(sin, (1, 2))


def _pad_rows(a, rows):
    return jnp.pad(a, ((0, rows - a.shape[0]),) + ((0, 0),) * (a.ndim - 1))


_SMALL = ("pre_norm", "post_norm", "q_norm", "k_norm", "conv_dw_b", "conv_ln_g", "conv_ln_b", "sg_ln_g", "sg_ln_b",
          "sg_w", "sg_b")


def _pack(parts):
    flat = jnp.concatenate([p.reshape(-1, LANES) for p in parts], axis=0)
    return _pad_rows(flat, -(-flat.shape[0] // 8) * 8)


def _unpack(slab, shapes):
    out, r = [], 0
    for shp in shapes:
        n = 1
        for d in shp:
            n *= d
        out.append(slab[r:r + n // LANES].reshape(shp))
        r += n // LANES
    return out


def kernel(x, pre_norm, post_norm, w_in, w_out, q_norm, k_norm, conv_dw, conv_dw_b, conv_ln_g, conv_ln_b, sg_ln_g, sg_ln_b, sg_w, sg_b, loss_target, m_pre_norm, m_post_norm, m_w_in, m_w_out, m_q_norm, m_k_norm, m_conv_dw, m_conv_dw_b, m_conv_ln_g, m_conv_ln_b, m_sg_ln_g, m_sg_ln_b, m_sg_w, m_sg_b, v_pre_norm, v_post_norm, v_w_in, v_w_out, v_q_norm, v_k_norm, v_conv_dw, v_conv_dw_b, v_conv_ln_g, v_conv_ln_b, v_sg_ln_g, v_sg_ln_b, v_sg_w, v_sg_b):
    depth = w_in.shape[0]
    s = x.shape[1]
    assert x.shape[0] == 1 and s % SG_CHUNK == 0 and x.shape[2] == D_MODEL
    tq = min(512, s)
    tk = min(512, s // 2)
    t = min(256, tk)
    shard_cols = w_in.shape[2]
    chip = 2 * lax.axis_index("x") + lax.axis_index("y")

    w_in_sh, w_out_sh = w_in.astype(BF16), w_out.astype(BF16)
    whole_w_in = lambda g: jnp.concatenate([g[j] for j in range(N_CHIPS)], axis=2)
    w_in_bf = [whole_w_in(_exchange(False, (w_in_sh[0:1],), "gather_w_in_first")[0])[0]]
    riders = ((w_in_sh[1:],) if depth > 1 else ()) + (w_out_sh, conv_dw)
    sgw_bf = sg_w.astype(BF16)
    sgwt_bf = jnp.swapaxes(sg_w, 2, 3).astype(BF16)

    cos, sin = _rope_tables(s)
    bd = jnp.kron(jnp.eye(N_HEADS, dtype=F32), jnp.full((HEAD_DIM, HEAD_DIM), 1.0 / HEAD_DIM, F32)).astype(BF16)

    def head_gains(l):
        return jnp.tile(q_norm[l], N_HEADS)[None, :], jnp.tile(k_norm[l], N_KV)[None, :]

    def layer_consts(l):
        cw = _pad_rows(cdw_full[l], 32)
        cp = _pad_rows(jnp.stack([conv_dw_b[l], conv_ln_g[l], conv_ln_b[l]]), 8)
        sp = _pad_rows(jnp.stack([sg_ln_g[l], sg_ln_b[l]]), 8)
        sgb = jnp.repeat(sg_b[l].T, HEAD_DIM, axis=1)
        return cw, cp, sp, sgb

    xs = [x[0]]
    saved = []
    for l in range(depth):
        qg, kg = head_gains(l)
        outs = _fwd_in(xs[l], pre_norm[l][None, :], w_in_bf[l], bd, qg, kg, cos, sin, t, gather=riders if l == 0 else ())
        proj, hc, q, k, v = outs[:5]
        if l == 0:
            gathered = list(outs[5:])
            if depth > 1:
                w_in_bf += list(whole_w_in(gathered.pop(0)))
            w_out_bf = jnp.concatenate([gathered[0][j] for j in range(N_CHIPS)], axis=1)
            cdw_full = jnp.concatenate([gathered[1][j] for j in range(N_CHIPS)], axis=2)
            w_out_t_bf = jnp.swapaxes(w_out_bf, 1, 2)
        cw, cp, sp, sgb = layer_consts(l)
        o, qa = _flash_fwd(q, k, v, tq, tk)
        outs = _fwd_out(xs[l], proj, o, hc, cw, cp, sp, sgw_bf[l], sgb, w_out_bf[l], post_norm[l][None, :], t,
                        target=loss_target[0] if l == depth - 1 else None)
        saved.append((proj, hc, qa, k, v, o, outs[0], outs[2]))
        if l < depth - 1:
            xs.append(outs[1])
    dy, sq = outs[1], outs[3]
    loss = lax.psum(0.5 * jnp.sum(sq) / D_MODEL, ("x", "y", "c"))

    g_small = {n: [] for n in _SMALL + ("conv_dw",)}
    received = [None] * depth
    pieces = ()
    for l in reversed(range(depth)):
        cw, cp, sp, sgb = layer_consts(l)
        qg, kg = head_gains(l)
        proj, hc, qa, k, v, o, mix, kept = saved[l]
        outs = _bwd_out(dy, mix, proj, o, hc, kept, cw, cp, sp, sgw_bf[l], sgwt_bf[l], sgb, w_out_t_bf[l],
                        post_norm[l][None, :], t, scatter=pieces)
        doa, dgs, dc0, gwo, gpost, gcw, gvec, gsgw, gsgb = outs[:9]
        if pieces:
            received[l + 1] = outs[9:]
        dq, dk, dv = _flash_bwd(qa, doa, k, v, tq, tk)
        dy, dproj_bf, h_bf, gpre, gq, gk = _bwd_in(dy, xs[l], proj, dq, dk, dv, dgs, dc0, cw, pre_norm[l][None, :],
                                                  jnp.swapaxes(w_in_bf[l], 0, 1), bd, qg, kg, cos, sin, t)
        gwi = _grad_w_in(h_bf, dproj_bf, min(512, s))
        pieces = (jnp.stack([gwi[:, j * shard_cols:(j + 1) * shard_cols] for j in range(N_CHIPS)]).astype(BF16),
                  gwo.reshape(N_CHIPS, gwo.shape[0] // N_CHIPS, gwo.shape[1]).astype(BF16))
        g_small["pre_norm"].append(gpre[0])
        g_small["post_norm"].append(gpost[0])
        g_small["q_norm"].append(gq[0].reshape(N_HEADS, HEAD_DIM).sum(0))
        g_small["k_norm"].append(gk[0].reshape(N_KV, HEAD_DIM).sum(0))
        g_small["conv_dw"].append(gcw[:CONV_K])
        g_small["conv_dw_b"].append(gvec[0])
        g_small["conv_ln_g"].append(gvec[1])
        g_small["conv_ln_b"].append(gvec[2])
        g_small["sg_ln_g"].append(gvec[3])
        g_small["sg_ln_b"].append(gvec[4])
        g_small["sg_w"].append(gsgw)
        g_small["sg_b"].append(gsgb.reshape(SG_CHUNK, SG_W // HEAD_DIM, HEAD_DIM).sum(-1).T)
    grad_x = dy[None]
    g_small = {n: jnp.stack(vals[::-1]) for n, vals in g_small.items()}

    received[0] = _exchange(True, pieces, "scatter_grads")
    s_in = _sum_chips([r[0] for r in received], 256, "sum_chips_w_in")
    s_out = _sum_chips([r[1] for r in received], 256, "sum_chips_w_out")
    t_in, t_out = _swap_with_sibling(s_in, s_out)
    grad_w_in, delta_w_in, new_m_w_in, new_v_w_in = _adamw(w_in, s_in, t_in, m_w_in, v_w_in, 256, "adamw_w_in")
    grad_w_out, delta_w_out, new_m_w_out, new_v_w_out = _adamw(w_out, s_out, t_out, m_w_out, v_w_out, 256, "adamw_w_out")

    small_w = dict(pre_norm=pre_norm, post_norm=post_norm, q_norm=q_norm, k_norm=k_norm, conv_dw_b=conv_dw_b,
                   conv_ln_g=conv_ln_g, conv_ln_b=conv_ln_b, sg_ln_g=sg_ln_g, sg_ln_b=sg_ln_b, sg_w=sg_w, sg_b=sg_b)
    small_m = dict(pre_norm=m_pre_norm, post_norm=m_post_norm, q_norm=m_q_norm, k_norm=m_k_norm, conv_dw_b=m_conv_dw_b,
                   conv_ln_g=m_conv_ln_g, conv_ln_b=m_conv_ln_b, sg_ln_g=m_sg_ln_g, sg_ln_b=m_sg_ln_b, sg_w=m_sg_w,
                   sg_b=m_sg_b)
    small_v = dict(pre_norm=v_pre_norm, post_norm=v_post_norm, q_norm=v_q_norm, k_norm=v_k_norm, conv_dw_b=v_conv_dw_b,
                   conv_ln_g=v_conv_ln_g, conv_ln_b=v_conv_ln_b, sg_ln_g=v_sg_ln_g, sg_ln_b=v_sg_ln_b, sg_w=v_sg_w,
                   sg_b=v_sg_b)
    shapes = [small_w[n].shape for n in _SMALL]
    red = _allreduce_small(_pack([g_small[n] for n in _SMALL] + [g_small["conv_dw"]]))
    n_rep = sum(small_w[n].size for n in _SMALL) // LANES
    g_cdw_full = red[n_rep:n_rep + g_small["conv_dw"].size // LANES].reshape(g_small["conv_dw"].shape)
    cdw_cols = conv_dw.shape[2]
    g_cdw = lax.dynamic_slice_in_dim(g_cdw_full, chip * cdw_cols, cdw_cols, axis=2)
    g_slab = _pack([red[:n_rep], g_cdw])
    w_slab = _pack([small_w[n] for n in _SMALL] + [conv_dw])
    m_slab = _pack([small_m[n] for n in _SMALL] + [m_conv_dw])
    v_slab = _pack([small_v[n] for n in _SMALL] + [v_conv_dw])
    rows = w_slab.shape[0]
    outs = _adamw(w_slab[None], g_slab[None], jnp.zeros_like(g_slab)[None], m_slab[None], v_slab[None], rows, "adamw_small")
    unpacked = [dict(zip(_SMALL + ("conv_dw",), _unpack(o_[0], shapes + [conv_dw.shape]))) for o_ in outs]

    big = [dict(w_in=a, w_out=b) for a, b in ((grad_w_in, grad_w_out), (delta_w_in, delta_w_out),
                                             (new_m_w_in, new_m_w_out), (new_v_w_in, new_v_w_out))]
    order = ("pre_norm", "post_norm", "w_in", "w_out", "q_norm", "k_norm", "conv_dw", "conv_dw_b", "conv_ln_g",
             "conv_ln_b", "sg_ln_g", "sg_ln_b", "sg_w", "sg_b")
    result = [loss, grad_x]
    for kind in range(4):
        for name in order:
            result.append(big[kind][name] if name in big[kind] else unpacked[kind][name])
    return tuple(result)
```

```python
import jax
import jax.numpy as jnp
from jax import lax
from jax.experimental import pallas as pl
from jax.experimental.pallas import tpu as pltpu

F32 = jnp.float32
BF16 = jnp.bfloat16
MESH = pl.DeviceIdType.MESH

EPS = 1e-6
D_MODEL = 1024
D_IN = 2816
HEAD_DIM = 64
LANES = 128
SUBLANES = 8
N_HEADS = 8
N_KV = 2
GROUP = N_HEADS // N_KV
GRID_W = 64
ROPE_THETA = 10000.0
CONV_K = 31
CONV_W = 256
SG_W = 256
SG_CHUNK = 128
KEPT_W = CONV_W + 2 * SG_W
HALO = 16
ATT_SCALE = HEAD_DIM ** -0.5

ATT_W = N_HEADS * HEAD_DIM
C_Q, C_K, C_V, C_GA, C_A1, C_A2, C_GC, C_U, C_VS, C_GS = 0, 512, 640, 768, 1280, 1536, 1792, 2048, 2304, 2560
GATES_W = D_IN - C_GA - 2 * CONV_W

ADAM_LR = 0.001
ADAM_B1 = 0.9
ADAM_B2 = 0.999
ADAM_EPS = 1e-08
ADAM_WD = 0.01
ADAM_STEP = 10

N_CHIPS = 4
N_DEV = 8


def _dot(a, b):
    return jnp.dot(a, b, preferred_element_type=F32)


def _group_mean(x, bd_bf):
    hi = x.astype(BF16)
    lo = (x - hi.astype(F32)).astype(BF16)
    return _dot(hi, bd_bf) + _dot(lo, bd_bf)


def _dot_nt(a, b):
    return lax.dot_general(a, b, (((1,), (1,)), ((), ())), preferred_element_type=F32)


def _dot_tn(a, b):
    return lax.dot_general(a, b, (((0,), (0,)), ((), ())), preferred_element_type=F32)


def _lane(shape):
    return lax.broadcasted_iota(jnp.int32, shape, 1)


def _sigmoid(x):
    return 1.0 / (1.0 + jnp.exp(-x))


def _silu_fwd_bwd(x):
    s = _sigmoid(x)
    return x * s, s * (1.0 + x * (1.0 - s))


def _erf(x):
    x = jnp.clip(x, -4.0, 4.0)
    x2 = x * x
    a = -2.72614225801306e-10
    a = a * x2 + 2.77068142495902e-08
    a = a * x2 + -2.10102402082508e-06
    a = a * x2 + -5.69250639462346e-05
    a = a * x2 + -7.34990630326855e-04
    a = a * x2 + -2.95459980854025e-03
    a = a * x2 + -1.60960333262415e-02
    b = -1.45660718464996e-05
    b = b * x2 + -2.13374055278905e-04
    b = b * x2 + -1.68282697438203e-03
    b = b * x2 + -7.37332916720468e-03
    b = b * x2 + -1.42647390514189e-02
    return x * a / b


def _gelu_fwd_bwd(x, cdf=None):
    if cdf is None:
        cdf = 0.5 * (1.0 + _erf(x * 0.7071067811865476))
    pdf = jnp.exp(-0.5 * x * x) * 0.3989422804014327
    return x * cdf, cdf + x * pdf, cdf


def _rms(x):
    return lax.rsqrt(jnp.mean(x * x, axis=-1, keepdims=True) + EPS)


def _ln_hat(x):
    mu = jnp.mean(x, axis=-1, keepdims=True)
    xc = x - mu
    rs = lax.rsqrt(jnp.mean(xc * xc, axis=-1, keepdims=True) + EPS)
    return xc * rs, rs


def _ln_bwd(dxh, xh, rs):
    return rs * (dxh - jnp.mean(dxh, axis=-1, keepdims=True) - xh * jnp.mean(dxh * xh, axis=-1, keepdims=True))


def _swap16(z):
    parts = []
    for i in range(z.shape[1] // LANES):
        blk = z[:, i * LANES:(i + 1) * LANES]
        lane = _lane(blk.shape)
        parts.append(jnp.where((lane & 16) == 0, pltpu.roll(blk, LANES - 16, 1), pltpu.roll(blk, 16, 1)))
    return parts[0] if len(parts) == 1 else jnp.concatenate(parts, axis=1)


def _head_slab(pair, odd):
    src = pltpu.roll(pair, HEAD_DIM, 1) if odd else pair
    return jnp.where(_lane(pair.shape) < HEAD_DIM, src, 0.0)


def _pair_merge(even, odd):
    return jnp.where(_lane(even.shape) < HEAD_DIM, even, pltpu.roll(odd, HEAD_DIM, 1))


def _heads_to_cat(ref, n_heads):
    pairs = [_pair_merge(ref[2 * p], ref[2 * p + 1]) for p in range(n_heads // 2)]
    return pairs[0] if len(pairs) == 1 else jnp.concatenate(pairs, axis=1)


def _split3(x):
    hi = x.astype(BF16).astype(F32)
    r = x - hi
    mid = r.astype(BF16).astype(F32)
    lo = (r - mid).astype(BF16).astype(F32)
    return hi, mid, lo


def _with_spare(slab, hi, mid, lo):
    lane = _lane(slab.shape)
    return jnp.where(lane == HEAD_DIM, hi, jnp.where(lane == HEAD_DIM + 1, mid, jnp.where(lane == HEAD_DIM + 2, lo, slab)))


def _with_ones(slab):
    lane = _lane(slab.shape)
    return jnp.where((lane >= HEAD_DIM) & (lane < HEAD_DIM + 3), 1.0, slab)


def _conv_window(rot_ref, prev_ref, main, next_ref, first, last, t):
    n = t + 2 * HALO
    full = jnp.concatenate([jnp.where(first, 0.0, prev_ref[...]), main, jnp.where(last, 0.0, next_ref[...])], axis=0)
    rot_ref[0] = full
    for b in range(1, SUBLANES):
        rot_ref[b] = pltpu.roll(full, n - b, 0)


def _tap(rot_ref, start, t):
    a, b = divmod(start, SUBLANES)
    return rot_ref[b, SUBLANES * a:SUBLANES * a + t, :]


def _sgu_mix(v1_bf, w_ref, n_chunks):
    rows = []
    for n in range(n_chunks):
        pairs = []
        for p in range(SG_W // LANES):
            xp = v1_bf[n * SG_CHUNK:(n + 1) * SG_CHUNK, p * LANES:(p + 1) * LANES]
            me = _dot(w_ref[2 * p], xp)
            mo = _dot(w_ref[2 * p + 1], xp)
            pairs.append(jnp.where(_lane(me.shape) < HEAD_DIM, me, mo))
        rows.append(jnp.concatenate(pairs, axis=1))
    return rows[0] if len(rows) == 1 else jnp.concatenate(rows, axis=0)


def _halo_specs(t, s, width):
    per = t // HALO
    nblk = s // HALO
    prev = pl.BlockSpec((HALO, width), lambda i: (jnp.maximum(i * per - 1, 0), 0))
    nxt = pl.BlockSpec((HALO, width), lambda i: (jnp.minimum((i + 1) * per, nblk - 1), 0))
    return prev, nxt


def _const_spec(shape):
    nd = len(shape)
    return pl.BlockSpec(shape, lambda i: (0,) * nd)


def _arb(n=1):
    return pltpu.CompilerParams(dimension_semantics=("arbitrary",) * n)


def _fwd_in(x, g_pre, w_in_bf, bd, qg, kg, cos, sin, t, gather=()):
    s = x.shape[0]
    ex_args, ex_in, ex_out, ex_shapes, ex_sems, ex_hook = _riding_exchange(False, gather)
    n_ex = len(gather)

    def body(*refs):
        x_ref, g_ref, w_ref, bd_ref, qg_ref, kg_ref, cos_ref, sin_ref = refs[:8]
        proj_ref, hc_ref, q_ref, k_ref, v_ref = refs[8 + n_ex:13 + n_ex]
        if n_ex:
            ex_hook(refs[8:8 + n_ex], refs[13 + n_ex:13 + 2 * n_ex], refs[13 + 2 * n_ex:])
        xv = x_ref[...]
        h = (xv * _rms(xv) * g_ref[...]).astype(BF16)
        proj = _dot(h, w_ref[...])
        proj_ref[...] = proj
        cos_pair = cos_ref[...]
        sin_pair = sin_ref[...]
        cosv = jnp.concatenate([cos_pair] * (N_HEADS // 2), axis=1)
        sinv = jnp.concatenate([sin_pair] * (N_HEADS // 2), axis=1)
        q = proj[:, C_Q:C_K]
        qn = q * lax.rsqrt(_group_mean(q * q, bd_ref[...]) + EPS) * qg_ref[...]
        qr = (qn * cosv + _swap16(qn) * sinv) * ATT_SCALE
        for hh in range(N_HEADS):
            pair = qr[:, (hh // 2) * LANES:(hh // 2 + 1) * LANES]
            q_ref[hh] = _head_slab(pair, hh % 2 == 1).astype(BF16)
        k = proj[:, C_K:C_V]
        kn = k * lax.rsqrt(_group_mean(k * k, bd_ref[0:LANES, 0:LANES]) + EPS) * kg_ref[...]
        kr = kn * cos_pair + _swap16(kn) * sin_pair
        vv = proj[:, C_V:C_GA]
        for hh in range(N_KV):
            k_ref[hh] = _with_ones(_head_slab(kr, hh == 1)).astype(BF16)
            v_ref[hh] = _with_ones(_head_slab(vv, hh == 1)).astype(BF16)
        hc_ref[...] = proj[:, C_A1:C_A2] * _sigmoid(proj[:, C_A2:C_GC])

    row = lambda w: pl.BlockSpec((t, w), lambda i: (i, 0))
    heads = lambda n: pl.BlockSpec((n, t, LANES), lambda i: (0, i, 0))
    return pl.pallas_call(
        body, name="fwd_in_gather" if n_ex else "fwd_in", grid=(s // t,),
        in_specs=[row(D_MODEL), _const_spec((1, D_MODEL)), _const_spec((D_MODEL, D_IN)), _const_spec((ATT_W, ATT_W)),
                  _const_spec((1, ATT_W)), _const_spec((1, LANES)), row(LANES), row(LANES)] + ex_in,
        out_specs=[row(D_IN), row(CONV_W), heads(N_HEADS), heads(N_KV), heads(N_KV)] + ex_out,
        out_shape=[jax.ShapeDtypeStruct((s, D_IN), F32), jax.ShapeDtypeStruct((s, CONV_W), F32),
                   jax.ShapeDtypeStruct((N_HEADS, s, LANES), BF16), jax.ShapeDtypeStruct((N_KV, s, LANES), BF16),
                   jax.ShapeDtypeStruct((N_KV, s, LANES), BF16)] + ex_shapes,
        scratch_shapes=ex_sems if n_ex else [],
        compiler_params=_arb(),
    )(x, g_pre, w_in_bf, bd, qg, kg, cos, sin, *ex_args)


def _chunk_rows(c, tk):
    return pl.ds(c * tk, tk) if isinstance(c, int) else pl.ds(pl.multiple_of(c * tk, tk), tk)


def _three_stage_pipeline(nk, per_trip, stage1, stage2, stage3, peel):
    assert nk % 2 == 0 and per_trip % 2 == 0

    def step(t, parity, first=False, last=False):
        if not last:
            stage1(t + 1, 1 - parity)
        stage2(parity)
        if not first:
            stage3(t - 1, 1 - parity)

    stage1(0, 0)
    if not peel:
        while nk % per_trip:
            per_trip //= 2

        def whole_trip(i, carry):
            for u in range(per_trip):
                c = per_trip * i + u
                stage1(jnp.minimum(c + 1, nk - 1), 1 - u % 2)
                stage2(u % 2)
                stage3(jnp.maximum(c - 1, 0), 1 - u % 2)
            return carry

        lax.fori_loop(0, nk // per_trip, whole_trip, 0)
        stage3(nk - 1, 1)
        return

    step(0, 0, first=True)
    n_trips, left = divmod(nk - 2, per_trip)

    def trip(i, carry):
        for u in range(per_trip):
            step(1 + per_trip * i + u, (1 + u) % 2)
        return carry

    if n_trips:
        lax.fori_loop(0, n_trips, trip, 0)
    for t in range(1 + n_trips * per_trip, 1 + n_trips * per_trip + left):
        step(t, t % 2)
    step(nk - 1, 1, last=True)
    stage3(nk - 1, 1)


def _flash_fwd(q, k, v, tq, tk):
    s = q.shape[1]
    rows = GROUP * tq
    nk = s // tk

    def body(q_ref, k_ref, vt_ref, o_ref, qa_ref, m_scr, acc_scr, s0, s1, p0, p1, a0, a1):
        s_bufs, p_bufs, a_bufs = (s0, s1), (p0, p1), (a0, a1)
        qv = q_ref[...].reshape(rows, LANES)
        q_t = qv.astype(F32).T
        q_t_bf = q_t.astype(BF16)
        m_scr[...] = jnp.full((1, rows), -jnp.inf, F32)
        acc_scr[...] = jnp.zeros((LANES, rows), F32)

        def scores(c, slot):
            s_bufs[slot][...] = _dot(k_ref[_chunk_rows(c, tk), :], q_t_bf)

        def softmax(slot):
            for h in range(GROUP):
                r = slice(h * tq, (h + 1) * tq)
                sc = s_bufs[slot][:, r]
                m_prev = m_scr[:, r]
                m_new = jnp.maximum(m_prev, jnp.max(sc, axis=0, keepdims=True))
                p_bufs[slot][:, r] = jnp.exp((sc - m_new).astype(BF16))
                a_bufs[slot][:, r] = jnp.exp(m_prev - m_new)
                m_scr[:, r] = m_new

        def weighted_values(c, slot):
            acc_scr[...] = a_bufs[slot][...] * acc_scr[...] + _dot(vt_ref[c], p_bufs[slot][...])

        _three_stage_pipeline(nk, 6, scores, softmax, weighted_values, peel=True)

        acc = acc_scr[...]
        row = lax.broadcasted_iota(jnp.int32, acc.shape, 0)
        l = jnp.sum(jnp.where(row == HEAD_DIM, acc, 0.0), axis=0, keepdims=True)
        o_ref[...] = jnp.where(row < HEAD_DIM, acc / l, 0.0).T.reshape(GROUP, tq, LANES)
        hi, mid, lo = _split3(-(m_scr[...] + jnp.log(l)))
        qa_t = jnp.where(row == HEAD_DIM, hi, jnp.where(row == HEAD_DIM + 1, mid,
                                                        jnp.where(row == HEAD_DIM + 2, lo, q_t)))
        qa_ref[...] = qa_t.T.astype(BF16).reshape(GROUP, tq, LANES)

    qspec = pl.BlockSpec((GROUP, tq, LANES), lambda j, i: (j, i, 0))
    kspec = pl.BlockSpec((None, s, LANES), lambda j, i: (j, 0, 0))
    vtspec = pl.BlockSpec((None, nk, LANES, tk), lambda j, i: (j, 0, 0, 0))
    v_t = jnp.swapaxes(v.reshape(N_KV, nk, tk, LANES), 2, 3)
    return pl.pallas_call(
        body, name="flash_fwd", grid=(N_KV, s // tq),
        in_specs=[qspec, kspec, vtspec],
        out_specs=[qspec, qspec],
        out_shape=[jax.ShapeDtypeStruct((N_HEADS, s, LANES), F32), jax.ShapeDtypeStruct((N_HEADS, s, LANES), BF16)],
        scratch_shapes=[pltpu.VMEM((1, rows), F32), pltpu.VMEM((LANES, rows), F32),
                        pltpu.VMEM((tk, rows), F32), pltpu.VMEM((tk, rows), F32),
                        pltpu.VMEM((tk, rows), BF16), pltpu.VMEM((tk, rows), BF16),
                        pltpu.VMEM((1, rows), F32), pltpu.VMEM((1, rows), F32)],
        compiler_params=_arb(2),
    )(q, k, v_t)


def _groups_fwd(proj_ref, o_ref, hext_ref, cw_ref, cp_ref, sp_ref, sgw_ref, sgb_ref, t, kept=None):
    proj = proj_ref[...]
    r = {}
    r["att"] = _heads_to_cat(o_ref, N_HEADS)
    r["gate_a"], r["dgate_a"] = _silu_fwd_bwd(proj[:, C_GA:C_A1])
    r["att_g"] = r["att"] * r["gate_a"]
    if kept is None:
        c0 = jnp.zeros((t, CONV_W), F32) + cp_ref[0:1, :]
        for kk in range(CONV_K):
            c0 = c0 + cw_ref[kk:kk + 1, :] * _tap(hext_ref, kk + 1, t)
        cdf_u = cdf_v = None
    else:
        c0, cdf_u, cdf_v = kept[:, 0:CONV_W], kept[:, CONV_W:CONV_W + SG_W], kept[:, CONV_W + SG_W:KEPT_W]
    r["xh_c"], r["rs_c"] = _ln_hat(c0)
    r["c1"] = r["xh_c"] * cp_ref[1:2, :] + cp_ref[2:3, :]
    r["sg_c1"] = _sigmoid(r["c1"])
    r["c2"] = r["c1"] * r["sg_c1"]
    r["gate_c"], r["dgate_c"] = _silu_fwd_bwd(proj[:, C_GC:C_U])
    r["cnv_g"] = r["c2"] * r["gate_c"]
    r["gu"], r["dgu"], cdf_u = _gelu_fwd_bwd(proj[:, C_U:C_VS], cdf_u)
    gv, r["dgv"], cdf_v = _gelu_fwd_bwd(proj[:, C_VS:C_GS], cdf_v)
    r["kept"] = jnp.concatenate([c0, cdf_u, cdf_v], axis=1)
    r["xh_s"], r["rs_s"] = _ln_hat(gv)
    v1 = r["xh_s"] * sp_ref[0:1, :] + sp_ref[1:2, :]
    r["v1_bf"] = v1.astype(BF16)
    r["mixed"] = _sgu_mix(r["v1_bf"], sgw_ref, t // SG_CHUNK) + jnp.concatenate([sgb_ref[...]] * (t // SG_CHUNK), axis=0)
    r["um"] = r["gu"] * r["mixed"]
    r["gate_s"], r["dgate_s"] = _silu_fwd_bwd(proj[:, C_GS:D_IN])
    r["sgu_g"] = r["um"] * r["gate_s"]
    r["mc_bf"] = jnp.concatenate([r["att_g"], r["cnv_g"], r["sgu_g"]], axis=1).astype(BF16)
    return r


def _fwd_out(x, proj, o, hc, cw, cp, sp, sgw_bf, sgb, w_out_bf, g_post, t, target=None):
    s = x.shape[0]
    last_layer = target is not None

    def body(*refs):
        (x_ref, proj_ref, o_ref, hc_ref, hp_ref, hn_ref, cw_ref, cp_ref, sp_ref, sgw_ref, sgb_ref,
         w_ref, g_ref) = refs[:13]
        rest = refs[13:]
        if last_layer:
            t_ref, mix_ref, out_ref, kept_ref, sq_ref, hext_ref = rest
        else:
            mix_ref, out_ref, kept_ref, hext_ref = rest
        i = pl.program_id(0)
        _conv_window(hext_ref, hp_ref, hc_ref[...], hn_ref, i == 0, i == pl.num_programs(0) - 1, t)
        r = _groups_fwd(proj_ref, o_ref, hext_ref, cw_ref, cp_ref, sp_ref, sgw_ref, sgb_ref, t)
        kept_ref[...] = r["kept"]
        mix = _dot(r["mc_bf"], w_ref[...])
        mix_ref[...] = mix
        y = x_ref[...] + mix * _rms(mix) * g_ref[...]
        if last_layer:
            @pl.when(i == 0)
            def _():
                sq_ref[...] = jnp.zeros_like(sq_ref)

            err = y - t_ref[...]
            out_ref[...] = err * (1.0 / D_MODEL)
            sq_ref[...] += jnp.sum(err * err, axis=0, keepdims=True)
        else:
            out_ref[...] = y

    row = lambda w: pl.BlockSpec((t, w), lambda i: (i, 0))
    hprev, hnext = _halo_specs(t, s, CONV_W)
    big = jax.ShapeDtypeStruct((s, D_MODEL), F32)
    return pl.pallas_call(
        body, name="fwd_out_loss" if last_layer else "fwd_out", grid=(s // t,),
        in_specs=[row(D_MODEL), row(D_IN), pl.BlockSpec((N_HEADS, t, LANES), lambda i: (0, i, 0)), row(CONV_W),
                  hprev, hnext, _const_spec((32, CONV_W)), _const_spec((8, CONV_W)), _const_spec((8, SG_W)),
                  _const_spec((4, SG_CHUNK, SG_CHUNK)), _const_spec((SG_CHUNK, SG_W)),
                  _const_spec((D_MODEL, D_MODEL)), _const_spec((1, D_MODEL))] + ([row(D_MODEL)] if last_layer else []),
        out_specs=[row(D_MODEL), row(D_MODEL), row(KEPT_W)] + ([_const_spec((1, D_MODEL))] if last_layer else []),
        out_shape=[big, big, jax.ShapeDtypeStruct((s, KEPT_W), F32)]
        + ([jax.ShapeDtypeStruct((1, D_MODEL), F32)] if last_layer else []),
        scratch_shapes=[pltpu.VMEM((SUBLANES, t + 2 * HALO, CONV_W), F32)],
        compiler_params=_arb(),
    )(*((x, proj, o, hc, hc, hc, cw, cp, sp, sgw_bf, sgb, w_out_bf, g_post) + ((target,) if last_layer else ())))


def _bwd_out(dy, mix, proj, o, hc, kept, cw, cp, sp, sgw_bf, sgwt_bf, sgb, w_out_t_bf, g_post, t, scatter=()):
    s = dy.shape[0]
    n_chunks = t // SG_CHUNK
    ex_args, ex_in, ex_out, ex_shapes, ex_sems, ex_hook = _riding_exchange(True, scatter)
    n_ex = len(scatter)

    def body(*refs):
        (dy_ref, mix_ref, proj_ref, o_ref, hc_ref, hp_ref, hn_ref, kept_ref, cw_ref, cp_ref, sp_ref, sgw_ref,
         sgwt_ref, sgb_ref, wt_ref, g_ref) = refs[:16]
        (do_ref, dgs_ref, dc0_ref, gwo_ref, gpost_ref, gcw_ref, gvec_ref, gsgw_ref,
         gsgb_ref) = refs[16 + n_ex:25 + n_ex]
        hext_ref = refs[25 + 2 * n_ex]
        if n_ex:
            ex_hook(refs[16:16 + n_ex], refs[25 + n_ex:25 + 2 * n_ex], refs[26 + 2 * n_ex:])
        i = pl.program_id(0)

        @pl.when(i == 0)
        def _():
            for ref in (gwo_ref, gpost_ref, gcw_ref, gvec_ref, gsgw_ref, gsgb_ref):
                ref[...] = jnp.zeros_like(ref)

        _conv_window(hext_ref, hp_ref, hc_ref[...], hn_ref, i == 0, i == pl.num_programs(0) - 1, t)
        r = _groups_fwd(proj_ref, o_ref, hext_ref, cw_ref, cp_ref, sp_ref, sgw_ref, sgb_ref, t, kept=kept_ref[...])

        dyv = dy_ref[...]
        mix_v = mix_ref[...]
        rr = _rms(mix_v)
        gd = dyv * g_ref[...]
        dmix = rr * gd - mix_v * (rr * rr * rr * jnp.mean(gd * mix_v, axis=-1, keepdims=True))
        gpost_ref[...] += jnp.sum(dyv * mix_v * rr, axis=0, keepdims=True)
        dmix_bf = dmix.astype(BF16)
        gwo_ref[...] += _dot_tn(r["mc_bf"], dmix_bf)
        dmc = _dot(dmix_bf, wt_ref[...])

        d_att = dmc[:, 0:ATT_W]
        dg_att = d_att * r["att"] * r["dgate_a"]
        d_o = d_att * r["gate_a"]
        prod = d_o * r["att"]
        for p in range(N_HEADS // 2):
            sl = slice(p * LANES, (p + 1) * LANES)
            pr = prod[:, sl]
            tot = jnp.sum(pr, axis=1, keepdims=True)
            ev = jnp.sum(jnp.where(_lane(pr.shape) < HEAD_DIM, pr, 0.0), axis=1, keepdims=True)
            for odd, delta in ((False, ev), (True, tot - ev)):
                hi, mid, lo = _split3(-delta)
                do_ref[2 * p + int(odd)] = _with_spare(_head_slab(d_o[:, sl], odd), hi, mid, lo).astype(BF16)

        dcv = dmc[:, ATT_W:ATT_W + CONV_W]
        dg_conv = dcv * r["c2"] * r["dgate_c"]
        dc1 = dcv * r["gate_c"] * (r["sg_c1"] * (1.0 + r["c1"] * (1.0 - r["sg_c1"])))
        dc0 = _ln_bwd(dc1 * cp_ref[1:2, :], r["xh_c"], r["rs_c"])
        dc0_ref[...] = dc0
        for kk in range(CONV_K):
            gcw_ref[kk:kk + 1, :] += jnp.sum(dc0 * _tap(hext_ref, kk + 1, t), axis=0, keepdims=True)

        dsg = dmc[:, ATT_W + CONV_W:D_MODEL]
        dg_sg = dsg * r["um"] * r["dgate_s"]
        du = dsg * r["mixed"] * r["gate_s"] * r["dgu"]
        dmx = dsg * r["gu"] * r["gate_s"]
        dmx_bf = dmx.astype(BF16)
        sgb_sum = dmx[0:SG_CHUNK, :]
        for n in range(1, n_chunks):
            sgb_sum = sgb_sum + dmx[n * SG_CHUNK:(n + 1) * SG_CHUNK, :]
        gsgb_ref[...] += sgb_sum
        dv1_rows = []
        for n in range(n_chunks):
            pairs = []
            for p in range(SG_W // LANES):
                rs_ = slice(n * SG_CHUNK, (n + 1) * SG_CHUNK)
                ls_ = slice(p * LANES, (p + 1) * LANES)
                dm = dmx_bf[rs_, ls_]
                xp = r["v1_bf"][rs_, ls_]
                low = _lane(dm.shape) < HEAD_DIM
                zero = jnp.zeros_like(dm)
                gsgw_ref[2 * p] += _dot_nt(jnp.where(low, dm, zero), xp)
                gsgw_ref[2 * p + 1] += _dot_nt(jnp.where(low, zero, dm), xp)
                pairs.append(jnp.where(low, _dot(sgwt_ref[2 * p], dm), _dot(sgwt_ref[2 * p + 1], dm)))
            dv1_rows.append(jnp.concatenate(pairs, axis=1))
        dv1 = dv1_rows[0] if n_chunks == 1 else jnp.concatenate(dv1_rows, axis=0)
        dvs = _ln_bwd(dv1 * sp_ref[0:1, :], r["xh_s"], r["rs_s"]) * r["dgv"]

        zrow = jnp.zeros((1, CONV_W), F32)
        gvec_ref[...] += jnp.concatenate([
            jnp.sum(dc0, axis=0, keepdims=True),
            jnp.sum(dc1 * r["xh_c"], axis=0, keepdims=True),
            jnp.sum(dc1, axis=0, keepdims=True),
            jnp.sum(dv1 * r["xh_s"], axis=0, keepdims=True),
            jnp.sum(dv1, axis=0, keepdims=True),
            zrow, zrow, zrow], axis=0)
        dgs_ref[...] = jnp.concatenate([dg_att, dg_conv, du, dvs, dg_sg], axis=1)

    row = lambda w: pl.BlockSpec((t, w), lambda i: (i, 0))
    heads = pl.BlockSpec((N_HEADS, t, LANES), lambda i: (0, i, 0))
    hprev, hnext = _halo_specs(t, s, CONV_W)
    return pl.pallas_call(
        body, name="bwd_out_scatter" if n_ex else "bwd_out", grid=(s // t,),
        in_specs=[row(D_MODEL), row(D_MODEL), row(D_IN), heads, row(CONV_W), hprev, hnext, row(KEPT_W),
                  _const_spec((32, CONV_W)), _const_spec((8, CONV_W)), _const_spec((8, SG_W)),
                  _const_spec((4, SG_CHUNK, SG_CHUNK)), _const_spec((4, SG_CHUNK, SG_CHUNK)),
                  _const_spec((SG_CHUNK, SG_W)), _const_spec((D_MODEL, D_MODEL)), _const_spec((1, D_MODEL))] + ex_in,
        out_specs=[heads, row(GATES_W), row(CONV_W), _const_spec((D_MODEL, D_MODEL)), _const_spec((1, D_MODEL)),
                   _const_spec((32, CONV_W)), _const_spec((8, CONV_W)), _const_spec((4, SG_CHUNK, SG_CHUNK)),
                   _const_spec((SG_CHUNK, SG_W))] + ex_out,
        out_shape=[jax.ShapeDtypeStruct((N_HEADS, s, LANES), BF16), jax.ShapeDtypeStruct((s, GATES_W), F32),
                   jax.ShapeDtypeStruct((s, CONV_W), F32), jax.ShapeDtypeStruct((D_MODEL, D_MODEL), F32),
                   jax.ShapeDtypeStruct((1, D_MODEL), F32), jax.ShapeDtypeStruct((32, CONV_W), F32),
                   jax.ShapeDtypeStruct((8, CONV_W), F32), jax.ShapeDtypeStruct((4, SG_CHUNK, SG_CHUNK), F32),
                   jax.ShapeDtypeStruct((SG_CHUNK, SG_W), F32)] + ex_shapes,
        scratch_shapes=[pltpu.VMEM((SUBLANES, t + 2 * HALO, CONV_W), F32)] + (ex_sems if n_ex else []),
        compiler_params=_arb(),
    )(dy, mix, proj, o, hc, hc, hc, kept, cw, cp, sp, sgw_bf, sgwt_bf, sgb, w_out_t_bf, g_post, *ex_args)


def _flash_bwd(qa, doa, k, v, tq, tk):
    s = qa.shape[1]
    rows = GROUP * tq
    nk = s // tk
    n_q = s // tq

    def body(qa_ref, do_ref, k_ref, v_ref, dq_ref, dk_hbm, dv_hbm,
             dq_scr, dk_scr, dv_scr, s0, s1, d0, d1, p0, p1, e0, e1, sems):
        j, i = pl.program_id(0), pl.program_id(1)
        s_bufs, d_bufs, p_bufs, e_bufs = (s0, s1), (d0, d1), (p0, p1), (e0, e1)
        qv = qa_ref[...].reshape(rows, LANES)
        dov = do_ref[...].reshape(rows, LANES)
        q_t = qv.astype(F32).T.astype(BF16)
        do_t = dov.astype(F32).T.astype(BF16)
        dq_scr[...] = jnp.zeros((rows, LANES), F32)

        @pl.when(i == 0)
        def _():
            dk_scr[...] = jnp.zeros_like(dk_scr)
            dv_scr[...] = jnp.zeros_like(dv_scr)

        def at(c):
            return _chunk_rows(c, tk)

        def scores(c, slot):
            s_bufs[slot][...] = _dot_nt(qv, k_ref[at(c), :])
            d_bufs[slot][...] = _dot_nt(dov, v_ref[at(c), :])

        def probs(slot):
            for h in range(GROUP):
                r = slice(h * tq, (h + 1) * tq)
                p = jnp.exp(s_bufs[slot][r, :].astype(BF16))
                p_bufs[slot][r, :] = p
                e_bufs[slot][r, :] = (p.astype(F32) * d_bufs[slot][r, :]).astype(BF16)

        def grads(c, slot):
            ds = e_bufs[slot][...]
            dq_scr[...] += _dot(ds, k_ref[at(c), :])
            dv_scr[c] += _dot(do_t, p_bufs[slot][...])
            dk_scr[c] += _dot(q_t, ds)

        p1[...] = jnp.zeros((rows, tk), BF16)
        e1[...] = jnp.zeros((rows, tk), BF16)
        _three_stage_pipeline(nk, 4, scores, probs, grads, peel=False)
        dq_ref[...] = dq_scr[...].reshape(GROUP, tq, LANES)

        @pl.when(i == n_q - 1)
        def _():
            out = [pltpu.make_async_copy(dk_scr, dk_hbm.at[j], sems.at[0]),
                   pltpu.make_async_copy(dv_scr, dv_hbm.at[j], sems.at[1])]
            for cp in out:
                cp.start()
            for cp in out:
                cp.wait()

    qspec = pl.BlockSpec((GROUP, tq, LANES), lambda j, i: (j, i, 0))
    kvspec = pl.BlockSpec((None, s, LANES), lambda j, i: (j, 0, 0), pipeline_mode=pl.Buffered(1))
    hbm = pl.BlockSpec(memory_space=pl.ANY)
    stage_f32 = pltpu.VMEM((rows, tk), F32)
    stage_bf = pltpu.VMEM((rows, tk), BF16)
    kv_t = jax.ShapeDtypeStruct((N_KV, nk, LANES, tk), F32)
    return pl.pallas_call(
        body, name="flash_bwd", grid=(N_KV, n_q),
        in_specs=[qspec, qspec, kvspec, kvspec],
        out_specs=[qspec, hbm, hbm],
        out_shape=[jax.ShapeDtypeStruct((N_HEADS, s, LANES), F32), kv_t, kv_t],
        scratch_shapes=[pltpu.VMEM((rows, LANES), F32), pltpu.VMEM((nk, LANES, tk), F32), pltpu.VMEM((nk, LANES, tk), F32),
                        stage_f32, stage_f32, stage_f32, stage_f32, stage_bf, stage_bf, stage_bf, stage_bf,
                        pltpu.SemaphoreType.DMA((2,))],
        compiler_params=_arb(2),
    )(qa, doa, k, v)


def _bwd_in(dy, x, proj, dq, dk, dv, dgs, dc0, cw, g_pre, w_in_t_bf, bd, qg, kg, cos, sin, t):
    s = x.shape[0]

    def body(dy_ref, x_ref, proj_ref, dq_ref, dk_ref, dv_ref, dgs_ref, dc_ref, dcp_ref, dcn_ref, cw_ref, g_ref,
             wt_ref, bd_ref, qg_ref, kg_ref, cos_ref, sin_ref,
             dx_ref, dproj_ref, h_ref, gpre_ref, gq_ref, gk_ref, dext_ref):
        i = pl.program_id(0)

        @pl.when(i == 0)
        def _():
            for ref in (gpre_ref, gq_ref, gk_ref):
                ref[...] = jnp.zeros_like(ref)

        proj = proj_ref[...]
        cos_pair = cos_ref[...]
        sin_pair = sin_ref[...]
        cosv = jnp.concatenate([cos_pair] * (N_HEADS // 2), axis=1)
        sinv = jnp.concatenate([sin_pair] * (N_HEADS // 2), axis=1)

        def head_norm_bwd(dr, z, bdm, g, cs, sn, gacc_ref):
            dn = dr * cs + _swap16(dr * sn)
            rr = lax.rsqrt(_group_mean(z * z, bdm) + EPS)
            gdn = dn * g
            gacc_ref[...] += jnp.sum(dn * z * rr, axis=0, keepdims=True)
            return rr * gdn - z * (rr * rr * rr * _group_mean(gdn * z, bdm))

        dq_cat = _heads_to_cat(dq_ref, N_HEADS) * ATT_SCALE
        dzq = head_norm_bwd(dq_cat, proj[:, C_Q:C_K], bd_ref[...], qg_ref[...], cosv, sinv, gq_ref)

        def kv_pair(ref):
            return jnp.concatenate([ref[0, 0:HEAD_DIM, :], ref[1, 0:HEAD_DIM, :]], axis=0).T

        dk_cat = kv_pair(dk_ref)
        dzk = head_norm_bwd(dk_cat, proj[:, C_K:C_V], bd_ref[0:LANES, 0:LANES], kg_ref[...],
                            cos_pair, sin_pair, gk_ref)
        dv_cat = kv_pair(dv_ref)

        _conv_window(dext_ref, dcp_ref, dc_ref[...], dcn_ref, i == 0, i == pl.num_programs(0) - 1, t)
        dhc = jnp.zeros((t, CONV_W), F32)
        for kk in range(CONV_K):
            dhc = dhc + cw_ref[kk:kk + 1, :] * _tap(dext_ref, CONV_K - kk, t)
        sg = _sigmoid(proj[:, C_A2:C_GC])
        da1 = dhc * sg
        da2 = dhc * proj[:, C_A1:C_A2] * sg * (1.0 - sg)

        dgs = dgs_ref[...]
        dproj_bf = jnp.concatenate([dzq, dzk, dv_cat, dgs[:, 0:ATT_W], da1, da2, dgs[:, ATT_W:GATES_W]],
                                   axis=1).astype(BF16)
        dproj_ref[...] = dproj_bf
        dh = _dot(dproj_bf, wt_ref[...])

        xv = x_ref[...]
        rr = _rms(xv)
        gv = g_ref[...]
        h_ref[...] = (xv * rr * gv).astype(BF16)
        gdh = dh * gv
        gpre_ref[...] += jnp.sum(dh * xv * rr, axis=0, keepdims=True)
        dx_ref[...] = dy_ref[...] + rr * gdh - xv * (rr * rr * rr * jnp.mean(gdh * xv, axis=-1, keepdims=True))

    row = lambda w: pl.BlockSpec((t, w), lambda i: (i, 0))
    heads = lambda n: pl.BlockSpec((n, t, LANES), lambda i: (0, i, 0))
    hprev, hnext = _halo_specs(t, s, CONV_W)
    tk = dk.shape[3]
    assert tk % t == 0
    kv_t = pl.BlockSpec((N_KV, None, LANES, t), lambda i: (0, i // (tk // t), 0, i % (tk // t)))
    return pl.pallas_call(
        body, name="bwd_in", grid=(s // t,),
        in_specs=[row(D_MODEL), row(D_MODEL), row(D_IN), heads(N_HEADS), kv_t, kv_t, row(GATES_W),
                  row(CONV_W), hprev, hnext, _const_spec((32, CONV_W)), _const_spec((1, D_MODEL)),
                  _const_spec((D_IN, D_MODEL)), _const_spec((ATT_W, ATT_W)), _const_spec((1, ATT_W)),
                  _const_spec((1, LANES)), row(LANES), row(LANES)],
        out_specs=[row(D_MODEL), row(D_IN), row(D_MODEL), _const_spec((1, D_MODEL)), _const_spec((1, ATT_W)),
                   _const_spec((1, LANES))],
        out_shape=[jax.ShapeDtypeStruct((s, D_MODEL), F32), jax.ShapeDtypeStruct((s, D_IN), BF16),
                   jax.ShapeDtypeStruct((s, D_MODEL), BF16), jax.ShapeDtypeStruct((1, D_MODEL), F32),
                   jax.ShapeDtypeStruct((1, ATT_W), F32), jax.ShapeDtypeStruct((1, LANES), F32)],
        scratch_shapes=[pltpu.VMEM((SUBLANES, t + 2 * HALO, CONV_W), F32)],
        compiler_params=_arb(),
    )(dy, x, proj, dq, dk, dv, dgs, dc0, dc0, dc0, cw, g_pre, w_in_t_bf, bd, qg, kg, cos, sin)


def _grad_w_in(h_bf, dproj_bf, t):
    s = h_bf.shape[0]
    half = D_IN // 2

    def body(h_ref, d_ref, g_ref):
        @pl.when(pl.program_id(1) == 0)
        def _():
            g_ref[...] = jnp.zeros_like(g_ref)

        g_ref[...] += _dot_tn(h_ref[...], d_ref[...])

    return pl.pallas_call(
        body, name="grad_w_in", grid=(2, s // t),
        in_specs=[pl.BlockSpec((t, D_MODEL), lambda j, i: (i, 0)), pl.BlockSpec((t, half), lambda j, i: (i, j))],
        out_specs=pl.BlockSpec((D_MODEL, half), lambda j, i: (0, j)),
        out_shape=jax.ShapeDtypeStruct((D_MODEL, D_IN), F32),
        compiler_params=_arb(2),
    )(h_bf, dproj_bf)


def _place():
    x, y, c = lax.axis_index("x"), lax.axis_index("y"), lax.axis_index("c")
    chips = [(1 - x, y), (x, 1 - y), (1 - x, 1 - y)]
    return x, y, c, chips


def _any_specs(n):
    return [pl.BlockSpec(memory_space=pl.ANY)] * n


class _ChipExchange:
    def __init__(self, scatter, ins, outs, send_sems, recv_sems, local_sems):
        n = len(ins)
        x, y, c, chips = _place()
        mine = 2 * x + y
        src = (lambda a, piece: ins[a].at[piece]) if scatter else (lambda a, piece: ins[a])

        def copy(j, a, piece, slot, to):
            return pltpu.make_async_remote_copy(src_ref=src(a, piece), dst_ref=outs[a].at[slot],
                                                send_sem=send_sems.at[n * j + a], recv_sem=recv_sems.at[n * j + a],
                                                device_id=to, device_id_type=MESH)

        self.local = [pltpu.make_async_copy(src(a, mine), outs[a].at[mine], local_sems.at[a]) for a in range(n)]
        self.sends = [copy(j, a, 2 * px + py, mine, (px, py, c)) for j, (px, py) in enumerate(chips) for a in range(n)]
        self.arrivals = lambda: [copy(j, a, mine, 2 * px + py, (px, py, c))
                                 for j, (px, py) in enumerate(chips) for a in range(n)]

    def start(self):
        for cp in self.local + self.sends:
            cp.start()

    def finish(self):
        for cp in self.arrivals():
            cp.wait_recv()
        for cp in self.sends:
            cp.wait_send()
        for cp in self.local:
            cp.wait()

    @staticmethod
    def out_shapes(scatter, arrs):
        return [jax.ShapeDtypeStruct(a.shape if scatter else (N_CHIPS,) + a.shape, a.dtype) for a in arrs]

    @staticmethod
    def semaphores(n):
        return [pltpu.SemaphoreType.DMA((3 * n,)), pltpu.SemaphoreType.DMA((3 * n,)), pltpu.SemaphoreType.DMA((n,))]


def _exchange(scatter, arrs, name):
    n = len(arrs)

    def body(*refs):
        ex = _ChipExchange(scatter, refs[:n], refs[n:2 * n], *refs[2 * n:])
        ex.start()
        ex.finish()

    return pl.pallas_call(
        body, name=name, in_specs=_any_specs(n), out_specs=_any_specs(n),
        out_shape=_ChipExchange.out_shapes(scatter, arrs), scratch_shapes=_ChipExchange.semaphores(n),
    )(*arrs)


def _riding_exchange(scatter, arrs):
    n = len(arrs)

    def hook(ins, outs, sems):
        i = pl.program_id(0)

        @pl.when(i == 0)
        def _():
            _ChipExchange(scatter, ins, outs, *sems).start()

        @pl.when(i == pl.num_programs(0) - 1)
        def _():
            _ChipExchange(scatter, ins, outs, *sems).finish()

    return (tuple(arrs), _any_specs(n), _any_specs(n), _ChipExchange.out_shapes(scatter, arrs),
            _ChipExchange.semaphores(n), hook)


def _sum_chips(parts, rb, name):
    depth = len(parts)
    _, r, cdim = parts[0].shape

    def body(*refs):
        o_ref = refs[depth]
        for l in range(depth):
            def add(p_ref=refs[l]):
                part = lambda j: p_ref[j].astype(F32)
                o_ref[...] = ((part(0) + part(1)) + part(2)) + part(3)

            pl.when(pl.program_id(0) == l)(add)

    return pl.pallas_call(
        body, name=name, grid=(depth, r // rb),
        in_specs=[pl.BlockSpec((N_CHIPS, rb, cdim), lambda l, i: (0, i, 0))] * depth,
        out_specs=pl.BlockSpec((None, rb, cdim), lambda l, i: (l, i, 0)),
        out_shape=jax.ShapeDtypeStruct((depth, r, cdim), F32),
        compiler_params=_arb(2),
    )(*parts)


def _swap_with_sibling(a, b):
    arrs = (a, b)
    n = len(arrs)

    def body(*refs):
        ins, outs = refs[:n], refs[n:2 * n]
        send_sems, recv_sems = refs[2 * n:]
        x, y, c, _ = _place()
        cps = [pltpu.make_async_remote_copy(src_ref=ins[k], dst_ref=outs[k], send_sem=send_sems.at[k],
                                            recv_sem=recv_sems.at[k], device_id=(x, y, 1 - c), device_id_type=MESH)
               for k in range(n)]
        for cp in cps:
            cp.start()
        for cp in cps:
            cp.wait()

    return pl.pallas_call(
        body, name="swap_with_sibling",
        in_specs=_any_specs(n), out_specs=_any_specs(n),
        out_shape=[jax.ShapeDtypeStruct(v.shape, v.dtype) for v in arrs],
        scratch_shapes=[pltpu.SemaphoreType.DMA((n,)), pltpu.SemaphoreType.DMA((n,))],
    )(*arrs)


def _allreduce_small(slab):
    m, n = slab.shape

    def body(x_ref, out_ref, gath, send_sems, recv_sems, local_sem):
        x, y, c, chips = _place()
        me, sibling = (x, y, c), (x, y, 1 - c)

        def rows(px, py, pc):
            return gath.at[pl.ds(pl.multiple_of((4 * px + 2 * py + pc) * m, 8), m), :]

        def copy(k, block, to, src=None):
            return pltpu.make_async_remote_copy(src_ref=rows(*block) if src is None else src, dst_ref=rows(*block),
                                                send_sem=send_sems.at[k], recv_sem=recv_sems.at[k],
                                                device_id=to, device_id_type=MESH)

        mine = pltpu.make_async_copy(x_ref, rows(*me), local_sem)
        mine.start()
        first = [copy(0, me, sibling, src=x_ref)]
        first += [copy(1 + j, me, (*chip, c), src=x_ref) for j, chip in enumerate(chips)]
        for cp in first:
            cp.start()
        passed = [copy(4 + j, (*chip, c), sibling) for j, chip in enumerate(chips)]
        for j, chip in enumerate(chips):
            copy(1 + j, (*chip, c), me).wait_recv()
            passed[j].start()
        copy(0, sibling, me).wait_recv()
        for j, chip in enumerate(chips):
            copy(4 + j, (*chip, 1 - c), me).wait_recv()
        for cp in first + passed:
            cp.wait_send()
        mine.wait()
        total = gath[0:m, :]
        for d in range(1, N_DEV):
            total = total + gath[d * m:(d + 1) * m, :]
        out_ref[...] = total

    return pl.pallas_call(
        body, name="allreduce_small",
        in_specs=[pl.BlockSpec(memory_space=pltpu.VMEM)],
        out_specs=pl.BlockSpec(memory_space=pltpu.VMEM),
        out_shape=jax.ShapeDtypeStruct((m, n), F32),
        scratch_shapes=[pltpu.VMEM((N_DEV * m, n), F32), pltpu.SemaphoreType.DMA((7,)), pltpu.SemaphoreType.DMA((7,)),
                        pltpu.SemaphoreType.DMA],
    )(slab)


def _adamw(w, ga, gb, m, v, rb, name):
    depth, r, cdim = w.shape

    def body(w_ref, ga_ref, gb_ref, m_ref, v_ref, g_out, d_out, m_out, v_out):
        g = ga_ref[...] + gb_ref[...]
        m2 = ADAM_B1 * m_ref[...] + (1.0 - ADAM_B1) * g
        v2 = ADAM_B2 * v_ref[...] + (1.0 - ADAM_B2) * (g * g)
        m_hat = m2 / (1.0 - ADAM_B1 ** ADAM_STEP)
        v_hat = v2 / (1.0 - ADAM_B2 ** ADAM_STEP)
        g_out[...] = g
        d_out[...] = -ADAM_LR * (m_hat / (jnp.sqrt(v_hat) + ADAM_EPS) + ADAM_WD * w_ref[...])
        m_out[...] = m2
        v_out[...] = v2

    spec = pl.BlockSpec((None, rb, cdim), lambda l, i: (l, i, 0))
    shp = jax.ShapeDtypeStruct((depth, r, cdim), F32)
    return pl.pallas_call(
        body, name=name, grid=(depth, r // rb),
        in_specs=[spec] * 5, out_specs=[spec] * 4, out_shape=[shp] * 4,
        compiler_params=_arb(2),
    )(w, ga, gb, m, v)


def _rope_tables(s):
    t = jnp.arange(s, dtype=jnp.int32)
    row = (t // GRID_W).astype(F32)
    col = (t % GRID_W).astype(F32)
    half = HEAD_DIM // 4
    inv_freq = ROPE_THETA ** (-jnp.arange(half, dtype=F32) / half)
    ar = row[:, None] * inv_freq[None, :]
    ac = col[:, None] * inv_freq[None, :]
    cos = jnp.concatenate([jnp.cos(ar), jnp.cos(ar), jnp.cos(ac), jnp.cos(ac)], axis=1)
    sin = jnp.concatenate([-jnp.sin(ar), jnp.sin(ar), -jnp.sin(ac), jnp.sin(ac)], axis=1)
    return jnp.tile(cos, (1, 2)), jnp.tile(sin, (1, 2))


def _pad_rows(a, rows):
    return jnp.pad(a, ((0, rows - a.shape[0]),) + ((0, 0),) * (a.ndim - 1))


_SMALL = ("pre_norm", "post_norm", "q_norm", "k_norm", "conv_dw_b", "conv_ln_g", "conv_ln_b", "sg_ln_g", "sg_ln_b",
          "sg_w", "sg_b")


def _pack(parts):
    flat = jnp.concatenate([p.reshape(-1, LANES) for p in parts], axis=0)
    return _pad_rows(flat, -(-flat.shape[0] // 8) * 8)


def _unpack(slab, shapes):
    out, r = [], 0
    for shp in shapes:
        n = 1
        for d in shp:
            n *= d
        out.append(slab[r:r + n // LANES].reshape(shp))
        r += n // LANES
    return out


def kernel(x, pre_norm, post_norm, w_in, w_out, q_norm, k_norm, conv_dw, conv_dw_b, conv_ln_g, conv_ln_b, sg_ln_g, sg_ln_b, sg_w, sg_b, loss_target, m_pre_norm, m_post_norm, m_w_in, m_w_out, m_q_norm, m_k_norm, m_conv_dw, m_conv_dw_b, m_conv_ln_g, m_conv_ln_b, m_sg_ln_g, m_sg_ln_b, m_sg_w, m_sg_b, v_pre_norm, v_post_norm, v_w_in, v_w_out, v_q_norm, v_k_norm, v_conv_dw, v_conv_dw_b, v_conv_ln_g, v_conv_ln_b, v_sg_ln_g, v_sg_ln_b, v_sg_w, v_sg_b):
    depth = w_in.shape[0]
    s = x.shape[1]
    assert x.shape[0] == 1 and s % SG_CHUNK == 0 and x.shape[2] == D_MODEL
    tq = min(512, s)
    tk = min(512, s // 2)
    t = min(256, tk)
    shard_cols = w_in.shape[2]
    chip = 2 * lax.axis_index("x") + lax.axis_index("y")

    w_in_sh, w_out_sh = w_in.astype(BF16), w_out.astype(BF16)
    whole_w_in = lambda g: jnp.concatenate([g[j] for j in range(N_CHIPS)], axis=2)
    w_in_bf = [whole_w_in(_exchange(False, (w_in_sh[0:1],), "gather_w_in_first")[0])[0]]
    riders = ((w_in_sh[1:],) if depth > 1 else ()) + (w_out_sh, conv_dw)
    sgw_bf = sg_w.astype(BF16)
    sgwt_bf = jnp.swapaxes(sg_w, 2, 3).astype(BF16)

    cos, sin = _rope_tables(s)
    bd = jnp.kron(jnp.eye(N_HEADS, dtype=F32), jnp.full((HEAD_DIM, HEAD_DIM), 1.0 / HEAD_DIM, F32)).astype(BF16)

    def head_gains(l):
        return jnp.tile(q_norm[l], N_HEADS)[None, :], jnp.tile(k_norm[l], N_KV)[None, :]

    def layer_consts(l):
        cw = _pad_rows(cdw_full[l], 32)
        cp = _pad_rows(jnp.stack([conv_dw_b[l], conv_ln_g[l], conv_ln_b[l]]), 8)
        sp = _pad_rows(jnp.stack([sg_ln_g[l], sg_ln_b[l]]), 8)
        sgb = jnp.repeat(sg_b[l].T, HEAD_DIM, axis=1)
        return cw, cp, sp, sgb

    xs = [x[0]]
    saved = []
    for l in range(depth):
        qg, kg = head_gains(l)
        outs = _fwd_in(xs[l], pre_norm[l][None, :], w_in_bf[l], bd, qg, kg, cos, sin, t, gather=riders if l == 0 else ())
        proj, hc, q, k, v = outs[:5]
        if l == 0:
            gathered = list(outs[5:])
            if depth > 1:
                w_in_bf += list(whole_w_in(gathered.pop(0)))
            w_out_bf = jnp.concatenate([gathered[0][j] for j in range(N_CHIPS)], axis=1)
            cdw_full = jnp.concatenate([gathered[1][j] for j in range(N_CHIPS)], axis=2)
            w_out_t_bf = jnp.swapaxes(w_out_bf, 1, 2)
        cw, cp, sp, sgb = layer_consts(l)
        o, qa = _flash_fwd(q, k, v, tq, tk)
        outs = _fwd_out(xs[l], proj, o, hc, cw, cp, sp, sgw_bf[l], sgb, w_out_bf[l], post_norm[l][None, :], t,
                        target=loss_target[0] if l == depth - 1 else None)
        saved.append((proj, hc, qa, k, v, o, outs[0], outs[2]))
        if l < depth - 1:
            xs.append(outs[1])
    dy, sq = outs[1], outs[3]
    loss = lax.psum(0.5 * jnp.sum(sq) / D_MODEL, ("x", "y", "c"))

    g_small = {n: [] for n in _SMALL + ("conv_dw",)}
    received = [None] * depth
    pieces = ()
    for l in reversed(range(depth)):
        cw, cp, sp, sgb = layer_consts(l)
        qg, kg = head_gains(l)
        proj, hc, qa, k, v, o, mix, kept = saved[l]
        outs = _bwd_out(dy, mix, proj, o, hc, kept, cw, cp, sp, sgw_bf[l], sgwt_bf[l], sgb, w_out_t_bf[l],
                        post_norm[l][None, :], t, scatter=pieces)
        doa, dgs, dc0, gwo, gpost, gcw, gvec, gsgw, gsgb = outs[:9]
        if pieces:
            received[l + 1] = outs[9:]
        dq, dk, dv = _flash_bwd(qa, doa, k, v, tq, tk)
        dy, dproj_bf, h_bf, gpre, gq, gk = _bwd_in(dy, xs[l], proj, dq, dk, dv, dgs, dc0, cw, pre_norm[l][None, :],
                                                  jnp.swapaxes(w_in_bf[l], 0, 1), bd, qg, kg, cos, sin, t)
        gwi = _grad_w_in(h_bf, dproj_bf, min(512, s))
        pieces = (jnp.stack([gwi[:, j * shard_cols:(j + 1) * shard_cols] for j in range(N_CHIPS)]).astype(BF16),
                  gwo.reshape(N_CHIPS, gwo.shape[0] // N_CHIPS, gwo.shape[1]).astype(BF16))
        g_small["pre_norm"].append(gpre[0])
        g_small["post_norm"].append(gpost[0])
        g_small["q_norm"].append(gq[0].reshape(N_HEADS, HEAD_DIM).sum(0))
        g_small["k_norm"].append(gk[0].reshape(N_KV, HEAD_DIM).sum(0))
        g_small["conv_dw"].append(gcw[:CONV_K])
        g_small["conv_dw_b"].append(gvec[0])
        g_small["conv_ln_g"].append(gvec[1])
        g_small["conv_ln_b"].append(gvec[2])
        g_small["sg_ln_g"].append(gvec[3])
        g_small["sg_ln_b"].append(gvec[4])
        g_small["sg_w"].append(gsgw)
        g_small["sg_b"].append(gsgb.reshape(SG_CHUNK, SG_W // HEAD_DIM, HEAD_DIM).sum(-1).T)
    grad_x = dy[None]
    g_small = {n: jnp.stack(vals[::-1]) for n, vals in g_small.items()}

    received[0] = _exchange(True, pieces, "scatter_grads")
    s_in = _sum_chips([r[0] for r in received], 256, "sum_chips_w_in")
    s_out = _sum_chips([r[1] for r in received], 256, "sum_chips_w_out")
    t_in, t_out = _swap_with_sibling(s_in, s_out)
    grad_w_in, delta_w_in, new_m_w_in, new_v_w_in = _adamw(w_in, s_in, t_in, m_w_in, v_w_in, 256, "adamw_w_in")
    grad_w_out, delta_w_out, new_m_w_out, new_v_w_out = _adamw(w_out, s_out, t_out, m_w_out, v_w_out, 256, "adamw_w_out")

    small_w = dict(pre_norm=pre_norm, post_norm=post_norm, q_norm=q_norm, k_norm=k_norm, conv_dw_b=conv_dw_b,
                   conv_ln_g=conv_ln_g, conv_ln_b=conv_ln_b, sg_ln_g=sg_ln_g, sg_ln_b=sg_ln_b, sg_w=sg_w, sg_b=sg_b)
    small_m = dict(pre_norm=m_pre_norm, post_norm=m_post_norm, q_norm=m_q_norm, k_norm=m_k_norm, conv_dw_b=m_conv_dw_b,
                   conv_ln_g=m_conv_ln_g, conv_ln_b=m_conv_ln_b, sg_ln_g=m_sg_ln_g, sg_ln_b=m_sg_ln_b, sg_w=m_sg_w,
                   sg_b=m_sg_b)
    small_v = dict(pre_norm=v_pre_norm, post_norm=v_post_norm, q_norm=v_q_norm, k_norm=v_k_norm, conv_dw_b=v_conv_dw_b,
                   conv_ln_g=v_conv_ln_g, conv_ln_b=v_conv_ln_b, sg_ln_g=v_sg_ln_g, sg_ln_b=v_sg_ln_b, sg_w=v_sg_w,
                   sg_b=v_sg_b)
    shapes = [small_w[n].shape for n in _SMALL]
    red = _allreduce_small(_pack([g_small[n] for n in _SMALL] + [g_small["conv_dw"]]))
    n_rep = sum(small_w[n].size for n in _SMALL) // LANES
    g_cdw_full = red[n_rep:n_rep + g_small["conv_dw"].size // LANES].reshape(g_small["conv_dw"].shape)
    cdw_cols = conv_dw.shape[2]
    g_cdw = lax.dynamic_slice_in_dim(g_cdw_full, chip * cdw_cols, cdw_cols, axis=2)
    g_slab = _pack([red[:n_rep], g_cdw])
    w_slab = _pack([small_w[n] for n in _SMALL] + [conv_dw])
    m_slab = _pack([small_m[n] for n in _SMALL] + [m_conv_dw])
    v_slab = _pack([small_v[n] for n in _SMALL] + [v_conv_dw])
    rows = w_slab.shape[0]
    outs = _adamw(w_slab[None], g_slab[None], jnp.zeros_like(g_slab)[None], m_slab[None], v_slab[None], rows, "adamw_small")
    unpacked = [dict(zip(_SMALL + ("conv_dw",), _unpack(o_[0], shapes + [conv_dw.shape]))) for o_ in outs]

    big = [dict(w_in=a, w_out=b) for a, b in ((grad_w_in, grad_w_out), (delta_w_in, delta_w_out),
                                             (new_m_w_in, new_m_w_out), (new_v_w_in, new_v_w_out))]
    order = ("pre_norm", "post_norm", "w_in", "w_out", "q_norm", "k_norm", "conv_dw", "conv_dw_b", "conv_ln_g",
             "conv_ln_b", "sg_ln_g", "sg_ln_b", "sg_w", "sg_b")
    result = [loss, grad_x]
    for kind in range(4):
        for name in order:
            result.append(big[kind][name] if name in big[kind] else unpacked[kind][name])
    return tuple(result)
```

```python
import jax
import jax.numpy as jnp
from jax import lax
from jax.experimental import pallas as pl
from jax.experimental.pallas import tpu as pltpu

F32 = jnp.float32
BF16 = jnp.bfloat16
MESH = pl.DeviceIdType.MESH

EPS = 1e-6
D_MODEL = 1024
D_IN = 2816
HEAD_DIM = 64
LANES = 128
SUBLANES = 8
N_HEADS = 8
N_KV = 2
GROUP = N_HEADS // N_KV
GRID_W = 64
ROPE_THETA = 10000.0
CONV_K = 31
CONV_W = 256
SG_W = 256
SG_CHUNK = 128
KEPT_W = CONV_W + 2 * SG_W
HALO = 16
ATT_SCALE = HEAD_DIM ** -0.5

ATT_W = N_HEADS * HEAD_DIM
C_Q, C_K, C_V, C_GA, C_A1, C_A2, C_GC, C_U, C_VS, C_GS = 0, 512, 640, 768, 1280, 1536, 1792, 2048, 2304, 2560
GATES_W = D_IN - C_GA - 2 * CONV_W

ADAM_LR = 0.001
ADAM_B1 = 0.9
ADAM_B2 = 0.999
ADAM_EPS = 1e-08
ADAM_WD = 0.01
ADAM_STEP = 10

N_CHIPS = 4
N_DEV = 8


def _dot(a, b):
    return jnp.dot(a, b, preferred_element_type=F32)


def _group_mean(x, bd_bf):
    hi = x.astype(BF16)
    lo = (x - hi.astype(F32)).astype(BF16)
    return _dot(hi, bd_bf) + _dot(lo, bd_bf)


def _dot_nt(a, b):
    return lax.dot_general(a, b, (((1,), (1,)), ((), ())), preferred_element_type=F32)


def _dot_tn(a, b):
    return lax.dot_general(a, b, (((0,), (0,)), ((), ())), preferred_element_type=F32)


def _lane(shape):
    return lax.broadcasted_iota(jnp.int32, shape, 1)


def _sigmoid(x):
    return 1.0 / (1.0 + jnp.exp(-x))


def _silu_fwd_bwd(x):
    s = _sigmoid(x)
    return x * s, s * (1.0 + x * (1.0 - s))


def _erf(x):
    x = jnp.clip(x, -4.0, 4.0)
    x2 = x * x
    a = -2.72614225801306e-10
    a = a * x2 + 2.77068142495902e-08
    a = a * x2 + -2.10102402082508e-06
    a = a * x2 + -5.69250639462346e-05
    a = a * x2 + -7.34990630326855e-04
    a = a * x2 + -2.95459980854025e-03
    a = a * x2 + -1.60960333262415e-02
    b = -1.45660718464996e-05
    b = b * x2 + -2.13374055278905e-04
    b = b * x2 + -1.68282697438203e-03
    b = b * x2 + -7.37332916720468e-03
    b = b * x2 + -1.42647390514189e-02
    return x * a / b


def _gelu_fwd_bwd(x, cdf=None):
    if cdf is None:
        cdf = 0.5 * (1.0 + _erf(x * 0.7071067811865476))
    pdf = jnp.exp(-0.5 * x * x) * 0.3989422804014327
    return x * cdf, cdf + x * pdf, cdf


def _rms(x):
    return lax.rsqrt(jnp.mean(x * x, axis=-1, keepdims=True) + EPS)


def _ln_hat(x):
    mu = jnp.mean(x, axis=-1, keepdims=True)
    xc = x - mu
    rs = lax.rsqrt(jnp.mean(xc * xc, axis=-1, keepdims=True) + EPS)
    return xc * rs, rs


def _ln_bwd(dxh, xh, rs):
    return rs * (dxh - jnp.mean(dxh, axis=-1, keepdims=True) - xh * jnp.mean(dxh * xh, axis=-1, keepdims=True))


def _swap16(z):
    parts = []
    for i in range(z.shape[1] // LANES):
        blk = z[:, i * LANES:(i + 1) * LANES]
        lane = _lane(blk.shape)
        parts.append(jnp.where((lane & 16) == 0, pltpu.roll(blk, LANES - 16, 1), pltpu.roll(blk, 16, 1)))
    return parts[0] if len(parts) == 1 else jnp.concatenate(parts, axis=1)


def _head_slab(pair, odd):
    src = pltpu.roll(pair, HEAD_DIM, 1) if odd else pair
    return jnp.where(_lane(pair.shape) < HEAD_DIM, src, 0.0)


def _pair_merge(even, odd):
    return jnp.where(_lane(even.shape) < HEAD_DIM, even, pltpu.roll(odd, HEAD_DIM, 1))


def _heads_to_cat(ref, n_heads):
    pairs = [_pair_merge(ref[2 * p], ref[2 * p + 1]) for p in range(n_heads // 2)]
    return pairs[0] if len(pairs) == 1 else jnp.concatenate(pairs, axis=1)


def _split3(x):
    hi = x.astype(BF16).astype(F32)
    r = x - hi
    mid = r.astype(BF16).astype(F32)
    lo = (r - mid).astype(BF16).astype(F32)
    return hi, mid, lo


def _with_spare(slab, hi, mid, lo):
    lane = _lane(slab.shape)
    return jnp.where(lane == HEAD_DIM, hi, jnp.where(lane == HEAD_DIM + 1, mid, jnp.where(lane == HEAD_DIM + 2, lo, slab)))


def _with_ones(slab):
    lane = _lane(slab.shape)
    return jnp.where((lane >= HEAD_DIM) & (lane < HEAD_DIM + 3), 1.0, slab)


def _conv_window(rot_ref, prev_ref, main, next_ref, first, last, t):
    n = t + 2 * HALO
    full = jnp.concatenate([jnp.where(first, 0.0, prev_ref[...]), main, jnp.where(last, 0.0, next_ref[...])], axis=0)
    rot_ref[0] = full
    for b in range(1, SUBLANES):
        rot_ref[b] = pltpu.roll(full, n - b, 0)


def _tap(rot_ref, start, t):
    a, b = divmod(start, SUBLANES)
    return rot_ref[b, SUBLANES * a:SUBLANES * a + t, :]


def _sgu_mix(v1_bf, w_ref, n_chunks):
    rows = []
    for n in range(n_chunks):
        pairs = []
        for p in range(SG_W // LANES):
            xp = v1_bf[n * SG_CHUNK:(n + 1) * SG_CHUNK, p * LANES:(p + 1) * LANES]
            me = _dot(w_ref[2 * p], xp)
            mo = _dot(w_ref[2 * p + 1], xp)
            pairs.append(jnp.where(_lane(me.shape) < HEAD_DIM, me, mo))
        rows.append(jnp.concatenate(pairs, axis=1))
    return rows[0] if len(rows) == 1 else jnp.concatenate(rows, axis=0)


def _halo_specs(t, s, width):
    per = t // HALO
    nblk = s // HALO
    prev = pl.BlockSpec((HALO, width), lambda i: (jnp.maximum(i * per - 1, 0), 0))
    nxt = pl.BlockSpec((HALO, width), lambda i: (jnp.minimum((i + 1) * per, nblk - 1), 0))
    return prev, nxt


def _const_spec(shape):
    nd = len(shape)
    return pl.BlockSpec(shape, lambda i: (0,) * nd)


def _arb(n=1):
    return pltpu.CompilerParams(dimension_semantics=("arbitrary",) * n)


def _fwd_in(x, g_pre, w_in_bf, bd, qg, kg, cos, sin, t, gather=()):
    s = x.shape[0]
    ex_args, ex_in, ex_out, ex_shapes, ex_sems, ex_hook = _riding_exchange(False, gather)
    n_ex = len(gather)

    def body(*refs):
        x_ref, g_ref, w_ref, bd_ref, qg_ref, kg_ref, cos_ref, sin_ref = refs[:8]
        proj_ref, hc_ref, q_ref, k_ref, v_ref = refs[8 + n_ex:13 + n_ex]
        if n_ex:
            ex_hook(refs[8:8 + n_ex], refs[13 + n_ex:13 + 2 * n_ex], refs[13 + 2 * n_ex:])
        xv = x_ref[...]
        h = (xv * _rms(xv) * g_ref[...]).astype(BF16)
        proj = _dot(h, w_ref[...])
        proj_ref[...] = proj
        cos_pair = cos_ref[...]
        sin_pair = sin_ref[...]
        cosv = jnp.concatenate([cos_pair] * (N_HEADS // 2), axis=1)
        sinv = jnp.concatenate([sin_pair] * (N_HEADS // 2), axis=1)
        q = proj[:, C_Q:C_K]
        qn = q * lax.rsqrt(_group_mean(q * q, bd_ref[...]) + EPS) * qg_ref[...]
        qr = (qn * cosv + _swap16(qn) * sinv) * ATT_SCALE
        for hh in range(N_HEADS):
            pair = qr[:, (hh // 2) * LANES:(hh // 2 + 1) * LANES]
            q_ref[hh] = _head_slab(pair, hh % 2 == 1).astype(BF16)
        k = proj[:, C_K:C_V]
        kn = k * lax.rsqrt(_group_mean(k * k, bd_ref[0:LANES, 0:LANES]) + EPS) * kg_ref[...]
        kr = kn * cos_pair + _swap16(kn) * sin_pair
        vv = proj[:, C_V:C_GA]
        for hh in range(N_KV):
            k_ref[hh] = _with_ones(_head_slab(kr, hh == 1)).astype(BF16)
            v_ref[hh] = _with_ones(_head_slab(vv, hh == 1)).astype(BF16)
        hc_ref[...] = proj[:, C_A1:C_A2] * _sigmoid(proj[:, C_A2:C_GC])

    row = lambda w: pl.BlockSpec((t, w), lambda i: (i, 0))
    heads = lambda n: pl.BlockSpec((n, t, LANES), lambda i: (0, i, 0))
    return pl.pallas_call(
        body, name="fwd_in_gather" if n_ex else "fwd_in", grid=(s // t,),
        in_specs=[row(D_MODEL), _const_spec((1, D_MODEL)), _const_spec((D_MODEL, D_IN)), _const_spec((ATT_W, ATT_W)),
                  _const_spec((1, ATT_W)), _const_spec((1, LANES)), row(LANES), row(LANES)] + ex_in,
        out_specs=[row(D_IN), row(CONV_W), heads(N_HEADS), heads(N_KV), heads(N_KV)] + ex_out,
        out_shape=[jax.ShapeDtypeStruct((s, D_IN), F32), jax.ShapeDtypeStruct((s, CONV_W), F32),
                   jax.ShapeDtypeStruct((N_HEADS, s, LANES), BF16), jax.ShapeDtypeStruct((N_KV, s, LANES), BF16),
                   jax.ShapeDtypeStruct((N_KV, s, LANES), BF16)] + ex_shapes,
        scratch_shapes=ex_sems if n_ex else [],
        compiler_params=_arb(),
    )(x, g_pre, w_in_bf, bd, qg, kg, cos, sin, *ex_args)


def _chunk_rows(c, tk):
    return pl.ds(c * tk, tk) if isinstance(c, int) else pl.ds(pl.multiple_of(c * tk, tk), tk)


def _three_stage_pipeline(nk, per_trip, stage1, stage2, stage3, peel):
    assert nk % 2 == 0 and per_trip % 2 == 0

    def step(t, parity, first=False, last=False):
        if not last:
            stage1(t + 1, 1 - parity)
        stage2(parity)
        if not first:
            stage3(t - 1, 1 - parity)

    stage1(0, 0)
    if not peel:
        while nk % per_trip:
            per_trip //= 2

        def whole_trip(i, carry):
            for u in range(per_trip):
                c = per_trip * i + u
                stage1(jnp.minimum(c + 1, nk - 1), 1 - u % 2)
                stage2(u % 2)
                stage3(jnp.maximum(c - 1, 0), 1 - u % 2)
            return carry

        lax.fori_loop(0, nk // per_trip, whole_trip, 0)
        stage3(nk - 1, 1)
        return

    step(0, 0, first=True)
    n_trips, left = divmod(nk - 2, per_trip)

    def trip(i, carry):
        for u in range(per_trip):
            step(1 + per_trip * i + u, (1 + u) % 2)
        return carry

    if n_trips:
        lax.fori_loop(0, n_trips, trip, 0)
    for t in range(1 + n_trips * per_trip, 1 + n_trips * per_trip + left):
        step(t, t % 2)
    step(nk - 1, 1, last=True)
    stage3(nk - 1, 1)


def _flash_fwd(q, k, v, tq, tk):
    s = q.shape[1]
    rows = GROUP * tq
    nk = s // tk
    half = max(LANES, tq // 2)

    def body(q_ref, k_ref, vt_ref, o_ref, qa_ref, m_scr, acc_scr, s0, s1, p0, p1, a0, a1):
        s_bufs, p_bufs, a_bufs = (s0, s1), (p0, p1), (a0, a1)
        qv = q_ref[...].reshape(rows, LANES)
        q_t = qv.astype(F32).T
        q_t_bf = q_t.astype(BF16)
        m_scr[...] = jnp.full((1, rows), -jnp.inf, F32)
        acc_scr[...] = jnp.zeros((LANES, rows), F32)

        def scores(c, slot):
            s_bufs[slot][...] = _dot(k_ref[_chunk_rows(c, tk), :], q_t_bf)

        def softmax(slot):
            for r0 in range(0, rows, half):
                r = slice(r0, r0 + half)
                sc = s_bufs[slot][:, r]
                m_prev = m_scr[:, r]
                m_new = jnp.maximum(m_prev, jnp.max(sc, axis=0, keepdims=True))
                p_bufs[slot][:, r] = jnp.exp((sc - m_new).astype(BF16))
                a_bufs[slot][:, r] = jnp.exp(m_prev - m_new)
                m_scr[:, r] = m_new

        def weighted_values(c, slot):
            acc_scr[...] = a_bufs[slot][...] * acc_scr[...] + _dot(vt_ref[c], p_bufs[slot][...])

        _three_stage_pipeline(nk, 4, scores, softmax, weighted_values, peel=True)

        acc = acc_scr[...]
        row = lax.broadcasted_iota(jnp.int32, acc.shape, 0)
        l = jnp.sum(jnp.where(row == HEAD_DIM, acc, 0.0), axis=0, keepdims=True)
        o_ref[...] = jnp.where(row < HEAD_DIM, acc / l, 0.0).T.reshape(GROUP, tq, LANES)
        hi, mid, lo = _split3(-(m_scr[...] + jnp.log(l)))
        qa_t = jnp.where(row == HEAD_DIM, hi, jnp.where(row == HEAD_DIM + 1, mid,
                                                        jnp.where(row == HEAD_DIM + 2, lo, q_t)))
        qa_ref[...] = qa_t.T.astype(BF16).reshape(GROUP, tq, LANES)

    qspec = pl.BlockSpec((GROUP, tq, LANES), lambda j, i: (j, i, 0))
    kspec = pl.BlockSpec((None, s, LANES), lambda j, i: (j, 0, 0))
    vtspec = pl.BlockSpec((None, nk, LANES, tk), lambda j, i: (j, 0, 0, 0))
    v_t = jnp.swapaxes(v.reshape(N_KV, nk, tk, LANES), 2, 3)
    return pl.pallas_call(
        body, name="flash_fwd", grid=(N_KV, s // tq),
        in_specs=[qspec, kspec, vtspec],
        out_specs=[qspec, qspec],
        out_shape=[jax.ShapeDtypeStruct((N_HEADS, s, LANES), F32), jax.ShapeDtypeStruct((N_HEADS, s, LANES), BF16)],
        scratch_shapes=[pltpu.VMEM((1, rows), F32), pltpu.VMEM((LANES, rows), F32),
                        pltpu.VMEM((tk, rows), F32), pltpu.VMEM((tk, rows), F32),
                        pltpu.VMEM((tk, rows), BF16), pltpu.VMEM((tk, rows), BF16),
                        pltpu.VMEM((1, rows), F32), pltpu.VMEM((1, rows), F32)],
        compiler_params=_arb(2),
    )(q, k, v_t)


def _groups_fwd(proj_ref, o_ref, hext_ref, cw_ref, cp_ref, sp_ref, sgw_ref, sgb_ref, t, kept=None):
    proj = proj_ref[...]
    r = {}
    r["att"] = _heads_to_cat(o_ref, N_HEADS)
    r["gate_a"], r["dgate_a"] = _silu_fwd_bwd(proj[:, C_GA:C_A1])
    r["att_g"] = r["att"] * r["gate_a"]
    if kept is None:
        c0 = jnp.zeros((t, CONV_W), F32) + cp_ref[0:1, :]
        for kk in range(CONV_K):
            c0 = c0 + cw_ref[kk:kk + 1, :] * _tap(hext_ref, kk + 1, t)
        cdf_u = cdf_v = None
    else:
        c0, cdf_u, cdf_v = kept[:, 0:CONV_W], kept[:, CONV_W:CONV_W + SG_W], kept[:, CONV_W + SG_W:KEPT_W]
    r["xh_c"], r["rs_c"] = _ln_hat(c0)
    r["c1"] = r["xh_c"] * cp_ref[1:2, :] + cp_ref[2:3, :]
    r["sg_c1"] = _sigmoid(r["c1"])
    r["c2"] = r["c1"] * r["sg_c1"]
    r["gate_c"], r["dgate_c"] = _silu_fwd_bwd(proj[:, C_GC:C_U])
    r["cnv_g"] = r["c2"] * r["gate_c"]
    r["gu"], r["dgu"], cdf_u = _gelu_fwd_bwd(proj[:, C_U:C_VS], cdf_u)
    gv, r["dgv"], cdf_v = _gelu_fwd_bwd(proj[:, C_VS:C_GS], cdf_v)
    r["kept"] = jnp.concatenate([c0, cdf_u, cdf_v], axis=1)
    r["xh_s"], r["rs_s"] = _ln_hat(gv)
    v1 = r["xh_s"] * sp_ref[0:1, :] + sp_ref[1:2, :]
    r["v1_bf"] = v1.astype(BF16)
    r["mixed"] = _sgu_mix(r["v1_bf"], sgw_ref, t // SG_CHUNK) + jnp.concatenate([sgb_ref[...]] * (t // SG_CHUNK), axis=0)
    r["um"] = r["gu"] * r["mixed"]
    r["gate_s"], r["dgate_s"] = _silu_fwd_bwd(proj[:, C_GS:D_IN])
    r["sgu_g"] = r["um"] * r["gate_s"]
    r["mc_bf"] = jnp.concatenate([r["att_g"], r["cnv_g"], r["sgu_g"]], axis=1).astype(BF16)
    return r


def _fwd_out(x, proj, o, hc, cw, cp, sp, sgw_bf, sgb, w_out_bf, g_post, t, target=None):
    s = x.shape[0]
    last_layer = target is not None

    def body(*refs):
        (x_ref, proj_ref, o_ref, hc_ref, hp_ref, hn_ref, cw_ref, cp_ref, sp_ref, sgw_ref, sgb_ref,
         w_ref, g_ref) = refs[:13]
        rest = refs[13:]
        if last_layer:
            t_ref, mix_ref, out_ref, kept_ref, sq_ref, hext_ref = rest
        else:
            mix_ref, out_ref, kept_ref, hext_ref = rest
        i = pl.program_id(0)
        _conv_window(hext_ref, hp_ref, hc_ref[...], hn_ref, i == 0, i == pl.num_programs(0) - 1, t)
        r = _groups_fwd(proj_ref, o_ref, hext_ref, cw_ref, cp_ref, sp_ref, sgw_ref, sgb_ref, t)
        kept_ref[...] = r["kept"]
        mix = _dot(r["mc_bf"], w_ref[...])
        mix_ref[...] = mix
        y = x_ref[...] + mix * _rms(mix) * g_ref[...]
        if last_layer:
            @pl.when(i == 0)
            def _():
                sq_ref[...] = jnp.zeros_like(sq_ref)

            err = y - t_ref[...]
            out_ref[...] = err * (1.0 / D_MODEL)
            sq_ref[...] += jnp.sum(err * err, axis=0, keepdims=True)
        else:
            out_ref[...] = y

    row = lambda w: pl.BlockSpec((t, w), lambda i: (i, 0))
    hprev, hnext = _halo_specs(t, s, CONV_W)
    big = jax.ShapeDtypeStruct((s, D_MODEL), F32)
    return pl.pallas_call(
        body, name="fwd_out_loss" if last_layer else "fwd_out", grid=(s // t,),
        in_specs=[row(D_MODEL), row(D_IN), pl.BlockSpec((N_HEADS, t, LANES), lambda i: (0, i, 0)), row(CONV_W),
                  hprev, hnext, _const_spec((32, CONV_W)), _const_spec((8, CONV_W)), _const_spec((8, SG_W)),
                  _const_spec((4, SG_CHUNK, SG_CHUNK)), _const_spec((SG_CHUNK, SG_W)),
                  _const_spec((D_MODEL, D_MODEL)), _const_spec((1, D_MODEL))] + ([row(D_MODEL)] if last_layer else []),
        out_specs=[row(D_MODEL), row(D_MODEL), row(KEPT_W)] + ([_const_spec((1, D_MODEL))] if last_layer else []),
        out_shape=[big, big, jax.ShapeDtypeStruct((s, KEPT_W), F32)]
        + ([jax.ShapeDtypeStruct((1, D_MODEL), F32)] if last_layer else []),
        scratch_shapes=[pltpu.VMEM((SUBLANES, t + 2 * HALO, CONV_W), F32)],
        compiler_params=_arb(),
    )(*((x, proj, o, hc, hc, hc, cw, cp, sp, sgw_bf, sgb, w_out_bf, g_post) + ((target,) if last_layer else ())))


def _bwd_out(dy, mix, proj, o, hc, kept, cw, cp, sp, sgw_bf, sgwt_bf, sgb, w_out_t_bf, g_post, t, scatter=()):
    s = dy.shape[0]
    n_chunks = t // SG_CHUNK
    ex_args, ex_in, ex_out, ex_shapes, ex_sems, ex_hook = _riding_exchange(True, scatter)
    n_ex = len(scatter)

    def body(*refs):
        (dy_ref, mix_ref, proj_ref, o_ref, hc_ref, hp_ref, hn_ref, kept_ref, cw_ref, cp_ref, sp_ref, sgw_ref,
         sgwt_ref, sgb_ref, wt_ref, g_ref) = refs[:16]
        (do_ref, dgs_ref, dc0_ref, gwo_ref, gpost_ref, gcw_ref, gvec_ref, gsgw_ref,
         gsgb_ref) = refs[16 + n_ex:25 + n_ex]
        hext_ref = refs[25 + 2 * n_ex]
        if n_ex:
            ex_hook(refs[16:16 + n_ex], refs[25 + n_ex:25 + 2 * n_ex], refs[26 + 2 * n_ex:])
        i = pl.program_id(0)

        @pl.when(i == 0)
        def _():
            for ref in (gwo_ref, gpost_ref, gcw_ref, gvec_ref, gsgw_ref, gsgb_ref):
                ref[...] = jnp.zeros_like(ref)

        _conv_window(hext_ref, hp_ref, hc_ref[...], hn_ref, i == 0, i == pl.num_programs(0) - 1, t)
        r = _groups_fwd(proj_ref, o_ref, hext_ref, cw_ref, cp_ref, sp_ref, sgw_ref, sgb_ref, t, kept=kept_ref[...])

        dyv = dy_ref[...]
        mix_v = mix_ref[...]
        rr = _rms(mix_v)
        gd = dyv * g_ref[...]
        dmix = rr * gd - mix_v * (rr * rr * rr * jnp.mean(gd * mix_v, axis=-1, keepdims=True))
        gpost_ref[...] += jnp.sum(dyv * mix_v * rr, axis=0, keepdims=True)
        dmix_bf = dmix.astype(BF16)
        gwo_ref[...] += _dot_tn(r["mc_bf"], dmix_bf)
        dmc = _dot(dmix_bf, wt_ref[...])

        d_att = dmc[:, 0:ATT_W]
        dg_att = d_att * r["att"] * r["dgate_a"]
        d_o = d_att * r["gate_a"]
        prod = d_o * r["att"]
        for p in range(N_HEADS // 2):
            sl = slice(p * LANES, (p + 1) * LANES)
            pr = prod[:, sl]
            tot = jnp.sum(pr, axis=1, keepdims=True)
            ev = jnp.sum(jnp.where(_lane(pr.shape) < HEAD_DIM, pr, 0.0), axis=1, keepdims=True)
            for odd, delta in ((False, ev), (True, tot - ev)):
                hi, mid, lo = _split3(-delta)
                do_ref[2 * p + int(odd)] = _with_spare(_head_slab(d_o[:, sl], odd), hi, mid, lo).astype(BF16)

        dcv = dmc[:, ATT_W:ATT_W + CONV_W]
        dg_conv = dcv * r["c2"] * r["dgate_c"]
        dc1 = dcv * r["gate_c"] * (r["sg_c1"] * (1.0 + r["c1"] * (1.0 - r["sg_c1"])))
        dc0 = _ln_bwd(dc1 * cp_ref[1:2, :], r["xh_c"], r["rs_c"])
        dc0_ref[...] = dc0
        for kk in range(CONV_K):
            gcw_ref[kk:kk + 1, :] += jnp.sum(dc0 * _tap(hext_ref, kk + 1, t), axis=0, keepdims=True)

        dsg = dmc[:, ATT_W + CONV_W:D_MODEL]
        dg_sg = dsg * r["um"] * r["dgate_s"]
        du = dsg * r["mixed"] * r["gate_s"] * r["dgu"]
        dmx = dsg * r["gu"] * r["gate_s"]
        dmx_bf = dmx.astype(BF16)
        sgb_sum = dmx[0:SG_CHUNK, :]
        for n in range(1, n_chunks):
            sgb_sum = sgb_sum + dmx[n * SG_CHUNK:(n + 1) * SG_CHUNK, :]
        gsgb_ref[...] += sgb_sum
        dv1_rows = []
        for n in range(n_chunks):
            pairs = []
            for p in range(SG_W // LANES):
                rs_ = slice(n * SG_CHUNK, (n + 1) * SG_CHUNK)
                ls_ = slice(p * LANES, (p + 1) * LANES)
                dm = dmx_bf[rs_, ls_]
                xp = r["v1_bf"][rs_, ls_]
                low = _lane(dm.shape) < HEAD_DIM
                zero = jnp.zeros_like(dm)
                gsgw_ref[2 * p] += _dot_nt(jnp.where(low, dm, zero), xp)
                gsgw_ref[2 * p + 1] += _dot_nt(jnp.where(low, zero, dm), xp)
                pairs.append(jnp.where(low, _dot(sgwt_ref[2 * p], dm), _dot(sgwt_ref[2 * p + 1], dm)))
            dv1_rows.append(jnp.concatenate(pairs, axis=1))
        dv1 = dv1_rows[0] if n_chunks == 1 else jnp.concatenate(dv1_rows, axis=0)
        dvs = _ln_bwd(dv1 * sp_ref[0:1, :], r["xh_s"], r["rs_s"]) * r["dgv"]

        zrow = jnp.zeros((1, CONV_W), F32)
        gvec_ref[...] += jnp.concatenate([
            jnp.sum(dc0, axis=0, keepdims=True),
            jnp.sum(dc1 * r["xh_c"], axis=0, keepdims=True),
            jnp.sum(dc1, axis=0, keepdims=True),
            jnp.sum(dv1 * r["xh_s"], axis=0, keepdims=True),
            jnp.sum(dv1, axis=0, keepdims=True),
            zrow, zrow, zrow], axis=0)
        dgs_ref[...] = jnp.concatenate([dg_att, dg_conv, du, dvs, dg_sg], axis=1)

    row = lambda w: pl.BlockSpec((t, w), lambda i: (i, 0))
    heads = pl.BlockSpec((N_HEADS, t, LANES), lambda i: (0, i, 0))
    hprev, hnext = _halo_specs(t, s, CONV_W)
    return pl.pallas_call(
        body, name="bwd_out_scatter" if n_ex else "bwd_out", grid=(s // t,),
        in_specs=[row(D_MODEL), row(D_MODEL), row(D_IN), heads, row(CONV_W), hprev, hnext, row(KEPT_W),
                  _const_spec((32, CONV_W)), _const_spec((8, CONV_W)), _const_spec((8, SG_W)),
                  _const_spec((4, SG_CHUNK, SG_CHUNK)), _const_spec((4, SG_CHUNK, SG_CHUNK)),
                  _const_spec((SG_CHUNK, SG_W)), _const_spec((D_MODEL, D_MODEL)), _const_spec((1, D_MODEL))] + ex_in,
        out_specs=[heads, row(GATES_W), row(CONV_W), _const_spec((D_MODEL, D_MODEL)), _const_spec((1, D_MODEL)),
                   _const_spec((32, CONV_W)), _const_spec((8, CONV_W)), _const_spec((4, SG_CHUNK, SG_CHUNK)),
                   _const_spec((SG_CHUNK, SG_W))] + ex_out,
        out_shape=[jax.ShapeDtypeStruct((N_HEADS, s, LANES), BF16), jax.ShapeDtypeStruct((s, GATES_W), F32),
                   jax.ShapeDtypeStruct((s, CONV_W), F32), jax.ShapeDtypeStruct((D_MODEL, D_MODEL), F32),
                   jax.ShapeDtypeStruct((1, D_MODEL), F32), jax.ShapeDtypeStruct((32, CONV_W), F32),
                   jax.ShapeDtypeStruct((8, CONV_W), F32), jax.ShapeDtypeStruct((4, SG_CHUNK, SG_CHUNK), F32),
                   jax.ShapeDtypeStruct((SG_CHUNK, SG_W), F32)] + ex_shapes,
        scratch_shapes=[pltpu.VMEM((SUBLANES, t + 2 * HALO, CONV_W), F32)] + (ex_sems if n_ex else []),
        compiler_params=_arb(),
    )(dy, mix, proj, o, hc, hc, hc, kept, cw, cp, sp, sgw_bf, sgwt_bf, sgb, w_out_t_bf, g_post, *ex_args)


def _flash_bwd(qa, doa, k, v, tq, tk):
    s = qa.shape[1]
    rows = GROUP * tq
    nk = s // tk
    n_q = s // tq

    def body(qa_ref, do_ref, k_ref, v_ref, dq_ref, dk_hbm, dv_hbm,
             dq_scr, dk_scr, dv_scr, s0, s1, d0, d1, p0, p1, e0, e1, sems):
        j, i = pl.program_id(0), pl.program_id(1)
        s_bufs, d_bufs, p_bufs, e_bufs = (s0, s1), (d0, d1), (p0, p1), (e0, e1)
        qv = qa_ref[...].reshape(rows, LANES)
        dov = do_ref[...].reshape(rows, LANES)
        q_t = qv.astype(F32).T.astype(BF16)
        do_t = dov.astype(F32).T.astype(BF16)
        dq_scr[...] = jnp.zeros((rows, LANES), F32)

        @pl.when(i == 0)
        def _():
            dk_scr[...] = jnp.zeros_like(dk_scr)
            dv_scr[...] = jnp.zeros_like(dv_scr)

        def at(c):
            return _chunk_rows(c, tk)

        def scores(c, slot):
            s_bufs[slot][...] = _dot_nt(qv, k_ref[at(c), :])
            d_bufs[slot][...] = _dot_nt(dov, v_ref[at(c), :])

        def probs(slot):
            for h in range(GROUP):
                r = slice(h * tq, (h + 1) * tq)
                p = jnp.exp(s_bufs[slot][r, :].astype(BF16))
                p_bufs[slot][r, :] = p
                e_bufs[slot][r, :] = (p.astype(F32) * d_bufs[slot][r, :]).astype(BF16)

        def grads(c, slot):
            ds = e_bufs[slot][...]
            dq_scr[...] += _dot(ds, k_ref[at(c), :])
            dv_scr[c] += _dot(do_t, p_bufs[slot][...])
            dk_scr[c] += _dot(q_t, ds)

        p1[...] = jnp.zeros((rows, tk), BF16)
        e1[...] = jnp.zeros((rows, tk), BF16)
        _three_stage_pipeline(nk, 4, scores, probs, grads, peel=False)
        dq_ref[...] = dq_scr[...].reshape(GROUP, tq, LANES)

        @pl.when(i == n_q - 1)
        def _():
            out = [pltpu.make_async_copy(dk_scr, dk_hbm.at[j], sems.at[0]),
                   pltpu.make_async_copy(dv_scr, dv_hbm.at[j], sems.at[1])]
            for cp in out:
                cp.start()
            for cp in out:
                cp.wait()

    qspec = pl.BlockSpec((GROUP, tq, LANES), lambda j, i: (j, i, 0))
    kvspec = pl.BlockSpec((None, s, LANES), lambda j, i: (j, 0, 0), pipeline_mode=pl.Buffered(1))
    hbm = pl.BlockSpec(memory_space=pl.ANY)
    stage_f32 = pltpu.VMEM((rows, tk), F32)
    stage_bf = pltpu.VMEM((rows, tk), BF16)
    kv_t = jax.ShapeDtypeStruct((N_KV, nk, LANES, tk), F32)
    return pl.pallas_call(
        body, name="flash_bwd", grid=(N_KV, n_q),
        in_specs=[qspec, qspec, kvspec, kvspec],
        out_specs=[qspec, hbm, hbm],
        out_shape=[jax.ShapeDtypeStruct((N_HEADS, s, LANES), F32), kv_t, kv_t],
        scratch_shapes=[pltpu.VMEM((rows, LANES), F32), pltpu.VMEM((nk, LANES, tk), F32), pltpu.VMEM((nk, LANES, tk), F32),
                        stage_f32, stage_f32, stage_f32, stage_f32, stage_bf, stage_bf, stage_bf, stage_bf,
                        pltpu.SemaphoreType.DMA((2,))],
        compiler_params=_arb(2),
    )(qa, doa, k, v)


def _bwd_in(dy, x, proj, dq, dk, dv, dgs, dc0, cw, g_pre, w_in_t_bf, bd, qg, kg, cos, sin, t):
    s = x.shape[0]

    def body(dy_ref, x_ref, proj_ref, dq_ref, dk_ref, dv_ref, dgs_ref, dc_ref, dcp_ref, dcn_ref, cw_ref, g_ref,
             wt_ref, bd_ref, qg_ref, kg_ref, cos_ref, sin_ref,
             dx_ref, dproj_ref, h_ref, gpre_ref, gq_ref, gk_ref, dext_ref):
        i = pl.program_id(0)

        @pl.when(i == 0)
        def _():
            for ref in (gpre_ref, gq_ref, gk_ref):
                ref[...] = jnp.zeros_like(ref)

        proj = proj_ref[...]
        cos_pair = cos_ref[...]
        sin_pair = sin_ref[...]
        cosv = jnp.concatenate([cos_pair] * (N_HEADS // 2), axis=1)
        sinv = jnp.concatenate([sin_pair] * (N_HEADS // 2), axis=1)

        def head_norm_bwd(dr, z, bdm, g, cs, sn, gacc_ref):
            dn = dr * cs + _swap16(dr * sn)
            rr = lax.rsqrt(_group_mean(z * z, bdm) + EPS)
            gdn = dn * g
            gacc_ref[...] += jnp.sum(dn * z * rr, axis=0, keepdims=True)
            return rr * gdn - z * (rr * rr * rr * _group_mean(gdn * z, bdm))

        dq_cat = _heads_to_cat(dq_ref, N_HEADS) * ATT_SCALE
        dzq = head_norm_bwd(dq_cat, proj[:, C_Q:C_K], bd_ref[...], qg_ref[...], cosv, sinv, gq_ref)

        def kv_pair(ref):
            return jnp.concatenate([ref[0, 0:HEAD_DIM, :], ref[1, 0:HEAD_DIM, :]], axis=0).T

        dk_cat = kv_pair(dk_ref)
        dzk = head_norm_bwd(dk_cat, proj[:, C_K:C_V], bd_ref[0:LANES, 0:LANES], kg_ref[...],
                            cos_pair, sin_pair, gk_ref)
        dv_cat = kv_pair(dv_ref)

        _conv_window(dext_ref, dcp_ref, dc_ref[...], dcn_ref, i == 0, i == pl.num_programs(0) - 1, t)
        dhc = jnp.zeros((t, CONV_W), F32)
        for kk in range(CONV_K):
            dhc = dhc + cw_ref[kk:kk + 1, :] * _tap(dext_ref, CONV_K - kk, t)
        sg = _sigmoid(proj[:, C_A2:C_GC])
        da1 = dhc * sg
        da2 = dhc * proj[:, C_A1:C_A2] * sg * (1.0 - sg)

        dgs = dgs_ref[...]
        dproj_bf = jnp.concatenate([dzq, dzk, dv_cat, dgs[:, 0:ATT_W], da1, da2, dgs[:, ATT_W:GATES_W]],
                                   axis=1).astype(BF16)
        dproj_ref[...] = dproj_bf
        dh = _dot(dproj_bf, wt_ref[...])

        xv = x_ref[...]
        rr = _rms(xv)
        gv = g_ref[...]
        h_ref[...] = (xv * rr * gv).astype(BF16)
        gdh = dh * gv
        gpre_ref[...] += jnp.sum(dh * xv * rr, axis=0, keepdims=True)
        dx_ref[...] = dy_ref[...] + rr * gdh - xv * (rr * rr * rr * jnp.mean(gdh * xv, axis=-1, keepdims=True))

    row = lambda w: pl.BlockSpec((t, w), lambda i: (i, 0))
    heads = lambda n: pl.BlockSpec((n, t, LANES), lambda i: (0, i, 0))
    hprev, hnext = _halo_specs(t, s, CONV_W)
    tk = dk.shape[3]
    assert tk % t == 0
    kv_t = pl.BlockSpec((N_KV, None, LANES, t), lambda i: (0, i // (tk // t), 0, i % (tk // t)))
    return pl.pallas_call(
        body, name="bwd_in", grid=(s // t,),
        in_specs=[row(D_MODEL), row(D_MODEL), row(D_IN), heads(N_HEADS), kv_t, kv_t, row(GATES_W),
                  row(CONV_W), hprev, hnext, _const_spec((32, CONV_W)), _const_spec((1, D_MODEL)),
                  _const_spec((D_IN, D_MODEL)), _const_spec((ATT_W, ATT_W)), _const_spec((1, ATT_W)),
                  _const_spec((1, LANES)), row(LANES), row(LANES)],
        out_specs=[row(D_MODEL), row(D_IN), row(D_MODEL), _const_spec((1, D_MODEL)), _const_spec((1, ATT_W)),
                   _const_spec((1, LANES))],
        out_shape=[jax.ShapeDtypeStruct((s, D_MODEL), F32), jax.ShapeDtypeStruct((s, D_IN), BF16),
                   jax.ShapeDtypeStruct((s, D_MODEL), BF16), jax.ShapeDtypeStruct((1, D_MODEL), F32),
                   jax.ShapeDtypeStruct((1, ATT_W), F32), jax.ShapeDtypeStruct((1, LANES), F32)],
        scratch_shapes=[pltpu.VMEM((SUBLANES, t + 2 * HALO, CONV_W), F32)],
        compiler_params=_arb(),
    )(dy, x, proj, dq, dk, dv, dgs, dc0, dc0, dc0, cw, g_pre, w_in_t_bf, bd, qg, kg, cos, sin)


def _grad_w_in(h_bf, dproj_bf, t):
    s = h_bf.shape[0]
    half = D_IN // 2

    def body(h_ref, d_ref, g_ref):
        @pl.when(pl.program_id(1) == 0)
        def _():
            g_ref[...] = jnp.zeros_like(g_ref)

        g_ref[...] += _dot_tn(h_ref[...], d_ref[...])

    return pl.pallas_call(
        body, name="grad_w_in", grid=(2, s // t),
        in_specs=[pl.BlockSpec((t, D_MODEL), lambda j, i: (i, 0)), pl.BlockSpec((t, half), lambda j, i: (i, j))],
        out_specs=pl.BlockSpec((D_MODEL, half), lambda j, i: (0, j)),
        out_shape=jax.ShapeDtypeStruct((D_MODEL, D_IN), F32),
        compiler_params=_arb(2),
    )(h_bf, dproj_bf)


def _place():
    x, y, c = lax.axis_index("x"), lax.axis_index("y"), lax.axis_index("c")
    chips = [(1 - x, y), (x, 1 - y), (1 - x, 1 - y)]
    return x, y, c, chips


def _any_specs(n):
    return [pl.BlockSpec(memory_space=pl.ANY)] * n


class _ChipExchange:
    def __init__(self, scatter, ins, outs, send_sems, recv_sems, local_sems):
        n = len(ins)
        x, y, c, chips = _place()
        mine = 2 * x + y
        src = (lambda a, piece: ins[a].at[piece]) if scatter else (lambda a, piece: ins[a])

        def copy(j, a, piece, slot, to):
            return pltpu.make_async_remote_copy(src_ref=src(a, piece), dst_ref=outs[a].at[slot],
                                                send_sem=send_sems.at[n * j + a], recv_sem=recv_sems.at[n * j + a],
                                                device_id=to, device_id_type=MESH)

        self.local = [pltpu.make_async_copy(src(a, mine), outs[a].at[mine], local_sems.at[a]) for a in range(n)]
        self.sends = [copy(j, a, 2 * px + py, mine, (px, py, c)) for j, (px, py) in enumerate(chips) for a in range(n)]
        self.arrivals = lambda: [copy(j, a, mine, 2 * px + py, (px, py, c))
                                 for j, (px, py) in enumerate(chips) for a in range(n)]

    def start(self):
        for cp in self.local + self.sends:
            cp.start()

    def finish(self):
        for cp in self.arrivals():
            cp.wait_recv()
        for cp in self.sends:
            cp.wait_send()
        for cp in self.local:
            cp.wait()

    @staticmethod
    def out_shapes(scatter, arrs):
        return [jax.ShapeDtypeStruct(a.shape if scatter else (N_CHIPS,) + a.shape, a.dtype) for a in arrs]

    @staticmethod
    def semaphores(n):
        return [pltpu.SemaphoreType.DMA((3 * n,)), pltpu.SemaphoreType.DMA((3 * n,)), pltpu.SemaphoreType.DMA((n,))]


def _exchange(scatter, arrs, name):
    n = len(arrs)

    def body(*refs):
        ex = _ChipExchange(scatter, refs[:n], refs[n:2 * n], *refs[2 * n:])
        ex.start()
        ex.finish()

    return pl.pallas_call(
        body, name=name, in_specs=_any_specs(n), out_specs=_any_specs(n),
        out_shape=_ChipExchange.out_shapes(scatter, arrs), scratch_shapes=_ChipExchange.semaphores(n),
    )(*arrs)


def _riding_exchange(scatter, arrs):
    n = len(arrs)

    def hook(ins, outs, sems):
        i = pl.program_id(0)

        @pl.when(i == 0)
        def _():
            _ChipExchange(scatter, ins, outs, *sems).start()

        @pl.when(i == pl.num_programs(0) - 1)
        def _():
            _ChipExchange(scatter, ins, outs, *sems).finish()

    return (tuple(arrs), _any_specs(n), _any_specs(n), _ChipExchange.out_shapes(scatter, arrs),
            _ChipExchange.semaphores(n), hook)


def _sum_chips(parts, rb, name):
    depth = len(parts)
    _, r, cdim = parts[0].shape

    def body(*refs):
        o_ref = refs[depth]
        for l in range(depth):
            def add(p_ref=refs[l]):
                part = lambda j: p_ref[j].astype(F32)
                o_ref[...] = ((part(0) + part(1)) + part(2)) + part(3)

            pl.when(pl.program_id(0) == l)(add)

    return pl.pallas_call(
        body, name=name, grid=(depth, r // rb),
        in_specs=[pl.BlockSpec((N_CHIPS, rb, cdim), lambda l, i: (0, i, 0))] * depth,
        out_specs=pl.BlockSpec((None, rb, cdim), lambda l, i: (l, i, 0)),
        out_shape=jax.ShapeDtypeStruct((depth, r, cdim), F32),
        compiler_params=_arb(2),
    )(*parts)


def _swap_with_sibling(a, b):
    arrs = (a, b)
    n = len(arrs)

    def body(*refs):
        ins, outs = refs[:n], refs[n:2 * n]
        send_sems, recv_sems = refs[2 * n:]
        x, y, c, _ = _place()
        cps = [pltpu.make_async_remote_copy(src_ref=ins[k], dst_ref=outs[k], send_sem=send_sems.at[k],
                                            recv_sem=recv_sems.at[k], device_id=(x, y, 1 - c), device_id_type=MESH)
               for k in range(n)]
        for cp in cps:
            cp.start()
        for cp in cps:
            cp.wait()

    return pl.pallas_call(
        body, name="swap_with_sibling",
        in_specs=_any_specs(n), out_specs=_any_specs(n),
        out_shape=[jax.ShapeDtypeStruct(v.shape, v.dtype) for v in arrs],
        scratch_shapes=[pltpu.SemaphoreType.DMA((n,)), pltpu.SemaphoreType.DMA((n,))],
    )(*arrs)


def _allreduce_small(slab):
    m, n = slab.shape

    def body(x_ref, out_ref, gath, send_sems, recv_sems, local_sem):
        x, y, c, chips = _place()
        me, sibling = (x, y, c), (x, y, 1 - c)

        def rows(px, py, pc):
            return gath.at[pl.ds(pl.multiple_of((4 * px + 2 * py + pc) * m, 8), m), :]

        def copy(k, block, to, src=None):
            return pltpu.make_async_remote_copy(src_ref=rows(*block) if src is None else src, dst_ref=rows(*block),
                                                send_sem=send_sems.at[k], recv_sem=recv_sems.at[k],
                                                device_id=to, device_id_type=MESH)

        mine = pltpu.make_async_copy(x_ref, rows(*me), local_sem)
        mine.start()
        first = [copy(0, me, sibling, src=x_ref)]
        first += [copy(1 + j, me, (*chip, c), src=x_ref) for j, chip in enumerate(chips)]
        for cp in first:
            cp.start()
        passed = [copy(4 + j, (*chip, c), sibling) for j, chip in enumerate(chips)]
        for j, chip in enumerate(chips):
            copy(1 + j, (*chip, c), me).wait_recv()
            passed[j].start()
        copy(0, sibling, me).wait_recv()
        for j, chip in enumerate(chips):
            copy(4 + j, (*chip, 1 - c), me).wait_recv()
        for cp in first + passed:
            cp.wait_send()
        mine.wait()
        total = gath[0:m, :]
        for d in range(1, N_DEV):
            total = total + gath[d * m:(d + 1) * m, :]
        out_ref[...] = total

    return pl.pallas_call(
        body, name="allreduce_small",
        in_specs=[pl.BlockSpec(memory_space=pltpu.VMEM)],
        out_specs=pl.BlockSpec(memory_space=pltpu.VMEM),
        out_shape=jax.ShapeDtypeStruct((m, n), F32),
        scratch_shapes=[pltpu.VMEM((N_DEV * m, n), F32), pltpu.SemaphoreType.DMA((7,)), pltpu.SemaphoreType.DMA((7,)),
                        pltpu.SemaphoreType.DMA],
    )(slab)


def _adamw(w, ga, gb, m, v, rb, name):
    depth, r, cdim = w.shape

    def body(w_ref, ga_ref, gb_ref, m_ref, v_ref, g_out, d_out, m_out, v_out):
        g = ga_ref[...] + gb_ref[...]
        m2 = ADAM_B1 * m_ref[...] + (1.0 - ADAM_B1) * g
        v2 = ADAM_B2 * v_ref[...] + (1.0 - ADAM_B2) * (g * g)
        m_hat = m2 / (1.0 - ADAM_B1 ** ADAM_STEP)
        v_hat = v2 / (1.0 - ADAM_B2 ** ADAM_STEP)
        g_out[...] = g
        d_out[...] = -ADAM_LR * (m_hat / (jnp.sqrt(v_hat) + ADAM_EPS) + ADAM_WD * w_ref[...])
        m_out[...] = m2
        v_out[...] = v2

    spec = pl.BlockSpec((None, rb, cdim), lambda l, i: (l, i, 0))
    shp = jax.ShapeDtypeStruct((depth, r, cdim), F32)
    return pl.pallas_call(
        body, name=name, grid=(depth, r // rb),
        in_specs=[spec] * 5, out_specs=[spec] * 4, out_shape=[shp] * 4,
        compiler_params=_arb(2),
    )(w, ga, gb, m, v)


def _rope_tables(s):
    t = jnp.arange(s, dtype=jnp.int32)
    row = (t // GRID_W).astype(F32)
    col = (t % GRID_W).astype(F32)
    half = HEAD_DIM // 4
    inv_freq = ROPE_THETA ** (-jnp.arange(half, dtype=F32) / half)
    ar = row[:, None] * inv_freq[None, :]
    ac = col[:, None] * inv_freq[None, :]
    cos = jnp.concatenate([jnp.cos(ar), jnp.cos(ar), jnp.cos(ac), jnp.cos(ac)], axis=1)
    sin = jnp.concatenate([-jnp.sin(ar), jnp.sin(ar), -jnp.sin(ac), jnp.sin(ac)], axis=1)
    return jnp.tile(cos, (1, 2)), jnp.tile(sin, (1, 2))


def _pad_rows(a, rows):
    return jnp.pad(a, ((0, rows - a.shape[0]),) + ((0, 0),) * (a.ndim - 1))


_SMALL = ("pre_norm", "post_norm", "q_norm", "k_norm", "conv_dw_b", "conv_ln_g", "conv_ln_b", "sg_ln_g", "sg_ln_b",
          "sg_w", "sg_b")


def _pack(parts):
    flat = jnp.concatenate([p.reshape(-1, LANES) for p in parts], axis=0)
    return _pad_rows(flat, -(-flat.shape[0] // 8) * 8)


def _unpack(slab, shapes):
    out, r = [], 0
    for shp in shapes:
        n = 1
        for d in shp:
            n *= d
        out.append(slab[r:r + n // LANES].reshape(shp))
        r += n // LANES
    return out


def kernel(x, pre_norm, post_norm, w_in, w_out, q_norm, k_norm, conv_dw, conv_dw_b, conv_ln_g, conv_ln_b, sg_ln_g, sg_ln_b, sg_w, sg_b, loss_target, m_pre_norm, m_post_norm, m_w_in, m_w_out, m_q_norm, m_k_norm, m_conv_dw, m_conv_dw_b, m_conv_ln_g, m_conv_ln_b, m_sg_ln_g, m_sg_ln_b, m_sg_w, m_sg_b, v_pre_norm, v_post_norm, v_w_in, v_w_out, v_q_norm, v_k_norm, v_conv_dw, v_conv_dw_b, v_conv_ln_g, v_conv_ln_b, v_sg_ln_g, v_sg_ln_b, v_sg_w, v_sg_b):
    depth = w_in.shape[0]
    s = x.shape[1]
    assert x.shape[0] == 1 and s % SG_CHUNK == 0 and x.shape[2] == D_MODEL
    tq = min(512, s)
    tk = min(512, s // 2)
    t = min(256, tk)
    shard_cols = w_in.shape[2]
    chip = 2 * lax.axis_index("x") + lax.axis_index("y")

    w_in_sh, w_out_sh = w_in.astype(BF16), w_out.astype(BF16)
    whole_w_in = lambda g: jnp.concatenate([g[j] for j in range(N_CHIPS)], axis=2)
    w_in_bf = [whole_w_in(_exchange(False, (w_in_sh[0:1],), "gather_w_in_first")[0])[0]]
    riders = ((w_in_sh[1:],) if depth > 1 else ()) + (w_out_sh, conv_dw)
    sgw_bf = sg_w.astype(BF16)
    sgwt_bf = jnp.swapaxes(sg_w, 2, 3).astype(BF16)

    cos, sin = _rope_tables(s)
    bd = jnp.kron(jnp.eye(N_HEADS, dtype=F32), jnp.full((HEAD_DIM, HEAD_DIM), 1.0 / HEAD_DIM, F32)).astype(BF16)

    def head_gains(l):
        return jnp.tile(q_norm[l], N_HEADS)[None, :], jnp.tile(k_norm[l], N_KV)[None, :]

    def layer_consts(l):
        cw = _pad_rows(cdw_full[l], 32)
        cp = _pad_rows(jnp.stack([conv_dw_b[l], conv_ln_g[l], conv_ln_b[l]]), 8)
        sp = _pad_rows(jnp.stack([sg_ln_g[l], sg_ln_b[l]]), 8)
        sgb = jnp.repeat(sg_b[l].T, HEAD_DIM, axis=1)
        return cw, cp, sp, sgb

    xs = [x[0]]
    saved = []
    for l in range(depth):
        qg, kg = head_gains(l)
        outs = _fwd_in(xs[l], pre_norm[l][None, :], w_in_bf[l], bd, qg, kg, cos, sin, t, gather=riders if l == 0 else ())
        proj, hc, q, k, v = outs[:5]
        if l == 0:
            gathered = list(outs[5:])
            if depth > 1:
                w_in_bf += list(whole_w_in(gathered.pop(0)))
            w_out_bf = jnp.concatenate([gathered[0][j] for j in range(N_CHIPS)], axis=1)
            cdw_full = jnp.concatenate([gathered[1][j] for j in range(N_CHIPS)], axis=2)
            w_out_t_bf = jnp.swapaxes(w_out_bf, 1, 2)
        cw, cp, sp, sgb = layer_consts(l)
        o, qa = _flash_fwd(q, k, v, tq, tk)
        outs = _fwd_out(xs[l], proj, o, hc, cw, cp, sp, sgw_bf[l], sgb, w_out_bf[l], post_norm[l][None, :], t,
                        target=loss_target[0] if l == depth - 1 else None)
        saved.append((proj, hc, qa, k, v, o, outs[0], outs[2]))
        if l < depth - 1:
            xs.append(outs[1])
    dy, sq = outs[1], outs[3]
    loss = lax.psum(0.5 * jnp.sum(sq) / D_MODEL, ("x", "y", "c"))

    g_small = {n: [] for n in _SMALL + ("conv_dw",)}
    received = [None] * depth
    pieces = ()
    for l in reversed(range(depth)):
        cw, cp, sp, sgb = layer_consts(l)
        qg, kg = head_gains(l)
        proj, hc, qa, k, v, o, mix, kept = saved[l]
        outs = _bwd_out(dy, mix, proj, o, hc, kept, cw, cp, sp, sgw_bf[l], sgwt_bf[l], sgb, w_out_t_bf[l],
                        post_norm[l][None, :], t, scatter=pieces)
        doa, dgs, dc0, gwo, gpost, gcw, gvec, gsgw, gsgb = outs[:9]
        if pieces:
            received[l + 1] = outs[9:]
        dq, dk, dv = _flash_bwd(qa, doa, k, v, tq, tk)
        dy, dproj_bf, h_bf, gpre, gq, gk = _bwd_in(dy, xs[l], proj, dq, dk, dv, dgs, dc0, cw, pre_norm[l][None, :],
                                                  jnp.swapaxes(w_in_bf[l], 0, 1), bd, qg, kg, cos, sin, t)
        gwi = _grad_w_in(h_bf, dproj_bf, min(512, s))
        pieces = (jnp.stack([gwi[:, j * shard_cols:(j + 1) * shard_cols] for j in range(N_CHIPS)]).astype(BF16),
                  gwo.reshape(N_CHIPS, gwo.shape[0] // N_CHIPS, gwo.shape[1]).astype(BF16))
        g_small["pre_norm"].append(gpre[0])
        g_small["post_norm"].append(gpost[0])
        g_small["q_norm"].append(gq[0].reshape(N_HEADS, HEAD_DIM).sum(0))
        g_small["k_norm"].append(gk[0].reshape(N_KV, HEAD_DIM).sum(0))
        g_small["conv_dw"].append(gcw[:CONV_K])
        g_small["conv_dw_b"].append(gvec[0])
        g_small["conv_ln_g"].append(gvec[1])
        g_small["conv_ln_b"].append(gvec[2])
        g_small["sg_ln_g"].append(gvec[3])
        g_small["sg_ln_b"].append(gvec[4])
        g_small["sg_w"].append(gsgw)
        g_small["sg_b"].append(gsgb.reshape(SG_CHUNK, SG_W // HEAD_DIM, HEAD_DIM).sum(-1).T)
    grad_x = dy[None]
    g_small = {n: jnp.stack(vals[::-1]) for n, vals in g_small.items()}

    received[0] = _exchange(True, pieces, "scatter_grads")
    s_in = _sum_chips([r[0] for r in received], 256, "sum_chips_w_in")
    s_out = _sum_chips([r[1] for r in received], 256, "sum_chips_w_out")
    t_in, t_out = _swap_with_sibling(s_in, s_out)
    grad_w_in, delta_w_in, new_m_w_in, new_v_w_in = _adamw(w_in, s_in, t_in, m_w_in, v_w_in, 256, "adamw_w_in")
    grad_w_out, delta_w_out, new_m_w_out, new_v_w_out = _adamw(w_out, s_out, t_out, m_w_out, v_w_out, 256, "adamw_w_out")

    small_w = dict(pre_norm=pre_norm, post_norm=post_norm, q_norm=q_norm, k_norm=k_norm, conv_dw_b=conv_dw_b,
                   conv_ln_g=conv_ln_g, conv_ln_b=conv_ln_b, sg_ln_g=sg_ln_g, sg_ln_b=sg_ln_b, sg_w=sg_w, sg_b=sg_b)
    small_m = dict(pre_norm=m_pre_norm, post_norm=m_post_norm, q_norm=m_q_norm, k_norm=m_k_norm, conv_dw_b=m_conv_dw_b,
                   conv_ln_g=m_conv_ln_g, conv_ln_b=m_conv_ln_b, sg_ln_g=m_sg_ln_g, sg_ln_b=m_sg_ln_b, sg_w=m_sg_w,
                   sg_b=m_sg_b)
    small_v = dict(pre_norm=v_pre_norm, post_norm=v_post_norm, q_norm=v_q_norm, k_norm=v_k_norm, conv_dw_b=v_conv_dw_b,
                   conv_ln_g=v_conv_ln_g, conv_ln_b=v_conv_ln_b, sg_ln_g=v_sg_ln_g, sg_ln_b=v_sg_ln_b, sg_w=v_sg_w,
                   sg_b=v_sg_b)
    shapes = [small_w[n].shape for n in _SMALL]
    red = _allreduce_small(_pack([g_small[n] for n in _SMALL] + [g_small["conv_dw"]]))
    n_rep = sum(small_w[n].size for n in _SMALL) // LANES
    g_cdw_full = red[n_rep:n_rep + g_small["conv_dw"].size // LANES].reshape(g_small["conv_dw"].shape)
    cdw_cols = conv_dw.shape[2]
    g_cdw = lax.dynamic_slice_in_dim(g_cdw_full, chip * cdw_cols, cdw_cols, axis=2)
    g_slab = _pack([red[:n_rep], g_cdw])
    w_slab = _pack([small_w[n] for n in _SMALL] + [conv_dw])
    m_slab = _pack([small_m[n] for n in _SMALL] + [m_conv_dw])
    v_slab = _pack([small_v[n] for n in _SMALL] + [v_conv_dw])
    rows = w_slab.shape[0]
    outs = _adamw(w_slab[None], g_slab[None], jnp.zeros_like(g_slab)[None], m_slab[None], v_slab[None], rows, "adamw_small")
    unpacked = [dict(zip(_SMALL + ("conv_dw",), _unpack(o_[0], shapes + [conv_dw.shape]))) for o_ in outs]

    big = [dict(w_in=a, w_out=b) for a, b in ((grad_w_in, grad_w_out), (delta_w_in, delta_w_out),
                                             (new_m_w_in, new_m_w_out), (new_v_w_in, new_v_w_out))]
    order = ("pre_norm", "post_norm", "w_in", "w_out", "q_norm", "k_norm", "conv_dw", "conv_dw_b", "conv_ln_g",
             "conv_ln_b", "sg_ln_g", "sg_ln_b", "sg_w", "sg_b")
    result = [loss, grad_x]
    for kind in range(4):
        for name in order:
            result.append(big[kind][name] if name in big[kind] else unpacked[kind][name])
    return tuple(result)
```
